```python
import jax, jax.numpy as jnp
from jax import lax
import numpy as np

D_MODEL = 1024
BATCH = 8
SEQ = 2048
DEPTH = 1

GRID_W = 64
CTX_LEN = 256
CONV_W = 512
CONV_K = 3
N_HEADS = 8
N_KV_HEADS = 2
HEAD_DIM = 64
ATT_W = N_HEADS * HEAD_DIM
KV_W = N_KV_HEADS * HEAD_DIM
WINDOW = 128
BLOCK = 128
ROPE_BASE = 10000.0
N_GROUPS = 4
EXPERTS_PER_GROUP = 8
N_EXPERTS = N_GROUPS * EXPERTS_PER_GROUP
TOP_K = 2
EXPERT_FF = 256
N_MOD = 6
NORM_EPS = 1e-6
NEG_INF = -1e30
IN_SIZES = (CONV_W, CONV_W, CONV_W, ATT_W, KV_W, KV_W, D_MODEL, D_MODEL)
IN_COLS = sum(IN_SIZES)
KV_OFF = 3 * CONV_W + ATT_W

kernel_name = 'hybrid_shortconv_swa_hmoe_dit_block'


def rmsnorm(x, g):
    xf = x.astype(jnp.float32)
    inv = lax.rsqrt(jnp.mean(xf * xf, axis=-1, keepdims=True) + NORM_EPS)
    return (xf * inv).astype(x.dtype) * g


def modulate(h, shift, scale):
    return h * (1 + scale) + shift


def split_in(u):
    idx = np.cumsum(IN_SIZES)[:-1].tolist()
    return jnp.split(u, idx, axis=-1)


def axial_rope(rows):
    n_freq = HEAD_DIM // 4
    inv_freq = ROPE_BASE ** (-jnp.arange(n_freq, dtype=jnp.float32) / n_freq)
    row = jnp.repeat(jnp.arange(rows, dtype=jnp.float32), GRID_W)
    col = jnp.tile(jnp.arange(GRID_W, dtype=jnp.float32), rows)
    ang = jnp.concatenate([row[:, None] * inv_freq, col[:, None] * inv_freq], axis=-1)
    return jnp.cos(ang), jnp.sin(ang)


def apply_rope(t, cos, sin):
    half = HEAD_DIM // 2
    c = cos[:, None, :].astype(t.dtype)
    s = sin[:, None, :].astype(t.dtype)
    t1, t2 = t[..., :half], t[..., half:]
    return jnp.concatenate([t1 * c - t2 * s, t1 * s + t2 * c], axis=-1)


def short_conv_branch(b, cg, xin, w_conv, b_conv, w_a):
    u = cg * xin
    L = u.shape[1]
    pad = CONV_K // 2
    up = jnp.pad(u, ((0, 0), (pad, pad), (0, 0)))
    y = b_conv + sum(up[:, j:j + L] * w_conv[j] for j in range(CONV_K))
    return (b * y) @ w_a


def window_attention(q, k, v, k_ctx, v_ctx, sink):
    B, L = q.shape[:2]
    C = k_ctx.shape[1]
    nb = L // BLOCK
    G = N_HEADS // N_KV_HEADS
    scale = HEAD_DIM ** -0.5
    qb = q.reshape(B, nb, BLOCK, N_KV_HEADS, G, HEAD_DIM)

    def band(t):
        tp = jnp.pad(t, ((0, 0), (BLOCK, BLOCK), (0, 0), (0, 0)))
        tp = tp.reshape(B, nb + 2, BLOCK, N_KV_HEADS, HEAD_DIM)
        return jnp.concatenate([tp[:, :-2], tp[:, 1:-1], tp[:, 2:]], axis=2)

    kw, vw = band(k), band(v)
    s_win = jnp.einsum('bnqkgd,bnjkd->bnkgqj', qb, kw).astype(jnp.float32) * scale
    s_ctx = jnp.einsum('bnqkgd,bckd->bnkgqc', qb, k_ctx).astype(jnp.float32) * scale
    blk = jnp.arange(nb)[:, None, None] * BLOCK
    qpos = blk + jnp.arange(BLOCK)[None, :, None]
    kpos = blk - BLOCK + jnp.arange(3 * BLOCK)[None, None, :]
    valid = (jnp.abs(qpos - kpos) <= WINDOW) & (kpos >= 0) & (kpos < L)
    s_win = jnp.where(valid[None, :, None, None], s_win, NEG_INF)
    s_sink = jnp.broadcast_to(sink.astype(jnp.float32).reshape(N_KV_HEADS, G, 1, 1), s_win.shape[:-1] + (1,))
    p = jax.nn.softmax(jnp.concatenate([s_win, s_ctx, s_sink], axis=-1), axis=-1).astype(v.dtype)
    w3 = 3 * BLOCK
    o = (jnp.einsum('bnkgqj,bnjkd->bnqkgd', p[..., :w3], vw)
         + jnp.einsum('bnkgqc,bckd->bnqkgd', p[..., w3:w3 + C], v_ctx))
    return o.reshape(B, L, ATT_W)


def context_attention(q, k, v, sink):
    B, C = q.shape[:2]
    G = N_HEADS // N_KV_HEADS
    qg = q.reshape(B, C, N_KV_HEADS, G, HEAD_DIM)
    s = jnp.einsum('bqkgd,bckd->bkgqc', qg, k).astype(jnp.float32) * HEAD_DIM ** -0.5
    s_sink = jnp.broadcast_to(sink.astype(jnp.float32).reshape(N_KV_HEADS, G, 1, 1), s.shape[:-1] + (1,))
    p = jax.nn.softmax(jnp.concatenate([s, s_sink], axis=-1), axis=-1).astype(v.dtype)
    o = jnp.einsum('bkgqc,bckd->bqkgd', p[..., :C], v)
    return o.reshape(B, C, ATT_W)


def context_kv(hc, w_in):
    B, C = hc.shape[:2]
    k = (hc @ w_in[:, KV_OFF:KV_OFF + KV_W]).reshape(B, C, N_KV_HEADS, HEAD_DIM)
    v = (hc @ w_in[:, KV_OFF + KV_W:KV_OFF + 2 * KV_W]).reshape(B, C, N_KV_HEADS, HEAD_DIM)
    return k, v


def token_mixer_context(hc, w_in, w_conv, b_conv, w_a, w_b, sink, w_o):
    B, C = hc.shape[:2]
    b, cg, xin, q, k, v, ga, gb = split_in(hc @ w_in)
    k = k.reshape(B, C, N_KV_HEADS, HEAD_DIM)
    v = v.reshape(B, C, N_KV_HEADS, HEAD_DIM)
    ya = short_conv_branch(b, cg, xin, w_conv, b_conv, w_a)
    yb = context_attention(q.reshape(B, C, N_HEADS, HEAD_DIM), k, v, sink) @ w_b
    y = (jax.nn.sigmoid(ga) * ya + jax.nn.sigmoid(gb) * yb) @ w_o
    return y, k, v


def token_mixer_latent(h, k_ctx, v_ctx, cos, sin, w_in, w_conv, b_conv, w_a, w_b, sink, w_o):
    B, L = h.shape[:2]
    b, cg, xin, q, k, v, ga, gb = split_in(h @ w_in)
    ya = short_conv_branch(b, cg, xin, w_conv, b_conv, w_a)
    q = apply_rope(q.reshape(B, L, N_HEADS, HEAD_DIM), cos, sin)
    k = apply_rope(k.reshape(B, L, N_KV_HEADS, HEAD_DIM), cos, sin)
    v = v.reshape(B, L, N_KV_HEADS, HEAD_DIM)
    yb = window_attention(q, k, v, k_ctx, v_ctx, sink) @ w_b
    return (jax.nn.sigmoid(ga) * ya + jax.nn.sigmoid(gb) * yb) @ w_o


def hierarchical_moe(h, w_group, b_group, w_router, b_router, w_up, w_down):
    shp = h.shape
    t = h.reshape(-1, shp[-1])
    n = t.shape[0]
    g_prob = jax.nn.softmax((t @ w_group + b_group).astype(jnp.float32), axis=-1)
    p_g, g_idx = lax.top_k(g_prob, 1)
    e_logits = (t @ w_router + b_router).astype(jnp.float32).reshape(n, N_GROUPS, EXPERTS_PER_GROUP)
    e_sel = jnp.take_along_axis(e_logits, g_idx[:, :, None], axis=1)[:, 0]
    top_p, top_i = lax.top_k(jax.nn.softmax(e_sel, axis=-1), TOP_K)
    gate = p_g * top_p / jnp.sum(top_p, axis=-1, keepdims=True)
    ids = g_idx * EXPERTS_PER_GROUP + top_i
    combine = jnp.sum(jax.nn.one_hot(ids, N_EXPERTS, dtype=jnp.float32) * gate[..., None], axis=1).astype(t.dtype)
    y = jnp.zeros_like(t)
    for e in range(N_EXPERTS):
        a, u = jnp.split(t @ w_up[e], 2, axis=-1)
        y = y + combine[:, e:e + 1] * ((jax.nn.silu(a) * u) @ w_down[e])
    return y.reshape(shp)


def setup_inputs(seed: int = 0) -> dict:
    key = jax.random.key(seed)
    ks = jax.random.split(key, 24)
    f32 = jnp.float32
    D = D_MODEL

    def nrm(k, shape, scale):
        return jax.random.normal(k, shape, f32) * scale

    return {
        'x': nrm(ks[0], (BATCH, SEQ, D), 1.0),
        'c': nrm(ks[1], (BATCH, D), 1.0),
        'ctx': nrm(ks[2], (BATCH, CTX_LEN, D), 1.0),
        'c_ctx': nrm(ks[3], (D,), 1.0),
        'w_ada': nrm(ks[4], (DEPTH, D, N_MOD * D), 0.5 * D ** -0.5),
        'b_ada': nrm(ks[5], (DEPTH, N_MOD * D), 0.02),
        'norm1_g': 1.0 + nrm(ks[6], (DEPTH, D), 0.01),
        'w_in': nrm(ks[7], (DEPTH, D, IN_COLS), D ** -0.5),
        'w_conv': nrm(ks[8], (DEPTH, CONV_K, CONV_W), CONV_K ** -0.5),
        'b_conv': nrm(ks[9], (DEPTH, CONV_W), 0.02),
        'w_a': nrm(ks[10], (DEPTH, CONV_W, D), CONV_W ** -0.5),
        'w_b': nrm(ks[11], (DEPTH, ATT_W, D), ATT_W ** -0.5),
        'sink': nrm(ks[12], (DEPTH, N_HEADS), 0.5),
        'w_o': nrm(ks[13], (DEPTH, D, D), D ** -0.5),
        'norm2_g': 1.0 + nrm(ks[14], (DEPTH, D), 0.01),
        'w_group': nrm(ks[15], (DEPTH, D, N_GROUPS), D ** -0.5),
        'b_group': nrm(ks[16], (DEPTH, N_GROUPS), 0.01),
        'w_router': nrm(ks[17], (DEPTH, D, N_EXPERTS), D ** -0.5),
        'b_router': nrm(ks[18], (DEPTH, N_EXPERTS), 0.01),
        'w_up': nrm(ks[19], (DEPTH, N_EXPERTS, D, 2 * EXPERT_FF), D ** -0.5),
        'w_down': nrm(ks[20], (DEPTH, N_EXPERTS, EXPERT_FF, D), EXPERT_FF ** -0.5),
        'final_g': 1.0 + nrm(ks[21], (D,), 0.01),
    }


def reference(x, c, ctx, c_ctx, w_ada, b_ada, norm1_g, w_in, w_conv, b_conv, w_a, w_b, sink, w_o,
              norm2_g, w_group, b_group, w_router, b_router, w_up, w_down, final_g):
    L = x.shape[1]
    rows = L // GRID_W
    cos, sin = axial_rope(rows)
    xc = ctx
    for l in range(DEPTH):
        last = l == DEPTH - 1
        mod = (jax.nn.silu(c) @ w_ada[l] + b_ada[l])[:, None, :]
        sh1, sc1, g1, sh2, sc2, g2 = jnp.split(mod, N_MOD, axis=-1)
        modc = jax.nn.silu(c_ctx) @ w_ada[l] + b_ada[l]
        csh1, csc1, cg1, csh2, csc2, cg2 = jnp.split(modc, N_MOD, axis=-1)

        hc = modulate(rmsnorm(xc, norm1_g[l]), csh1, csc1)
        if last:
            k_ctx, v_ctx = context_kv(hc, w_in[l])
        else:
            yc, k_ctx, v_ctx = token_mixer_context(hc, w_in[l], w_conv[l], b_conv[l], w_a[l], w_b[l], sink[l], w_o[l])
            xc = xc + cg1 * yc
        h = modulate(rmsnorm(x, norm1_g[l]), sh1, sc1)
        x = x + g1 * token_mixer_latent(h, k_ctx, v_ctx, cos, sin, w_in[l], w_conv[l], b_conv[l],
                                        w_a[l], w_b[l], sink[l], w_o[l])

        h = modulate(rmsnorm(x, norm2_g[l]), sh2, sc2)
        x = x + g2 * hierarchical_moe(h, w_group[l], b_group[l], w_router[l], b_router[l], w_up[l], w_down[l])
        if not last:
            hc = modulate(rmsnorm(xc, norm2_g[l]), csh2, csc2)
            xc = xc + cg2 * hierarchical_moe(hc, w_group[l], b_group[l], w_router[l], b_router[l], w_up[l], w_down[l])
    return rmsnorm(x, final_g)
```

```python
import functools

import numpy as np
import jax
import jax.numpy as jnp
from jax import lax
from jax.experimental import pallas as pl
from jax.experimental.pallas import tpu as pltpu

F32 = jnp.float32
BF16 = jnp.bfloat16

D_MODEL = 1024
GRID_W = 64
CONV_W = 512
N_HEADS = 8
N_KV_HEADS = 2
HEAD_DIM = 64
ATT_W = N_HEADS * HEAD_DIM
KV_W = N_KV_HEADS * HEAD_DIM
BLOCK = 128
ROPE_BASE = 10000.0
N_GROUPS = 4
EXPERTS_PER_GROUP = 8
N_EXPERTS = N_GROUPS * EXPERTS_PER_GROUP
EXPERT_FF = 256
N_MOD = 6
NORM_EPS = 1e-6
NEG_INF = -1e30

OFF_B, OFF_CG, OFF_XIN, OFF_Q, OFF_K, OFF_V, OFF_GA, OFF_GB = (
    0, 512, 1024, 1536, 2048, 2176, 2304, 3328)
IN_COLS = 4352

LANES = 128
SEQ_TILE = 512
BLOCKS_PER_TILE = SEQ_TILE // BLOCK
ROUTE_ROWS = 8
GROUP_ROWS = 8
VMEM_LIMIT = 56 * 1024 * 1024


def _cparams(sem):
    return pltpu.CompilerParams(dimension_semantics=sem, vmem_limit_bytes=VMEM_LIMIT)


def _rms_mod(x, g, shift, scale):
    inv = lax.rsqrt(jnp.mean(x * x, axis=-1, keepdims=True) + NORM_EPS)
    return (x * inv) * g * (1.0 + scale) + shift


def _mod_kernel(c_ref, w_ref, b_ref, o_ref):
    c = c_ref[...]
    a = (c * jax.nn.sigmoid(c)).astype(BF16)
    o_ref[...] = jnp.dot(a, w_ref[...].astype(BF16), preferred_element_type=F32) + b_ref[...]


def _modulation(cc, w_ada, b_ada):
    rows = cc.shape[0]
    cols = w_ada.shape[1]
    tile = 1024
    return pl.pallas_call(
        _mod_kernel,
        grid=(cols // tile,),
        in_specs=[pl.BlockSpec((rows, D_MODEL), lambda j: (0, 0)),
                  pl.BlockSpec((D_MODEL, tile), lambda j: (0, j)),
                  pl.BlockSpec((1, tile), lambda j: (0, j))],
        out_specs=pl.BlockSpec((rows, tile), lambda j: (0, j)),
        out_shape=jax.ShapeDtypeStruct((rows, cols), F32),
        compiler_params=_cparams(("arbitrary",)),
        name="adaln_mod",
    )(cc, w_ada, b_ada)


def _ctx_kernel(x_ref, g_ref, sh_ref, sc_ref, w_ref, k_ref, vt_ref):
    h = _rms_mod(x_ref[0], g_ref[...], sh_ref[...], sc_ref[...]).astype(BF16)
    kv = jnp.dot(h, w_ref[...], preferred_element_type=F32)
    k_ref[0] = kv[:, :KV_W].astype(BF16)
    vt_ref[0] = kv[:, KV_W:].T.astype(BF16)


def _context_kv(ctx, norm_g, csh, csc, w_kv):
    bsz, clen, _ = ctx.shape
    return pl.pallas_call(
        _ctx_kernel,
        grid=(bsz,),
        in_specs=[pl.BlockSpec((1, clen, D_MODEL), lambda b: (b, 0, 0)),
                  pl.BlockSpec((1, D_MODEL), lambda b: (0, 0)),
                  pl.BlockSpec((1, D_MODEL), lambda b: (0, 0)),
                  pl.BlockSpec((1, D_MODEL), lambda b: (0, 0)),
                  pl.BlockSpec((D_MODEL, 2 * KV_W), lambda b: (0, 0))],
        out_specs=[pl.BlockSpec((1, clen, KV_W), lambda b: (b, 0, 0)),
                   pl.BlockSpec((1, KV_W, clen), lambda b: (b, 0, 0))],
        out_shape=[jax.ShapeDtypeStruct((bsz, clen, KV_W), BF16),
                   jax.ShapeDtypeStruct((bsz, KV_W, clen), BF16)],
        compiler_params=_cparams(("arbitrary",)),
        name="context_kv",
    )(ctx, norm_g, csh, csc, w_kv)


def _rope(t, cos, sin_signed):
    lane = lax.broadcasted_iota(jnp.int32, (1, LANES), 1)
    first_half = (lane % HEAD_DIM) < (HEAD_DIM // 2)
    outs = []
    for j in range(t.shape[1] // LANES):
        tj = t[:, j * LANES:(j + 1) * LANES]
        partner = jnp.where(first_half,
                            pltpu.roll(tj, LANES - HEAD_DIM // 2, axis=1),
                            pltpu.roll(tj, HEAD_DIM // 2, axis=1))
        outs.append(tj * cos + partner * sin_signed)
    return outs[0] if len(outs) == 1 else jnp.concatenate(outs, axis=1)


def _proj_a_kernel(x_ref, g_ref, sh_ref, sc_ref, w_ref, cos_ref, sin_ref, uc_ref, k_ref, vt_ref):
    h = _rms_mod(x_ref[0], g_ref[...], sh_ref[0], sc_ref[0]).astype(BF16)
    u = jnp.dot(h, w_ref[...], preferred_element_type=F32)
    uc_ref[0] = u[:, :CONV_W] * u[:, CONV_W:2 * CONV_W]
    k = u[:, 2 * CONV_W:2 * CONV_W + KV_W]
    k_ref[0] = _rope(k, cos_ref[...], sin_ref[...]).astype(BF16)
    vt_ref[0] = u[:, 2 * CONV_W + KV_W:].T.astype(BF16)


def _proj_a(x, norm_g, mod3, w_a_cols, cos_t, sin_t):
    bsz, seq, _ = x.shape
    nt = seq // SEQ_TILE
    wcols = w_a_cols.shape[1]
    return pl.pallas_call(
        _proj_a_kernel,
        grid=(nt, bsz),
        in_specs=[pl.BlockSpec((1, SEQ_TILE, D_MODEL), lambda t, b: (b, t, 0)),
                  pl.BlockSpec((1, D_MODEL), lambda t, b: (0, 0)),
                  pl.BlockSpec((1, 1, D_MODEL), lambda t, b: (b, 0, 0)),
                  pl.BlockSpec((1, 1, D_MODEL), lambda t, b: (b, 0, 1)),
                  pl.BlockSpec((D_MODEL, wcols), lambda t, b: (0, 0)),
                  pl.BlockSpec((SEQ_TILE, LANES), lambda t, b: (t, 0)),
                  pl.BlockSpec((SEQ_TILE, LANES), lambda t, b: (t, 0))],
        out_specs=[pl.BlockSpec((1, SEQ_TILE, CONV_W), lambda t, b: (b, t, 0)),
                   pl.BlockSpec((1, SEQ_TILE, KV_W), lambda t, b: (b, t, 0)),
                   pl.BlockSpec((1, KV_W, SEQ_TILE), lambda t, b: (b, 0, t))],
        out_shape=[jax.ShapeDtypeStruct((bsz, seq, CONV_W), F32),
                   jax.ShapeDtypeStruct((bsz, seq, KV_W), BF16),
                   jax.ShapeDtypeStruct((bsz, KV_W, seq), BF16)],
        compiler_params=_cparams(("arbitrary", "arbitrary")),
        name="proj_a",
    )(x, norm_g, mod3, mod3, w_a_cols, cos_t, sin_t)


def _attention_block(q_blk, k_win, k_ctx, vt_win, vt_ctx, bias, sink_row):
    q_t = q_blk.T.astype(BF16)
    zeros = jnp.zeros((HEAD_DIM, 4 * BLOCK), BF16)
    bias4 = jnp.concatenate([bias] * 4, axis=1)
    o_rows = []
    for g in range(N_KV_HEADS):
        qg = jnp.concatenate([q_t[(4 * g + h) * HEAD_DIM:(4 * g + h + 1) * HEAD_DIM, :]
                              for h in range(4)], axis=1)
        rhs = jnp.concatenate([qg, zeros] if g == 0 else [zeros, qg], axis=0)
        s_win = jnp.dot(k_win, rhs, preferred_element_type=F32) + bias4
        s_ctx = jnp.dot(k_ctx, rhs, preferred_element_type=F32)
        sink = sink_row[:, g * 4 * BLOCK:(g + 1) * 4 * BLOCK]
        m = jnp.maximum(jnp.maximum(jnp.max(s_win, axis=0, keepdims=True),
                                    jnp.max(s_ctx, axis=0, keepdims=True)), sink)
        p_win = jnp.exp(s_win - m)
        p_ctx = jnp.exp(s_ctx - m)
        denom = (jnp.sum(p_win, axis=0, keepdims=True) + jnp.sum(p_ctx, axis=0, keepdims=True)
                 + jnp.exp(sink - m))
        o_t = (jnp.dot(vt_win[g * HEAD_DIM:(g + 1) * HEAD_DIM, :], p_win.astype(BF16),
                       preferred_element_type=F32)
               + jnp.dot(vt_ctx[g * HEAD_DIM:(g + 1) * HEAD_DIM, :], p_ctx.astype(BF16),
                         preferred_element_type=F32))
        o_t = o_t / denom
        for h in range(4):
            o_rows.append(o_t[:, h * BLOCK:(h + 1) * BLOCK])
    return jnp.concatenate(o_rows, axis=0).T


def _route(logits_t):
    t = logits_t.shape[1]
    grow = lax.broadcasted_iota(jnp.int32, (GROUP_ROWS, t), 0)
    gl = jnp.where(grow < N_GROUPS, logits_t[0:GROUP_ROWS, :], NEG_INF)
    gm = jnp.max(gl, axis=0, keepdims=True)
    p_g = 1.0 / jnp.sum(jnp.exp(gl - gm), axis=0, keepdims=True)
    g_idx = jnp.min(jnp.where(gl == gm, grow, N_GROUPS), axis=0, keepdims=True)

    erow = lax.broadcasted_iota(jnp.int32, (N_EXPERTS, t), 0)
    el = logits_t[GROUP_ROWS:GROUP_ROWS + N_EXPERTS, :]
    sel = (erow // EXPERTS_PER_GROUP) == g_idx
    em = jnp.where(sel, el, NEG_INF)
    m1 = jnp.max(em, axis=0, keepdims=True)
    i1 = jnp.min(jnp.where(em == m1, erow, N_EXPERTS), axis=0, keepdims=True)
    em2 = jnp.where(erow == i1, NEG_INF, em)
    m2 = jnp.max(em2, axis=0, keepdims=True)
    i2 = jnp.min(jnp.where(em2 == m2, erow, N_EXPERTS), axis=0, keepdims=True)
    z = jnp.sum(jnp.where(sel, jnp.exp(el - m1), 0.0), axis=0, keepdims=True)
    p1 = 1.0 / z
    p2 = jnp.exp(m2 - m1) / z
    gate1 = p_g * p1 / (p1 + p2)
    gate2 = p_g * p2 / (p1 + p2)
    pad = jnp.zeros((ROUTE_ROWS - 4, t), F32)
    return jnp.concatenate([i1.astype(F32), i2.astype(F32), gate1, gate2, pad], axis=0)


def _mix_kernel(x_ref, g1n_ref, g2n_ref, sh1_ref, sc1_ref, gt1_ref, sh2_ref, sc2_ref,
                wb_ref, cos_ref, sin_ref,
                uc_ref, ucp_ref, ucn_ref, k_ref, kp_ref, kn_ref, vt_ref, vtp_ref, vtn_ref,
                kc_ref, vtc_ref, wconv_ref, bconv_ref, wa_ref, wbb_ref, wo_ref,
                sink_ref, bias_ref, wr_ref, br_ref,
                x1_ref, h2_ref, route_ref, routec_ref):
    t_idx = pl.program_id(0)
    n_tiles = pl.num_programs(0)
    is_first = t_idx == 0
    is_last = t_idx == n_tiles - 1

    x = x_ref[0]
    hb = _rms_mod(x, g1n_ref[...], sh1_ref[0], sc1_ref[0]).astype(BF16)

    bq = jnp.dot(hb, wb_ref[:, 0:2 * CONV_W], preferred_element_type=F32)
    uc = uc_ref[0]
    row = lax.broadcasted_iota(jnp.int32, (SEQ_TILE, 1), 0)
    prev_row = jnp.where(is_first, 0.0, ucp_ref[0][7:8, :])
    next_row = jnp.where(is_last, 0.0, ucn_ref[0][0:1, :])
    up = jnp.where(row == 0, prev_row, pltpu.roll(uc, 1, axis=0))
    dn = jnp.where(row == SEQ_TILE - 1, next_row, pltpu.roll(uc, SEQ_TILE - 1, axis=0))
    wconv = wconv_ref[...]
    y = bconv_ref[...] + (up * wconv[0:1, :] + uc * wconv[1:2, :] + dn * wconv[2:3, :])
    ya = jnp.dot((bq[:, :CONV_W] * y).astype(BF16), wa_ref[...], preferred_element_type=F32)

    q = _rope(bq[:, CONV_W:], cos_ref[...], sin_ref[...]) * (HEAD_DIM ** -0.5)
    k_all = jnp.concatenate([kp_ref[0], k_ref[0], kn_ref[0]], axis=0)
    vt_all = jnp.concatenate([vtp_ref[0], vt_ref[0], vtn_ref[0]], axis=1)
    k_ctx = kc_ref[0]
    vt_ctx = vtc_ref[0]
    bias = bias_ref[...]
    sink_row = sink_ref[...]
    key = lax.broadcasted_iota(jnp.int32, (3 * BLOCK, 1), 0)
    o_blocks = []
    for jb in range(BLOCKS_PER_TILE):
        bias_j = bias
        if jb == 0:
            bias_j = bias + jnp.where(jnp.logical_and(is_first, key < BLOCK), NEG_INF, 0.0)
        if jb == BLOCKS_PER_TILE - 1:
            bias_j = bias + jnp.where(jnp.logical_and(is_last, key >= 2 * BLOCK), NEG_INF, 0.0)
        o_blocks.append(_attention_block(
            q[jb * BLOCK:(jb + 1) * BLOCK, :],
            k_all[jb * BLOCK:(jb + 3) * BLOCK, :], k_ctx,
            vt_all[:, jb * BLOCK:(jb + 3) * BLOCK], vt_ctx, bias_j, sink_row))
    o = jnp.concatenate(o_blocks, axis=0).astype(BF16)
    yb = jnp.dot(o, wbb_ref[...], preferred_element_type=F32)

    ga = jnp.dot(hb, wb_ref[:, 2 * CONV_W:2 * CONV_W + D_MODEL], preferred_element_type=F32)
    gb = jnp.dot(hb, wb_ref[:, 2 * CONV_W + D_MODEL:], preferred_element_type=F32)
    merged = (jax.nn.sigmoid(ga) * ya + jax.nn.sigmoid(gb) * yb).astype(BF16)
    x1 = x + gt1_ref[0] * jnp.dot(merged, wo_ref[...], preferred_element_type=F32)
    x1_ref[0] = x1

    h2 = _rms_mod(x1, g2n_ref[...], sh2_ref[0], sc2_ref[0]).astype(BF16)
    h2_ref[0] = h2
    logits_t = lax.dot_general(wr_ref[...], h2, (((1,), (1,)), ((), ())),
                               preferred_element_type=F32) + br_ref[...]
    route = _route(logits_t)
    route_ref[0] = route
    padded = jnp.concatenate([route, jnp.zeros((LANES - ROUTE_ROWS, SEQ_TILE), F32)], axis=0)
    routec_ref[0] = padded.T


def _mix(x, norm1_g, norm2_g, mod3, w_b_cols, cos_t, sin_t, uc, k_rot, v_t, k_ctx, vt_ctx,
         w_conv, b_conv, w_a, w_b, w_o, sink_row, bias, wr_t, br_col):
    bsz, seq, _ = x.shape
    nt = seq // SEQ_TILE
    nblk = seq // BLOCK
    clen = k_ctx.shape[1]
    rows8 = SEQ_TILE // 8

    def mod_spec(j):
        return pl.BlockSpec((1, 1, D_MODEL), lambda t, b, j=j: (b, 0, j))

    def const_spec(shape):
        return pl.BlockSpec(shape, lambda t, b: tuple(0 for _ in shape))

    in_specs = [
        pl.BlockSpec((1, SEQ_TILE, D_MODEL), lambda t, b: (b, t, 0)),
        const_spec((1, D_MODEL)), const_spec((1, D_MODEL)),
        mod_spec(0), mod_spec(1), mod_spec(2), mod_spec(3), mod_spec(4),
        const_spec(w_b_cols.shape),
        pl.BlockSpec((SEQ_TILE, LANES), lambda t, b: (t, 0)),
        pl.BlockSpec((SEQ_TILE, LANES), lambda t, b: (t, 0)),
        pl.BlockSpec((1, SEQ_TILE, CONV_W), lambda t, b: (b, t, 0)),
        pl.BlockSpec((1, 8, CONV_W), lambda t, b: (b, jnp.maximum(t * rows8 - 1, 0), 0)),
        pl.BlockSpec((1, 8, CONV_W),
                     lambda t, b: (b, jnp.minimum((t + 1) * rows8, seq // 8 - 1), 0)),
        pl.BlockSpec((1, SEQ_TILE, KV_W), lambda t, b: (b, t, 0)),
        pl.BlockSpec((1, BLOCK, KV_W),
                     lambda t, b: (b, jnp.maximum(t * BLOCKS_PER_TILE - 1, 0), 0)),
        pl.BlockSpec((1, BLOCK, KV_W),
                     lambda t, b: (b, jnp.minimum((t + 1) * BLOCKS_PER_TILE, nblk - 1), 0)),
        pl.BlockSpec((1, KV_W, SEQ_TILE), lambda t, b: (b, 0, t)),
        pl.BlockSpec((1, KV_W, BLOCK),
                     lambda t, b: (b, 0, jnp.maximum(t * BLOCKS_PER_TILE - 1, 0))),
        pl.BlockSpec((1, KV_W, BLOCK),
                     lambda t, b: (b, 0, jnp.minimum((t + 1) * BLOCKS_PER_TILE, nblk - 1))),
        pl.BlockSpec((1, clen, KV_W), lambda t, b: (b, 0, 0)),
        pl.BlockSpec((1, KV_W, clen), lambda t, b: (b, 0, 0)),
        const_spec(w_conv.shape), const_spec(b_conv.shape),
        const_spec(w_a.shape), const_spec(w_b.shape), const_spec(w_o.shape),
        const_spec(sink_row.shape), const_spec(bias.shape),
        const_spec(wr_t.shape), const_spec(br_col.shape),
    ]
    out_specs = [
        pl.BlockSpec((1, SEQ_TILE, D_MODEL), lambda t, b: (b, t, 0)),
        pl.BlockSpec((1, SEQ_TILE, D_MODEL), lambda t, b: (b, t, 0)),
        pl.BlockSpec((1, ROUTE_ROWS, SEQ_TILE), lambda t, b: (b, 0, t)),
        pl.BlockSpec((1, SEQ_TILE, LANES), lambda t, b: (b, t, 0)),
    ]
    out_shape = [
        jax.ShapeDtypeStruct((bsz, seq, D_MODEL), F32),
        jax.ShapeDtypeStruct((bsz, seq, D_MODEL), BF16),
        jax.ShapeDtypeStruct((bsz, ROUTE_ROWS, seq), F32),
        jax.ShapeDtypeStruct((bsz, seq, LANES), F32),
    ]
    return pl.pallas_call(
        _mix_kernel,
        grid=(nt, bsz),
        in_specs=in_specs,
        out_specs=out_specs,
        out_shape=out_shape,
        compiler_params=_cparams(("arbitrary", "arbitrary")),
        name="token_mix",
    )(x, norm1_g, norm2_g, mod3, mod3, mod3, mod3, mod3, w_b_cols, cos_t, sin_t,
      uc, uc, uc, k_rot, k_rot, k_rot, v_t, v_t, v_t, k_ctx, vt_ctx,
      w_conv, b_conv, w_a, w_b, w_o, sink_row, bias, wr_t, br_col)


MOE_TILE = 1024


def _moe_dense_kernel(h_ref, rc_ref, x1_ref, gt2_ref, fg_ref, wu_ref, wd_ref, o_ref, acc_ref):
    e = pl.program_id(1)

    @pl.when(e == 0)
    def _():
        acc_ref[...] = jnp.zeros_like(acc_ref)

    au = jnp.dot(h_ref[...], wu_ref[0].astype(BF16), preferred_element_type=F32)
    a = au[:, :EXPERT_FF]
    act = (a * jax.nn.sigmoid(a)) * au[:, EXPERT_FF:]
    y = jnp.dot(act.astype(BF16), wd_ref[0].astype(BF16), preferred_element_type=F32)
    rc = rc_ref[...]
    ef = e.astype(F32)
    comb = (jnp.where(rc[:, 0:1] == ef, rc[:, 2:3], 0.0)
            + jnp.where(rc[:, 1:2] == ef, rc[:, 3:4], 0.0))
    acc_ref[...] += comb * y

    @pl.when(e == N_EXPERTS - 1)
    def _():
        x2 = x1_ref[...] + gt2_ref[0] * acc_ref[...]
        inv = lax.rsqrt(jnp.mean(x2 * x2, axis=-1, keepdims=True) + NORM_EPS)
        o_ref[...] = (x2 * inv) * fg_ref[...]


def _moe_dense(h2, route_c, x1, mod3, final_g, w_up, w_down, seq):
    n = h2.shape[0]
    tiles_per_seq = seq // MOE_TILE
    return pl.pallas_call(
        _moe_dense_kernel,
        grid=(n // MOE_TILE, N_EXPERTS),
        in_specs=[pl.BlockSpec((MOE_TILE, D_MODEL), lambda i, e: (i, 0)),
                  pl.BlockSpec((MOE_TILE, LANES), lambda i, e: (i, 0)),
                  pl.BlockSpec((MOE_TILE, D_MODEL), lambda i, e: (i, 0)),
                  pl.BlockSpec((1, 1, D_MODEL), lambda i, e: (i // tiles_per_seq, 0, 5)),
                  pl.BlockSpec((1, D_MODEL), lambda i, e: (0, 0)),
                  pl.BlockSpec((1, D_MODEL, 2 * EXPERT_FF), lambda i, e: (e, 0, 0)),
                  pl.BlockSpec((1, EXPERT_FF, D_MODEL), lambda i, e: (e, 0, 0))],
        out_specs=pl.BlockSpec((MOE_TILE, D_MODEL), lambda i, e: (i, 0)),
        out_shape=jax.ShapeDtypeStruct((n, D_MODEL), F32),
        scratch_shapes=[pltpu.VMEM((MOE_TILE, D_MODEL), F32)],
        compiler_params=_cparams(("arbitrary", "arbitrary")),
        name="moe_dense",
    )(h2, route_c, x1, mod3, final_g, w_up, w_down)


def _rope_tables(seq):
    n_freq = HEAD_DIM // 4
    inv_freq = ROPE_BASE ** (-jnp.arange(n_freq, dtype=F32) / n_freq)
    rows = seq // GRID_W
    row = jnp.repeat(jnp.arange(rows, dtype=F32), GRID_W)
    col = jnp.tile(jnp.arange(GRID_W, dtype=F32), rows)
    ang = jnp.concatenate([row[:, None] * inv_freq, col[:, None] * inv_freq], axis=-1)
    cos, sin = jnp.cos(ang), jnp.sin(ang)
    return jnp.tile(cos, (1, 4)), jnp.concatenate([-sin, sin, -sin, sin], axis=1)


def _window_bias():
    key = np.arange(3 * BLOCK)[:, None]
    qry = np.arange(BLOCK)[None, :]
    valid = (key - qry >= 0) & (key - qry <= 2 * BLOCK)
    return jnp.asarray(np.where(valid, 0.0, NEG_INF), F32)


def kernel(x, c, ctx, c_ctx, w_ada, b_ada, norm1_g, w_in, w_conv, b_conv, w_a, w_b, sink, w_o,
           norm2_g, w_group, b_group, w_router, b_router, w_up, w_down, final_g):
    bsz, seq, _ = x.shape
    assert w_ada.shape[0] == 1 and seq % SEQ_TILE == 0 and seq % MOE_TILE == 0

    cc = jnp.zeros((16, D_MODEL), F32).at[:bsz].set(c).at[bsz].set(c_ctx)
    mod = _modulation(cc, w_ada[0], b_ada[0][None, :])
    mod3 = mod[:bsz].reshape(bsz, 1, N_MOD * D_MODEL)
    csh1 = mod[bsz:bsz + 1, 0:D_MODEL]
    csc1 = mod[bsz:bsz + 1, D_MODEL:2 * D_MODEL]

    w = w_in[0]
    n1g = norm1_g[0][None, :]
    n2g = norm2_g[0][None, :]
    w_kv = w[:, OFF_K:OFF_GA].astype(BF16)
    w_a_cols = jnp.concatenate([w[:, OFF_CG:OFF_Q], w[:, OFF_K:OFF_GA]], axis=1).astype(BF16)
    w_b_cols = jnp.concatenate([w[:, OFF_B:OFF_CG], w[:, OFF_Q:OFF_K], w[:, OFF_GA:]],
                               axis=1).astype(BF16)
    cos_t, sin_t = _rope_tables(seq)

    k_ctx, vt_ctx = _context_kv(ctx, n1g, csh1, csc1, w_kv)
    uc, k_rot, v_t = _proj_a(x, n1g, mod3, w_a_cols, cos_t, sin_t)

    sink_row = jnp.repeat(sink[0].astype(F32), BLOCK)[None, :]
    wr_t = jnp.zeros((LANES, D_MODEL), F32)
    wr_t = wr_t.at[0:N_GROUPS].set(w_group[0].T)
    wr_t = wr_t.at[GROUP_ROWS:GROUP_ROWS + N_EXPERTS].set(w_router[0].T).astype(BF16)
    br_col = jnp.zeros((LANES, 1), F32)
    br_col = br_col.at[0:N_GROUPS, 0].set(b_group[0])
    br_col = br_col.at[GROUP_ROWS:GROUP_ROWS + N_EXPERTS, 0].set(b_router[0])

    x1, h2, _, route_c = _mix(
        x, n1g, n2g, mod3, w_b_cols, cos_t, sin_t, uc, k_rot, v_t, k_ctx, vt_ctx,
        w_conv[0], b_conv[0][None, :], w_a[0].astype(BF16), w_b[0].astype(BF16),
        w_o[0].astype(BF16), sink_row, _window_bias(), wr_t, br_col)

    n = bsz * seq
    out = _moe_dense(h2.reshape(n, D_MODEL), route_c.reshape(n, LANES), x1.reshape(n, D_MODEL),
                     mod3, final_g[None, :], w_up[0], w_down[0], seq)
    return out.reshape(bsz, seq, D_MODEL)
```

```python
import functools

import numpy as np
import jax
import jax.numpy as jnp
from jax import lax
from jax.experimental import pallas as pl
from jax.experimental.pallas import tpu as pltpu

F32 = jnp.float32
BF16 = jnp.bfloat16

D_MODEL = 1024
GRID_W = 64
CONV_W = 512
N_HEADS = 8
N_KV_HEADS = 2
HEAD_DIM = 64
ATT_W = N_HEADS * HEAD_DIM
KV_W = N_KV_HEADS * HEAD_DIM
BLOCK = 128
ROPE_BASE = 10000.0
N_GROUPS = 4
EXPERTS_PER_GROUP = 8
N_EXPERTS = N_GROUPS * EXPERTS_PER_GROUP
EXPERT_FF = 256
N_MOD = 6
NORM_EPS = 1e-6
NEG_INF = -1e30

OFF_B, OFF_CG, OFF_XIN, OFF_Q, OFF_K, OFF_V, OFF_GA, OFF_GB = (
    0, 512, 1024, 1536, 2048, 2176, 2304, 3328)
IN_COLS = 4352

LANES = 128
SEQ_TILE = 512
BLOCKS_PER_TILE = SEQ_TILE // BLOCK
ROUTE_ROWS = 8
GROUP_ROWS = 8
VMEM_LIMIT = 56 * 1024 * 1024


def _cparams(sem):
    return pltpu.CompilerParams(dimension_semantics=sem, vmem_limit_bytes=VMEM_LIMIT)


def _rms_mod(x, g, shift, scale):
    inv = lax.rsqrt(jnp.mean(x * x, axis=-1, keepdims=True) + NORM_EPS)
    return (x * inv) * g * (1.0 + scale) + shift


def _mod_kernel(c_ref, w_ref, b_ref, o_ref):
    c = c_ref[...]
    a = (c * jax.nn.sigmoid(c)).astype(BF16)
    o_ref[...] = jnp.dot(a, w_ref[...].astype(BF16), preferred_element_type=F32) + b_ref[...]


def _modulation(cc, w_ada, b_ada):
    rows = cc.shape[0]
    cols = w_ada.shape[1]
    tile = 1024
    return pl.pallas_call(
        _mod_kernel,
        grid=(cols // tile,),
        in_specs=[pl.BlockSpec((rows, D_MODEL), lambda j: (0, 0)),
                  pl.BlockSpec((D_MODEL, tile), lambda j: (0, j)),
                  pl.BlockSpec((1, tile), lambda j: (0, j))],
        out_specs=pl.BlockSpec((rows, tile), lambda j: (0, j)),
        out_shape=jax.ShapeDtypeStruct((rows, cols), F32),
        compiler_params=_cparams(("arbitrary",)),
        name="adaln_mod",
    )(cc, w_ada, b_ada)


def _ctx_kernel(x_ref, g_ref, sh_ref, sc_ref, w_ref, k_ref, vt_ref):
    h = _rms_mod(x_ref[0], g_ref[...], sh_ref[...], sc_ref[...]).astype(BF16)
    kv = jnp.dot(h, w_ref[...], preferred_element_type=F32)
    k_ref[0] = kv[:, :KV_W].astype(BF16)
    vt_ref[0] = kv[:, KV_W:].T.astype(BF16)


def _context_kv(ctx, norm_g, csh, csc, w_kv):
    bsz, clen, _ = ctx.shape
    return pl.pallas_call(
        _ctx_kernel,
        grid=(bsz,),
        in_specs=[pl.BlockSpec((1, clen, D_MODEL), lambda b: (b, 0, 0)),
                  pl.BlockSpec((1, D_MODEL), lambda b: (0, 0)),
                  pl.BlockSpec((1, D_MODEL), lambda b: (0, 0)),
                  pl.BlockSpec((1, D_MODEL), lambda b: (0, 0)),
                  pl.BlockSpec((D_MODEL, 2 * KV_W), lambda b: (0, 0))],
        out_specs=[pl.BlockSpec((1, clen, KV_W), lambda b: (b, 0, 0)),
                   pl.BlockSpec((1, KV_W, clen), lambda b: (b, 0, 0))],
        out_shape=[jax.ShapeDtypeStruct((bsz, clen, KV_W), BF16),
                   jax.ShapeDtypeStruct((bsz, KV_W, clen), BF16)],
        compiler_params=_cparams(("arbitrary",)),
        name="context_kv",
    )(ctx, norm_g, csh, csc, w_kv)


def _rope(t, cos, sin_signed):
    lane = lax.broadcasted_iota(jnp.int32, (1, LANES), 1)
    first_half = (lane % HEAD_DIM) < (HEAD_DIM // 2)
    outs = []
    for j in range(t.shape[1] // LANES):
        tj = t[:, j * LANES:(j + 1) * LANES]
        partner = jnp.where(first_half,
                            pltpu.roll(tj, LANES - HEAD_DIM // 2, axis=1),
                            pltpu.roll(tj, HEAD_DIM // 2, axis=1))
        outs.append(tj * cos + partner * sin_signed)
    return outs[0] if len(outs) == 1 else jnp.concatenate(outs, axis=1)


def _proj_a_kernel(x_ref, g_ref, sh_ref, sc_ref, w_ref, cos_ref, sin_ref, uc_ref, k_ref, vt_ref):
    h = _rms_mod(x_ref[0], g_ref[...], sh_ref[0], sc_ref[0]).astype(BF16)
    u = jnp.dot(h, w_ref[...], preferred_element_type=F32)
    uc_ref[0] = u[:, :CONV_W] * u[:, CONV_W:2 * CONV_W]
    k = u[:, 2 * CONV_W:2 * CONV_W + KV_W]
    k_ref[0] = _rope(k, cos_ref[...], sin_ref[...]).astype(BF16)
    vt_ref[0] = u[:, 2 * CONV_W + KV_W:].T.astype(BF16)


def _proj_a(x, norm_g, mod3, w_a_cols, cos_t, sin_t):
    bsz, seq, _ = x.shape
    nt = seq // SEQ_TILE
    wcols = w_a_cols.shape[1]
    return pl.pallas_call(
        _proj_a_kernel,
        grid=(nt, bsz),
        in_specs=[pl.BlockSpec((1, SEQ_TILE, D_MODEL), lambda t, b: (b, t, 0)),
                  pl.BlockSpec((1, D_MODEL), lambda t, b: (0, 0)),
                  pl.BlockSpec((1, 1, D_MODEL), lambda t, b: (b, 0, 0)),
                  pl.BlockSpec((1, 1, D_MODEL), lambda t, b: (b, 0, 1)),
                  pl.BlockSpec((D_MODEL, wcols), lambda t, b: (0, 0)),
                  pl.BlockSpec((SEQ_TILE, LANES), lambda t, b: (t, 0)),
                  pl.BlockSpec((SEQ_TILE, LANES), lambda t, b: (t, 0))],
        out_specs=[pl.BlockSpec((1, SEQ_TILE, CONV_W), lambda t, b: (b, t, 0)),
                   pl.BlockSpec((1, SEQ_TILE, KV_W), lambda t, b: (b, t, 0)),
                   pl.BlockSpec((1, KV_W, SEQ_TILE), lambda t, b: (b, 0, t))],
        out_shape=[jax.ShapeDtypeStruct((bsz, seq, CONV_W), F32),
                   jax.ShapeDtypeStruct((bsz, seq, KV_W), BF16),
                   jax.ShapeDtypeStruct((bsz, KV_W, seq), BF16)],
        compiler_params=_cparams(("arbitrary", "arbitrary")),
        name="proj_a",
    )(x, norm_g, mod3, mod3, w_a_cols, cos_t, sin_t)


def _attention_block(q_blk, k_win, k_ctx, vt_win, vt_ctx, bias, sink_row):
    q_t = q_blk.T.astype(BF16)
    zeros = jnp.zeros((HEAD_DIM, 4 * BLOCK), BF16)
    bias4 = jnp.concatenate([bias] * 4, axis=1)
    o_rows = []
    for g in range(N_KV_HEADS):
        qg = jnp.concatenate([q_t[(4 * g + h) * HEAD_DIM:(4 * g + h + 1) * HEAD_DIM, :]
                              for h in range(4)], axis=1)
        rhs = jnp.concatenate([qg, zeros] if g == 0 else [zeros, qg], axis=0)
        s_win = jnp.dot(k_win, rhs, preferred_element_type=F32) + bias4
        s_ctx = jnp.dot(k_ctx, rhs, preferred_element_type=F32)
        sink = sink_row[:, g * 4 * BLOCK:(g + 1) * 4 * BLOCK]
        m = jnp.maximum(jnp.maximum(jnp.max(s_win, axis=0, keepdims=True),
                                    jnp.max(s_ctx, axis=0, keepdims=True)), sink)
        p_win = jnp.exp(s_win - m)
        p_ctx = jnp.exp(s_ctx - m)
        denom = (jnp.sum(p_win, axis=0, keepdims=True) + jnp.sum(p_ctx, axis=0, keepdims=True)
                 + jnp.exp(sink - m))
        o_t = (jnp.dot(vt_win[g * HEAD_DIM:(g + 1) * HEAD_DIM, :], p_win.astype(BF16),
                       preferred_element_type=F32)
               + jnp.dot(vt_ctx[g * HEAD_DIM:(g + 1) * HEAD_DIM, :], p_ctx.astype(BF16),
                         preferred_element_type=F32))
        o_t = o_t / denom
        for h in range(4):
            o_rows.append(o_t[:, h * BLOCK:(h + 1) * BLOCK])
    return jnp.concatenate(o_rows, axis=0).T


def _route(logits_t):
    t = logits_t.shape[1]
    grow = lax.broadcasted_iota(jnp.int32, (GROUP_ROWS, t), 0)
    gl = jnp.where(grow < N_GROUPS, logits_t[0:GROUP_ROWS, :], NEG_INF)
    gm = jnp.max(gl, axis=0, keepdims=True)
    p_g = 1.0 / jnp.sum(jnp.exp(gl - gm), axis=0, keepdims=True)
    g_idx = jnp.min(jnp.where(gl == gm, grow, N_GROUPS), axis=0, keepdims=True)

    erow = lax.broadcasted_iota(jnp.int32, (N_EXPERTS, t), 0)
    el = logits_t[GROUP_ROWS:GROUP_ROWS + N_EXPERTS, :]
    sel = (erow // EXPERTS_PER_GROUP) == g_idx
    em = jnp.where(sel, el, NEG_INF)
    m1 = jnp.max(em, axis=0, keepdims=True)
    i1 = jnp.min(jnp.where(em == m1, erow, N_EXPERTS), axis=0, keepdims=True)
    em2 = jnp.where(erow == i1, NEG_INF, em)
    m2 = jnp.max(em2, axis=0, keepdims=True)
    i2 = jnp.min(jnp.where(em2 == m2, erow, N_EXPERTS), axis=0, keepdims=True)
    z = jnp.sum(jnp.where(sel, jnp.exp(el - m1), 0.0), axis=0, keepdims=True)
    p1 = 1.0 / z
    p2 = jnp.exp(m2 - m1) / z
    gate1 = p_g * p1 / (p1 + p2)
    gate2 = p_g * p2 / (p1 + p2)
    pad = jnp.zeros((ROUTE_ROWS - 4, t), F32)
    return jnp.concatenate([i1.astype(F32), i2.astype(F32), gate1, gate2, pad], axis=0)


def _mix_kernel(x_ref, g1n_ref, g2n_ref, sh1_ref, sc1_ref, gt1_ref, sh2_ref, sc2_ref,
                wb_ref, cos_ref, sin_ref,
                uc_ref, ucp_ref, ucn_ref, k_ref, kp_ref, kn_ref, vt_ref, vtp_ref, vtn_ref,
                kc_ref, vtc_ref, wconv_ref, bconv_ref, wa_ref, wbb_ref, wo_ref,
                sink_ref, bias_ref, wr_ref, br_ref,
                x1_ref, h2_ref, route_ref):
    t_idx = pl.program_id(0)
    n_tiles = pl.num_programs(0)
    is_first = t_idx == 0
    is_last = t_idx == n_tiles - 1

    x = x_ref[0]
    hb = _rms_mod(x, g1n_ref[...], sh1_ref[0], sc1_ref[0]).astype(BF16)

    bq = jnp.dot(hb, wb_ref[:, 0:2 * CONV_W], preferred_element_type=F32)
    uc = uc_ref[0]
    row = lax.broadcasted_iota(jnp.int32, (SEQ_TILE, 1), 0)
    prev_row = jnp.where(is_first, 0.0, ucp_ref[0][7:8, :])
    next_row = jnp.where(is_last, 0.0, ucn_ref[0][0:1, :])
    up = jnp.where(row == 0, prev_row, pltpu.roll(uc, 1, axis=0))
    dn = jnp.where(row == SEQ_TILE - 1, next_row, pltpu.roll(uc, SEQ_TILE - 1, axis=0))
    wconv = wconv_ref[...]
    y = bconv_ref[...] + (up * wconv[0:1, :] + uc * wconv[1:2, :] + dn * wconv[2:3, :])
    ya = jnp.dot((bq[:, :CONV_W] * y).astype(BF16), wa_ref[...], preferred_element_type=F32)

    q = _rope(bq[:, CONV_W:], cos_ref[...], sin_ref[...]) * (HEAD_DIM ** -0.5)
    k_all = jnp.concatenate([kp_ref[0], k_ref[0], kn_ref[0]], axis=0)
    vt_all = jnp.concatenate([vtp_ref[0], vt_ref[0], vtn_ref[0]], axis=1)
    k_ctx = kc_ref[0]
    vt_ctx = vtc_ref[0]
    bias = bias_ref[...]
    sink_row = sink_ref[...]
    key = lax.broadcasted_iota(jnp.int32, (3 * BLOCK, 1), 0)
    o_blocks = []
    for jb in range(BLOCKS_PER_TILE):
        bias_j = bias
        if jb == 0:
            bias_j = bias + jnp.where(jnp.logical_and(is_first, key < BLOCK), NEG_INF, 0.0)
        if jb == BLOCKS_PER_TILE - 1:
            bias_j = bias + jnp.where(jnp.logical_and(is_last, key >= 2 * BLOCK), NEG_INF, 0.0)
        o_blocks.append(_attention_block(
            q[jb * BLOCK:(jb + 1) * BLOCK, :],
            k_all[jb * BLOCK:(jb + 3) * BLOCK, :], k_ctx,
            vt_all[:, jb * BLOCK:(jb + 3) * BLOCK], vt_ctx, bias_j, sink_row))
    o = jnp.concatenate(o_blocks, axis=0).astype(BF16)
    yb = jnp.dot(o, wbb_ref[...], preferred_element_type=F32)

    ga = jnp.dot(hb, wb_ref[:, 2 * CONV_W:2 * CONV_W + D_MODEL], preferred_element_type=F32)
    gb = jnp.dot(hb, wb_ref[:, 2 * CONV_W + D_MODEL:], preferred_element_type=F32)
    merged = (jax.nn.sigmoid(ga) * ya + jax.nn.sigmoid(gb) * yb).astype(BF16)
    x1 = x + gt1_ref[0] * jnp.dot(merged, wo_ref[...], preferred_element_type=F32)
    x1_ref[0] = x1

    h2 = _rms_mod(x1, g2n_ref[...], sh2_ref[0], sc2_ref[0]).astype(BF16)
    h2_ref[0] = h2
    logits_t = lax.dot_general(wr_ref[...], h2, (((1,), (1,)), ((), ())),
                               preferred_element_type=F32) + br_ref[...]
    route_ref[0] = _route(logits_t)


def _mix(x, norm1_g, norm2_g, mod3, w_b_cols, cos_t, sin_t, uc, k_rot, v_t, k_ctx, vt_ctx,
         w_conv, b_conv, w_a, w_b, w_o, sink_row, bias, wr_t, br_col):
    bsz, seq, _ = x.shape
    nt = seq // SEQ_TILE
    nblk = seq // BLOCK
    clen = k_ctx.shape[1]
    rows8 = SEQ_TILE // 8

    def mod_spec(j):
        return pl.BlockSpec((1, 1, D_MODEL), lambda t, b, j=j: (b, 0, j))

    def const_spec(shape):
        return pl.BlockSpec(shape, lambda t, b: tuple(0 for _ in shape))

    in_specs = [
        pl.BlockSpec((1, SEQ_TILE, D_MODEL), lambda t, b: (b, t, 0)),
        const_spec((1, D_MODEL)), const_spec((1, D_MODEL)),
        mod_spec(0), mod_spec(1), mod_spec(2), mod_spec(3), mod_spec(4),
        const_spec(w_b_cols.shape),
        pl.BlockSpec((SEQ_TILE, LANES), lambda t, b: (t, 0)),
        pl.BlockSpec((SEQ_TILE, LANES), lambda t, b: (t, 0)),
        pl.BlockSpec((1, SEQ_TILE, CONV_W), lambda t, b: (b, t, 0)),
        pl.BlockSpec((1, 8, CONV_W), lambda t, b: (b, jnp.maximum(t * rows8 - 1, 0), 0)),
        pl.BlockSpec((1, 8, CONV_W),
                     lambda t, b: (b, jnp.minimum((t + 1) * rows8, seq // 8 - 1), 0)),
        pl.BlockSpec((1, SEQ_TILE, KV_W), lambda t, b: (b, t, 0)),
        pl.BlockSpec((1, BLOCK, KV_W),
                     lambda t, b: (b, jnp.maximum(t * BLOCKS_PER_TILE - 1, 0), 0)),
        pl.BlockSpec((1, BLOCK, KV_W),
                     lambda t, b: (b, jnp.minimum((t + 1) * BLOCKS_PER_TILE, nblk - 1), 0)),
        pl.BlockSpec((1, KV_W, SEQ_TILE), lambda t, b: (b, 0, t)),
        pl.BlockSpec((1, KV_W, BLOCK),
                     lambda t, b: (b, 0, jnp.maximum(t * BLOCKS_PER_TILE - 1, 0))),
        pl.BlockSpec((1, KV_W, BLOCK),
                     lambda t, b: (b, 0, jnp.minimum((t + 1) * BLOCKS_PER_TILE, nblk - 1))),
        pl.BlockSpec((1, clen, KV_W), lambda t, b: (b, 0, 0)),
        pl.BlockSpec((1, KV_W, clen), lambda t, b: (b, 0, 0)),
        const_spec(w_conv.shape), const_spec(b_conv.shape),
        const_spec(w_a.shape), const_spec(w_b.shape), const_spec(w_o.shape),
        const_spec(sink_row.shape), const_spec(bias.shape),
        const_spec(wr_t.shape), const_spec(br_col.shape),
    ]
    out_specs = [
        pl.BlockSpec((1, SEQ_TILE, D_MODEL), lambda t, b: (b, t, 0)),
        pl.BlockSpec((1, SEQ_TILE, D_MODEL), lambda t, b: (b, t, 0)),
        pl.BlockSpec((1, ROUTE_ROWS, SEQ_TILE), lambda t, b: (b, 0, t)),
    ]
    out_shape = [
        jax.ShapeDtypeStruct((bsz, seq, D_MODEL), F32),
        jax.ShapeDtypeStruct((bsz, seq, D_MODEL), BF16),
        jax.ShapeDtypeStruct((bsz, ROUTE_ROWS, seq), F32),
    ]
    return pl.pallas_call(
        _mix_kernel,
        grid=(nt, bsz),
        in_specs=in_specs,
        out_specs=out_specs,
        out_shape=out_shape,
        compiler_params=_cparams(("arbitrary", "arbitrary")),
        name="token_mix",
    )(x, norm1_g, norm2_g, mod3, mod3, mod3, mod3, mod3, w_b_cols, cos_t, sin_t,
      uc, uc, uc, k_rot, k_rot, k_rot, v_t, v_t, v_t, k_ctx, vt_ctx,
      w_conv, b_conv, w_a, w_b, w_o, sink_row, bias, wr_t, br_col)


CHUNK = SEQ_TILE
ROW_BLOCK = 16
CHUNK_ROWS = -(-(2 * CHUNK + N_EXPERTS * (ROW_BLOCK - 1)) // 256) * 256
GATE_COLS = LANES
XL_COLS = D_MODEL + GATE_COLS
DISP_ROWS = 256


def _bf16_parts(v):
    hi = v.astype(BF16).astype(F32)
    r1 = v - hi
    mid = r1.astype(BF16).astype(F32)
    lo = (r1 - mid).astype(BF16).astype(F32)
    return hi, mid, lo


def _dispatch_kernel(h_ref, route_ref, upper_ref, lower_ref, xl_ref, yz_ref, pos_ref, nb_ref):
    route = route_ref[0]
    e1 = route[0:1, :].astype(jnp.int32)
    e2 = route[1:2, :].astype(jnp.int32)
    erow = lax.broadcasted_iota(jnp.int32, (N_EXPERTS, CHUNK), 0)
    hit1 = erow == e1
    hit2 = erow == e2
    onehot = jnp.where(hit1, 1.0, 0.0) + jnp.where(hit2, 1.0, 0.0)
    cum = jnp.dot(onehot.astype(BF16), upper_ref[...], preferred_element_type=F32)
    cnt = jnp.sum(onehot, axis=1, keepdims=True)
    nblk = jnp.floor((cnt + (ROW_BLOCK - 1)) * (1.0 / ROW_BLOCK))
    nblk_b = jnp.broadcast_to(nblk, (N_EXPERTS, LANES))
    seg = jnp.dot(lower_ref[...], nblk_b.astype(BF16), preferred_element_type=F32) * ROW_BLOCK
    base = seg[:, 0:1] + cum
    pos1 = jnp.sum(jnp.where(hit1, base, 0.0), axis=0, keepdims=True)
    pos2 = jnp.sum(jnp.where(hit2, base, 0.0), axis=0, keepdims=True)
    p1i = pos1.astype(jnp.int32)
    p2i = pos2.astype(jnp.int32)

    prow = lax.broadcasted_iota(jnp.int32, (LANES, CHUNK), 0)
    g1p = _bf16_parts(route[2:3, :])
    g2p = _bf16_parts(route[3:4, :])

    def parts_rows(parts):
        rows = jnp.where(prow == 0, parts[0],
                         jnp.where(prow == 1, parts[1], jnp.where(prow == 2, parts[2], 0.0)))
        return rows.astype(BF16)

    gp1 = parts_rows(g1p)
    gp2 = parts_rows(g2p)
    h = h_ref[...]
    nt_dims = (((1,), (1,)), ((), ()))
    for rc in range(CHUNK_ROWS // DISP_ROWS):
        r = lax.broadcasted_iota(jnp.int32, (DISP_ROWS, CHUNK), 0) + rc * DISP_ROWS
        sel1 = jnp.where(r == p1i, 1.0, 0.0).astype(BF16)
        sel2 = jnp.where(r == p2i, 1.0, 0.0).astype(BF16)
        rows = pl.ds(rc * DISP_ROWS, DISP_ROWS)
        xl_ref[rows, 0:D_MODEL] = jnp.dot(sel1 + sel2, h, preferred_element_type=F32).astype(BF16)
        xl_ref[rows, D_MODEL:XL_COLS] = (
            lax.dot_general(sel1, gp1, nt_dims, preferred_element_type=F32)
            + lax.dot_general(sel2, gp2, nt_dims, preferred_element_type=F32)).astype(BF16)

    yz_ref[...] = jnp.zeros_like(yz_ref)
    pos_rows = jnp.where(prow == 0, pos1, jnp.where(prow == 1, pos2, 0.0))
    pos_ref[...] = pos_rows.T
    nb_ref[0] = nblk_b.astype(jnp.int32)


def _dispatch(h2, route, upper, lower):
    n = h2.shape[0]
    nch = n // CHUNK
    bsz = route.shape[0]
    per_seq = route.shape[2] // CHUNK
    return pl.pallas_call(
        _dispatch_kernel,
        grid=(nch,),
        in_specs=[pl.BlockSpec((CHUNK, D_MODEL), lambda c: (c, 0)),
                  pl.BlockSpec((1, ROUTE_ROWS, CHUNK), lambda c: (c // per_seq, 0, c % per_seq)),
                  pl.BlockSpec(upper.shape, lambda c: (0, 0)),
                  pl.BlockSpec(lower.shape, lambda c: (0, 0))],
        out_specs=[pl.BlockSpec((CHUNK_ROWS, XL_COLS), lambda c: (c, 0)),
                   pl.BlockSpec((CHUNK_ROWS, D_MODEL), lambda c: (c, 0)),
                   pl.BlockSpec((CHUNK, LANES), lambda c: (c, 0)),
                   pl.BlockSpec((1, N_EXPERTS, LANES), lambda c: (c, 0, 0))],
        out_shape=[jax.ShapeDtypeStruct((nch * CHUNK_ROWS, XL_COLS), BF16),
                   jax.ShapeDtypeStruct((nch * CHUNK_ROWS, D_MODEL), BF16),
                   jax.ShapeDtypeStruct((n, LANES), F32),
                   jax.ShapeDtypeStruct((nch, N_EXPERTS, LANES), jnp.int32)],
        compiler_params=_cparams(("arbitrary",)),
        name="moe_dispatch",
    )(h2, route, upper, lower)


TILE_BLOCKS = 16
TILE_ROWS = TILE_BLOCKS * ROW_BLOCK


def _tile_plan(nb, nch):
    max_blocks = nch * (2 * CHUNK + N_EXPERTS * (ROW_BLOCK - 1)) // ROW_BLOCK
    n_tiles = max_blocks // TILE_BLOCKS + N_EXPERTS
    nbe = jnp.sum(nb, axis=0)
    nte = (nbe + TILE_BLOCKS - 1) // TILE_BLOCKS
    tile_end = jnp.cumsum(nte)
    tile_start = tile_end - nte
    n_act = tile_end[-1]
    i = jnp.arange(n_tiles, dtype=jnp.int32)
    te = jnp.sum((i[:, None] >= tile_end[None, :]).astype(jnp.int32), axis=1)
    te = jnp.minimum(te, N_EXPERTS - 1)
    te = jnp.where(i < n_act, te, te[jnp.maximum(n_act - 1, 0)])
    j = TILE_BLOCKS * (i - tile_start[te])[:, None] + jnp.arange(TILE_BLOCKS, dtype=jnp.int32)[None, :]
    valid = (j < nbe[te][:, None]) & (i < n_act)[:, None]
    cb_incl = jnp.cumsum(nb, axis=0)
    cb_t = cb_incl[:, te]
    c = jnp.sum((cb_t.T[:, None, :] <= j[:, :, None]).astype(jnp.int32), axis=2)
    c = jnp.minimum(c, nch - 1)
    cb_excl = cb_incl - nb
    seg_off = ROW_BLOCK * (jnp.cumsum(nb, axis=1) - nb)
    e_b = jnp.broadcast_to(te[:, None], c.shape)
    row = c * CHUNK_ROWS + seg_off[c, e_b] + ROW_BLOCK * (j - cb_excl[c, e_b])
    row = jnp.where(valid, row, 0).astype(jnp.int32)
    nvalid = jnp.sum(valid.astype(jnp.int32), axis=1)
    return te.astype(jnp.int32), nvalid, row.reshape(-1)


def _gmm_kernel(te_ref, nv_ref, row_ref, xl_hbm, yz_hbm, wu_ref, wd_ref, yl_hbm,
                xbuf, ybuf, wu_bf, wd_bf, in_sem, out_sem):
    del yz_hbm
    i = pl.program_id(0)
    n_tiles = pl.num_programs(0)
    slot = i % 2

    def gather(tile, s, b):
        row = pl.multiple_of(row_ref[tile * TILE_BLOCKS + b], ROW_BLOCK)
        return pltpu.make_async_copy(xl_hbm.at[pl.ds(row, ROW_BLOCK), :],
                                     xbuf.at[s, pl.ds(b * ROW_BLOCK, ROW_BLOCK), :], in_sem.at[s])

    def scatter(tile, s, b):
        row = pl.multiple_of(row_ref[tile * TILE_BLOCKS + b], ROW_BLOCK)
        return pltpu.make_async_copy(ybuf.at[s, pl.ds(b * ROW_BLOCK, ROW_BLOCK), :],
                                     yl_hbm.at[pl.ds(row, ROW_BLOCK), :], out_sem.at[s])

    def for_valid_blocks(tile, fn):
        nv = nv_ref[tile]
        for b in range(TILE_BLOCKS):
            @pl.when(b < nv)
            def _():
                fn(b)

    @pl.when(i == 0)
    def _():
        xbuf[...] = jnp.zeros_like(xbuf)
        for_valid_blocks(0, lambda b: gather(0, 0, b).start())

    @pl.when(i + 1 < n_tiles)
    def _():
        for_valid_blocks(i + 1, lambda b: gather(i + 1, 1 - slot, b).start())

    for_valid_blocks(i, lambda b: gather(i, slot, b).wait())

    @pl.when(i >= 2)
    def _():
        for_valid_blocks(i - 2, lambda b: scatter(i - 2, slot, b).wait())

    new_expert = jnp.logical_or(i == 0, te_ref[i] != te_ref[jnp.maximum(i - 1, 0)])

    @pl.when(new_expert)
    def _():
        wu_bf[...] = wu_ref[0].astype(BF16)
        wd_bf[...] = wd_ref[0].astype(BF16)

    @pl.when(nv_ref[i] > 0)
    def _():
        x = xbuf[slot]
        gate = jnp.sum(x[:, D_MODEL:].astype(F32), axis=1, keepdims=True)
        au = jnp.dot(x[:, :D_MODEL], wu_bf[...], preferred_element_type=F32)
        a = au[:, :EXPERT_FF]
        act = (a * jax.nn.sigmoid(a)) * au[:, EXPERT_FF:]
        y = jnp.dot(act.astype(BF16), wd_bf[...], preferred_element_type=F32)
        ybuf[slot] = (gate * y).astype(BF16)

    for_valid_blocks(i, lambda b: scatter(i, slot, b).start())

    @pl.when(i == n_tiles - 1)
    def _():
        for_valid_blocks(i - 1, lambda b: scatter(i - 1, 1 - slot, b).wait())
        for_valid_blocks(i, lambda b: scatter(i, slot, b).wait())


def _grouped_mlp(te, nvalid, rows, xl, yz, w_up, w_down):
    n_tiles = te.shape[0]
    grid_spec = pltpu.PrefetchScalarGridSpec(
        num_scalar_prefetch=3,
        grid=(n_tiles,),
        in_specs=[pl.BlockSpec(memory_space=pl.ANY),
                  pl.BlockSpec(memory_space=pl.ANY),
                  pl.BlockSpec((1, D_MODEL, 2 * EXPERT_FF), lambda i, te, nv, rw: (te[i], 0, 0)),
                  pl.BlockSpec((1, EXPERT_FF, D_MODEL), lambda i, te, nv, rw: (te[i], 0, 0))],
        out_specs=pl.BlockSpec(memory_space=pl.ANY),
        scratch_shapes=[pltpu.VMEM((2, TILE_ROWS, XL_COLS), BF16),
                        pltpu.VMEM((2, TILE_ROWS, D_MODEL), BF16),
                        pltpu.VMEM((D_MODEL, 2 * EXPERT_FF), BF16),
                        pltpu.VMEM((EXPERT_FF, D_MODEL), BF16),
                        pltpu.SemaphoreType.DMA((2,)),
                        pltpu.SemaphoreType.DMA((2,))])
    return pl.pallas_call(
        _gmm_kernel,
        grid_spec=grid_spec,
        out_shape=jax.ShapeDtypeStruct(yz.shape, yz.dtype),
        input_output_aliases={4: 0},
        compiler_params=_cparams(("arbitrary",)),
        name="moe_experts",
    )(te, nvalid, rows, xl, yz, w_up, w_down)


def _combine_kernel(yl_ref, pos_ref, x1_ref, gt2_ref, fg_ref, o_ref):
    pos = pos_ref[...]
    p1 = pos[:, 0:1].astype(jnp.int32)
    p2 = pos[:, 1:2].astype(jnp.int32)
    r = lax.broadcasted_iota(jnp.int32, (CHUNK, CHUNK_ROWS), 1)
    sel = (jnp.where(r == p1, 1.0, 0.0) + jnp.where(r == p2, 1.0, 0.0)).astype(BF16)
    y = jnp.dot(sel, yl_ref[...], preferred_element_type=F32)
    x2 = x1_ref[...] + gt2_ref[0] * y
    inv = lax.rsqrt(jnp.mean(x2 * x2, axis=-1, keepdims=True) + NORM_EPS)
    o_ref[...] = (x2 * inv) * fg_ref[...]


def _combine(yl, pos_c, x1, mod3, final_g, seq):
    n = x1.shape[0]
    per_seq = seq // CHUNK
    return pl.pallas_call(
        _combine_kernel,
        grid=(n // CHUNK,),
        in_specs=[pl.BlockSpec((CHUNK_ROWS, D_MODEL), lambda c: (c, 0)),
                  pl.BlockSpec((CHUNK, LANES), lambda c: (c, 0)),
                  pl.BlockSpec((CHUNK, D_MODEL), lambda c: (c, 0)),
                  pl.BlockSpec((1, 1, D_MODEL), lambda c: (c // per_seq, 0, 5)),
                  pl.BlockSpec((1, D_MODEL), lambda c: (0, 0))],
        out_specs=pl.BlockSpec((CHUNK, D_MODEL), lambda c: (c, 0)),
        out_shape=jax.ShapeDtypeStruct((n, D_MODEL), F32),
        compiler_params=_cparams(("arbitrary",)),
        name="moe_combine",
    )(yl, pos_c, x1, mod3, final_g)


def _rope_tables(seq):
    n_freq = HEAD_DIM // 4
    inv_freq = ROPE_BASE ** (-jnp.arange(n_freq, dtype=F32) / n_freq)
    rows = seq // GRID_W
    row = jnp.repeat(jnp.arange(rows, dtype=F32), GRID_W)
    col = jnp.tile(jnp.arange(GRID_W, dtype=F32), rows)
    ang = jnp.concatenate([row[:, None] * inv_freq, col[:, None] * inv_freq], axis=-1)
    cos, sin = jnp.cos(ang), jnp.sin(ang)
    return jnp.tile(cos, (1, 4)), jnp.concatenate([-sin, sin, -sin, sin], axis=1)


def _window_bias():
    key = np.arange(3 * BLOCK)[:, None]
    qry = np.arange(BLOCK)[None, :]
    valid = (key - qry >= 0) & (key - qry <= 2 * BLOCK)
    return jnp.asarray(np.where(valid, 0.0, NEG_INF), F32)


def kernel(x, c, ctx, c_ctx, w_ada, b_ada, norm1_g, w_in, w_conv, b_conv, w_a, w_b, sink, w_o,
           norm2_g, w_group, b_group, w_router, b_router, w_up, w_down, final_g):
    bsz, seq, _ = x.shape
    assert w_ada.shape[0] == 1 and seq % SEQ_TILE == 0

    cc = jnp.zeros((16, D_MODEL), F32).at[:bsz].set(c).at[bsz].set(c_ctx)
    mod = _modulation(cc, w_ada[0], b_ada[0][None, :])
    mod3 = mod[:bsz].reshape(bsz, 1, N_MOD * D_MODEL)
    csh1 = mod[bsz:bsz + 1, 0:D_MODEL]
    csc1 = mod[bsz:bsz + 1, D_MODEL:2 * D_MODEL]

    w = w_in[0]
    n1g = norm1_g[0][None, :]
    n2g = norm2_g[0][None, :]
    w_kv = w[:, OFF_K:OFF_GA].astype(BF16)
    w_a_cols = jnp.concatenate([w[:, OFF_CG:OFF_Q], w[:, OFF_K:OFF_GA]], axis=1).astype(BF16)
    w_b_cols = jnp.concatenate([w[:, OFF_B:OFF_CG], w[:, OFF_Q:OFF_K], w[:, OFF_GA:]],
                               axis=1).astype(BF16)
    cos_t, sin_t = _rope_tables(seq)

    k_ctx, vt_ctx = _context_kv(ctx, n1g, csh1, csc1, w_kv)
    uc, k_rot, v_t = _proj_a(x, n1g, mod3, w_a_cols, cos_t, sin_t)

    sink_row = jnp.repeat(sink[0].astype(F32), BLOCK)[None, :]
    wr_t = jnp.zeros((LANES, D_MODEL), F32)
    wr_t = wr_t.at[0:N_GROUPS].set(w_group[0].T)
    wr_t = wr_t.at[GROUP_ROWS:GROUP_ROWS + N_EXPERTS].set(w_router[0].T).astype(BF16)
    br_col = jnp.zeros((LANES, 1), F32)
    br_col = br_col.at[0:N_GROUPS, 0].set(b_group[0])
    br_col = br_col.at[GROUP_ROWS:GROUP_ROWS + N_EXPERTS, 0].set(b_router[0])

    x1, h2, route = _mix(
        x, n1g, n2g, mod3, w_b_cols, cos_t, sin_t, uc, k_rot, v_t, k_ctx, vt_ctx,
        w_conv[0], b_conv[0][None, :], w_a[0].astype(BF16), w_b[0].astype(BF16),
        w_o[0].astype(BF16), sink_row, _window_bias(), wr_t, br_col)

    n = bsz * seq
    nch = n // CHUNK
    upper = jnp.asarray(np.triu(np.ones((CHUNK, CHUNK), np.float32), 1), BF16)
    lower = jnp.asarray(np.tril(np.ones((N_EXPERTS, N_EXPERTS), np.float32), -1), BF16)
    xl, yz, pos_c, nb = _dispatch(h2.reshape(n, D_MODEL), route, upper, lower)
    te, nvalid, rows = _tile_plan(nb[:, :, 0], nch)
    yl = _grouped_mlp(te, nvalid, rows, xl, yz, w_up[0], w_down[0])
    out = _combine(yl, pos_c, x1.reshape(n, D_MODEL), mod3, final_g[None, :], seq)
    return out.reshape(bsz, seq, D_MODEL)
```

```python
import numpy as np
import jax
import jax.numpy as jnp
from jax import lax
from jax.experimental import pallas as pl
from jax.experimental.pallas import tpu as pltpu

F32 = jnp.float32
BF16 = jnp.bfloat16
I32 = jnp.int32

D_MODEL = 1024
GRID_W = 64
CONV_W = 512
N_HEADS = 8
N_KV_HEADS = 2
HEAD_DIM = 64
ATT_W = N_HEADS * HEAD_DIM
KV_W = N_KV_HEADS * HEAD_DIM
BLOCK = 128
ROPE_BASE = 10000.0
N_GROUPS = 4
EXPERTS_PER_GROUP = 8
N_EXPERTS = N_GROUPS * EXPERTS_PER_GROUP
EXPERT_FF = 256
N_MOD = 6
NORM_EPS = 1e-6
NEG_INF = -1e30

OFF_B, OFF_CG, OFF_XIN, OFF_Q, OFF_K, OFF_V, OFF_GA, OFF_GB = (
    0, 512, 1024, 1536, 2048, 2176, 2304, 3328)
IN_COLS = 4352

LANES = 128
SEQ_TILE = 512
BLOCKS_PER_TILE = SEQ_TILE // BLOCK
ROUTE_ROWS = 8
GROUP_ROWS = 8
VMEM_LIMIT = 56 * 1024 * 1024

CHUNK = SEQ_TILE
ROW_BLOCK = 16
CHUNK_ROWS = -(-(2 * CHUNK + N_EXPERTS * (ROW_BLOCK - 1)) // 256) * 256
TRASH_BLOCKS = 4
CHUNK_STRIDE = CHUNK_ROWS + TRASH_BLOCKS * ROW_BLOCK
GATE_COLS = LANES
XL_COLS = D_MODEL + GATE_COLS
DISP_ROWS = 256
TILE_BLOCKS = 16
TILE_ROWS = TILE_BLOCKS * ROW_BLOCK
X_SLOTS = 4
Y_SLOTS = 4
W_SLOTS = 3


def _cparams(sem):
    return pltpu.CompilerParams(dimension_semantics=sem, vmem_limit_bytes=VMEM_LIMIT)


def _rms_mod(x, g, shift, scale):
    inv = lax.rsqrt(jnp.mean(x * x, axis=-1, keepdims=True) + NORM_EPS)
    return (x * inv) * g * (1.0 + scale) + shift


def _mod_kernel(c_ref, w_ref, b_ref, o_ref):
    c = c_ref[...]
    a = (c * jax.nn.sigmoid(c)).astype(BF16)
    o_ref[...] = jnp.dot(a, w_ref[...].astype(BF16), preferred_element_type=F32) + b_ref[...]


def _modulation(cc, w_ada, b_ada):
    rows = cc.shape[0]
    cols = w_ada.shape[1]
    tile = 1024
    return pl.pallas_call(
        _mod_kernel,
        grid=(cols // tile,),
        in_specs=[pl.BlockSpec((rows, D_MODEL), lambda j: (0, 0)),
                  pl.BlockSpec((D_MODEL, tile), lambda j: (0, j)),
                  pl.BlockSpec((1, tile), lambda j: (0, j))],
        out_specs=pl.BlockSpec((rows, tile), lambda j: (0, j)),
        out_shape=jax.ShapeDtypeStruct((rows, cols), F32),
        compiler_params=_cparams(("arbitrary",)),
        name="adaln_mod",
    )(cc, w_ada, b_ada)


def _ctx_kernel(x_ref, g_ref, sh_ref, sc_ref, w_ref, k_ref, vt_ref):
    h = _rms_mod(x_ref[0], g_ref[...], sh_ref[...], sc_ref[...]).astype(BF16)
    kv = jnp.dot(h, w_ref[...], preferred_element_type=F32)
    k_ref[0] = kv[:, :KV_W].astype(BF16)
    vt_ref[0] = kv[:, KV_W:].T.astype(BF16)


def _context_kv(ctx, norm_g, csh, csc, w_kv):
    bsz, clen, _ = ctx.shape
    return pl.pallas_call(
        _ctx_kernel,
        grid=(bsz,),
        in_specs=[pl.BlockSpec((1, clen, D_MODEL), lambda b: (b, 0, 0)),
                  pl.BlockSpec((1, D_MODEL), lambda b: (0, 0)),
                  pl.BlockSpec((1, D_MODEL), lambda b: (0, 0)),
                  pl.BlockSpec((1, D_MODEL), lambda b: (0, 0)),
                  pl.BlockSpec((D_MODEL, 2 * KV_W), lambda b: (0, 0))],
        out_specs=[pl.BlockSpec((1, clen, KV_W), lambda b: (b, 0, 0)),
                   pl.BlockSpec((1, KV_W, clen), lambda b: (b, 0, 0))],
        out_shape=[jax.ShapeDtypeStruct((bsz, clen, KV_W), BF16),
                   jax.ShapeDtypeStruct((bsz, KV_W, clen), BF16)],
        compiler_params=_cparams(("arbitrary",)),
        name="context_kv",
    )(ctx, norm_g, csh, csc, w_kv)


def _rope(t, cos, sin_signed):
    lane = lax.broadcasted_iota(I32, (1, LANES), 1)
    first_half = (lane % HEAD_DIM) < (HEAD_DIM // 2)
    outs = []
    for j in range(t.shape[1] // LANES):
        tj = t[:, j * LANES:(j + 1) * LANES]
        partner = jnp.where(first_half,
                            pltpu.roll(tj, LANES - HEAD_DIM // 2, axis=1),
                            pltpu.roll(tj, HEAD_DIM // 2, axis=1))
        outs.append(tj * cos + partner * sin_signed)
    return outs[0] if len(outs) == 1 else jnp.concatenate(outs, axis=1)


def _proj_a_kernel(x_ref, g_ref, sh_ref, sc_ref, w_ref, cos_ref, sin_ref, uc_ref, k_ref, vt_ref):
    h = _rms_mod(x_ref[0], g_ref[...], sh_ref[0], sc_ref[0]).astype(BF16)
    u = jnp.dot(h, w_ref[...], preferred_element_type=F32)
    uc_ref[0] = u[:, :CONV_W] * u[:, CONV_W:2 * CONV_W]
    k = u[:, 2 * CONV_W:2 * CONV_W + KV_W]
    k_ref[0] = _rope(k, cos_ref[...], sin_ref[...]).astype(BF16)
    vt_ref[0] = u[:, 2 * CONV_W + KV_W:].T.astype(BF16)


def _proj_a(x, norm_g, mod3, w_a_cols, cos_t, sin_t):
    bsz, seq, _ = x.shape
    nt = seq // SEQ_TILE
    wcols = w_a_cols.shape[1]
    return pl.pallas_call(
        _proj_a_kernel,
        grid=(nt, bsz),
        in_specs=[pl.BlockSpec((1, SEQ_TILE, D_MODEL), lambda t, b: (b, t, 0)),
                  pl.BlockSpec((1, D_MODEL), lambda t, b: (0, 0)),
                  pl.BlockSpec((1, 1, D_MODEL), lambda t, b: (b, 0, 0)),
                  pl.BlockSpec((1, 1, D_MODEL), lambda t, b: (b, 0, 1)),
                  pl.BlockSpec((D_MODEL, wcols), lambda t, b: (0, 0)),
                  pl.BlockSpec((SEQ_TILE, LANES), lambda t, b: (t, 0)),
                  pl.BlockSpec((SEQ_TILE, LANES), lambda t, b: (t, 0))],
        out_specs=[pl.BlockSpec((1, SEQ_TILE, CONV_W), lambda t, b: (b, t, 0)),
                   pl.BlockSpec((1, SEQ_TILE, KV_W), lambda t, b: (b, t, 0)),
                   pl.BlockSpec((1, KV_W, SEQ_TILE), lambda t, b: (b, 0, t))],
        out_shape=[jax.ShapeDtypeStruct((bsz, seq, CONV_W), F32),
                   jax.ShapeDtypeStruct((bsz, seq, KV_W), BF16),
                   jax.ShapeDtypeStruct((bsz, KV_W, seq), BF16)],
        compiler_params=_cparams(("arbitrary", "arbitrary")),
        name="proj_a",
    )(x, norm_g, mod3, mod3, w_a_cols, cos_t, sin_t)


def _attention_block(q_blk, k_win, k_ctx, vt_win, vt_ctx, bias, sink_row):
    q_t = q_blk.T.astype(BF16)
    zeros = jnp.zeros((HEAD_DIM, 4 * BLOCK), BF16)
    bias4 = jnp.concatenate([bias] * 4, axis=1)
    o_rows = []
    for g in range(N_KV_HEADS):
        qg = jnp.concatenate([q_t[(4 * g + h) * HEAD_DIM:(4 * g + h + 1) * HEAD_DIM, :]
                              for h in range(4)], axis=1)
        rhs = jnp.concatenate([qg, zeros] if g == 0 else [zeros, qg], axis=0)
        s_win = jnp.dot(k_win, rhs, preferred_element_type=F32) + bias4
        s_ctx = jnp.dot(k_ctx, rhs, preferred_element_type=F32)
        sink = sink_row[:, g * 4 * BLOCK:(g + 1) * 4 * BLOCK]
        m = jnp.maximum(jnp.maximum(jnp.max(s_win, axis=0, keepdims=True),
                                    jnp.max(s_ctx, axis=0, keepdims=True)), sink)
        p_win = jnp.exp(s_win - m)
        p_ctx = jnp.exp(s_ctx - m)
        denom = (jnp.sum(p_win, axis=0, keepdims=True) + jnp.sum(p_ctx, axis=0, keepdims=True)
                 + jnp.exp(sink - m))
        o_t = (jnp.dot(vt_win[g * HEAD_DIM:(g + 1) * HEAD_DIM, :], p_win.astype(BF16),
                       preferred_element_type=F32)
               + jnp.dot(vt_ctx[g * HEAD_DIM:(g + 1) * HEAD_DIM, :], p_ctx.astype(BF16),
                         preferred_element_type=F32))
        o_t = o_t / denom
        for h in range(4):
            o_rows.append(o_t[:, h * BLOCK:(h + 1) * BLOCK])
    return jnp.concatenate(o_rows, axis=0).T


def _route(logits_t):
    t = logits_t.shape[1]
    grow = lax.broadcasted_iota(I32, (GROUP_ROWS, t), 0)
    gl = jnp.where(grow < N_GROUPS, logits_t[0:GROUP_ROWS, :], NEG_INF)
    gm = jnp.max(gl, axis=0, keepdims=True)
    p_g = 1.0 / jnp.sum(jnp.exp(gl - gm), axis=0, keepdims=True)
    g_idx = jnp.min(jnp.where(gl == gm, grow, N_GROUPS), axis=0, keepdims=True)

    erow = lax.broadcasted_iota(I32, (N_EXPERTS, t), 0)
    el = logits_t[GROUP_ROWS:GROUP_ROWS + N_EXPERTS, :]
    sel = (erow // EXPERTS_PER_GROUP) == g_idx
    em = jnp.where(sel, el, NEG_INF)
    m1 = jnp.max(em, axis=0, keepdims=True)
    i1 = jnp.min(jnp.where(em == m1, erow, N_EXPERTS), axis=0, keepdims=True)
    em2 = jnp.where(erow == i1, NEG_INF, em)
    m2 = jnp.max(em2, axis=0, keepdims=True)
    i2 = jnp.min(jnp.where(em2 == m2, erow, N_EXPERTS), axis=0, keepdims=True)
    z = jnp.sum(jnp.where(sel, jnp.exp(el - m1), 0.0), axis=0, keepdims=True)
    p1 = 1.0 / z
    p2 = jnp.exp(m2 - m1) / z
    gate1 = p_g * p1 / (p1 + p2)
    gate2 = p_g * p2 / (p1 + p2)
    pad = jnp.zeros((ROUTE_ROWS - 4, t), F32)
    return jnp.concatenate([i1.astype(F32), i2.astype(F32), gate1, gate2, pad], axis=0)


def _mix_kernel(x_ref, g1n_ref, g2n_ref, sh1_ref, sc1_ref, gt1_ref, sh2_ref, sc2_ref,
                wb_ref, cos_ref, sin_ref,
                uc_ref, ucp_ref, ucn_ref, k_ref, kp_ref, kn_ref, vt_ref, vtp_ref, vtn_ref,
                kc_ref, vtc_ref, wconv_ref, bconv_ref, wa_ref, wbb_ref, wo_ref,
                sink_ref, bias_ref, wr_ref, br_ref,
                x1_ref, h2_ref, route_ref, yz_ref):
    t_idx = pl.program_id(0)
    n_tiles = pl.num_programs(0)
    is_first = t_idx == 0
    is_last = t_idx == n_tiles - 1

    x = x_ref[0]
    hb = _rms_mod(x, g1n_ref[...], sh1_ref[0], sc1_ref[0]).astype(BF16)

    bq = jnp.dot(hb, wb_ref[:, 0:2 * CONV_W], preferred_element_type=F32)
    uc = uc_ref[0]
    row = lax.broadcasted_iota(I32, (SEQ_TILE, 1), 0)
    prev_row = jnp.where(is_first, 0.0, ucp_ref[0][7:8, :])
    next_row = jnp.where(is_last, 0.0, ucn_ref[0][0:1, :])
    up = jnp.where(row == 0, prev_row, pltpu.roll(uc, 1, axis=0))
    dn = jnp.where(row == SEQ_TILE - 1, next_row, pltpu.roll(uc, SEQ_TILE - 1, axis=0))
    wconv = wconv_ref[...]
    y = bconv_ref[...] + (up * wconv[0:1, :] + uc * wconv[1:2, :] + dn * wconv[2:3, :])
    ya = jnp.dot((bq[:, :CONV_W] * y).astype(BF16), wa_ref[...], preferred_element_type=F32)

    q = _rope(bq[:, CONV_W:], cos_ref[...], sin_ref[...]) * (HEAD_DIM ** -0.5)
    k_all = jnp.concatenate([kp_ref[0], k_ref[0], kn_ref[0]], axis=0)
    vt_all = jnp.concatenate([vtp_ref[0], vt_ref[0], vtn_ref[0]], axis=1)
    k_ctx = kc_ref[0]
    vt_ctx = vtc_ref[0]
    bias = bias_ref[...]
    sink_row = sink_ref[...]
    key = lax.broadcasted_iota(I32, (3 * BLOCK, 1), 0)
    o_blocks = []
    for jb in range(BLOCKS_PER_TILE):
        bias_j = bias
        if jb == 0:
            bias_j = bias + jnp.where(jnp.logical_and(is_first, key < BLOCK), NEG_INF, 0.0)
        if jb == BLOCKS_PER_TILE - 1:
            bias_j = bias + jnp.where(jnp.logical_and(is_last, key >= 2 * BLOCK), NEG_INF, 0.0)
        o_blocks.append(_attention_block(
            q[jb * BLOCK:(jb + 1) * BLOCK, :],
            k_all[jb * BLOCK:(jb + 3) * BLOCK, :], k_ctx,
            vt_all[:, jb * BLOCK:(jb + 3) * BLOCK], vt_ctx, bias_j, sink_row))
    o = jnp.concatenate(o_blocks, axis=0).astype(BF16)
    yb = jnp.dot(o, wbb_ref[...], preferred_element_type=F32)

    ga = jnp.dot(hb, wb_ref[:, 2 * CONV_W:2 * CONV_W + D_MODEL], preferred_element_type=F32)
    gb = jnp.dot(hb, wb_ref[:, 2 * CONV_W + D_MODEL:], preferred_element_type=F32)
    merged = (jax.nn.sigmoid(ga) * ya + jax.nn.sigmoid(gb) * yb).astype(BF16)
    x1 = x + gt1_ref[0] * jnp.dot(merged, wo_ref[...], preferred_element_type=F32)
    x1_ref[0] = x1

    h2 = _rms_mod(x1, g2n_ref[...], sh2_ref[0], sc2_ref[0]).astype(BF16)
    h2_ref[0] = h2
    logits_t = lax.dot_general(wr_ref[...], h2, (((1,), (1,)), ((), ())),
                               preferred_element_type=F32) + br_ref[...]
    route_ref[0] = _route(logits_t)
    yz_ref[...] = jnp.zeros_like(yz_ref)


def _mix(x, norm1_g, norm2_g, mod3, w_b_cols, cos_t, sin_t, uc, k_rot, v_t, k_ctx, vt_ctx,
         w_conv, b_conv, w_a, w_b, w_o, sink_row, bias, wr_t, br_col):
    bsz, seq, _ = x.shape
    nt = seq // SEQ_TILE
    nblk = seq // BLOCK
    clen = k_ctx.shape[1]
    rows8 = SEQ_TILE // 8

    def mod_spec(j):
        return pl.BlockSpec((1, 1, D_MODEL), lambda t, b, j=j: (b, 0, j))

    def const_spec(shape):
        return pl.BlockSpec(shape, lambda t, b: tuple(0 for _ in shape))

    in_specs = [
        pl.BlockSpec((1, SEQ_TILE, D_MODEL), lambda t, b: (b, t, 0)),
        const_spec((1, D_MODEL)), const_spec((1, D_MODEL)),
        mod_spec(0), mod_spec(1), mod_spec(2), mod_spec(3), mod_spec(4),
        const_spec(w_b_cols.shape),
        pl.BlockSpec((SEQ_TILE, LANES), lambda t, b: (t, 0)),
        pl.BlockSpec((SEQ_TILE, LANES), lambda t, b: (t, 0)),
        pl.BlockSpec((1, SEQ_TILE, CONV_W), lambda t, b: (b, t, 0)),
        pl.BlockSpec((1, 8, CONV_W), lambda t, b: (b, jnp.maximum(t * rows8 - 1, 0), 0)),
        pl.BlockSpec((1, 8, CONV_W),
                     lambda t, b: (b, jnp.minimum((t + 1) * rows8, seq // 8 - 1), 0)),
        pl.BlockSpec((1, SEQ_TILE, KV_W), lambda t, b: (b, t, 0)),
        pl.BlockSpec((1, BLOCK, KV_W),
                     lambda t, b: (b, jnp.maximum(t * BLOCKS_PER_TILE - 1, 0), 0)),
        pl.BlockSpec((1, BLOCK, KV_W),
                     lambda t, b: (b, jnp.minimum((t + 1) * BLOCKS_PER_TILE, nblk - 1), 0)),
        pl.BlockSpec((1, KV_W, SEQ_TILE), lambda t, b: (b, 0, t)),
        pl.BlockSpec((1, KV_W, BLOCK),
                     lambda t, b: (b, 0, jnp.maximum(t * BLOCKS_PER_TILE - 1, 0))),
        pl.BlockSpec((1, KV_W, BLOCK),
                     lambda t, b: (b, 0, jnp.minimum((t + 1) * BLOCKS_PER_TILE, nblk - 1))),
        pl.BlockSpec((1, clen, KV_W), lambda t, b: (b, 0, 0)),
        pl.BlockSpec((1, KV_W, clen), lambda t, b: (b, 0, 0)),
        const_spec(w_conv.shape), const_spec(b_conv.shape),
        const_spec(w_a.shape), const_spec(w_b.shape), const_spec(w_o.shape),
        const_spec(sink_row.shape), const_spec(bias.shape),
        const_spec(wr_t.shape), const_spec(br_col.shape),
    ]
    out_specs = [
        pl.BlockSpec((1, SEQ_TILE, D_MODEL), lambda t, b: (b, t, 0)),
        pl.BlockSpec((1, SEQ_TILE, D_MODEL), lambda t, b: (b, t, 0)),
        pl.BlockSpec((1, ROUTE_ROWS, SEQ_TILE), lambda t, b: (b, 0, t)),
        pl.BlockSpec((CHUNK_STRIDE, D_MODEL), lambda t, b: (b * nt + t, 0)),
    ]
    out_shape = [
        jax.ShapeDtypeStruct((bsz, seq, D_MODEL), F32),
        jax.ShapeDtypeStruct((bsz, seq, D_MODEL), BF16),
        jax.ShapeDtypeStruct((bsz, ROUTE_ROWS, seq), F32),
        jax.ShapeDtypeStruct((bsz * nt * CHUNK_STRIDE, D_MODEL), BF16),
    ]
    return pl.pallas_call(
        _mix_kernel,
        grid=(nt, bsz),
        in_specs=in_specs,
        out_specs=out_specs,
        out_shape=out_shape,
        compiler_params=_cparams(("arbitrary", "arbitrary")),
        name="token_mix",
    )(x, norm1_g, norm2_g, mod3, mod3, mod3, mod3, mod3, w_b_cols, cos_t, sin_t,
      uc, uc, uc, k_rot, k_rot, k_rot, v_t, v_t, v_t, k_ctx, vt_ctx,
      w_conv, b_conv, w_a, w_b, w_o, sink_row, bias, wr_t, br_col)


def _bf16_parts(v):
    hi = v.astype(BF16).astype(F32)
    r1 = v - hi
    mid = r1.astype(BF16).astype(F32)
    lo = (r1 - mid).astype(BF16).astype(F32)
    return hi, mid, lo


def _dispatch_kernel(h_ref, route_ref, upper_ref, lower_ref, xl_ref, pos_ref, nb_ref):
    route = route_ref[0]
    e1 = route[0:1, :].astype(I32)
    e2 = route[1:2, :].astype(I32)
    erow = lax.broadcasted_iota(I32, (N_EXPERTS, CHUNK), 0)
    hit1 = erow == e1
    hit2 = erow == e2
    onehot = jnp.where(hit1, 1.0, 0.0) + jnp.where(hit2, 1.0, 0.0)
    cum = jnp.dot(onehot.astype(BF16), upper_ref[...], preferred_element_type=F32)
    cnt = jnp.sum(onehot, axis=1, keepdims=True)
    nblk = jnp.floor((cnt + (ROW_BLOCK - 1)) * (1.0 / ROW_BLOCK))
    nblk_b = jnp.broadcast_to(nblk, (N_EXPERTS, LANES))
    seg = jnp.dot(lower_ref[...], nblk_b.astype(BF16), preferred_element_type=F32) * ROW_BLOCK
    base = seg[:, 0:1] + cum
    pos1 = jnp.sum(jnp.where(hit1, base, 0.0), axis=0, keepdims=True)
    pos2 = jnp.sum(jnp.where(hit2, base, 0.0), axis=0, keepdims=True)
    p1i = pos1.astype(I32)
    p2i = pos2.astype(I32)
    used_rows = (jnp.sum(nblk) * ROW_BLOCK).astype(I32)

    prow = lax.broadcasted_iota(I32, (LANES, CHUNK), 0)

    def parts_rows(parts):
        rows = jnp.where(prow == 0, parts[0],
                         jnp.where(prow == 1, parts[1], jnp.where(prow == 2, parts[2], 0.0)))
        return rows.astype(BF16)

    gp1 = parts_rows(_bf16_parts(route[2:3, :]))
    gp2 = parts_rows(_bf16_parts(route[3:4, :]))
    nt_dims = (((1,), (1,)), ((), ()))

    for rc in range(CHUNK_ROWS // DISP_ROWS):
        rows = pl.ds(rc * DISP_ROWS, DISP_ROWS)

        def live(rc=rc, rows=rows):
            r = lax.broadcasted_iota(I32, (DISP_ROWS, CHUNK), 0) + rc * DISP_ROWS
            sel1 = jnp.where(r == p1i, 1.0, 0.0).astype(BF16)
            sel2 = jnp.where(r == p2i, 1.0, 0.0).astype(BF16)
            xl_ref[rows, 0:D_MODEL] = jnp.dot(sel1 + sel2, h_ref[...],
                                              preferred_element_type=F32).astype(BF16)
            xl_ref[rows, D_MODEL:XL_COLS] = (
                lax.dot_general(sel1, gp1, nt_dims, preferred_element_type=F32)
                + lax.dot_general(sel2, gp2, nt_dims, preferred_element_type=F32)).astype(BF16)

        def dead(rows=rows):
            xl_ref[rows, :] = jnp.zeros((DISP_ROWS, XL_COLS), BF16)

        if (rc + 1) * DISP_ROWS <= 2 * CHUNK:
            live()
        else:
            pl.when(rc * DISP_ROWS < used_rows)(live)
            pl.when(rc * DISP_ROWS >= used_rows)(dead)
    xl_ref[CHUNK_ROWS:CHUNK_STRIDE, :] = jnp.zeros((CHUNK_STRIDE - CHUNK_ROWS, XL_COLS), BF16)

    pos_rows = jnp.where(prow == 0, pos1, jnp.where(prow == 1, pos2, 0.0))
    pos_ref[...] = pos_rows.T
    nb_ref[0] = nblk_b.astype(I32)


def _dispatch(h2, route, upper, lower):
    n = h2.shape[0]
    nch = n // CHUNK
    per_seq = route.shape[2] // CHUNK
    return pl.pallas_call(
        _dispatch_kernel,
        grid=(nch,),
        in_specs=[pl.BlockSpec((CHUNK, D_MODEL), lambda c: (c, 0)),
                  pl.BlockSpec((1, ROUTE_ROWS, CHUNK), lambda c: (c // per_seq, 0, c % per_seq)),
                  pl.BlockSpec(upper.shape, lambda c: (0, 0)),
                  pl.BlockSpec(lower.shape, lambda c: (0, 0))],
        out_specs=[pl.BlockSpec((CHUNK_STRIDE, XL_COLS), lambda c: (c, 0)),
                   pl.BlockSpec((CHUNK, LANES), lambda c: (c, 0)),
                   pl.BlockSpec((1, N_EXPERTS, LANES), lambda c: (c, 0, 0))],
        out_shape=[jax.ShapeDtypeStruct((nch * CHUNK_STRIDE, XL_COLS), BF16),
                   jax.ShapeDtypeStruct((n, LANES), F32),
                   jax.ShapeDtypeStruct((nch, N_EXPERTS, LANES), I32)],
        compiler_params=_cparams(("arbitrary",)),
        name="moe_dispatch",
    )(h2, route, upper, lower)


def _max_tiles(nch):
    max_blocks = nch * (2 * CHUNK + N_EXPERTS * (ROW_BLOCK - 1)) // ROW_BLOCK
    return max_blocks // TILE_BLOCKS + N_EXPERTS


def _masked_prefix(le, values):
    delta = values - jnp.concatenate([jnp.zeros((1,), values.dtype), values[:-1]])
    return jnp.sum(jnp.where(le, delta[None, :], 0), axis=1)


def _tile_plan(nb, nch):
    n_tiles = _max_tiles(nch)
    nbt = nb.T
    nbe = jnp.sum(nbt, axis=1)
    nte = (nbe + TILE_BLOCKS - 1) // TILE_BLOCKS
    tile_end = jnp.cumsum(nte)
    tile_start = tile_end - nte
    n_act = tile_end[-1]
    tiles = jnp.arange(n_tiles, dtype=I32)
    te = jnp.sum((tile_end[None, :] <= tiles[:, None]).astype(I32), axis=1)
    e_ar = jnp.arange(N_EXPERTS, dtype=I32)
    active = nte > 0
    te = jnp.where(tiles < n_act, te, jnp.max(jnp.where(active, e_ar, 0)))
    first = jnp.logical_and(te != jnp.concatenate([jnp.full((1,), -1, I32), te[:-1]]),
                            tiles < n_act).astype(I32)
    rank = jnp.cumsum(active.astype(I32)) - 1
    k_ar = jnp.arange(N_EXPERTS + W_SLOTS, dtype=I32)
    eseq = jnp.sum(jnp.where(jnp.logical_and(active[None, :], rank[None, :] == k_ar[:, None]),
                             e_ar[None, :], 0), axis=1)
    n_exp = jnp.sum(active.astype(I32))

    cb_excl = jnp.cumsum(nbt, axis=1) - nbt
    gs = TILE_BLOCKS * tile_start[:, None] + cb_excl
    seg_off = (ROW_BLOCK * (jnp.cumsum(nb, axis=1) - nb)).T
    base_row = jnp.arange(nch, dtype=I32)[None, :] * CHUNK_STRIDE + seg_off
    gs_f = gs.reshape(-1)
    slots = jnp.arange(n_tiles * TILE_BLOCKS, dtype=I32)
    le = gs_f[None, :] <= slots[:, None]
    row = ROW_BLOCK * slots + _masked_prefix(le, (base_row - ROW_BLOCK * gs).reshape(-1))
    valid = slots < _masked_prefix(le, (gs + nbt).reshape(-1))
    t = slots % (nch * TRASH_BLOCKS)
    trash = (t % nch) * CHUNK_STRIDE + CHUNK_ROWS + (t // nch) * ROW_BLOCK
    grow = jnp.where(valid, row, 0).astype(I32)
    srow = jnp.where(valid, row, trash).astype(I32)
    return (n_act.reshape(1).astype(I32), n_exp.reshape(1), te.astype(I32), first, eseq,
            grow, srow)


def _gmm_kernel(nact_ref, nexp_ref, te_ref, first_ref, eseq_ref, grow_ref, srow_ref,
                xl_hbm, yz_hbm, wu_hbm, wd_hbm, yl_hbm,
                xbuf, ybuf, wu_st, wd_st, wu_bf, wd_bf, in_sem, out_sem, w_sem):
    del te_ref, yz_hbm
    n_act = nact_ref[0]
    n_exp = nexp_ref[0]
    prefetch = X_SLOTS - 1

    def gather(tile, b):
        s = tile % X_SLOTS
        row = pl.multiple_of(grow_ref[tile * TILE_BLOCKS + b], ROW_BLOCK)
        return pltpu.make_async_copy(xl_hbm.at[pl.ds(row, ROW_BLOCK), :],
                                     xbuf.at[s, pl.ds(b * ROW_BLOCK, ROW_BLOCK), :], in_sem.at[s])

    def scatter(tile, b):
        s = tile % Y_SLOTS
        row = pl.multiple_of(srow_ref[tile * TILE_BLOCKS + b], ROW_BLOCK)
        return pltpu.make_async_copy(ybuf.at[s, pl.ds(b * ROW_BLOCK, ROW_BLOCK), :],
                                     yl_hbm.at[pl.ds(row, ROW_BLOCK), :], out_sem.at[s])

    def weights(q):
        e = eseq_ref[q]
        s = q % W_SLOTS
        return (pltpu.make_async_copy(wu_hbm.at[e], wu_st.at[s], w_sem.at[0, s]),
                pltpu.make_async_copy(wd_hbm.at[e], wd_st.at[s], w_sem.at[1, s]))

    def all_blocks(fn):
        for b in range(TILE_BLOCKS):
            fn(b)

    for q0 in range(W_SLOTS - 1):
        @pl.when(q0 < n_exp)
        def _(q0=q0):
            for cp in weights(q0):
                cp.start()
    for t0 in range(prefetch):
        @pl.when(t0 < n_act)
        def _(t0=t0):
            all_blocks(lambda b: gather(t0, b).start())

    def body(i, q):
        @pl.when(i + prefetch < n_act)
        def _():
            all_blocks(lambda b: gather(i + prefetch, b).start())

        all_blocks(lambda b: gather(i, b).wait())

        @pl.when(i >= Y_SLOTS)
        def _():
            all_blocks(lambda b: scatter(i - Y_SLOTS, b).wait())

        is_first = first_ref[i] == 1

        @pl.when(is_first)
        def _():
            for cp in weights(q):
                cp.wait()
            ws = q % W_SLOTS
            wu_bf[...] = wu_st[ws].astype(BF16)
            wd_bf[...] = wd_st[ws].astype(BF16)

            @pl.when(q + W_SLOTS - 1 < n_exp)
            def _():
                for cp in weights(q + W_SLOTS - 1):
                    cp.start()

        x = xbuf[i % X_SLOTS]
        gate = jnp.sum(x[:, D_MODEL:].astype(F32), axis=1, keepdims=True)
        au = jnp.dot(x[:, :D_MODEL], wu_bf[...], preferred_element_type=F32)
        a = au[:, :EXPERT_FF]
        act = (a * jax.nn.sigmoid(a)) * au[:, EXPERT_FF:]
        y = jnp.dot(act.astype(BF16), wd_bf[...], preferred_element_type=F32)
        ybuf[i % Y_SLOTS] = (gate * y).astype(BF16)

        all_blocks(lambda b: scatter(i, b).start())
        return q + is_first.astype(I32)

    lax.fori_loop(0, n_act, body, jnp.int32(0))

    for k in range(Y_SLOTS):
        @pl.when(n_act - 1 - k >= 0)
        def _(k=k):
            all_blocks(lambda b: scatter(n_act - 1 - k, b).wait())


def _grouped_mlp(plan, xl, yz, w_up, w_down):
    grid_spec = pltpu.PrefetchScalarGridSpec(
        num_scalar_prefetch=len(plan),
        grid=(1,),
        in_specs=[pl.BlockSpec(memory_space=pl.ANY)] * 4,
        out_specs=pl.BlockSpec(memory_space=pl.ANY),
        scratch_shapes=[pltpu.VMEM((X_SLOTS, TILE_ROWS, XL_COLS), BF16),
                        pltpu.VMEM((Y_SLOTS, TILE_ROWS, D_MODEL), BF16),
                        pltpu.VMEM((W_SLOTS, D_MODEL, 2 * EXPERT_FF), F32),
                        pltpu.VMEM((W_SLOTS, EXPERT_FF, D_MODEL), F32),
                        pltpu.VMEM((D_MODEL, 2 * EXPERT_FF), BF16),
                        pltpu.VMEM((EXPERT_FF, D_MODEL), BF16),
                        pltpu.SemaphoreType.DMA((X_SLOTS,)),
                        pltpu.SemaphoreType.DMA((Y_SLOTS,)),
                        pltpu.SemaphoreType.DMA((2, W_SLOTS))])
    return pl.pallas_call(
        _gmm_kernel,
        grid_spec=grid_spec,
        out_shape=jax.ShapeDtypeStruct(yz.shape, yz.dtype),
        input_output_aliases={len(plan) + 1: 0},
        compiler_params=_cparams(("arbitrary",)),
        name="moe_experts",
    )(*plan, xl, yz, w_up, w_down)


def _combine_kernel(used_ref, yl_ref, pos_ref, x1_ref, gt2_ref, fg_ref, o_ref, acc_ref):
    used_rows = used_ref[pl.program_id(0)]
    pos = pos_ref[...]
    p1 = pos[:, 0:1].astype(I32)
    p2 = pos[:, 1:2].astype(I32)

    def partial_sum(k0, k1):
        r = lax.broadcasted_iota(I32, (CHUNK, k1 - k0), 1) + k0
        sel = (jnp.where(r == p1, 1.0, 0.0) + jnp.where(r == p2, 1.0, 0.0)).astype(BF16)
        return jnp.dot(sel, yl_ref[k0:k1, :], preferred_element_type=F32)

    acc_ref[...] = partial_sum(0, 2 * CHUNK)
    for k0 in range(2 * CHUNK, CHUNK_ROWS, DISP_ROWS):
        @pl.when(k0 < used_rows)
        def _(k0=k0):
            acc_ref[...] += partial_sum(k0, k0 + DISP_ROWS)

    x2 = x1_ref[...] + gt2_ref[0] * acc_ref[...]
    inv = lax.rsqrt(jnp.mean(x2 * x2, axis=-1, keepdims=True) + NORM_EPS)
    o_ref[...] = (x2 * inv) * fg_ref[...]


def _combine(used_rows, yl, pos_c, x1, mod3, final_g, seq):
    n = x1.shape[0]
    per_seq = seq // CHUNK
    grid_spec = pltpu.PrefetchScalarGridSpec(
        num_scalar_prefetch=1,
        grid=(n // CHUNK,),
        in_specs=[pl.BlockSpec((CHUNK_STRIDE, D_MODEL), lambda c, u: (c, 0)),
                  pl.BlockSpec((CHUNK, LANES), lambda c, u: (c, 0)),
                  pl.BlockSpec((CHUNK, D_MODEL), lambda c, u: (c, 0)),
                  pl.BlockSpec((1, 1, D_MODEL), lambda c, u: (c // per_seq, 0, 5)),
                  pl.BlockSpec((1, D_MODEL), lambda c, u: (0, 0))],
        out_specs=pl.BlockSpec((CHUNK, D_MODEL), lambda c, u: (c, 0)),
        scratch_shapes=[pltpu.VMEM((CHUNK, D_MODEL), F32)])
    return pl.pallas_call(
        _combine_kernel,
        grid_spec=grid_spec,
        out_shape=jax.ShapeDtypeStruct((n, D_MODEL), F32),
        compiler_params=_cparams(("arbitrary",)),
        name="moe_combine",
    )(used_rows, yl, pos_c, x1, mod3, final_g)


def _rope_tables(seq):
    n_freq = HEAD_DIM // 4
    inv_freq = ROPE_BASE ** (-jnp.arange(n_freq, dtype=F32) / n_freq)
    rows = seq // GRID_W
    row = jnp.repeat(jnp.arange(rows, dtype=F32), GRID_W)
    col = jnp.tile(jnp.arange(GRID_W, dtype=F32), rows)
    ang = jnp.concatenate([row[:, None] * inv_freq, col[:, None] * inv_freq], axis=-1)
    cos, sin = jnp.cos(ang), jnp.sin(ang)
    return jnp.tile(cos, (1, 4)), jnp.concatenate([-sin, sin, -sin, sin], axis=1)


def _window_bias():
    key = np.arange(3 * BLOCK)[:, None]
    qry = np.arange(BLOCK)[None, :]
    valid = (key - qry >= 0) & (key - qry <= 2 * BLOCK)
    return jnp.asarray(np.where(valid, 0.0, NEG_INF), F32)


def kernel(x, c, ctx, c_ctx, w_ada, b_ada, norm1_g, w_in, w_conv, b_conv, w_a, w_b, sink, w_o,
           norm2_g, w_group, b_group, w_router, b_router, w_up, w_down, final_g):
    bsz, seq, _ = x.shape
    assert w_ada.shape[0] == 1 and seq % SEQ_TILE == 0

    cc = jnp.zeros((16, D_MODEL), F32).at[:bsz].set(c).at[bsz].set(c_ctx)
    mod = _modulation(cc, w_ada[0], b_ada[0][None, :])
    mod3 = mod[:bsz].reshape(bsz, 1, N_MOD * D_MODEL)
    csh1 = mod[bsz:bsz + 1, 0:D_MODEL]
    csc1 = mod[bsz:bsz + 1, D_MODEL:2 * D_MODEL]

    w = w_in[0]
    n1g = norm1_g[0][None, :]
    n2g = norm2_g[0][None, :]
    w_kv = w[:, OFF_K:OFF_GA].astype(BF16)
    w_a_cols = jnp.concatenate([w[:, OFF_CG:OFF_Q], w[:, OFF_K:OFF_GA]], axis=1).astype(BF16)
    w_b_cols = jnp.concatenate([w[:, OFF_B:OFF_CG], w[:, OFF_Q:OFF_K], w[:, OFF_GA:]],
                               axis=1).astype(BF16)
    cos_t, sin_t = _rope_tables(seq)

    k_ctx, vt_ctx = _context_kv(ctx, n1g, csh1, csc1, w_kv)
    uc, k_rot, v_t = _proj_a(x, n1g, mod3, w_a_cols, cos_t, sin_t)

    sink_row = jnp.repeat(sink[0].astype(F32), BLOCK)[None, :]
    wr_t = jnp.zeros((LANES, D_MODEL), F32)
    wr_t = wr_t.at[0:N_GROUPS].set(w_group[0].T)
    wr_t = wr_t.at[GROUP_ROWS:GROUP_ROWS + N_EXPERTS].set(w_router[0].T).astype(BF16)
    br_col = jnp.zeros((LANES, 1), F32)
    br_col = br_col.at[0:N_GROUPS, 0].set(b_group[0])
    br_col = br_col.at[GROUP_ROWS:GROUP_ROWS + N_EXPERTS, 0].set(b_router[0])

    x1, h2, route, yz = _mix(
        x, n1g, n2g, mod3, w_b_cols, cos_t, sin_t, uc, k_rot, v_t, k_ctx, vt_ctx,
        w_conv[0], b_conv[0][None, :], w_a[0].astype(BF16), w_b[0].astype(BF16),
        w_o[0].astype(BF16), sink_row, _window_bias(), wr_t, br_col)

    n = bsz * seq
    nch = n // CHUNK
    upper = jnp.asarray(np.triu(np.ones((CHUNK, CHUNK), np.float32), 1), BF16)
    lower = jnp.asarray(np.tril(np.ones((N_EXPERTS, N_EXPERTS), np.float32), -1), BF16)
    xl, pos_c, nb = _dispatch(h2.reshape(n, D_MODEL), route, upper, lower)
    nb = nb[:, :, 0]
    yl = _grouped_mlp(_tile_plan(nb, nch), xl, yz, w_up[0], w_down[0])
    used_rows = (ROW_BLOCK * jnp.sum(nb, axis=1)).astype(I32)
    out = _combine(used_rows, yl, pos_c, x1.reshape(n, D_MODEL), mod3, final_g[None, :], seq)
    return out.reshape(bsz, seq, D_MODEL)
```

```python
import numpy as np
import jax
import jax.numpy as jnp
from jax import lax
from jax.experimental import pallas as pl
from jax.experimental.pallas import tpu as pltpu

F32 = jnp.float32
BF16 = jnp.bfloat16
I32 = jnp.int32

D_MODEL = 1024
GRID_W = 64
CONV_W = 512
N_HEADS = 8
N_KV_HEADS = 2
HEAD_DIM = 64
ATT_W = N_HEADS * HEAD_DIM
KV_W = N_KV_HEADS * HEAD_DIM
BLOCK = 128
ROPE_BASE = 10000.0
N_GROUPS = 4
EXPERTS_PER_GROUP = 8
N_EXPERTS = N_GROUPS * EXPERTS_PER_GROUP
EXPERT_FF = 256
N_MOD = 6
NORM_EPS = 1e-6
NEG_INF = -1e30
LOG2E = 1.4426950408889634
ONES_ROWS = 16

OFF_B, OFF_CG, OFF_XIN, OFF_Q, OFF_K, OFF_V, OFF_GA, OFF_GB = (
    0, 512, 1024, 1536, 2048, 2176, 2304, 3328)
IN_COLS = 4352

LANES = 128
SEQ_TILE = 512
BLOCKS_PER_TILE = SEQ_TILE // BLOCK
ROUTE_ROWS = 8
GROUP_ROWS = 8
VMEM_LIMIT = 56 * 1024 * 1024

CHUNK = SEQ_TILE
ROW_BLOCK = 16
CHUNK_ROWS = -(-(2 * CHUNK + N_EXPERTS * (ROW_BLOCK - 1)) // 256) * 256
TRASH_BLOCKS = 4
CHUNK_STRIDE = CHUNK_ROWS + TRASH_BLOCKS * ROW_BLOCK
GATE_COLS = LANES
XL_COLS = D_MODEL + GATE_COLS
DISP_ROWS = 256
TILE_BLOCKS = 32
TILE_ROWS = TILE_BLOCKS * ROW_BLOCK
SUB_ROWS = 256
X_SLOTS = 4
Y_SLOTS = 4
W_SLOTS = 3


def _cparams(sem):
    return pltpu.CompilerParams(dimension_semantics=sem, vmem_limit_bytes=VMEM_LIMIT)


def _rms_mod(x, g, shift, scale):
    inv = lax.rsqrt(jnp.mean(x * x, axis=-1, keepdims=True) + NORM_EPS)
    return (x * inv) * g * (1.0 + scale) + shift


def _mod_kernel(c_ref, w_ref, b_ref, o_ref):
    c = c_ref[...]
    a = (c * jax.nn.sigmoid(c)).astype(BF16)
    o_ref[...] = jnp.dot(a, w_ref[...].astype(BF16), preferred_element_type=F32) + b_ref[...]


def _modulation(cc, w_ada, b_ada):
    rows = cc.shape[0]
    cols = w_ada.shape[1]
    tile = 1024
    return pl.pallas_call(
        _mod_kernel,
        grid=(cols // tile,),
        in_specs=[pl.BlockSpec((rows, D_MODEL), lambda j: (0, 0)),
                  pl.BlockSpec((D_MODEL, tile), lambda j: (0, j)),
                  pl.BlockSpec((1, tile), lambda j: (0, j))],
        out_specs=pl.BlockSpec((rows, tile), lambda j: (0, j)),
        out_shape=jax.ShapeDtypeStruct((rows, cols), F32),
        compiler_params=_cparams(("arbitrary",)),
        name="adaln_mod",
    )(cc, w_ada, b_ada)


def _ctx_kernel(x_ref, g_ref, sh_ref, sc_ref, w_ref, k_ref, vt_ref):
    h = _rms_mod(x_ref[0], g_ref[...], sh_ref[...], sc_ref[...]).astype(BF16)
    kv = jnp.dot(h, w_ref[...], preferred_element_type=F32)
    k_ref[0] = kv[:, :KV_W].astype(BF16)
    vt_ref[0] = kv[:, KV_W:].T.astype(BF16)


def _context_kv(ctx, norm_g, csh, csc, w_kv):
    bsz, clen, _ = ctx.shape
    return pl.pallas_call(
        _ctx_kernel,
        grid=(bsz,),
        in_specs=[pl.BlockSpec((1, clen, D_MODEL), lambda b: (b, 0, 0)),
                  pl.BlockSpec((1, D_MODEL), lambda b: (0, 0)),
                  pl.BlockSpec((1, D_MODEL), lambda b: (0, 0)),
                  pl.BlockSpec((1, D_MODEL), lambda b: (0, 0)),
                  pl.BlockSpec((D_MODEL, 2 * KV_W), lambda b: (0, 0))],
        out_specs=[pl.BlockSpec((1, clen, KV_W), lambda b: (b, 0, 0)),
                   pl.BlockSpec((1, KV_W, clen), lambda b: (b, 0, 0))],
        out_shape=[jax.ShapeDtypeStruct((bsz, clen, KV_W), BF16),
                   jax.ShapeDtypeStruct((bsz, KV_W, clen), BF16)],
        compiler_params=_cparams(("arbitrary",)),
        name="context_kv",
    )(ctx, norm_g, csh, csc, w_kv)


def _rope(t, cos, sin_signed):
    lane = lax.broadcasted_iota(I32, (1, LANES), 1)
    first_half = (lane % HEAD_DIM) < (HEAD_DIM // 2)
    outs = []
    for j in range(t.shape[1] // LANES):
        tj = t[:, j * LANES:(j + 1) * LANES]
        partner = jnp.where(first_half,
                            pltpu.roll(tj, LANES - HEAD_DIM // 2, axis=1),
                            pltpu.roll(tj, HEAD_DIM // 2, axis=1))
        outs.append(tj * cos + partner * sin_signed)
    return outs[0] if len(outs) == 1 else jnp.concatenate(outs, axis=1)


def _proj_a_kernel(x_ref, g_ref, sh_ref, sc_ref, w_ref, cos_ref, sin_ref, uc_ref, k_ref, vt_ref):
    h = _rms_mod(x_ref[0], g_ref[...], sh_ref[0], sc_ref[0]).astype(BF16)
    u = jnp.dot(h, w_ref[...], preferred_element_type=F32)
    uc_ref[0] = u[:, :CONV_W] * u[:, CONV_W:2 * CONV_W]
    k = u[:, 2 * CONV_W:2 * CONV_W + KV_W]
    k_ref[0] = _rope(k, cos_ref[...], sin_ref[...]).astype(BF16)
    vt_ref[0] = u[:, 2 * CONV_W + KV_W:].T.astype(BF16)


def _proj_a(x, norm_g, mod3, w_a_cols, cos_t, sin_t):
    bsz, seq, _ = x.shape
    nt = seq // SEQ_TILE
    wcols = w_a_cols.shape[1]
    return pl.pallas_call(
        _proj_a_kernel,
        grid=(nt, bsz),
        in_specs=[pl.BlockSpec((1, SEQ_TILE, D_MODEL), lambda t, b: (b, t, 0)),
                  pl.BlockSpec((1, D_MODEL), lambda t, b: (0, 0)),
                  pl.BlockSpec((1, 1, D_MODEL), lambda t, b: (b, 0, 0)),
                  pl.BlockSpec((1, 1, D_MODEL), lambda t, b: (b, 0, 1)),
                  pl.BlockSpec((D_MODEL, wcols), lambda t, b: (0, 0)),
                  pl.BlockSpec((SEQ_TILE, LANES), lambda t, b: (t, 0)),
                  pl.BlockSpec((SEQ_TILE, LANES), lambda t, b: (t, 0))],
        out_specs=[pl.BlockSpec((1, SEQ_TILE, CONV_W), lambda t, b: (b, t, 0)),
                   pl.BlockSpec((1, SEQ_TILE, KV_W), lambda t, b: (b, t, 0)),
                   pl.BlockSpec((1, KV_W, SEQ_TILE), lambda t, b: (b, 0, t))],
        out_shape=[jax.ShapeDtypeStruct((bsz, seq, CONV_W), F32),
                   jax.ShapeDtypeStruct((bsz, seq, KV_W), BF16),
                   jax.ShapeDtypeStruct((bsz, KV_W, seq), BF16)],
        compiler_params=_cparams(("arbitrary", "arbitrary")),
        name="proj_a",
    )(x, norm_g, mod3, mod3, w_a_cols, cos_t, sin_t)


def _sigmoid(x):
    return 1.0 / (1.0 + jnp.exp2(x * (-LOG2E)))


def _attention_block(q_blk, k_win, k_ctx, vt_win, vt_ctx, bias_prev, bias_next, sink_row):
    q_t = q_blk.T.astype(BF16)
    zeros = jnp.zeros((HEAD_DIM, 4 * BLOCK), BF16)
    bias_p4 = jnp.concatenate([bias_prev] * 4, axis=1)
    bias_n4 = jnp.concatenate([bias_next] * 4, axis=1)

    def ones_row(width):
        r = lax.broadcasted_iota(I32, (ONES_ROWS, width), 0)
        return jnp.where(r == 0, 1.0, 0.0).astype(BF16)

    ones_win = ones_row(3 * BLOCK)
    ones_ctx = ones_row(k_ctx.shape[0])
    o_rows = []
    for g in range(N_KV_HEADS):
        qg = jnp.concatenate([q_t[(4 * g + h) * HEAD_DIM:(4 * g + h + 1) * HEAD_DIM, :]
                              for h in range(4)], axis=1)
        rhs = jnp.concatenate([qg, zeros] if g == 0 else [zeros, qg], axis=0)
        s_win = jnp.dot(k_win, rhs, preferred_element_type=F32)
        s_prev = s_win[0:BLOCK] + bias_p4
        s_own = s_win[BLOCK:2 * BLOCK]
        s_next = s_win[2 * BLOCK:] + bias_n4
        s_ctx = jnp.dot(k_ctx, rhs, preferred_element_type=F32)
        sink = sink_row[:, g * 4 * BLOCK:(g + 1) * 4 * BLOCK]
        m = sink
        for s in (s_prev, s_own, s_next, s_ctx):
            m = jnp.maximum(m, jnp.max(s, axis=0, keepdims=True))
        p_win = jnp.concatenate([jnp.exp2(s - m).astype(BF16) for s in (s_prev, s_own, s_next)],
                                axis=0)
        p_ctx = jnp.exp2(s_ctx - m).astype(BF16)
        v_win = jnp.concatenate([vt_win[g * HEAD_DIM:(g + 1) * HEAD_DIM, :], ones_win], axis=0)
        v_ctx = jnp.concatenate([vt_ctx[g * HEAD_DIM:(g + 1) * HEAD_DIM, :], ones_ctx], axis=0)
        o_ext = (jnp.dot(v_win, p_win, preferred_element_type=F32)
                 + jnp.dot(v_ctx, p_ctx, preferred_element_type=F32))
        denom = o_ext[HEAD_DIM:HEAD_DIM + 1, :] + jnp.exp2(sink - m)
        o_t = o_ext[:HEAD_DIM, :] / denom
        for h in range(4):
            o_rows.append(o_t[:, h * BLOCK:(h + 1) * BLOCK])
    return jnp.concatenate(o_rows, axis=0).T


def _route(logits_t):
    t = logits_t.shape[1]
    grow = lax.broadcasted_iota(I32, (GROUP_ROWS, t), 0)
    gl = jnp.where(grow < N_GROUPS, logits_t[0:GROUP_ROWS, :], NEG_INF)
    gm = jnp.max(gl, axis=0, keepdims=True)
    p_g = 1.0 / jnp.sum(jnp.exp(gl - gm), axis=0, keepdims=True)
    g_idx = jnp.min(jnp.where(gl == gm, grow, N_GROUPS), axis=0, keepdims=True)

    erow = lax.broadcasted_iota(I32, (N_EXPERTS, t), 0)
    el = logits_t[GROUP_ROWS:GROUP_ROWS + N_EXPERTS, :]
    sel = (erow // EXPERTS_PER_GROUP) == g_idx
    em = jnp.where(sel, el, NEG_INF)
    m1 = jnp.max(em, axis=0, keepdims=True)
    i1 = jnp.min(jnp.where(em == m1, erow, N_EXPERTS), axis=0, keepdims=True)
    em2 = jnp.where(erow == i1, NEG_INF, em)
    m2 = jnp.max(em2, axis=0, keepdims=True)
    i2 = jnp.min(jnp.where(em2 == m2, erow, N_EXPERTS), axis=0, keepdims=True)
    z = jnp.sum(jnp.where(sel, jnp.exp(el - m1), 0.0), axis=0, keepdims=True)
    p1 = 1.0 / z
    p2 = jnp.exp(m2 - m1) / z
    gate1 = p_g * p1 / (p1 + p2)
    gate2 = p_g * p2 / (p1 + p2)
    pad = jnp.zeros((ROUTE_ROWS - 4, t), F32)
    return jnp.concatenate([i1.astype(F32), i2.astype(F32), gate1, gate2, pad], axis=0)


def _mix_kernel(x_ref, g1n_ref, g2n_ref, sh1_ref, sc1_ref, gt1_ref, sh2_ref, sc2_ref,
                wb_ref, cos_ref, sin_ref,
                uc_ref, ucp_ref, ucn_ref, k_ref, kp_ref, kn_ref, vt_ref, vtp_ref, vtn_ref,
                kc_ref, vtc_ref, wconv_ref, bconv_ref, wa_ref, wbb_ref, wo_ref,
                sink_ref, bias_ref, wr_ref, br_ref, wu_ref, wd_ref,
                x1_ref, h2_ref, route_ref, yz_ref, wub_ref, wdb_ref):
    t_idx = pl.program_id(0)
    n_tiles = pl.num_programs(0)
    is_first = t_idx == 0
    is_last = t_idx == n_tiles - 1

    x = x_ref[0]
    hb = _rms_mod(x, g1n_ref[...], sh1_ref[0], sc1_ref[0]).astype(BF16)

    bq = jnp.dot(hb, wb_ref[:, 0:2 * CONV_W], preferred_element_type=F32)
    uc = uc_ref[0]
    row = lax.broadcasted_iota(I32, (SEQ_TILE, 1), 0)
    prev_row = jnp.where(is_first, 0.0, ucp_ref[0][7:8, :])
    next_row = jnp.where(is_last, 0.0, ucn_ref[0][0:1, :])
    up = jnp.where(row == 0, prev_row, pltpu.roll(uc, 1, axis=0))
    dn = jnp.where(row == SEQ_TILE - 1, next_row, pltpu.roll(uc, SEQ_TILE - 1, axis=0))
    wconv = wconv_ref[...]
    y = bconv_ref[...] + (up * wconv[0:1, :] + uc * wconv[1:2, :] + dn * wconv[2:3, :])
    ya = jnp.dot((bq[:, :CONV_W] * y).astype(BF16), wa_ref[...], preferred_element_type=F32)

    q = _rope(bq[:, CONV_W:], cos_ref[...], sin_ref[...]) * (HEAD_DIM ** -0.5 * LOG2E)
    k_all = jnp.concatenate([kp_ref[0], k_ref[0], kn_ref[0]], axis=0)
    vt_all = jnp.concatenate([vtp_ref[0], vt_ref[0], vtn_ref[0]], axis=1)
    k_ctx = kc_ref[0]
    vt_ctx = vtc_ref[0]
    bias_prev = bias_ref[0:BLOCK, :]
    bias_next = bias_ref[2 * BLOCK:3 * BLOCK, :]
    sink_row = sink_ref[...] * LOG2E
    o_blocks = []
    for jb in range(BLOCKS_PER_TILE):
        bias_p, bias_n = bias_prev, bias_next
        if jb == 0:
            bias_p = bias_prev + jnp.where(is_first, NEG_INF, 0.0)
        if jb == BLOCKS_PER_TILE - 1:
            bias_n = bias_next + jnp.where(is_last, NEG_INF, 0.0)
        o_blocks.append(_attention_block(
            q[jb * BLOCK:(jb + 1) * BLOCK, :],
            k_all[jb * BLOCK:(jb + 3) * BLOCK, :], k_ctx,
            vt_all[:, jb * BLOCK:(jb + 3) * BLOCK], vt_ctx, bias_p, bias_n, sink_row))
    o = jnp.concatenate(o_blocks, axis=0).astype(BF16)
    yb = jnp.dot(o, wbb_ref[...], preferred_element_type=F32)

    ga = jnp.dot(hb, wb_ref[:, 2 * CONV_W:2 * CONV_W + D_MODEL], preferred_element_type=F32)
    gb = jnp.dot(hb, wb_ref[:, 2 * CONV_W + D_MODEL:], preferred_element_type=F32)
    merged = (_sigmoid(ga) * ya + _sigmoid(gb) * yb).astype(BF16)
    x1 = x + gt1_ref[0] * jnp.dot(merged, wo_ref[...], preferred_element_type=F32)
    x1_ref[0] = x1

    h2 = _rms_mod(x1, g2n_ref[...], sh2_ref[0], sc2_ref[0]).astype(BF16)
    h2_ref[0] = h2
    logits_t = lax.dot_general(wr_ref[...], h2, (((1,), (1,)), ((), ())),
                               preferred_element_type=F32) + br_ref[...]
    route_ref[0] = _route(logits_t)
    yz_ref[...] = jnp.zeros_like(yz_ref)
    wub_ref[...] = wu_ref[...].astype(BF16)
    wdb_ref[...] = wd_ref[...].astype(BF16)


def _mix(x, norm1_g, norm2_g, mod3, w_b_cols, cos_t, sin_t, uc, k_rot, v_t, k_ctx, vt_ctx,
         w_conv, b_conv, w_a, w_b, w_o, sink_row, bias, wr_t, br_col, w_up, w_down):
    bsz, seq, _ = x.shape
    nt = seq // SEQ_TILE
    nblk = seq // BLOCK
    clen = k_ctx.shape[1]
    rows8 = SEQ_TILE // 8
    assert N_EXPERTS % (nt * bsz) == 0, "each grid step converts an equal share of the experts"
    e_step = N_EXPERTS // (nt * bsz)

    def mod_spec(j):
        return pl.BlockSpec((1, 1, D_MODEL), lambda t, b, j=j: (b, 0, j))

    def const_spec(shape):
        return pl.BlockSpec(shape, lambda t, b: tuple(0 for _ in shape))

    in_specs = [
        pl.BlockSpec((1, SEQ_TILE, D_MODEL), lambda t, b: (b, t, 0)),
        const_spec((1, D_MODEL)), const_spec((1, D_MODEL)),
        mod_spec(0), mod_spec(1), mod_spec(2), mod_spec(3), mod_spec(4),
        const_spec(w_b_cols.shape),
        pl.BlockSpec((SEQ_TILE, LANES), lambda t, b: (t, 0)),
        pl.BlockSpec((SEQ_TILE, LANES), lambda t, b: (t, 0)),
        pl.BlockSpec((1, SEQ_TILE, CONV_W), lambda t, b: (b, t, 0)),
        pl.BlockSpec((1, 8, CONV_W), lambda t, b: (b, jnp.maximum(t * rows8 - 1, 0), 0)),
        pl.BlockSpec((1, 8, CONV_W),
                     lambda t, b: (b, jnp.minimum((t + 1) * rows8, seq // 8 - 1), 0)),
        pl.BlockSpec((1, SEQ_TILE, KV_W), lambda t, b: (b, t, 0)),
        pl.BlockSpec((1, BLOCK, KV_W),
                     lambda t, b: (b, jnp.maximum(t * BLOCKS_PER_TILE - 1, 0), 0)),
        pl.BlockSpec((1, BLOCK, KV_W),
                     lambda t, b: (b, jnp.minimum((t + 1) * BLOCKS_PER_TILE, nblk - 1), 0)),
        pl.BlockSpec((1, KV_W, SEQ_TILE), lambda t, b: (b, 0, t)),
        pl.BlockSpec((1, KV_W, BLOCK),
                     lambda t, b: (b, 0, jnp.maximum(t * BLOCKS_PER_TILE - 1, 0))),
        pl.BlockSpec((1, KV_W, BLOCK),
                     lambda t, b: (b, 0, jnp.minimum((t + 1) * BLOCKS_PER_TILE, nblk - 1))),
        pl.BlockSpec((1, clen, KV_W), lambda t, b: (b, 0, 0)),
        pl.BlockSpec((1, KV_W, clen), lambda t, b: (b, 0, 0)),
        const_spec(w_conv.shape), const_spec(b_conv.shape),
        const_spec(w_a.shape), const_spec(w_b.shape), const_spec(w_o.shape),
        const_spec(sink_row.shape), const_spec(bias.shape),
        const_spec(wr_t.shape), const_spec(br_col.shape),
        pl.BlockSpec((e_step,) + w_up.shape[1:], lambda t, b: (t * bsz + b, 0, 0)),
        pl.BlockSpec((e_step,) + w_down.shape[1:], lambda t, b: (t * bsz + b, 0, 0)),
    ]
    out_specs = [
        pl.BlockSpec((1, SEQ_TILE, D_MODEL), lambda t, b: (b, t, 0)),
        pl.BlockSpec((1, SEQ_TILE, D_MODEL), lambda t, b: (b, t, 0)),
        pl.BlockSpec((1, ROUTE_ROWS, SEQ_TILE), lambda t, b: (b, 0, t)),
        pl.BlockSpec((CHUNK_STRIDE, D_MODEL), lambda t, b: (b * nt + t, 0)),
        pl.BlockSpec((e_step,) + w_up.shape[1:], lambda t, b: (t * bsz + b, 0, 0)),
        pl.BlockSpec((e_step,) + w_down.shape[1:], lambda t, b: (t * bsz + b, 0, 0)),
    ]
    out_shape = [
        jax.ShapeDtypeStruct((bsz, seq, D_MODEL), F32),
        jax.ShapeDtypeStruct((bsz, seq, D_MODEL), BF16),
        jax.ShapeDtypeStruct((bsz, ROUTE_ROWS, seq), F32),
        jax.ShapeDtypeStruct((bsz * nt * CHUNK_STRIDE, D_MODEL), BF16),
        jax.ShapeDtypeStruct(w_up.shape, BF16),
        jax.ShapeDtypeStruct(w_down.shape, BF16),
    ]
    return pl.pallas_call(
        _mix_kernel,
        grid=(nt, bsz),
        in_specs=in_specs,
        out_specs=out_specs,
        out_shape=out_shape,
        compiler_params=_cparams(("arbitrary", "arbitrary")),
        name="token_mix",
    )(x, norm1_g, norm2_g, mod3, mod3, mod3, mod3, mod3, w_b_cols, cos_t, sin_t,
      uc, uc, uc, k_rot, k_rot, k_rot, v_t, v_t, v_t, k_ctx, vt_ctx,
      w_conv, b_conv, w_a, w_b, w_o, sink_row, bias, wr_t, br_col, w_up, w_down)


def _bf16_parts(v):
    hi = v.astype(BF16).astype(F32)
    r1 = v - hi
    mid = r1.astype(BF16).astype(F32)
    lo = (r1 - mid).astype(BF16).astype(F32)
    return hi, mid, lo


def _dispatch_kernel(h_ref, route_ref, upper_ref, lower_ref, xl_ref, pos_ref, nb_ref):
    route = route_ref[0]
    e1 = route[0:1, :].astype(I32)
    e2 = route[1:2, :].astype(I32)
    erow = lax.broadcasted_iota(I32, (N_EXPERTS, CHUNK), 0)
    hit1 = erow == e1
    hit2 = erow == e2
    onehot = jnp.where(hit1, 1.0, 0.0) + jnp.where(hit2, 1.0, 0.0)
    cum = jnp.dot(onehot.astype(BF16), upper_ref[...], preferred_element_type=F32)
    cnt = jnp.sum(onehot, axis=1, keepdims=True)
    nblk = jnp.floor((cnt + (ROW_BLOCK - 1)) * (1.0 / ROW_BLOCK))
    nblk_b = jnp.broadcast_to(nblk, (N_EXPERTS, LANES))
    seg = jnp.dot(lower_ref[...], nblk_b.astype(BF16), preferred_element_type=F32) * ROW_BLOCK
    base = seg[:, 0:1] + cum
    pos1 = jnp.sum(jnp.where(hit1, base, 0.0), axis=0, keepdims=True)
    pos2 = jnp.sum(jnp.where(hit2, base, 0.0), axis=0, keepdims=True)
    p1i = pos1.astype(I32)
    p2i = pos2.astype(I32)
    used_rows = (jnp.sum(nblk) * ROW_BLOCK).astype(I32)

    prow = lax.broadcasted_iota(I32, (LANES, CHUNK), 0)
    gate1 = route[2:3, :]
    gate2 = route[3:4, :]
    glane = lax.broadcasted_iota(I32, (1, GATE_COLS), 1)

    for rc in range(CHUNK_ROWS // DISP_ROWS):
        rows = pl.ds(rc * DISP_ROWS, DISP_ROWS)

        def live(rc=rc, rows=rows):
            r = lax.broadcasted_iota(I32, (DISP_ROWS, CHUNK), 0) + rc * DISP_ROWS
            m1 = r == p1i
            m2 = r == p2i
            sel = jnp.where(m1, 1.0, jnp.where(m2, 1.0, 0.0)).astype(BF16)
            xl_ref[rows, 0:D_MODEL] = jnp.dot(sel, h_ref[...],
                                              preferred_element_type=F32).astype(BF16)
            gate = jnp.sum(jnp.where(m1, gate1, jnp.where(m2, gate2, 0.0)), axis=1, keepdims=True)
            hi, mid, lo = _bf16_parts(gate)
            xl_ref[rows, D_MODEL:XL_COLS] = jnp.where(
                glane == 0, hi, jnp.where(glane == 1, mid, jnp.where(glane == 2, lo, 0.0))
            ).astype(BF16)

        def dead(rows=rows):
            xl_ref[rows, :] = jnp.zeros((DISP_ROWS, XL_COLS), BF16)

        if (rc + 1) * DISP_ROWS <= 2 * CHUNK:
            live()
        else:
            pl.when(rc * DISP_ROWS < used_rows)(live)
            pl.when(rc * DISP_ROWS >= used_rows)(dead)
    xl_ref[CHUNK_ROWS:CHUNK_STRIDE, :] = jnp.zeros((CHUNK_STRIDE - CHUNK_ROWS, XL_COLS), BF16)

    pos_rows = jnp.where(prow == 0, pos1, jnp.where(prow == 1, pos2, 0.0))
    pos_ref[...] = pos_rows.T
    nb_ref[0] = nblk_b.astype(I32)


def _dispatch(h2, route, upper, lower):
    n = h2.shape[0]
    nch = n // CHUNK
    per_seq = route.shape[2] // CHUNK
    return pl.pallas_call(
        _dispatch_kernel,
        grid=(nch,),
        in_specs=[pl.BlockSpec((CHUNK, D_MODEL), lambda c: (c, 0)),
                  pl.BlockSpec((1, ROUTE_ROWS, CHUNK), lambda c: (c // per_seq, 0, c % per_seq)),
                  pl.BlockSpec(upper.shape, lambda c: (0, 0)),
                  pl.BlockSpec(lower.shape, lambda c: (0, 0))],
        out_specs=[pl.BlockSpec((CHUNK_STRIDE, XL_COLS), lambda c: (c, 0)),
                   pl.BlockSpec((CHUNK, LANES), lambda c: (c, 0)),
                   pl.BlockSpec((1, N_EXPERTS, LANES), lambda c: (c, 0, 0))],
        out_shape=[jax.ShapeDtypeStruct((nch * CHUNK_STRIDE, XL_COLS), BF16),
                   jax.ShapeDtypeStruct((n, LANES), F32),
                   jax.ShapeDtypeStruct((nch, N_EXPERTS, LANES), I32)],
        compiler_params=_cparams(("arbitrary",)),
        name="moe_dispatch",
    )(h2, route, upper, lower)


def _max_tiles(nch):
    max_blocks = nch * (2 * CHUNK + N_EXPERTS * (ROW_BLOCK - 1)) // ROW_BLOCK
    return max_blocks // TILE_BLOCKS + N_EXPERTS


def _masked_prefix(le, values):
    delta = values - jnp.concatenate([jnp.zeros((1,), values.dtype), values[:-1]])
    return jnp.sum(jnp.where(le, delta[None, :], 0), axis=1)


def _tile_plan(nb, nch):
    n_tiles = _max_tiles(nch)
    nbt = nb.T
    nbe = jnp.sum(nbt, axis=1)
    nte = (nbe + TILE_BLOCKS - 1) // TILE_BLOCKS
    tile_end = jnp.cumsum(nte)
    tile_start = tile_end - nte
    n_act = tile_end[-1]
    tiles = jnp.arange(n_tiles, dtype=I32)
    te = jnp.sum((tile_end[None, :] <= tiles[:, None]).astype(I32), axis=1)
    e_ar = jnp.arange(N_EXPERTS, dtype=I32)
    active = nte > 0
    te = jnp.where(tiles < n_act, te, jnp.max(jnp.where(active, e_ar, 0)))
    first = jnp.logical_and(te != jnp.concatenate([jnp.full((1,), -1, I32), te[:-1]]),
                            tiles < n_act).astype(I32)
    rank = jnp.cumsum(active.astype(I32)) - 1
    k_ar = jnp.arange(N_EXPERTS + W_SLOTS, dtype=I32)
    eseq = jnp.sum(jnp.where(jnp.logical_and(active[None, :], rank[None, :] == k_ar[:, None]),
                             e_ar[None, :], 0), axis=1)
    n_exp = jnp.sum(active.astype(I32))

    cb_excl = jnp.cumsum(nbt, axis=1) - nbt
    gs = TILE_BLOCKS * tile_start[:, None] + cb_excl
    seg_off = (ROW_BLOCK * (jnp.cumsum(nb, axis=1) - nb)).T
    base_row = jnp.arange(nch, dtype=I32)[None, :] * CHUNK_STRIDE + seg_off
    gs_f = gs.reshape(-1)
    slots = jnp.arange(n_tiles * TILE_BLOCKS, dtype=I32)
    le = gs_f[None, :] <= slots[:, None]
    row = ROW_BLOCK * slots + _masked_prefix(le, (base_row - ROW_BLOCK * gs).reshape(-1))
    valid = slots < _masked_prefix(le, (gs + nbt).reshape(-1))
    t = slots % (nch * TRASH_BLOCKS)
    trash = (t % nch) * CHUNK_STRIDE + CHUNK_ROWS + (t // nch) * ROW_BLOCK
    grow = (jnp.where(valid, row, 0) // ROW_BLOCK).astype(I32)
    srow = (jnp.where(valid, row, trash) // ROW_BLOCK).astype(I32)
    return (n_act.reshape(1).astype(I32), n_exp.reshape(1), te.astype(I32), first, eseq,
            grow, srow)


def _gmm_kernel(nact_ref, nexp_ref, te_ref, first_ref, eseq_ref, grow_ref, srow_ref,
                xl_hbm, yz_hbm, wu_hbm, wd_hbm, yl_hbm,
                xbuf, ybuf, wu_st, wd_st, in_sem, out_sem, w_sem):
    del te_ref, yz_hbm
    n_act = nact_ref[0]
    n_exp = nexp_ref[0]
    prefetch = X_SLOTS - 1

    def gather(tile, b):
        s = tile % X_SLOTS
        blk = grow_ref[tile * TILE_BLOCKS + b]
        return pltpu.make_async_copy(xl_hbm.at[blk], xbuf.at[s, b], in_sem.at[s])

    def scatter(tile, b):
        s = tile % Y_SLOTS
        blk = srow_ref[tile * TILE_BLOCKS + b]
        return pltpu.make_async_copy(ybuf.at[s, b], yl_hbm.at[blk], out_sem.at[s])

    def weights(q):
        e = eseq_ref[q]
        s = q % W_SLOTS
        return (pltpu.make_async_copy(wu_hbm.at[e], wu_st.at[s], w_sem.at[0, s]),
                pltpu.make_async_copy(wd_hbm.at[e], wd_st.at[s], w_sem.at[1, s]))

    def all_blocks(fn):
        for b in range(TILE_BLOCKS):
            fn(b)

    for q0 in range(W_SLOTS - 1):
        @pl.when(q0 < n_exp)
        def _(q0=q0):
            for cp in weights(q0):
                cp.start()
    for t0 in range(prefetch):
        @pl.when(t0 < n_act)
        def _(t0=t0):
            all_blocks(lambda b: gather(t0, b).start())

    def body(i, q):
        @pl.when(i + prefetch < n_act)
        def _():
            all_blocks(lambda b: gather(i + prefetch, b).start())

        all_blocks(lambda b: gather(i, b).wait())

        @pl.when(i >= Y_SLOTS)
        def _():
            all_blocks(lambda b: scatter(i - Y_SLOTS, b).wait())

        is_first = first_ref[i] == 1

        @pl.when(is_first)
        def _():
            for cp in weights(q):
                cp.wait()

            @pl.when(q + W_SLOTS - 1 < n_exp)
            def _():
                for cp in weights(q + W_SLOTS - 1):
                    cp.start()

        q = q + is_first.astype(I32)
        ws = (q - 1) % W_SLOTS
        xs = i % X_SLOTS
        ys = i % Y_SLOTS
        sub_blocks = SUB_ROWS // ROW_BLOCK
        for h in range(TILE_ROWS // SUB_ROWS):
            blocks = pl.ds(h * sub_blocks, sub_blocks)
            x = xbuf[xs, blocks].reshape(SUB_ROWS, XL_COLS)
            gate = jnp.sum(x[:, D_MODEL:].astype(F32), axis=1, keepdims=True)
            au = jnp.dot(x[:, :D_MODEL], wu_st[ws], preferred_element_type=F32)
            a = au[:, :EXPERT_FF]
            act = (a * _sigmoid(a)) * au[:, EXPERT_FF:]
            y = jnp.dot(act.astype(BF16), wd_st[ws], preferred_element_type=F32)
            ybuf[ys, blocks] = (gate * y).astype(BF16).reshape(sub_blocks, ROW_BLOCK, D_MODEL)

        all_blocks(lambda b: scatter(i, b).start())
        return q

    lax.fori_loop(0, n_act, body, jnp.int32(0))

    for k in range(Y_SLOTS):
        @pl.when(n_act - 1 - k >= 0)
        def _(k=k):
            all_blocks(lambda b: scatter(n_act - 1 - k, b).wait())


def _grouped_mlp(plan, xl, yz, w_up, w_down):
    grid_spec = pltpu.PrefetchScalarGridSpec(
        num_scalar_prefetch=len(plan),
        grid=(1,),
        in_specs=[pl.BlockSpec(memory_space=pl.ANY)] * 4,
        out_specs=pl.BlockSpec(memory_space=pl.ANY),
        scratch_shapes=[pltpu.VMEM((X_SLOTS, TILE_BLOCKS, ROW_BLOCK, XL_COLS), BF16),
                        pltpu.VMEM((Y_SLOTS, TILE_BLOCKS, ROW_BLOCK, D_MODEL), BF16),
                        pltpu.VMEM((W_SLOTS, D_MODEL, 2 * EXPERT_FF), BF16),
                        pltpu.VMEM((W_SLOTS, EXPERT_FF, D_MODEL), BF16),
                        pltpu.SemaphoreType.DMA((X_SLOTS,)),
                        pltpu.SemaphoreType.DMA((Y_SLOTS,)),
                        pltpu.SemaphoreType.DMA((2, W_SLOTS))])
    return pl.pallas_call(
        _gmm_kernel,
        grid_spec=grid_spec,
        out_shape=jax.ShapeDtypeStruct(yz.shape, yz.dtype),
        input_output_aliases={len(plan) + 1: 0},
        compiler_params=_cparams(("arbitrary",)),
        name="moe_experts",
    )(*plan, xl, yz, w_up, w_down)


def _combine_kernel(used_ref, yl_ref, pos_ref, x1_ref, gt2_ref, fg_ref, o_ref, acc_ref):
    used_rows = used_ref[pl.program_id(0)]
    pos = pos_ref[...]
    p1 = pos[:, 0:1].astype(I32)
    p2 = pos[:, 1:2].astype(I32)

    def partial_sum(k0, k1):
        r = lax.broadcasted_iota(I32, (CHUNK, k1 - k0), 1) + k0
        sel = jnp.where(r == p1, 1.0, jnp.where(r == p2, 1.0, 0.0)).astype(BF16)
        return jnp.dot(sel, yl_ref[k0:k1, :], preferred_element_type=F32)

    acc_ref[...] = partial_sum(0, 2 * CHUNK)
    for k0 in range(2 * CHUNK, CHUNK_ROWS, DISP_ROWS):
        @pl.when(k0 < used_rows)
        def _(k0=k0):
            acc_ref[...] += partial_sum(k0, k0 + DISP_ROWS)

    x2 = x1_ref[...] + gt2_ref[0] * acc_ref[...]
    inv = lax.rsqrt(jnp.mean(x2 * x2, axis=-1, keepdims=True) + NORM_EPS)
    o_ref[...] = (x2 * inv) * fg_ref[...]


def _combine(used_rows, yl, pos_c, x1, mod3, final_g, seq):
    n = x1.shape[0]
    per_seq = seq // CHUNK
    grid_spec = pltpu.PrefetchScalarGridSpec(
        num_scalar_prefetch=1,
        grid=(n // CHUNK,),
        in_specs=[pl.BlockSpec((CHUNK_STRIDE, D_MODEL), lambda c, u: (c, 0)),
                  pl.BlockSpec((CHUNK, LANES), lambda c, u: (c, 0)),
                  pl.BlockSpec((CHUNK, D_MODEL), lambda c, u: (c, 0)),
                  pl.BlockSpec((1, 1, D_MODEL), lambda c, u: (c // per_seq, 0, 5)),
                  pl.BlockSpec((1, D_MODEL), lambda c, u: (0, 0))],
        out_specs=pl.BlockSpec((CHUNK, D_MODEL), lambda c, u: (c, 0)),
        scratch_shapes=[pltpu.VMEM((CHUNK, D_MODEL), F32)])
    return pl.pallas_call(
        _combine_kernel,
        grid_spec=grid_spec,
        out_shape=jax.ShapeDtypeStruct((n, D_MODEL), F32),
        compiler_params=_cparams(("arbitrary",)),
        name="moe_combine",
    )(used_rows, yl, pos_c, x1, mod3, final_g)


def _rope_tables(seq):
    n_freq = HEAD_DIM // 4
    inv_freq = ROPE_BASE ** (-jnp.arange(n_freq, dtype=F32) / n_freq)
    rows = seq // GRID_W
    row = jnp.repeat(jnp.arange(rows, dtype=F32), GRID_W)
    col = jnp.tile(jnp.arange(GRID_W, dtype=F32), rows)
    ang = jnp.concatenate([row[:, None] * inv_freq, col[:, None] * inv_freq], axis=-1)
    cos, sin = jnp.cos(ang), jnp.sin(ang)
    return jnp.tile(cos, (1, 4)), jnp.concatenate([-sin, sin, -sin, sin], axis=1)


def _window_bias():
    key = np.arange(3 * BLOCK)[:, None]
    qry = np.arange(BLOCK)[None, :]
    valid = (key - qry >= 0) & (key - qry <= 2 * BLOCK)
    return jnp.asarray(np.where(valid, 0.0, NEG_INF), F32)


def kernel(x, c, ctx, c_ctx, w_ada, b_ada, norm1_g, w_in, w_conv, b_conv, w_a, w_b, sink, w_o,
           norm2_g, w_group, b_group, w_router, b_router, w_up, w_down, final_g):
    bsz, seq, _ = x.shape
    assert w_ada.shape[0] == 1 and seq % SEQ_TILE == 0

    cc = jnp.zeros((16, D_MODEL), F32).at[:bsz].set(c).at[bsz].set(c_ctx)
    mod = _modulation(cc, w_ada[0], b_ada[0][None, :])
    mod3 = mod[:bsz].reshape(bsz, 1, N_MOD * D_MODEL)
    csh1 = mod[bsz:bsz + 1, 0:D_MODEL]
    csc1 = mod[bsz:bsz + 1, D_MODEL:2 * D_MODEL]

    w = w_in[0]
    n1g = norm1_g[0][None, :]
    n2g = norm2_g[0][None, :]
    w_kv = w[:, OFF_K:OFF_GA].astype(BF16)
    w_a_cols = jnp.concatenate([w[:, OFF_CG:OFF_Q], w[:, OFF_K:OFF_GA]], axis=1).astype(BF16)
    w_b_cols = jnp.concatenate([w[:, OFF_B:OFF_CG], w[:, OFF_Q:OFF_K], w[:, OFF_GA:]],
                               axis=1).astype(BF16)
    cos_t, sin_t = _rope_tables(seq)

    k_ctx, vt_ctx = _context_kv(ctx, n1g, csh1, csc1, w_kv)
    uc, k_rot, v_t = _proj_a(x, n1g, mod3, w_a_cols, cos_t, sin_t)

    sink_row = jnp.repeat(sink[0].astype(F32), BLOCK)[None, :]
    wr_t = jnp.zeros((LANES, D_MODEL), F32)
    wr_t = wr_t.at[0:N_GROUPS].set(w_group[0].T)
    wr_t = wr_t.at[GROUP_ROWS:GROUP_ROWS + N_EXPERTS].set(w_router[0].T).astype(BF16)
    br_col = jnp.zeros((LANES, 1), F32)
    br_col = br_col.at[0:N_GROUPS, 0].set(b_group[0])
    br_col = br_col.at[GROUP_ROWS:GROUP_ROWS + N_EXPERTS, 0].set(b_router[0])

    x1, h2, route, yz, w_up_bf, w_down_bf = _mix(
        x, n1g, n2g, mod3, w_b_cols, cos_t, sin_t, uc, k_rot, v_t, k_ctx, vt_ctx,
        w_conv[0], b_conv[0][None, :], w_a[0].astype(BF16), w_b[0].astype(BF16),
        w_o[0].astype(BF16), sink_row, _window_bias(), wr_t, br_col, w_up[0], w_down[0])

    n = bsz * seq
    nch = n // CHUNK
    upper = jnp.asarray(np.triu(np.ones((CHUNK, CHUNK), np.float32), 1), BF16)
    lower = jnp.asarray(np.tril(np.ones((N_EXPERTS, N_EXPERTS), np.float32), -1), BF16)
    xl, pos_c, nb = _dispatch(h2.reshape(n, D_MODEL), route, upper, lower)
    nb = nb[:, :, 0]
    yl = _grouped_mlp(_tile_plan(nb, nch), xl.reshape(-1, ROW_BLOCK, XL_COLS),
                      yz.reshape(-1, ROW_BLOCK, D_MODEL), w_up_bf, w_down_bf)
    yl = yl.reshape(-1, D_MODEL)
    used_rows = (ROW_BLOCK * jnp.sum(nb, axis=1)).astype(I32)
    out = _combine(used_rows, yl, pos_c, x1.reshape(n, D_MODEL), mod3, final_g[None, :], seq)
    return out.reshape(bsz, seq, D_MODEL)
```

```python
import numpy as np
import jax
import jax.numpy as jnp
from jax import lax
from jax.experimental import pallas as pl
from jax.experimental.pallas import tpu as pltpu

F32 = jnp.float32
BF16 = jnp.bfloat16
I32 = jnp.int32

D_MODEL = 1024
GRID_W = 64
CONV_W = 512
N_HEADS = 8
N_KV_HEADS = 2
HEAD_DIM = 64
ATT_W = N_HEADS * HEAD_DIM
KV_W = N_KV_HEADS * HEAD_DIM
BLOCK = 128
ROPE_BASE = 10000.0
N_GROUPS = 4
EXPERTS_PER_GROUP = 8
N_EXPERTS = N_GROUPS * EXPERTS_PER_GROUP
EXPERT_FF = 256
N_MOD = 6
NORM_EPS = 1e-6
NEG_INF = -1e30
LOG2E = 1.4426950408889634
ONES_ROWS = 16

OFF_B, OFF_CG, OFF_XIN, OFF_Q, OFF_K, OFF_V, OFF_GA, OFF_GB = (
    0, 512, 1024, 1536, 2048, 2176, 2304, 3328)
IN_COLS = 4352

LANES = 128
SEQ_TILE = 512
BLOCKS_PER_TILE = SEQ_TILE // BLOCK
ROUTE_ROWS = 8
GROUP_ROWS = 8
VMEM_LIMIT = 56 * 1024 * 1024

CHUNK = SEQ_TILE
ROW_BLOCK = 16
CHUNK_ROWS = -(-(2 * CHUNK + N_EXPERTS * (ROW_BLOCK - 1)) // 256) * 256
TRASH_BLOCKS = 4
CHUNK_STRIDE = CHUNK_ROWS + TRASH_BLOCKS * ROW_BLOCK
GATE_COLS = LANES
XL_COLS = D_MODEL + GATE_COLS
DISP_ROWS = 256
TILE_BLOCKS = 32
TILE_ROWS = TILE_BLOCKS * ROW_BLOCK
SUB_ROWS = 256
X_SLOTS = 4
Y_SLOTS = 4
W_SLOTS = 3


def _cparams(sem):
    return pltpu.CompilerParams(dimension_semantics=sem, vmem_limit_bytes=VMEM_LIMIT)


def _rms_mod(x, g, shift, scale):
    inv = lax.rsqrt(jnp.mean(x * x, axis=-1, keepdims=True) + NORM_EPS)
    return (x * inv) * g * (1.0 + scale) + shift


def _mod_kernel(c_ref, w_ref, b_ref, o_ref):
    c = c_ref[...]
    a = (c * jax.nn.sigmoid(c)).astype(BF16)
    o_ref[...] = jnp.dot(a, w_ref[...].astype(BF16), preferred_element_type=F32) + b_ref[...]


def _modulation(cc, w_ada, b_ada):
    rows = cc.shape[0]
    cols = w_ada.shape[1]
    tile = 1024
    return pl.pallas_call(
        _mod_kernel,
        grid=(cols // tile,),
        in_specs=[pl.BlockSpec((rows, D_MODEL), lambda j: (0, 0)),
                  pl.BlockSpec((D_MODEL, tile), lambda j: (0, j)),
                  pl.BlockSpec((1, tile), lambda j: (0, j))],
        out_specs=pl.BlockSpec((rows, tile), lambda j: (0, j)),
        out_shape=jax.ShapeDtypeStruct((rows, cols), F32),
        compiler_params=_cparams(("arbitrary",)),
        name="adaln_mod",
    )(cc, w_ada, b_ada)


def _ctx_kernel(x_ref, g_ref, sh_ref, sc_ref, w_ref, k_ref, vt_ref):
    h = _rms_mod(x_ref[0], g_ref[...], sh_ref[...], sc_ref[...]).astype(BF16)
    kv = jnp.dot(h, w_ref[...], preferred_element_type=F32)
    k_ref[0] = kv[:, :KV_W].astype(BF16)
    vt_ref[0] = kv[:, KV_W:].T.astype(BF16)


def _context_kv(ctx, norm_g, csh, csc, w_kv):
    bsz, clen, _ = ctx.shape
    return pl.pallas_call(
        _ctx_kernel,
        grid=(bsz,),
        in_specs=[pl.BlockSpec((1, clen, D_MODEL), lambda b: (b, 0, 0)),
                  pl.BlockSpec((1, D_MODEL), lambda b: (0, 0)),
                  pl.BlockSpec((1, D_MODEL), lambda b: (0, 0)),
                  pl.BlockSpec((1, D_MODEL), lambda b: (0, 0)),
                  pl.BlockSpec((D_MODEL, 2 * KV_W), lambda b: (0, 0))],
        out_specs=[pl.BlockSpec((1, clen, KV_W), lambda b: (b, 0, 0)),
                   pl.BlockSpec((1, KV_W, clen), lambda b: (b, 0, 0))],
        out_shape=[jax.ShapeDtypeStruct((bsz, clen, KV_W), BF16),
                   jax.ShapeDtypeStruct((bsz, KV_W, clen), BF16)],
        compiler_params=_cparams(("arbitrary",)),
        name="context_kv",
    )(ctx, norm_g, csh, csc, w_kv)


def _rope(t, cos, sin_signed):
    lane = lax.broadcasted_iota(I32, (1, LANES), 1)
    first_half = (lane % HEAD_DIM) < (HEAD_DIM // 2)
    outs = []
    for j in range(t.shape[1] // LANES):
        tj = t[:, j * LANES:(j + 1) * LANES]
        partner = jnp.where(first_half,
                            pltpu.roll(tj, LANES - HEAD_DIM // 2, axis=1),
                            pltpu.roll(tj, HEAD_DIM // 2, axis=1))
        outs.append(tj * cos + partner * sin_signed)
    return outs[0] if len(outs) == 1 else jnp.concatenate(outs, axis=1)


def _proj_a_kernel(x_ref, g_ref, sh_ref, sc_ref, w_ref, cos_ref, sin_ref, uc_ref, k_ref, vt_ref):
    h = _rms_mod(x_ref[0], g_ref[...], sh_ref[0], sc_ref[0]).astype(BF16)
    u = jnp.dot(h, w_ref[...], preferred_element_type=F32)
    uc_ref[0] = u[:, :CONV_W] * u[:, CONV_W:2 * CONV_W]
    k = u[:, 2 * CONV_W:2 * CONV_W + KV_W]
    k_ref[0] = _rope(k, cos_ref[...], sin_ref[...]).astype(BF16)
    vt_ref[0] = u[:, 2 * CONV_W + KV_W:].T.astype(BF16)


def _proj_a(x, norm_g, mod3, w_a_cols, cos_t, sin_t):
    bsz, seq, _ = x.shape
    nt = seq // SEQ_TILE
    wcols = w_a_cols.shape[1]
    return pl.pallas_call(
        _proj_a_kernel,
        grid=(nt, bsz),
        in_specs=[pl.BlockSpec((1, SEQ_TILE, D_MODEL), lambda t, b: (b, t, 0)),
                  pl.BlockSpec((1, D_MODEL), lambda t, b: (0, 0)),
                  pl.BlockSpec((1, 1, D_MODEL), lambda t, b: (b, 0, 0)),
                  pl.BlockSpec((1, 1, D_MODEL), lambda t, b: (b, 0, 1)),
                  pl.BlockSpec((D_MODEL, wcols), lambda t, b: (0, 0)),
                  pl.BlockSpec((SEQ_TILE, LANES), lambda t, b: (t, 0)),
                  pl.BlockSpec((SEQ_TILE, LANES), lambda t, b: (t, 0))],
        out_specs=[pl.BlockSpec((1, SEQ_TILE, CONV_W), lambda t, b: (b, t, 0)),
                   pl.BlockSpec((1, SEQ_TILE, KV_W), lambda t, b: (b, t, 0)),
                   pl.BlockSpec((1, KV_W, SEQ_TILE), lambda t, b: (b, 0, t))],
        out_shape=[jax.ShapeDtypeStruct((bsz, seq, CONV_W), F32),
                   jax.ShapeDtypeStruct((bsz, seq, KV_W), BF16),
                   jax.ShapeDtypeStruct((bsz, KV_W, seq), BF16)],
        compiler_params=_cparams(("arbitrary", "arbitrary")),
        name="proj_a",
    )(x, norm_g, mod3, mod3, w_a_cols, cos_t, sin_t)


def _sigmoid(x):
    return 1.0 / (1.0 + jnp.exp2(x * (-LOG2E)))


def _attn_scores(q_t, g, k_win, k_ctx, bias_p4, bias_n4):
    zeros = jnp.zeros((HEAD_DIM, 4 * BLOCK), BF16)
    qg = jnp.concatenate([q_t[(4 * g + h) * HEAD_DIM:(4 * g + h + 1) * HEAD_DIM, :]
                          for h in range(4)], axis=1)
    rhs = jnp.concatenate([qg, zeros] if g == 0 else [zeros, qg], axis=0)
    s_win = jnp.dot(k_win, rhs, preferred_element_type=F32)
    s_ctx = jnp.dot(k_ctx, rhs, preferred_element_type=F32)
    return (s_win[0:BLOCK] + bias_p4, s_win[BLOCK:2 * BLOCK], s_win[2 * BLOCK:] + bias_n4, s_ctx)


def _attn_probs(scores, sink):
    m = sink
    for s in scores:
        m = jnp.maximum(m, jnp.max(s, axis=0, keepdims=True))
    p_win = jnp.concatenate([jnp.exp2(s - m).astype(BF16) for s in scores[:3]], axis=0)
    p_ctx = jnp.exp2(scores[3] - m).astype(BF16)
    return p_win, p_ctx, m


def _attn_values(probs, sink, vt_win_g, vt_ctx_g):
    p_win, p_ctx, m = probs

    def with_ones(vt):
        r = lax.broadcasted_iota(I32, (ONES_ROWS, vt.shape[1]), 0)
        return jnp.concatenate([vt, jnp.where(r == 0, 1.0, 0.0).astype(BF16)], axis=0)

    o_ext = (jnp.dot(with_ones(vt_win_g), p_win, preferred_element_type=F32)
             + jnp.dot(with_ones(vt_ctx_g), p_ctx, preferred_element_type=F32))
    denom = o_ext[HEAD_DIM:HEAD_DIM + 1, :] + jnp.exp2(sink - m)
    return o_ext[:HEAD_DIM, :] / denom


def _route(logits_t):
    t = logits_t.shape[1]
    grow = lax.broadcasted_iota(I32, (GROUP_ROWS, t), 0)
    gl = jnp.where(grow < N_GROUPS, logits_t[0:GROUP_ROWS, :], NEG_INF)
    gm = jnp.max(gl, axis=0, keepdims=True)
    p_g = 1.0 / jnp.sum(jnp.exp(gl - gm), axis=0, keepdims=True)
    g_idx = jnp.min(jnp.where(gl == gm, grow, N_GROUPS), axis=0, keepdims=True)

    erow = lax.broadcasted_iota(I32, (N_EXPERTS, t), 0)
    el = logits_t[GROUP_ROWS:GROUP_ROWS + N_EXPERTS, :]
    sel = (erow // EXPERTS_PER_GROUP) == g_idx
    em = jnp.where(sel, el, NEG_INF)
    m1 = jnp.max(em, axis=0, keepdims=True)
    i1 = jnp.min(jnp.where(em == m1, erow, N_EXPERTS), axis=0, keepdims=True)
    em2 = jnp.where(erow == i1, NEG_INF, em)
    m2 = jnp.max(em2, axis=0, keepdims=True)
    i2 = jnp.min(jnp.where(em2 == m2, erow, N_EXPERTS), axis=0, keepdims=True)
    z = jnp.sum(jnp.where(sel, jnp.exp(el - m1), 0.0), axis=0, keepdims=True)
    p1 = 1.0 / z
    p2 = jnp.exp(m2 - m1) / z
    gate1 = p_g * p1 / (p1 + p2)
    gate2 = p_g * p2 / (p1 + p2)
    pad = jnp.zeros((ROUTE_ROWS - 4, t), F32)
    return jnp.concatenate([i1.astype(F32), i2.astype(F32), gate1, gate2, pad], axis=0)


def _mix_kernel(x_ref, g1n_ref, g2n_ref, sh1_ref, sc1_ref, gt1_ref, sh2_ref, sc2_ref,
                wb_ref, cos_ref, sin_ref,
                uc_ref, ucp_ref, ucn_ref, k_ref, kp_ref, kn_ref, vt_ref, vtp_ref, vtn_ref,
                kc_ref, vtc_ref, wconv_ref, bconv_ref, wa_ref, wbb_ref, wo_ref,
                sink_ref, bias_ref, wr_ref, br_ref, wu_ref, wd_ref,
                x1_ref, h2_ref, route_ref, yz_ref, wub_ref, wdb_ref):
    t_idx = pl.program_id(0)
    n_tiles = pl.num_programs(0)
    is_first = t_idx == 0
    is_last = t_idx == n_tiles - 1

    x = x_ref[0]
    hb = _rms_mod(x, g1n_ref[...], sh1_ref[0], sc1_ref[0]).astype(BF16)

    bq = jnp.dot(hb, wb_ref[:, 0:2 * CONV_W], preferred_element_type=F32)
    uc = uc_ref[0]
    row = lax.broadcasted_iota(I32, (SEQ_TILE, 1), 0)
    prev_row = jnp.where(is_first, 0.0, ucp_ref[0][7:8, :])
    next_row = jnp.where(is_last, 0.0, ucn_ref[0][0:1, :])
    up = jnp.where(row == 0, prev_row, pltpu.roll(uc, 1, axis=0))
    dn = jnp.where(row == SEQ_TILE - 1, next_row, pltpu.roll(uc, SEQ_TILE - 1, axis=0))
    wconv = wconv_ref[...]
    y = bconv_ref[...] + (up * wconv[0:1, :] + uc * wconv[1:2, :] + dn * wconv[2:3, :])
    ya = jnp.dot((bq[:, :CONV_W] * y).astype(BF16), wa_ref[...], preferred_element_type=F32)

    q = _rope(bq[:, CONV_W:], cos_ref[...], sin_ref[...]) * (HEAD_DIM ** -0.5 * LOG2E)
    k_all = jnp.concatenate([kp_ref[0], k_ref[0], kn_ref[0]], axis=0)
    vt_all = jnp.concatenate([vtp_ref[0], vt_ref[0], vtn_ref[0]], axis=1)
    k_ctx = kc_ref[0]
    vt_ctx = vtc_ref[0]
    bias_prev = bias_ref[0:BLOCK, :]
    bias_next = bias_ref[2 * BLOCK:3 * BLOCK, :]
    sink_row = sink_ref[...] * LOG2E
    n_units = BLOCKS_PER_TILE * N_KV_HEADS
    gate_cols = 2 * D_MODEL // n_units
    gate_chunks = []

    def gate_chunk(u):
        c0 = 2 * CONV_W + u * gate_cols
        gate_chunks.append(jnp.dot(hb, wb_ref[:, c0:c0 + gate_cols], preferred_element_type=F32))

    q_ts, biases = [], []
    for jb in range(BLOCKS_PER_TILE):
        bias_p, bias_n = bias_prev, bias_next
        if jb == 0:
            bias_p = bias_prev + jnp.where(is_first, NEG_INF, 0.0)
        if jb == BLOCKS_PER_TILE - 1:
            bias_n = bias_next + jnp.where(is_last, NEG_INF, 0.0)
        biases.append((jnp.concatenate([bias_p] * 4, axis=1), jnp.concatenate([bias_n] * 4, axis=1)))
        q_ts.append(q[jb * BLOCK:(jb + 1) * BLOCK, :].T.astype(BF16))

    scores, probs, outs = {}, {}, {}
    for t in range(n_units + 2):
        if t == 1:
            yz_ref[...] = jnp.zeros_like(yz_ref)
        if t == 3:
            wub_ref[...] = wu_ref[...].astype(BF16)
        if t == 5:
            wdb_ref[...] = wd_ref[...].astype(BF16)
        if t < n_units:
            jb, g = divmod(t, N_KV_HEADS)
            scores[t] = _attn_scores(q_ts[jb], g, k_all[jb * BLOCK:(jb + 3) * BLOCK, :], k_ctx,
                                     *biases[jb])
            gate_chunk(t)
        if 0 <= t - 1 < n_units:
            g = (t - 1) % N_KV_HEADS
            probs[t - 1] = _attn_probs(scores.pop(t - 1),
                                       sink_row[:, g * 4 * BLOCK:(g + 1) * 4 * BLOCK])
        if 0 <= t - 2 < n_units:
            jb, g = divmod(t - 2, N_KV_HEADS)
            outs[t - 2] = _attn_values(
                probs.pop(t - 2), sink_row[:, g * 4 * BLOCK:(g + 1) * 4 * BLOCK],
                vt_all[g * HEAD_DIM:(g + 1) * HEAD_DIM, jb * BLOCK:(jb + 3) * BLOCK],
                vt_ctx[g * HEAD_DIM:(g + 1) * HEAD_DIM, :])
    o_blocks = []
    for jb in range(BLOCKS_PER_TILE):
        o_rows = [outs[jb * N_KV_HEADS + g][:, h * BLOCK:(h + 1) * BLOCK]
                  for g in range(N_KV_HEADS) for h in range(4)]
        o_blocks.append(jnp.concatenate(o_rows, axis=0).T)
    gates = jnp.concatenate(gate_chunks, axis=1)

    half_blocks = BLOCKS_PER_TILE // 2
    for hf in range(2):
        r0, r1 = hf * half_blocks * BLOCK, (hf + 1) * half_blocks * BLOCK
        o = jnp.concatenate(o_blocks[hf * half_blocks:(hf + 1) * half_blocks], axis=0).astype(BF16)
        yb = jnp.dot(o, wbb_ref[...], preferred_element_type=F32)
        merged = (_sigmoid(gates[r0:r1, :D_MODEL]) * ya[r0:r1]
                  + _sigmoid(gates[r0:r1, D_MODEL:]) * yb).astype(BF16)
        x1 = x[r0:r1] + gt1_ref[0] * jnp.dot(merged, wo_ref[...], preferred_element_type=F32)
        x1_ref[0, r0:r1, :] = x1
        h2 = _rms_mod(x1, g2n_ref[...], sh2_ref[0], sc2_ref[0]).astype(BF16)
        h2_ref[0, r0:r1, :] = h2
        logits_t = lax.dot_general(wr_ref[...], h2, (((1,), (1,)), ((), ())),
                                   preferred_element_type=F32) + br_ref[...]
        route_ref[0, :, r0:r1] = _route(logits_t)


def _mix(x, norm1_g, norm2_g, mod3, w_b_cols, cos_t, sin_t, uc, k_rot, v_t, k_ctx, vt_ctx,
         w_conv, b_conv, w_a, w_b, w_o, sink_row, bias, wr_t, br_col, w_up, w_down):
    bsz, seq, _ = x.shape
    nt = seq // SEQ_TILE
    nblk = seq // BLOCK
    clen = k_ctx.shape[1]
    rows8 = SEQ_TILE // 8
    assert N_EXPERTS % (nt * bsz) == 0, "each grid step converts an equal share of the experts"
    e_step = N_EXPERTS // (nt * bsz)

    def mod_spec(j):
        return pl.BlockSpec((1, 1, D_MODEL), lambda t, b, j=j: (b, 0, j))

    def const_spec(shape):
        return pl.BlockSpec(shape, lambda t, b: tuple(0 for _ in shape))

    in_specs = [
        pl.BlockSpec((1, SEQ_TILE, D_MODEL), lambda t, b: (b, t, 0)),
        const_spec((1, D_MODEL)), const_spec((1, D_MODEL)),
        mod_spec(0), mod_spec(1), mod_spec(2), mod_spec(3), mod_spec(4),
        const_spec(w_b_cols.shape),
        pl.BlockSpec((SEQ_TILE, LANES), lambda t, b: (t, 0)),
        pl.BlockSpec((SEQ_TILE, LANES), lambda t, b: (t, 0)),
        pl.BlockSpec((1, SEQ_TILE, CONV_W), lambda t, b: (b, t, 0)),
        pl.BlockSpec((1, 8, CONV_W), lambda t, b: (b, jnp.maximum(t * rows8 - 1, 0), 0)),
        pl.BlockSpec((1, 8, CONV_W),
                     lambda t, b: (b, jnp.minimum((t + 1) * rows8, seq // 8 - 1), 0)),
        pl.BlockSpec((1, SEQ_TILE, KV_W), lambda t, b: (b, t, 0)),
        pl.BlockSpec((1, BLOCK, KV_W),
                     lambda t, b: (b, jnp.maximum(t * BLOCKS_PER_TILE - 1, 0), 0)),
        pl.BlockSpec((1, BLOCK, KV_W),
                     lambda t, b: (b, jnp.minimum((t + 1) * BLOCKS_PER_TILE, nblk - 1), 0)),
        pl.BlockSpec((1, KV_W, SEQ_TILE), lambda t, b: (b, 0, t)),
        pl.BlockSpec((1, KV_W, BLOCK),
                     lambda t, b: (b, 0, jnp.maximum(t * BLOCKS_PER_TILE - 1, 0))),
        pl.BlockSpec((1, KV_W, BLOCK),
                     lambda t, b: (b, 0, jnp.minimum((t + 1) * BLOCKS_PER_TILE, nblk - 1))),
        pl.BlockSpec((1, clen, KV_W), lambda t, b: (b, 0, 0)),
        pl.BlockSpec((1, KV_W, clen), lambda t, b: (b, 0, 0)),
        const_spec(w_conv.shape), const_spec(b_conv.shape),
        const_spec(w_a.shape), const_spec(w_b.shape), const_spec(w_o.shape),
        const_spec(sink_row.shape), const_spec(bias.shape),
        const_spec(wr_t.shape), const_spec(br_col.shape),
        pl.BlockSpec((e_step,) + w_up.shape[1:], lambda t, b: (t * bsz + b, 0, 0)),
        pl.BlockSpec((e_step,) + w_down.shape[1:], lambda t, b: (t * bsz + b, 0, 0)),
    ]
    out_specs = [
        pl.BlockSpec((1, SEQ_TILE, D_MODEL), lambda t, b: (b, t, 0)),
        pl.BlockSpec((1, SEQ_TILE, D_MODEL), lambda t, b: (b, t, 0)),
        pl.BlockSpec((1, ROUTE_ROWS, SEQ_TILE), lambda t, b: (b, 0, t)),
        pl.BlockSpec((CHUNK_STRIDE, D_MODEL), lambda t, b: (b * nt + t, 0)),
        pl.BlockSpec((e_step,) + w_up.shape[1:], lambda t, b: (t * bsz + b, 0, 0)),
        pl.BlockSpec((e_step,) + w_down.shape[1:], lambda t, b: (t * bsz + b, 0, 0)),
    ]
    out_shape = [
        jax.ShapeDtypeStruct((bsz, seq, D_MODEL), F32),
        jax.ShapeDtypeStruct((bsz, seq, D_MODEL), BF16),
        jax.ShapeDtypeStruct((bsz, ROUTE_ROWS, seq), F32),
        jax.ShapeDtypeStruct((bsz * nt * CHUNK_STRIDE, D_MODEL), BF16),
        jax.ShapeDtypeStruct(w_up.shape, BF16),
        jax.ShapeDtypeStruct(w_down.shape, BF16),
    ]
    return pl.pallas_call(
        _mix_kernel,
        grid=(nt, bsz),
        in_specs=in_specs,
        out_specs=out_specs,
        out_shape=out_shape,
        compiler_params=_cparams(("arbitrary", "arbitrary")),
        name="token_mix",
    )(x, norm1_g, norm2_g, mod3, mod3, mod3, mod3, mod3, w_b_cols, cos_t, sin_t,
      uc, uc, uc, k_rot, k_rot, k_rot, v_t, v_t, v_t, k_ctx, vt_ctx,
      w_conv, b_conv, w_a, w_b, w_o, sink_row, bias, wr_t, br_col, w_up, w_down)


def _bf16_parts(v):
    hi = v.astype(BF16).astype(F32)
    r1 = v - hi
    mid = r1.astype(BF16).astype(F32)
    lo = (r1 - mid).astype(BF16).astype(F32)
    return hi, mid, lo


def _dispatch_kernel(h_ref, route_ref, upper_ref, lower_ref, xl_ref, pos_ref, nb_ref):
    route = route_ref[0]
    e1 = route[0:1, :].astype(I32)
    e2 = route[1:2, :].astype(I32)
    erow = lax.broadcasted_iota(I32, (N_EXPERTS, CHUNK), 0)
    hit1 = erow == e1
    hit2 = erow == e2
    onehot = jnp.where(hit1, 1.0, 0.0) + jnp.where(hit2, 1.0, 0.0)
    cum = jnp.dot(onehot.astype(BF16), upper_ref[...], preferred_element_type=F32)
    cnt = jnp.sum(onehot, axis=1, keepdims=True)
    nblk = jnp.floor((cnt + (ROW_BLOCK - 1)) * (1.0 / ROW_BLOCK))
    nblk_b = jnp.broadcast_to(nblk, (N_EXPERTS, LANES))
    seg = jnp.dot(lower_ref[...], nblk_b.astype(BF16), preferred_element_type=F32) * ROW_BLOCK
    base = seg[:, 0:1] + cum
    pos1 = jnp.sum(jnp.where(hit1, base, 0.0), axis=0, keepdims=True)
    pos2 = jnp.sum(jnp.where(hit2, base, 0.0), axis=0, keepdims=True)
    p1i = pos1.astype(I32)
    p2i = pos2.astype(I32)
    used_rows = (jnp.sum(nblk) * ROW_BLOCK).astype(I32)

    prow = lax.broadcasted_iota(I32, (LANES, CHUNK), 0)
    gate1 = route[2:3, :]
    gate2 = route[3:4, :]
    glane = lax.broadcasted_iota(I32, (1, GATE_COLS), 1)

    for rc in range(CHUNK_ROWS // DISP_ROWS):
        rows = pl.ds(rc * DISP_ROWS, DISP_ROWS)

        def live(rc=rc, rows=rows):
            r = lax.broadcasted_iota(I32, (DISP_ROWS, CHUNK), 0) + rc * DISP_ROWS
            m1 = r == p1i
            m2 = r == p2i
            sel = jnp.where(m1, 1.0, jnp.where(m2, 1.0, 0.0)).astype(BF16)
            xl_ref[rows, 0:D_MODEL] = jnp.dot(sel, h_ref[...],
                                              preferred_element_type=F32).astype(BF16)
            gate = jnp.sum(jnp.where(m1, gate1, jnp.where(m2, gate2, 0.0)), axis=1, keepdims=True)
            hi, mid, lo = _bf16_parts(gate)
            xl_ref[rows, D_MODEL:XL_COLS] = jnp.where(
                glane == 0, hi, jnp.where(glane == 1, mid, jnp.where(glane == 2, lo, 0.0))
            ).astype(BF16)

        def dead(rows=rows):
            xl_ref[rows, :] = jnp.zeros((DISP_ROWS, XL_COLS), BF16)

        if (rc + 1) * DISP_ROWS <= 2 * CHUNK:
            live()
        else:
            pl.when(rc * DISP_ROWS < used_rows)(live)
            pl.when(rc * DISP_ROWS >= used_rows)(dead)
    xl_ref[CHUNK_ROWS:CHUNK_STRIDE, :] = jnp.zeros((CHUNK_STRIDE - CHUNK_ROWS, XL_COLS), BF16)

    pos_rows = jnp.where(prow == 0, pos1, jnp.where(prow == 1, pos2, 0.0))
    pos_ref[...] = pos_rows.T
    nb_ref[0] = nblk_b.astype(I32)


def _dispatch(h2, route, upper, lower):
    n = h2.shape[0]
    nch = n // CHUNK
    per_seq = route.shape[2] // CHUNK
    return pl.pallas_call(
        _dispatch_kernel,
        grid=(nch,),
        in_specs=[pl.BlockSpec((CHUNK, D_MODEL), lambda c: (c, 0)),
                  pl.BlockSpec((1, ROUTE_ROWS, CHUNK), lambda c: (c // per_seq, 0, c % per_seq)),
                  pl.BlockSpec(upper.shape, lambda c: (0, 0)),
                  pl.BlockSpec(lower.shape, lambda c: (0, 0))],
        out_specs=[pl.BlockSpec((CHUNK_STRIDE, XL_COLS), lambda c: (c, 0)),
                   pl.BlockSpec((CHUNK, LANES), lambda c: (c, 0)),
                   pl.BlockSpec((1, N_EXPERTS, LANES), lambda c: (c, 0, 0))],
        out_shape=[jax.ShapeDtypeStruct((nch * CHUNK_STRIDE, XL_COLS), BF16),
                   jax.ShapeDtypeStruct((n, LANES), F32),
                   jax.ShapeDtypeStruct((nch, N_EXPERTS, LANES), I32)],
        compiler_params=_cparams(("arbitrary",)),
        name="moe_dispatch",
    )(h2, route, upper, lower)


def _max_tiles(nch):
    max_blocks = nch * (2 * CHUNK + N_EXPERTS * (ROW_BLOCK - 1)) // ROW_BLOCK
    return max_blocks // TILE_BLOCKS + N_EXPERTS


def _masked_prefix(le, values):
    delta = values - jnp.concatenate([jnp.zeros((1,), values.dtype), values[:-1]])
    return jnp.sum(jnp.where(le, delta[None, :], 0), axis=1)


def _tile_plan(nb, nch):
    n_tiles = _max_tiles(nch)
    nbt = nb.T
    nbe = jnp.sum(nbt, axis=1)
    nte = (nbe + TILE_BLOCKS - 1) // TILE_BLOCKS
    tile_end = jnp.cumsum(nte)
    tile_start = tile_end - nte
    n_act = tile_end[-1]
    tiles = jnp.arange(n_tiles, dtype=I32)
    te = jnp.sum((tile_end[None, :] <= tiles[:, None]).astype(I32), axis=1)
    e_ar = jnp.arange(N_EXPERTS, dtype=I32)
    active = nte > 0
    te = jnp.where(tiles < n_act, te, jnp.max(jnp.where(active, e_ar, 0)))
    first = jnp.logical_and(te != jnp.concatenate([jnp.full((1,), -1, I32), te[:-1]]),
                            tiles < n_act).astype(I32)
    rank = jnp.cumsum(active.astype(I32)) - 1
    k_ar = jnp.arange(N_EXPERTS + W_SLOTS, dtype=I32)
    eseq = jnp.sum(jnp.where(jnp.logical_and(active[None, :], rank[None, :] == k_ar[:, None]),
                             e_ar[None, :], 0), axis=1)
    n_exp = jnp.sum(active.astype(I32))

    cb_excl = jnp.cumsum(nbt, axis=1) - nbt
    gs = TILE_BLOCKS * tile_start[:, None] + cb_excl
    seg_off = (ROW_BLOCK * (jnp.cumsum(nb, axis=1) - nb)).T
    base_row = jnp.arange(nch, dtype=I32)[None, :] * CHUNK_STRIDE + seg_off
    gs_f = gs.reshape(-1)
    slots = jnp.arange(n_tiles * TILE_BLOCKS, dtype=I32)
    le = gs_f[None, :] <= slots[:, None]
    row = ROW_BLOCK * slots + _masked_prefix(le, (base_row - ROW_BLOCK * gs).reshape(-1))
    valid = slots < _masked_prefix(le, (gs + nbt).reshape(-1))
    t = slots % (nch * TRASH_BLOCKS)
    trash = (t % nch) * CHUNK_STRIDE + CHUNK_ROWS + (t // nch) * ROW_BLOCK
    grow = (jnp.where(valid, row, 0) // ROW_BLOCK).astype(I32)
    srow = (jnp.where(valid, row, trash) // ROW_BLOCK).astype(I32)
    return (n_act.reshape(1).astype(I32), n_exp.reshape(1), te.astype(I32), first, eseq,
            grow, srow)


def _gmm_kernel(nact_ref, nexp_ref, te_ref, first_ref, eseq_ref, grow_ref, srow_ref,
                xl_hbm, yz_hbm, wu_hbm, wd_hbm, yl_hbm,
                xbuf, ybuf, wu_st, wd_st, in_sem, out_sem, w_sem):
    del te_ref, yz_hbm
    n_act = nact_ref[0]
    n_exp = nexp_ref[0]
    prefetch = X_SLOTS - 1

    def gather(tile, b):
        s = tile % X_SLOTS
        blk = grow_ref[tile * TILE_BLOCKS + b]
        return pltpu.make_async_copy(xl_hbm.at[blk], xbuf.at[s, b], in_sem.at[s])

    def scatter(tile, b):
        s = tile % Y_SLOTS
        blk = srow_ref[tile * TILE_BLOCKS + b]
        return pltpu.make_async_copy(ybuf.at[s, b], yl_hbm.at[blk], out_sem.at[s])

    def weights(q):
        e = eseq_ref[q]
        s = q % W_SLOTS
        return (pltpu.make_async_copy(wu_hbm.at[e], wu_st.at[s], w_sem.at[0, s]),
                pltpu.make_async_copy(wd_hbm.at[e], wd_st.at[s], w_sem.at[1, s]))

    def all_blocks(fn):
        for b in range(TILE_BLOCKS):
            fn(b)

    for q0 in range(W_SLOTS - 1):
        @pl.when(q0 < n_exp)
        def _(q0=q0):
            for cp in weights(q0):
                cp.start()
    for t0 in range(prefetch):
        @pl.when(t0 < n_act)
        def _(t0=t0):
            all_blocks(lambda b: gather(t0, b).start())

    def body(i, q):
        @pl.when(i + prefetch < n_act)
        def _():
            all_blocks(lambda b: gather(i + prefetch, b).start())

        all_blocks(lambda b: gather(i, b).wait())

        @pl.when(i >= Y_SLOTS)
        def _():
            all_blocks(lambda b: scatter(i - Y_SLOTS, b).wait())

        is_first = first_ref[i] == 1

        @pl.when(is_first)
        def _():
            for cp in weights(q):
                cp.wait()

            @pl.when(q + W_SLOTS - 1 < n_exp)
            def _():
                for cp in weights(q + W_SLOTS - 1):
                    cp.start()

        q = q + is_first.astype(I32)
        ws = (q - 1) % W_SLOTS
        xs = i % X_SLOTS
        ys = i % Y_SLOTS
        sub_blocks = SUB_ROWS // ROW_BLOCK
        for h in range(TILE_ROWS // SUB_ROWS):
            blocks = pl.ds(h * sub_blocks, sub_blocks)
            x = xbuf[xs, blocks].reshape(SUB_ROWS, XL_COLS)
            gate = jnp.sum(x[:, D_MODEL:].astype(F32), axis=1, keepdims=True)
            au = jnp.dot(x[:, :D_MODEL], wu_st[ws], preferred_element_type=F32)
            a = au[:, :EXPERT_FF]
            act = (a * _sigmoid(a)) * au[:, EXPERT_FF:]
            y = jnp.dot(act.astype(BF16), wd_st[ws], preferred_element_type=F32)
            ybuf[ys, blocks] = (gate * y).astype(BF16).reshape(sub_blocks, ROW_BLOCK, D_MODEL)

        all_blocks(lambda b: scatter(i, b).start())
        return q

    lax.fori_loop(0, n_act, body, jnp.int32(0))

    for k in range(Y_SLOTS):
        @pl.when(n_act - 1 - k >= 0)
        def _(k=k):
            all_blocks(lambda b: scatter(n_act - 1 - k, b).wait())


def _grouped_mlp(plan, xl, yz, w_up, w_down):
    grid_spec = pltpu.PrefetchScalarGridSpec(
        num_scalar_prefetch=len(plan),
        grid=(1,),
        in_specs=[pl.BlockSpec(memory_space=pl.ANY)] * 4,
        out_specs=pl.BlockSpec(memory_space=pl.ANY),
        scratch_shapes=[pltpu.VMEM((X_SLOTS, TILE_BLOCKS, ROW_BLOCK, XL_COLS), BF16),
                        pltpu.VMEM((Y_SLOTS, TILE_BLOCKS, ROW_BLOCK, D_MODEL), BF16),
                        pltpu.VMEM((W_SLOTS, D_MODEL, 2 * EXPERT_FF), BF16),
                        pltpu.VMEM((W_SLOTS, EXPERT_FF, D_MODEL), BF16),
                        pltpu.SemaphoreType.DMA((X_SLOTS,)),
                        pltpu.SemaphoreType.DMA((Y_SLOTS,)),
                        pltpu.SemaphoreType.DMA((2, W_SLOTS))])
    return pl.pallas_call(
        _gmm_kernel,
        grid_spec=grid_spec,
        out_shape=jax.ShapeDtypeStruct(yz.shape, yz.dtype),
        input_output_aliases={len(plan) + 1: 0},
        compiler_params=_cparams(("arbitrary",)),
        name="moe_experts",
    )(*plan, xl, yz, w_up, w_down)


def _combine_kernel(used_ref, yl_ref, pos_ref, x1_ref, gt2_ref, fg_ref, o_ref, acc_ref):
    used_rows = used_ref[pl.program_id(0)]
    pos = pos_ref[...]
    p1 = pos[:, 0:1].astype(I32)
    p2 = pos[:, 1:2].astype(I32)

    def partial_sum(k0, k1):
        r = lax.broadcasted_iota(I32, (CHUNK, k1 - k0), 1) + k0
        sel = jnp.where(r == p1, 1.0, jnp.where(r == p2, 1.0, 0.0)).astype(BF16)
        return jnp.dot(sel, yl_ref[k0:k1, :], preferred_element_type=F32)

    acc_ref[...] = partial_sum(0, 2 * CHUNK)
    for k0 in range(2 * CHUNK, CHUNK_ROWS, DISP_ROWS):
        @pl.when(k0 < used_rows)
        def _(k0=k0):
            acc_ref[...] += partial_sum(k0, k0 + DISP_ROWS)

    x2 = x1_ref[...] + gt2_ref[0] * acc_ref[...]
    inv = lax.rsqrt(jnp.mean(x2 * x2, axis=-1, keepdims=True) + NORM_EPS)
    o_ref[...] = (x2 * inv) * fg_ref[...]


def _combine(used_rows, yl, pos_c, x1, mod3, final_g, seq):
    n = x1.shape[0]
    per_seq = seq // CHUNK
    grid_spec = pltpu.PrefetchScalarGridSpec(
        num_scalar_prefetch=1,
        grid=(n // CHUNK,),
        in_specs=[pl.BlockSpec((CHUNK_STRIDE, D_MODEL), lambda c, u: (c, 0)),
                  pl.BlockSpec((CHUNK, LANES), lambda c, u: (c, 0)),
                  pl.BlockSpec((CHUNK, D_MODEL), lambda c, u: (c, 0)),
                  pl.BlockSpec((1, 1, D_MODEL), lambda c, u: (c // per_seq, 0, 5)),
                  pl.BlockSpec((1, D_MODEL), lambda c, u: (0, 0))],
        out_specs=pl.BlockSpec((CHUNK, D_MODEL), lambda c, u: (c, 0)),
        scratch_shapes=[pltpu.VMEM((CHUNK, D_MODEL), F32)])
    return pl.pallas_call(
        _combine_kernel,
        grid_spec=grid_spec,
        out_shape=jax.ShapeDtypeStruct((n, D_MODEL), F32),
        compiler_params=_cparams(("arbitrary",)),
        name="moe_combine",
    )(used_rows, yl, pos_c, x1, mod3, final_g)


def _rope_tables(seq):
    n_freq = HEAD_DIM // 4
    inv_freq = ROPE_BASE ** (-jnp.arange(n_freq, dtype=F32) / n_freq)
    rows = seq // GRID_W
    row = jnp.repeat(jnp.arange(rows, dtype=F32), GRID_W)
    col = jnp.tile(jnp.arange(GRID_W, dtype=F32), rows)
    ang = jnp.concatenate([row[:, None] * inv_freq, col[:, None] * inv_freq], axis=-1)
    cos, sin = jnp.cos(ang), jnp.sin(ang)
    return jnp.tile(cos, (1, 4)), jnp.concatenate([-sin, sin, -sin, sin], axis=1)


def _window_bias():
    key = np.arange(3 * BLOCK)[:, None]
    qry = np.arange(BLOCK)[None, :]
    valid = (key - qry >= 0) & (key - qry <= 2 * BLOCK)
    return jnp.asarray(np.where(valid, 0.0, NEG_INF), F32)


def kernel(x, c, ctx, c_ctx, w_ada, b_ada, norm1_g, w_in, w_conv, b_conv, w_a, w_b, sink, w_o,
           norm2_g, w_group, b_group, w_router, b_router, w_up, w_down, final_g):
    bsz, seq, _ = x.shape
    assert w_ada.shape[0] == 1 and seq % SEQ_TILE == 0

    cc = jnp.zeros((16, D_MODEL), F32).at[:bsz].set(c).at[bsz].set(c_ctx)
    mod = _modulation(cc, w_ada[0], b_ada[0][None, :])
    mod3 = mod[:bsz].reshape(bsz, 1, N_MOD * D_MODEL)
    csh1 = mod[bsz:bsz + 1, 0:D_MODEL]
    csc1 = mod[bsz:bsz + 1, D_MODEL:2 * D_MODEL]

    w = w_in[0]
    n1g = norm1_g[0][None, :]
    n2g = norm2_g[0][None, :]
    w_kv = w[:, OFF_K:OFF_GA].astype(BF16)
    w_a_cols = jnp.concatenate([w[:, OFF_CG:OFF_Q], w[:, OFF_K:OFF_GA]], axis=1).astype(BF16)
    w_b_cols = jnp.concatenate([w[:, OFF_B:OFF_CG], w[:, OFF_Q:OFF_K], w[:, OFF_GA:]],
                               axis=1).astype(BF16)
    cos_t, sin_t = _rope_tables(seq)

    k_ctx, vt_ctx = _context_kv(ctx, n1g, csh1, csc1, w_kv)
    uc, k_rot, v_t = _proj_a(x, n1g, mod3, w_a_cols, cos_t, sin_t)

    sink_row = jnp.repeat(sink[0].astype(F32), BLOCK)[None, :]
    wr_t = jnp.zeros((LANES, D_MODEL), F32)
    wr_t = wr_t.at[0:N_GROUPS].set(w_group[0].T)
    wr_t = wr_t.at[GROUP_ROWS:GROUP_ROWS + N_EXPERTS].set(w_router[0].T).astype(BF16)
    br_col = jnp.zeros((LANES, 1), F32)
    br_col = br_col.at[0:N_GROUPS, 0].set(b_group[0])
    br_col = br_col.at[GROUP_ROWS:GROUP_ROWS + N_EXPERTS, 0].set(b_router[0])

    x1, h2, route, yz, w_up_bf, w_down_bf = _mix(
        x, n1g, n2g, mod3, w_b_cols, cos_t, sin_t, uc, k_rot, v_t, k_ctx, vt_ctx,
        w_conv[0], b_conv[0][None, :], w_a[0].astype(BF16), w_b[0].astype(BF16),
        w_o[0].astype(BF16), sink_row, _window_bias(), wr_t, br_col, w_up[0], w_down[0])

    n = bsz * seq
    nch = n // CHUNK
    upper = jnp.asarray(np.triu(np.ones((CHUNK, CHUNK), np.float32), 1), BF16)
    lower = jnp.asarray(np.tril(np.ones((N_EXPERTS, N_EXPERTS), np.float32), -1), BF16)
    xl, pos_c, nb = _dispatch(h2.reshape(n, D_MODEL), route, upper, lower)
    nb = nb[:, :, 0]
    yl = _grouped_mlp(_tile_plan(nb, nch), xl.reshape(-1, ROW_BLOCK, XL_COLS),
                      yz.reshape(-1, ROW_BLOCK, D_MODEL), w_up_bf, w_down_bf)
    yl = yl.reshape(-1, D_MODEL)
    used_rows = (ROW_BLOCK * jnp.sum(nb, axis=1)).astype(I32)
    out = _combine(used_rows, yl, pos_c, x1.reshape(n, D_MODEL), mod3, final_g[None, :], seq)
    return out.reshape(bsz, seq, D_MODEL)
```

```python
import numpy as np
import jax
import jax.numpy as jnp
from jax import lax
from jax.experimental import pallas as pl
from jax.experimental.pallas import tpu as pltpu

F32 = jnp.float32
BF16 = jnp.bfloat16
I32 = jnp.int32

D_MODEL = 1024
GRID_W = 64
CONV_W = 512
N_HEADS = 8
N_KV_HEADS = 2
HEAD_DIM = 64
ATT_W = N_HEADS * HEAD_DIM
KV_W = N_KV_HEADS * HEAD_DIM
BLOCK = 128
ROPE_BASE = 10000.0
N_GROUPS = 4
EXPERTS_PER_GROUP = 8
N_EXPERTS = N_GROUPS * EXPERTS_PER_GROUP
EXPERT_FF = 256
N_MOD = 6
NORM_EPS = 1e-6
NEG_INF = -1e30
LOG2E = 1.4426950408889634
ONES_ROWS = 16

OFF_B, OFF_CG, OFF_XIN, OFF_Q, OFF_K, OFF_V, OFF_GA, OFF_GB = (
    0, 512, 1024, 1536, 2048, 2176, 2304, 3328)
IN_COLS = 4352

LANES = 128
SEQ_TILE = 512
BLOCKS_PER_TILE = SEQ_TILE // BLOCK
ROUTE_ROWS = 8
GROUP_ROWS = 8
VMEM_LIMIT = 56 * 1024 * 1024

CHUNK = SEQ_TILE
ROW_BLOCK = 16
CHUNK_ROWS = -(-(2 * CHUNK + N_EXPERTS * (ROW_BLOCK - 1)) // 256) * 256
TRASH_BLOCKS = 4
CHUNK_STRIDE = CHUNK_ROWS + TRASH_BLOCKS * ROW_BLOCK
GATE_COLS = LANES
XL_COLS = D_MODEL + GATE_COLS
DISP_ROWS = 256
TILE_BLOCKS = 32
TILE_ROWS = TILE_BLOCKS * ROW_BLOCK
SUB_ROWS = 256
X_SLOTS = 4
Y_SLOTS = 4
W_SLOTS = 3


def _cparams(sem):
    return pltpu.CompilerParams(dimension_semantics=sem, vmem_limit_bytes=VMEM_LIMIT)


def _rms_mod(x, g, shift, scale):
    inv = lax.rsqrt(jnp.mean(x * x, axis=-1, keepdims=True) + NORM_EPS)
    return (x * inv) * g * (1.0 + scale) + shift


def _mod_kernel(c_ref, w_ref, b_ref, o_ref):
    c = c_ref[...]
    a = (c * jax.nn.sigmoid(c)).astype(BF16)
    o_ref[...] = jnp.dot(a, w_ref[...].astype(BF16), preferred_element_type=F32) + b_ref[...]


def _modulation(cc, w_ada, b_ada):
    rows = cc.shape[0]
    cols = w_ada.shape[1]
    tile = 1024
    return pl.pallas_call(
        _mod_kernel,
        grid=(cols // tile,),
        in_specs=[pl.BlockSpec((rows, D_MODEL), lambda j: (0, 0)),
                  pl.BlockSpec((D_MODEL, tile), lambda j: (0, j)),
                  pl.BlockSpec((1, tile), lambda j: (0, j))],
        out_specs=pl.BlockSpec((rows, tile), lambda j: (0, j)),
        out_shape=jax.ShapeDtypeStruct((rows, cols), F32),
        compiler_params=_cparams(("arbitrary",)),
        name="adaln_mod",
    )(cc, w_ada, b_ada)


def _ctx_kernel(x_ref, g_ref, sh_ref, sc_ref, w_ref, k_ref, vt_ref):
    h = _rms_mod(x_ref[0], g_ref[...], sh_ref[...], sc_ref[...]).astype(BF16)
    kv = jnp.dot(h, w_ref[...], preferred_element_type=F32)
    k_ref[0] = kv[:, :KV_W].astype(BF16)
    vt_ref[0] = kv[:, KV_W:].T.astype(BF16)


def _context_kv(ctx, norm_g, csh, csc, w_kv):
    bsz, clen, _ = ctx.shape
    return pl.pallas_call(
        _ctx_kernel,
        grid=(bsz,),
        in_specs=[pl.BlockSpec((1, clen, D_MODEL), lambda b: (b, 0, 0)),
                  pl.BlockSpec((1, D_MODEL), lambda b: (0, 0)),
                  pl.BlockSpec((1, D_MODEL), lambda b: (0, 0)),
                  pl.BlockSpec((1, D_MODEL), lambda b: (0, 0)),
                  pl.BlockSpec((D_MODEL, 2 * KV_W), lambda b: (0, 0))],
        out_specs=[pl.BlockSpec((1, clen, KV_W), lambda b: (b, 0, 0)),
                   pl.BlockSpec((1, KV_W, clen), lambda b: (b, 0, 0))],
        out_shape=[jax.ShapeDtypeStruct((bsz, clen, KV_W), BF16),
                   jax.ShapeDtypeStruct((bsz, KV_W, clen), BF16)],
        compiler_params=_cparams(("arbitrary",)),
        name="context_kv",
    )(ctx, norm_g, csh, csc, w_kv)


def _rope(t, cos, sin_signed):
    lane = lax.broadcasted_iota(I32, (1, LANES), 1)
    first_half = (lane % HEAD_DIM) < (HEAD_DIM // 2)
    outs = []
    for j in range(t.shape[1] // LANES):
        tj = t[:, j * LANES:(j + 1) * LANES]
        partner = jnp.where(first_half,
                            pltpu.roll(tj, LANES - HEAD_DIM // 2, axis=1),
                            pltpu.roll(tj, HEAD_DIM // 2, axis=1))
        outs.append(tj * cos + partner * sin_signed)
    return outs[0] if len(outs) == 1 else jnp.concatenate(outs, axis=1)


def _proj_a_kernel(x_ref, g_ref, sh_ref, sc_ref, w_ref, cos_ref, sin_ref, uc_ref, k_ref, vt_ref):
    h = _rms_mod(x_ref[0], g_ref[...], sh_ref[0], sc_ref[0]).astype(BF16)
    u = jnp.dot(h, w_ref[...], preferred_element_type=F32)
    uc_ref[0] = u[:, :CONV_W] * u[:, CONV_W:2 * CONV_W]
    k = u[:, 2 * CONV_W:2 * CONV_W + KV_W]
    k_ref[0] = _rope(k, cos_ref[...], sin_ref[...]).astype(BF16)
    vt_ref[0] = u[:, 2 * CONV_W + KV_W:].T.astype(BF16)


def _proj_a(x, norm_g, mod3, w_a_cols, cos_t, sin_t):
    bsz, seq, _ = x.shape
    nt = seq // SEQ_TILE
    wcols = w_a_cols.shape[1]
    return pl.pallas_call(
        _proj_a_kernel,
        grid=(nt, bsz),
        in_specs=[pl.BlockSpec((1, SEQ_TILE, D_MODEL), lambda t, b: (b, t, 0)),
                  pl.BlockSpec((1, D_MODEL), lambda t, b: (0, 0)),
                  pl.BlockSpec((1, 1, D_MODEL), lambda t, b: (b, 0, 0)),
                  pl.BlockSpec((1, 1, D_MODEL), lambda t, b: (b, 0, 1)),
                  pl.BlockSpec((D_MODEL, wcols), lambda t, b: (0, 0)),
                  pl.BlockSpec((SEQ_TILE, LANES), lambda t, b: (t, 0)),
                  pl.BlockSpec((SEQ_TILE, LANES), lambda t, b: (t, 0))],
        out_specs=[pl.BlockSpec((1, SEQ_TILE, CONV_W), lambda t, b: (b, t, 0)),
                   pl.BlockSpec((1, SEQ_TILE, KV_W), lambda t, b: (b, t, 0)),
                   pl.BlockSpec((1, KV_W, SEQ_TILE), lambda t, b: (b, 0, t))],
        out_shape=[jax.ShapeDtypeStruct((bsz, seq, CONV_W), F32),
                   jax.ShapeDtypeStruct((bsz, seq, KV_W), BF16),
                   jax.ShapeDtypeStruct((bsz, KV_W, seq), BF16)],
        compiler_params=_cparams(("arbitrary", "arbitrary")),
        name="proj_a",
    )(x, norm_g, mod3, mod3, w_a_cols, cos_t, sin_t)


def _sigmoid(x):
    return 1.0 / (1.0 + jnp.exp2(x * (-LOG2E)))


def _attn_scores(q_t, g, k_win, k_ctx, bias_p4, bias_n4):
    zeros = jnp.zeros((HEAD_DIM, 4 * BLOCK), BF16)
    qg = jnp.concatenate([q_t[(4 * g + h) * HEAD_DIM:(4 * g + h + 1) * HEAD_DIM, :]
                          for h in range(4)], axis=1)
    rhs = jnp.concatenate([qg, zeros] if g == 0 else [zeros, qg], axis=0)
    s_win = jnp.dot(k_win, rhs, preferred_element_type=F32)
    s_ctx = jnp.dot(k_ctx, rhs, preferred_element_type=F32)
    return (s_win[0:BLOCK] + bias_p4, s_win[BLOCK:2 * BLOCK], s_win[2 * BLOCK:] + bias_n4, s_ctx)


def _attn_probs(scores, sink):
    m = sink
    for s in scores:
        m = jnp.maximum(m, jnp.max(s, axis=0, keepdims=True))
    p_win = jnp.concatenate([jnp.exp2(s - m).astype(BF16) for s in scores[:3]], axis=0)
    p_ctx = jnp.exp2(scores[3] - m).astype(BF16)
    return p_win, p_ctx, m


def _attn_values(probs, sink, vt_win_g, vt_ctx_g):
    p_win, p_ctx, m = probs

    def with_ones(vt):
        r = lax.broadcasted_iota(I32, (ONES_ROWS, vt.shape[1]), 0)
        return jnp.concatenate([vt, jnp.where(r == 0, 1.0, 0.0).astype(BF16)], axis=0)

    o_ext = (jnp.dot(with_ones(vt_win_g), p_win, preferred_element_type=F32)
             + jnp.dot(with_ones(vt_ctx_g), p_ctx, preferred_element_type=F32))
    denom = o_ext[HEAD_DIM:HEAD_DIM + 1, :] + jnp.exp2(sink - m)
    return o_ext[:HEAD_DIM, :] / denom


def _route(logits_t):
    t = logits_t.shape[1]
    grow = lax.broadcasted_iota(I32, (GROUP_ROWS, t), 0)
    gl = jnp.where(grow < N_GROUPS, logits_t[0:GROUP_ROWS, :], NEG_INF)
    gm = jnp.max(gl, axis=0, keepdims=True)
    p_g = 1.0 / jnp.sum(jnp.exp(gl - gm), axis=0, keepdims=True)
    g_idx = jnp.min(jnp.where(gl == gm, grow, N_GROUPS), axis=0, keepdims=True)

    erow = lax.broadcasted_iota(I32, (N_EXPERTS, t), 0)
    el = logits_t[GROUP_ROWS:GROUP_ROWS + N_EXPERTS, :]
    sel = (erow // EXPERTS_PER_GROUP) == g_idx
    em = jnp.where(sel, el, NEG_INF)
    m1 = jnp.max(em, axis=0, keepdims=True)
    i1 = jnp.min(jnp.where(em == m1, erow, N_EXPERTS), axis=0, keepdims=True)
    em2 = jnp.where(erow == i1, NEG_INF, em)
    m2 = jnp.max(em2, axis=0, keepdims=True)
    i2 = jnp.min(jnp.where(em2 == m2, erow, N_EXPERTS), axis=0, keepdims=True)
    z = jnp.sum(jnp.where(sel, jnp.exp(el - m1), 0.0), axis=0, keepdims=True)
    p1 = 1.0 / z
    p2 = jnp.exp(m2 - m1) / z
    gate1 = p_g * p1 / (p1 + p2)
    gate2 = p_g * p2 / (p1 + p2)
    pad = jnp.zeros((ROUTE_ROWS - 4, t), F32)
    return jnp.concatenate([i1.astype(F32), i2.astype(F32), gate1, gate2, pad], axis=0)


def _mix_kernel(x_ref, g1n_ref, g2n_ref, sh1_ref, sc1_ref, gt1_ref, sh2_ref, sc2_ref,
                wb_ref, cos_ref, sin_ref,
                uc_ref, ucp_ref, ucn_ref, k_ref, kp_ref, kn_ref, vt_ref, vtp_ref, vtn_ref,
                kc_ref, vtc_ref, wconv_ref, bconv_ref, wa_ref, wbb_ref, wo_ref,
                sink_ref, bias_ref, wr_ref, br_ref, wu_ref, wd_ref,
                x1_ref, h2_ref, route_ref, yz_ref, wub_ref, wdb_ref):
    t_idx = pl.program_id(0)
    n_tiles = pl.num_programs(0)
    is_first = t_idx == 0
    is_last = t_idx == n_tiles - 1

    x = x_ref[0]
    hb = _rms_mod(x, g1n_ref[...], sh1_ref[0], sc1_ref[0]).astype(BF16)

    bq = jnp.dot(hb, wb_ref[:, 0:2 * CONV_W], preferred_element_type=F32)
    uc = uc_ref[0]
    row = lax.broadcasted_iota(I32, (SEQ_TILE, 1), 0)
    prev_row = jnp.where(is_first, 0.0, ucp_ref[0][7:8, :])
    next_row = jnp.where(is_last, 0.0, ucn_ref[0][0:1, :])
    up = jnp.where(row == 0, prev_row, pltpu.roll(uc, 1, axis=0))
    dn = jnp.where(row == SEQ_TILE - 1, next_row, pltpu.roll(uc, SEQ_TILE - 1, axis=0))
    wconv = wconv_ref[...]
    y = bconv_ref[...] + (up * wconv[0:1, :] + uc * wconv[1:2, :] + dn * wconv[2:3, :])
    ya = jnp.dot((bq[:, :CONV_W] * y).astype(BF16), wa_ref[...], preferred_element_type=F32)

    q = _rope(bq[:, CONV_W:], cos_ref[...], sin_ref[...]) * (HEAD_DIM ** -0.5 * LOG2E)
    k_all = jnp.concatenate([kp_ref[0], k_ref[0], kn_ref[0]], axis=0)
    vt_all = jnp.concatenate([vtp_ref[0], vt_ref[0], vtn_ref[0]], axis=1)
    k_ctx = kc_ref[0]
    vt_ctx = vtc_ref[0]
    bias_prev = bias_ref[0:BLOCK, :]
    bias_next = bias_ref[2 * BLOCK:3 * BLOCK, :]
    sink_row = sink_ref[...] * LOG2E
    n_units = BLOCKS_PER_TILE * N_KV_HEADS
    gate_cols = 2 * D_MODEL // n_units
    gate_chunks = []

    def gate_chunk(u):
        c0 = 2 * CONV_W + u * gate_cols
        gate_chunks.append(jnp.dot(hb, wb_ref[:, c0:c0 + gate_cols], preferred_element_type=F32))

    q_ts, biases = [], []
    for jb in range(BLOCKS_PER_TILE):
        bias_p, bias_n = bias_prev, bias_next
        if jb == 0:
            bias_p = bias_prev + jnp.where(is_first, NEG_INF, 0.0)
        if jb == BLOCKS_PER_TILE - 1:
            bias_n = bias_next + jnp.where(is_last, NEG_INF, 0.0)
        biases.append((jnp.concatenate([bias_p] * 4, axis=1), jnp.concatenate([bias_n] * 4, axis=1)))
        q_ts.append(q[jb * BLOCK:(jb + 1) * BLOCK, :].T.astype(BF16))

    scores, probs, outs = {}, {}, {}
    for t in range(n_units + 2):
        if t == 1:
            yz_ref[...] = jnp.zeros_like(yz_ref)
        if t == 3:
            wub_ref[...] = wu_ref[...].astype(BF16)
        if t == 5:
            wdb_ref[...] = wd_ref[...].astype(BF16)
        if t < n_units:
            jb, g = divmod(t, N_KV_HEADS)
            scores[t] = _attn_scores(q_ts[jb], g, k_all[jb * BLOCK:(jb + 3) * BLOCK, :], k_ctx,
                                     *biases[jb])
            gate_chunk(t)
        if 0 <= t - 1 < n_units:
            g = (t - 1) % N_KV_HEADS
            probs[t - 1] = _attn_probs(scores.pop(t - 1),
                                       sink_row[:, g * 4 * BLOCK:(g + 1) * 4 * BLOCK])
        if 0 <= t - 2 < n_units:
            jb, g = divmod(t - 2, N_KV_HEADS)
            outs[t - 2] = _attn_values(
                probs.pop(t - 2), sink_row[:, g * 4 * BLOCK:(g + 1) * 4 * BLOCK],
                vt_all[g * HEAD_DIM:(g + 1) * HEAD_DIM, jb * BLOCK:(jb + 3) * BLOCK],
                vt_ctx[g * HEAD_DIM:(g + 1) * HEAD_DIM, :])
    o_blocks = []
    for jb in range(BLOCKS_PER_TILE):
        o_rows = [outs[jb * N_KV_HEADS + g][:, h * BLOCK:(h + 1) * BLOCK]
                  for g in range(N_KV_HEADS) for h in range(4)]
        o_blocks.append(jnp.concatenate(o_rows, axis=0).T)
    gates = jnp.concatenate(gate_chunks, axis=1)

    half_blocks = BLOCKS_PER_TILE // 2
    for hf in range(2):
        r0, r1 = hf * half_blocks * BLOCK, (hf + 1) * half_blocks * BLOCK
        o = jnp.concatenate(o_blocks[hf * half_blocks:(hf + 1) * half_blocks], axis=0).astype(BF16)
        yb = jnp.dot(o, wbb_ref[...], preferred_element_type=F32)
        merged = (_sigmoid(gates[r0:r1, :D_MODEL]) * ya[r0:r1]
                  + _sigmoid(gates[r0:r1, D_MODEL:]) * yb).astype(BF16)
        x1 = x[r0:r1] + gt1_ref[0] * jnp.dot(merged, wo_ref[...], preferred_element_type=F32)
        x1_ref[0, r0:r1, :] = x1
        h2 = _rms_mod(x1, g2n_ref[...], sh2_ref[0], sc2_ref[0]).astype(BF16)
        h2_ref[0, r0:r1, :] = h2
        logits_t = lax.dot_general(wr_ref[...], h2, (((1,), (1,)), ((), ())),
                                   preferred_element_type=F32) + br_ref[...]
        route_ref[0, :, r0:r1] = _route(logits_t)


def _mix(x, norm1_g, norm2_g, mod3, w_b_cols, cos_t, sin_t, uc, k_rot, v_t, k_ctx, vt_ctx,
         w_conv, b_conv, w_a, w_b, w_o, sink_row, bias, wr_t, br_col, w_up, w_down):
    bsz, seq, _ = x.shape
    nt = seq // SEQ_TILE
    nblk = seq // BLOCK
    clen = k_ctx.shape[1]
    rows8 = SEQ_TILE // 8
    assert N_EXPERTS % (nt * bsz) == 0, "each grid step converts an equal share of the experts"
    e_step = N_EXPERTS // (nt * bsz)

    def mod_spec(j):
        return pl.BlockSpec((1, 1, D_MODEL), lambda t, b, j=j: (b, 0, j))

    def const_spec(shape):
        return pl.BlockSpec(shape, lambda t, b: tuple(0 for _ in shape))

    in_specs = [
        pl.BlockSpec((1, SEQ_TILE, D_MODEL), lambda t, b: (b, t, 0)),
        const_spec((1, D_MODEL)), const_spec((1, D_MODEL)),
        mod_spec(0), mod_spec(1), mod_spec(2), mod_spec(3), mod_spec(4),
        const_spec(w_b_cols.shape),
        pl.BlockSpec((SEQ_TILE, LANES), lambda t, b: (t, 0)),
        pl.BlockSpec((SEQ_TILE, LANES), lambda t, b: (t, 0)),
        pl.BlockSpec((1, SEQ_TILE, CONV_W), lambda t, b: (b, t, 0)),
        pl.BlockSpec((1, 8, CONV_W), lambda t, b: (b, jnp.maximum(t * rows8 - 1, 0), 0)),
        pl.BlockSpec((1, 8, CONV_W),
                     lambda t, b: (b, jnp.minimum((t + 1) * rows8, seq // 8 - 1), 0)),
        pl.BlockSpec((1, SEQ_TILE, KV_W), lambda t, b: (b, t, 0)),
        pl.BlockSpec((1, BLOCK, KV_W),
                     lambda t, b: (b, jnp.maximum(t * BLOCKS_PER_TILE - 1, 0), 0)),
        pl.BlockSpec((1, BLOCK, KV_W),
                     lambda t, b: (b, jnp.minimum((t + 1) * BLOCKS_PER_TILE, nblk - 1), 0)),
        pl.BlockSpec((1, KV_W, SEQ_TILE), lambda t, b: (b, 0, t)),
        pl.BlockSpec((1, KV_W, BLOCK),
                     lambda t, b: (b, 0, jnp.maximum(t * BLOCKS_PER_TILE - 1, 0))),
        pl.BlockSpec((1, KV_W, BLOCK),
                     lambda t, b: (b, 0, jnp.minimum((t + 1) * BLOCKS_PER_TILE, nblk - 1))),
        pl.BlockSpec((1, clen, KV_W), lambda t, b: (b, 0, 0)),
        pl.BlockSpec((1, KV_W, clen), lambda t, b: (b, 0, 0)),
        const_spec(w_conv.shape), const_spec(b_conv.shape),
        const_spec(w_a.shape), const_spec(w_b.shape), const_spec(w_o.shape),
        const_spec(sink_row.shape), const_spec(bias.shape),
        const_spec(wr_t.shape), const_spec(br_col.shape),
        pl.BlockSpec((e_step,) + w_up.shape[1:], lambda t, b: (t * bsz + b, 0, 0)),
        pl.BlockSpec((e_step,) + w_down.shape[1:], lambda t, b: (t * bsz + b, 0, 0)),
    ]
    out_specs = [
        pl.BlockSpec((1, SEQ_TILE, D_MODEL), lambda t, b: (b, t, 0)),
        pl.BlockSpec((1, SEQ_TILE, D_MODEL), lambda t, b: (b, t, 0)),
        pl.BlockSpec((1, ROUTE_ROWS, SEQ_TILE), lambda t, b: (b, 0, t)),
        pl.BlockSpec((CHUNK_STRIDE, D_MODEL), lambda t, b: (b * nt + t, 0)),
        pl.BlockSpec((e_step,) + w_up.shape[1:], lambda t, b: (t * bsz + b, 0, 0)),
        pl.BlockSpec((e_step,) + w_down.shape[1:], lambda t, b: (t * bsz + b, 0, 0)),
    ]
    out_shape = [
        jax.ShapeDtypeStruct((bsz, seq, D_MODEL), F32),
        jax.ShapeDtypeStruct((bsz, seq, D_MODEL), BF16),
        jax.ShapeDtypeStruct((bsz, ROUTE_ROWS, seq), F32),
        jax.ShapeDtypeStruct((bsz * nt * CHUNK_STRIDE, D_MODEL), BF16),
        jax.ShapeDtypeStruct(w_up.shape, BF16),
        jax.ShapeDtypeStruct(w_down.shape, BF16),
    ]
    return pl.pallas_call(
        _mix_kernel,
        grid=(nt, bsz),
        in_specs=in_specs,
        out_specs=out_specs,
        out_shape=out_shape,
        compiler_params=_cparams(("arbitrary", "arbitrary")),
        name="token_mix",
    )(x, norm1_g, norm2_g, mod3, mod3, mod3, mod3, mod3, w_b_cols, cos_t, sin_t,
      uc, uc, uc, k_rot, k_rot, k_rot, v_t, v_t, v_t, k_ctx, vt_ctx,
      w_conv, b_conv, w_a, w_b, w_o, sink_row, bias, wr_t, br_col, w_up, w_down)


def _bf16_parts(v):
    hi = v.astype(BF16).astype(F32)
    r1 = v - hi
    mid = r1.astype(BF16).astype(F32)
    lo = (r1 - mid).astype(BF16).astype(F32)
    return hi, mid, lo


def _dispatch_kernel(h_ref, route_ref, upper_ref, lower_ref, xl_ref, pos_ref, nb_ref):
    route = route_ref[0]
    e1 = route[0:1, :].astype(I32)
    e2 = route[1:2, :].astype(I32)
    erow = lax.broadcasted_iota(I32, (N_EXPERTS, CHUNK), 0)
    hit1 = erow == e1
    hit2 = erow == e2
    onehot = jnp.where(hit1, 1.0, 0.0) + jnp.where(hit2, 1.0, 0.0)
    cum = jnp.dot(onehot.astype(BF16), upper_ref[...], preferred_element_type=F32)
    cnt = jnp.sum(onehot, axis=1, keepdims=True)
    nblk = jnp.floor((cnt + (ROW_BLOCK - 1)) * (1.0 / ROW_BLOCK))
    nblk_b = jnp.broadcast_to(nblk, (N_EXPERTS, LANES))
    seg = jnp.dot(lower_ref[...], nblk_b.astype(BF16), preferred_element_type=F32) * ROW_BLOCK
    base = seg[:, 0:1] + cum
    pos1 = jnp.sum(jnp.where(hit1, base, 0.0), axis=0, keepdims=True)
    pos2 = jnp.sum(jnp.where(hit2, base, 0.0), axis=0, keepdims=True)
    p1i = pos1.astype(I32)
    p2i = pos2.astype(I32)
    used_rows = (jnp.sum(nblk) * ROW_BLOCK).astype(I32)

    prow = lax.broadcasted_iota(I32, (LANES, CHUNK), 0)
    parts = _bf16_parts(route[2:3, :]) + _bf16_parts(route[3:4, :])
    gpart_rows = jnp.zeros((LANES, CHUNK), F32)
    for j, part in enumerate(parts):
        gpart_rows = jnp.where(prow == j, part, gpart_rows)
    h_ext = jnp.concatenate([h_ref[...], gpart_rows.T.astype(BF16)], axis=1)
    glane = lax.broadcasted_iota(I32, (1, GATE_COLS), 1)
    one = jnp.ones((), BF16)
    zero = jnp.zeros((), BF16)
    r16 = lax.broadcasted_iota(I32, (DISP_ROWS, CHUNK), 0).astype(jnp.int16)

    def select(rc):
        q1 = (p1i - rc * DISP_ROWS).astype(jnp.int16)
        q2 = (p2i - rc * DISP_ROWS).astype(jnp.int16)
        sel1 = jnp.where(r16 == q1, one, zero)
        sel = jnp.where(r16 == q2, one, sel1)
        return sel, jnp.max(sel1, axis=1, keepdims=True)

    def permute(rc, sel, is_slot1):
        rows = pl.ds(rc * DISP_ROWS, DISP_ROWS)
        xg = jnp.dot(sel, h_ext, preferred_element_type=F32)
        xl_ref[rows, 0:D_MODEL] = xg[:, :D_MODEL].astype(BF16)
        g6 = xg[:, D_MODEL:]
        g3 = jnp.where(is_slot1.astype(F32) > 0.0, g6, pltpu.roll(g6, GATE_COLS - 3, axis=1))
        xl_ref[rows, D_MODEL:XL_COLS] = jnp.where(glane < 3, g3, 0.0).astype(BF16)

    n_always = 2 * CHUNK // DISP_ROWS + 1
    staged = select(0)
    for rc in range(n_always):
        nxt = select(rc + 1) if rc + 1 < n_always else None
        permute(rc, *staged)
        staged = nxt
    for rc in range(n_always, CHUNK_ROWS // DISP_ROWS):
        @pl.when(rc * DISP_ROWS < used_rows)
        def _(rc=rc):
            permute(rc, *select(rc))

        @pl.when(rc * DISP_ROWS >= used_rows)
        def _(rc=rc):
            xl_ref[pl.ds(rc * DISP_ROWS, DISP_ROWS), :] = jnp.zeros((DISP_ROWS, XL_COLS), BF16)
    xl_ref[CHUNK_ROWS:CHUNK_STRIDE, :] = jnp.zeros((CHUNK_STRIDE - CHUNK_ROWS, XL_COLS), BF16)

    pos_rows = jnp.where(prow == 0, pos1, jnp.where(prow == 1, pos2, 0.0))
    pos_ref[...] = pos_rows.T
    nb_ref[0] = nblk_b.astype(I32)


def _dispatch(h2, route, upper, lower):
    n = h2.shape[0]
    nch = n // CHUNK
    per_seq = route.shape[2] // CHUNK
    return pl.pallas_call(
        _dispatch_kernel,
        grid=(nch,),
        in_specs=[pl.BlockSpec((CHUNK, D_MODEL), lambda c: (c, 0)),
                  pl.BlockSpec((1, ROUTE_ROWS, CHUNK), lambda c: (c // per_seq, 0, c % per_seq)),
                  pl.BlockSpec(upper.shape, lambda c: (0, 0)),
                  pl.BlockSpec(lower.shape, lambda c: (0, 0))],
        out_specs=[pl.BlockSpec((CHUNK_STRIDE, XL_COLS), lambda c: (c, 0)),
                   pl.BlockSpec((CHUNK, LANES), lambda c: (c, 0)),
                   pl.BlockSpec((1, N_EXPERTS, LANES), lambda c: (c, 0, 0))],
        out_shape=[jax.ShapeDtypeStruct((nch * CHUNK_STRIDE, XL_COLS), BF16),
                   jax.ShapeDtypeStruct((n, LANES), F32),
                   jax.ShapeDtypeStruct((nch, N_EXPERTS, LANES), I32)],
        compiler_params=_cparams(("arbitrary",)),
        name="moe_dispatch",
    )(h2, route, upper, lower)


def _max_tiles(nch):
    max_blocks = nch * (2 * CHUNK + N_EXPERTS * (ROW_BLOCK - 1)) // ROW_BLOCK
    return max_blocks // TILE_BLOCKS + N_EXPERTS


def _masked_prefix(le, values):
    delta = values - jnp.concatenate([jnp.zeros((1,), values.dtype), values[:-1]])
    return jnp.sum(jnp.where(le, delta[None, :], 0), axis=1)


def _tile_plan(nb, nch):
    n_tiles = _max_tiles(nch)
    nbt = nb.T
    nbe = jnp.sum(nbt, axis=1)
    nte = (nbe + TILE_BLOCKS - 1) // TILE_BLOCKS
    tile_end = jnp.cumsum(nte)
    tile_start = tile_end - nte
    n_act = tile_end[-1]
    tiles = jnp.arange(n_tiles, dtype=I32)
    te = jnp.sum((tile_end[None, :] <= tiles[:, None]).astype(I32), axis=1)
    e_ar = jnp.arange(N_EXPERTS, dtype=I32)
    active = nte > 0
    te = jnp.where(tiles < n_act, te, jnp.max(jnp.where(active, e_ar, 0)))
    first = jnp.logical_and(te != jnp.concatenate([jnp.full((1,), -1, I32), te[:-1]]),
                            tiles < n_act).astype(I32)
    rank = jnp.cumsum(active.astype(I32)) - 1
    k_ar = jnp.arange(N_EXPERTS + W_SLOTS, dtype=I32)
    eseq = jnp.sum(jnp.where(jnp.logical_and(active[None, :], rank[None, :] == k_ar[:, None]),
                             e_ar[None, :], 0), axis=1)
    n_exp = jnp.sum(active.astype(I32))

    cb_excl = jnp.cumsum(nbt, axis=1) - nbt
    gs = TILE_BLOCKS * tile_start[:, None] + cb_excl
    seg_off = (ROW_BLOCK * (jnp.cumsum(nb, axis=1) - nb)).T
    base_row = jnp.arange(nch, dtype=I32)[None, :] * CHUNK_STRIDE + seg_off
    gs_f = gs.reshape(-1)
    slots = jnp.arange(n_tiles * TILE_BLOCKS, dtype=I32)
    le = gs_f[None, :] <= slots[:, None]
    row = ROW_BLOCK * slots + _masked_prefix(le, (base_row - ROW_BLOCK * gs).reshape(-1))
    valid = slots < _masked_prefix(le, (gs + nbt).reshape(-1))
    t = slots % (nch * TRASH_BLOCKS)
    trash = (t % nch) * CHUNK_STRIDE + CHUNK_ROWS + (t // nch) * ROW_BLOCK
    grow = (jnp.where(valid, row, 0) // ROW_BLOCK).astype(I32)
    srow = (jnp.where(valid, row, trash) // ROW_BLOCK).astype(I32)
    return (n_act.reshape(1).astype(I32), n_exp.reshape(1), te.astype(I32), first, eseq,
            grow, srow)


def _gmm_kernel(nact_ref, nexp_ref, te_ref, first_ref, eseq_ref, grow_ref, srow_ref,
                xl_hbm, yz_hbm, wu_hbm, wd_hbm, yl_hbm,
                xbuf, ybuf, wu_st, wd_st, in_sem, out_sem, w_sem):
    del te_ref, yz_hbm
    n_act = nact_ref[0]
    n_exp = nexp_ref[0]
    prefetch = X_SLOTS - 1

    def gather(tile, b):
        s = tile % X_SLOTS
        blk = grow_ref[tile * TILE_BLOCKS + b]
        return pltpu.make_async_copy(xl_hbm.at[blk], xbuf.at[s, b], in_sem.at[s])

    def scatter(tile, b):
        s = tile % Y_SLOTS
        blk = srow_ref[tile * TILE_BLOCKS + b]
        return pltpu.make_async_copy(ybuf.at[s, b], yl_hbm.at[blk], out_sem.at[s])

    def weights(q):
        e = eseq_ref[q]
        s = q % W_SLOTS
        return (pltpu.make_async_copy(wu_hbm.at[e], wu_st.at[s], w_sem.at[0, s]),
                pltpu.make_async_copy(wd_hbm.at[e], wd_st.at[s], w_sem.at[1, s]))

    def all_blocks(fn):
        for b in range(TILE_BLOCKS):
            fn(b)

    for q0 in range(W_SLOTS - 1):
        @pl.when(q0 < n_exp)
        def _(q0=q0):
            for cp in weights(q0):
                cp.start()
    for t0 in range(prefetch):
        @pl.when(t0 < n_act)
        def _(t0=t0):
            all_blocks(lambda b: gather(t0, b).start())

    def body(i, q):
        @pl.when(i + prefetch < n_act)
        def _():
            all_blocks(lambda b: gather(i + prefetch, b).start())

        all_blocks(lambda b: gather(i, b).wait())

        @pl.when(i >= Y_SLOTS)
        def _():
            all_blocks(lambda b: scatter(i - Y_SLOTS, b).wait())

        is_first = first_ref[i] == 1

        @pl.when(is_first)
        def _():
            for cp in weights(q):
                cp.wait()

            @pl.when(q + W_SLOTS - 1 < n_exp)
            def _():
                for cp in weights(q + W_SLOTS - 1):
                    cp.start()

        q = q + is_first.astype(I32)
        ws = (q - 1) % W_SLOTS
        xs = i % X_SLOTS
        ys = i % Y_SLOTS
        sub_blocks = SUB_ROWS // ROW_BLOCK
        for h in range(TILE_ROWS // SUB_ROWS):
            blocks = pl.ds(h * sub_blocks, sub_blocks)
            x = xbuf[xs, blocks].reshape(SUB_ROWS, XL_COLS)
            gate = jnp.sum(x[:, D_MODEL:].astype(F32), axis=1, keepdims=True)
            au = jnp.dot(x[:, :D_MODEL], wu_st[ws], preferred_element_type=F32)
            a = au[:, :EXPERT_FF]
            act = (a * _sigmoid(a)) * au[:, EXPERT_FF:]
            y = jnp.dot(act.astype(BF16), wd_st[ws], preferred_element_type=F32)
            ybuf[ys, blocks] = (gate * y).astype(BF16).reshape(sub_blocks, ROW_BLOCK, D_MODEL)

        all_blocks(lambda b: scatter(i, b).start())
        return q

    lax.fori_loop(0, n_act, body, jnp.int32(0))

    for k in range(Y_SLOTS):
        @pl.when(n_act - 1 - k >= 0)
        def _(k=k):
            all_blocks(lambda b: scatter(n_act - 1 - k, b).wait())


def _grouped_mlp(plan, xl, yz, w_up, w_down):
    grid_spec = pltpu.PrefetchScalarGridSpec(
        num_scalar_prefetch=len(plan),
        grid=(1,),
        in_specs=[pl.BlockSpec(memory_space=pl.ANY)] * 4,
        out_specs=pl.BlockSpec(memory_space=pl.ANY),
        scratch_shapes=[pltpu.VMEM((X_SLOTS, TILE_BLOCKS, ROW_BLOCK, XL_COLS), BF16),
                        pltpu.VMEM((Y_SLOTS, TILE_BLOCKS, ROW_BLOCK, D_MODEL), BF16),
                        pltpu.VMEM((W_SLOTS, D_MODEL, 2 * EXPERT_FF), BF16),
                        pltpu.VMEM((W_SLOTS, EXPERT_FF, D_MODEL), BF16),
                        pltpu.SemaphoreType.DMA((X_SLOTS,)),
                        pltpu.SemaphoreType.DMA((Y_SLOTS,)),
                        pltpu.SemaphoreType.DMA((2, W_SLOTS))])
    return pl.pallas_call(
        _gmm_kernel,
        grid_spec=grid_spec,
        out_shape=jax.ShapeDtypeStruct(yz.shape, yz.dtype),
        input_output_aliases={len(plan) + 1: 0},
        compiler_params=_cparams(("arbitrary",)),
        name="moe_experts",
    )(*plan, xl, yz, w_up, w_down)


def _combine_kernel(used_ref, yl_ref, pos_ref, x1_ref, gt2_ref, fg_ref, o_ref):
    used_rows = used_ref[pl.program_id(0)]
    half = CHUNK // 2

    def body(k_rows):
        sels = []
        for hf in range(2):
            pos = pos_ref[hf * half:(hf + 1) * half, :]
            p1 = pos[:, 0:1].astype(I32)
            p2 = pos[:, 1:2].astype(I32)
            r = lax.broadcasted_iota(I32, (half, k_rows), 1)
            sels.append(jnp.where(r == p1, 1.0, jnp.where(r == p2, 1.0, 0.0)).astype(BF16))
        for hf in range(2):
            rows = pl.ds(hf * half, half)
            y = jnp.dot(sels[hf], yl_ref[0:k_rows, :], preferred_element_type=F32)
            x2 = x1_ref[rows, :] + gt2_ref[0] * y
            inv = lax.rsqrt(jnp.mean(x2 * x2, axis=-1, keepdims=True) + NORM_EPS)
            o_ref[rows, :] = (x2 * inv) * fg_ref[...]

    k_common = 2 * CHUNK + DISP_ROWS
    pl.when(used_rows <= k_common)(lambda: body(k_common))
    pl.when(used_rows > k_common)(lambda: body(CHUNK_ROWS))


def _combine(used_rows, yl, pos_c, x1, mod3, final_g, seq):
    n = x1.shape[0]
    per_seq = seq // CHUNK
    grid_spec = pltpu.PrefetchScalarGridSpec(
        num_scalar_prefetch=1,
        grid=(n // CHUNK,),
        in_specs=[pl.BlockSpec((CHUNK_STRIDE, D_MODEL), lambda c, u: (c, 0)),
                  pl.BlockSpec((CHUNK, LANES), lambda c, u: (c, 0)),
                  pl.BlockSpec((CHUNK, D_MODEL), lambda c, u: (c, 0)),
                  pl.BlockSpec((1, 1, D_MODEL), lambda c, u: (c // per_seq, 0, 5)),
                  pl.BlockSpec((1, D_MODEL), lambda c, u: (0, 0))],
        out_specs=pl.BlockSpec((CHUNK, D_MODEL), lambda c, u: (c, 0)))
    return pl.pallas_call(
        _combine_kernel,
        grid_spec=grid_spec,
        out_shape=jax.ShapeDtypeStruct((n, D_MODEL), F32),
        compiler_params=_cparams(("arbitrary",)),
        name="moe_combine",
    )(used_rows, yl, pos_c, x1, mod3, final_g)


def _rope_tables(seq):
    n_freq = HEAD_DIM // 4
    inv_freq = ROPE_BASE ** (-jnp.arange(n_freq, dtype=F32) / n_freq)
    rows = seq // GRID_W
    row = jnp.repeat(jnp.arange(rows, dtype=F32), GRID_W)
    col = jnp.tile(jnp.arange(GRID_W, dtype=F32), rows)
    ang = jnp.concatenate([row[:, None] * inv_freq, col[:, None] * inv_freq], axis=-1)
    cos, sin = jnp.cos(ang), jnp.sin(ang)
    return jnp.tile(cos, (1, 4)), jnp.concatenate([-sin, sin, -sin, sin], axis=1)


def _window_bias():
    key = np.arange(3 * BLOCK)[:, None]
    qry = np.arange(BLOCK)[None, :]
    valid = (key - qry >= 0) & (key - qry <= 2 * BLOCK)
    return jnp.asarray(np.where(valid, 0.0, NEG_INF), F32)


def kernel(x, c, ctx, c_ctx, w_ada, b_ada, norm1_g, w_in, w_conv, b_conv, w_a, w_b, sink, w_o,
           norm2_g, w_group, b_group, w_router, b_router, w_up, w_down, final_g):
    bsz, seq, _ = x.shape
    assert w_ada.shape[0] == 1 and seq % SEQ_TILE == 0

    cc = jnp.zeros((16, D_MODEL), F32).at[:bsz].set(c).at[bsz].set(c_ctx)
    mod = _modulation(cc, w_ada[0], b_ada[0][None, :])
    mod3 = mod[:bsz].reshape(bsz, 1, N_MOD * D_MODEL)
    csh1 = mod[bsz:bsz + 1, 0:D_MODEL]
    csc1 = mod[bsz:bsz + 1, D_MODEL:2 * D_MODEL]

    w = w_in[0]
    n1g = norm1_g[0][None, :]
    n2g = norm2_g[0][None, :]
    w_kv = w[:, OFF_K:OFF_GA].astype(BF16)
    w_a_cols = jnp.concatenate([w[:, OFF_CG:OFF_Q], w[:, OFF_K:OFF_GA]], axis=1).astype(BF16)
    w_b_cols = jnp.concatenate([w[:, OFF_B:OFF_CG], w[:, OFF_Q:OFF_K], w[:, OFF_GA:]],
                               axis=1).astype(BF16)
    cos_t, sin_t = _rope_tables(seq)

    k_ctx, vt_ctx = _context_kv(ctx, n1g, csh1, csc1, w_kv)
    uc, k_rot, v_t = _proj_a(x, n1g, mod3, w_a_cols, cos_t, sin_t)

    sink_row = jnp.repeat(sink[0].astype(F32), BLOCK)[None, :]
    wr_t = jnp.zeros((LANES, D_MODEL), F32)
    wr_t = wr_t.at[0:N_GROUPS].set(w_group[0].T)
    wr_t = wr_t.at[GROUP_ROWS:GROUP_ROWS + N_EXPERTS].set(w_router[0].T).astype(BF16)
    br_col = jnp.zeros((LANES, 1), F32)
    br_col = br_col.at[0:N_GROUPS, 0].set(b_group[0])
    br_col = br_col.at[GROUP_ROWS:GROUP_ROWS + N_EXPERTS, 0].set(b_router[0])

    x1, h2, route, yz, w_up_bf, w_down_bf = _mix(
        x, n1g, n2g, mod3, w_b_cols, cos_t, sin_t, uc, k_rot, v_t, k_ctx, vt_ctx,
        w_conv[0], b_conv[0][None, :], w_a[0].astype(BF16), w_b[0].astype(BF16),
        w_o[0].astype(BF16), sink_row, _window_bias(), wr_t, br_col, w_up[0], w_down[0])

    n = bsz * seq
    nch = n // CHUNK
    upper = jnp.asarray(np.triu(np.ones((CHUNK, CHUNK), np.float32), 1), BF16)
    lower = jnp.asarray(np.tril(np.ones((N_EXPERTS, N_EXPERTS), np.float32), -1), BF16)
    xl, pos_c, nb = _dispatch(h2.reshape(n, D_MODEL), route, upper, lower)
    nb = nb[:, :, 0]
    yl = _grouped_mlp(_tile_plan(nb, nch), xl.reshape(-1, ROW_BLOCK, XL_COLS),
                      yz.reshape(-1, ROW_BLOCK, D_MODEL), w_up_bf, w_down_bf)
    yl = yl.reshape(-1, D_MODEL)
    used_rows = (ROW_BLOCK * jnp.sum(nb, axis=1)).astype(I32)
    out = _combine(used_rows, yl, pos_c, x1.reshape(n, D_MODEL), mod3, final_g[None, :], seq)
    return out.reshape(bsz, seq, D_MODEL)
```

```python
import numpy as np
import jax
import jax.numpy as jnp
from jax import lax
from jax.experimental import pallas as pl
from jax.experimental.pallas import tpu as pltpu

F32 = jnp.float32
BF16 = jnp.bfloat16
I32 = jnp.int32

D_MODEL = 1024
GRID_W = 64
CONV_W = 512
N_HEADS = 8
N_KV_HEADS = 2
HEAD_DIM = 64
ATT_W = N_HEADS * HEAD_DIM
KV_W = N_KV_HEADS * HEAD_DIM
BLOCK = 128
ROPE_BASE = 10000.0
N_GROUPS = 4
EXPERTS_PER_GROUP = 8
N_EXPERTS = N_GROUPS * EXPERTS_PER_GROUP
EXPERT_FF = 256
N_MOD = 6
NORM_EPS = 1e-6
NEG_INF = -1e30
LOG2E = 1.4426950408889634
ONES_ROWS = 16

OFF_B, OFF_CG, OFF_XIN, OFF_Q, OFF_K, OFF_V, OFF_GA, OFF_GB = (
    0, 512, 1024, 1536, 2048, 2176, 2304, 3328)
IN_COLS = 4352

LANES = 128
SEQ_TILE = 512
BLOCKS_PER_TILE = SEQ_TILE // BLOCK
ROUTE_ROWS = 8
GROUP_ROWS = 8
VMEM_LIMIT = 56 * 1024 * 1024

CHUNK = SEQ_TILE
ROW_BLOCK = 16
CHUNK_ROWS = -(-(2 * CHUNK + N_EXPERTS * (ROW_BLOCK - 1)) // 256) * 256
TRASH_BLOCKS = 4
CHUNK_STRIDE = CHUNK_ROWS + TRASH_BLOCKS * ROW_BLOCK
GATE_COLS = LANES
XL_COLS = D_MODEL + GATE_COLS
DISP_ROWS = 256
TILE_BLOCKS = 32
TILE_ROWS = TILE_BLOCKS * ROW_BLOCK
SUB_ROWS = 256
X_SLOTS = 4
Y_SLOTS = 4
W_SLOTS = 3


def _cparams(sem):
    return pltpu.CompilerParams(dimension_semantics=sem, vmem_limit_bytes=VMEM_LIMIT)


def _rms_mod(x, g, shift, scale):
    inv = lax.rsqrt(jnp.mean(x * x, axis=-1, keepdims=True) + NORM_EPS)
    return (x * inv) * (g * (1.0 + scale)) + shift


def _mod_kernel(c_ref, w_ref, b_ref, o_ref):
    c = c_ref[...]
    a = (c * jax.nn.sigmoid(c)).astype(BF16)
    o_ref[...] = jnp.dot(a, w_ref[...].astype(BF16), preferred_element_type=F32) + b_ref[...]


def _modulation(cc, w_ada, b_ada):
    rows = cc.shape[0]
    cols = w_ada.shape[1]
    tile = 1024
    return pl.pallas_call(
        _mod_kernel,
        grid=(cols // tile,),
        in_specs=[pl.BlockSpec((rows, D_MODEL), lambda j: (0, 0)),
                  pl.BlockSpec((D_MODEL, tile), lambda j: (0, j)),
                  pl.BlockSpec((1, tile), lambda j: (0, j))],
        out_specs=pl.BlockSpec((rows, tile), lambda j: (0, j)),
        out_shape=jax.ShapeDtypeStruct((rows, cols), F32),
        compiler_params=_cparams(("arbitrary",)),
        name="adaln_mod",
    )(cc, w_ada, b_ada)


def _ctx_kernel(x_ref, g_ref, sh_ref, sc_ref, w_ref, k_ref, vt_ref):
    h = _rms_mod(x_ref[0], g_ref[...], sh_ref[...], sc_ref[...]).astype(BF16)
    kv = jnp.dot(h, w_ref[...], preferred_element_type=F32)
    k_ref[0] = kv[:, :KV_W].astype(BF16)
    vt_ref[0] = kv[:, KV_W:].T.astype(BF16)


def _context_kv(ctx, norm_g, csh, csc, w_kv):
    bsz, clen, _ = ctx.shape
    return pl.pallas_call(
        _ctx_kernel,
        grid=(bsz,),
        in_specs=[pl.BlockSpec((1, clen, D_MODEL), lambda b: (b, 0, 0)),
                  pl.BlockSpec((1, D_MODEL), lambda b: (0, 0)),
                  pl.BlockSpec((1, D_MODEL), lambda b: (0, 0)),
                  pl.BlockSpec((1, D_MODEL), lambda b: (0, 0)),
                  pl.BlockSpec((D_MODEL, 2 * KV_W), lambda b: (0, 0))],
        out_specs=[pl.BlockSpec((1, clen, KV_W), lambda b: (b, 0, 0)),
                   pl.BlockSpec((1, KV_W, clen), lambda b: (b, 0, 0))],
        out_shape=[jax.ShapeDtypeStruct((bsz, clen, KV_W), BF16),
                   jax.ShapeDtypeStruct((bsz, KV_W, clen), BF16)],
        compiler_params=_cparams(("arbitrary",)),
        name="context_kv",
    )(ctx, norm_g, csh, csc, w_kv)


def _rope(t, cos, sin_signed):
    lane = lax.broadcasted_iota(I32, (1, LANES), 1)
    first_half = (lane % HEAD_DIM) < (HEAD_DIM // 2)
    outs = []
    for j in range(t.shape[1] // LANES):
        tj = t[:, j * LANES:(j + 1) * LANES]
        partner = jnp.where(first_half,
                            pltpu.roll(tj, LANES - HEAD_DIM // 2, axis=1),
                            pltpu.roll(tj, HEAD_DIM // 2, axis=1))
        outs.append(tj * cos + partner * sin_signed)
    return outs[0] if len(outs) == 1 else jnp.concatenate(outs, axis=1)


def _proj_a_kernel(x_ref, g_ref, sh_ref, sc_ref, w_ref, cos_ref, sin_ref, uc_ref, k_ref, vt_ref):
    half = SEQ_TILE // 2
    halves = [slice(hf * half, (hf + 1) * half) for hf in range(2)]
    hs = [_rms_mod(x_ref[0, rows, :], g_ref[...], sh_ref[0], sc_ref[0]).astype(BF16)
          for rows in halves]
    for h, rows in zip(hs, halves):
        u = jnp.dot(h, w_ref[...], preferred_element_type=F32)
        uc_ref[0, rows, :] = u[:, :CONV_W] * u[:, CONV_W:2 * CONV_W]
        k = u[:, 2 * CONV_W:2 * CONV_W + KV_W]
        k_ref[0, rows, :] = _rope(k, cos_ref[rows, :], sin_ref[rows, :]).astype(BF16)
        vt_ref[0, :, rows] = u[:, 2 * CONV_W + KV_W:].T.astype(BF16)


def _proj_a(x, norm_g, mod3, w_a_cols, cos_t, sin_t):
    bsz, seq, _ = x.shape
    nt = seq // SEQ_TILE
    wcols = w_a_cols.shape[1]
    return pl.pallas_call(
        _proj_a_kernel,
        grid=(nt, bsz),
        in_specs=[pl.BlockSpec((1, SEQ_TILE, D_MODEL), lambda t, b: (b, t, 0)),
                  pl.BlockSpec((1, D_MODEL), lambda t, b: (0, 0)),
                  pl.BlockSpec((1, 1, D_MODEL), lambda t, b: (b, 0, 0)),
                  pl.BlockSpec((1, 1, D_MODEL), lambda t, b: (b, 0, 1)),
                  pl.BlockSpec((D_MODEL, wcols), lambda t, b: (0, 0)),
                  pl.BlockSpec((SEQ_TILE, LANES), lambda t, b: (t, 0)),
                  pl.BlockSpec((SEQ_TILE, LANES), lambda t, b: (t, 0))],
        out_specs=[pl.BlockSpec((1, SEQ_TILE, CONV_W), lambda t, b: (b, t, 0)),
                   pl.BlockSpec((1, SEQ_TILE, KV_W), lambda t, b: (b, t, 0)),
                   pl.BlockSpec((1, KV_W, SEQ_TILE), lambda t, b: (b, 0, t))],
        out_shape=[jax.ShapeDtypeStruct((bsz, seq, CONV_W), F32),
                   jax.ShapeDtypeStruct((bsz, seq, KV_W), BF16),
                   jax.ShapeDtypeStruct((bsz, KV_W, seq), BF16)],
        compiler_params=_cparams(("arbitrary", "arbitrary")),
        name="proj_a",
    )(x, norm_g, mod3, mod3, w_a_cols, cos_t, sin_t)


def _sigmoid(x):
    return 1.0 / (1.0 + jnp.exp2(x * (-LOG2E)))


def _attn_scores(q_t, g, k_win, k_ctx, bias_p4, bias_n4):
    zeros = jnp.zeros((HEAD_DIM, 4 * BLOCK), BF16)
    qg = jnp.concatenate([q_t[(4 * g + h) * HEAD_DIM:(4 * g + h + 1) * HEAD_DIM, :]
                          for h in range(4)], axis=1)
    rhs = jnp.concatenate([qg, zeros] if g == 0 else [zeros, qg], axis=0)
    s_win = jnp.dot(k_win, rhs, preferred_element_type=F32)
    s_ctx = jnp.dot(k_ctx, rhs, preferred_element_type=F32)
    return (s_win[0:BLOCK] + bias_p4, s_win[BLOCK:2 * BLOCK], s_win[2 * BLOCK:] + bias_n4, s_ctx)


def _attn_probs(scores, sink):
    m = sink
    for s in scores:
        m = jnp.maximum(m, jnp.max(s, axis=0, keepdims=True))
    p_win = jnp.concatenate([jnp.exp2(s - m).astype(BF16) for s in scores[:3]], axis=0)
    p_ctx = jnp.exp2(scores[3] - m).astype(BF16)
    return p_win, p_ctx, m


def _attn_values(probs, sink, vt_win_g, vt_ctx_g):
    p_win, p_ctx, m = probs

    def with_ones(vt):
        r = lax.broadcasted_iota(I32, (ONES_ROWS, vt.shape[1]), 0)
        return jnp.concatenate([vt, jnp.where(r == 0, 1.0, 0.0).astype(BF16)], axis=0)

    o_ext = (jnp.dot(with_ones(vt_win_g), p_win, preferred_element_type=F32)
             + jnp.dot(with_ones(vt_ctx_g), p_ctx, preferred_element_type=F32))
    denom = o_ext[HEAD_DIM:HEAD_DIM + 1, :] + jnp.exp2(sink - m)
    return o_ext[:HEAD_DIM, :] / denom


def _route(logits_t):
    t = logits_t.shape[1]
    grow = lax.broadcasted_iota(I32, (GROUP_ROWS, t), 0)
    gl = jnp.where(grow < N_GROUPS, logits_t[0:GROUP_ROWS, :], NEG_INF)
    gm = jnp.max(gl, axis=0, keepdims=True)
    p_g = 1.0 / jnp.sum(jnp.exp(gl - gm), axis=0, keepdims=True)
    g_idx = jnp.min(jnp.where(gl == gm, grow, N_GROUPS), axis=0, keepdims=True)

    erow = lax.broadcasted_iota(I32, (N_EXPERTS, t), 0)
    el = logits_t[GROUP_ROWS:GROUP_ROWS + N_EXPERTS, :]
    sel = (erow // EXPERTS_PER_GROUP) == g_idx
    em = jnp.where(sel, el, NEG_INF)
    m1 = jnp.max(em, axis=0, keepdims=True)
    i1 = jnp.min(jnp.where(em == m1, erow, N_EXPERTS), axis=0, keepdims=True)
    em2 = jnp.where(erow == i1, NEG_INF, em)
    m2 = jnp.max(em2, axis=0, keepdims=True)
    i2 = jnp.min(jnp.where(em2 == m2, erow, N_EXPERTS), axis=0, keepdims=True)
    z = jnp.sum(jnp.where(sel, jnp.exp(el - m1), 0.0), axis=0, keepdims=True)
    p1 = 1.0 / z
    p2 = jnp.exp(m2 - m1) / z
    gate1 = p_g * p1 / (p1 + p2)
    gate2 = p_g * p2 / (p1 + p2)
    pad = jnp.zeros((ROUTE_ROWS - 4, t), F32)
    return jnp.concatenate([i1.astype(F32), i2.astype(F32), gate1, gate2, pad], axis=0)


def _mix_kernel(x_ref, g1n_ref, g2n_ref, sh1_ref, sc1_ref, gt1_ref, sh2_ref, sc2_ref,
                wb_ref, cos_ref, sin_ref,
                uc_ref, ucp_ref, ucn_ref, k_ref, kp_ref, kn_ref, vt_ref, vtp_ref, vtn_ref,
                kc_ref, vtc_ref, wconv_ref, bconv_ref, wa_ref, wbb_ref, wo_ref,
                sink_ref, bias_ref, wr_ref, br_ref, wu_ref, wd_ref,
                x1_ref, h2_ref, route_ref, yz_ref, wub_ref, wdb_ref):
    t_idx = pl.program_id(0)
    n_tiles = pl.num_programs(0)
    is_first = t_idx == 0
    is_last = t_idx == n_tiles - 1

    x = x_ref[0]
    half = SEQ_TILE // 2
    halves = [slice(hf * half, (hf + 1) * half) for hf in range(2)]

    uc = uc_ref[0]
    row = lax.broadcasted_iota(I32, (SEQ_TILE, 1), 0)
    prev_row = jnp.where(is_first, 0.0, ucp_ref[0][7:8, :])
    next_row = jnp.where(is_last, 0.0, ucn_ref[0][0:1, :])
    up = jnp.where(row == 0, prev_row, pltpu.roll(uc, 1, axis=0))
    dn = jnp.where(row == SEQ_TILE - 1, next_row, pltpu.roll(uc, SEQ_TILE - 1, axis=0))
    wconv = wconv_ref[...]
    y = bconv_ref[...] + (up * wconv[0:1, :] + uc * wconv[1:2, :] + dn * wconv[2:3, :])

    hbs = [_rms_mod(x[rows], g1n_ref[...], sh1_ref[0], sc1_ref[0]).astype(BF16) for rows in halves]
    yas, qs = [], []
    for hf, rows in enumerate(halves):
        bq = jnp.dot(hbs[hf], wb_ref[:, 0:2 * CONV_W], preferred_element_type=F32)
        yas.append(jnp.dot((bq[:, :CONV_W] * y[rows]).astype(BF16), wa_ref[...],
                           preferred_element_type=F32))
        qs.append(_rope(bq[:, CONV_W:], cos_ref[rows, :], sin_ref[rows, :])
                  * (HEAD_DIM ** -0.5 * LOG2E))
    hb = jnp.concatenate(hbs, axis=0)

    k_all = jnp.concatenate([kp_ref[0], k_ref[0], kn_ref[0]], axis=0)
    vt_all = jnp.concatenate([vtp_ref[0], vt_ref[0], vtn_ref[0]], axis=1)
    k_ctx = kc_ref[0]
    vt_ctx = vtc_ref[0]
    bias_prev = bias_ref[0:BLOCK, :]
    bias_next = bias_ref[2 * BLOCK:3 * BLOCK, :]
    sink_row = sink_ref[...] * LOG2E
    n_units = BLOCKS_PER_TILE * N_KV_HEADS
    gate_cols = 2 * D_MODEL // n_units
    gate_chunks = []

    def gate_chunk(u):
        c0 = 2 * CONV_W + u * gate_cols
        gate_chunks.append(jnp.dot(hb, wb_ref[:, c0:c0 + gate_cols], preferred_element_type=F32))

    q_ts, biases = [], []
    for jb in range(BLOCKS_PER_TILE):
        bias_p, bias_n = bias_prev, bias_next
        if jb == 0:
            bias_p = bias_prev + jnp.where(is_first, NEG_INF, 0.0)
        if jb == BLOCKS_PER_TILE - 1:
            bias_n = bias_next + jnp.where(is_last, NEG_INF, 0.0)
        biases.append((jnp.concatenate([bias_p] * 4, axis=1), jnp.concatenate([bias_n] * 4, axis=1)))
        hf, jh = divmod(jb, BLOCKS_PER_TILE // 2)
        q_ts.append(qs[hf][jh * BLOCK:(jh + 1) * BLOCK, :].T.astype(BF16))

    scores, probs, outs = {}, {}, {}
    for t in range(n_units + 2):
        if t == 1:
            yz_ref[...] = jnp.zeros_like(yz_ref)
        if t == 3:
            wub_ref[...] = wu_ref[...].astype(BF16)
        if t == 5:
            wdb_ref[...] = wd_ref[...].astype(BF16)
        if t < n_units:
            jb, g = divmod(t, N_KV_HEADS)
            scores[t] = _attn_scores(q_ts[jb], g, k_all[jb * BLOCK:(jb + 3) * BLOCK, :], k_ctx,
                                     *biases[jb])
            gate_chunk(t)
        if 0 <= t - 1 < n_units:
            g = (t - 1) % N_KV_HEADS
            probs[t - 1] = _attn_probs(scores.pop(t - 1),
                                       sink_row[:, g * 4 * BLOCK:(g + 1) * 4 * BLOCK])
        if 0 <= t - 2 < n_units:
            jb, g = divmod(t - 2, N_KV_HEADS)
            outs[t - 2] = _attn_values(
                probs.pop(t - 2), sink_row[:, g * 4 * BLOCK:(g + 1) * 4 * BLOCK],
                vt_all[g * HEAD_DIM:(g + 1) * HEAD_DIM, jb * BLOCK:(jb + 3) * BLOCK],
                vt_ctx[g * HEAD_DIM:(g + 1) * HEAD_DIM, :])
    o_blocks = []
    for jb in range(BLOCKS_PER_TILE):
        o_rows = [outs[jb * N_KV_HEADS + g][:, h * BLOCK:(h + 1) * BLOCK]
                  for g in range(N_KV_HEADS) for h in range(4)]
        o_blocks.append(jnp.concatenate(o_rows, axis=0).T)
    gates = jnp.concatenate(gate_chunks, axis=1)

    half_blocks = BLOCKS_PER_TILE // 2
    for hf in range(2):
        r0, r1 = hf * half_blocks * BLOCK, (hf + 1) * half_blocks * BLOCK
        o = jnp.concatenate(o_blocks[hf * half_blocks:(hf + 1) * half_blocks], axis=0).astype(BF16)
        yb = jnp.dot(o, wbb_ref[...], preferred_element_type=F32)
        merged = (_sigmoid(gates[r0:r1, :D_MODEL]) * yas[hf]
                  + _sigmoid(gates[r0:r1, D_MODEL:]) * yb).astype(BF16)
        x1 = x[r0:r1] + gt1_ref[0] * jnp.dot(merged, wo_ref[...], preferred_element_type=F32)
        x1_ref[0, r0:r1, :] = x1
        h2 = _rms_mod(x1, g2n_ref[...], sh2_ref[0], sc2_ref[0]).astype(BF16)
        h2_ref[0, r0:r1, :] = h2
        logits_t = lax.dot_general(wr_ref[...], h2, (((1,), (1,)), ((), ())),
                                   preferred_element_type=F32) + br_ref[...]
        route_ref[0, :, r0:r1] = _route(logits_t)


def _mix(x, norm1_g, norm2_g, mod3, w_b_cols, cos_t, sin_t, uc, k_rot, v_t, k_ctx, vt_ctx,
         w_conv, b_conv, w_a, w_b, w_o, sink_row, bias, wr_t, br_col, w_up, w_down):
    bsz, seq, _ = x.shape
    nt = seq // SEQ_TILE
    nblk = seq // BLOCK
    clen = k_ctx.shape[1]
    rows8 = SEQ_TILE // 8
    assert N_EXPERTS % (nt * bsz) == 0, "each grid step converts an equal share of the experts"
    e_step = N_EXPERTS // (nt * bsz)

    def mod_spec(j):
        return pl.BlockSpec((1, 1, D_MODEL), lambda t, b, j=j: (b, 0, j))

    def const_spec(shape):
        return pl.BlockSpec(shape, lambda t, b: tuple(0 for _ in shape))

    in_specs = [
        pl.BlockSpec((1, SEQ_TILE, D_MODEL), lambda t, b: (b, t, 0)),
        const_spec((1, D_MODEL)), const_spec((1, D_MODEL)),
        mod_spec(0), mod_spec(1), mod_spec(2), mod_spec(3), mod_spec(4),
        const_spec(w_b_cols.shape),
        pl.BlockSpec((SEQ_TILE, LANES), lambda t, b: (t, 0)),
        pl.BlockSpec((SEQ_TILE, LANES), lambda t, b: (t, 0)),
        pl.BlockSpec((1, SEQ_TILE, CONV_W), lambda t, b: (b, t, 0)),
        pl.BlockSpec((1, 8, CONV_W), lambda t, b: (b, jnp.maximum(t * rows8 - 1, 0), 0)),
        pl.BlockSpec((1, 8, CONV_W),
                     lambda t, b: (b, jnp.minimum((t + 1) * rows8, seq // 8 - 1), 0)),
        pl.BlockSpec((1, SEQ_TILE, KV_W), lambda t, b: (b, t, 0)),
        pl.BlockSpec((1, BLOCK, KV_W),
                     lambda t, b: (b, jnp.maximum(t * BLOCKS_PER_TILE - 1, 0), 0)),
        pl.BlockSpec((1, BLOCK, KV_W),
                     lambda t, b: (b, jnp.minimum((t + 1) * BLOCKS_PER_TILE, nblk - 1), 0)),
        pl.BlockSpec((1, KV_W, SEQ_TILE), lambda t, b: (b, 0, t)),
        pl.BlockSpec((1, KV_W, BLOCK),
                     lambda t, b: (b, 0, jnp.maximum(t * BLOCKS_PER_TILE - 1, 0))),
        pl.BlockSpec((1, KV_W, BLOCK),
                     lambda t, b: (b, 0, jnp.minimum((t + 1) * BLOCKS_PER_TILE, nblk - 1))),
        pl.BlockSpec((1, clen, KV_W), lambda t, b: (b, 0, 0)),
        pl.BlockSpec((1, KV_W, clen), lambda t, b: (b, 0, 0)),
        const_spec(w_conv.shape), const_spec(b_conv.shape),
        const_spec(w_a.shape), const_spec(w_b.shape), const_spec(w_o.shape),
        const_spec(sink_row.shape), const_spec(bias.shape),
        const_spec(wr_t.shape), const_spec(br_col.shape),
        pl.BlockSpec((e_step,) + w_up.shape[1:], lambda t, b: (t * bsz + b, 0, 0)),
        pl.BlockSpec((e_step,) + w_down.shape[1:], lambda t, b: (t * bsz + b, 0, 0)),
    ]
    out_specs = [
        pl.BlockSpec((1, SEQ_TILE, D_MODEL), lambda t, b: (b, t, 0)),
        pl.BlockSpec((1, SEQ_TILE, D_MODEL), lambda t, b: (b, t, 0)),
        pl.BlockSpec((1, ROUTE_ROWS, SEQ_TILE), lambda t, b: (b, 0, t)),
        pl.BlockSpec((CHUNK_STRIDE, D_MODEL), lambda t, b: (b * nt + t, 0)),
        pl.BlockSpec((e_step,) + w_up.shape[1:], lambda t, b: (t * bsz + b, 0, 0)),
        pl.BlockSpec((e_step,) + w_down.shape[1:], lambda t, b: (t * bsz + b, 0, 0)),
    ]
    out_shape = [
        jax.ShapeDtypeStruct((bsz, seq, D_MODEL), F32),
        jax.ShapeDtypeStruct((bsz, seq, D_MODEL), BF16),
        jax.ShapeDtypeStruct((bsz, ROUTE_ROWS, seq), F32),
        jax.ShapeDtypeStruct((bsz * nt * CHUNK_STRIDE, D_MODEL), BF16),
        jax.ShapeDtypeStruct(w_up.shape, BF16),
        jax.ShapeDtypeStruct(w_down.shape, BF16),
    ]
    return pl.pallas_call(
        _mix_kernel,
        grid=(nt, bsz),
        in_specs=in_specs,
        out_specs=out_specs,
        out_shape=out_shape,
        compiler_params=_cparams(("arbitrary", "arbitrary")),
        name="token_mix",
    )(x, norm1_g, norm2_g, mod3, mod3, mod3, mod3, mod3, w_b_cols, cos_t, sin_t,
      uc, uc, uc, k_rot, k_rot, k_rot, v_t, v_t, v_t, k_ctx, vt_ctx,
      w_conv, b_conv, w_a, w_b, w_o, sink_row, bias, wr_t, br_col, w_up, w_down)


def _bf16_parts(v):
    hi = v.astype(BF16).astype(F32)
    r1 = v - hi
    mid = r1.astype(BF16).astype(F32)
    lo = (r1 - mid).astype(BF16).astype(F32)
    return hi, mid, lo


def _dispatch_kernel(h_ref, route_ref, upper_ref, lower_ref, xl_ref, pos_ref, nb_ref):
    route = route_ref[0]
    e1 = route[0:1, :].astype(I32)
    e2 = route[1:2, :].astype(I32)
    erow = lax.broadcasted_iota(I32, (N_EXPERTS, CHUNK), 0)
    hit1 = erow == e1
    hit2 = erow == e2
    onehot = jnp.where(hit1, 1.0, 0.0) + jnp.where(hit2, 1.0, 0.0)
    cum = jnp.dot(onehot.astype(BF16), upper_ref[...], preferred_element_type=F32)
    cnt = jnp.sum(onehot, axis=1, keepdims=True)
    nblk = jnp.floor((cnt + (ROW_BLOCK - 1)) * (1.0 / ROW_BLOCK))
    nblk_b = jnp.broadcast_to(nblk, (N_EXPERTS, LANES))
    seg = jnp.dot(lower_ref[...], nblk_b.astype(BF16), preferred_element_type=F32) * ROW_BLOCK
    base = seg[:, 0:1] + cum
    pos1 = jnp.sum(jnp.where(hit1, base, 0.0), axis=0, keepdims=True)
    pos2 = jnp.sum(jnp.where(hit2, base, 0.0), axis=0, keepdims=True)
    p1i = pos1.astype(I32)
    p2i = pos2.astype(I32)
    used_rows = (jnp.sum(nblk) * ROW_BLOCK).astype(I32)

    prow = lax.broadcasted_iota(I32, (LANES, CHUNK), 0)
    parts = _bf16_parts(route[2:3, :]) + _bf16_parts(route[3:4, :])
    gpart_rows = jnp.zeros((LANES, CHUNK), F32)
    for j, part in enumerate(parts):
        gpart_rows = jnp.where(prow == j, part, gpart_rows)
    h_ext = jnp.concatenate([h_ref[...], gpart_rows.T.astype(BF16)], axis=1)
    glane = lax.broadcasted_iota(I32, (1, GATE_COLS), 1)
    one = jnp.ones((), BF16)
    zero = jnp.zeros((), BF16)
    r16 = lax.broadcasted_iota(I32, (DISP_ROWS, CHUNK), 0).astype(jnp.int16)

    def select(rc):
        q1 = (p1i - rc * DISP_ROWS).astype(jnp.int16)
        q2 = (p2i - rc * DISP_ROWS).astype(jnp.int16)
        sel1 = jnp.where(r16 == q1, one, zero)
        sel = jnp.where(r16 == q2, one, sel1)
        return sel, jnp.max(sel1, axis=1, keepdims=True)

    def permute(rc, sel, is_slot1):
        rows = pl.ds(rc * DISP_ROWS, DISP_ROWS)
        xg = jnp.dot(sel, h_ext, preferred_element_type=F32)
        xl_ref[rows, 0:D_MODEL] = xg[:, :D_MODEL].astype(BF16)
        g6 = xg[:, D_MODEL:]
        g3 = jnp.where(is_slot1.astype(F32) > 0.0, g6, pltpu.roll(g6, GATE_COLS - 3, axis=1))
        xl_ref[rows, D_MODEL:XL_COLS] = jnp.where(glane < 3, g3, 0.0).astype(BF16)

    n_always = 2 * CHUNK // DISP_ROWS + 1
    staged = select(0)
    for rc in range(n_always):
        nxt = select(rc + 1) if rc + 1 < n_always else None
        permute(rc, *staged)
        staged = nxt
    for rc in range(n_always, CHUNK_ROWS // DISP_ROWS):
        @pl.when(rc * DISP_ROWS < used_rows)
        def _(rc=rc):
            permute(rc, *select(rc))

        @pl.when(rc * DISP_ROWS >= used_rows)
        def _(rc=rc):
            xl_ref[pl.ds(rc * DISP_ROWS, DISP_ROWS), :] = jnp.zeros((DISP_ROWS, XL_COLS), BF16)
    xl_ref[CHUNK_ROWS:CHUNK_STRIDE, :] = jnp.zeros((CHUNK_STRIDE - CHUNK_ROWS, XL_COLS), BF16)

    pos_rows = jnp.where(prow == 0, pos1, jnp.where(prow == 1, pos2, 0.0))
    pos_ref[...] = pos_rows.T
    nb_ref[0] = nblk_b.astype(I32)


def _dispatch(h2, route, upper, lower):
    n = h2.shape[0]
    nch = n // CHUNK
    per_seq = route.shape[2] // CHUNK
    return pl.pallas_call(
        _dispatch_kernel,
        grid=(nch,),
        in_specs=[pl.BlockSpec((CHUNK, D_MODEL), lambda c: (c, 0)),
                  pl.BlockSpec((1, ROUTE_ROWS, CHUNK), lambda c: (c // per_seq, 0, c % per_seq)),
                  pl.BlockSpec(upper.shape, lambda c: (0, 0)),
                  pl.BlockSpec(lower.shape, lambda c: (0, 0))],
        out_specs=[pl.BlockSpec((CHUNK_STRIDE, XL_COLS), lambda c: (c, 0)),
                   pl.BlockSpec((CHUNK, LANES), lambda c: (c, 0)),
                   pl.BlockSpec((1, N_EXPERTS, LANES), lambda c: (c, 0, 0))],
        out_shape=[jax.ShapeDtypeStruct((nch * CHUNK_STRIDE, XL_COLS), BF16),
                   jax.ShapeDtypeStruct((n, LANES), F32),
                   jax.ShapeDtypeStruct((nch, N_EXPERTS, LANES), I32)],
        compiler_params=_cparams(("arbitrary",)),
        name="moe_dispatch",
    )(h2, route, upper, lower)


def _max_tiles(nch):
    max_blocks = nch * (2 * CHUNK + N_EXPERTS * (ROW_BLOCK - 1)) // ROW_BLOCK
    return max_blocks // TILE_BLOCKS + N_EXPERTS


def _masked_prefix(le, values):
    delta = values - jnp.concatenate([jnp.zeros((1,), values.dtype), values[:-1]])
    return jnp.sum(jnp.where(le, delta[None, :], 0), axis=1)


def _tile_plan(nb, nch):
    n_tiles = _max_tiles(nch)
    nbt = nb.T
    nbe = jnp.sum(nbt, axis=1)
    nte = (nbe + TILE_BLOCKS - 1) // TILE_BLOCKS
    tile_end = jnp.cumsum(nte)
    tile_start = tile_end - nte
    n_act = tile_end[-1]
    tiles = jnp.arange(n_tiles, dtype=I32)
    te = jnp.sum((tile_end[None, :] <= tiles[:, None]).astype(I32), axis=1)
    e_ar = jnp.arange(N_EXPERTS, dtype=I32)
    active = nte > 0
    te = jnp.where(tiles < n_act, te, jnp.max(jnp.where(active, e_ar, 0)))
    first = jnp.logical_and(te != jnp.concatenate([jnp.full((1,), -1, I32), te[:-1]]),
                            tiles < n_act).astype(I32)
    rank = jnp.cumsum(active.astype(I32)) - 1
    k_ar = jnp.arange(N_EXPERTS + W_SLOTS, dtype=I32)
    eseq = jnp.sum(jnp.where(jnp.logical_and(active[None, :], rank[None, :] == k_ar[:, None]),
                             e_ar[None, :], 0), axis=1)
    n_exp = jnp.sum(active.astype(I32))

    cb_excl = jnp.cumsum(nbt, axis=1) - nbt
    gs = TILE_BLOCKS * tile_start[:, None] + cb_excl
    seg_off = (ROW_BLOCK * (jnp.cumsum(nb, axis=1) - nb)).T
    base_row = jnp.arange(nch, dtype=I32)[None, :] * CHUNK_STRIDE + seg_off
    gs_f = gs.reshape(-1)
    slots = jnp.arange(n_tiles * TILE_BLOCKS, dtype=I32)
    le = gs_f[None, :] <= slots[:, None]
    row = ROW_BLOCK * slots + _masked_prefix(le, (base_row - ROW_BLOCK * gs).reshape(-1))
    valid = slots < _masked_prefix(le, (gs + nbt).reshape(-1))
    t = slots % (nch * TRASH_BLOCKS)
    trash = (t % nch) * CHUNK_STRIDE + CHUNK_ROWS + (t // nch) * ROW_BLOCK
    grow = (jnp.where(valid, row, 0) // ROW_BLOCK).astype(I32)
    srow = (jnp.where(valid, row, trash) // ROW_BLOCK).astype(I32)
    return (n_act.reshape(1).astype(I32), n_exp.reshape(1), te.astype(I32), first, eseq,
            grow, srow)


def _gmm_kernel(nact_ref, nexp_ref, te_ref, first_ref, eseq_ref, grow_ref, srow_ref,
                xl_hbm, yz_hbm, wu_hbm, wd_hbm, yl_hbm,
                xbuf, ybuf, wu_st, wd_st, in_sem, out_sem, w_sem):
    del te_ref, yz_hbm
    n_act = nact_ref[0]
    n_exp = nexp_ref[0]
    prefetch = X_SLOTS - 1

    def gather(tile, b):
        s = tile % X_SLOTS
        blk = grow_ref[tile * TILE_BLOCKS + b]
        return pltpu.make_async_copy(xl_hbm.at[blk], xbuf.at[s, b], in_sem.at[s])

    def scatter(tile, b):
        s = tile % Y_SLOTS
        blk = srow_ref[tile * TILE_BLOCKS + b]
        return pltpu.make_async_copy(ybuf.at[s, b], yl_hbm.at[blk], out_sem.at[s])

    def weights(q):
        e = eseq_ref[q]
        s = q % W_SLOTS
        return (pltpu.make_async_copy(wu_hbm.at[e], wu_st.at[s], w_sem.at[0, s]),
                pltpu.make_async_copy(wd_hbm.at[e], wd_st.at[s], w_sem.at[1, s]))

    def all_blocks(fn):
        for b in range(TILE_BLOCKS):
            fn(b)

    for q0 in range(W_SLOTS - 1):
        @pl.when(q0 < n_exp)
        def _(q0=q0):
            for cp in weights(q0):
                cp.start()
    for t0 in range(prefetch):
        @pl.when(t0 < n_act)
        def _(t0=t0):
            all_blocks(lambda b: gather(t0, b).start())

    def body(i, q):
        @pl.when(i + prefetch < n_act)
        def _():
            all_blocks(lambda b: gather(i + prefetch, b).start())

        all_blocks(lambda b: gather(i, b).wait())

        @pl.when(i >= Y_SLOTS)
        def _():
            all_blocks(lambda b: scatter(i - Y_SLOTS, b).wait())

        is_first = first_ref[i] == 1

        @pl.when(is_first)
        def _():
            for cp in weights(q):
                cp.wait()

            @pl.when(q + W_SLOTS - 1 < n_exp)
            def _():
                for cp in weights(q + W_SLOTS - 1):
                    cp.start()

        q = q + is_first.astype(I32)
        ws = (q - 1) % W_SLOTS
        xs = i % X_SLOTS
        ys = i % Y_SLOTS
        sub_blocks = SUB_ROWS // ROW_BLOCK
        for h in range(TILE_ROWS // SUB_ROWS):
            blocks = pl.ds(h * sub_blocks, sub_blocks)
            x = xbuf[xs, blocks].reshape(SUB_ROWS, XL_COLS)
            gate = jnp.sum(x[:, D_MODEL:].astype(F32), axis=1, keepdims=True)
            au = jnp.dot(x[:, :D_MODEL], wu_st[ws], preferred_element_type=F32)
            a = au[:, :EXPERT_FF]
            act = (a * _sigmoid(a)) * au[:, EXPERT_FF:]
            y = jnp.dot(act.astype(BF16), wd_st[ws], preferred_element_type=F32)
            ybuf[ys, blocks] = (gate * y).astype(BF16).reshape(sub_blocks, ROW_BLOCK, D_MODEL)

        all_blocks(lambda b: scatter(i, b).start())
        return q

    lax.fori_loop(0, n_act, body, jnp.int32(0))

    for k in range(Y_SLOTS):
        @pl.when(n_act - 1 - k >= 0)
        def _(k=k):
            all_blocks(lambda b: scatter(n_act - 1 - k, b).wait())


def _grouped_mlp(plan, xl, yz, w_up, w_down):
    grid_spec = pltpu.PrefetchScalarGridSpec(
        num_scalar_prefetch=len(plan),
        grid=(1,),
        in_specs=[pl.BlockSpec(memory_space=pl.ANY)] * 4,
        out_specs=pl.BlockSpec(memory_space=pl.ANY),
        scratch_shapes=[pltpu.VMEM((X_SLOTS, TILE_BLOCKS, ROW_BLOCK, XL_COLS), BF16),
                        pltpu.VMEM((Y_SLOTS, TILE_BLOCKS, ROW_BLOCK, D_MODEL), BF16),
                        pltpu.VMEM((W_SLOTS, D_MODEL, 2 * EXPERT_FF), BF16),
                        pltpu.VMEM((W_SLOTS, EXPERT_FF, D_MODEL), BF16),
                        pltpu.SemaphoreType.DMA((X_SLOTS,)),
                        pltpu.SemaphoreType.DMA((Y_SLOTS,)),
                        pltpu.SemaphoreType.DMA((2, W_SLOTS))])
    return pl.pallas_call(
        _gmm_kernel,
        grid_spec=grid_spec,
        out_shape=jax.ShapeDtypeStruct(yz.shape, yz.dtype),
        input_output_aliases={len(plan) + 1: 0},
        compiler_params=_cparams(("arbitrary",)),
        name="moe_experts",
    )(*plan, xl, yz, w_up, w_down)


COMMON_ROWS = 2 * CHUNK + DISP_ROWS
TAIL_ROWS = CHUNK_STRIDE - COMMON_ROWS


def _combine_kernel(used_ref, yl_ref, ylt_ref, pos_ref, x1_ref, gt2_ref, fg_ref, o_ref):
    used_rows = used_ref[pl.program_id(0)]
    half = CHUNK // 2

    def body(k_rows):
        sels = []
        for hf in range(2):
            pos = pos_ref[hf * half:(hf + 1) * half, :]
            p1 = pos[:, 0:1].astype(I32)
            p2 = pos[:, 1:2].astype(I32)
            r = lax.broadcasted_iota(I32, (half, k_rows), 1)
            sels.append(jnp.where(r == p1, 1.0, jnp.where(r == p2, 1.0, 0.0)).astype(BF16))
        for hf in range(2):
            rows = pl.ds(hf * half, half)
            y = jnp.dot(sels[hf][:, :COMMON_ROWS], yl_ref[0], preferred_element_type=F32)
            if k_rows > COMMON_ROWS:
                y = y + jnp.dot(sels[hf][:, COMMON_ROWS:], ylt_ref[0, 0:k_rows - COMMON_ROWS, :],
                                preferred_element_type=F32)
            x2 = x1_ref[rows, :] + gt2_ref[0] * y
            inv = lax.rsqrt(jnp.mean(x2 * x2, axis=-1, keepdims=True) + NORM_EPS)
            o_ref[rows, :] = (x2 * inv) * fg_ref[...]

    pl.when(used_rows <= COMMON_ROWS)(lambda: body(COMMON_ROWS))
    pl.when(used_rows > COMMON_ROWS)(lambda: body(CHUNK_ROWS))


def _combine(used_rows, yl, pos_c, x1, mod3, final_g, seq):
    n = x1.shape[0]
    per_seq = seq // CHUNK
    assert COMMON_ROWS % TAIL_ROWS == 0
    yl3 = yl.reshape(n // CHUNK, CHUNK_STRIDE, D_MODEL)

    def tail_index(c, u):
        return (jnp.where(u[c] > COMMON_ROWS, c, 0), COMMON_ROWS // TAIL_ROWS, 0)

    grid_spec = pltpu.PrefetchScalarGridSpec(
        num_scalar_prefetch=1,
        grid=(n // CHUNK,),
        in_specs=[pl.BlockSpec((1, COMMON_ROWS, D_MODEL), lambda c, u: (c, 0, 0)),
                  pl.BlockSpec((1, TAIL_ROWS, D_MODEL), tail_index),
                  pl.BlockSpec((CHUNK, LANES), lambda c, u: (c, 0)),
                  pl.BlockSpec((CHUNK, D_MODEL), lambda c, u: (c, 0)),
                  pl.BlockSpec((1, 1, D_MODEL), lambda c, u: (c // per_seq, 0, 5)),
                  pl.BlockSpec((1, D_MODEL), lambda c, u: (0, 0))],
        out_specs=pl.BlockSpec((CHUNK, D_MODEL), lambda c, u: (c, 0)))
    return pl.pallas_call(
        _combine_kernel,
        grid_spec=grid_spec,
        out_shape=jax.ShapeDtypeStruct((n, D_MODEL), F32),
        compiler_params=_cparams(("arbitrary",)),
        name="moe_combine",
    )(used_rows, yl3, yl3, pos_c, x1, mod3, final_g)


def _rope_tables(seq):
    n_freq = HEAD_DIM // 4
    inv_freq = ROPE_BASE ** (-jnp.arange(n_freq, dtype=F32) / n_freq)
    rows = seq // GRID_W
    row = jnp.repeat(jnp.arange(rows, dtype=F32), GRID_W)
    col = jnp.tile(jnp.arange(GRID_W, dtype=F32), rows)
    ang = jnp.concatenate([row[:, None] * inv_freq, col[:, None] * inv_freq], axis=-1)
    cos, sin = jnp.cos(ang), jnp.sin(ang)
    return jnp.tile(cos, (1, 4)), jnp.concatenate([-sin, sin, -sin, sin], axis=1)


def _window_bias():
    key = np.arange(3 * BLOCK)[:, None]
    qry = np.arange(BLOCK)[None, :]
    valid = (key - qry >= 0) & (key - qry <= 2 * BLOCK)
    return jnp.asarray(np.where(valid, 0.0, NEG_INF), F32)


def kernel(x, c, ctx, c_ctx, w_ada, b_ada, norm1_g, w_in, w_conv, b_conv, w_a, w_b, sink, w_o,
           norm2_g, w_group, b_group, w_router, b_router, w_up, w_down, final_g):
    bsz, seq, _ = x.shape
    assert w_ada.shape[0] == 1 and seq % SEQ_TILE == 0

    cc = jnp.zeros((16, D_MODEL), F32).at[:bsz].set(c).at[bsz].set(c_ctx)
    mod = _modulation(cc, w_ada[0], b_ada[0][None, :])
    mod3 = mod[:bsz].reshape(bsz, 1, N_MOD * D_MODEL)
    csh1 = mod[bsz:bsz + 1, 0:D_MODEL]
    csc1 = mod[bsz:bsz + 1, D_MODEL:2 * D_MODEL]

    w = w_in[0]
    n1g = norm1_g[0][None, :]
    n2g = norm2_g[0][None, :]
    w_kv = w[:, OFF_K:OFF_GA].astype(BF16)
    w_a_cols = jnp.concatenate([w[:, OFF_CG:OFF_Q], w[:, OFF_K:OFF_GA]], axis=1).astype(BF16)
    w_b_cols = jnp.concatenate([w[:, OFF_B:OFF_CG], w[:, OFF_Q:OFF_K], w[:, OFF_GA:]],
                               axis=1).astype(BF16)
    cos_t, sin_t = _rope_tables(seq)

    k_ctx, vt_ctx = _context_kv(ctx, n1g, csh1, csc1, w_kv)
    uc, k_rot, v_t = _proj_a(x, n1g, mod3, w_a_cols, cos_t, sin_t)

    sink_row = jnp.repeat(sink[0].astype(F32), BLOCK)[None, :]
    wr_t = jnp.zeros((LANES, D_MODEL), F32)
    wr_t = wr_t.at[0:N_GROUPS].set(w_group[0].T)
    wr_t = wr_t.at[GROUP_ROWS:GROUP_ROWS + N_EXPERTS].set(w_router[0].T).astype(BF16)
    br_col = jnp.zeros((LANES, 1), F32)
    br_col = br_col.at[0:N_GROUPS, 0].set(b_group[0])
    br_col = br_col.at[GROUP_ROWS:GROUP_ROWS + N_EXPERTS, 0].set(b_router[0])

    x1, h2, route, yz, w_up_bf, w_down_bf = _mix(
        x, n1g, n2g, mod3, w_b_cols, cos_t, sin_t, uc, k_rot, v_t, k_ctx, vt_ctx,
        w_conv[0], b_conv[0][None, :], w_a[0].astype(BF16), w_b[0].astype(BF16),
        w_o[0].astype(BF16), sink_row, _window_bias(), wr_t, br_col, w_up[0], w_down[0])

    n = bsz * seq
    nch = n // CHUNK
    upper = jnp.asarray(np.triu(np.ones((CHUNK, CHUNK), np.float32), 1), BF16)
    lower = jnp.asarray(np.tril(np.ones((N_EXPERTS, N_EXPERTS), np.float32), -1), BF16)
    xl, pos_c, nb = _dispatch(h2.reshape(n, D_MODEL), route, upper, lower)
    nb = nb[:, :, 0]
    yl = _grouped_mlp(_tile_plan(nb, nch), xl.reshape(-1, ROW_BLOCK, XL_COLS),
                      yz.reshape(-1, ROW_BLOCK, D_MODEL), w_up_bf, w_down_bf)
    yl = yl.reshape(-1, D_MODEL)
    used_rows = (ROW_BLOCK * jnp.sum(nb, axis=1)).astype(I32)
    out = _combine(used_rows, yl, pos_c, x1.reshape(n, D_MODEL), mod3, final_g[None, :], seq)
    return out.reshape(bsz, seq, D_MODEL)
```

```python
import numpy as np
import jax
import jax.numpy as jnp
from jax import lax
from jax.experimental import pallas as pl
from jax.experimental.pallas import tpu as pltpu

F32 = jnp.float32
BF16 = jnp.bfloat16
I32 = jnp.int32

D_MODEL = 1024
GRID_W = 64
CONV_W = 512
N_HEADS = 8
N_KV_HEADS = 2
HEAD_DIM = 64
ATT_W = N_HEADS * HEAD_DIM
KV_W = N_KV_HEADS * HEAD_DIM
BLOCK = 128
ROPE_BASE = 10000.0
N_GROUPS = 4
EXPERTS_PER_GROUP = 8
N_EXPERTS = N_GROUPS * EXPERTS_PER_GROUP
EXPERT_FF = 256
N_MOD = 6
NORM_EPS = 1e-6
NEG_INF = -1e30
LOG2E = 1.4426950408889634
ONES_ROWS = 16

OFF_B, OFF_CG, OFF_XIN, OFF_Q, OFF_K, OFF_V, OFF_GA, OFF_GB = (
    0, 512, 1024, 1536, 2048, 2176, 2304, 3328)
IN_COLS = 4352

LANES = 128
SEQ_TILE = 512
BLOCKS_PER_TILE = SEQ_TILE // BLOCK
ROUTE_ROWS = 8
GROUP_ROWS = 8
ROUTER_ROWS = 48
VMEM_LIMIT = 56 * 1024 * 1024

CHUNK = SEQ_TILE
ROW_BLOCK = 16
CHUNK_ROWS = -(-(2 * CHUNK + N_EXPERTS * (ROW_BLOCK - 1)) // 256) * 256
CHUNK_BLOCKS = CHUNK_ROWS // ROW_BLOCK
GATE_COLS = LANES
XL_COLS = D_MODEL + GATE_COLS
DISP_ROWS = 256
TILE_BLOCKS = 32
TILE_ROWS = TILE_BLOCKS * ROW_BLOCK
SUB_ROWS = 256
X_SLOTS = 4
Y_SLOTS = 4
W_SLOTS = 3


def _cparams(sem):
    return pltpu.CompilerParams(dimension_semantics=sem, vmem_limit_bytes=VMEM_LIMIT)


def _rms_mod(x, g, shift, scale):
    inv = lax.rsqrt(jnp.mean(x * x, axis=-1, keepdims=True) + NORM_EPS)
    return (x * inv) * (g * (1.0 + scale)) + shift


def _mod_kernel(c_ref, w_ref, b_ref, o_ref):
    c = c_ref[...]
    a = (c * jax.nn.sigmoid(c)).astype(BF16)
    o_ref[...] = jnp.dot(a, w_ref[...].astype(BF16), preferred_element_type=F32) + b_ref[...]


def _modulation(cc, w_ada, b_ada):
    rows = cc.shape[0]
    cols = w_ada.shape[1]
    tile = 1024
    return pl.pallas_call(
        _mod_kernel,
        grid=(cols // tile,),
        in_specs=[pl.BlockSpec((rows, D_MODEL), lambda j: (0, 0)),
                  pl.BlockSpec((D_MODEL, tile), lambda j: (0, j)),
                  pl.BlockSpec((1, tile), lambda j: (0, j))],
        out_specs=pl.BlockSpec((rows, tile), lambda j: (0, j)),
        out_shape=jax.ShapeDtypeStruct((rows, cols), F32),
        compiler_params=_cparams(("arbitrary",)),
        name="adaln_mod",
    )(cc, w_ada, b_ada)


def _ctx_kernel(x_ref, g_ref, sh_ref, sc_ref, w_ref, k_ref, vt_ref):
    h = _rms_mod(x_ref[0], g_ref[...], sh_ref[...], sc_ref[...]).astype(BF16)
    kv = jnp.dot(h, w_ref[...], preferred_element_type=F32)
    k_ref[0] = kv[:, :KV_W].astype(BF16)
    vt_ref[0] = kv[:, KV_W:].T.astype(BF16)


def _context_kv(ctx, norm_g, csh, csc, w_kv):
    bsz, clen, _ = ctx.shape
    return pl.pallas_call(
        _ctx_kernel,
        grid=(bsz,),
        in_specs=[pl.BlockSpec((1, clen, D_MODEL), lambda b: (b, 0, 0)),
                  pl.BlockSpec((1, D_MODEL), lambda b: (0, 0)),
                  pl.BlockSpec((1, D_MODEL), lambda b: (0, 0)),
                  pl.BlockSpec((1, D_MODEL), lambda b: (0, 0)),
                  pl.BlockSpec((D_MODEL, 2 * KV_W), lambda b: (0, 0))],
        out_specs=[pl.BlockSpec((1, clen, KV_W), lambda b: (b, 0, 0)),
                   pl.BlockSpec((1, KV_W, clen), lambda b: (b, 0, 0))],
        out_shape=[jax.ShapeDtypeStruct((bsz, clen, KV_W), BF16),
                   jax.ShapeDtypeStruct((bsz, KV_W, clen), BF16)],
        compiler_params=_cparams(("arbitrary",)),
        name="context_kv",
    )(ctx, norm_g, csh, csc, w_kv)


def _rope(t, cos, sin_signed):
    lane = lax.broadcasted_iota(I32, (1, LANES), 1)
    first_half = (lane % HEAD_DIM) < (HEAD_DIM // 2)
    outs = []
    for j in range(t.shape[1] // LANES):
        tj = t[:, j * LANES:(j + 1) * LANES]
        partner = jnp.where(first_half,
                            pltpu.roll(tj, LANES - HEAD_DIM // 2, axis=1),
                            pltpu.roll(tj, HEAD_DIM // 2, axis=1))
        outs.append(tj * cos + partner * sin_signed)
    return outs[0] if len(outs) == 1 else jnp.concatenate(outs, axis=1)


def _proj_a_kernel(x_ref, g_ref, sh_ref, sc_ref, w_ref, cos_ref, sin_ref, uc_ref, k_ref, vt_ref):
    half = SEQ_TILE // 2
    halves = [slice(hf * half, (hf + 1) * half) for hf in range(2)]
    hs = [_rms_mod(x_ref[0, rows, :], g_ref[...], sh_ref[0], sc_ref[0]).astype(BF16)
          for rows in halves]
    for h, rows in zip(hs, halves):
        u = jnp.dot(h, w_ref[...], preferred_element_type=F32)
        uc_ref[0, rows, :] = u[:, :CONV_W] * u[:, CONV_W:2 * CONV_W]
        k = u[:, 2 * CONV_W:2 * CONV_W + KV_W]
        k_ref[0, rows, :] = _rope(k, cos_ref[rows, :], sin_ref[rows, :]).astype(BF16)
        vt_ref[0, :, rows] = u[:, 2 * CONV_W + KV_W:].T.astype(BF16)


def _proj_a(x, norm_g, mod3, w_a_cols, cos_t, sin_t):
    bsz, seq, _ = x.shape
    nt = seq // SEQ_TILE
    wcols = w_a_cols.shape[1]
    return pl.pallas_call(
        _proj_a_kernel,
        grid=(nt, bsz),
        in_specs=[pl.BlockSpec((1, SEQ_TILE, D_MODEL), lambda t, b: (b, t, 0)),
                  pl.BlockSpec((1, D_MODEL), lambda t, b: (0, 0)),
                  pl.BlockSpec((1, 1, D_MODEL), lambda t, b: (b, 0, 0)),
                  pl.BlockSpec((1, 1, D_MODEL), lambda t, b: (b, 0, 1)),
                  pl.BlockSpec((D_MODEL, wcols), lambda t, b: (0, 0)),
                  pl.BlockSpec((SEQ_TILE, LANES), lambda t, b: (t, 0)),
                  pl.BlockSpec((SEQ_TILE, LANES), lambda t, b: (t, 0))],
        out_specs=[pl.BlockSpec((1, SEQ_TILE, CONV_W), lambda t, b: (b, t, 0)),
                   pl.BlockSpec((1, SEQ_TILE, KV_W), lambda t, b: (b, t, 0)),
                   pl.BlockSpec((1, KV_W, SEQ_TILE), lambda t, b: (b, 0, t))],
        out_shape=[jax.ShapeDtypeStruct((bsz, seq, CONV_W), F32),
                   jax.ShapeDtypeStruct((bsz, seq, KV_W), BF16),
                   jax.ShapeDtypeStruct((bsz, KV_W, seq), BF16)],
        compiler_params=_cparams(("arbitrary", "arbitrary")),
        name="proj_a",
    )(x, norm_g, mod3, mod3, w_a_cols, cos_t, sin_t)


def _sigmoid(x):
    return 1.0 / (1.0 + jnp.exp2(x * (-LOG2E)))


def _attn_scores(q_t, g, k_win, k_ctx, bias_p4, bias_n4):
    zeros = jnp.zeros((HEAD_DIM, 4 * BLOCK), BF16)
    qg = jnp.concatenate([q_t[(4 * g + h) * HEAD_DIM:(4 * g + h + 1) * HEAD_DIM, :]
                          for h in range(4)], axis=1)
    rhs = jnp.concatenate([qg, zeros] if g == 0 else [zeros, qg], axis=0)
    s_win = jnp.dot(k_win, rhs, preferred_element_type=F32)
    s_ctx = jnp.dot(k_ctx, rhs, preferred_element_type=F32)
    return (s_win[0:BLOCK] + bias_p4, s_win[BLOCK:2 * BLOCK], s_win[2 * BLOCK:] + bias_n4, s_ctx)


def _attn_probs(scores, sink):
    m = sink
    for s in scores:
        m = jnp.maximum(m, jnp.max(s, axis=0, keepdims=True))
    p_win = jnp.concatenate([jnp.exp2(s - m).astype(BF16) for s in scores[:3]], axis=0)
    p_ctx = jnp.exp2(scores[3] - m).astype(BF16)
    return p_win, p_ctx, m


def _attn_values(probs, sink, vt_win_g, vt_ctx_g):
    p_win, p_ctx, m = probs

    def with_ones(vt):
        r = lax.broadcasted_iota(I32, (ONES_ROWS, vt.shape[1]), 0)
        return jnp.concatenate([vt, jnp.where(r == 0, 1.0, 0.0).astype(BF16)], axis=0)

    o_ext = (jnp.dot(with_ones(vt_win_g), p_win, preferred_element_type=F32)
             + jnp.dot(with_ones(vt_ctx_g), p_ctx, preferred_element_type=F32))
    denom = o_ext[HEAD_DIM:HEAD_DIM + 1, :] + jnp.exp2(sink - m)
    return o_ext[:HEAD_DIM, :] / denom


def _route(logits_t):
    t = logits_t.shape[1]
    grow = lax.broadcasted_iota(I32, (GROUP_ROWS, t), 0)
    gl = jnp.where(grow < N_GROUPS, logits_t[0:GROUP_ROWS, :], NEG_INF)
    gm = jnp.max(gl, axis=0, keepdims=True)
    p_g = 1.0 / jnp.sum(jnp.exp(gl - gm), axis=0, keepdims=True)
    g_idx = jnp.min(jnp.where(gl == gm, grow, N_GROUPS), axis=0, keepdims=True)

    erow = lax.broadcasted_iota(I32, (N_EXPERTS, t), 0)
    el = logits_t[GROUP_ROWS:GROUP_ROWS + N_EXPERTS, :]
    sel = (erow // EXPERTS_PER_GROUP) == g_idx
    em = jnp.where(sel, el, NEG_INF)
    m1 = jnp.max(em, axis=0, keepdims=True)
    i1 = jnp.min(jnp.where(em == m1, erow, N_EXPERTS), axis=0, keepdims=True)
    em2 = jnp.where(erow == i1, NEG_INF, em)
    m2 = jnp.max(em2, axis=0, keepdims=True)
    i2 = jnp.min(jnp.where(em2 == m2, erow, N_EXPERTS), axis=0, keepdims=True)
    z = jnp.sum(jnp.where(sel, jnp.exp(el - m1), 0.0), axis=0, keepdims=True)
    p1 = 1.0 / z
    p2 = jnp.exp(m2 - m1) / z
    gate1 = p_g * p1 / (p1 + p2)
    gate2 = p_g * p2 / (p1 + p2)
    pad = jnp.zeros((ROUTE_ROWS - 4, t), F32)
    return jnp.concatenate([i1.astype(F32), i2.astype(F32), gate1, gate2, pad], axis=0)


def _mix_kernel(x_ref, g1n_ref, g2n_ref, sh1_ref, sc1_ref, gt1_ref, sh2_ref, sc2_ref,
                wb_ref, cos_ref, sin_ref,
                uc_ref, ucp_ref, ucn_ref, k_ref, kp_ref, kn_ref, vt_ref, vtp_ref, vtn_ref,
                kc_ref, vtc_ref, wconv_ref, bconv_ref, wa_ref, wbb_ref, wo_ref,
                sink_ref, bias_ref, wr_ref, br_ref, wu_ref, wd_ref,
                x1_ref, h2_ref, route_ref, wub_ref, wdb_ref):
    t_idx = pl.program_id(0)
    n_tiles = pl.num_programs(0)
    is_first = t_idx == 0
    is_last = t_idx == n_tiles - 1

    x = x_ref[0]
    half = SEQ_TILE // 2
    halves = [slice(hf * half, (hf + 1) * half) for hf in range(2)]

    uc = uc_ref[0]
    row = lax.broadcasted_iota(I32, (SEQ_TILE, 1), 0)
    prev_row = jnp.where(is_first, 0.0, ucp_ref[0][7:8, :])
    next_row = jnp.where(is_last, 0.0, ucn_ref[0][0:1, :])
    up = jnp.where(row == 0, prev_row, pltpu.roll(uc, 1, axis=0))
    dn = jnp.where(row == SEQ_TILE - 1, next_row, pltpu.roll(uc, SEQ_TILE - 1, axis=0))
    wconv = wconv_ref[...]
    y = bconv_ref[...] + (up * wconv[0:1, :] + uc * wconv[1:2, :] + dn * wconv[2:3, :])

    hbs = [_rms_mod(x[rows], g1n_ref[...], sh1_ref[0], sc1_ref[0]).astype(BF16) for rows in halves]
    yas, qs = [], []
    for hf, rows in enumerate(halves):
        bq = jnp.dot(hbs[hf], wb_ref[:, 0:2 * CONV_W], preferred_element_type=F32)
        yas.append(jnp.dot((bq[:, :CONV_W] * y[rows]).astype(BF16), wa_ref[...],
                           preferred_element_type=F32))
        qs.append(_rope(bq[:, CONV_W:], cos_ref[rows, :], sin_ref[rows, :])
                  * (HEAD_DIM ** -0.5 * LOG2E))
    hb = jnp.concatenate(hbs, axis=0)

    k_all = jnp.concatenate([kp_ref[0], k_ref[0], kn_ref[0]], axis=0)
    vt_all = jnp.concatenate([vtp_ref[0], vt_ref[0], vtn_ref[0]], axis=1)
    k_ctx = kc_ref[0]
    vt_ctx = vtc_ref[0]
    bias_prev = bias_ref[0:BLOCK, :]
    bias_next = bias_ref[2 * BLOCK:3 * BLOCK, :]
    sink_row = sink_ref[...] * LOG2E
    n_units = BLOCKS_PER_TILE * N_KV_HEADS
    gate_cols = 2 * D_MODEL // n_units
    gate_chunks = []

    def gate_chunk(u):
        c0 = 2 * CONV_W + u * gate_cols
        gate_chunks.append(jnp.dot(hb, wb_ref[:, c0:c0 + gate_cols], preferred_element_type=F32))

    q_ts, biases = [], []
    for jb in range(BLOCKS_PER_TILE):
        bias_p, bias_n = bias_prev, bias_next
        if jb == 0:
            bias_p = bias_prev + jnp.where(is_first, NEG_INF, 0.0)
        if jb == BLOCKS_PER_TILE - 1:
            bias_n = bias_next + jnp.where(is_last, NEG_INF, 0.0)
        biases.append((jnp.concatenate([bias_p] * 4, axis=1), jnp.concatenate([bias_n] * 4, axis=1)))
        hf, jh = divmod(jb, BLOCKS_PER_TILE // 2)
        q_ts.append(qs[hf][jh * BLOCK:(jh + 1) * BLOCK, :].T.astype(BF16))

    scores, probs, outs = {}, {}, {}
    for t in range(n_units + 2):
        if t == 3:
            wub_ref[...] = wu_ref[...].astype(BF16)
        if t == 5:
            wdb_ref[...] = wd_ref[...].astype(BF16)
        if t < n_units:
            jb, g = divmod(t, N_KV_HEADS)
            scores[t] = _attn_scores(q_ts[jb], g, k_all[jb * BLOCK:(jb + 3) * BLOCK, :], k_ctx,
                                     *biases[jb])
            gate_chunk(t)
        if 0 <= t - 1 < n_units:
            g = (t - 1) % N_KV_HEADS
            probs[t - 1] = _attn_probs(scores.pop(t - 1),
                                       sink_row[:, g * 4 * BLOCK:(g + 1) * 4 * BLOCK])
        if 0 <= t - 2 < n_units:
            jb, g = divmod(t - 2, N_KV_HEADS)
            outs[t - 2] = _attn_values(
                probs.pop(t - 2), sink_row[:, g * 4 * BLOCK:(g + 1) * 4 * BLOCK],
                vt_all[g * HEAD_DIM:(g + 1) * HEAD_DIM, jb * BLOCK:(jb + 3) * BLOCK],
                vt_ctx[g * HEAD_DIM:(g + 1) * HEAD_DIM, :])
    o_blocks = []
    for jb in range(BLOCKS_PER_TILE):
        o_rows = [outs[jb * N_KV_HEADS + g][:, h * BLOCK:(h + 1) * BLOCK]
                  for g in range(N_KV_HEADS) for h in range(4)]
        o_blocks.append(jnp.concatenate(o_rows, axis=0).T)
    gates = jnp.concatenate(gate_chunks, axis=1)

    half_blocks = BLOCKS_PER_TILE // 2
    for hf in range(2):
        r0, r1 = hf * half_blocks * BLOCK, (hf + 1) * half_blocks * BLOCK
        o = jnp.concatenate(o_blocks[hf * half_blocks:(hf + 1) * half_blocks], axis=0).astype(BF16)
        yb = jnp.dot(o, wbb_ref[...], preferred_element_type=F32)
        merged = (_sigmoid(gates[r0:r1, :D_MODEL]) * yas[hf]
                  + _sigmoid(gates[r0:r1, D_MODEL:]) * yb).astype(BF16)
        x1 = x[r0:r1] + gt1_ref[0] * jnp.dot(merged, wo_ref[...], preferred_element_type=F32)
        x1_ref[0, r0:r1, :] = x1
        h2 = _rms_mod(x1, g2n_ref[...], sh2_ref[0], sc2_ref[0]).astype(BF16)
        h2_ref[0, r0:r1, :] = h2
        logits_t = lax.dot_general(wr_ref[...], h2, (((1,), (1,)), ((), ())),
                                   preferred_element_type=F32) + br_ref[...]
        route_ref[0, :, r0:r1] = _route(logits_t)


def _mix(x, norm1_g, norm2_g, mod3, w_b_cols, cos_t, sin_t, uc, k_rot, v_t, k_ctx, vt_ctx,
         w_conv, b_conv, w_a, w_b, w_o, sink_row, bias, wr_t, br_col, w_up, w_down):
    bsz, seq, _ = x.shape
    nt = seq // SEQ_TILE
    nblk = seq // BLOCK
    clen = k_ctx.shape[1]
    rows8 = SEQ_TILE // 8
    assert N_EXPERTS % (nt * bsz) == 0, "each grid step converts an equal share of the experts"
    e_step = N_EXPERTS // (nt * bsz)

    def mod_spec(j):
        return pl.BlockSpec((1, 1, D_MODEL), lambda t, b, j=j: (b, 0, j))

    def const_spec(shape):
        return pl.BlockSpec(shape, lambda t, b: tuple(0 for _ in shape))

    in_specs = [
        pl.BlockSpec((1, SEQ_TILE, D_MODEL), lambda t, b: (b, t, 0)),
        const_spec((1, D_MODEL)), const_spec((1, D_MODEL)),
        mod_spec(0), mod_spec(1), mod_spec(2), mod_spec(3), mod_spec(4),
        const_spec(w_b_cols.shape),
        pl.BlockSpec((SEQ_TILE, LANES), lambda t, b: (t, 0)),
        pl.BlockSpec((SEQ_TILE, LANES), lambda t, b: (t, 0)),
        pl.BlockSpec((1, SEQ_TILE, CONV_W), lambda t, b: (b, t, 0)),
        pl.BlockSpec((1, 8, CONV_W), lambda t, b: (b, jnp.maximum(t * rows8 - 1, 0), 0)),
        pl.BlockSpec((1, 8, CONV_W),
                     lambda t, b: (b, jnp.minimum((t + 1) * rows8, seq // 8 - 1), 0)),
        pl.BlockSpec((1, SEQ_TILE, KV_W), lambda t, b: (b, t, 0)),
        pl.BlockSpec((1, BLOCK, KV_W),
                     lambda t, b: (b, jnp.maximum(t * BLOCKS_PER_TILE - 1, 0), 0)),
        pl.BlockSpec((1, BLOCK, KV_W),
                     lambda t, b: (b, jnp.minimum((t + 1) * BLOCKS_PER_TILE, nblk - 1), 0)),
        pl.BlockSpec((1, KV_W, SEQ_TILE), lambda t, b: (b, 0, t)),
        pl.BlockSpec((1, KV_W, BLOCK),
                     lambda t, b: (b, 0, jnp.maximum(t * BLOCKS_PER_TILE - 1, 0))),
        pl.BlockSpec((1, KV_W, BLOCK),
                     lambda t, b: (b, 0, jnp.minimum((t + 1) * BLOCKS_PER_TILE, nblk - 1))),
        pl.BlockSpec((1, clen, KV_W), lambda t, b: (b, 0, 0)),
        pl.BlockSpec((1, KV_W, clen), lambda t, b: (b, 0, 0)),
        const_spec(w_conv.shape), const_spec(b_conv.shape),
        const_spec(w_a.shape), const_spec(w_b.shape), const_spec(w_o.shape),
        const_spec(sink_row.shape), const_spec(bias.shape),
        const_spec(wr_t.shape), const_spec(br_col.shape),
        pl.BlockSpec((e_step,) + w_up.shape[1:], lambda t, b: (t * bsz + b, 0, 0)),
        pl.BlockSpec((e_step,) + w_down.shape[1:], lambda t, b: (t * bsz + b, 0, 0)),
    ]
    out_specs = [
        pl.BlockSpec((1, SEQ_TILE, D_MODEL), lambda t, b: (b, t, 0)),
        pl.BlockSpec((1, SEQ_TILE, D_MODEL), lambda t, b: (b, t, 0)),
        pl.BlockSpec((1, ROUTE_ROWS, SEQ_TILE), lambda t, b: (b, 0, t)),
        pl.BlockSpec((e_step,) + w_up.shape[1:], lambda t, b: (t * bsz + b, 0, 0)),
        pl.BlockSpec((e_step,) + w_down.shape[1:], lambda t, b: (t * bsz + b, 0, 0)),
    ]
    out_shape = [
        jax.ShapeDtypeStruct((bsz, seq, D_MODEL), F32),
        jax.ShapeDtypeStruct((bsz, seq, D_MODEL), BF16),
        jax.ShapeDtypeStruct((bsz, ROUTE_ROWS, seq), F32),
        jax.ShapeDtypeStruct(w_up.shape, BF16),
        jax.ShapeDtypeStruct(w_down.shape, BF16),
    ]
    return pl.pallas_call(
        _mix_kernel,
        grid=(nt, bsz),
        in_specs=in_specs,
        out_specs=out_specs,
        out_shape=out_shape,
        compiler_params=_cparams(("arbitrary", "arbitrary")),
        name="token_mix",
    )(x, norm1_g, norm2_g, mod3, mod3, mod3, mod3, mod3, w_b_cols, cos_t, sin_t,
      uc, uc, uc, k_rot, k_rot, k_rot, v_t, v_t, v_t, k_ctx, vt_ctx,
      w_conv, b_conv, w_a, w_b, w_o, sink_row, bias, wr_t, br_col, w_up, w_down)


def _bf16_parts(v):
    hi = v.astype(BF16).astype(F32)
    r1 = v - hi
    mid = r1.astype(BF16).astype(F32)
    lo = (r1 - mid).astype(BF16).astype(F32)
    return hi, mid, lo


def _dispatch_kernel(h_ref, route_ref, upper_ref, lower_ref, xl_ref, pos_ref, nb_ref):
    route = route_ref[0]
    e1 = route[0:1, :].astype(I32)
    e2 = route[1:2, :].astype(I32)
    erow = lax.broadcasted_iota(I32, (N_EXPERTS, CHUNK), 0)
    hit1 = erow == e1
    hit2 = erow == e2
    onehot = jnp.where(hit1, 1.0, 0.0) + jnp.where(hit2, 1.0, 0.0)
    cum = jnp.dot(onehot.astype(BF16), upper_ref[...], preferred_element_type=F32)
    cnt = jnp.sum(onehot, axis=1, keepdims=True)
    nblk = jnp.floor((cnt + (ROW_BLOCK - 1)) * (1.0 / ROW_BLOCK))
    nblk_b = jnp.broadcast_to(nblk, (N_EXPERTS, LANES))
    seg = jnp.dot(lower_ref[...], nblk_b.astype(BF16), preferred_element_type=F32) * ROW_BLOCK
    base = seg[:, 0:1] + cum
    pos1 = jnp.sum(jnp.where(hit1, base, 0.0), axis=0, keepdims=True)
    pos2 = jnp.sum(jnp.where(hit2, base, 0.0), axis=0, keepdims=True)
    p1i = pos1.astype(I32)
    p2i = pos2.astype(I32)
    used_rows = (jnp.sum(nblk) * ROW_BLOCK).astype(I32)

    prow = lax.broadcasted_iota(I32, (LANES, CHUNK), 0)
    parts = _bf16_parts(route[2:3, :]) + _bf16_parts(route[3:4, :])
    gpart_rows = jnp.zeros((LANES, CHUNK), F32)
    for j, part in enumerate(parts):
        gpart_rows = jnp.where(prow == j, part, gpart_rows)
    h_ext = jnp.concatenate([h_ref[...], gpart_rows.T.astype(BF16)], axis=1)
    glane = lax.broadcasted_iota(I32, (1, GATE_COLS), 1)
    one = jnp.ones((), BF16)
    zero = jnp.zeros((), BF16)
    r16 = lax.broadcasted_iota(I32, (DISP_ROWS, CHUNK), 0).astype(jnp.int16)

    def select(rc):
        q1 = (p1i - rc * DISP_ROWS).astype(jnp.int16)
        q2 = (p2i - rc * DISP_ROWS).astype(jnp.int16)
        sel1 = jnp.where(r16 == q1, one, zero)
        sel = jnp.where(r16 == q2, one, sel1)
        return sel, jnp.max(sel1, axis=1, keepdims=True)

    def permute(rc, sel, is_slot1):
        rows = pl.ds(rc * DISP_ROWS, DISP_ROWS)
        xg = jnp.dot(sel, h_ext, preferred_element_type=F32)
        xl_ref[rows, 0:D_MODEL] = xg[:, :D_MODEL].astype(BF16)
        g6 = xg[:, D_MODEL:]
        g3 = jnp.where(is_slot1.astype(F32) > 0.0, g6, pltpu.roll(g6, GATE_COLS - 3, axis=1))
        xl_ref[rows, D_MODEL:XL_COLS] = jnp.where(glane < 3, g3, 0.0).astype(BF16)

    n_always = 2 * CHUNK // DISP_ROWS + 1
    staged = select(0)
    for rc in range(n_always):
        nxt = select(rc + 1) if rc + 1 < n_always else None
        permute(rc, *staged)
        staged = nxt
    for rc in range(n_always, CHUNK_ROWS // DISP_ROWS):
        @pl.when(rc * DISP_ROWS < used_rows)
        def _(rc=rc):
            permute(rc, *select(rc))

        @pl.when(rc * DISP_ROWS >= used_rows)
        def _(rc=rc):
            xl_ref[pl.ds(rc * DISP_ROWS, DISP_ROWS), :] = jnp.zeros((DISP_ROWS, XL_COLS), BF16)

    pos_rows = jnp.where(prow == 0, pos1, jnp.where(prow == 1, pos2, 0.0))
    pos_ref[...] = pos_rows.T
    nb_ref[0] = nblk_b.astype(I32)


def _dispatch(h2, route, upper, lower):
    n = h2.shape[0]
    nch = n // CHUNK
    per_seq = route.shape[2] // CHUNK
    return pl.pallas_call(
        _dispatch_kernel,
        grid=(nch,),
        in_specs=[pl.BlockSpec((CHUNK, D_MODEL), lambda c: (c, 0)),
                  pl.BlockSpec((1, ROUTE_ROWS, CHUNK), lambda c: (c // per_seq, 0, c % per_seq)),
                  pl.BlockSpec(upper.shape, lambda c: (0, 0)),
                  pl.BlockSpec(lower.shape, lambda c: (0, 0))],
        out_specs=[pl.BlockSpec((CHUNK_ROWS, XL_COLS), lambda c: (c, 0)),
                   pl.BlockSpec((CHUNK, LANES), lambda c: (c, 0)),
                   pl.BlockSpec((1, N_EXPERTS, LANES), lambda c: (c, 0, 0))],
        out_shape=[jax.ShapeDtypeStruct((nch * CHUNK_ROWS, XL_COLS), BF16),
                   jax.ShapeDtypeStruct((n, LANES), F32),
                   jax.ShapeDtypeStruct((nch, N_EXPERTS, LANES), I32)],
        compiler_params=_cparams(("arbitrary",)),
        name="moe_dispatch",
    )(h2, route, upper, lower)


def _max_tiles(nch):
    max_blocks = nch * (2 * CHUNK + N_EXPERTS * (ROW_BLOCK - 1)) // ROW_BLOCK
    return max_blocks // TILE_BLOCKS + N_EXPERTS


def _masked_prefix(le, values):
    delta = values - jnp.concatenate([jnp.zeros((1,), values.dtype), values[:-1]])
    return jnp.sum(jnp.where(le, delta[None, :], 0), axis=1)


def _tile_plan(nb, nch):
    n_tiles = _max_tiles(nch)
    nbt = nb.T
    nbe = jnp.sum(nbt, axis=1)
    nte = (nbe + TILE_BLOCKS - 1) // TILE_BLOCKS
    tile_end = jnp.cumsum(nte)
    tile_start = tile_end - nte
    n_act = tile_end[-1]
    tiles = jnp.arange(n_tiles, dtype=I32)
    te = jnp.sum((tile_end[None, :] <= tiles[:, None]).astype(I32), axis=1)
    e_ar = jnp.arange(N_EXPERTS, dtype=I32)
    active = nte > 0
    te = jnp.where(tiles < n_act, te, jnp.max(jnp.where(active, e_ar, 0)))
    first = jnp.logical_and(te != jnp.concatenate([jnp.full((1,), -1, I32), te[:-1]]),
                            tiles < n_act).astype(I32)
    rank = jnp.cumsum(active.astype(I32)) - 1
    k_ar = jnp.arange(N_EXPERTS + W_SLOTS, dtype=I32)
    eseq = jnp.sum(jnp.where(jnp.logical_and(active[None, :], rank[None, :] == k_ar[:, None]),
                             e_ar[None, :], 0), axis=1)
    n_exp = jnp.sum(active.astype(I32))

    cb_excl = jnp.cumsum(nbt, axis=1) - nbt
    gs = TILE_BLOCKS * tile_start[:, None] + cb_excl
    seg_blk = jnp.cumsum(nb, axis=1) - nb
    base_blk = jnp.arange(nch, dtype=I32)[None, :] * CHUNK_BLOCKS + seg_blk.T
    gs_f = gs.reshape(-1)
    slots = jnp.arange(n_tiles * TILE_BLOCKS, dtype=I32)
    le = gs_f[None, :] <= slots[:, None]
    blk = slots + _masked_prefix(le, (base_blk - gs).reshape(-1))
    valid = slots < _masked_prefix(le, (gs + nbt).reshape(-1))
    gblk = jnp.where(valid, blk, 0).astype(I32)

    m = jnp.arange(CHUNK_BLOCKS, dtype=I32)
    le_c = seg_blk[:, None, :] <= m[None, :, None]
    shift = (gs.T - seg_blk)
    delta = shift - jnp.concatenate([jnp.zeros((nch, 1), I32), shift[:, :-1]], axis=1)
    slot_of = m[None, :] + jnp.sum(jnp.where(le_c, delta[:, None, :], 0), axis=2)
    used_blocks = jnp.sum(nb, axis=1)
    slot_of = jnp.where(m[None, :] < used_blocks[:, None], slot_of, 0).astype(I32)
    return ((n_act.reshape(1).astype(I32), n_exp.reshape(1), first, eseq, gblk),
            slot_of.reshape(-1))


def _gmm_kernel(nact_ref, nexp_ref, first_ref, eseq_ref, gblk_ref,
                xl_hbm, wu_hbm, wd_hbm, yt_hbm,
                xbuf, ybuf, zbuf, wu_st, wd_st, in_sem, out_sem, w_sem, z_sem):
    n_act = nact_ref[0]
    n_exp = nexp_ref[0]
    n_tiles = yt_hbm.shape[0] // TILE_BLOCKS
    prefetch = X_SLOTS - 1

    def gather(tile, b):
        s = tile % X_SLOTS
        blk = gblk_ref[tile * TILE_BLOCKS + b]
        return pltpu.make_async_copy(xl_hbm.at[blk], xbuf.at[s, b], in_sem.at[s])

    def write_back(tile):
        s = tile % Y_SLOTS
        return pltpu.make_async_copy(ybuf.at[s], yt_hbm.at[pl.ds(tile * TILE_BLOCKS, TILE_BLOCKS)],
                                     out_sem.at[s])

    def zero_fill(tile):
        return pltpu.make_async_copy(zbuf, yt_hbm.at[pl.ds(tile * TILE_BLOCKS, TILE_BLOCKS)], z_sem)

    def weights(q):
        e = eseq_ref[q]
        s = q % W_SLOTS
        return (pltpu.make_async_copy(wu_hbm.at[e], wu_st.at[s], w_sem.at[0, s]),
                pltpu.make_async_copy(wd_hbm.at[e], wd_st.at[s], w_sem.at[1, s]))

    sub_blocks = SUB_ROWS // ROW_BLOCK

    def tile_blocks(tile, fn):
        del tile
        for b in range(TILE_BLOCKS):
            fn(b)

    for q0 in range(W_SLOTS - 1):
        @pl.when(q0 < n_exp)
        def _(q0=q0):
            for cp in weights(q0):
                cp.start()
    for t0 in range(prefetch):
        @pl.when(t0 < n_act)
        def _(t0=t0):
            tile_blocks(t0, lambda b: gather(t0, b).start())

    def body(i, q):
        @pl.when(i + prefetch < n_act)
        def _():
            tile_blocks(i + prefetch, lambda b: gather(i + prefetch, b).start())

        tile_blocks(i, lambda b: gather(i, b).wait())

        @pl.when(i >= Y_SLOTS)
        def _():
            write_back(i - Y_SLOTS).wait()

        is_first = first_ref[i] == 1

        @pl.when(is_first)
        def _():
            for cp in weights(q):
                cp.wait()

            @pl.when(q + W_SLOTS - 1 < n_exp)
            def _():
                for cp in weights(q + W_SLOTS - 1):
                    cp.start()

        q = q + is_first.astype(I32)
        ws = (q - 1) % W_SLOTS
        xs = i % X_SLOTS
        ys = i % Y_SLOTS

        def sub_tile(h):
            blocks = pl.ds(h * sub_blocks, sub_blocks)
            x = xbuf[xs, blocks].reshape(SUB_ROWS, XL_COLS)
            gate = jnp.sum(x[:, D_MODEL:].astype(F32), axis=1, keepdims=True)
            au = jnp.dot(x[:, :D_MODEL], wu_st[ws], preferred_element_type=F32)
            a = au[:, :EXPERT_FF]
            act = (a * _sigmoid(a)) * au[:, EXPERT_FF:]
            y = jnp.dot(act.astype(BF16), wd_st[ws], preferred_element_type=F32)
            ybuf[ys, blocks] = (gate * y).astype(BF16).reshape(sub_blocks, ROW_BLOCK, D_MODEL)

        for h in range(TILE_ROWS // SUB_ROWS):
            sub_tile(h)
        write_back(i).start()

        @pl.when(n_act + i < n_tiles)
        def _():
            zero_fill(n_act + i).start()
        return q

    zbuf[...] = jnp.zeros_like(zbuf)
    lax.fori_loop(0, n_act, body, jnp.int32(0))

    def fill_rest(t, carry):
        zero_fill(t).start()
        return carry

    lax.fori_loop(jnp.minimum(2 * n_act, n_tiles), n_tiles, fill_rest, 0)

    def drain(t, carry):
        zero_fill(t).wait()
        return carry

    lax.fori_loop(n_act, n_tiles, drain, 0)
    for k in range(Y_SLOTS):
        @pl.when(n_act - 1 - k >= 0)
        def _(k=k):
            write_back(n_act - 1 - k).wait()


def _grouped_mlp(plan, xl, w_up, w_down, n_tiles):
    grid_spec = pltpu.PrefetchScalarGridSpec(
        num_scalar_prefetch=len(plan),
        grid=(1,),
        in_specs=[pl.BlockSpec(memory_space=pl.ANY)] * 3,
        out_specs=pl.BlockSpec(memory_space=pl.ANY),
        scratch_shapes=[pltpu.VMEM((X_SLOTS, TILE_BLOCKS, ROW_BLOCK, XL_COLS), BF16),
                        pltpu.VMEM((Y_SLOTS, TILE_BLOCKS, ROW_BLOCK, D_MODEL), BF16),
                        pltpu.VMEM((TILE_BLOCKS, ROW_BLOCK, D_MODEL), BF16),
                        pltpu.VMEM((W_SLOTS, D_MODEL, 2 * EXPERT_FF), BF16),
                        pltpu.VMEM((W_SLOTS, EXPERT_FF, D_MODEL), BF16),
                        pltpu.SemaphoreType.DMA((X_SLOTS,)),
                        pltpu.SemaphoreType.DMA((Y_SLOTS,)),
                        pltpu.SemaphoreType.DMA((2, W_SLOTS)),
                        pltpu.SemaphoreType.DMA(())])
    return pl.pallas_call(
        _gmm_kernel,
        grid_spec=grid_spec,
        out_shape=jax.ShapeDtypeStruct((n_tiles * TILE_BLOCKS, ROW_BLOCK, D_MODEL), BF16),
        compiler_params=_cparams(("arbitrary",)),
        name="moe_experts",
    )(*plan, xl, w_up, w_down)


COMMON_ROWS = 2 * CHUNK + DISP_ROWS
COMMON_BLOCKS = COMMON_ROWS // ROW_BLOCK


def _combine_kernel(used_ref, slot_ref, yt_hbm, pos_ref, x1_ref, gt2_ref, fg_ref, o_ref, ybuf, sem):
    c = pl.program_id(0)
    n_chunks = pl.num_programs(0)
    used_rows = used_ref[c]
    half = CHUNK // 2

    def fetch(chunk, b):
        s = chunk % 2
        slot = slot_ref[chunk * CHUNK_BLOCKS + b]
        return pltpu.make_async_copy(yt_hbm.at[slot], ybuf.at[s, b], sem.at[s])

    def chunk_blocks(chunk, fn):
        for b in range(COMMON_BLOCKS):
            fn(b)

        @pl.when(used_ref[chunk] > COMMON_ROWS)
        def _():
            for b in range(COMMON_BLOCKS, CHUNK_BLOCKS):
                fn(b)

    @pl.when(c == 0)
    def _():
        chunk_blocks(0, lambda b: fetch(0, b).start())

    @pl.when(c + 1 < n_chunks)
    def _():
        chunk_blocks(c + 1, lambda b: fetch(c + 1, b).start())

    chunk_blocks(c, lambda b: fetch(c, b).wait())

    def body(k_rows):
        sels = []
        for hf in range(2):
            pos = pos_ref[hf * half:(hf + 1) * half, :]
            p1 = pos[:, 0:1].astype(I32)
            p2 = pos[:, 1:2].astype(I32)
            r = lax.broadcasted_iota(I32, (half, k_rows), 1)
            sels.append(jnp.where(r == p1, 1.0, jnp.where(r == p2, 1.0, 0.0)).astype(BF16))
        yl = ybuf[c % 2, 0:k_rows // ROW_BLOCK].reshape(k_rows, D_MODEL)
        for hf in range(2):
            rows = pl.ds(hf * half, half)
            y = jnp.dot(sels[hf], yl, preferred_element_type=F32)
            x2 = x1_ref[rows, :] + gt2_ref[0] * y
            inv = lax.rsqrt(jnp.mean(x2 * x2, axis=-1, keepdims=True) + NORM_EPS)
            o_ref[rows, :] = (x2 * inv) * fg_ref[...]

    pl.when(used_rows <= COMMON_ROWS)(lambda: body(COMMON_ROWS))
    pl.when(used_rows > COMMON_ROWS)(lambda: body(CHUNK_ROWS))


def _combine(used_rows, slot_of, yt, pos_c, x1, mod3, final_g, seq):
    n = x1.shape[0]
    per_seq = seq // CHUNK
    grid_spec = pltpu.PrefetchScalarGridSpec(
        num_scalar_prefetch=2,
        grid=(n // CHUNK,),
        in_specs=[pl.BlockSpec(memory_space=pl.ANY),
                  pl.BlockSpec((CHUNK, LANES), lambda c, u, s: (c, 0)),
                  pl.BlockSpec((CHUNK, D_MODEL), lambda c, u, s: (c, 0)),
                  pl.BlockSpec((1, 1, D_MODEL), lambda c, u, s: (c // per_seq, 0, 5)),
                  pl.BlockSpec((1, D_MODEL), lambda c, u, s: (0, 0))],
        out_specs=pl.BlockSpec((CHUNK, D_MODEL), lambda c, u, s: (c, 0)),
        scratch_shapes=[pltpu.VMEM((2, CHUNK_BLOCKS, ROW_BLOCK, D_MODEL), BF16),
                        pltpu.SemaphoreType.DMA((2,))])
    return pl.pallas_call(
        _combine_kernel,
        grid_spec=grid_spec,
        out_shape=jax.ShapeDtypeStruct((n, D_MODEL), F32),
        compiler_params=_cparams(("arbitrary",)),
        name="moe_combine",
    )(used_rows, slot_of, yt, pos_c, x1, mod3, final_g)


def _rope_tables(seq):
    n_freq = HEAD_DIM // 4
    inv_freq = ROPE_BASE ** (-jnp.arange(n_freq, dtype=F32) / n_freq)
    rows = seq // GRID_W
    row = jnp.repeat(jnp.arange(rows, dtype=F32), GRID_W)
    col = jnp.tile(jnp.arange(GRID_W, dtype=F32), rows)
    ang = jnp.concatenate([row[:, None] * inv_freq, col[:, None] * inv_freq], axis=-1)
    cos, sin = jnp.cos(ang), jnp.sin(ang)
    return jnp.tile(cos, (1, 4)), jnp.concatenate([-sin, sin, -sin, sin], axis=1)


def _window_bias():
    key = np.arange(3 * BLOCK)[:, None]
    qry = np.arange(BLOCK)[None, :]
    valid = (key - qry >= 0) & (key - qry <= 2 * BLOCK)
    return jnp.asarray(np.where(valid, 0.0, NEG_INF), F32)


def kernel(x, c, ctx, c_ctx, w_ada, b_ada, norm1_g, w_in, w_conv, b_conv, w_a, w_b, sink, w_o,
           norm2_g, w_group, b_group, w_router, b_router, w_up, w_down, final_g):
    bsz, seq, _ = x.shape
    assert w_ada.shape[0] == 1 and seq % SEQ_TILE == 0

    cc = jnp.zeros((16, D_MODEL), F32).at[:bsz].set(c).at[bsz].set(c_ctx)
    mod = _modulation(cc, w_ada[0], b_ada[0][None, :])
    mod3 = mod[:bsz].reshape(bsz, 1, N_MOD * D_MODEL)
    csh1 = mod[bsz:bsz + 1, 0:D_MODEL]
    csc1 = mod[bsz:bsz + 1, D_MODEL:2 * D_MODEL]

    w = w_in[0]
    n1g = norm1_g[0][None, :]
    n2g = norm2_g[0][None, :]
    w_kv = w[:, OFF_K:OFF_GA].astype(BF16)
    w_a_cols = jnp.concatenate([w[:, OFF_CG:OFF_Q], w[:, OFF_K:OFF_GA]], axis=1).astype(BF16)
    w_b_cols = jnp.concatenate([w[:, OFF_B:OFF_CG], w[:, OFF_Q:OFF_K], w[:, OFF_GA:]],
                               axis=1).astype(BF16)
    cos_t, sin_t = _rope_tables(seq)

    k_ctx, vt_ctx = _context_kv(ctx, n1g, csh1, csc1, w_kv)
    uc, k_rot, v_t = _proj_a(x, n1g, mod3, w_a_cols, cos_t, sin_t)

    sink_row = jnp.repeat(sink[0].astype(F32), BLOCK)[None, :]
    pad_g = jnp.zeros((D_MODEL, GROUP_ROWS - N_GROUPS), F32)
    pad_e = jnp.zeros((D_MODEL, ROUTER_ROWS - GROUP_ROWS - N_EXPERTS), F32)
    wr_t = jnp.concatenate([w_group[0], pad_g, w_router[0], pad_e], axis=1).T.astype(BF16)
    br_col = jnp.concatenate([b_group[0], pad_g[0], b_router[0], pad_e[0]])[:, None]

    x1, h2, route, w_up_bf, w_down_bf = _mix(
        x, n1g, n2g, mod3, w_b_cols, cos_t, sin_t, uc, k_rot, v_t, k_ctx, vt_ctx,
        w_conv[0], b_conv[0][None, :], w_a[0].astype(BF16), w_b[0].astype(BF16),
        w_o[0].astype(BF16), sink_row, _window_bias(), wr_t, br_col, w_up[0], w_down[0])

    n = bsz * seq
    nch = n // CHUNK
    upper = jnp.asarray(np.triu(np.ones((CHUNK, CHUNK), np.float32), 1), BF16)
    lower = jnp.asarray(np.tril(np.ones((N_EXPERTS, N_EXPERTS), np.float32), -1), BF16)
    xl, pos_c, nb = _dispatch(h2.reshape(n, D_MODEL), route, upper, lower)
    nb = nb[:, :, 0]
    expert_plan, slot_of = _tile_plan(nb, nch)
    yt = _grouped_mlp(expert_plan, xl.reshape(-1, ROW_BLOCK, XL_COLS), w_up_bf, w_down_bf,
                      _max_tiles(nch))
    used_rows = (ROW_BLOCK * jnp.sum(nb, axis=1)).astype(I32)
    out = _combine(used_rows, slot_of, yt, pos_c, x1.reshape(n, D_MODEL), mod3, final_g[None, :],
                   seq)
    return out.reshape(bsz, seq, D_MODEL)
```

```python
import numpy as np
import jax
import jax.numpy as jnp
from jax import lax
from jax.experimental import pallas as pl
from jax.experimental.pallas import tpu as pltpu

F32 = jnp.float32
BF16 = jnp.bfloat16
I32 = jnp.int32

D_MODEL = 1024
GRID_W = 64
CONV_W = 512
N_HEADS = 8
N_KV_HEADS = 2
HEAD_DIM = 64
ATT_W = N_HEADS * HEAD_DIM
KV_W = N_KV_HEADS * HEAD_DIM
BLOCK = 128
ROPE_BASE = 10000.0
N_GROUPS = 4
EXPERTS_PER_GROUP = 8
N_EXPERTS = N_GROUPS * EXPERTS_PER_GROUP
EXPERT_FF = 256
N_MOD = 6
NORM_EPS = 1e-6
NEG_INF = -1e30
LOG2E = 1.4426950408889634
ONES_ROWS = 16

OFF_B, OFF_CG, OFF_XIN, OFF_Q, OFF_K, OFF_V, OFF_GA, OFF_GB = (
    0, 512, 1024, 1536, 2048, 2176, 2304, 3328)
IN_COLS = 4352

LANES = 128
SEQ_TILE = 512
BLOCKS_PER_TILE = SEQ_TILE // BLOCK
ROUTE_ROWS = 8
GROUP_ROWS = 8
ROUTER_ROWS = 48
VMEM_LIMIT = 56 * 1024 * 1024

CHUNK = SEQ_TILE
ROW_BLOCK = 16
CHUNK_ROWS = -(-(2 * CHUNK + N_EXPERTS * (ROW_BLOCK - 1)) // 256) * 256
CHUNK_BLOCKS = CHUNK_ROWS // ROW_BLOCK
GATE_COLS = LANES
XL_COLS = D_MODEL + GATE_COLS
DISP_ROWS = 256
TILE_BLOCKS = 32
TILE_ROWS = TILE_BLOCKS * ROW_BLOCK
SUB_ROWS = 256
X_SLOTS = 4
Y_SLOTS = 4
W_SLOTS = 3


def _cparams(sem):
    return pltpu.CompilerParams(dimension_semantics=sem, vmem_limit_bytes=VMEM_LIMIT)


def _rms_mod(x, g, shift, scale):
    inv = lax.rsqrt(jnp.mean(x * x, axis=-1, keepdims=True) + NORM_EPS)
    return (x * inv) * (g * (1.0 + scale)) + shift


def _mod_kernel(c_ref, w_ref, b_ref, o_ref):
    c = c_ref[...]
    a = (c * jax.nn.sigmoid(c)).astype(BF16)
    o_ref[...] = jnp.dot(a, w_ref[...].astype(BF16), preferred_element_type=F32) + b_ref[...]


def _modulation(cc, w_ada, b_ada):
    rows = cc.shape[0]
    cols = w_ada.shape[1]
    tile = 1024
    return pl.pallas_call(
        _mod_kernel,
        grid=(cols // tile,),
        in_specs=[pl.BlockSpec((rows, D_MODEL), lambda j: (0, 0)),
                  pl.BlockSpec((D_MODEL, tile), lambda j: (0, j)),
                  pl.BlockSpec((1, tile), lambda j: (0, j))],
        out_specs=pl.BlockSpec((rows, tile), lambda j: (0, j)),
        out_shape=jax.ShapeDtypeStruct((rows, cols), F32),
        compiler_params=_cparams(("arbitrary",)),
        name="adaln_mod",
    )(cc, w_ada, b_ada)


def _ctx_kernel(x_ref, g_ref, sh_ref, sc_ref, w_ref, k_ref, vt_ref):
    h = _rms_mod(x_ref[0], g_ref[...], sh_ref[...], sc_ref[...]).astype(BF16)
    kv = jnp.dot(h, w_ref[...], preferred_element_type=F32)
    k_ref[0] = kv[:, :KV_W].astype(BF16)
    vt_ref[0] = kv[:, KV_W:].T.astype(BF16)


def _context_kv(ctx, norm_g, csh, csc, w_kv):
    bsz, clen, _ = ctx.shape
    return pl.pallas_call(
        _ctx_kernel,
        grid=(bsz,),
        in_specs=[pl.BlockSpec((1, clen, D_MODEL), lambda b: (b, 0, 0)),
                  pl.BlockSpec((1, D_MODEL), lambda b: (0, 0)),
                  pl.BlockSpec((1, D_MODEL), lambda b: (0, 0)),
                  pl.BlockSpec((1, D_MODEL), lambda b: (0, 0)),
                  pl.BlockSpec((D_MODEL, 2 * KV_W), lambda b: (0, 0))],
        out_specs=[pl.BlockSpec((1, clen, KV_W), lambda b: (b, 0, 0)),
                   pl.BlockSpec((1, KV_W, clen), lambda b: (b, 0, 0))],
        out_shape=[jax.ShapeDtypeStruct((bsz, clen, KV_W), BF16),
                   jax.ShapeDtypeStruct((bsz, KV_W, clen), BF16)],
        compiler_params=_cparams(("arbitrary",)),
        name="context_kv",
    )(ctx, norm_g, csh, csc, w_kv)


def _rope(t, cos, sin_signed):
    lane = lax.broadcasted_iota(I32, (1, LANES), 1)
    first_half = (lane % HEAD_DIM) < (HEAD_DIM // 2)
    outs = []
    for j in range(t.shape[1] // LANES):
        tj = t[:, j * LANES:(j + 1) * LANES]
        partner = jnp.where(first_half,
                            pltpu.roll(tj, LANES - HEAD_DIM // 2, axis=1),
                            pltpu.roll(tj, HEAD_DIM // 2, axis=1))
        outs.append(tj * cos + partner * sin_signed)
    return outs[0] if len(outs) == 1 else jnp.concatenate(outs, axis=1)


def _proj_a_kernel(x_ref, g_ref, sh_ref, sc_ref, w_ref, cos_ref, sin_ref, uc_ref, k_ref, vt_ref):
    half = SEQ_TILE // 2
    halves = [slice(hf * half, (hf + 1) * half) for hf in range(2)]
    hs = [_rms_mod(x_ref[0, rows, :], g_ref[...], sh_ref[0], sc_ref[0]).astype(BF16)
          for rows in halves]
    for h, rows in zip(hs, halves):
        u = jnp.dot(h, w_ref[...], preferred_element_type=F32)
        uc_ref[0, rows, :] = u[:, :CONV_W] * u[:, CONV_W:2 * CONV_W]
        k = u[:, 2 * CONV_W:2 * CONV_W + KV_W]
        k_ref[0, rows, :] = _rope(k, cos_ref[rows, :], sin_ref[rows, :]).astype(BF16)
        vt_ref[0, :, rows] = u[:, 2 * CONV_W + KV_W:].T.astype(BF16)


def _proj_a(x, norm_g, mod3, w_a_cols, cos_t, sin_t):
    bsz, seq, _ = x.shape
    nt = seq // SEQ_TILE
    wcols = w_a_cols.shape[1]
    return pl.pallas_call(
        _proj_a_kernel,
        grid=(nt, bsz),
        in_specs=[pl.BlockSpec((1, SEQ_TILE, D_MODEL), lambda t, b: (b, t, 0)),
                  pl.BlockSpec((1, D_MODEL), lambda t, b: (0, 0)),
                  pl.BlockSpec((1, 1, D_MODEL), lambda t, b: (b, 0, 0)),
                  pl.BlockSpec((1, 1, D_MODEL), lambda t, b: (b, 0, 1)),
                  pl.BlockSpec((D_MODEL, wcols), lambda t, b: (0, 0)),
                  pl.BlockSpec((SEQ_TILE, LANES), lambda t, b: (t, 0)),
                  pl.BlockSpec((SEQ_TILE, LANES), lambda t, b: (t, 0))],
        out_specs=[pl.BlockSpec((1, SEQ_TILE, CONV_W), lambda t, b: (b, t, 0)),
                   pl.BlockSpec((1, SEQ_TILE, KV_W), lambda t, b: (b, t, 0)),
                   pl.BlockSpec((1, KV_W, SEQ_TILE), lambda t, b: (b, 0, t))],
        out_shape=[jax.ShapeDtypeStruct((bsz, seq, CONV_W), F32),
                   jax.ShapeDtypeStruct((bsz, seq, KV_W), BF16),
                   jax.ShapeDtypeStruct((bsz, KV_W, seq), BF16)],
        compiler_params=_cparams(("arbitrary", "arbitrary")),
        name="proj_a",
    )(x, norm_g, mod3, mod3, w_a_cols, cos_t, sin_t)


def _sigmoid(x):
    return 1.0 / (1.0 + jnp.exp2(x * (-LOG2E)))


def _attn_scores(q_t, g, k_win, k_ctx, bias_p4, bias_n4):
    zeros = jnp.zeros((HEAD_DIM, 4 * BLOCK), BF16)
    qg = jnp.concatenate([q_t[(4 * g + h) * HEAD_DIM:(4 * g + h + 1) * HEAD_DIM, :]
                          for h in range(4)], axis=1)
    rhs = jnp.concatenate([qg, zeros] if g == 0 else [zeros, qg], axis=0)
    s_win = jnp.dot(k_win, rhs, preferred_element_type=F32)
    s_ctx = jnp.dot(k_ctx, rhs, preferred_element_type=F32)
    return (s_win[0:BLOCK] + bias_p4, s_win[BLOCK:2 * BLOCK], s_win[2 * BLOCK:] + bias_n4, s_ctx)


def _attn_probs(scores, sink):
    m = sink
    for s in scores:
        m = jnp.maximum(m, jnp.max(s, axis=0, keepdims=True))
    p_win = jnp.concatenate([jnp.exp2(s - m).astype(BF16) for s in scores[:3]], axis=0)
    p_ctx = jnp.exp2(scores[3] - m).astype(BF16)
    return p_win, p_ctx, m


def _attn_values(probs, sink, vt_win_g, vt_ctx_g):
    p_win, p_ctx, m = probs

    def with_ones(vt):
        r = lax.broadcasted_iota(I32, (ONES_ROWS, vt.shape[1]), 0)
        return jnp.concatenate([vt, jnp.where(r == 0, 1.0, 0.0).astype(BF16)], axis=0)

    o_ext = (jnp.dot(with_ones(vt_win_g), p_win, preferred_element_type=F32)
             + jnp.dot(with_ones(vt_ctx_g), p_ctx, preferred_element_type=F32))
    denom = o_ext[HEAD_DIM:HEAD_DIM + 1, :] + jnp.exp2(sink - m)
    return o_ext[:HEAD_DIM, :] / denom


def _route(logits_t):
    t = logits_t.shape[1]
    grow = lax.broadcasted_iota(I32, (GROUP_ROWS, t), 0)
    gl = jnp.where(grow < N_GROUPS, logits_t[0:GROUP_ROWS, :], NEG_INF)
    gm = jnp.max(gl, axis=0, keepdims=True)
    p_g = 1.0 / jnp.sum(jnp.exp(gl - gm), axis=0, keepdims=True)
    g_idx = jnp.min(jnp.where(gl == gm, grow, N_GROUPS), axis=0, keepdims=True)

    erow = lax.broadcasted_iota(I32, (N_EXPERTS, t), 0)
    el = logits_t[GROUP_ROWS:GROUP_ROWS + N_EXPERTS, :]
    sel = (erow // EXPERTS_PER_GROUP) == g_idx
    em = jnp.where(sel, el, NEG_INF)
    m1 = jnp.max(em, axis=0, keepdims=True)
    i1 = jnp.min(jnp.where(em == m1, erow, N_EXPERTS), axis=0, keepdims=True)
    em2 = jnp.where(erow == i1, NEG_INF, em)
    m2 = jnp.max(em2, axis=0, keepdims=True)
    i2 = jnp.min(jnp.where(em2 == m2, erow, N_EXPERTS), axis=0, keepdims=True)
    z = jnp.sum(jnp.where(sel, jnp.exp(el - m1), 0.0), axis=0, keepdims=True)
    p1 = 1.0 / z
    p2 = jnp.exp(m2 - m1) / z
    gate1 = p_g * p1 / (p1 + p2)
    gate2 = p_g * p2 / (p1 + p2)
    pad = jnp.zeros((ROUTE_ROWS - 4, t), F32)
    return jnp.concatenate([i1.astype(F32), i2.astype(F32), gate1, gate2, pad], axis=0)


def _mix_kernel(x_ref, g1n_ref, g2n_ref, sh1_ref, sc1_ref, gt1_ref, sh2_ref, sc2_ref,
                wb_ref, cos_ref, sin_ref,
                uc_ref, ucp_ref, ucn_ref, k_ref, kp_ref, kn_ref, vt_ref, vtp_ref, vtn_ref,
                kc_ref, vtc_ref, wconv_ref, bconv_ref, wa_ref, wbb_ref, wo_ref,
                sink_ref, bias_ref, wr_ref, br_ref, wu_ref, wd_ref,
                x1_ref, h2_ref, route_ref, wub_ref, wdb_ref):
    t_idx = pl.program_id(0)
    n_tiles = pl.num_programs(0)
    is_first = t_idx == 0
    is_last = t_idx == n_tiles - 1

    x = x_ref[0]
    half = SEQ_TILE // 2
    halves = [slice(hf * half, (hf + 1) * half) for hf in range(2)]

    uc = uc_ref[0]
    row = lax.broadcasted_iota(I32, (SEQ_TILE, 1), 0)
    prev_row = jnp.where(is_first, 0.0, ucp_ref[0][7:8, :])
    next_row = jnp.where(is_last, 0.0, ucn_ref[0][0:1, :])
    up = jnp.where(row == 0, prev_row, pltpu.roll(uc, 1, axis=0))
    dn = jnp.where(row == SEQ_TILE - 1, next_row, pltpu.roll(uc, SEQ_TILE - 1, axis=0))
    wconv = wconv_ref[...]
    y = bconv_ref[...] + (up * wconv[0:1, :] + uc * wconv[1:2, :] + dn * wconv[2:3, :])

    hbs = [_rms_mod(x[rows], g1n_ref[...], sh1_ref[0], sc1_ref[0]).astype(BF16) for rows in halves]
    yas, qs = [], []
    for hf, rows in enumerate(halves):
        bq = jnp.dot(hbs[hf], wb_ref[:, 0:2 * CONV_W], preferred_element_type=F32)
        yas.append(jnp.dot((bq[:, :CONV_W] * y[rows]).astype(BF16), wa_ref[...],
                           preferred_element_type=F32))
        qs.append(_rope(bq[:, CONV_W:], cos_ref[rows, :], sin_ref[rows, :])
                  * (HEAD_DIM ** -0.5 * LOG2E))
    hb = jnp.concatenate(hbs, axis=0)

    k_all = jnp.concatenate([kp_ref[0], k_ref[0], kn_ref[0]], axis=0)
    vt_all = jnp.concatenate([vtp_ref[0], vt_ref[0], vtn_ref[0]], axis=1)
    k_ctx = kc_ref[0]
    vt_ctx = vtc_ref[0]
    bias_prev = bias_ref[0:BLOCK, :]
    bias_next = bias_ref[2 * BLOCK:3 * BLOCK, :]
    sink_row = sink_ref[...] * LOG2E
    n_units = BLOCKS_PER_TILE * N_KV_HEADS
    gate_cols = 2 * D_MODEL // n_units
    gate_chunks = []

    def gate_chunk(u):
        c0 = 2 * CONV_W + u * gate_cols
        gate_chunks.append(jnp.dot(hb, wb_ref[:, c0:c0 + gate_cols], preferred_element_type=F32))

    q_ts, biases = [], []
    for jb in range(BLOCKS_PER_TILE):
        bias_p, bias_n = bias_prev, bias_next
        if jb == 0:
            bias_p = bias_prev + jnp.where(is_first, NEG_INF, 0.0)
        if jb == BLOCKS_PER_TILE - 1:
            bias_n = bias_next + jnp.where(is_last, NEG_INF, 0.0)
        biases.append((jnp.concatenate([bias_p] * 4, axis=1), jnp.concatenate([bias_n] * 4, axis=1)))
        hf, jh = divmod(jb, BLOCKS_PER_TILE // 2)
        q_ts.append(qs[hf][jh * BLOCK:(jh + 1) * BLOCK, :].T.astype(BF16))

    scores, probs, outs = {}, {}, {}
    for t in range(n_units + 2):
        if t == 3:
            wub_ref[...] = wu_ref[...].astype(BF16)
        if t == 5:
            wdb_ref[...] = wd_ref[...].astype(BF16)
        if t < n_units:
            jb, g = divmod(t, N_KV_HEADS)
            scores[t] = _attn_scores(q_ts[jb], g, k_all[jb * BLOCK:(jb + 3) * BLOCK, :], k_ctx,
                                     *biases[jb])
            gate_chunk(t)
        if 0 <= t - 1 < n_units:
            g = (t - 1) % N_KV_HEADS
            probs[t - 1] = _attn_probs(scores.pop(t - 1),
                                       sink_row[:, g * 4 * BLOCK:(g + 1) * 4 * BLOCK])
        if 0 <= t - 2 < n_units:
            jb, g = divmod(t - 2, N_KV_HEADS)
            outs[t - 2] = _attn_values(
                probs.pop(t - 2), sink_row[:, g * 4 * BLOCK:(g + 1) * 4 * BLOCK],
                vt_all[g * HEAD_DIM:(g + 1) * HEAD_DIM, jb * BLOCK:(jb + 3) * BLOCK],
                vt_ctx[g * HEAD_DIM:(g + 1) * HEAD_DIM, :])
    o_blocks = []
    for jb in range(BLOCKS_PER_TILE):
        o_rows = [outs[jb * N_KV_HEADS + g][:, h * BLOCK:(h + 1) * BLOCK]
                  for g in range(N_KV_HEADS) for h in range(4)]
        o_blocks.append(jnp.concatenate(o_rows, axis=0).T)
    gates = jnp.concatenate(gate_chunks, axis=1)

    half_blocks = BLOCKS_PER_TILE // 2
    for hf in range(2):
        r0, r1 = hf * half_blocks * BLOCK, (hf + 1) * half_blocks * BLOCK
        o = jnp.concatenate(o_blocks[hf * half_blocks:(hf + 1) * half_blocks], axis=0).astype(BF16)
        yb = jnp.dot(o, wbb_ref[...], preferred_element_type=F32)
        merged = (_sigmoid(gates[r0:r1, :D_MODEL]) * yas[hf]
                  + _sigmoid(gates[r0:r1, D_MODEL:]) * yb).astype(BF16)
        x1 = x[r0:r1] + gt1_ref[0] * jnp.dot(merged, wo_ref[...], preferred_element_type=F32)
        x1_ref[0, r0:r1, :] = x1
        h2 = _rms_mod(x1, g2n_ref[...], sh2_ref[0], sc2_ref[0]).astype(BF16)
        h2_ref[0, r0:r1, :] = h2
        logits_t = lax.dot_general(wr_ref[...], h2, (((1,), (1,)), ((), ())),
                                   preferred_element_type=F32) + br_ref[...]
        route_ref[0, :, r0:r1] = _route(logits_t)


def _mix(x, norm1_g, norm2_g, mod3, w_b_cols, cos_t, sin_t, uc, k_rot, v_t, k_ctx, vt_ctx,
         w_conv, b_conv, w_a, w_b, w_o, sink_row, bias, wr_t, br_col, w_up, w_down):
    bsz, seq, _ = x.shape
    nt = seq // SEQ_TILE
    nblk = seq // BLOCK
    clen = k_ctx.shape[1]
    rows8 = SEQ_TILE // 8
    assert N_EXPERTS % (nt * bsz) == 0, "each grid step converts an equal share of the experts"
    e_step = N_EXPERTS // (nt * bsz)

    def mod_spec(j):
        return pl.BlockSpec((1, 1, D_MODEL), lambda t, b, j=j: (b, 0, j))

    def const_spec(shape):
        return pl.BlockSpec(shape, lambda t, b: tuple(0 for _ in shape))

    in_specs = [
        pl.BlockSpec((1, SEQ_TILE, D_MODEL), lambda t, b: (b, t, 0)),
        const_spec((1, D_MODEL)), const_spec((1, D_MODEL)),
        mod_spec(0), mod_spec(1), mod_spec(2), mod_spec(3), mod_spec(4),
        const_spec(w_b_cols.shape),
        pl.BlockSpec((SEQ_TILE, LANES), lambda t, b: (t, 0)),
        pl.BlockSpec((SEQ_TILE, LANES), lambda t, b: (t, 0)),
        pl.BlockSpec((1, SEQ_TILE, CONV_W), lambda t, b: (b, t, 0)),
        pl.BlockSpec((1, 8, CONV_W), lambda t, b: (b, jnp.maximum(t * rows8 - 1, 0), 0)),
        pl.BlockSpec((1, 8, CONV_W),
                     lambda t, b: (b, jnp.minimum((t + 1) * rows8, seq // 8 - 1), 0)),
        pl.BlockSpec((1, SEQ_TILE, KV_W), lambda t, b: (b, t, 0)),
        pl.BlockSpec((1, BLOCK, KV_W),
                     lambda t, b: (b, jnp.maximum(t * BLOCKS_PER_TILE - 1, 0), 0)),
        pl.BlockSpec((1, BLOCK, KV_W),
                     lambda t, b: (b, jnp.minimum((t + 1) * BLOCKS_PER_TILE, nblk - 1), 0)),
        pl.BlockSpec((1, KV_W, SEQ_TILE), lambda t, b: (b, 0, t)),
        pl.BlockSpec((1, KV_W, BLOCK),
                     lambda t, b: (b, 0, jnp.maximum(t * BLOCKS_PER_TILE - 1, 0))),
        pl.BlockSpec((1, KV_W, BLOCK),
                     lambda t, b: (b, 0, jnp.minimum((t + 1) * BLOCKS_PER_TILE, nblk - 1))),
        pl.BlockSpec((1, clen, KV_W), lambda t, b: (b, 0, 0)),
        pl.BlockSpec((1, KV_W, clen), lambda t, b: (b, 0, 0)),
        const_spec(w_conv.shape), const_spec(b_conv.shape),
        const_spec(w_a.shape), const_spec(w_b.shape), const_spec(w_o.shape),
        const_spec(sink_row.shape), const_spec(bias.shape),
        const_spec(wr_t.shape), const_spec(br_col.shape),
        pl.BlockSpec((e_step,) + w_up.shape[1:], lambda t, b: (t * bsz + b, 0, 0)),
        pl.BlockSpec((e_step,) + w_down.shape[1:], lambda t, b: (t * bsz + b, 0, 0)),
    ]
    out_specs = [
        pl.BlockSpec((1, SEQ_TILE, D_MODEL), lambda t, b: (b, t, 0)),
        pl.BlockSpec((1, SEQ_TILE, D_MODEL), lambda t, b: (b, t, 0)),
        pl.BlockSpec((1, ROUTE_ROWS, SEQ_TILE), lambda t, b: (b, 0, t)),
        pl.BlockSpec((e_step,) + w_up.shape[1:], lambda t, b: (t * bsz + b, 0, 0)),
        pl.BlockSpec((e_step,) + w_down.shape[1:], lambda t, b: (t * bsz + b, 0, 0)),
    ]
    out_shape = [
        jax.ShapeDtypeStruct((bsz, seq, D_MODEL), F32),
        jax.ShapeDtypeStruct((bsz, seq, D_MODEL), BF16),
        jax.ShapeDtypeStruct((bsz, ROUTE_ROWS, seq), F32),
        jax.ShapeDtypeStruct(w_up.shape, BF16),
        jax.ShapeDtypeStruct(w_down.shape, BF16),
    ]
    return pl.pallas_call(
        _mix_kernel,
        grid=(nt, bsz),
        in_specs=in_specs,
        out_specs=out_specs,
        out_shape=out_shape,
        compiler_params=_cparams(("arbitrary", "arbitrary")),
        name="token_mix",
    )(x, norm1_g, norm2_g, mod3, mod3, mod3, mod3, mod3, w_b_cols, cos_t, sin_t,
      uc, uc, uc, k_rot, k_rot, k_rot, v_t, v_t, v_t, k_ctx, vt_ctx,
      w_conv, b_conv, w_a, w_b, w_o, sink_row, bias, wr_t, br_col, w_up, w_down)


def _bf16_parts(v):
    hi = v.astype(BF16).astype(F32)
    r1 = v - hi
    mid = r1.astype(BF16).astype(F32)
    lo = (r1 - mid).astype(BF16).astype(F32)
    return hi, mid, lo


def _dispatch_kernel(h_ref, route_ref, upper_ref, lower_ref, xl_ref, pos_ref, nb_ref):
    route = route_ref[0]
    e1 = route[0:1, :].astype(I32)
    e2 = route[1:2, :].astype(I32)
    erow = lax.broadcasted_iota(I32, (N_EXPERTS, CHUNK), 0)
    hit1 = erow == e1
    hit2 = erow == e2
    onehot = jnp.where(hit1, 1.0, 0.0) + jnp.where(hit2, 1.0, 0.0)
    cum = jnp.dot(onehot.astype(BF16), upper_ref[...], preferred_element_type=F32)
    cnt = jnp.sum(onehot, axis=1, keepdims=True)
    nblk = jnp.floor((cnt + (ROW_BLOCK - 1)) * (1.0 / ROW_BLOCK))
    nblk_b = jnp.broadcast_to(nblk, (N_EXPERTS, LANES))
    seg = jnp.dot(lower_ref[...], nblk_b.astype(BF16), preferred_element_type=F32) * ROW_BLOCK
    base = seg[:, 0:1] + cum
    pos1 = jnp.sum(jnp.where(hit1, base, 0.0), axis=0, keepdims=True)
    pos2 = jnp.sum(jnp.where(hit2, base, 0.0), axis=0, keepdims=True)
    p1i = pos1.astype(I32)
    p2i = pos2.astype(I32)
    used_rows = (jnp.sum(nblk) * ROW_BLOCK).astype(I32)

    prow = lax.broadcasted_iota(I32, (LANES, CHUNK), 0)
    parts = _bf16_parts(route[2:3, :]) + _bf16_parts(route[3:4, :])
    gpart_rows = jnp.zeros((LANES, CHUNK), F32)
    for j, part in enumerate(parts):
        gpart_rows = jnp.where(prow == j, part, gpart_rows)
    h_ext = jnp.concatenate([h_ref[...], gpart_rows.T.astype(BF16)], axis=1)
    glane = lax.broadcasted_iota(I32, (1, GATE_COLS), 1)
    one = jnp.ones((), BF16)
    zero = jnp.zeros((), BF16)
    r16 = lax.broadcasted_iota(I32, (DISP_ROWS, CHUNK), 0).astype(jnp.int16)

    def select(rc):
        q1 = (p1i - rc * DISP_ROWS).astype(jnp.int16)
        q2 = (p2i - rc * DISP_ROWS).astype(jnp.int16)
        sel1 = jnp.where(r16 == q1, one, zero)
        sel = jnp.where(r16 == q2, one, sel1)
        return sel, jnp.max(sel1, axis=1, keepdims=True)

    def permute(rc, sel, is_slot1):
        rows = pl.ds(rc * DISP_ROWS, DISP_ROWS)
        xg = jnp.dot(sel, h_ext, preferred_element_type=F32)
        xl_ref[rows, 0:D_MODEL] = xg[:, :D_MODEL].astype(BF16)
        g6 = xg[:, D_MODEL:]
        g3 = jnp.where(is_slot1.astype(F32) > 0.0, g6, pltpu.roll(g6, GATE_COLS - 3, axis=1))
        xl_ref[rows, D_MODEL:XL_COLS] = jnp.where(glane < 3, g3, 0.0).astype(BF16)

    n_always = 2 * CHUNK // DISP_ROWS + 1
    staged = select(0)
    for rc in range(n_always):
        nxt = select(rc + 1) if rc + 1 < n_always else None
        permute(rc, *staged)
        staged = nxt
    for rc in range(n_always, CHUNK_ROWS // DISP_ROWS):
        @pl.when(rc * DISP_ROWS < used_rows)
        def _(rc=rc):
            permute(rc, *select(rc))

        @pl.when(rc * DISP_ROWS >= used_rows)
        def _(rc=rc):
            xl_ref[pl.ds(rc * DISP_ROWS, DISP_ROWS), :] = jnp.zeros((DISP_ROWS, XL_COLS), BF16)

    pos_rows = jnp.where(prow == 0, pos1, jnp.where(prow == 1, pos2, 0.0))
    pos_ref[...] = pos_rows.T
    nb_ref[0] = nblk_b.astype(I32)


def _dispatch(h2, route, upper, lower):
    n = h2.shape[0]
    nch = n // CHUNK
    per_seq = route.shape[2] // CHUNK
    return pl.pallas_call(
        _dispatch_kernel,
        grid=(nch,),
        in_specs=[pl.BlockSpec((CHUNK, D_MODEL), lambda c: (c, 0)),
                  pl.BlockSpec((1, ROUTE_ROWS, CHUNK), lambda c: (c // per_seq, 0, c % per_seq)),
                  pl.BlockSpec(upper.shape, lambda c: (0, 0)),
                  pl.BlockSpec(lower.shape, lambda c: (0, 0))],
        out_specs=[pl.BlockSpec((CHUNK_ROWS, XL_COLS), lambda c: (c, 0)),
                   pl.BlockSpec((CHUNK, LANES), lambda c: (c, 0)),
                   pl.BlockSpec((1, N_EXPERTS, LANES), lambda c: (c, 0, 0))],
        out_shape=[jax.ShapeDtypeStruct((nch * CHUNK_ROWS, XL_COLS), BF16),
                   jax.ShapeDtypeStruct((n, LANES), F32),
                   jax.ShapeDtypeStruct((nch, N_EXPERTS, LANES), I32)],
        compiler_params=_cparams(("arbitrary",)),
        name="moe_dispatch",
    )(h2, route, upper, lower)


def _max_tiles(nch):
    max_blocks = nch * (2 * CHUNK + N_EXPERTS * (ROW_BLOCK - 1)) // ROW_BLOCK
    return max_blocks // TILE_BLOCKS + N_EXPERTS


def _masked_prefix(le, values):
    delta = values - jnp.concatenate([jnp.zeros((1,), values.dtype), values[:-1]])
    return jnp.sum(jnp.where(le, delta[None, :], 0), axis=1)


def _tile_plan(nb, nch):
    n_tiles = _max_tiles(nch)
    nbt = nb.T
    nbe = jnp.sum(nbt, axis=1)
    nte = (nbe + TILE_BLOCKS - 1) // TILE_BLOCKS
    tile_end = jnp.cumsum(nte)
    tile_start = tile_end - nte
    n_act = tile_end[-1]
    tiles = jnp.arange(n_tiles, dtype=I32)
    te = jnp.sum((tile_end[None, :] <= tiles[:, None]).astype(I32), axis=1)
    e_ar = jnp.arange(N_EXPERTS, dtype=I32)
    active = nte > 0
    te = jnp.where(tiles < n_act, te, jnp.max(jnp.where(active, e_ar, 0)))
    first = jnp.logical_and(te != jnp.concatenate([jnp.full((1,), -1, I32), te[:-1]]),
                            tiles < n_act).astype(I32)
    rank = jnp.cumsum(active.astype(I32)) - 1
    k_ar = jnp.arange(N_EXPERTS + W_SLOTS, dtype=I32)
    eseq = jnp.sum(jnp.where(jnp.logical_and(active[None, :], rank[None, :] == k_ar[:, None]),
                             e_ar[None, :], 0), axis=1)
    n_exp = jnp.sum(active.astype(I32))

    cb_excl = jnp.cumsum(nbt, axis=1) - nbt
    gs = TILE_BLOCKS * tile_start[:, None] + cb_excl
    seg_blk = jnp.cumsum(nb, axis=1) - nb
    base_blk = jnp.arange(nch, dtype=I32)[None, :] * CHUNK_BLOCKS + seg_blk.T
    gs_f = gs.reshape(-1)
    slots = jnp.arange(n_tiles * TILE_BLOCKS, dtype=I32)
    le = gs_f[None, :] <= slots[:, None]
    blk = slots + _masked_prefix(le, (base_blk - gs).reshape(-1))
    valid = slots < _masked_prefix(le, (gs + nbt).reshape(-1))
    gblk = jnp.concatenate([jnp.where(valid, blk, 0).astype(I32),
                            jnp.zeros(((X_SLOTS - 1) * TILE_BLOCKS,), I32)])

    m = jnp.arange(CHUNK_BLOCKS, dtype=I32)
    le_c = seg_blk[:, None, :] <= m[None, :, None]
    shift = (gs.T - seg_blk)
    delta = shift - jnp.concatenate([jnp.zeros((nch, 1), I32), shift[:, :-1]], axis=1)
    slot_of = m[None, :] + jnp.sum(jnp.where(le_c, delta[:, None, :], 0), axis=2)
    used_blocks = jnp.sum(nb, axis=1)
    slot_of = jnp.where(m[None, :] < used_blocks[:, None], slot_of, 0).astype(I32)
    return ((n_act.reshape(1).astype(I32), n_exp.reshape(1), first, eseq, gblk),
            slot_of.reshape(-1))


def _gmm_kernel(nact_ref, nexp_ref, first_ref, eseq_ref, gblk_ref,
                xl_hbm, wu_hbm, wd_hbm, yt_hbm,
                xbuf, ybuf, zbuf, wu_st, wd_st, in_sem, out_sem, w_sem, z_sem):
    n_act = nact_ref[0]
    n_exp = nexp_ref[0]
    n_tiles = yt_hbm.shape[0] // TILE_BLOCKS
    prefetch = X_SLOTS - 1

    def gather(tile, b):
        s = tile % X_SLOTS
        blk = gblk_ref[tile * TILE_BLOCKS + b]
        return pltpu.make_async_copy(xl_hbm.at[blk], xbuf.at[s, b], in_sem.at[s])

    def write_back(tile):
        s = tile % Y_SLOTS
        return pltpu.make_async_copy(ybuf.at[s], yt_hbm.at[pl.ds(tile * TILE_BLOCKS, TILE_BLOCKS)],
                                     out_sem.at[s])

    def zero_fill(tile):
        return pltpu.make_async_copy(zbuf, yt_hbm.at[pl.ds(tile * TILE_BLOCKS, TILE_BLOCKS)], z_sem)

    def weights(q):
        e = eseq_ref[q]
        s = q % W_SLOTS
        return (pltpu.make_async_copy(wu_hbm.at[e], wu_st.at[s], w_sem.at[0, s]),
                pltpu.make_async_copy(wd_hbm.at[e], wd_st.at[s], w_sem.at[1, s]))

    sub_blocks = SUB_ROWS // ROW_BLOCK

    def tile_blocks(tile, fn):
        del tile
        for b in range(TILE_BLOCKS):
            fn(b)

    for q0 in range(W_SLOTS - 1):
        @pl.when(q0 < n_exp)
        def _(q0=q0):
            for cp in weights(q0):
                cp.start()
    for t0 in range(prefetch):
        tile_blocks(t0, lambda b, t0=t0: gather(t0, b).start())

    def body(i, q):
        tile_blocks(i, lambda b: gather(i, b).wait())

        @pl.when(i >= Y_SLOTS)
        def _():
            write_back(i - Y_SLOTS).wait()

        is_first = first_ref[i] == 1

        @pl.when(is_first)
        def _():
            for cp in weights(q):
                cp.wait()

            @pl.when(q + W_SLOTS - 1 < n_exp)
            def _():
                for cp in weights(q + W_SLOTS - 1):
                    cp.start()

        q = q + is_first.astype(I32)
        ws = (q - 1) % W_SLOTS
        xs = i % X_SLOTS
        ys = i % Y_SLOTS

        def sub_tile(h):
            blocks = pl.ds(h * sub_blocks, sub_blocks)
            x = xbuf[xs, blocks].reshape(SUB_ROWS, XL_COLS)
            gate = jnp.sum(x[:, D_MODEL:].astype(F32), axis=1, keepdims=True)
            au = jnp.dot(x[:, :D_MODEL], wu_st[ws], preferred_element_type=F32)
            if h == 0:
                tile_blocks(i + prefetch, lambda b: gather(i + prefetch, b).start())
            a = au[:, :EXPERT_FF]
            act = (a * _sigmoid(a)) * au[:, EXPERT_FF:]
            y = jnp.dot(act.astype(BF16), wd_st[ws], preferred_element_type=F32)
            ybuf[ys, blocks] = (gate * y).astype(BF16).reshape(sub_blocks, ROW_BLOCK, D_MODEL)

        for h in range(TILE_ROWS // SUB_ROWS):
            sub_tile(h)
        write_back(i).start()

        @pl.when(n_act + i < n_tiles)
        def _():
            zero_fill(n_act + i).start()
        return q

    zbuf[...] = jnp.zeros_like(zbuf)
    lax.fori_loop(0, n_act, body, jnp.int32(0))

    def fill_rest(t, carry):
        zero_fill(t).start()
        return carry

    lax.fori_loop(jnp.minimum(2 * n_act, n_tiles), n_tiles, fill_rest, 0)

    def drain(t, carry):
        zero_fill(t).wait()
        return carry

    lax.fori_loop(n_act, n_tiles, drain, 0)
    for k in range(prefetch):
        tile_blocks(n_act + k, lambda b, k=k: gather(n_act + k, b).wait())
    for k in range(Y_SLOTS):
        @pl.when(n_act - 1 - k >= 0)
        def _(k=k):
            write_back(n_act - 1 - k).wait()


def _grouped_mlp(plan, xl, w_up, w_down, n_tiles):
    grid_spec = pltpu.PrefetchScalarGridSpec(
        num_scalar_prefetch=len(plan),
        grid=(1,),
        in_specs=[pl.BlockSpec(memory_space=pl.ANY)] * 3,
        out_specs=pl.BlockSpec(memory_space=pl.ANY),
        scratch_shapes=[pltpu.VMEM((X_SLOTS, TILE_BLOCKS, ROW_BLOCK, XL_COLS), BF16),
                        pltpu.VMEM((Y_SLOTS, TILE_BLOCKS, ROW_BLOCK, D_MODEL), BF16),
                        pltpu.VMEM((TILE_BLOCKS, ROW_BLOCK, D_MODEL), BF16),
                        pltpu.VMEM((W_SLOTS, D_MODEL, 2 * EXPERT_FF), BF16),
                        pltpu.VMEM((W_SLOTS, EXPERT_FF, D_MODEL), BF16),
                        pltpu.SemaphoreType.DMA((X_SLOTS,)),
                        pltpu.SemaphoreType.DMA((Y_SLOTS,)),
                        pltpu.SemaphoreType.DMA((2, W_SLOTS)),
                        pltpu.SemaphoreType.DMA(())])
    return pl.pallas_call(
        _gmm_kernel,
        grid_spec=grid_spec,
        out_shape=jax.ShapeDtypeStruct((n_tiles * TILE_BLOCKS, ROW_BLOCK, D_MODEL), BF16),
        compiler_params=_cparams(("arbitrary",)),
        name="moe_experts",
    )(*plan, xl, w_up, w_down)


COMMON_ROWS = 2 * CHUNK + DISP_ROWS
COMMON_BLOCKS = COMMON_ROWS // ROW_BLOCK


def _combine_kernel(used_ref, slot_ref, yt_hbm, pos_ref, x1_ref, gt2_ref, fg_ref, o_ref, ybuf, sem):
    c = pl.program_id(0)
    n_chunks = pl.num_programs(0)
    used_rows = used_ref[c]
    half = CHUNK // 2

    def fetch(chunk, b):
        s = chunk % 2
        slot = slot_ref[chunk * CHUNK_BLOCKS + b]
        return pltpu.make_async_copy(yt_hbm.at[slot], ybuf.at[s, b], sem.at[s])

    def chunk_blocks(chunk, fn):
        for b in range(COMMON_BLOCKS):
            fn(b)

        @pl.when(used_ref[chunk] > COMMON_ROWS)
        def _():
            for b in range(COMMON_BLOCKS, CHUNK_BLOCKS):
                fn(b)

    @pl.when(c == 0)
    def _():
        chunk_blocks(0, lambda b: fetch(0, b).start())

    @pl.when(c + 1 < n_chunks)
    def _():
        chunk_blocks(c + 1, lambda b: fetch(c + 1, b).start())

    chunk_blocks(c, lambda b: fetch(c, b).wait())

    def body(k_rows):
        sels = []
        for hf in range(2):
            pos = pos_ref[hf * half:(hf + 1) * half, :]
            p1 = pos[:, 0:1].astype(I32)
            p2 = pos[:, 1:2].astype(I32)
            r = lax.broadcasted_iota(I32, (half, k_rows), 1)
            sels.append(jnp.where(r == p1, 1.0, jnp.where(r == p2, 1.0, 0.0)).astype(BF16))
        yl = ybuf[c % 2, 0:k_rows // ROW_BLOCK].reshape(k_rows, D_MODEL)
        for hf in range(2):
            rows = pl.ds(hf * half, half)
            y = jnp.dot(sels[hf], yl, preferred_element_type=F32)
            x2 = x1_ref[rows, :] + gt2_ref[0] * y
            inv = lax.rsqrt(jnp.mean(x2 * x2, axis=-1, keepdims=True) + NORM_EPS)
            o_ref[rows, :] = (x2 * inv) * fg_ref[...]

    pl.when(used_rows <= COMMON_ROWS)(lambda: body(COMMON_ROWS))
    pl.when(used_rows > COMMON_ROWS)(lambda: body(CHUNK_ROWS))


def _combine(used_rows, slot_of, yt, pos_c, x1, mod3, final_g, seq):
    n = x1.shape[0]
    per_seq = seq // CHUNK
    grid_spec = pltpu.PrefetchScalarGridSpec(
        num_scalar_prefetch=2,
        grid=(n // CHUNK,),
        in_specs=[pl.BlockSpec(memory_space=pl.ANY),
                  pl.BlockSpec((CHUNK, LANES), lambda c, u, s: (c, 0)),
                  pl.BlockSpec((CHUNK, D_MODEL), lambda c, u, s: (c, 0)),
                  pl.BlockSpec((1, 1, D_MODEL), lambda c, u, s: (c // per_seq, 0, 5)),
                  pl.BlockSpec((1, D_MODEL), lambda c, u, s: (0, 0))],
        out_specs=pl.BlockSpec((CHUNK, D_MODEL), lambda c, u, s: (c, 0)),
        scratch_shapes=[pltpu.VMEM((2, CHUNK_BLOCKS, ROW_BLOCK, D_MODEL), BF16),
                        pltpu.SemaphoreType.DMA((2,))])
    return pl.pallas_call(
        _combine_kernel,
        grid_spec=grid_spec,
        out_shape=jax.ShapeDtypeStruct((n, D_MODEL), F32),
        compiler_params=_cparams(("arbitrary",)),
        name="moe_combine",
    )(used_rows, slot_of, yt, pos_c, x1, mod3, final_g)


def _rope_tables(seq):
    n_freq = HEAD_DIM // 4
    inv_freq = ROPE_BASE ** (-jnp.arange(n_freq, dtype=F32) / n_freq)
    rows = seq // GRID_W
    row = jnp.repeat(jnp.arange(rows, dtype=F32), GRID_W)
    col = jnp.tile(jnp.arange(GRID_W, dtype=F32), rows)
    ang = jnp.concatenate([row[:, None] * inv_freq, col[:, None] * inv_freq], axis=-1)
    cos, sin = jnp.cos(ang), jnp.sin(ang)
    return jnp.tile(cos, (1, 4)), jnp.concatenate([-sin, sin, -sin, sin], axis=1)


def _window_bias():
    key = np.arange(3 * BLOCK)[:, None]
    qry = np.arange(BLOCK)[None, :]
    valid = (key - qry >= 0) & (key - qry <= 2 * BLOCK)
    return jnp.asarray(np.where(valid, 0.0, NEG_INF), F32)


def kernel(x, c, ctx, c_ctx, w_ada, b_ada, norm1_g, w_in, w_conv, b_conv, w_a, w_b, sink, w_o,
           norm2_g, w_group, b_group, w_router, b_router, w_up, w_down, final_g):
    bsz, seq, _ = x.shape
    assert w_ada.shape[0] == 1 and seq % SEQ_TILE == 0

    cc = jnp.zeros((16, D_MODEL), F32).at[:bsz].set(c).at[bsz].set(c_ctx)
    mod = _modulation(cc, w_ada[0], b_ada[0][None, :])
    mod3 = mod[:bsz].reshape(bsz, 1, N_MOD * D_MODEL)
    csh1 = mod[bsz:bsz + 1, 0:D_MODEL]
    csc1 = mod[bsz:bsz + 1, D_MODEL:2 * D_MODEL]

    w = w_in[0]
    n1g = norm1_g[0][None, :]
    n2g = norm2_g[0][None, :]
    w_kv = w[:, OFF_K:OFF_GA].astype(BF16)
    w_a_cols = jnp.concatenate([w[:, OFF_CG:OFF_Q], w[:, OFF_K:OFF_GA]], axis=1).astype(BF16)
    w_b_cols = jnp.concatenate([w[:, OFF_B:OFF_CG], w[:, OFF_Q:OFF_K], w[:, OFF_GA:]],
                               axis=1).astype(BF16)
    cos_t, sin_t = _rope_tables(seq)

    k_ctx, vt_ctx = _context_kv(ctx, n1g, csh1, csc1, w_kv)
    uc, k_rot, v_t = _proj_a(x, n1g, mod3, w_a_cols, cos_t, sin_t)

    sink_row = jnp.repeat(sink[0].astype(F32), BLOCK)[None, :]
    pad_g = jnp.zeros((D_MODEL, GROUP_ROWS - N_GROUPS), F32)
    pad_e = jnp.zeros((D_MODEL, ROUTER_ROWS - GROUP_ROWS - N_EXPERTS), F32)
    wr_t = jnp.concatenate([w_group[0], pad_g, w_router[0], pad_e], axis=1).T.astype(BF16)
    br_col = jnp.concatenate([b_group[0], pad_g[0], b_router[0], pad_e[0]])[:, None]

    x1, h2, route, w_up_bf, w_down_bf = _mix(
        x, n1g, n2g, mod3, w_b_cols, cos_t, sin_t, uc, k_rot, v_t, k_ctx, vt_ctx,
        w_conv[0], b_conv[0][None, :], w_a[0].astype(BF16), w_b[0].astype(BF16),
        w_o[0].astype(BF16), sink_row, _window_bias(), wr_t, br_col, w_up[0], w_down[0])

    n = bsz * seq
    nch = n // CHUNK
    upper = jnp.asarray(np.triu(np.ones((CHUNK, CHUNK), np.float32), 1), BF16)
    lower = jnp.asarray(np.tril(np.ones((N_EXPERTS, N_EXPERTS), np.float32), -1), BF16)
    xl, pos_c, nb = _dispatch(h2.reshape(n, D_MODEL), route, upper, lower)
    nb = nb[:, :, 0]
    expert_plan, slot_of = _tile_plan(nb, nch)
    yt = _grouped_mlp(expert_plan, xl.reshape(-1, ROW_BLOCK, XL_COLS), w_up_bf, w_down_bf,
                      _max_tiles(nch))
    used_rows = (ROW_BLOCK * jnp.sum(nb, axis=1)).astype(I32)
    out = _combine(used_rows, slot_of, yt, pos_c, x1.reshape(n, D_MODEL), mod3, final_g[None, :],
                   seq)
    return out.reshape(bsz, seq, D_MODEL)
```

```python
import numpy as np
import jax
import jax.numpy as jnp
from jax import lax
from jax.experimental import pallas as pl
from jax.experimental.pallas import tpu as pltpu

F32 = jnp.float32
BF16 = jnp.bfloat16
I32 = jnp.int32

D_MODEL = 1024
GRID_W = 64
CONV_W = 512
N_HEADS = 8
N_KV_HEADS = 2
HEAD_DIM = 64
ATT_W = N_HEADS * HEAD_DIM
KV_W = N_KV_HEADS * HEAD_DIM
BLOCK = 128
ROPE_BASE = 10000.0
N_GROUPS = 4
EXPERTS_PER_GROUP = 8
N_EXPERTS = N_GROUPS * EXPERTS_PER_GROUP
EXPERT_FF = 256
N_MOD = 6
NORM_EPS = 1e-6
NEG_INF = -1e30
LOG2E = 1.4426950408889634
ONES_ROWS = 16

OFF_B, OFF_CG, OFF_XIN, OFF_Q, OFF_K, OFF_V, OFF_GA, OFF_GB = (
    0, 512, 1024, 1536, 2048, 2176, 2304, 3328)
IN_COLS = 4352

LANES = 128
SEQ_TILE = 512
BLOCKS_PER_TILE = SEQ_TILE // BLOCK
ROUTE_ROWS = 8
GROUP_ROWS = 8
ROUTER_ROWS = 48
VMEM_LIMIT = 56 * 1024 * 1024

CHUNK = SEQ_TILE
ROW_BLOCK = 16
CHUNK_ROWS = -(-(2 * CHUNK + N_EXPERTS * (ROW_BLOCK - 1)) // 256) * 256
CHUNK_BLOCKS = CHUNK_ROWS // ROW_BLOCK
GATE_COLS = LANES
XL_COLS = D_MODEL + GATE_COLS
DISP_ROWS = 256
TILE_BLOCKS = 32
TILE_ROWS = TILE_BLOCKS * ROW_BLOCK
SUB_ROWS = 256
X_SLOTS = 6
Y_SLOTS = 4
W_SLOTS = 3


def _cparams(sem):
    return pltpu.CompilerParams(dimension_semantics=sem, vmem_limit_bytes=VMEM_LIMIT)


def _rms_mod(x, g, shift, scale):
    inv = lax.rsqrt(jnp.mean(x * x, axis=-1, keepdims=True) + NORM_EPS)
    return (x * inv) * (g * (1.0 + scale)) + shift


def _mod_kernel(c_ref, w_ref, b_ref, o_ref):
    c = c_ref[...]
    a = (c * jax.nn.sigmoid(c)).astype(BF16)
    o_ref[...] = jnp.dot(a, w_ref[...].astype(BF16), preferred_element_type=F32) + b_ref[...]


def _modulation(cc, w_ada, b_ada):
    rows = cc.shape[0]
    cols = w_ada.shape[1]
    tile = 1024
    return pl.pallas_call(
        _mod_kernel,
        grid=(cols // tile,),
        in_specs=[pl.BlockSpec((rows, D_MODEL), lambda j: (0, 0)),
                  pl.BlockSpec((D_MODEL, tile), lambda j: (0, j)),
                  pl.BlockSpec((1, tile), lambda j: (0, j))],
        out_specs=pl.BlockSpec((rows, tile), lambda j: (0, j)),
        out_shape=jax.ShapeDtypeStruct((rows, cols), F32),
        compiler_params=_cparams(("arbitrary",)),
        name="adaln_mod",
    )(cc, w_ada, b_ada)


def _ctx_kernel(x_ref, g_ref, sh_ref, sc_ref, w_ref, k_ref, vt_ref):
    h = _rms_mod(x_ref[0], g_ref[...], sh_ref[...], sc_ref[...]).astype(BF16)
    kv = jnp.dot(h, w_ref[...], preferred_element_type=F32)
    k_ref[0] = kv[:, :KV_W].astype(BF16)
    vt_ref[0] = kv[:, KV_W:].T.astype(BF16)


def _context_kv(ctx, norm_g, csh, csc, w_kv):
    bsz, clen, _ = ctx.shape
    return pl.pallas_call(
        _ctx_kernel,
        grid=(bsz,),
        in_specs=[pl.BlockSpec((1, clen, D_MODEL), lambda b: (b, 0, 0)),
                  pl.BlockSpec((1, D_MODEL), lambda b: (0, 0)),
                  pl.BlockSpec((1, D_MODEL), lambda b: (0, 0)),
                  pl.BlockSpec((1, D_MODEL), lambda b: (0, 0)),
                  pl.BlockSpec((D_MODEL, 2 * KV_W), lambda b: (0, 0))],
        out_specs=[pl.BlockSpec((1, clen, KV_W), lambda b: (b, 0, 0)),
                   pl.BlockSpec((1, KV_W, clen), lambda b: (b, 0, 0))],
        out_shape=[jax.ShapeDtypeStruct((bsz, clen, KV_W), BF16),
                   jax.ShapeDtypeStruct((bsz, KV_W, clen), BF16)],
        compiler_params=_cparams(("arbitrary",)),
        name="context_kv",
    )(ctx, norm_g, csh, csc, w_kv)


def _rope(t, cos, sin_signed):
    lane = lax.broadcasted_iota(I32, (1, LANES), 1)
    first_half = (lane % HEAD_DIM) < (HEAD_DIM // 2)
    outs = []
    for j in range(t.shape[1] // LANES):
        tj = t[:, j * LANES:(j + 1) * LANES]
        partner = jnp.where(first_half,
                            pltpu.roll(tj, LANES - HEAD_DIM // 2, axis=1),
                            pltpu.roll(tj, HEAD_DIM // 2, axis=1))
        outs.append(tj * cos + partner * sin_signed)
    return outs[0] if len(outs) == 1 else jnp.concatenate(outs, axis=1)


def _proj_a_kernel(x_ref, g_ref, sh_ref, sc_ref, w_ref, cos_ref, sin_ref, uc_ref, k_ref, vt_ref):
    half = SEQ_TILE // 2
    halves = [slice(hf * half, (hf + 1) * half) for hf in range(2)]
    hs = [_rms_mod(x_ref[0, rows, :], g_ref[...], sh_ref[0], sc_ref[0]).astype(BF16)
          for rows in halves]
    for h, rows in zip(hs, halves):
        u = jnp.dot(h, w_ref[...], preferred_element_type=F32)
        uc_ref[0, rows, :] = u[:, :CONV_W] * u[:, CONV_W:2 * CONV_W]
        k = u[:, 2 * CONV_W:2 * CONV_W + KV_W]
        k_ref[0, rows, :] = _rope(k, cos_ref[rows, :], sin_ref[rows, :]).astype(BF16)
        vt_ref[0, :, rows] = u[:, 2 * CONV_W + KV_W:].T.astype(BF16)


def _proj_a(x, norm_g, mod3, w_a_cols, cos_t, sin_t):
    bsz, seq, _ = x.shape
    nt = seq // SEQ_TILE
    wcols = w_a_cols.shape[1]
    return pl.pallas_call(
        _proj_a_kernel,
        grid=(nt, bsz),
        in_specs=[pl.BlockSpec((1, SEQ_TILE, D_MODEL), lambda t, b: (b, t, 0)),
                  pl.BlockSpec((1, D_MODEL), lambda t, b: (0, 0)),
                  pl.BlockSpec((1, 1, D_MODEL), lambda t, b: (b, 0, 0)),
                  pl.BlockSpec((1, 1, D_MODEL), lambda t, b: (b, 0, 1)),
                  pl.BlockSpec((D_MODEL, wcols), lambda t, b: (0, 0)),
                  pl.BlockSpec((SEQ_TILE, LANES), lambda t, b: (t, 0)),
                  pl.BlockSpec((SEQ_TILE, LANES), lambda t, b: (t, 0))],
        out_specs=[pl.BlockSpec((1, SEQ_TILE, CONV_W), lambda t, b: (b, t, 0)),
                   pl.BlockSpec((1, SEQ_TILE, KV_W), lambda t, b: (b, t, 0)),
                   pl.BlockSpec((1, KV_W, SEQ_TILE), lambda t, b: (b, 0, t))],
        out_shape=[jax.ShapeDtypeStruct((bsz, seq, CONV_W), F32),
                   jax.ShapeDtypeStruct((bsz, seq, KV_W), BF16),
                   jax.ShapeDtypeStruct((bsz, KV_W, seq), BF16)],
        compiler_params=_cparams(("arbitrary", "arbitrary")),
        name="proj_a",
    )(x, norm_g, mod3, mod3, w_a_cols, cos_t, sin_t)


def _sigmoid(x):
    return 1.0 / (1.0 + jnp.exp2(x * (-LOG2E)))


def _attn_scores(q_t, g, k_win, k_ctx, bias_p4, bias_n4):
    zeros = jnp.zeros((HEAD_DIM, 4 * BLOCK), BF16)
    qg = jnp.concatenate([q_t[(4 * g + h) * HEAD_DIM:(4 * g + h + 1) * HEAD_DIM, :]
                          for h in range(4)], axis=1)
    rhs = jnp.concatenate([qg, zeros] if g == 0 else [zeros, qg], axis=0)
    s_win = jnp.dot(k_win, rhs, preferred_element_type=F32)
    s_ctx = jnp.dot(k_ctx, rhs, preferred_element_type=F32)
    return (s_win[0:BLOCK] + bias_p4, s_win[BLOCK:2 * BLOCK], s_win[2 * BLOCK:] + bias_n4, s_ctx)


def _attn_probs(scores, sink):
    m = sink
    for s in scores:
        m = jnp.maximum(m, jnp.max(s, axis=0, keepdims=True))
    p_win = jnp.concatenate([jnp.exp2(s - m).astype(BF16) for s in scores[:3]], axis=0)
    p_ctx = jnp.exp2(scores[3] - m).astype(BF16)
    return p_win, p_ctx, m


def _attn_values(probs, sink, vt_win_g, vt_ctx_g):
    p_win, p_ctx, m = probs

    def with_ones(vt):
        r = lax.broadcasted_iota(I32, (ONES_ROWS, vt.shape[1]), 0)
        return jnp.concatenate([vt, jnp.where(r == 0, 1.0, 0.0).astype(BF16)], axis=0)

    o_ext = (jnp.dot(with_ones(vt_win_g), p_win, preferred_element_type=F32)
             + jnp.dot(with_ones(vt_ctx_g), p_ctx, preferred_element_type=F32))
    denom = o_ext[HEAD_DIM:HEAD_DIM + 1, :] + jnp.exp2(sink - m)
    return o_ext[:HEAD_DIM, :] / denom


def _route(logits_t):
    t = logits_t.shape[1]
    grow = lax.broadcasted_iota(I32, (GROUP_ROWS, t), 0)
    gl = jnp.where(grow < N_GROUPS, logits_t[0:GROUP_ROWS, :], NEG_INF)
    gm = jnp.max(gl, axis=0, keepdims=True)
    p_g = 1.0 / jnp.sum(jnp.exp(gl - gm), axis=0, keepdims=True)
    g_idx = jnp.min(jnp.where(gl == gm, grow, N_GROUPS), axis=0, keepdims=True)

    erow = lax.broadcasted_iota(I32, (N_EXPERTS, t), 0)
    el = logits_t[GROUP_ROWS:GROUP_ROWS + N_EXPERTS, :]
    sel = (erow // EXPERTS_PER_GROUP) == g_idx
    em = jnp.where(sel, el, NEG_INF)
    m1 = jnp.max(em, axis=0, keepdims=True)
    i1 = jnp.min(jnp.where(em == m1, erow, N_EXPERTS), axis=0, keepdims=True)
    em2 = jnp.where(erow == i1, NEG_INF, em)
    m2 = jnp.max(em2, axis=0, keepdims=True)
    i2 = jnp.min(jnp.where(em2 == m2, erow, N_EXPERTS), axis=0, keepdims=True)
    z = jnp.sum(jnp.where(sel, jnp.exp(el - m1), 0.0), axis=0, keepdims=True)
    p1 = 1.0 / z
    p2 = jnp.exp(m2 - m1) / z
    gate1 = p_g * p1 / (p1 + p2)
    gate2 = p_g * p2 / (p1 + p2)
    pad = jnp.zeros((ROUTE_ROWS - 4, t), F32)
    return jnp.concatenate([i1.astype(F32), i2.astype(F32), gate1, gate2, pad], axis=0)


def _mix_kernel(x_ref, g1n_ref, g2n_ref, sh1_ref, sc1_ref, gt1_ref, sh2_ref, sc2_ref,
                wb_ref, cos_ref, sin_ref,
                uc_ref, ucp_ref, ucn_ref, k_ref, kp_ref, kn_ref, vt_ref, vtp_ref, vtn_ref,
                kc_ref, vtc_ref, wconv_ref, bconv_ref, wa_ref, wbb_ref, wo_ref,
                sink_ref, bias_ref, wr_ref, br_ref, wu_ref, wd_ref,
                x1_ref, h2_ref, route_ref, wub_ref, wdb_ref):
    t_idx = pl.program_id(0)
    n_tiles = pl.num_programs(0)
    is_first = t_idx == 0
    is_last = t_idx == n_tiles - 1

    x = x_ref[0]
    half = SEQ_TILE // 2
    halves = [slice(hf * half, (hf + 1) * half) for hf in range(2)]

    uc = uc_ref[0]
    row = lax.broadcasted_iota(I32, (SEQ_TILE, 1), 0)
    prev_row = jnp.where(is_first, 0.0, ucp_ref[0][7:8, :])
    next_row = jnp.where(is_last, 0.0, ucn_ref[0][0:1, :])
    up = jnp.where(row == 0, prev_row, pltpu.roll(uc, 1, axis=0))
    dn = jnp.where(row == SEQ_TILE - 1, next_row, pltpu.roll(uc, SEQ_TILE - 1, axis=0))
    wconv = wconv_ref[...]
    y = bconv_ref[...] + (up * wconv[0:1, :] + uc * wconv[1:2, :] + dn * wconv[2:3, :])

    hbs = [_rms_mod(x[rows], g1n_ref[...], sh1_ref[0], sc1_ref[0]).astype(BF16) for rows in halves]
    yas, qs = [], []
    for hf, rows in enumerate(halves):
        bq = jnp.dot(hbs[hf], wb_ref[:, 0:2 * CONV_W], preferred_element_type=F32)
        yas.append(jnp.dot((bq[:, :CONV_W] * y[rows]).astype(BF16), wa_ref[...],
                           preferred_element_type=F32))
        qs.append(_rope(bq[:, CONV_W:], cos_ref[rows, :], sin_ref[rows, :])
                  * (HEAD_DIM ** -0.5 * LOG2E))
    hb = jnp.concatenate(hbs, axis=0)

    k_all = jnp.concatenate([kp_ref[0], k_ref[0], kn_ref[0]], axis=0)
    vt_all = jnp.concatenate([vtp_ref[0], vt_ref[0], vtn_ref[0]], axis=1)
    k_ctx = kc_ref[0]
    vt_ctx = vtc_ref[0]
    bias_prev = bias_ref[0:BLOCK, :]
    bias_next = bias_ref[2 * BLOCK:3 * BLOCK, :]
    sink_row = sink_ref[...] * LOG2E
    n_units = BLOCKS_PER_TILE * N_KV_HEADS
    gate_cols = 2 * D_MODEL // n_units
    gate_chunks = []

    def gate_chunk(u):
        c0 = 2 * CONV_W + u * gate_cols
        gate_chunks.append(jnp.dot(hb, wb_ref[:, c0:c0 + gate_cols], preferred_element_type=F32))

    q_ts, biases = [], []
    for jb in range(BLOCKS_PER_TILE):
        bias_p, bias_n = bias_prev, bias_next
        if jb == 0:
            bias_p = bias_prev + jnp.where(is_first, NEG_INF, 0.0)
        if jb == BLOCKS_PER_TILE - 1:
            bias_n = bias_next + jnp.where(is_last, NEG_INF, 0.0)
        biases.append((jnp.concatenate([bias_p] * 4, axis=1), jnp.concatenate([bias_n] * 4, axis=1)))
        hf, jh = divmod(jb, BLOCKS_PER_TILE // 2)
        q_ts.append(qs[hf][jh * BLOCK:(jh + 1) * BLOCK, :].T.astype(BF16))

    scores, probs, outs = {}, {}, {}
    for t in range(n_units + 2):
        if t == 3:
            wub_ref[...] = wu_ref[...].astype(BF16)
        if t == 5:
            wdb_ref[...] = wd_ref[...].astype(BF16)
        if t < n_units:
            jb, g = divmod(t, N_KV_HEADS)
            scores[t] = _attn_scores(q_ts[jb], g, k_all[jb * BLOCK:(jb + 3) * BLOCK, :], k_ctx,
                                     *biases[jb])
            gate_chunk(t)
        if 0 <= t - 1 < n_units:
            g = (t - 1) % N_KV_HEADS
            probs[t - 1] = _attn_probs(scores.pop(t - 1),
                                       sink_row[:, g * 4 * BLOCK:(g + 1) * 4 * BLOCK])
        if 0 <= t - 2 < n_units:
            jb, g = divmod(t - 2, N_KV_HEADS)
            outs[t - 2] = _attn_values(
                probs.pop(t - 2), sink_row[:, g * 4 * BLOCK:(g + 1) * 4 * BLOCK],
                vt_all[g * HEAD_DIM:(g + 1) * HEAD_DIM, jb * BLOCK:(jb + 3) * BLOCK],
                vt_ctx[g * HEAD_DIM:(g + 1) * HEAD_DIM, :])
    o_blocks = []
    for jb in range(BLOCKS_PER_TILE):
        o_rows = [outs[jb * N_KV_HEADS + g][:, h * BLOCK:(h + 1) * BLOCK]
                  for g in range(N_KV_HEADS) for h in range(4)]
        o_blocks.append(jnp.concatenate(o_rows, axis=0).T)
    gates = jnp.concatenate(gate_chunks, axis=1)

    half_blocks = BLOCKS_PER_TILE // 2
    for hf in range(2):
        r0, r1 = hf * half_blocks * BLOCK, (hf + 1) * half_blocks * BLOCK
        o = jnp.concatenate(o_blocks[hf * half_blocks:(hf + 1) * half_blocks], axis=0).astype(BF16)
        yb = jnp.dot(o, wbb_ref[...], preferred_element_type=F32)
        merged = (_sigmoid(gates[r0:r1, :D_MODEL]) * yas[hf]
                  + _sigmoid(gates[r0:r1, D_MODEL:]) * yb).astype(BF16)
        x1 = x[r0:r1] + gt1_ref[0] * jnp.dot(merged, wo_ref[...], preferred_element_type=F32)
        x1_ref[0, r0:r1, :] = x1
        h2 = _rms_mod(x1, g2n_ref[...], sh2_ref[0], sc2_ref[0]).astype(BF16)
        h2_ref[0, r0:r1, :] = h2
        logits_t = lax.dot_general(wr_ref[...], h2, (((1,), (1,)), ((), ())),
                                   preferred_element_type=F32) + br_ref[...]
        route_ref[0, :, r0:r1] = _route(logits_t)


def _mix(x, norm1_g, norm2_g, mod3, w_b_cols, cos_t, sin_t, uc, k_rot, v_t, k_ctx, vt_ctx,
         w_conv, b_conv, w_a, w_b, w_o, sink_row, bias, wr_t, br_col, w_up, w_down):
    bsz, seq, _ = x.shape
    nt = seq // SEQ_TILE
    nblk = seq // BLOCK
    clen = k_ctx.shape[1]
    rows8 = SEQ_TILE // 8
    assert N_EXPERTS % (nt * bsz) == 0, "each grid step converts an equal share of the experts"
    e_step = N_EXPERTS // (nt * bsz)

    def mod_spec(j):
        return pl.BlockSpec((1, 1, D_MODEL), lambda t, b, j=j: (b, 0, j))

    def const_spec(shape):
        return pl.BlockSpec(shape, lambda t, b: tuple(0 for _ in shape))

    in_specs = [
        pl.BlockSpec((1, SEQ_TILE, D_MODEL), lambda t, b: (b, t, 0)),
        const_spec((1, D_MODEL)), const_spec((1, D_MODEL)),
        mod_spec(0), mod_spec(1), mod_spec(2), mod_spec(3), mod_spec(4),
        const_spec(w_b_cols.shape),
        pl.BlockSpec((SEQ_TILE, LANES), lambda t, b: (t, 0)),
        pl.BlockSpec((SEQ_TILE, LANES), lambda t, b: (t, 0)),
        pl.BlockSpec((1, SEQ_TILE, CONV_W), lambda t, b: (b, t, 0)),
        pl.BlockSpec((1, 8, CONV_W), lambda t, b: (b, jnp.maximum(t * rows8 - 1, 0), 0)),
        pl.BlockSpec((1, 8, CONV_W),
                     lambda t, b: (b, jnp.minimum((t + 1) * rows8, seq // 8 - 1), 0)),
        pl.BlockSpec((1, SEQ_TILE, KV_W), lambda t, b: (b, t, 0)),
        pl.BlockSpec((1, BLOCK, KV_W),
                     lambda t, b: (b, jnp.maximum(t * BLOCKS_PER_TILE - 1, 0), 0)),
        pl.BlockSpec((1, BLOCK, KV_W),
                     lambda t, b: (b, jnp.minimum((t + 1) * BLOCKS_PER_TILE, nblk - 1), 0)),
        pl.BlockSpec((1, KV_W, SEQ_TILE), lambda t, b: (b, 0, t)),
        pl.BlockSpec((1, KV_W, BLOCK),
                     lambda t, b: (b, 0, jnp.maximum(t * BLOCKS_PER_TILE - 1, 0))),
        pl.BlockSpec((1, KV_W, BLOCK),
                     lambda t, b: (b, 0, jnp.minimum((t + 1) * BLOCKS_PER_TILE, nblk - 1))),
        pl.BlockSpec((1, clen, KV_W), lambda t, b: (b, 0, 0)),
        pl.BlockSpec((1, KV_W, clen), lambda t, b: (b, 0, 0)),
        const_spec(w_conv.shape), const_spec(b_conv.shape),
        const_spec(w_a.shape), const_spec(w_b.shape), const_spec(w_o.shape),
        const_spec(sink_row.shape), const_spec(bias.shape),
        const_spec(wr_t.shape), const_spec(br_col.shape),
        pl.BlockSpec((e_step,) + w_up.shape[1:], lambda t, b: (t * bsz + b, 0, 0)),
        pl.BlockSpec((e_step,) + w_down.shape[1:], lambda t, b: (t * bsz + b, 0, 0)),
    ]
    out_specs = [
        pl.BlockSpec((1, SEQ_TILE, D_MODEL), lambda t, b: (b, t, 0)),
        pl.BlockSpec((1, SEQ_TILE, D_MODEL), lambda t, b: (b, t, 0)),
        pl.BlockSpec((1, ROUTE_ROWS, SEQ_TILE), lambda t, b: (b, 0, t)),
        pl.BlockSpec((e_step,) + w_up.shape[1:], lambda t, b: (t * bsz + b, 0, 0)),
        pl.BlockSpec((e_step,) + w_down.shape[1:], lambda t, b: (t * bsz + b, 0, 0)),
    ]
    out_shape = [
        jax.ShapeDtypeStruct((bsz, seq, D_MODEL), F32),
        jax.ShapeDtypeStruct((bsz, seq, D_MODEL), BF16),
        jax.ShapeDtypeStruct((bsz, ROUTE_ROWS, seq), F32),
        jax.ShapeDtypeStruct(w_up.shape, BF16),
        jax.ShapeDtypeStruct(w_down.shape, BF16),
    ]
    return pl.pallas_call(
        _mix_kernel,
        grid=(nt, bsz),
        in_specs=in_specs,
        out_specs=out_specs,
        out_shape=out_shape,
        compiler_params=_cparams(("arbitrary", "arbitrary")),
        name="token_mix",
    )(x, norm1_g, norm2_g, mod3, mod3, mod3, mod3, mod3, w_b_cols, cos_t, sin_t,
      uc, uc, uc, k_rot, k_rot, k_rot, v_t, v_t, v_t, k_ctx, vt_ctx,
      w_conv, b_conv, w_a, w_b, w_o, sink_row, bias, wr_t, br_col, w_up, w_down)


def _bf16_parts(v):
    hi = v.astype(BF16).astype(F32)
    r1 = v - hi
    mid = r1.astype(BF16).astype(F32)
    lo = (r1 - mid).astype(BF16).astype(F32)
    return hi, mid, lo


def _dispatch_kernel(h_ref, route_ref, upper_ref, lower_ref, xl_ref, pos_ref, nb_ref):
    route = route_ref[0]
    e1 = route[0:1, :].astype(I32)
    e2 = route[1:2, :].astype(I32)
    erow = lax.broadcasted_iota(I32, (N_EXPERTS, CHUNK), 0)
    hit1 = erow == e1
    hit2 = erow == e2
    onehot = jnp.where(hit1, 1.0, 0.0) + jnp.where(hit2, 1.0, 0.0)
    cum = jnp.dot(onehot.astype(BF16), upper_ref[...], preferred_element_type=F32)
    cnt = jnp.sum(onehot, axis=1, keepdims=True)
    nblk = jnp.floor((cnt + (ROW_BLOCK - 1)) * (1.0 / ROW_BLOCK))
    nblk_b = jnp.broadcast_to(nblk, (N_EXPERTS, LANES))
    seg = jnp.dot(lower_ref[...], nblk_b.astype(BF16), preferred_element_type=F32) * ROW_BLOCK
    base = seg[:, 0:1] + cum
    pos1 = jnp.sum(jnp.where(hit1, base, 0.0), axis=0, keepdims=True)
    pos2 = jnp.sum(jnp.where(hit2, base, 0.0), axis=0, keepdims=True)
    p1i = pos1.astype(I32)
    p2i = pos2.astype(I32)
    used_rows = (jnp.sum(nblk) * ROW_BLOCK).astype(I32)

    prow = lax.broadcasted_iota(I32, (LANES, CHUNK), 0)
    parts = _bf16_parts(route[2:3, :]) + _bf16_parts(route[3:4, :])
    gpart_rows = jnp.zeros((LANES, CHUNK), F32)
    for j, part in enumerate(parts):
        gpart_rows = jnp.where(prow == j, part, gpart_rows)
    h_ext = jnp.concatenate([h_ref[...], gpart_rows.T.astype(BF16)], axis=1)
    glane = lax.broadcasted_iota(I32, (1, GATE_COLS), 1)
    one = jnp.ones((), BF16)
    zero = jnp.zeros((), BF16)
    r16 = lax.broadcasted_iota(I32, (DISP_ROWS, CHUNK), 0).astype(jnp.int16)

    def select(rc):
        q1 = (p1i - rc * DISP_ROWS).astype(jnp.int16)
        q2 = (p2i - rc * DISP_ROWS).astype(jnp.int16)
        sel1 = jnp.where(r16 == q1, one, zero)
        sel = jnp.where(r16 == q2, one, sel1)
        return sel, jnp.max(sel1, axis=1, keepdims=True)

    def permute(rc, sel, is_slot1):
        rows = pl.ds(rc * DISP_ROWS, DISP_ROWS)
        xg = jnp.dot(sel, h_ext, preferred_element_type=F32)
        xl_ref[rows, 0:D_MODEL] = xg[:, :D_MODEL].astype(BF16)
        g6 = xg[:, D_MODEL:]
        g3 = jnp.where(is_slot1.astype(F32) > 0.0, g6, pltpu.roll(g6, GATE_COLS - 3, axis=1))
        xl_ref[rows, D_MODEL:XL_COLS] = jnp.where(glane < 3, g3, 0.0).astype(BF16)

    n_always = 2 * CHUNK // DISP_ROWS + 1
    staged = select(0)
    for rc in range(n_always):
        nxt = select(rc + 1) if rc + 1 < n_always else None
        permute(rc, *staged)
        staged = nxt
    for rc in range(n_always, CHUNK_ROWS // DISP_ROWS):
        @pl.when(rc * DISP_ROWS < used_rows)
        def _(rc=rc):
            permute(rc, *select(rc))

        @pl.when(rc * DISP_ROWS >= used_rows)
        def _(rc=rc):
            xl_ref[pl.ds(rc * DISP_ROWS, DISP_ROWS), :] = jnp.zeros((DISP_ROWS, XL_COLS), BF16)

    pos_rows = jnp.where(prow == 0, pos1, jnp.where(prow == 1, pos2, 0.0))
    pos_ref[...] = pos_rows.T
    nb_ref[0] = nblk_b.astype(I32)


def _dispatch(h2, route, upper, lower):
    n = h2.shape[0]
    nch = n // CHUNK
    per_seq = route.shape[2] // CHUNK
    return pl.pallas_call(
        _dispatch_kernel,
        grid=(nch,),
        in_specs=[pl.BlockSpec((CHUNK, D_MODEL), lambda c: (c, 0)),
                  pl.BlockSpec((1, ROUTE_ROWS, CHUNK), lambda c: (c // per_seq, 0, c % per_seq)),
                  pl.BlockSpec(upper.shape, lambda c: (0, 0)),
                  pl.BlockSpec(lower.shape, lambda c: (0, 0))],
        out_specs=[pl.BlockSpec((CHUNK_ROWS, XL_COLS), lambda c: (c, 0)),
                   pl.BlockSpec((CHUNK, LANES), lambda c: (c, 0)),
                   pl.BlockSpec((1, N_EXPERTS, LANES), lambda c: (c, 0, 0))],
        out_shape=[jax.ShapeDtypeStruct((nch * CHUNK_ROWS, XL_COLS), BF16),
                   jax.ShapeDtypeStruct((n, LANES), F32),
                   jax.ShapeDtypeStruct((nch, N_EXPERTS, LANES), I32)],
        compiler_params=_cparams(("arbitrary",)),
        name="moe_dispatch",
    )(h2, route, upper, lower)


def _max_tiles(nch):
    max_blocks = nch * (2 * CHUNK + N_EXPERTS * (ROW_BLOCK - 1)) // ROW_BLOCK
    return max_blocks // TILE_BLOCKS + N_EXPERTS


def _masked_prefix(le, values):
    delta = values - jnp.concatenate([jnp.zeros((1,), values.dtype), values[:-1]])
    return jnp.sum(jnp.where(le, delta[None, :], 0), axis=1)


def _tile_plan(nb, nch):
    n_tiles = _max_tiles(nch)
    nbt = nb.T
    nbe = jnp.sum(nbt, axis=1)
    nte = (nbe + TILE_BLOCKS - 1) // TILE_BLOCKS
    tile_end = jnp.cumsum(nte)
    tile_start = tile_end - nte
    n_act = tile_end[-1]
    tiles = jnp.arange(n_tiles, dtype=I32)
    te = jnp.sum((tile_end[None, :] <= tiles[:, None]).astype(I32), axis=1)
    e_ar = jnp.arange(N_EXPERTS, dtype=I32)
    active = nte > 0
    te = jnp.where(tiles < n_act, te, jnp.max(jnp.where(active, e_ar, 0)))
    first = jnp.logical_and(te != jnp.concatenate([jnp.full((1,), -1, I32), te[:-1]]),
                            tiles < n_act).astype(I32)
    rank = jnp.cumsum(active.astype(I32)) - 1
    k_ar = jnp.arange(N_EXPERTS + W_SLOTS, dtype=I32)
    eseq = jnp.sum(jnp.where(jnp.logical_and(active[None, :], rank[None, :] == k_ar[:, None]),
                             e_ar[None, :], 0), axis=1)
    n_exp = jnp.sum(active.astype(I32))

    cb_excl = jnp.cumsum(nbt, axis=1) - nbt
    gs = TILE_BLOCKS * tile_start[:, None] + cb_excl
    seg_blk = jnp.cumsum(nb, axis=1) - nb
    base_blk = jnp.arange(nch, dtype=I32)[None, :] * CHUNK_BLOCKS + seg_blk.T
    gs_f = gs.reshape(-1)
    slots = jnp.arange(n_tiles * TILE_BLOCKS, dtype=I32)
    le = gs_f[None, :] <= slots[:, None]
    blk = slots + _masked_prefix(le, (base_blk - gs).reshape(-1))
    valid = slots < _masked_prefix(le, (gs + nbt).reshape(-1))
    gblk = jnp.concatenate([jnp.where(valid, blk, 0).astype(I32),
                            jnp.zeros(((X_SLOTS - 1) * TILE_BLOCKS,), I32)])

    m = jnp.arange(CHUNK_BLOCKS, dtype=I32)
    le_c = seg_blk[:, None, :] <= m[None, :, None]
    shift = (gs.T - seg_blk)
    delta = shift - jnp.concatenate([jnp.zeros((nch, 1), I32), shift[:, :-1]], axis=1)
    slot_of = m[None, :] + jnp.sum(jnp.where(le_c, delta[:, None, :], 0), axis=2)
    used_blocks = jnp.sum(nb, axis=1)
    slot_of = jnp.where(m[None, :] < used_blocks[:, None], slot_of, 0).astype(I32)
    return ((n_act.reshape(1).astype(I32), n_exp.reshape(1), first, eseq, gblk),
            slot_of.reshape(-1))


def _gmm_kernel(nact_ref, nexp_ref, first_ref, eseq_ref, gblk_ref,
                xl_hbm, wu_hbm, wd_hbm, yt_hbm,
                xbuf, ybuf, zbuf, wu_st, wd_st, in_sem, out_sem, w_sem, z_sem):
    n_act = nact_ref[0]
    n_exp = nexp_ref[0]
    n_tiles = yt_hbm.shape[0] // TILE_BLOCKS
    prefetch = X_SLOTS - 1

    def gather(tile, b):
        s = tile % X_SLOTS
        blk = gblk_ref[tile * TILE_BLOCKS + b]
        return pltpu.make_async_copy(xl_hbm.at[blk], xbuf.at[s, b], in_sem.at[s])

    def write_back(tile):
        s = tile % Y_SLOTS
        return pltpu.make_async_copy(ybuf.at[s], yt_hbm.at[pl.ds(tile * TILE_BLOCKS, TILE_BLOCKS)],
                                     out_sem.at[s])

    def zero_fill(tile):
        return pltpu.make_async_copy(zbuf, yt_hbm.at[pl.ds(tile * TILE_BLOCKS, TILE_BLOCKS)], z_sem)

    def weights(q):
        e = eseq_ref[q]
        s = q % W_SLOTS
        return (pltpu.make_async_copy(wu_hbm.at[e], wu_st.at[s], w_sem.at[0, s]),
                pltpu.make_async_copy(wd_hbm.at[e], wd_st.at[s], w_sem.at[1, s]))

    sub_blocks = SUB_ROWS // ROW_BLOCK

    def tile_blocks(tile, fn):
        del tile
        for b in range(TILE_BLOCKS):
            fn(b)

    for q0 in range(W_SLOTS - 1):
        @pl.when(q0 < n_exp)
        def _(q0=q0):
            for cp in weights(q0):
                cp.start()
    for t0 in range(prefetch):
        tile_blocks(t0, lambda b, t0=t0: gather(t0, b).start())

    def body(i, q):
        tile_blocks(i, lambda b: gather(i, b).wait())

        @pl.when(i >= Y_SLOTS)
        def _():
            write_back(i - Y_SLOTS).wait()

        is_first = first_ref[i] == 1

        @pl.when(is_first)
        def _():
            for cp in weights(q):
                cp.wait()

            @pl.when(q + W_SLOTS - 1 < n_exp)
            def _():
                for cp in weights(q + W_SLOTS - 1):
                    cp.start()

        q = q + is_first.astype(I32)
        ws = (q - 1) % W_SLOTS
        xs = i % X_SLOTS
        ys = i % Y_SLOTS

        def sub_tile(h):
            blocks = pl.ds(h * sub_blocks, sub_blocks)
            x = xbuf[xs, blocks].reshape(SUB_ROWS, XL_COLS)
            gate = jnp.sum(x[:, D_MODEL:].astype(F32), axis=1, keepdims=True)
            au = jnp.dot(x[:, :D_MODEL], wu_st[ws], preferred_element_type=F32)
            if h == 0:
                tile_blocks(i + prefetch, lambda b: gather(i + prefetch, b).start())
            a = au[:, :EXPERT_FF]
            act = (a * _sigmoid(a)) * au[:, EXPERT_FF:]
            y = jnp.dot(act.astype(BF16), wd_st[ws], preferred_element_type=F32)
            ybuf[ys, blocks] = (gate * y).astype(BF16).reshape(sub_blocks, ROW_BLOCK, D_MODEL)

        for h in range(TILE_ROWS // SUB_ROWS):
            sub_tile(h)
        write_back(i).start()

        @pl.when(n_act + i < n_tiles)
        def _():
            zero_fill(n_act + i).start()
        return q

    zbuf[...] = jnp.zeros_like(zbuf)
    lax.fori_loop(0, n_act, body, jnp.int32(0))

    def fill_rest(t, carry):
        zero_fill(t).start()
        return carry

    lax.fori_loop(jnp.minimum(2 * n_act, n_tiles), n_tiles, fill_rest, 0)

    def drain(t, carry):
        zero_fill(t).wait()
        return carry

    lax.fori_loop(n_act, n_tiles, drain, 0)
    for k in range(prefetch):
        tile_blocks(n_act + k, lambda b, k=k: gather(n_act + k, b).wait())
    for k in range(Y_SLOTS):
        @pl.when(n_act - 1 - k >= 0)
        def _(k=k):
            write_back(n_act - 1 - k).wait()


def _grouped_mlp(plan, xl, w_up, w_down, n_tiles):
    grid_spec = pltpu.PrefetchScalarGridSpec(
        num_scalar_prefetch=len(plan),
        grid=(1,),
        in_specs=[pl.BlockSpec(memory_space=pl.ANY)] * 3,
        out_specs=pl.BlockSpec(memory_space=pl.ANY),
        scratch_shapes=[pltpu.VMEM((X_SLOTS, TILE_BLOCKS, ROW_BLOCK, XL_COLS), BF16),
                        pltpu.VMEM((Y_SLOTS, TILE_BLOCKS, ROW_BLOCK, D_MODEL), BF16),
                        pltpu.VMEM((TILE_BLOCKS, ROW_BLOCK, D_MODEL), BF16),
                        pltpu.VMEM((W_SLOTS, D_MODEL, 2 * EXPERT_FF), BF16),
                        pltpu.VMEM((W_SLOTS, EXPERT_FF, D_MODEL), BF16),
                        pltpu.SemaphoreType.DMA((X_SLOTS,)),
                        pltpu.SemaphoreType.DMA((Y_SLOTS,)),
                        pltpu.SemaphoreType.DMA((2, W_SLOTS)),
                        pltpu.SemaphoreType.DMA(())])
    return pl.pallas_call(
        _gmm_kernel,
        grid_spec=grid_spec,
        out_shape=jax.ShapeDtypeStruct((n_tiles * TILE_BLOCKS, ROW_BLOCK, D_MODEL), BF16),
        compiler_params=_cparams(("arbitrary",)),
        name="moe_experts",
    )(*plan, xl, w_up, w_down)


COMMON_ROWS = 2 * CHUNK + DISP_ROWS
COMMON_BLOCKS = COMMON_ROWS // ROW_BLOCK


def _combine_kernel(used_ref, slot_ref, yt_hbm, pos_ref, x1_ref, gt2_ref, fg_ref, o_ref, ybuf, sem):
    c = pl.program_id(0)
    n_chunks = pl.num_programs(0)
    used_rows = used_ref[c]
    half = CHUNK // 2

    def fetch(chunk, b):
        s = chunk % 2
        slot = slot_ref[chunk * CHUNK_BLOCKS + b]
        return pltpu.make_async_copy(yt_hbm.at[slot], ybuf.at[s, b], sem.at[s])

    def chunk_blocks(chunk, fn):
        for b in range(COMMON_BLOCKS):
            fn(b)

        @pl.when(used_ref[chunk] > COMMON_ROWS)
        def _():
            for b in range(COMMON_BLOCKS, CHUNK_BLOCKS):
                fn(b)

    @pl.when(c == 0)
    def _():
        chunk_blocks(0, lambda b: fetch(0, b).start())

    @pl.when(c + 1 < n_chunks)
    def _():
        chunk_blocks(c + 1, lambda b: fetch(c + 1, b).start())

    chunk_blocks(c, lambda b: fetch(c, b).wait())

    def body(k_rows):
        sels = []
        for hf in range(2):
            pos = pos_ref[hf * half:(hf + 1) * half, :]
            p1 = pos[:, 0:1].astype(I32)
            p2 = pos[:, 1:2].astype(I32)
            r = lax.broadcasted_iota(I32, (half, k_rows), 1)
            sels.append(jnp.where(r == p1, 1.0, jnp.where(r == p2, 1.0, 0.0)).astype(BF16))
        yl = ybuf[c % 2, 0:k_rows // ROW_BLOCK].reshape(k_rows, D_MODEL)
        for hf in range(2):
            rows = pl.ds(hf * half, half)
            y = jnp.dot(sels[hf], yl, preferred_element_type=F32)
            x2 = x1_ref[rows, :] + gt2_ref[0] * y
            inv = lax.rsqrt(jnp.mean(x2 * x2, axis=-1, keepdims=True) + NORM_EPS)
            o_ref[rows, :] = (x2 * inv) * fg_ref[...]

    pl.when(used_rows <= COMMON_ROWS)(lambda: body(COMMON_ROWS))
    pl.when(used_rows > COMMON_ROWS)(lambda: body(CHUNK_ROWS))


def _combine(used_rows, slot_of, yt, pos_c, x1, mod3, final_g, seq):
    n = x1.shape[0]
    per_seq = seq // CHUNK
    grid_spec = pltpu.PrefetchScalarGridSpec(
        num_scalar_prefetch=2,
        grid=(n // CHUNK,),
        in_specs=[pl.BlockSpec(memory_space=pl.ANY),
                  pl.BlockSpec((CHUNK, LANES), lambda c, u, s: (c, 0)),
                  pl.BlockSpec((CHUNK, D_MODEL), lambda c, u, s: (c, 0)),
                  pl.BlockSpec((1, 1, D_MODEL), lambda c, u, s: (c // per_seq, 0, 5)),
                  pl.BlockSpec((1, D_MODEL), lambda c, u, s: (0, 0))],
        out_specs=pl.BlockSpec((CHUNK, D_MODEL), lambda c, u, s: (c, 0)),
        scratch_shapes=[pltpu.VMEM((2, CHUNK_BLOCKS, ROW_BLOCK, D_MODEL), BF16),
                        pltpu.SemaphoreType.DMA((2,))])
    return pl.pallas_call(
        _combine_kernel,
        grid_spec=grid_spec,
        out_shape=jax.ShapeDtypeStruct((n, D_MODEL), F32),
        compiler_params=_cparams(("arbitrary",)),
        name="moe_combine",
    )(used_rows, slot_of, yt, pos_c, x1, mod3, final_g)


def _rope_tables(seq):
    n_freq = HEAD_DIM // 4
    inv_freq = ROPE_BASE ** (-jnp.arange(n_freq, dtype=F32) / n_freq)
    rows = seq // GRID_W
    row = jnp.repeat(jnp.arange(rows, dtype=F32), GRID_W)
    col = jnp.tile(jnp.arange(GRID_W, dtype=F32), rows)
    ang = jnp.concatenate([row[:, None] * inv_freq, col[:, None] * inv_freq], axis=-1)
    cos, sin = jnp.cos(ang), jnp.sin(ang)
    return jnp.tile(cos, (1, 4)), jnp.concatenate([-sin, sin, -sin, sin], axis=1)


def _window_bias():
    key = np.arange(3 * BLOCK)[:, None]
    qry = np.arange(BLOCK)[None, :]
    valid = (key - qry >= 0) & (key - qry <= 2 * BLOCK)
    return jnp.asarray(np.where(valid, 0.0, NEG_INF), F32)


def kernel(x, c, ctx, c_ctx, w_ada, b_ada, norm1_g, w_in, w_conv, b_conv, w_a, w_b, sink, w_o,
           norm2_g, w_group, b_group, w_router, b_router, w_up, w_down, final_g):
    bsz, seq, _ = x.shape
    assert w_ada.shape[0] == 1 and seq % SEQ_TILE == 0

    cc = jnp.zeros((16, D_MODEL), F32).at[:bsz].set(c).at[bsz].set(c_ctx)
    mod = _modulation(cc, w_ada[0], b_ada[0][None, :])
    mod3 = mod[:bsz].reshape(bsz, 1, N_MOD * D_MODEL)
    csh1 = mod[bsz:bsz + 1, 0:D_MODEL]
    csc1 = mod[bsz:bsz + 1, D_MODEL:2 * D_MODEL]

    w = w_in[0]
    n1g = norm1_g[0][None, :]
    n2g = norm2_g[0][None, :]
    w_kv = w[:, OFF_K:OFF_GA].astype(BF16)
    w_a_cols = jnp.concatenate([w[:, OFF_CG:OFF_Q], w[:, OFF_K:OFF_GA]], axis=1).astype(BF16)
    w_b_cols = jnp.concatenate([w[:, OFF_B:OFF_CG], w[:, OFF_Q:OFF_K], w[:, OFF_GA:]],
                               axis=1).astype(BF16)
    cos_t, sin_t = _rope_tables(seq)

    k_ctx, vt_ctx = _context_kv(ctx, n1g, csh1, csc1, w_kv)
    uc, k_rot, v_t = _proj_a(x, n1g, mod3, w_a_cols, cos_t, sin_t)

    sink_row = jnp.repeat(sink[0].astype(F32), BLOCK)[None, :]
    pad_g = jnp.zeros((D_MODEL, GROUP_ROWS - N_GROUPS), F32)
    pad_e = jnp.zeros((D_MODEL, ROUTER_ROWS - GROUP_ROWS - N_EXPERTS), F32)
    wr_t = jnp.concatenate([w_group[0], pad_g, w_router[0], pad_e], axis=1).T.astype(BF16)
    br_col = jnp.concatenate([b_group[0], pad_g[0], b_router[0], pad_e[0]])[:, None]

    x1, h2, route, w_up_bf, w_down_bf = _mix(
        x, n1g, n2g, mod3, w_b_cols, cos_t, sin_t, uc, k_rot, v_t, k_ctx, vt_ctx,
        w_conv[0], b_conv[0][None, :], w_a[0].astype(BF16), w_b[0].astype(BF16),
        w_o[0].astype(BF16), sink_row, _window_bias(), wr_t, br_col, w_up[0], w_down[0])

    n = bsz * seq
    nch = n // CHUNK
    upper = jnp.asarray(np.triu(np.ones((CHUNK, CHUNK), np.float32), 1), BF16)
    lower = jnp.asarray(np.tril(np.ones((N_EXPERTS, N_EXPERTS), np.float32), -1), BF16)
    xl, pos_c, nb = _dispatch(h2.reshape(n, D_MODEL), route, upper, lower)
    nb = nb[:, :, 0]
    expert_plan, slot_of = _tile_plan(nb, nch)
    yt = _grouped_mlp(expert_plan, xl.reshape(-1, ROW_BLOCK, XL_COLS), w_up_bf, w_down_bf,
                      _max_tiles(nch))
    used_rows = (ROW_BLOCK * jnp.sum(nb, axis=1)).astype(I32)
    out = _combine(used_rows, slot_of, yt, pos_c, x1.reshape(n, D_MODEL), mod3, final_g[None, :],
                   seq)
    return out.reshape(bsz, seq, D_MODEL)
```

```python
import numpy as np
import jax
import jax.numpy as jnp
from jax import lax
from jax.experimental import pallas as pl
from jax.experimental.pallas import tpu as pltpu

F32 = jnp.float32
BF16 = jnp.bfloat16
I32 = jnp.int32

D_MODEL = 1024
GRID_W = 64
CONV_W = 512
N_HEADS = 8
N_KV_HEADS = 2
HEAD_DIM = 64
ATT_W = N_HEADS * HEAD_DIM
KV_W = N_KV_HEADS * HEAD_DIM
BLOCK = 128
ROPE_BASE = 10000.0
N_GROUPS = 4
EXPERTS_PER_GROUP = 8
N_EXPERTS = N_GROUPS * EXPERTS_PER_GROUP
EXPERT_FF = 256
N_MOD = 6
NORM_EPS = 1e-6
NEG_INF = -1e30
LOG2E = 1.4426950408889634
ONES_ROWS = 16

OFF_B, OFF_CG, OFF_XIN, OFF_Q, OFF_K, OFF_V, OFF_GA, OFF_GB = (
    0, 512, 1024, 1536, 2048, 2176, 2304, 3328)
IN_COLS = 4352

LANES = 128
SEQ_TILE = 512
BLOCKS_PER_TILE = SEQ_TILE // BLOCK
ROUTE_ROWS = 8
GROUP_ROWS = 8
ROUTER_ROWS = 48
VMEM_LIMIT = 56 * 1024 * 1024

CHUNK = SEQ_TILE
ROW_BLOCK = 16
CHUNK_ROWS = -(-(2 * CHUNK + N_EXPERTS * (ROW_BLOCK - 1)) // 256) * 256
CHUNK_BLOCKS = CHUNK_ROWS // ROW_BLOCK
GATE_COLS = LANES
XL_COLS = D_MODEL + GATE_COLS
DISP_ROWS = 256
TILE_BLOCKS = 32
TILE_ROWS = TILE_BLOCKS * ROW_BLOCK
SUB_ROWS = 256
X_SLOTS = 6
Y_SLOTS = 4
W_SLOTS = 3
WRITE_DMA_PRIORITY = 1


def _cparams(sem):
    return pltpu.CompilerParams(dimension_semantics=sem, vmem_limit_bytes=VMEM_LIMIT)


def _rms_mod(x, g, shift, scale):
    inv = lax.rsqrt(jnp.mean(x * x, axis=-1, keepdims=True) + NORM_EPS)
    return (x * inv) * (g * (1.0 + scale)) + shift


def _mod_kernel(c_ref, w_ref, b_ref, o_ref):
    c = c_ref[...]
    a = (c * jax.nn.sigmoid(c)).astype(BF16)
    o_ref[...] = jnp.dot(a, w_ref[...].astype(BF16), preferred_element_type=F32) + b_ref[...]


def _modulation(cc, w_ada, b_ada):
    rows = cc.shape[0]
    cols = w_ada.shape[1]
    tile = 1024
    return pl.pallas_call(
        _mod_kernel,
        grid=(cols // tile,),
        in_specs=[pl.BlockSpec((rows, D_MODEL), lambda j: (0, 0)),
                  pl.BlockSpec((D_MODEL, tile), lambda j: (0, j)),
                  pl.BlockSpec((1, tile), lambda j: (0, j))],
        out_specs=pl.BlockSpec((rows, tile), lambda j: (0, j)),
        out_shape=jax.ShapeDtypeStruct((rows, cols), F32),
        compiler_params=_cparams(("arbitrary",)),
        name="adaln_mod",
    )(cc, w_ada, b_ada)


def _ctx_kernel(x_ref, g_ref, sh_ref, sc_ref, w_ref, k_ref, vt_ref):
    h = _rms_mod(x_ref[0], g_ref[...], sh_ref[...], sc_ref[...]).astype(BF16)
    kv = jnp.dot(h, w_ref[...], preferred_element_type=F32)
    k_ref[0] = kv[:, :KV_W].astype(BF16)
    vt_ref[0] = kv[:, KV_W:].T.astype(BF16)


def _context_kv(ctx, norm_g, csh, csc, w_kv):
    bsz, clen, _ = ctx.shape
    return pl.pallas_call(
        _ctx_kernel,
        grid=(bsz,),
        in_specs=[pl.BlockSpec((1, clen, D_MODEL), lambda b: (b, 0, 0)),
                  pl.BlockSpec((1, D_MODEL), lambda b: (0, 0)),
                  pl.BlockSpec((1, D_MODEL), lambda b: (0, 0)),
                  pl.BlockSpec((1, D_MODEL), lambda b: (0, 0)),
                  pl.BlockSpec((D_MODEL, 2 * KV_W), lambda b: (0, 0))],
        out_specs=[pl.BlockSpec((1, clen, KV_W), lambda b: (b, 0, 0)),
                   pl.BlockSpec((1, KV_W, clen), lambda b: (b, 0, 0))],
        out_shape=[jax.ShapeDtypeStruct((bsz, clen, KV_W), BF16),
                   jax.ShapeDtypeStruct((bsz, KV_W, clen), BF16)],
        compiler_params=_cparams(("arbitrary",)),
        name="context_kv",
    )(ctx, norm_g, csh, csc, w_kv)


def _rope(t, cos, sin_signed):
    lane = lax.broadcasted_iota(I32, (1, LANES), 1)
    first_half = (lane % HEAD_DIM) < (HEAD_DIM // 2)
    outs = []
    for j in range(t.shape[1] // LANES):
        tj = t[:, j * LANES:(j + 1) * LANES]
        partner = jnp.where(first_half,
                            pltpu.roll(tj, LANES - HEAD_DIM // 2, axis=1),
                            pltpu.roll(tj, HEAD_DIM // 2, axis=1))
        outs.append(tj * cos + partner * sin_signed)
    return outs[0] if len(outs) == 1 else jnp.concatenate(outs, axis=1)


def _proj_a_kernel(x_ref, g_ref, sh_ref, sc_ref, w_ref, cos_ref, sin_ref, uc_ref, k_ref, vt_ref):
    half = SEQ_TILE // 2
    halves = [slice(hf * half, (hf + 1) * half) for hf in range(2)]
    hs = [_rms_mod(x_ref[0, rows, :], g_ref[...], sh_ref[0], sc_ref[0]).astype(BF16)
          for rows in halves]
    for h, rows in zip(hs, halves):
        u = jnp.dot(h, w_ref[...], preferred_element_type=F32)
        uc_ref[0, rows, :] = u[:, :CONV_W] * u[:, CONV_W:2 * CONV_W]
        k = u[:, 2 * CONV_W:2 * CONV_W + KV_W]
        k_ref[0, rows, :] = _rope(k, cos_ref[rows, :], sin_ref[rows, :]).astype(BF16)
        vt_ref[0, :, rows] = u[:, 2 * CONV_W + KV_W:].T.astype(BF16)


def _proj_a(x, norm_g, mod3, w_a_cols, cos_t, sin_t):
    bsz, seq, _ = x.shape
    nt = seq // SEQ_TILE
    wcols = w_a_cols.shape[1]
    return pl.pallas_call(
        _proj_a_kernel,
        grid=(nt, bsz),
        in_specs=[pl.BlockSpec((1, SEQ_TILE, D_MODEL), lambda t, b: (b, t, 0)),
                  pl.BlockSpec((1, D_MODEL), lambda t, b: (0, 0)),
                  pl.BlockSpec((1, 1, D_MODEL), lambda t, b: (b, 0, 0)),
                  pl.BlockSpec((1, 1, D_MODEL), lambda t, b: (b, 0, 1)),
                  pl.BlockSpec((D_MODEL, wcols), lambda t, b: (0, 0)),
                  pl.BlockSpec((SEQ_TILE, LANES), lambda t, b: (t, 0)),
                  pl.BlockSpec((SEQ_TILE, LANES), lambda t, b: (t, 0))],
        out_specs=[pl.BlockSpec((1, SEQ_TILE, CONV_W), lambda t, b: (b, t, 0)),
                   pl.BlockSpec((1, SEQ_TILE, KV_W), lambda t, b: (b, t, 0)),
                   pl.BlockSpec((1, KV_W, SEQ_TILE), lambda t, b: (b, 0, t))],
        out_shape=[jax.ShapeDtypeStruct((bsz, seq, CONV_W), F32),
                   jax.ShapeDtypeStruct((bsz, seq, KV_W), BF16),
                   jax.ShapeDtypeStruct((bsz, KV_W, seq), BF16)],
        compiler_params=_cparams(("arbitrary", "arbitrary")),
        name="proj_a",
    )(x, norm_g, mod3, mod3, w_a_cols, cos_t, sin_t)


def _sigmoid(x):
    return 1.0 / (1.0 + jnp.exp2(x * (-LOG2E)))


def _attn_scores(q_t, g, k_win, k_ctx, bias_p4, bias_n4):
    zeros = jnp.zeros((HEAD_DIM, 4 * BLOCK), BF16)
    qg = jnp.concatenate([q_t[(4 * g + h) * HEAD_DIM:(4 * g + h + 1) * HEAD_DIM, :]
                          for h in range(4)], axis=1)
    rhs = jnp.concatenate([qg, zeros] if g == 0 else [zeros, qg], axis=0)
    s_win = jnp.dot(k_win, rhs, preferred_element_type=F32)
    s_ctx = jnp.dot(k_ctx, rhs, preferred_element_type=F32)
    return (s_win[0:BLOCK] + bias_p4, s_win[BLOCK:2 * BLOCK], s_win[2 * BLOCK:] + bias_n4, s_ctx)


def _attn_probs(scores, sink):
    m = sink
    for s in scores:
        m = jnp.maximum(m, jnp.max(s, axis=0, keepdims=True))
    p_win = jnp.concatenate([jnp.exp2(s - m).astype(BF16) for s in scores[:3]], axis=0)
    p_ctx = jnp.exp2(scores[3] - m).astype(BF16)
    return p_win, p_ctx, m


def _attn_values(probs, sink, vt_win_g, vt_ctx_g):
    p_win, p_ctx, m = probs

    def with_ones(vt):
        r = lax.broadcasted_iota(I32, (ONES_ROWS, vt.shape[1]), 0)
        return jnp.concatenate([vt, jnp.where(r == 0, 1.0, 0.0).astype(BF16)], axis=0)

    o_ext = (jnp.dot(with_ones(vt_win_g), p_win, preferred_element_type=F32)
             + jnp.dot(with_ones(vt_ctx_g), p_ctx, preferred_element_type=F32))
    denom = o_ext[HEAD_DIM:HEAD_DIM + 1, :] + jnp.exp2(sink - m)
    return o_ext[:HEAD_DIM, :] / denom


def _route(logits_t):
    t = logits_t.shape[1]
    grow = lax.broadcasted_iota(I32, (GROUP_ROWS, t), 0)
    gl = jnp.where(grow < N_GROUPS, logits_t[0:GROUP_ROWS, :], NEG_INF)
    gm = jnp.max(gl, axis=0, keepdims=True)
    p_g = 1.0 / jnp.sum(jnp.exp(gl - gm), axis=0, keepdims=True)
    g_idx = jnp.min(jnp.where(gl == gm, grow, N_GROUPS), axis=0, keepdims=True)

    erow = lax.broadcasted_iota(I32, (N_EXPERTS, t), 0)
    el = logits_t[GROUP_ROWS:GROUP_ROWS + N_EXPERTS, :]
    sel = (erow // EXPERTS_PER_GROUP) == g_idx
    em = jnp.where(sel, el, NEG_INF)
    m1 = jnp.max(em, axis=0, keepdims=True)
    i1 = jnp.min(jnp.where(em == m1, erow, N_EXPERTS), axis=0, keepdims=True)
    em2 = jnp.where(erow == i1, NEG_INF, em)
    m2 = jnp.max(em2, axis=0, keepdims=True)
    i2 = jnp.min(jnp.where(em2 == m2, erow, N_EXPERTS), axis=0, keepdims=True)
    z = jnp.sum(jnp.where(sel, jnp.exp(el - m1), 0.0), axis=0, keepdims=True)
    p1 = 1.0 / z
    p2 = jnp.exp(m2 - m1) / z
    gate1 = p_g * p1 / (p1 + p2)
    gate2 = p_g * p2 / (p1 + p2)
    pad = jnp.zeros((ROUTE_ROWS - 4, t), F32)
    return jnp.concatenate([i1.astype(F32), i2.astype(F32), gate1, gate2, pad], axis=0)


def _mix_kernel(x_ref, g1n_ref, g2n_ref, sh1_ref, sc1_ref, gt1_ref, sh2_ref, sc2_ref,
                wb_ref, cos_ref, sin_ref,
                uc_ref, ucp_ref, ucn_ref, k_ref, kp_ref, kn_ref, vt_ref, vtp_ref, vtn_ref,
                kc_ref, vtc_ref, wconv_ref, bconv_ref, wa_ref, wbb_ref, wo_ref,
                sink_ref, bias_ref, wr_ref, br_ref, wu_ref, wd_ref,
                x1_ref, h2_ref, route_ref, wub_ref, wdb_ref):
    t_idx = pl.program_id(0)
    n_tiles = pl.num_programs(0)
    is_first = t_idx == 0
    is_last = t_idx == n_tiles - 1

    x = x_ref[0]
    half = SEQ_TILE // 2
    halves = [slice(hf * half, (hf + 1) * half) for hf in range(2)]

    uc = uc_ref[0]
    row = lax.broadcasted_iota(I32, (SEQ_TILE, 1), 0)
    prev_row = jnp.where(is_first, 0.0, ucp_ref[0][7:8, :])
    next_row = jnp.where(is_last, 0.0, ucn_ref[0][0:1, :])
    up = jnp.where(row == 0, prev_row, pltpu.roll(uc, 1, axis=0))
    dn = jnp.where(row == SEQ_TILE - 1, next_row, pltpu.roll(uc, SEQ_TILE - 1, axis=0))
    wconv = wconv_ref[...]
    y = bconv_ref[...] + (up * wconv[0:1, :] + uc * wconv[1:2, :] + dn * wconv[2:3, :])

    hbs = [_rms_mod(x[rows], g1n_ref[...], sh1_ref[0], sc1_ref[0]).astype(BF16) for rows in halves]
    yas, qs = [], []
    for hf, rows in enumerate(halves):
        bq = jnp.dot(hbs[hf], wb_ref[:, 0:2 * CONV_W], preferred_element_type=F32)
        yas.append(jnp.dot((bq[:, :CONV_W] * y[rows]).astype(BF16), wa_ref[...],
                           preferred_element_type=F32))
        qs.append(_rope(bq[:, CONV_W:], cos_ref[rows, :], sin_ref[rows, :])
                  * (HEAD_DIM ** -0.5 * LOG2E))
    hb = jnp.concatenate(hbs, axis=0)

    k_all = jnp.concatenate([kp_ref[0], k_ref[0], kn_ref[0]], axis=0)
    vt_all = jnp.concatenate([vtp_ref[0], vt_ref[0], vtn_ref[0]], axis=1)
    k_ctx = kc_ref[0]
    vt_ctx = vtc_ref[0]
    bias_prev = bias_ref[0:BLOCK, :]
    bias_next = bias_ref[2 * BLOCK:3 * BLOCK, :]
    sink_row = sink_ref[...] * LOG2E
    n_units = BLOCKS_PER_TILE * N_KV_HEADS
    gate_cols = 2 * D_MODEL // n_units
    gate_chunks = []

    def gate_chunk(u):
        c0 = 2 * CONV_W + u * gate_cols
        gate_chunks.append(jnp.dot(hb, wb_ref[:, c0:c0 + gate_cols], preferred_element_type=F32))

    q_ts, biases = [], []
    for jb in range(BLOCKS_PER_TILE):
        bias_p, bias_n = bias_prev, bias_next
        if jb == 0:
            bias_p = bias_prev + jnp.where(is_first, NEG_INF, 0.0)
        if jb == BLOCKS_PER_TILE - 1:
            bias_n = bias_next + jnp.where(is_last, NEG_INF, 0.0)
        biases.append((jnp.concatenate([bias_p] * 4, axis=1), jnp.concatenate([bias_n] * 4, axis=1)))
        hf, jh = divmod(jb, BLOCKS_PER_TILE // 2)
        q_ts.append(qs[hf][jh * BLOCK:(jh + 1) * BLOCK, :].T.astype(BF16))

    scores, probs, outs = {}, {}, {}
    for t in range(n_units + 2):
        if t == 3:
            wub_ref[...] = wu_ref[...].astype(BF16)
        if t == 5:
            wdb_ref[...] = wd_ref[...].astype(BF16)
        if t < n_units:
            jb, g = divmod(t, N_KV_HEADS)
            scores[t] = _attn_scores(q_ts[jb], g, k_all[jb * BLOCK:(jb + 3) * BLOCK, :], k_ctx,
                                     *biases[jb])
            gate_chunk(t)
        if 0 <= t - 1 < n_units:
            g = (t - 1) % N_KV_HEADS
            probs[t - 1] = _attn_probs(scores.pop(t - 1),
                                       sink_row[:, g * 4 * BLOCK:(g + 1) * 4 * BLOCK])
        if 0 <= t - 2 < n_units:
            jb, g = divmod(t - 2, N_KV_HEADS)
            outs[t - 2] = _attn_values(
                probs.pop(t - 2), sink_row[:, g * 4 * BLOCK:(g + 1) * 4 * BLOCK],
                vt_all[g * HEAD_DIM:(g + 1) * HEAD_DIM, jb * BLOCK:(jb + 3) * BLOCK],
                vt_ctx[g * HEAD_DIM:(g + 1) * HEAD_DIM, :])
    o_blocks = []
    for jb in range(BLOCKS_PER_TILE):
        o_rows = [outs[jb * N_KV_HEADS + g][:, h * BLOCK:(h + 1) * BLOCK]
                  for g in range(N_KV_HEADS) for h in range(4)]
        o_blocks.append(jnp.concatenate(o_rows, axis=0).T)
    gates = jnp.concatenate(gate_chunks, axis=1)

    half_blocks = BLOCKS_PER_TILE // 2
    for hf in range(2):
        r0, r1 = hf * half_blocks * BLOCK, (hf + 1) * half_blocks * BLOCK
        o = jnp.concatenate(o_blocks[hf * half_blocks:(hf + 1) * half_blocks], axis=0).astype(BF16)
        yb = jnp.dot(o, wbb_ref[...], preferred_element_type=F32)
        merged = (_sigmoid(gates[r0:r1, :D_MODEL]) * yas[hf]
                  + _sigmoid(gates[r0:r1, D_MODEL:]) * yb).astype(BF16)
        x1 = x[r0:r1] + gt1_ref[0] * jnp.dot(merged, wo_ref[...], preferred_element_type=F32)
        x1_ref[0, r0:r1, :] = x1
        h2 = _rms_mod(x1, g2n_ref[...], sh2_ref[0], sc2_ref[0]).astype(BF16)
        h2_ref[0, r0:r1, :] = h2
        logits_t = lax.dot_general(wr_ref[...], h2, (((1,), (1,)), ((), ())),
                                   preferred_element_type=F32) + br_ref[...]
        route_ref[0, :, r0:r1] = _route(logits_t)


def _mix(x, norm1_g, norm2_g, mod3, w_b_cols, cos_t, sin_t, uc, k_rot, v_t, k_ctx, vt_ctx,
         w_conv, b_conv, w_a, w_b, w_o, sink_row, bias, wr_t, br_col, w_up, w_down):
    bsz, seq, _ = x.shape
    nt = seq // SEQ_TILE
    nblk = seq // BLOCK
    clen = k_ctx.shape[1]
    rows8 = SEQ_TILE // 8
    assert N_EXPERTS % (nt * bsz) == 0, "each grid step converts an equal share of the experts"
    e_step = N_EXPERTS // (nt * bsz)

    def mod_spec(j):
        return pl.BlockSpec((1, 1, D_MODEL), lambda t, b, j=j: (b, 0, j))

    def const_spec(shape):
        return pl.BlockSpec(shape, lambda t, b: tuple(0 for _ in shape))

    in_specs = [
        pl.BlockSpec((1, SEQ_TILE, D_MODEL), lambda t, b: (b, t, 0)),
        const_spec((1, D_MODEL)), const_spec((1, D_MODEL)),
        mod_spec(0), mod_spec(1), mod_spec(2), mod_spec(3), mod_spec(4),
        const_spec(w_b_cols.shape),
        pl.BlockSpec((SEQ_TILE, LANES), lambda t, b: (t, 0)),
        pl.BlockSpec((SEQ_TILE, LANES), lambda t, b: (t, 0)),
        pl.BlockSpec((1, SEQ_TILE, CONV_W), lambda t, b: (b, t, 0)),
        pl.BlockSpec((1, 8, CONV_W), lambda t, b: (b, jnp.maximum(t * rows8 - 1, 0), 0)),
        pl.BlockSpec((1, 8, CONV_W),
                     lambda t, b: (b, jnp.minimum((t + 1) * rows8, seq // 8 - 1), 0)),
        pl.BlockSpec((1, SEQ_TILE, KV_W), lambda t, b: (b, t, 0)),
        pl.BlockSpec((1, BLOCK, KV_W),
                     lambda t, b: (b, jnp.maximum(t * BLOCKS_PER_TILE - 1, 0), 0)),
        pl.BlockSpec((1, BLOCK, KV_W),
                     lambda t, b: (b, jnp.minimum((t + 1) * BLOCKS_PER_TILE, nblk - 1), 0)),
        pl.BlockSpec((1, KV_W, SEQ_TILE), lambda t, b: (b, 0, t)),
        pl.BlockSpec((1, KV_W, BLOCK),
                     lambda t, b: (b, 0, jnp.maximum(t * BLOCKS_PER_TILE - 1, 0))),
        pl.BlockSpec((1, KV_W, BLOCK),
                     lambda t, b: (b, 0, jnp.minimum((t + 1) * BLOCKS_PER_TILE, nblk - 1))),
        pl.BlockSpec((1, clen, KV_W), lambda t, b: (b, 0, 0)),
        pl.BlockSpec((1, KV_W, clen), lambda t, b: (b, 0, 0)),
        const_spec(w_conv.shape), const_spec(b_conv.shape),
        const_spec(w_a.shape), const_spec(w_b.shape), const_spec(w_o.shape),
        const_spec(sink_row.shape), const_spec(bias.shape),
        const_spec(wr_t.shape), const_spec(br_col.shape),
        pl.BlockSpec((e_step,) + w_up.shape[1:], lambda t, b: (t * bsz + b, 0, 0)),
        pl.BlockSpec((e_step,) + w_down.shape[1:], lambda t, b: (t * bsz + b, 0, 0)),
    ]
    out_specs = [
        pl.BlockSpec((1, SEQ_TILE, D_MODEL), lambda t, b: (b, t, 0)),
        pl.BlockSpec((1, SEQ_TILE, D_MODEL), lambda t, b: (b, t, 0)),
        pl.BlockSpec((1, ROUTE_ROWS, SEQ_TILE), lambda t, b: (b, 0, t)),
        pl.BlockSpec((e_step,) + w_up.shape[1:], lambda t, b: (t * bsz + b, 0, 0)),
        pl.BlockSpec((e_step,) + w_down.shape[1:], lambda t, b: (t * bsz + b, 0, 0)),
    ]
    out_shape = [
        jax.ShapeDtypeStruct((bsz, seq, D_MODEL), F32),
        jax.ShapeDtypeStruct((bsz, seq, D_MODEL), BF16),
        jax.ShapeDtypeStruct((bsz, ROUTE_ROWS, seq), F32),
        jax.ShapeDtypeStruct(w_up.shape, BF16),
        jax.ShapeDtypeStruct(w_down.shape, BF16),
    ]
    return pl.pallas_call(
        _mix_kernel,
        grid=(nt, bsz),
        in_specs=in_specs,
        out_specs=out_specs,
        out_shape=out_shape,
        compiler_params=_cparams(("arbitrary", "arbitrary")),
        name="token_mix",
    )(x, norm1_g, norm2_g, mod3, mod3, mod3, mod3, mod3, w_b_cols, cos_t, sin_t,
      uc, uc, uc, k_rot, k_rot, k_rot, v_t, v_t, v_t, k_ctx, vt_ctx,
      w_conv, b_conv, w_a, w_b, w_o, sink_row, bias, wr_t, br_col, w_up, w_down)


def _bf16_parts(v):
    hi = v.astype(BF16).astype(F32)
    r1 = v - hi
    mid = r1.astype(BF16).astype(F32)
    lo = (r1 - mid).astype(BF16).astype(F32)
    return hi, mid, lo


def _dispatch_kernel(h_ref, route_ref, upper_ref, lower_ref, xl_ref, pos_ref, nb_ref):
    route = route_ref[0]
    e1 = route[0:1, :].astype(I32)
    e2 = route[1:2, :].astype(I32)
    erow = lax.broadcasted_iota(I32, (N_EXPERTS, CHUNK), 0)
    hit1 = erow == e1
    hit2 = erow == e2
    onehot = jnp.where(hit1, 1.0, 0.0) + jnp.where(hit2, 1.0, 0.0)
    cum = jnp.dot(onehot.astype(BF16), upper_ref[...], preferred_element_type=F32)
    cnt = jnp.sum(onehot, axis=1, keepdims=True)
    nblk = jnp.floor((cnt + (ROW_BLOCK - 1)) * (1.0 / ROW_BLOCK))
    nblk_b = jnp.broadcast_to(nblk, (N_EXPERTS, LANES))
    seg = jnp.dot(lower_ref[...], nblk_b.astype(BF16), preferred_element_type=F32) * ROW_BLOCK
    base = seg[:, 0:1] + cum
    pos1 = jnp.sum(jnp.where(hit1, base, 0.0), axis=0, keepdims=True)
    pos2 = jnp.sum(jnp.where(hit2, base, 0.0), axis=0, keepdims=True)
    p1i = pos1.astype(I32)
    p2i = pos2.astype(I32)
    used_rows = (jnp.sum(nblk) * ROW_BLOCK).astype(I32)

    prow = lax.broadcasted_iota(I32, (LANES, CHUNK), 0)
    parts = _bf16_parts(route[2:3, :]) + _bf16_parts(route[3:4, :])
    gpart_rows = jnp.zeros((LANES, CHUNK), F32)
    for j, part in enumerate(parts):
        gpart_rows = jnp.where(prow == j, part, gpart_rows)
    h_ext = jnp.concatenate([h_ref[...], gpart_rows.T.astype(BF16)], axis=1)
    glane = lax.broadcasted_iota(I32, (1, GATE_COLS), 1)
    one = jnp.ones((), BF16)
    zero = jnp.zeros((), BF16)
    r16 = lax.broadcasted_iota(I32, (DISP_ROWS, CHUNK), 0).astype(jnp.int16)

    def select(rc):
        q1 = (p1i - rc * DISP_ROWS).astype(jnp.int16)
        q2 = (p2i - rc * DISP_ROWS).astype(jnp.int16)
        sel1 = jnp.where(r16 == q1, one, zero)
        sel = jnp.where(r16 == q2, one, sel1)
        return sel, jnp.max(sel1, axis=1, keepdims=True)

    def permute(rc, sel, is_slot1):
        rows = pl.ds(rc * DISP_ROWS, DISP_ROWS)
        xg = jnp.dot(sel, h_ext, preferred_element_type=F32)
        xl_ref[rows, 0:D_MODEL] = xg[:, :D_MODEL].astype(BF16)
        g6 = xg[:, D_MODEL:]
        g3 = jnp.where(is_slot1.astype(F32) > 0.0, g6, pltpu.roll(g6, GATE_COLS - 3, axis=1))
        xl_ref[rows, D_MODEL:XL_COLS] = jnp.where(glane < 3, g3, 0.0).astype(BF16)

    n_always = 2 * CHUNK // DISP_ROWS + 1
    staged = select(0)
    for rc in range(n_always):
        nxt = select(rc + 1) if rc + 1 < n_always else None
        permute(rc, *staged)
        staged = nxt
    for rc in range(n_always, CHUNK_ROWS // DISP_ROWS):
        @pl.when(rc * DISP_ROWS < used_rows)
        def _(rc=rc):
            permute(rc, *select(rc))

        @pl.when(rc * DISP_ROWS >= used_rows)
        def _(rc=rc):
            xl_ref[pl.ds(rc * DISP_ROWS, DISP_ROWS), :] = jnp.zeros((DISP_ROWS, XL_COLS), BF16)

    pos_rows = jnp.where(prow == 0, pos1, jnp.where(prow == 1, pos2, 0.0))
    pos_ref[...] = pos_rows.T
    nb_ref[0] = nblk_b.astype(I32)


def _dispatch(h2, route, upper, lower):
    n = h2.shape[0]
    nch = n // CHUNK
    per_seq = route.shape[2] // CHUNK
    return pl.pallas_call(
        _dispatch_kernel,
        grid=(nch,),
        in_specs=[pl.BlockSpec((CHUNK, D_MODEL), lambda c: (c, 0)),
                  pl.BlockSpec((1, ROUTE_ROWS, CHUNK), lambda c: (c // per_seq, 0, c % per_seq)),
                  pl.BlockSpec(upper.shape, lambda c: (0, 0)),
                  pl.BlockSpec(lower.shape, lambda c: (0, 0))],
        out_specs=[pl.BlockSpec((CHUNK_ROWS, XL_COLS), lambda c: (c, 0)),
                   pl.BlockSpec((CHUNK, LANES), lambda c: (c, 0)),
                   pl.BlockSpec((1, N_EXPERTS, LANES), lambda c: (c, 0, 0))],
        out_shape=[jax.ShapeDtypeStruct((nch * CHUNK_ROWS, XL_COLS), BF16),
                   jax.ShapeDtypeStruct((n, LANES), F32),
                   jax.ShapeDtypeStruct((nch, N_EXPERTS, LANES), I32)],
        compiler_params=_cparams(("arbitrary",)),
        name="moe_dispatch",
    )(h2, route, upper, lower)


def _max_tiles(nch):
    max_blocks = nch * (2 * CHUNK + N_EXPERTS * (ROW_BLOCK - 1)) // ROW_BLOCK
    return max_blocks // TILE_BLOCKS + N_EXPERTS


def _masked_prefix(le, values):
    delta = values - jnp.concatenate([jnp.zeros((1,), values.dtype), values[:-1]])
    return jnp.sum(jnp.where(le, delta[None, :], 0), axis=1)


def _tile_plan(nb, nch):
    n_tiles = _max_tiles(nch)
    nbt = nb.T
    nbe = jnp.sum(nbt, axis=1)
    nte = (nbe + TILE_BLOCKS - 1) // TILE_BLOCKS
    tile_end = jnp.cumsum(nte)
    tile_start = tile_end - nte
    n_act = tile_end[-1]
    tiles = jnp.arange(n_tiles, dtype=I32)
    te = jnp.sum((tile_end[None, :] <= tiles[:, None]).astype(I32), axis=1)
    e_ar = jnp.arange(N_EXPERTS, dtype=I32)
    active = nte > 0
    te = jnp.where(tiles < n_act, te, jnp.max(jnp.where(active, e_ar, 0)))
    first = jnp.logical_and(te != jnp.concatenate([jnp.full((1,), -1, I32), te[:-1]]),
                            tiles < n_act).astype(I32)
    rank = jnp.cumsum(active.astype(I32)) - 1
    k_ar = jnp.arange(N_EXPERTS + W_SLOTS, dtype=I32)
    eseq = jnp.sum(jnp.where(jnp.logical_and(active[None, :], rank[None, :] == k_ar[:, None]),
                             e_ar[None, :], 0), axis=1)
    n_exp = jnp.sum(active.astype(I32))

    cb_excl = jnp.cumsum(nbt, axis=1) - nbt
    gs = TILE_BLOCKS * tile_start[:, None] + cb_excl
    seg_blk = jnp.cumsum(nb, axis=1) - nb
    base_blk = jnp.arange(nch, dtype=I32)[None, :] * CHUNK_BLOCKS + seg_blk.T
    gs_f = gs.reshape(-1)
    slots = jnp.arange(n_tiles * TILE_BLOCKS, dtype=I32)
    le = gs_f[None, :] <= slots[:, None]
    blk = slots + _masked_prefix(le, (base_blk - gs).reshape(-1))
    valid = slots < _masked_prefix(le, (gs + nbt).reshape(-1))
    gblk = jnp.concatenate([jnp.where(valid, blk, 0).astype(I32),
                            jnp.zeros(((X_SLOTS - 1) * TILE_BLOCKS,), I32)])

    m = jnp.arange(CHUNK_BLOCKS, dtype=I32)
    le_c = seg_blk[:, None, :] <= m[None, :, None]
    shift = (gs.T - seg_blk)
    delta = shift - jnp.concatenate([jnp.zeros((nch, 1), I32), shift[:, :-1]], axis=1)
    slot_of = m[None, :] + jnp.sum(jnp.where(le_c, delta[:, None, :], 0), axis=2)
    used_blocks = jnp.sum(nb, axis=1)
    slot_of = jnp.where(m[None, :] < used_blocks[:, None], slot_of, 0).astype(I32)
    return ((n_act.reshape(1).astype(I32), n_exp.reshape(1), first, eseq, gblk),
            slot_of.reshape(-1))


def _gmm_kernel(nact_ref, nexp_ref, first_ref, eseq_ref, gblk_ref,
                xl_hbm, wu_hbm, wd_hbm, yt_hbm,
                xbuf, ybuf, zbuf, wu_st, wd_st, in_sem, out_sem, w_sem, z_sem):
    n_act = nact_ref[0]
    n_exp = nexp_ref[0]
    n_tiles = yt_hbm.shape[0] // TILE_BLOCKS
    prefetch = X_SLOTS - 1

    def gather(tile, b):
        s = tile % X_SLOTS
        blk = gblk_ref[tile * TILE_BLOCKS + b]
        return pltpu.make_async_copy(xl_hbm.at[blk], xbuf.at[s, b], in_sem.at[s])

    def write_back(tile):
        s = tile % Y_SLOTS
        return pltpu.make_async_copy(ybuf.at[s], yt_hbm.at[pl.ds(tile * TILE_BLOCKS, TILE_BLOCKS)],
                                     out_sem.at[s])

    def zero_fill(tile):
        return pltpu.make_async_copy(zbuf, yt_hbm.at[pl.ds(tile * TILE_BLOCKS, TILE_BLOCKS)], z_sem)

    def weights(q):
        e = eseq_ref[q]
        s = q % W_SLOTS
        return (pltpu.make_async_copy(wu_hbm.at[e], wu_st.at[s], w_sem.at[0, s]),
                pltpu.make_async_copy(wd_hbm.at[e], wd_st.at[s], w_sem.at[1, s]))

    sub_blocks = SUB_ROWS // ROW_BLOCK

    def tile_blocks(tile, fn):
        del tile
        for b in range(TILE_BLOCKS):
            fn(b)

    for q0 in range(W_SLOTS - 1):
        @pl.when(q0 < n_exp)
        def _(q0=q0):
            for cp in weights(q0):
                cp.start()
    for t0 in range(prefetch):
        tile_blocks(t0, lambda b, t0=t0: gather(t0, b).start())

    def body(i, q):
        tile_blocks(i, lambda b: gather(i, b).wait())

        @pl.when(i >= Y_SLOTS)
        def _():
            write_back(i - Y_SLOTS).wait()

        is_first = first_ref[i] == 1

        @pl.when(is_first)
        def _():
            for cp in weights(q):
                cp.wait()

            @pl.when(q + W_SLOTS - 1 < n_exp)
            def _():
                for cp in weights(q + W_SLOTS - 1):
                    cp.start()

        q = q + is_first.astype(I32)
        ws = (q - 1) % W_SLOTS
        xs = i % X_SLOTS
        ys = i % Y_SLOTS

        def sub_tile(h):
            blocks = pl.ds(h * sub_blocks, sub_blocks)
            x = xbuf[xs, blocks].reshape(SUB_ROWS, XL_COLS)
            gate = jnp.sum(x[:, D_MODEL:].astype(F32), axis=1, keepdims=True)
            au = jnp.dot(x[:, :D_MODEL], wu_st[ws], preferred_element_type=F32)
            if h == 0:
                tile_blocks(i + prefetch, lambda b: gather(i + prefetch, b).start())
            a = au[:, :EXPERT_FF]
            act = (a * _sigmoid(a)) * au[:, EXPERT_FF:]
            y = jnp.dot(act.astype(BF16), wd_st[ws], preferred_element_type=F32)
            ybuf[ys, blocks] = (gate * y).astype(BF16).reshape(sub_blocks, ROW_BLOCK, D_MODEL)

        for h in range(TILE_ROWS // SUB_ROWS):
            sub_tile(h)
        write_back(i).start(priority=WRITE_DMA_PRIORITY)

        @pl.when(n_act + i < n_tiles)
        def _():
            zero_fill(n_act + i).start(priority=WRITE_DMA_PRIORITY)
        return q

    zbuf[...] = jnp.zeros_like(zbuf)
    lax.fori_loop(0, n_act, body, jnp.int32(0))

    def fill_rest(t, carry):
        zero_fill(t).start(priority=WRITE_DMA_PRIORITY)
        return carry

    lax.fori_loop(jnp.minimum(2 * n_act, n_tiles), n_tiles, fill_rest, 0)

    def drain(t, carry):
        zero_fill(t).wait()
        return carry

    lax.fori_loop(n_act, n_tiles, drain, 0)
    for k in range(prefetch):
        tile_blocks(n_act + k, lambda b, k=k: gather(n_act + k, b).wait())
    for k in range(Y_SLOTS):
        @pl.when(n_act - 1 - k >= 0)
        def _(k=k):
            write_back(n_act - 1 - k).wait()


def _grouped_mlp(plan, xl, w_up, w_down, n_tiles):
    grid_spec = pltpu.PrefetchScalarGridSpec(
        num_scalar_prefetch=len(plan),
        grid=(1,),
        in_specs=[pl.BlockSpec(memory_space=pl.ANY)] * 3,
        out_specs=pl.BlockSpec(memory_space=pl.ANY),
        scratch_shapes=[pltpu.VMEM((X_SLOTS, TILE_BLOCKS, ROW_BLOCK, XL_COLS), BF16),
                        pltpu.VMEM((Y_SLOTS, TILE_BLOCKS, ROW_BLOCK, D_MODEL), BF16),
                        pltpu.VMEM((TILE_BLOCKS, ROW_BLOCK, D_MODEL), BF16),
                        pltpu.VMEM((W_SLOTS, D_MODEL, 2 * EXPERT_FF), BF16),
                        pltpu.VMEM((W_SLOTS, EXPERT_FF, D_MODEL), BF16),
                        pltpu.SemaphoreType.DMA((X_SLOTS,)),
                        pltpu.SemaphoreType.DMA((Y_SLOTS,)),
                        pltpu.SemaphoreType.DMA((2, W_SLOTS)),
                        pltpu.SemaphoreType.DMA(())])
    return pl.pallas_call(
        _gmm_kernel,
        grid_spec=grid_spec,
        out_shape=jax.ShapeDtypeStruct((n_tiles * TILE_BLOCKS, ROW_BLOCK, D_MODEL), BF16),
        compiler_params=_cparams(("arbitrary",)),
        name="moe_experts",
    )(*plan, xl, w_up, w_down)


COMMON_ROWS = 2 * CHUNK + DISP_ROWS
COMMON_BLOCKS = COMMON_ROWS // ROW_BLOCK


def _combine_kernel(used_ref, slot_ref, yt_hbm, pos_ref, x1_ref, gt2_ref, fg_ref, o_ref, ybuf, sem):
    c = pl.program_id(0)
    n_chunks = pl.num_programs(0)
    used_rows = used_ref[c]
    half = CHUNK // 2

    def fetch(chunk, b):
        s = chunk % 2
        slot = slot_ref[chunk * CHUNK_BLOCKS + b]
        return pltpu.make_async_copy(yt_hbm.at[slot], ybuf.at[s, b], sem.at[s])

    def chunk_blocks(chunk, fn):
        for b in range(COMMON_BLOCKS):
            fn(b)

        @pl.when(used_ref[chunk] > COMMON_ROWS)
        def _():
            for b in range(COMMON_BLOCKS, CHUNK_BLOCKS):
                fn(b)

    @pl.when(c == 0)
    def _():
        chunk_blocks(0, lambda b: fetch(0, b).start())

    @pl.when(c + 1 < n_chunks)
    def _():
        chunk_blocks(c + 1, lambda b: fetch(c + 1, b).start())

    chunk_blocks(c, lambda b: fetch(c, b).wait())

    def body(k_rows):
        sels = []
        for hf in range(2):
            pos = pos_ref[hf * half:(hf + 1) * half, :]
            p1 = pos[:, 0:1].astype(I32)
            p2 = pos[:, 1:2].astype(I32)
            r = lax.broadcasted_iota(I32, (half, k_rows), 1)
            sels.append(jnp.where(r == p1, 1.0, jnp.where(r == p2, 1.0, 0.0)).astype(BF16))
        yl = ybuf[c % 2, 0:k_rows // ROW_BLOCK].reshape(k_rows, D_MODEL)
        for hf in range(2):
            rows = pl.ds(hf * half, half)
            y = jnp.dot(sels[hf], yl, preferred_element_type=F32)
            x2 = x1_ref[rows, :] + gt2_ref[0] * y
            inv = lax.rsqrt(jnp.mean(x2 * x2, axis=-1, keepdims=True) + NORM_EPS)
            o_ref[rows, :] = (x2 * inv) * fg_ref[...]

    pl.when(used_rows <= COMMON_ROWS)(lambda: body(COMMON_ROWS))
    pl.when(used_rows > COMMON_ROWS)(lambda: body(CHUNK_ROWS))


def _combine(used_rows, slot_of, yt, pos_c, x1, mod3, final_g, seq):
    n = x1.shape[0]
    per_seq = seq // CHUNK
    grid_spec = pltpu.PrefetchScalarGridSpec(
        num_scalar_prefetch=2,
        grid=(n // CHUNK,),
        in_specs=[pl.BlockSpec(memory_space=pl.ANY),
                  pl.BlockSpec((CHUNK, LANES), lambda c, u, s: (c, 0)),
                  pl.BlockSpec((CHUNK, D_MODEL), lambda c, u, s: (c, 0)),
                  pl.BlockSpec((1, 1, D_MODEL), lambda c, u, s: (c // per_seq, 0, 5)),
                  pl.BlockSpec((1, D_MODEL), lambda c, u, s: (0, 0))],
        out_specs=pl.BlockSpec((CHUNK, D_MODEL), lambda c, u, s: (c, 0)),
        scratch_shapes=[pltpu.VMEM((2, CHUNK_BLOCKS, ROW_BLOCK, D_MODEL), BF16),
                        pltpu.SemaphoreType.DMA((2,))])
    return pl.pallas_call(
        _combine_kernel,
        grid_spec=grid_spec,
        out_shape=jax.ShapeDtypeStruct((n, D_MODEL), F32),
        compiler_params=_cparams(("arbitrary",)),
        name="moe_combine",
    )(used_rows, slot_of, yt, pos_c, x1, mod3, final_g)


def _rope_tables(seq):
    n_freq = HEAD_DIM // 4
    inv_freq = ROPE_BASE ** (-jnp.arange(n_freq, dtype=F32) / n_freq)
    rows = seq // GRID_W
    row = jnp.repeat(jnp.arange(rows, dtype=F32), GRID_W)
    col = jnp.tile(jnp.arange(GRID_W, dtype=F32), rows)
    ang = jnp.concatenate([row[:, None] * inv_freq, col[:, None] * inv_freq], axis=-1)
    cos, sin = jnp.cos(ang), jnp.sin(ang)
    return jnp.tile(cos, (1, 4)), jnp.concatenate([-sin, sin, -sin, sin], axis=1)


def _window_bias():
    key = np.arange(3 * BLOCK)[:, None]
    qry = np.arange(BLOCK)[None, :]
    valid = (key - qry >= 0) & (key - qry <= 2 * BLOCK)
    return jnp.asarray(np.where(valid, 0.0, NEG_INF), F32)


def kernel(x, c, ctx, c_ctx, w_ada, b_ada, norm1_g, w_in, w_conv, b_conv, w_a, w_b, sink, w_o,
           norm2_g, w_group, b_group, w_router, b_router, w_up, w_down, final_g):
    bsz, seq, _ = x.shape
    assert w_ada.shape[0] == 1 and seq % SEQ_TILE == 0

    cc = jnp.zeros((16, D_MODEL), F32).at[:bsz].set(c).at[bsz].set(c_ctx)
    mod = _modulation(cc, w_ada[0], b_ada[0][None, :])
    mod3 = mod[:bsz].reshape(bsz, 1, N_MOD * D_MODEL)
    csh1 = mod[bsz:bsz + 1, 0:D_MODEL]
    csc1 = mod[bsz:bsz + 1, D_MODEL:2 * D_MODEL]

    w = w_in[0]
    n1g = norm1_g[0][None, :]
    n2g = norm2_g[0][None, :]
    w_kv = w[:, OFF_K:OFF_GA].astype(BF16)
    w_a_cols = jnp.concatenate([w[:, OFF_CG:OFF_Q], w[:, OFF_K:OFF_GA]], axis=1).astype(BF16)
    w_b_cols = jnp.concatenate([w[:, OFF_B:OFF_CG], w[:, OFF_Q:OFF_K], w[:, OFF_GA:]],
                               axis=1).astype(BF16)
    cos_t, sin_t = _rope_tables(seq)

    k_ctx, vt_ctx = _context_kv(ctx, n1g, csh1, csc1, w_kv)
    uc, k_rot, v_t = _proj_a(x, n1g, mod3, w_a_cols, cos_t, sin_t)

    sink_row = jnp.repeat(sink[0].astype(F32), BLOCK)[None, :]
    pad_g = jnp.zeros((D_MODEL, GROUP_ROWS - N_GROUPS), F32)
    pad_e = jnp.zeros((D_MODEL, ROUTER_ROWS - GROUP_ROWS - N_EXPERTS), F32)
    wr_t = jnp.concatenate([w_group[0], pad_g, w_router[0], pad_e], axis=1).T.astype(BF16)
    br_col = jnp.concatenate([b_group[0], pad_g[0], b_router[0], pad_e[0]])[:, None]

    x1, h2, route, w_up_bf, w_down_bf = _mix(
        x, n1g, n2g, mod3, w_b_cols, cos_t, sin_t, uc, k_rot, v_t, k_ctx, vt_ctx,
        w_conv[0], b_conv[0][None, :], w_a[0].astype(BF16), w_b[0].astype(BF16),
        w_o[0].astype(BF16), sink_row, _window_bias(), wr_t, br_col, w_up[0], w_down[0])

    n = bsz * seq
    nch = n // CHUNK
    upper = jnp.asarray(np.triu(np.ones((CHUNK, CHUNK), np.float32), 1), BF16)
    lower = jnp.asarray(np.tril(np.ones((N_EXPERTS, N_EXPERTS), np.float32), -1), BF16)
    xl, pos_c, nb = _dispatch(h2.reshape(n, D_MODEL), route, upper, lower)
    nb = nb[:, :, 0]
    expert_plan, slot_of = _tile_plan(nb, nch)
    yt = _grouped_mlp(expert_plan, xl.reshape(-1, ROW_BLOCK, XL_COLS), w_up_bf, w_down_bf,
                      _max_tiles(nch))
    used_rows = (ROW_BLOCK * jnp.sum(nb, axis=1)).astype(I32)
    out = _combine(used_rows, slot_of, yt, pos_c, x1.reshape(n, D_MODEL), mod3, final_g[None, :],
                   seq)
    return out.reshape(bsz, seq, D_MODEL)
```

```python
import numpy as np
import jax
import jax.numpy as jnp
from jax import lax
from jax.experimental import pallas as pl
from jax.experimental.pallas import tpu as pltpu

F32 = jnp.float32
BF16 = jnp.bfloat16
I32 = jnp.int32

D_MODEL = 1024
GRID_W = 64
CONV_W = 512
N_HEADS = 8
N_KV_HEADS = 2
HEAD_DIM = 64
ATT_W = N_HEADS * HEAD_DIM
KV_W = N_KV_HEADS * HEAD_DIM
BLOCK = 128
ROPE_BASE = 10000.0
N_GROUPS = 4
EXPERTS_PER_GROUP = 8
N_EXPERTS = N_GROUPS * EXPERTS_PER_GROUP
EXPERT_FF = 256
N_MOD = 6
NORM_EPS = 1e-6
NEG_INF = -1e30
LOG2E = 1.4426950408889634
ONES_ROWS = 16

OFF_B, OFF_CG, OFF_XIN, OFF_Q, OFF_K, OFF_V, OFF_GA, OFF_GB = (
    0, 512, 1024, 1536, 2048, 2176, 2304, 3328)
IN_COLS = 4352

LANES = 128
SEQ_TILE = 512
BLOCKS_PER_TILE = SEQ_TILE // BLOCK
ROUTE_ROWS = 8
GROUP_ROWS = 8
ROUTER_ROWS = 48
VMEM_LIMIT = 56 * 1024 * 1024

CHUNK = SEQ_TILE
ROW_BLOCK = 16
CHUNK_ROWS = -(-(2 * CHUNK + N_EXPERTS * (ROW_BLOCK - 1)) // 256) * 256
CHUNK_BLOCKS = CHUNK_ROWS // ROW_BLOCK
GATE_COLS = LANES
XL_COLS = D_MODEL + GATE_COLS
DISP_ROWS = 256
TILE_BLOCKS = 32
TILE_ROWS = TILE_BLOCKS * ROW_BLOCK
SUB_ROWS = 512
X_SLOTS = 6
Y_SLOTS = 4
W_SLOTS = 3


def _cparams(sem):
    return pltpu.CompilerParams(dimension_semantics=sem, vmem_limit_bytes=VMEM_LIMIT)


def _rms_mod(x, g, shift, scale):
    inv = lax.rsqrt(jnp.mean(x * x, axis=-1, keepdims=True) + NORM_EPS)
    return (x * inv) * (g * (1.0 + scale)) + shift


def _mod_kernel(c_ref, w_ref, b_ref, o_ref):
    c = c_ref[...]
    a = (c * jax.nn.sigmoid(c)).astype(BF16)
    o_ref[...] = jnp.dot(a, w_ref[...].astype(BF16), preferred_element_type=F32) + b_ref[...]


def _modulation(cc, w_ada, b_ada):
    rows = cc.shape[0]
    cols = w_ada.shape[1]
    tile = 1024
    return pl.pallas_call(
        _mod_kernel,
        grid=(cols // tile,),
        in_specs=[pl.BlockSpec((rows, D_MODEL), lambda j: (0, 0)),
                  pl.BlockSpec((D_MODEL, tile), lambda j: (0, j)),
                  pl.BlockSpec((1, tile), lambda j: (0, j))],
        out_specs=pl.BlockSpec((rows, tile), lambda j: (0, j)),
        out_shape=jax.ShapeDtypeStruct((rows, cols), F32),
        compiler_params=_cparams(("arbitrary",)),
        name="adaln_mod",
    )(cc, w_ada, b_ada)


def _ctx_kernel(x_ref, g_ref, sh_ref, sc_ref, w_ref, k_ref, vt_ref):
    h = _rms_mod(x_ref[0], g_ref[...], sh_ref[...], sc_ref[...]).astype(BF16)
    kv = jnp.dot(h, w_ref[...], preferred_element_type=F32)
    k_ref[0] = kv[:, :KV_W].astype(BF16)
    vt_ref[0] = kv[:, KV_W:].T.astype(BF16)


def _context_kv(ctx, norm_g, csh, csc, w_kv):
    bsz, clen, _ = ctx.shape
    return pl.pallas_call(
        _ctx_kernel,
        grid=(bsz,),
        in_specs=[pl.BlockSpec((1, clen, D_MODEL), lambda b: (b, 0, 0)),
                  pl.BlockSpec((1, D_MODEL), lambda b: (0, 0)),
                  pl.BlockSpec((1, D_MODEL), lambda b: (0, 0)),
                  pl.BlockSpec((1, D_MODEL), lambda b: (0, 0)),
                  pl.BlockSpec((D_MODEL, 2 * KV_W), lambda b: (0, 0))],
        out_specs=[pl.BlockSpec((1, clen, KV_W), lambda b: (b, 0, 0)),
                   pl.BlockSpec((1, KV_W, clen), lambda b: (b, 0, 0))],
        out_shape=[jax.ShapeDtypeStruct((bsz, clen, KV_W), BF16),
                   jax.ShapeDtypeStruct((bsz, KV_W, clen), BF16)],
        compiler_params=_cparams(("arbitrary",)),
        name="context_kv",
    )(ctx, norm_g, csh, csc, w_kv)


def _rope(t, cos, sin_signed):
    lane = lax.broadcasted_iota(I32, (1, LANES), 1)
    first_half = (lane % HEAD_DIM) < (HEAD_DIM // 2)
    outs = []
    for j in range(t.shape[1] // LANES):
        tj = t[:, j * LANES:(j + 1) * LANES]
        partner = jnp.where(first_half,
                            pltpu.roll(tj, LANES - HEAD_DIM // 2, axis=1),
                            pltpu.roll(tj, HEAD_DIM // 2, axis=1))
        outs.append(tj * cos + partner * sin_signed)
    return outs[0] if len(outs) == 1 else jnp.concatenate(outs, axis=1)


def _proj_a_kernel(x_ref, g_ref, sh_ref, sc_ref, w_ref, cos_ref, sin_ref, uc_ref, k_ref, vt_ref):
    half = SEQ_TILE // 2
    halves = [slice(hf * half, (hf + 1) * half) for hf in range(2)]
    hs = [_rms_mod(x_ref[0, rows, :], g_ref[...], sh_ref[0], sc_ref[0]).astype(BF16)
          for rows in halves]
    for h, rows in zip(hs, halves):
        u = jnp.dot(h, w_ref[...], preferred_element_type=F32)
        uc_ref[0, rows, :] = u[:, :CONV_W] * u[:, CONV_W:2 * CONV_W]
        k = u[:, 2 * CONV_W:2 * CONV_W + KV_W]
        k_ref[0, rows, :] = _rope(k, cos_ref[rows, :], sin_ref[rows, :]).astype(BF16)
        vt_ref[0, :, rows] = u[:, 2 * CONV_W + KV_W:].T.astype(BF16)


def _proj_a(x, norm_g, mod3, w_a_cols, cos_t, sin_t):
    bsz, seq, _ = x.shape
    nt = seq // SEQ_TILE
    wcols = w_a_cols.shape[1]
    return pl.pallas_call(
        _proj_a_kernel,
        grid=(nt, bsz),
        in_specs=[pl.BlockSpec((1, SEQ_TILE, D_MODEL), lambda t, b: (b, t, 0)),
                  pl.BlockSpec((1, D_MODEL), lambda t, b: (0, 0)),
                  pl.BlockSpec((1, 1, D_MODEL), lambda t, b: (b, 0, 0)),
                  pl.BlockSpec((1, 1, D_MODEL), lambda t, b: (b, 0, 1)),
                  pl.BlockSpec((D_MODEL, wcols), lambda t, b: (0, 0)),
                  pl.BlockSpec((SEQ_TILE, LANES), lambda t, b: (t, 0)),
                  pl.BlockSpec((SEQ_TILE, LANES), lambda t, b: (t, 0))],
        out_specs=[pl.BlockSpec((1, SEQ_TILE, CONV_W), lambda t, b: (b, t, 0)),
                   pl.BlockSpec((1, SEQ_TILE, KV_W), lambda t, b: (b, t, 0)),
                   pl.BlockSpec((1, KV_W, SEQ_TILE), lambda t, b: (b, 0, t))],
        out_shape=[jax.ShapeDtypeStruct((bsz, seq, CONV_W), F32),
                   jax.ShapeDtypeStruct((bsz, seq, KV_W), BF16),
                   jax.ShapeDtypeStruct((bsz, KV_W, seq), BF16)],
        compiler_params=_cparams(("arbitrary", "arbitrary")),
        name="proj_a",
    )(x, norm_g, mod3, mod3, w_a_cols, cos_t, sin_t)


def _sigmoid(x):
    return 1.0 / (1.0 + jnp.exp2(x * (-LOG2E)))


def _attn_scores(q_t, g, k_win, k_ctx, bias_p4, bias_n4):
    zeros = jnp.zeros((HEAD_DIM, 4 * BLOCK), BF16)
    qg = jnp.concatenate([q_t[(4 * g + h) * HEAD_DIM:(4 * g + h + 1) * HEAD_DIM, :]
                          for h in range(4)], axis=1)
    rhs = jnp.concatenate([qg, zeros] if g == 0 else [zeros, qg], axis=0)
    s_win = jnp.dot(k_win, rhs, preferred_element_type=F32)
    s_ctx = jnp.dot(k_ctx, rhs, preferred_element_type=F32)
    return (s_win[0:BLOCK] + bias_p4, s_win[BLOCK:2 * BLOCK], s_win[2 * BLOCK:] + bias_n4, s_ctx)


def _attn_probs(scores, sink):
    m = sink
    for s in scores:
        m = jnp.maximum(m, jnp.max(s, axis=0, keepdims=True))
    p_win = jnp.concatenate([jnp.exp2(s - m).astype(BF16) for s in scores[:3]], axis=0)
    p_ctx = jnp.exp2(scores[3] - m).astype(BF16)
    return p_win, p_ctx, m


def _attn_values(probs, sink, vt_win_g, vt_ctx_g):
    p_win, p_ctx, m = probs

    def with_ones(vt):
        r = lax.broadcasted_iota(I32, (ONES_ROWS, vt.shape[1]), 0)
        return jnp.concatenate([vt, jnp.where(r == 0, 1.0, 0.0).astype(BF16)], axis=0)

    o_ext = (jnp.dot(with_ones(vt_win_g), p_win, preferred_element_type=F32)
             + jnp.dot(with_ones(vt_ctx_g), p_ctx, preferred_element_type=F32))
    denom = o_ext[HEAD_DIM:HEAD_DIM + 1, :] + jnp.exp2(sink - m)
    return o_ext[:HEAD_DIM, :] / denom


def _route(logits_t):
    t = logits_t.shape[1]
    grow = lax.broadcasted_iota(I32, (GROUP_ROWS, t), 0)
    gl = jnp.where(grow < N_GROUPS, logits_t[0:GROUP_ROWS, :], NEG_INF)
    gm = jnp.max(gl, axis=0, keepdims=True)
    p_g = 1.0 / jnp.sum(jnp.exp(gl - gm), axis=0, keepdims=True)
    g_idx = jnp.min(jnp.where(gl == gm, grow, N_GROUPS), axis=0, keepdims=True)

    erow = lax.broadcasted_iota(I32, (N_EXPERTS, t), 0)
    el = logits_t[GROUP_ROWS:GROUP_ROWS + N_EXPERTS, :]
    sel = (erow // EXPERTS_PER_GROUP) == g_idx
    em = jnp.where(sel, el, NEG_INF)
    m1 = jnp.max(em, axis=0, keepdims=True)
    i1 = jnp.min(jnp.where(em == m1, erow, N_EXPERTS), axis=0, keepdims=True)
    em2 = jnp.where(erow == i1, NEG_INF, em)
    m2 = jnp.max(em2, axis=0, keepdims=True)
    i2 = jnp.min(jnp.where(em2 == m2, erow, N_EXPERTS), axis=0, keepdims=True)
    z = jnp.sum(jnp.where(sel, jnp.exp(el - m1), 0.0), axis=0, keepdims=True)
    p1 = 1.0 / z
    p2 = jnp.exp(m2 - m1) / z
    gate1 = p_g * p1 / (p1 + p2)
    gate2 = p_g * p2 / (p1 + p2)
    pad = jnp.zeros((ROUTE_ROWS - 4, t), F32)
    return jnp.concatenate([i1.astype(F32), i2.astype(F32), gate1, gate2, pad], axis=0)


def _mix_kernel(x_ref, g1n_ref, g2n_ref, sh1_ref, sc1_ref, gt1_ref, sh2_ref, sc2_ref,
                wb_ref, cos_ref, sin_ref,
                uc_ref, ucp_ref, ucn_ref, k_ref, kp_ref, kn_ref, vt_ref, vtp_ref, vtn_ref,
                kc_ref, vtc_ref, wconv_ref, bconv_ref, wa_ref, wbb_ref, wo_ref,
                sink_ref, bias_ref, wr_ref, br_ref, wu_ref, wd_ref,
                x1_ref, h2_ref, route_ref, wub_ref, wdb_ref):
    t_idx = pl.program_id(0)
    n_tiles = pl.num_programs(0)
    is_first = t_idx == 0
    is_last = t_idx == n_tiles - 1

    x = x_ref[0]
    half = SEQ_TILE // 2
    halves = [slice(hf * half, (hf + 1) * half) for hf in range(2)]

    uc = uc_ref[0]
    row = lax.broadcasted_iota(I32, (SEQ_TILE, 1), 0)
    prev_row = jnp.where(is_first, 0.0, ucp_ref[0][7:8, :])
    next_row = jnp.where(is_last, 0.0, ucn_ref[0][0:1, :])
    up = jnp.where(row == 0, prev_row, pltpu.roll(uc, 1, axis=0))
    dn = jnp.where(row == SEQ_TILE - 1, next_row, pltpu.roll(uc, SEQ_TILE - 1, axis=0))
    wconv = wconv_ref[...]
    y = bconv_ref[...] + (up * wconv[0:1, :] + uc * wconv[1:2, :] + dn * wconv[2:3, :])

    hbs = [_rms_mod(x[rows], g1n_ref[...], sh1_ref[0], sc1_ref[0]).astype(BF16) for rows in halves]
    yas, qs = [], []
    for hf, rows in enumerate(halves):
        bq = jnp.dot(hbs[hf], wb_ref[:, 0:2 * CONV_W], preferred_element_type=F32)
        yas.append(jnp.dot((bq[:, :CONV_W] * y[rows]).astype(BF16), wa_ref[...],
                           preferred_element_type=F32))
        qs.append(_rope(bq[:, CONV_W:], cos_ref[rows, :], sin_ref[rows, :])
                  * (HEAD_DIM ** -0.5 * LOG2E))
    hb = jnp.concatenate(hbs, axis=0)

    k_all = jnp.concatenate([kp_ref[0], k_ref[0], kn_ref[0]], axis=0)
    vt_all = jnp.concatenate([vtp_ref[0], vt_ref[0], vtn_ref[0]], axis=1)
    k_ctx = kc_ref[0]
    vt_ctx = vtc_ref[0]
    bias_prev = bias_ref[0:BLOCK, :]
    bias_next = bias_ref[2 * BLOCK:3 * BLOCK, :]
    sink_row = sink_ref[...] * LOG2E
    n_units = BLOCKS_PER_TILE * N_KV_HEADS
    gate_cols = 2 * D_MODEL // n_units
    gate_chunks = []

    def gate_chunk(u):
        c0 = 2 * CONV_W + u * gate_cols
        gate_chunks.append(jnp.dot(hb, wb_ref[:, c0:c0 + gate_cols], preferred_element_type=F32))

    q_ts, biases = [], []
    for jb in range(BLOCKS_PER_TILE):
        bias_p, bias_n = bias_prev, bias_next
        if jb == 0:
            bias_p = bias_prev + jnp.where(is_first, NEG_INF, 0.0)
        if jb == BLOCKS_PER_TILE - 1:
            bias_n = bias_next + jnp.where(is_last, NEG_INF, 0.0)
        biases.append((jnp.concatenate([bias_p] * 4, axis=1), jnp.concatenate([bias_n] * 4, axis=1)))
        hf, jh = divmod(jb, BLOCKS_PER_TILE // 2)
        q_ts.append(qs[hf][jh * BLOCK:(jh + 1) * BLOCK, :].T.astype(BF16))

    scores, probs, outs = {}, {}, {}
    for t in range(n_units + 2):
        if t == 3:
            wub_ref[...] = wu_ref[...].astype(BF16)
        if t == 5:
            wdb_ref[...] = wd_ref[...].astype(BF16)
        if t < n_units:
            jb, g = divmod(t, N_KV_HEADS)
            scores[t] = _attn_scores(q_ts[jb], g, k_all[jb * BLOCK:(jb + 3) * BLOCK, :], k_ctx,
                                     *biases[jb])
            gate_chunk(t)
        if 0 <= t - 1 < n_units:
            g = (t - 1) % N_KV_HEADS
            probs[t - 1] = _attn_probs(scores.pop(t - 1),
                                       sink_row[:, g * 4 * BLOCK:(g + 1) * 4 * BLOCK])
        if 0 <= t - 2 < n_units:
            jb, g = divmod(t - 2, N_KV_HEADS)
            outs[t - 2] = _attn_values(
                probs.pop(t - 2), sink_row[:, g * 4 * BLOCK:(g + 1) * 4 * BLOCK],
                vt_all[g * HEAD_DIM:(g + 1) * HEAD_DIM, jb * BLOCK:(jb + 3) * BLOCK],
                vt_ctx[g * HEAD_DIM:(g + 1) * HEAD_DIM, :])
    o_blocks = []
    for jb in range(BLOCKS_PER_TILE):
        o_rows = [outs[jb * N_KV_HEADS + g][:, h * BLOCK:(h + 1) * BLOCK]
                  for g in range(N_KV_HEADS) for h in range(4)]
        o_blocks.append(jnp.concatenate(o_rows, axis=0).T)
    gates = jnp.concatenate(gate_chunks, axis=1)

    half_blocks = BLOCKS_PER_TILE // 2
    for hf in range(2):
        r0, r1 = hf * half_blocks * BLOCK, (hf + 1) * half_blocks * BLOCK
        o = jnp.concatenate(o_blocks[hf * half_blocks:(hf + 1) * half_blocks], axis=0).astype(BF16)
        yb = jnp.dot(o, wbb_ref[...], preferred_element_type=F32)
        merged = (_sigmoid(gates[r0:r1, :D_MODEL]) * yas[hf]
                  + _sigmoid(gates[r0:r1, D_MODEL:]) * yb).astype(BF16)
        x1 = x[r0:r1] + gt1_ref[0] * jnp.dot(merged, wo_ref[...], preferred_element_type=F32)
        x1_ref[0, r0:r1, :] = x1
        h2 = _rms_mod(x1, g2n_ref[...], sh2_ref[0], sc2_ref[0]).astype(BF16)
        h2_ref[0, r0:r1, :] = h2
        logits_t = lax.dot_general(wr_ref[...], h2, (((1,), (1,)), ((), ())),
                                   preferred_element_type=F32) + br_ref[...]
        route_ref[0, :, r0:r1] = _route(logits_t)


def _mix(x, norm1_g, norm2_g, mod3, w_b_cols, cos_t, sin_t, uc, k_rot, v_t, k_ctx, vt_ctx,
         w_conv, b_conv, w_a, w_b, w_o, sink_row, bias, wr_t, br_col, w_up, w_down):
    bsz, seq, _ = x.shape
    nt = seq // SEQ_TILE
    nblk = seq // BLOCK
    clen = k_ctx.shape[1]
    rows8 = SEQ_TILE // 8
    assert N_EXPERTS % (nt * bsz) == 0, "each grid step converts an equal share of the experts"
    e_step = N_EXPERTS // (nt * bsz)

    def mod_spec(j):
        return pl.BlockSpec((1, 1, D_MODEL), lambda t, b, j=j: (b, 0, j))

    def const_spec(shape):
        return pl.BlockSpec(shape, lambda t, b: tuple(0 for _ in shape))

    in_specs = [
        pl.BlockSpec((1, SEQ_TILE, D_MODEL), lambda t, b: (b, t, 0)),
        const_spec((1, D_MODEL)), const_spec((1, D_MODEL)),
        mod_spec(0), mod_spec(1), mod_spec(2), mod_spec(3), mod_spec(4),
        const_spec(w_b_cols.shape),
        pl.BlockSpec((SEQ_TILE, LANES), lambda t, b: (t, 0)),
        pl.BlockSpec((SEQ_TILE, LANES), lambda t, b: (t, 0)),
        pl.BlockSpec((1, SEQ_TILE, CONV_W), lambda t, b: (b, t, 0)),
        pl.BlockSpec((1, 8, CONV_W), lambda t, b: (b, jnp.maximum(t * rows8 - 1, 0), 0)),
        pl.BlockSpec((1, 8, CONV_W),
                     lambda t, b: (b, jnp.minimum((t + 1) * rows8, seq // 8 - 1), 0)),
        pl.BlockSpec((1, SEQ_TILE, KV_W), lambda t, b: (b, t, 0)),
        pl.BlockSpec((1, BLOCK, KV_W),
                     lambda t, b: (b, jnp.maximum(t * BLOCKS_PER_TILE - 1, 0), 0)),
        pl.BlockSpec((1, BLOCK, KV_W),
                     lambda t, b: (b, jnp.minimum((t + 1) * BLOCKS_PER_TILE, nblk - 1), 0)),
        pl.BlockSpec((1, KV_W, SEQ_TILE), lambda t, b: (b, 0, t)),
        pl.BlockSpec((1, KV_W, BLOCK),
                     lambda t, b: (b, 0, jnp.maximum(t * BLOCKS_PER_TILE - 1, 0))),
        pl.BlockSpec((1, KV_W, BLOCK),
                     lambda t, b: (b, 0, jnp.minimum((t + 1) * BLOCKS_PER_TILE, nblk - 1))),
        pl.BlockSpec((1, clen, KV_W), lambda t, b: (b, 0, 0)),
        pl.BlockSpec((1, KV_W, clen), lambda t, b: (b, 0, 0)),
        const_spec(w_conv.shape), const_spec(b_conv.shape),
        const_spec(w_a.shape), const_spec(w_b.shape), const_spec(w_o.shape),
        const_spec(sink_row.shape), const_spec(bias.shape),
        const_spec(wr_t.shape), const_spec(br_col.shape),
        pl.BlockSpec((e_step,) + w_up.shape[1:], lambda t, b: (t * bsz + b, 0, 0)),
        pl.BlockSpec((e_step,) + w_down.shape[1:], lambda t, b: (t * bsz + b, 0, 0)),
    ]
    out_specs = [
        pl.BlockSpec((1, SEQ_TILE, D_MODEL), lambda t, b: (b, t, 0)),
        pl.BlockSpec((1, SEQ_TILE, D_MODEL), lambda t, b: (b, t, 0)),
        pl.BlockSpec((1, ROUTE_ROWS, SEQ_TILE), lambda t, b: (b, 0, t)),
        pl.BlockSpec((e_step,) + w_up.shape[1:], lambda t, b: (t * bsz + b, 0, 0)),
        pl.BlockSpec((e_step,) + w_down.shape[1:], lambda t, b: (t * bsz + b, 0, 0)),
    ]
    out_shape = [
        jax.ShapeDtypeStruct((bsz, seq, D_MODEL), F32),
        jax.ShapeDtypeStruct((bsz, seq, D_MODEL), BF16),
        jax.ShapeDtypeStruct((bsz, ROUTE_ROWS, seq), F32),
        jax.ShapeDtypeStruct(w_up.shape, BF16),
        jax.ShapeDtypeStruct(w_down.shape, BF16),
    ]
    return pl.pallas_call(
        _mix_kernel,
        grid=(nt, bsz),
        in_specs=in_specs,
        out_specs=out_specs,
        out_shape=out_shape,
        compiler_params=_cparams(("arbitrary", "arbitrary")),
        name="token_mix",
    )(x, norm1_g, norm2_g, mod3, mod3, mod3, mod3, mod3, w_b_cols, cos_t, sin_t,
      uc, uc, uc, k_rot, k_rot, k_rot, v_t, v_t, v_t, k_ctx, vt_ctx,
      w_conv, b_conv, w_a, w_b, w_o, sink_row, bias, wr_t, br_col, w_up, w_down)


def _bf16_parts(v):
    hi = v.astype(BF16).astype(F32)
    r1 = v - hi
    mid = r1.astype(BF16).astype(F32)
    lo = (r1 - mid).astype(BF16).astype(F32)
    return hi, mid, lo


def _dispatch_kernel(h_ref, route_ref, upper_ref, lower_ref, xl_ref, pos_ref, nb_ref):
    route = route_ref[0]
    e1 = route[0:1, :].astype(I32)
    e2 = route[1:2, :].astype(I32)
    erow = lax.broadcasted_iota(I32, (N_EXPERTS, CHUNK), 0)
    hit1 = erow == e1
    hit2 = erow == e2
    onehot = jnp.where(hit1, 1.0, 0.0) + jnp.where(hit2, 1.0, 0.0)
    cum = jnp.dot(onehot.astype(BF16), upper_ref[...], preferred_element_type=F32)
    cnt = jnp.sum(onehot, axis=1, keepdims=True)
    nblk = jnp.floor((cnt + (ROW_BLOCK - 1)) * (1.0 / ROW_BLOCK))
    nblk_b = jnp.broadcast_to(nblk, (N_EXPERTS, LANES))
    seg = jnp.dot(lower_ref[...], nblk_b.astype(BF16), preferred_element_type=F32) * ROW_BLOCK
    base = seg[:, 0:1] + cum
    pos1 = jnp.sum(jnp.where(hit1, base, 0.0), axis=0, keepdims=True)
    pos2 = jnp.sum(jnp.where(hit2, base, 0.0), axis=0, keepdims=True)
    p1i = pos1.astype(I32)
    p2i = pos2.astype(I32)
    used_rows = (jnp.sum(nblk) * ROW_BLOCK).astype(I32)

    prow = lax.broadcasted_iota(I32, (LANES, CHUNK), 0)
    parts = _bf16_parts(route[2:3, :]) + _bf16_parts(route[3:4, :])
    gpart_rows = jnp.zeros((LANES, CHUNK), F32)
    for j, part in enumerate(parts):
        gpart_rows = jnp.where(prow == j, part, gpart_rows)
    h_ext = jnp.concatenate([h_ref[...], gpart_rows.T.astype(BF16)], axis=1)
    glane = lax.broadcasted_iota(I32, (1, GATE_COLS), 1)
    one = jnp.ones((), BF16)
    zero = jnp.zeros((), BF16)
    r16 = lax.broadcasted_iota(I32, (DISP_ROWS, CHUNK), 0).astype(jnp.int16)

    def select(rc):
        q1 = (p1i - rc * DISP_ROWS).astype(jnp.int16)
        q2 = (p2i - rc * DISP_ROWS).astype(jnp.int16)
        sel1 = jnp.where(r16 == q1, one, zero)
        sel = jnp.where(r16 == q2, one, sel1)
        return sel, jnp.max(sel1, axis=1, keepdims=True)

    def permute(rc, sel, is_slot1):
        rows = pl.ds(rc * DISP_ROWS, DISP_ROWS)
        xg = jnp.dot(sel, h_ext, preferred_element_type=F32)
        xl_ref[rows, 0:D_MODEL] = xg[:, :D_MODEL].astype(BF16)
        g6 = xg[:, D_MODEL:]
        g3 = jnp.where(is_slot1.astype(F32) > 0.0, g6, pltpu.roll(g6, GATE_COLS - 3, axis=1))
        xl_ref[rows, D_MODEL:XL_COLS] = jnp.where(glane < 3, g3, 0.0).astype(BF16)

    n_always = 2 * CHUNK // DISP_ROWS + 1
    staged = select(0)
    for rc in range(n_always):
        nxt = select(rc + 1) if rc + 1 < n_always else None
        permute(rc, *staged)
        staged = nxt
    for rc in range(n_always, CHUNK_ROWS // DISP_ROWS):
        @pl.when(rc * DISP_ROWS < used_rows)
        def _(rc=rc):
            permute(rc, *select(rc))

        @pl.when(rc * DISP_ROWS >= used_rows)
        def _(rc=rc):
            xl_ref[pl.ds(rc * DISP_ROWS, DISP_ROWS), :] = jnp.zeros((DISP_ROWS, XL_COLS), BF16)

    pos_rows = jnp.where(prow == 0, pos1, jnp.where(prow == 1, pos2, 0.0))
    pos_ref[...] = pos_rows.T
    nb_ref[0] = nblk_b.astype(I32)


def _dispatch(h2, route, upper, lower):
    n = h2.shape[0]
    nch = n // CHUNK
    per_seq = route.shape[2] // CHUNK
    return pl.pallas_call(
        _dispatch_kernel,
        grid=(nch,),
        in_specs=[pl.BlockSpec((CHUNK, D_MODEL), lambda c: (c, 0)),
                  pl.BlockSpec((1, ROUTE_ROWS, CHUNK), lambda c: (c // per_seq, 0, c % per_seq)),
                  pl.BlockSpec(upper.shape, lambda c: (0, 0)),
                  pl.BlockSpec(lower.shape, lambda c: (0, 0))],
        out_specs=[pl.BlockSpec((CHUNK_ROWS, XL_COLS), lambda c: (c, 0)),
                   pl.BlockSpec((CHUNK, LANES), lambda c: (c, 0)),
                   pl.BlockSpec((1, N_EXPERTS, LANES), lambda c: (c, 0, 0))],
        out_shape=[jax.ShapeDtypeStruct((nch * CHUNK_ROWS, XL_COLS), BF16),
                   jax.ShapeDtypeStruct((n, LANES), F32),
                   jax.ShapeDtypeStruct((nch, N_EXPERTS, LANES), I32)],
        compiler_params=_cparams(("arbitrary",)),
        name="moe_dispatch",
    )(h2, route, upper, lower)


def _max_tiles(nch):
    max_blocks = nch * (2 * CHUNK + N_EXPERTS * (ROW_BLOCK - 1)) // ROW_BLOCK
    return max_blocks // TILE_BLOCKS + N_EXPERTS


def _masked_prefix(le, values):
    delta = values - jnp.concatenate([jnp.zeros((1,), values.dtype), values[:-1]])
    return jnp.sum(jnp.where(le, delta[None, :], 0), axis=1)


def _tile_plan(nb, nch):
    n_tiles = _max_tiles(nch)
    nbt = nb.T
    nbe = jnp.sum(nbt, axis=1)
    nte = (nbe + TILE_BLOCKS - 1) // TILE_BLOCKS
    tile_end = jnp.cumsum(nte)
    tile_start = tile_end - nte
    n_act = tile_end[-1]
    tiles = jnp.arange(n_tiles, dtype=I32)
    te = jnp.sum((tile_end[None, :] <= tiles[:, None]).astype(I32), axis=1)
    e_ar = jnp.arange(N_EXPERTS, dtype=I32)
    active = nte > 0
    te = jnp.where(tiles < n_act, te, jnp.max(jnp.where(active, e_ar, 0)))
    first = jnp.logical_and(te != jnp.concatenate([jnp.full((1,), -1, I32), te[:-1]]),
                            tiles < n_act).astype(I32)
    rank = jnp.cumsum(active.astype(I32)) - 1
    k_ar = jnp.arange(N_EXPERTS + W_SLOTS, dtype=I32)
    eseq = jnp.sum(jnp.where(jnp.logical_and(active[None, :], rank[None, :] == k_ar[:, None]),
                             e_ar[None, :], 0), axis=1)
    n_exp = jnp.sum(active.astype(I32))

    cb_excl = jnp.cumsum(nbt, axis=1) - nbt
    gs = TILE_BLOCKS * tile_start[:, None] + cb_excl
    seg_blk = jnp.cumsum(nb, axis=1) - nb
    base_blk = jnp.arange(nch, dtype=I32)[None, :] * CHUNK_BLOCKS + seg_blk.T
    gs_f = gs.reshape(-1)
    slots = jnp.arange(n_tiles * TILE_BLOCKS, dtype=I32)
    le = gs_f[None, :] <= slots[:, None]
    blk = slots + _masked_prefix(le, (base_blk - gs).reshape(-1))
    valid = slots < _masked_prefix(le, (gs + nbt).reshape(-1))
    gblk = jnp.concatenate([jnp.where(valid, blk, 0).astype(I32),
                            jnp.zeros(((X_SLOTS - 1) * TILE_BLOCKS,), I32)])

    m = jnp.arange(CHUNK_BLOCKS, dtype=I32)
    le_c = seg_blk[:, None, :] <= m[None, :, None]
    shift = (gs.T - seg_blk)
    delta = shift - jnp.concatenate([jnp.zeros((nch, 1), I32), shift[:, :-1]], axis=1)
    slot_of = m[None, :] + jnp.sum(jnp.where(le_c, delta[:, None, :], 0), axis=2)
    used_blocks = jnp.sum(nb, axis=1)
    slot_of = jnp.where(m[None, :] < used_blocks[:, None], slot_of, 0).astype(I32)
    return ((n_act.reshape(1).astype(I32), n_exp.reshape(1), first, eseq, gblk),
            slot_of.reshape(-1))


def _gmm_kernel(nact_ref, nexp_ref, first_ref, eseq_ref, gblk_ref,
                xl_hbm, wu_hbm, wd_hbm, yt_hbm,
                xbuf, ybuf, zbuf, wu_st, wd_st, in_sem, out_sem, w_sem, z_sem):
    n_act = nact_ref[0]
    n_exp = nexp_ref[0]
    n_tiles = yt_hbm.shape[0] // TILE_BLOCKS
    prefetch = X_SLOTS - 1

    def gather(tile, b):
        s = tile % X_SLOTS
        blk = gblk_ref[tile * TILE_BLOCKS + b]
        return pltpu.make_async_copy(xl_hbm.at[blk], xbuf.at[s, b], in_sem.at[s])

    def write_back(tile):
        s = tile % Y_SLOTS
        return pltpu.make_async_copy(ybuf.at[s], yt_hbm.at[pl.ds(tile * TILE_BLOCKS, TILE_BLOCKS)],
                                     out_sem.at[s])

    def zero_fill(tile):
        return pltpu.make_async_copy(zbuf, yt_hbm.at[pl.ds(tile * TILE_BLOCKS, TILE_BLOCKS)], z_sem)

    def weights(q):
        e = eseq_ref[q]
        s = q % W_SLOTS
        return (pltpu.make_async_copy(wu_hbm.at[e], wu_st.at[s], w_sem.at[0, s]),
                pltpu.make_async_copy(wd_hbm.at[e], wd_st.at[s], w_sem.at[1, s]))

    sub_blocks = SUB_ROWS // ROW_BLOCK

    def tile_blocks(tile, fn):
        del tile
        for b in range(TILE_BLOCKS):
            fn(b)

    for q0 in range(W_SLOTS - 1):
        @pl.when(q0 < n_exp)
        def _(q0=q0):
            for cp in weights(q0):
                cp.start()
    for t0 in range(prefetch):
        tile_blocks(t0, lambda b, t0=t0: gather(t0, b).start())

    def body(i, q):
        tile_blocks(i, lambda b: gather(i, b).wait())

        @pl.when(i >= Y_SLOTS)
        def _():
            write_back(i - Y_SLOTS).wait()

        is_first = first_ref[i] == 1

        @pl.when(is_first)
        def _():
            for cp in weights(q):
                cp.wait()

            @pl.when(q + W_SLOTS - 1 < n_exp)
            def _():
                for cp in weights(q + W_SLOTS - 1):
                    cp.start()

        q = q + is_first.astype(I32)
        ws = (q - 1) % W_SLOTS
        xs = i % X_SLOTS
        ys = i % Y_SLOTS

        def sub_tile(h):
            blocks = pl.ds(h * sub_blocks, sub_blocks)
            x = xbuf[xs, blocks].reshape(SUB_ROWS, XL_COLS)
            gate = jnp.sum(x[:, D_MODEL:].astype(F32), axis=1, keepdims=True)
            au = jnp.dot(x[:, :D_MODEL], wu_st[ws], preferred_element_type=F32)
            if h == 0:
                tile_blocks(i + prefetch, lambda b: gather(i + prefetch, b).start())
            a = au[:, :EXPERT_FF]
            act = (a * _sigmoid(a)) * au[:, EXPERT_FF:]
            y = jnp.dot(act.astype(BF16), wd_st[ws], preferred_element_type=F32)
            ybuf[ys, blocks] = (gate * y).astype(BF16).reshape(sub_blocks, ROW_BLOCK, D_MODEL)

        for h in range(TILE_ROWS // SUB_ROWS):
            sub_tile(h)
        write_back(i).start()

        @pl.when(n_act + i < n_tiles)
        def _():
            zero_fill(n_act + i).start()
        return q

    zbuf[...] = jnp.zeros_like(zbuf)
    lax.fori_loop(0, n_act, body, jnp.int32(0))

    def fill_rest(t, carry):
        zero_fill(t).start()
        return carry

    lax.fori_loop(jnp.minimum(2 * n_act, n_tiles), n_tiles, fill_rest, 0)

    def drain(t, carry):
        zero_fill(t).wait()
        return carry

    lax.fori_loop(n_act, n_tiles, drain, 0)
    for k in range(prefetch):
        tile_blocks(n_act + k, lambda b, k=k: gather(n_act + k, b).wait())
    for k in range(Y_SLOTS):
        @pl.when(n_act - 1 - k >= 0)
        def _(k=k):
            write_back(n_act - 1 - k).wait()


def _grouped_mlp(plan, xl, w_up, w_down, n_tiles):
    grid_spec = pltpu.PrefetchScalarGridSpec(
        num_scalar_prefetch=len(plan),
        grid=(1,),
        in_specs=[pl.BlockSpec(memory_space=pl.ANY)] * 3,
        out_specs=pl.BlockSpec(memory_space=pl.ANY),
        scratch_shapes=[pltpu.VMEM((X_SLOTS, TILE_BLOCKS, ROW_BLOCK, XL_COLS), BF16),
                        pltpu.VMEM((Y_SLOTS, TILE_BLOCKS, ROW_BLOCK, D_MODEL), BF16),
                        pltpu.VMEM((TILE_BLOCKS, ROW_BLOCK, D_MODEL), BF16),
                        pltpu.VMEM((W_SLOTS, D_MODEL, 2 * EXPERT_FF), BF16),
                        pltpu.VMEM((W_SLOTS, EXPERT_FF, D_MODEL), BF16),
                        pltpu.SemaphoreType.DMA((X_SLOTS,)),
                        pltpu.SemaphoreType.DMA((Y_SLOTS,)),
                        pltpu.SemaphoreType.DMA((2, W_SLOTS)),
                        pltpu.SemaphoreType.DMA(())])
    return pl.pallas_call(
        _gmm_kernel,
        grid_spec=grid_spec,
        out_shape=jax.ShapeDtypeStruct((n_tiles * TILE_BLOCKS, ROW_BLOCK, D_MODEL), BF16),
        compiler_params=_cparams(("arbitrary",)),
        name="moe_experts",
    )(*plan, xl, w_up, w_down)


COMMON_ROWS = 2 * CHUNK + DISP_ROWS
COMMON_BLOCKS = COMMON_ROWS // ROW_BLOCK


def _combine_kernel(used_ref, slot_ref, yt_hbm, pos_ref, x1_ref, gt2_ref, fg_ref, o_ref, ybuf, sem):
    c = pl.program_id(0)
    n_chunks = pl.num_programs(0)
    used_rows = used_ref[c]
    half = CHUNK // 2

    def fetch(chunk, b):
        s = chunk % 2
        slot = slot_ref[chunk * CHUNK_BLOCKS + b]
        return pltpu.make_async_copy(yt_hbm.at[slot], ybuf.at[s, b], sem.at[s])

    def chunk_blocks(chunk, fn):
        for b in range(COMMON_BLOCKS):
            fn(b)

        @pl.when(used_ref[chunk] > COMMON_ROWS)
        def _():
            for b in range(COMMON_BLOCKS, CHUNK_BLOCKS):
                fn(b)

    @pl.when(c == 0)
    def _():
        chunk_blocks(0, lambda b: fetch(0, b).start())

    @pl.when(c + 1 < n_chunks)
    def _():
        chunk_blocks(c + 1, lambda b: fetch(c + 1, b).start())

    chunk_blocks(c, lambda b: fetch(c, b).wait())

    def body(k_rows):
        sels = []
        for hf in range(2):
            pos = pos_ref[hf * half:(hf + 1) * half, :]
            p1 = pos[:, 0:1].astype(I32)
            p2 = pos[:, 1:2].astype(I32)
            r = lax.broadcasted_iota(I32, (half, k_rows), 1)
            sels.append(jnp.where(r == p1, 1.0, jnp.where(r == p2, 1.0, 0.0)).astype(BF16))
        yl = ybuf[c % 2, 0:k_rows // ROW_BLOCK].reshape(k_rows, D_MODEL)
        for hf in range(2):
            rows = pl.ds(hf * half, half)
            y = jnp.dot(sels[hf], yl, preferred_element_type=F32)
            x2 = x1_ref[rows, :] + gt2_ref[0] * y
            inv = lax.rsqrt(jnp.mean(x2 * x2, axis=-1, keepdims=True) + NORM_EPS)
            o_ref[rows, :] = (x2 * inv) * fg_ref[...]

    pl.when(used_rows <= COMMON_ROWS)(lambda: body(COMMON_ROWS))
    pl.when(used_rows > COMMON_ROWS)(lambda: body(CHUNK_ROWS))


def _combine(used_rows, slot_of, yt, pos_c, x1, mod3, final_g, seq):
    n = x1.shape[0]
    per_seq = seq // CHUNK
    grid_spec = pltpu.PrefetchScalarGridSpec(
        num_scalar_prefetch=2,
        grid=(n // CHUNK,),
        in_specs=[pl.BlockSpec(memory_space=pl.ANY),
                  pl.BlockSpec((CHUNK, LANES), lambda c, u, s: (c, 0)),
                  pl.BlockSpec((CHUNK, D_MODEL), lambda c, u, s: (c, 0)),
                  pl.BlockSpec((1, 1, D_MODEL), lambda c, u, s: (c // per_seq, 0, 5)),
                  pl.BlockSpec((1, D_MODEL), lambda c, u, s: (0, 0))],
        out_specs=pl.BlockSpec((CHUNK, D_MODEL), lambda c, u, s: (c, 0)),
        scratch_shapes=[pltpu.VMEM((2, CHUNK_BLOCKS, ROW_BLOCK, D_MODEL), BF16),
                        pltpu.SemaphoreType.DMA((2,))])
    return pl.pallas_call(
        _combine_kernel,
        grid_spec=grid_spec,
        out_shape=jax.ShapeDtypeStruct((n, D_MODEL), F32),
        compiler_params=_cparams(("arbitrary",)),
        name="moe_combine",
    )(used_rows, slot_of, yt, pos_c, x1, mod3, final_g)


def _rope_tables(seq):
    n_freq = HEAD_DIM // 4
    inv_freq = ROPE_BASE ** (-jnp.arange(n_freq, dtype=F32) / n_freq)
    rows = seq // GRID_W
    row = jnp.repeat(jnp.arange(rows, dtype=F32), GRID_W)
    col = jnp.tile(jnp.arange(GRID_W, dtype=F32), rows)
    ang = jnp.concatenate([row[:, None] * inv_freq, col[:, None] * inv_freq], axis=-1)
    cos, sin = jnp.cos(ang), jnp.sin(ang)
    return jnp.tile(cos, (1, 4)), jnp.concatenate([-sin, sin, -sin, sin], axis=1)


def _window_bias():
    key = np.arange(3 * BLOCK)[:, None]
    qry = np.arange(BLOCK)[None, :]
    valid = (key - qry >= 0) & (key - qry <= 2 * BLOCK)
    return jnp.asarray(np.where(valid, 0.0, NEG_INF), F32)


def kernel(x, c, ctx, c_ctx, w_ada, b_ada, norm1_g, w_in, w_conv, b_conv, w_a, w_b, sink, w_o,
           norm2_g, w_group, b_group, w_router, b_router, w_up, w_down, final_g):
    bsz, seq, _ = x.shape
    assert w_ada.shape[0] == 1 and seq % SEQ_TILE == 0

    cc = jnp.zeros((16, D_MODEL), F32).at[:bsz].set(c).at[bsz].set(c_ctx)
    mod = _modulation(cc, w_ada[0], b_ada[0][None, :])
    mod3 = mod[:bsz].reshape(bsz, 1, N_MOD * D_MODEL)
    csh1 = mod[bsz:bsz + 1, 0:D_MODEL]
    csc1 = mod[bsz:bsz + 1, D_MODEL:2 * D_MODEL]

    w = w_in[0]
    n1g = norm1_g[0][None, :]
    n2g = norm2_g[0][None, :]
    w_kv = w[:, OFF_K:OFF_GA].astype(BF16)
    w_a_cols = jnp.concatenate([w[:, OFF_CG:OFF_Q], w[:, OFF_K:OFF_GA]], axis=1).astype(BF16)
    w_b_cols = jnp.concatenate([w[:, OFF_B:OFF_CG], w[:, OFF_Q:OFF_K], w[:, OFF_GA:]],
                               axis=1).astype(BF16)
    cos_t, sin_t = _rope_tables(seq)

    k_ctx, vt_ctx = _context_kv(ctx, n1g, csh1, csc1, w_kv)
    uc, k_rot, v_t = _proj_a(x, n1g, mod3, w_a_cols, cos_t, sin_t)

    sink_row = jnp.repeat(sink[0].astype(F32), BLOCK)[None, :]
    pad_g = jnp.zeros((D_MODEL, GROUP_ROWS - N_GROUPS), F32)
    pad_e = jnp.zeros((D_MODEL, ROUTER_ROWS - GROUP_ROWS - N_EXPERTS), F32)
    wr_t = jnp.concatenate([w_group[0], pad_g, w_router[0], pad_e], axis=1).T.astype(BF16)
    br_col = jnp.concatenate([b_group[0], pad_g[0], b_router[0], pad_e[0]])[:, None]

    x1, h2, route, w_up_bf, w_down_bf = _mix(
        x, n1g, n2g, mod3, w_b_cols, cos_t, sin_t, uc, k_rot, v_t, k_ctx, vt_ctx,
        w_conv[0], b_conv[0][None, :], w_a[0].astype(BF16), w_b[0].astype(BF16),
        w_o[0].astype(BF16), sink_row, _window_bias(), wr_t, br_col, w_up[0], w_down[0])

    n = bsz * seq
    nch = n // CHUNK
    upper = jnp.asarray(np.triu(np.ones((CHUNK, CHUNK), np.float32), 1), BF16)
    lower = jnp.asarray(np.tril(np.ones((N_EXPERTS, N_EXPERTS), np.float32), -1), BF16)
    xl, pos_c, nb = _dispatch(h2.reshape(n, D_MODEL), route, upper, lower)
    nb = nb[:, :, 0]
    expert_plan, slot_of = _tile_plan(nb, nch)
    yt = _grouped_mlp(expert_plan, xl.reshape(-1, ROW_BLOCK, XL_COLS), w_up_bf, w_down_bf,
                      _max_tiles(nch))
    used_rows = (ROW_BLOCK * jnp.sum(nb, axis=1)).astype(I32)
    out = _combine(used_rows, slot_of, yt, pos_c, x1.reshape(n, D_MODEL), mod3, final_g[None, :],
                   seq)
    return out.reshape(bsz, seq, D_MODEL)
```

```python
import numpy as np
import jax
import jax.numpy as jnp
from jax import lax
from jax.experimental import pallas as pl
from jax.experimental.pallas import tpu as pltpu

F32 = jnp.float32
BF16 = jnp.bfloat16
I32 = jnp.int32

D_MODEL = 1024
GRID_W = 64
CONV_W = 512
N_HEADS = 8
N_KV_HEADS = 2
HEAD_DIM = 64
ATT_W = N_HEADS * HEAD_DIM
KV_W = N_KV_HEADS * HEAD_DIM
BLOCK = 128
ROPE_BASE = 10000.0
N_GROUPS = 4
EXPERTS_PER_GROUP = 8
N_EXPERTS = N_GROUPS * EXPERTS_PER_GROUP
EXPERT_FF = 256
N_MOD = 6
NORM_EPS = 1e-6
NEG_INF = -1e30
LOG2E = 1.4426950408889634
ONES_ROWS = 16

OFF_B, OFF_CG, OFF_XIN, OFF_Q, OFF_K, OFF_V, OFF_GA, OFF_GB = (
    0, 512, 1024, 1536, 2048, 2176, 2304, 3328)
IN_COLS = 4352

LANES = 128
SEQ_TILE = 512
ROW_SPLITS = 1
BLOCKS_PER_TILE = SEQ_TILE // BLOCK
ROUTE_ROWS = 8
GROUP_ROWS = 8
ROUTER_ROWS = 48
VMEM_LIMIT = 56 * 1024 * 1024

CHUNK = SEQ_TILE
ROW_BLOCK = 16
CHUNK_ROWS = -(-(2 * CHUNK + N_EXPERTS * (ROW_BLOCK - 1)) // 256) * 256
CHUNK_BLOCKS = CHUNK_ROWS // ROW_BLOCK
GATE_COLS = LANES
XL_COLS = D_MODEL + GATE_COLS
COMMON_ROWS = 2 * CHUNK + 256
COMMON_BLOCKS = COMMON_ROWS // ROW_BLOCK
TILE_BLOCKS = 32
TILE_ROWS = TILE_BLOCKS * ROW_BLOCK
SUB_ROWS = 512
X_SLOTS = 6
Y_SLOTS = 4
W_SLOTS = 3


def _cparams(sem):
    return pltpu.CompilerParams(dimension_semantics=sem, vmem_limit_bytes=VMEM_LIMIT)


def _rms_mod(x, g, shift, scale):
    inv = lax.rsqrt(jnp.mean(x * x, axis=-1, keepdims=True) + NORM_EPS)
    return (x * inv) * (g * (1.0 + scale)) + shift


def _mod_kernel(c_ref, w_ref, b_ref, o_ref):
    c = c_ref[...]
    a = (c * jax.nn.sigmoid(c)).astype(BF16)
    o_ref[...] = jnp.dot(a, w_ref[...].astype(BF16), preferred_element_type=F32) + b_ref[...]


def _modulation(cc, w_ada, b_ada):
    rows = cc.shape[0]
    cols = w_ada.shape[1]
    tile = 1024
    return pl.pallas_call(
        _mod_kernel,
        grid=(cols // tile,),
        in_specs=[pl.BlockSpec((rows, D_MODEL), lambda j: (0, 0)),
                  pl.BlockSpec((D_MODEL, tile), lambda j: (0, j)),
                  pl.BlockSpec((1, tile), lambda j: (0, j))],
        out_specs=pl.BlockSpec((rows, tile), lambda j: (0, j)),
        out_shape=jax.ShapeDtypeStruct((rows, cols), F32),
        compiler_params=_cparams(("arbitrary",)),
        name="adaln_mod",
    )(cc, w_ada, b_ada)


def _ctx_kernel(x_ref, g_ref, sh_ref, sc_ref, w_ref, k_ref, vt_ref):
    h = _rms_mod(x_ref[0], g_ref[...], sh_ref[...], sc_ref[...]).astype(BF16)
    kv = jnp.dot(h, w_ref[...], preferred_element_type=F32)
    k_ref[0] = kv[:, :KV_W].astype(BF16)
    vt_ref[0] = kv[:, KV_W:].T.astype(BF16)


def _context_kv(ctx, norm_g, csh, csc, w_kv):
    bsz, clen, _ = ctx.shape
    return pl.pallas_call(
        _ctx_kernel,
        grid=(bsz,),
        in_specs=[pl.BlockSpec((1, clen, D_MODEL), lambda b: (b, 0, 0)),
                  pl.BlockSpec((1, D_MODEL), lambda b: (0, 0)),
                  pl.BlockSpec((1, D_MODEL), lambda b: (0, 0)),
                  pl.BlockSpec((1, D_MODEL), lambda b: (0, 0)),
                  pl.BlockSpec((D_MODEL, 2 * KV_W), lambda b: (0, 0))],
        out_specs=[pl.BlockSpec((1, clen, KV_W), lambda b: (b, 0, 0)),
                   pl.BlockSpec((1, KV_W, clen), lambda b: (b, 0, 0))],
        out_shape=[jax.ShapeDtypeStruct((bsz, clen, KV_W), BF16),
                   jax.ShapeDtypeStruct((bsz, KV_W, clen), BF16)],
        compiler_params=_cparams(("arbitrary",)),
        name="context_kv",
    )(ctx, norm_g, csh, csc, w_kv)


def _rope(t, cos, sin_signed):
    lane = lax.broadcasted_iota(I32, (1, LANES), 1)
    first_half = (lane % HEAD_DIM) < (HEAD_DIM // 2)
    outs = []
    for j in range(t.shape[1] // LANES):
        tj = t[:, j * LANES:(j + 1) * LANES]
        partner = jnp.where(first_half,
                            pltpu.roll(tj, LANES - HEAD_DIM // 2, axis=1),
                            pltpu.roll(tj, HEAD_DIM // 2, axis=1))
        outs.append(tj * cos + partner * sin_signed)
    return outs[0] if len(outs) == 1 else jnp.concatenate(outs, axis=1)


def _proj_a_kernel(x_ref, g_ref, sh_ref, sc_ref, w_ref, cos_ref, sin_ref, uc_ref, k_ref, vt_ref):
    half = SEQ_TILE // ROW_SPLITS
    halves = [slice(hf * half, (hf + 1) * half) for hf in range(ROW_SPLITS)]
    hs = [_rms_mod(x_ref[0, rows, :], g_ref[...], sh_ref[0], sc_ref[0]).astype(BF16)
          for rows in halves]
    for h, rows in zip(hs, halves):
        u = jnp.dot(h, w_ref[...], preferred_element_type=F32)
        uc_ref[0, rows, :] = u[:, :CONV_W] * u[:, CONV_W:2 * CONV_W]
        k = u[:, 2 * CONV_W:2 * CONV_W + KV_W]
        k_ref[0, rows, :] = _rope(k, cos_ref[rows, :], sin_ref[rows, :]).astype(BF16)
        vt_ref[0, :, rows] = u[:, 2 * CONV_W + KV_W:].T.astype(BF16)


def _proj_a(x, norm_g, mod3, w_a_cols, cos_t, sin_t):
    bsz, seq, _ = x.shape
    nt = seq // SEQ_TILE
    wcols = w_a_cols.shape[1]
    return pl.pallas_call(
        _proj_a_kernel,
        grid=(nt, bsz),
        in_specs=[pl.BlockSpec((1, SEQ_TILE, D_MODEL), lambda t, b: (b, t, 0)),
                  pl.BlockSpec((1, D_MODEL), lambda t, b: (0, 0)),
                  pl.BlockSpec((1, 1, D_MODEL), lambda t, b: (b, 0, 0)),
                  pl.BlockSpec((1, 1, D_MODEL), lambda t, b: (b, 0, 1)),
                  pl.BlockSpec((D_MODEL, wcols), lambda t, b: (0, 0)),
                  pl.BlockSpec((SEQ_TILE, LANES), lambda t, b: (t, 0)),
                  pl.BlockSpec((SEQ_TILE, LANES), lambda t, b: (t, 0))],
        out_specs=[pl.BlockSpec((1, SEQ_TILE, CONV_W), lambda t, b: (b, t, 0)),
                   pl.BlockSpec((1, SEQ_TILE, KV_W), lambda t, b: (b, t, 0)),
                   pl.BlockSpec((1, KV_W, SEQ_TILE), lambda t, b: (b, 0, t))],
        out_shape=[jax.ShapeDtypeStruct((bsz, seq, CONV_W), F32),
                   jax.ShapeDtypeStruct((bsz, seq, KV_W), BF16),
                   jax.ShapeDtypeStruct((bsz, KV_W, seq), BF16)],
        compiler_params=_cparams(("arbitrary", "arbitrary")),
        name="proj_a",
    )(x, norm_g, mod3, mod3, w_a_cols, cos_t, sin_t)


def _sigmoid(x):
    return 1.0 / (1.0 + jnp.exp2(x * (-LOG2E)))


def _attn_scores(q_t, g, k_win, k_ctx, bias_p4, bias_n4):
    zeros = jnp.zeros((HEAD_DIM, 4 * BLOCK), BF16)
    qg = jnp.concatenate([q_t[(4 * g + h) * HEAD_DIM:(4 * g + h + 1) * HEAD_DIM, :]
                          for h in range(4)], axis=1)
    rhs = jnp.concatenate([qg, zeros] if g == 0 else [zeros, qg], axis=0)
    s_win = jnp.dot(k_win, rhs, preferred_element_type=F32)
    s_ctx = jnp.dot(k_ctx, rhs, preferred_element_type=F32)
    return (s_win[0:BLOCK] + bias_p4, s_win[BLOCK:2 * BLOCK], s_win[2 * BLOCK:] + bias_n4, s_ctx)


def _attn_probs(scores, sink):
    m = sink
    for s in scores:
        m = jnp.maximum(m, jnp.max(s, axis=0, keepdims=True))
    p_win = jnp.concatenate([jnp.exp2(s - m).astype(BF16) for s in scores[:3]], axis=0)
    p_ctx = jnp.exp2(scores[3] - m).astype(BF16)
    return p_win, p_ctx, m


def _attn_values(probs, sink, vt_win_g, vt_ctx_g):
    p_win, p_ctx, m = probs

    def with_ones(vt):
        r = lax.broadcasted_iota(I32, (ONES_ROWS, vt.shape[1]), 0)
        return jnp.concatenate([vt, jnp.where(r == 0, 1.0, 0.0).astype(BF16)], axis=0)

    o_ext = (jnp.dot(with_ones(vt_win_g), p_win, preferred_element_type=F32)
             + jnp.dot(with_ones(vt_ctx_g), p_ctx, preferred_element_type=F32))
    denom = o_ext[HEAD_DIM:HEAD_DIM + 1, :] + jnp.exp2(sink - m)
    return o_ext[:HEAD_DIM, :] / denom


def _route(logits_t):
    t = logits_t.shape[1]
    grow = lax.broadcasted_iota(I32, (GROUP_ROWS, t), 0)
    gl = jnp.where(grow < N_GROUPS, logits_t[0:GROUP_ROWS, :], NEG_INF)
    gm = jnp.max(gl, axis=0, keepdims=True)
    p_g = 1.0 / jnp.sum(jnp.exp(gl - gm), axis=0, keepdims=True)
    g_idx = jnp.min(jnp.where(gl == gm, grow, N_GROUPS), axis=0, keepdims=True)

    erow = lax.broadcasted_iota(I32, (N_EXPERTS, t), 0)
    el = logits_t[GROUP_ROWS:GROUP_ROWS + N_EXPERTS, :]
    sel = (erow // EXPERTS_PER_GROUP) == g_idx
    em = jnp.where(sel, el, NEG_INF)
    m1 = jnp.max(em, axis=0, keepdims=True)
    i1 = jnp.min(jnp.where(em == m1, erow, N_EXPERTS), axis=0, keepdims=True)
    em2 = jnp.where(erow == i1, NEG_INF, em)
    m2 = jnp.max(em2, axis=0, keepdims=True)
    i2 = jnp.min(jnp.where(em2 == m2, erow, N_EXPERTS), axis=0, keepdims=True)
    z = jnp.sum(jnp.where(sel, jnp.exp(el - m1), 0.0), axis=0, keepdims=True)
    p1 = 1.0 / z
    p2 = jnp.exp(m2 - m1) / z
    gate1 = p_g * p1 / (p1 + p2)
    gate2 = p_g * p2 / (p1 + p2)
    pad = jnp.zeros((ROUTE_ROWS - 4, t), F32)
    return jnp.concatenate([i1.astype(F32), i2.astype(F32), gate1, gate2, pad], axis=0)


def _mix_kernel(x_ref, g1n_ref, g2n_ref, sh1_ref, sc1_ref, gt1_ref, sh2_ref, sc2_ref,
                wb_ref, cos_ref, sin_ref,
                uc_ref, ucp_ref, ucn_ref, k_ref, kp_ref, kn_ref, vt_ref, vtp_ref, vtn_ref,
                kc_ref, vtc_ref, wconv_ref, bconv_ref, wa_ref, wbb_ref, wo_ref,
                sink_ref, bias_ref, wr_ref, br_ref, wu_ref, wd_ref,
                x1_ref, h2_ref, route_ref, wub_ref, wdb_ref):
    t_idx = pl.program_id(0)
    n_tiles = pl.num_programs(0)
    is_first = t_idx == 0
    is_last = t_idx == n_tiles - 1

    x = x_ref[0]
    half = SEQ_TILE // ROW_SPLITS
    halves = [slice(hf * half, (hf + 1) * half) for hf in range(ROW_SPLITS)]

    uc = uc_ref[0]
    row = lax.broadcasted_iota(I32, (SEQ_TILE, 1), 0)
    prev_row = jnp.where(is_first, 0.0, ucp_ref[0][7:8, :])
    next_row = jnp.where(is_last, 0.0, ucn_ref[0][0:1, :])
    up = jnp.where(row == 0, prev_row, pltpu.roll(uc, 1, axis=0))
    dn = jnp.where(row == SEQ_TILE - 1, next_row, pltpu.roll(uc, SEQ_TILE - 1, axis=0))
    wconv = wconv_ref[...]
    y = bconv_ref[...] + (up * wconv[0:1, :] + uc * wconv[1:2, :] + dn * wconv[2:3, :])

    hbs = [_rms_mod(x[rows], g1n_ref[...], sh1_ref[0], sc1_ref[0]).astype(BF16) for rows in halves]
    yas, qs = [], []
    for hf, rows in enumerate(halves):
        bq = jnp.dot(hbs[hf], wb_ref[:, 0:2 * CONV_W], preferred_element_type=F32)
        yas.append(jnp.dot((bq[:, :CONV_W] * y[rows]).astype(BF16), wa_ref[...],
                           preferred_element_type=F32))
        qs.append(_rope(bq[:, CONV_W:], cos_ref[rows, :], sin_ref[rows, :])
                  * (HEAD_DIM ** -0.5 * LOG2E))
    hb = jnp.concatenate(hbs, axis=0)

    k_all = jnp.concatenate([kp_ref[0], k_ref[0], kn_ref[0]], axis=0)
    vt_all = jnp.concatenate([vtp_ref[0], vt_ref[0], vtn_ref[0]], axis=1)
    k_ctx = kc_ref[0]
    vt_ctx = vtc_ref[0]
    bias_prev = bias_ref[0:BLOCK, :]
    bias_next = bias_ref[2 * BLOCK:3 * BLOCK, :]
    sink_row = sink_ref[...] * LOG2E
    n_units = BLOCKS_PER_TILE * N_KV_HEADS
    gate_cols = 2 * D_MODEL // n_units
    gate_chunks = []

    def gate_chunk(u):
        c0 = 2 * CONV_W + u * gate_cols
        gate_chunks.append(jnp.dot(hb, wb_ref[:, c0:c0 + gate_cols], preferred_element_type=F32))

    q_ts, biases = [], []
    for jb in range(BLOCKS_PER_TILE):
        bias_p, bias_n = bias_prev, bias_next
        if jb == 0:
            bias_p = bias_prev + jnp.where(is_first, NEG_INF, 0.0)
        if jb == BLOCKS_PER_TILE - 1:
            bias_n = bias_next + jnp.where(is_last, NEG_INF, 0.0)
        biases.append((jnp.concatenate([bias_p] * 4, axis=1), jnp.concatenate([bias_n] * 4, axis=1)))
        hf, jh = divmod(jb, BLOCKS_PER_TILE // ROW_SPLITS)
        q_ts.append(qs[hf][jh * BLOCK:(jh + 1) * BLOCK, :].T.astype(BF16))

    scores, probs, outs = {}, {}, {}
    for t in range(n_units + 2):
        if t == 3:
            wub_ref[...] = wu_ref[...].astype(BF16)
        if t == 5:
            wdb_ref[...] = wd_ref[...].astype(BF16)
        if t < n_units:
            jb, g = divmod(t, N_KV_HEADS)
            scores[t] = _attn_scores(q_ts[jb], g, k_all[jb * BLOCK:(jb + 3) * BLOCK, :], k_ctx,
                                     *biases[jb])
            gate_chunk(t)
        if 0 <= t - 1 < n_units:
            g = (t - 1) % N_KV_HEADS
            probs[t - 1] = _attn_probs(scores.pop(t - 1),
                                       sink_row[:, g * 4 * BLOCK:(g + 1) * 4 * BLOCK])
        if 0 <= t - 2 < n_units:
            jb, g = divmod(t - 2, N_KV_HEADS)
            outs[t - 2] = _attn_values(
                probs.pop(t - 2), sink_row[:, g * 4 * BLOCK:(g + 1) * 4 * BLOCK],
                vt_all[g * HEAD_DIM:(g + 1) * HEAD_DIM, jb * BLOCK:(jb + 3) * BLOCK],
                vt_ctx[g * HEAD_DIM:(g + 1) * HEAD_DIM, :])
    o_blocks = []
    for jb in range(BLOCKS_PER_TILE):
        o_rows = [outs[jb * N_KV_HEADS + g][:, h * BLOCK:(h + 1) * BLOCK]
                  for g in range(N_KV_HEADS) for h in range(4)]
        o_blocks.append(jnp.concatenate(o_rows, axis=0).T)
    gates = jnp.concatenate(gate_chunks, axis=1)

    half_blocks = BLOCKS_PER_TILE // ROW_SPLITS
    for hf in range(ROW_SPLITS):
        r0, r1 = hf * half_blocks * BLOCK, (hf + 1) * half_blocks * BLOCK
        o = jnp.concatenate(o_blocks[hf * half_blocks:(hf + 1) * half_blocks], axis=0).astype(BF16)
        yb = jnp.dot(o, wbb_ref[...], preferred_element_type=F32)
        merged = (_sigmoid(gates[r0:r1, :D_MODEL]) * yas[hf]
                  + _sigmoid(gates[r0:r1, D_MODEL:]) * yb).astype(BF16)
        x1 = x[r0:r1] + gt1_ref[0] * jnp.dot(merged, wo_ref[...], preferred_element_type=F32)
        x1_ref[0, r0:r1, :] = x1
        h2 = _rms_mod(x1, g2n_ref[...], sh2_ref[0], sc2_ref[0]).astype(BF16)
        h2_ref[0, r0:r1, :] = h2
        logits_t = lax.dot_general(wr_ref[...], h2, (((1,), (1,)), ((), ())),
                                   preferred_element_type=F32) + br_ref[...]
        route_ref[0, :, r0:r1] = _route(logits_t)


def _mix(x, norm1_g, norm2_g, mod3, w_b_cols, cos_t, sin_t, uc, k_rot, v_t, k_ctx, vt_ctx,
         w_conv, b_conv, w_a, w_b, w_o, sink_row, bias, wr_t, br_col, w_up, w_down):
    bsz, seq, _ = x.shape
    nt = seq // SEQ_TILE
    nblk = seq // BLOCK
    clen = k_ctx.shape[1]
    rows8 = SEQ_TILE // 8
    assert N_EXPERTS % (nt * bsz) == 0, "each grid step converts an equal share of the experts"
    e_step = N_EXPERTS // (nt * bsz)

    def mod_spec(j):
        return pl.BlockSpec((1, 1, D_MODEL), lambda t, b, j=j: (b, 0, j))

    def const_spec(shape):
        return pl.BlockSpec(shape, lambda t, b: tuple(0 for _ in shape))

    in_specs = [
        pl.BlockSpec((1, SEQ_TILE, D_MODEL), lambda t, b: (b, t, 0)),
        const_spec((1, D_MODEL)), const_spec((1, D_MODEL)),
        mod_spec(0), mod_spec(1), mod_spec(2), mod_spec(3), mod_spec(4),
        const_spec(w_b_cols.shape),
        pl.BlockSpec((SEQ_TILE, LANES), lambda t, b: (t, 0)),
        pl.BlockSpec((SEQ_TILE, LANES), lambda t, b: (t, 0)),
        pl.BlockSpec((1, SEQ_TILE, CONV_W), lambda t, b: (b, t, 0)),
        pl.BlockSpec((1, 8, CONV_W), lambda t, b: (b, jnp.maximum(t * rows8 - 1, 0), 0)),
        pl.BlockSpec((1, 8, CONV_W),
                     lambda t, b: (b, jnp.minimum((t + 1) * rows8, seq // 8 - 1), 0)),
        pl.BlockSpec((1, SEQ_TILE, KV_W), lambda t, b: (b, t, 0)),
        pl.BlockSpec((1, BLOCK, KV_W),
                     lambda t, b: (b, jnp.maximum(t * BLOCKS_PER_TILE - 1, 0), 0)),
        pl.BlockSpec((1, BLOCK, KV_W),
                     lambda t, b: (b, jnp.minimum((t + 1) * BLOCKS_PER_TILE, nblk - 1), 0)),
        pl.BlockSpec((1, KV_W, SEQ_TILE), lambda t, b: (b, 0, t)),
        pl.BlockSpec((1, KV_W, BLOCK),
                     lambda t, b: (b, 0, jnp.maximum(t * BLOCKS_PER_TILE - 1, 0))),
        pl.BlockSpec((1, KV_W, BLOCK),
                     lambda t, b: (b, 0, jnp.minimum((t + 1) * BLOCKS_PER_TILE, nblk - 1))),
        pl.BlockSpec((1, clen, KV_W), lambda t, b: (b, 0, 0)),
        pl.BlockSpec((1, KV_W, clen), lambda t, b: (b, 0, 0)),
        const_spec(w_conv.shape), const_spec(b_conv.shape),
        const_spec(w_a.shape), const_spec(w_b.shape), const_spec(w_o.shape),
        const_spec(sink_row.shape), const_spec(bias.shape),
        const_spec(wr_t.shape), const_spec(br_col.shape),
        pl.BlockSpec((e_step,) + w_up.shape[1:], lambda t, b: (t * bsz + b, 0, 0)),
        pl.BlockSpec((e_step,) + w_down.shape[1:], lambda t, b: (t * bsz + b, 0, 0)),
    ]
    out_specs = [
        pl.BlockSpec((1, SEQ_TILE, D_MODEL), lambda t, b: (b, t, 0)),
        pl.BlockSpec((1, SEQ_TILE, D_MODEL), lambda t, b: (b, t, 0)),
        pl.BlockSpec((1, ROUTE_ROWS, SEQ_TILE), lambda t, b: (b, 0, t)),
        pl.BlockSpec((e_step,) + w_up.shape[1:], lambda t, b: (t * bsz + b, 0, 0)),
        pl.BlockSpec((e_step,) + w_down.shape[1:], lambda t, b: (t * bsz + b, 0, 0)),
    ]
    out_shape = [
        jax.ShapeDtypeStruct((bsz, seq, D_MODEL), F32),
        jax.ShapeDtypeStruct((bsz, seq, D_MODEL), BF16),
        jax.ShapeDtypeStruct((bsz, ROUTE_ROWS, seq), F32),
        jax.ShapeDtypeStruct(w_up.shape, BF16),
        jax.ShapeDtypeStruct(w_down.shape, BF16),
    ]
    return pl.pallas_call(
        _mix_kernel,
        grid=(nt, bsz),
        in_specs=in_specs,
        out_specs=out_specs,
        out_shape=out_shape,
        compiler_params=_cparams(("arbitrary", "arbitrary")),
        name="token_mix",
    )(x, norm1_g, norm2_g, mod3, mod3, mod3, mod3, mod3, w_b_cols, cos_t, sin_t,
      uc, uc, uc, k_rot, k_rot, k_rot, v_t, v_t, v_t, k_ctx, vt_ctx,
      w_conv, b_conv, w_a, w_b, w_o, sink_row, bias, wr_t, br_col, w_up, w_down)


def _bf16_parts(v):
    hi = v.astype(BF16).astype(F32)
    r1 = v - hi
    mid = r1.astype(BF16).astype(F32)
    lo = (r1 - mid).astype(BF16).astype(F32)
    return hi, mid, lo


def _dispatch_kernel(h_ref, route_ref, upper_ref, lower_ref, xl_ref, pos_ref, nb_ref):
    route = route_ref[0]
    e1 = route[0:1, :].astype(I32)
    e2 = route[1:2, :].astype(I32)
    erow = lax.broadcasted_iota(I32, (N_EXPERTS, CHUNK), 0)
    hit1 = erow == e1
    hit2 = erow == e2
    onehot = jnp.where(hit1, 1.0, 0.0) + jnp.where(hit2, 1.0, 0.0)
    cum = jnp.dot(onehot.astype(BF16), upper_ref[...], preferred_element_type=F32)
    cnt = jnp.sum(onehot, axis=1, keepdims=True)
    nblk = jnp.floor((cnt + (ROW_BLOCK - 1)) * (1.0 / ROW_BLOCK))
    nblk_b = jnp.broadcast_to(nblk, (N_EXPERTS, LANES))
    seg = jnp.dot(lower_ref[...], nblk_b.astype(BF16), preferred_element_type=F32) * ROW_BLOCK
    base = seg[:, 0:1] + cum
    pos1 = jnp.sum(jnp.where(hit1, base, 0.0), axis=0, keepdims=True)
    pos2 = jnp.sum(jnp.where(hit2, base, 0.0), axis=0, keepdims=True)
    p1i = pos1.astype(I32)
    p2i = pos2.astype(I32)
    used_rows = (jnp.sum(nblk) * ROW_BLOCK).astype(I32)

    prow = lax.broadcasted_iota(I32, (LANES, CHUNK), 0)
    parts = _bf16_parts(route[2:3, :]) + _bf16_parts(route[3:4, :])
    gpart_rows = jnp.zeros((LANES, CHUNK), F32)
    for j, part in enumerate(parts):
        gpart_rows = jnp.where(prow == j, part, gpart_rows)
    h_ext = jnp.concatenate([h_ref[...], gpart_rows.T.astype(BF16)], axis=1)
    glane = lax.broadcasted_iota(I32, (1, GATE_COLS), 1)
    one = jnp.ones((), BF16)
    zero = jnp.zeros((), BF16)
    def permute(row0, n_rows):
        r16 = lax.broadcasted_iota(I32, (n_rows, CHUNK), 0).astype(jnp.int16)
        q1 = (p1i - row0).astype(jnp.int16)
        q2 = (p2i - row0).astype(jnp.int16)
        sel1 = jnp.where(r16 == q1, one, zero)
        sel = jnp.where(r16 == q2, one, sel1)
        is_slot1 = jnp.max(sel1, axis=1, keepdims=True)
        rows = pl.ds(row0, n_rows)
        xg = jnp.dot(sel, h_ext, preferred_element_type=F32)
        xl_ref[rows, 0:D_MODEL] = xg[:, :D_MODEL].astype(BF16)
        g6 = xg[:, D_MODEL:]
        g3 = jnp.where(is_slot1.astype(F32) > 0.0, g6, pltpu.roll(g6, GATE_COLS - 3, axis=1))
        xl_ref[rows, D_MODEL:XL_COLS] = jnp.where(glane < 3, g3, 0.0).astype(BF16)

    permute(0, COMMON_ROWS)
    tail_rows = CHUNK_ROWS - COMMON_ROWS

    @pl.when(COMMON_ROWS < used_rows)
    def _():
        permute(COMMON_ROWS, tail_rows)

    @pl.when(COMMON_ROWS >= used_rows)
    def _():
        xl_ref[pl.ds(COMMON_ROWS, tail_rows), :] = jnp.zeros((tail_rows, XL_COLS), BF16)

    pos_rows = jnp.where(prow == 0, pos1, jnp.where(prow == 1, pos2, 0.0))
    pos_ref[...] = pos_rows.T
    nb_ref[0] = nblk_b.astype(I32)


def _dispatch(h2, route, upper, lower):
    n = h2.shape[0]
    nch = n // CHUNK
    per_seq = route.shape[2] // CHUNK
    return pl.pallas_call(
        _dispatch_kernel,
        grid=(nch,),
        in_specs=[pl.BlockSpec((CHUNK, D_MODEL), lambda c: (c, 0)),
                  pl.BlockSpec((1, ROUTE_ROWS, CHUNK), lambda c: (c // per_seq, 0, c % per_seq)),
                  pl.BlockSpec(upper.shape, lambda c: (0, 0)),
                  pl.BlockSpec(lower.shape, lambda c: (0, 0))],
        out_specs=[pl.BlockSpec((CHUNK_ROWS, XL_COLS), lambda c: (c, 0)),
                   pl.BlockSpec((CHUNK, LANES), lambda c: (c, 0)),
                   pl.BlockSpec((1, N_EXPERTS, LANES), lambda c: (c, 0, 0))],
        out_shape=[jax.ShapeDtypeStruct((nch * CHUNK_ROWS, XL_COLS), BF16),
                   jax.ShapeDtypeStruct((n, LANES), F32),
                   jax.ShapeDtypeStruct((nch, N_EXPERTS, LANES), I32)],
        compiler_params=_cparams(("arbitrary",)),
        name="moe_dispatch",
    )(h2, route, upper, lower)


def _max_tiles(nch):
    max_blocks = nch * (2 * CHUNK + N_EXPERTS * (ROW_BLOCK - 1)) // ROW_BLOCK
    return max_blocks // TILE_BLOCKS + N_EXPERTS


def _masked_prefix(le, values):
    delta = values - jnp.concatenate([jnp.zeros((1,), values.dtype), values[:-1]])
    return jnp.sum(jnp.where(le, delta[None, :], 0), axis=1)


def _tile_plan(nb, nch):
    n_tiles = _max_tiles(nch)
    nbt = nb.T
    nbe = jnp.sum(nbt, axis=1)
    nte = (nbe + TILE_BLOCKS - 1) // TILE_BLOCKS
    tile_end = jnp.cumsum(nte)
    tile_start = tile_end - nte
    n_act = tile_end[-1]
    tiles = jnp.arange(n_tiles, dtype=I32)
    te = jnp.sum((tile_end[None, :] <= tiles[:, None]).astype(I32), axis=1)
    e_ar = jnp.arange(N_EXPERTS, dtype=I32)
    active = nte > 0
    te = jnp.where(tiles < n_act, te, jnp.max(jnp.where(active, e_ar, 0)))
    first = jnp.logical_and(te != jnp.concatenate([jnp.full((1,), -1, I32), te[:-1]]),
                            tiles < n_act).astype(I32)
    rank = jnp.cumsum(active.astype(I32)) - 1
    k_ar = jnp.arange(N_EXPERTS + W_SLOTS, dtype=I32)
    eseq = jnp.sum(jnp.where(jnp.logical_and(active[None, :], rank[None, :] == k_ar[:, None]),
                             e_ar[None, :], 0), axis=1)
    n_exp = jnp.sum(active.astype(I32))

    cb_excl = jnp.cumsum(nbt, axis=1) - nbt
    gs = TILE_BLOCKS * tile_start[:, None] + cb_excl
    seg_blk = jnp.cumsum(nb, axis=1) - nb
    base_blk = jnp.arange(nch, dtype=I32)[None, :] * CHUNK_BLOCKS + seg_blk.T
    gs_f = gs.reshape(-1)
    slots = jnp.arange(n_tiles * TILE_BLOCKS, dtype=I32)
    le = gs_f[None, :] <= slots[:, None]
    blk = slots + _masked_prefix(le, (base_blk - gs).reshape(-1))
    valid = slots < _masked_prefix(le, (gs + nbt).reshape(-1))
    gblk = jnp.concatenate([jnp.where(valid, blk, 0).astype(I32),
                            jnp.zeros(((X_SLOTS - 1) * TILE_BLOCKS,), I32)])

    m = jnp.arange(CHUNK_BLOCKS, dtype=I32)
    le_c = seg_blk[:, None, :] <= m[None, :, None]
    shift = (gs.T - seg_blk)
    delta = shift - jnp.concatenate([jnp.zeros((nch, 1), I32), shift[:, :-1]], axis=1)
    slot_of = m[None, :] + jnp.sum(jnp.where(le_c, delta[:, None, :], 0), axis=2)
    used_blocks = jnp.sum(nb, axis=1)
    slot_of = jnp.where(m[None, :] < used_blocks[:, None], slot_of, 0).astype(I32)
    return ((n_act.reshape(1).astype(I32), n_exp.reshape(1), first, eseq, gblk),
            slot_of.reshape(-1))


def _gmm_kernel(nact_ref, nexp_ref, first_ref, eseq_ref, gblk_ref,
                xl_hbm, wu_hbm, wd_hbm, yt_hbm,
                xbuf, ybuf, zbuf, wu_st, wd_st, in_sem, out_sem, w_sem, z_sem):
    n_act = nact_ref[0]
    n_exp = nexp_ref[0]
    n_tiles = yt_hbm.shape[0] // TILE_BLOCKS
    prefetch = X_SLOTS - 1

    def gather(tile, b):
        s = tile % X_SLOTS
        blk = gblk_ref[tile * TILE_BLOCKS + b]
        return pltpu.make_async_copy(xl_hbm.at[blk], xbuf.at[s, b], in_sem.at[s])

    def write_back(tile):
        s = tile % Y_SLOTS
        return pltpu.make_async_copy(ybuf.at[s], yt_hbm.at[pl.ds(tile * TILE_BLOCKS, TILE_BLOCKS)],
                                     out_sem.at[s])

    def zero_fill(tile):
        return pltpu.make_async_copy(zbuf, yt_hbm.at[pl.ds(tile * TILE_BLOCKS, TILE_BLOCKS)], z_sem)

    def weights(q):
        e = eseq_ref[q]
        s = q % W_SLOTS
        return (pltpu.make_async_copy(wu_hbm.at[e], wu_st.at[s], w_sem.at[0, s]),
                pltpu.make_async_copy(wd_hbm.at[e], wd_st.at[s], w_sem.at[1, s]))

    sub_blocks = SUB_ROWS // ROW_BLOCK

    def tile_blocks(tile, fn):
        del tile
        for b in range(TILE_BLOCKS):
            fn(b)

    for q0 in range(W_SLOTS - 1):
        @pl.when(q0 < n_exp)
        def _(q0=q0):
            for cp in weights(q0):
                cp.start()
    for t0 in range(prefetch):
        tile_blocks(t0, lambda b, t0=t0: gather(t0, b).start())

    def body(i, q):
        tile_blocks(i, lambda b: gather(i, b).wait())

        @pl.when(i >= Y_SLOTS)
        def _():
            write_back(i - Y_SLOTS).wait()

        is_first = first_ref[i] == 1

        @pl.when(is_first)
        def _():
            for cp in weights(q):
                cp.wait()

            @pl.when(q + W_SLOTS - 1 < n_exp)
            def _():
                for cp in weights(q + W_SLOTS - 1):
                    cp.start()

        q = q + is_first.astype(I32)
        ws = (q - 1) % W_SLOTS
        xs = i % X_SLOTS
        ys = i % Y_SLOTS

        def sub_tile(h):
            blocks = pl.ds(h * sub_blocks, sub_blocks)
            x = xbuf[xs, blocks].reshape(SUB_ROWS, XL_COLS)
            gate = jnp.sum(x[:, D_MODEL:].astype(F32), axis=1, keepdims=True)
            au = jnp.dot(x[:, :D_MODEL], wu_st[ws], preferred_element_type=F32)
            if h == 0:
                tile_blocks(i + prefetch, lambda b: gather(i + prefetch, b).start())
            a = au[:, :EXPERT_FF]
            act = (a * _sigmoid(a)) * au[:, EXPERT_FF:]
            y = jnp.dot(act.astype(BF16), wd_st[ws], preferred_element_type=F32)
            ybuf[ys, blocks] = (gate * y).astype(BF16).reshape(sub_blocks, ROW_BLOCK, D_MODEL)

        for h in range(TILE_ROWS // SUB_ROWS):
            sub_tile(h)
        write_back(i).start()

        @pl.when(n_act + i < n_tiles)
        def _():
            zero_fill(n_act + i).start()
        return q

    zbuf[...] = jnp.zeros_like(zbuf)
    lax.fori_loop(0, n_act, body, jnp.int32(0))

    def fill_rest(t, carry):
        zero_fill(t).start()
        return carry

    lax.fori_loop(jnp.minimum(2 * n_act, n_tiles), n_tiles, fill_rest, 0)

    def drain(t, carry):
        zero_fill(t).wait()
        return carry

    lax.fori_loop(n_act, n_tiles, drain, 0)
    for k in range(prefetch):
        tile_blocks(n_act + k, lambda b, k=k: gather(n_act + k, b).wait())
    for k in range(Y_SLOTS):
        @pl.when(n_act - 1 - k >= 0)
        def _(k=k):
            write_back(n_act - 1 - k).wait()


def _grouped_mlp(plan, xl, w_up, w_down, n_tiles):
    grid_spec = pltpu.PrefetchScalarGridSpec(
        num_scalar_prefetch=len(plan),
        grid=(1,),
        in_specs=[pl.BlockSpec(memory_space=pl.ANY)] * 3,
        out_specs=pl.BlockSpec(memory_space=pl.ANY),
        scratch_shapes=[pltpu.VMEM((X_SLOTS, TILE_BLOCKS, ROW_BLOCK, XL_COLS), BF16),
                        pltpu.VMEM((Y_SLOTS, TILE_BLOCKS, ROW_BLOCK, D_MODEL), BF16),
                        pltpu.VMEM((TILE_BLOCKS, ROW_BLOCK, D_MODEL), BF16),
                        pltpu.VMEM((W_SLOTS, D_MODEL, 2 * EXPERT_FF), BF16),
                        pltpu.VMEM((W_SLOTS, EXPERT_FF, D_MODEL), BF16),
                        pltpu.SemaphoreType.DMA((X_SLOTS,)),
                        pltpu.SemaphoreType.DMA((Y_SLOTS,)),
                        pltpu.SemaphoreType.DMA((2, W_SLOTS)),
                        pltpu.SemaphoreType.DMA(())])
    return pl.pallas_call(
        _gmm_kernel,
        grid_spec=grid_spec,
        out_shape=jax.ShapeDtypeStruct((n_tiles * TILE_BLOCKS, ROW_BLOCK, D_MODEL), BF16),
        compiler_params=_cparams(("arbitrary",)),
        name="moe_experts",
    )(*plan, xl, w_up, w_down)


def _combine_kernel(used_ref, slot_ref, yt_hbm, pos_ref, x1_ref, gt2_ref, fg_ref, o_ref, ybuf, sem):
    c = pl.program_id(0)
    n_chunks = pl.num_programs(0)
    used_rows = used_ref[c]
    half = CHUNK // ROW_SPLITS

    def fetch(chunk, b):
        s = chunk % 2
        slot = slot_ref[chunk * CHUNK_BLOCKS + b]
        return pltpu.make_async_copy(yt_hbm.at[slot], ybuf.at[s, b], sem.at[s])

    def chunk_blocks(chunk, fn):
        for b in range(COMMON_BLOCKS):
            fn(b)

        @pl.when(used_ref[chunk] > COMMON_ROWS)
        def _():
            for b in range(COMMON_BLOCKS, CHUNK_BLOCKS):
                fn(b)

    @pl.when(c == 0)
    def _():
        chunk_blocks(0, lambda b: fetch(0, b).start())

    @pl.when(c + 1 < n_chunks)
    def _():
        chunk_blocks(c + 1, lambda b: fetch(c + 1, b).start())

    chunk_blocks(c, lambda b: fetch(c, b).wait())

    def body(k_rows):
        sels = []
        for hf in range(ROW_SPLITS):
            pos = pos_ref[hf * half:(hf + 1) * half, :]
            p1 = pos[:, 0:1].astype(I32)
            p2 = pos[:, 1:2].astype(I32)
            r = lax.broadcasted_iota(I32, (half, k_rows), 1)
            sels.append(jnp.where(r == p1, 1.0, jnp.where(r == p2, 1.0, 0.0)).astype(BF16))
        yl = ybuf[c % 2, 0:k_rows // ROW_BLOCK].reshape(k_rows, D_MODEL)
        for hf in range(ROW_SPLITS):
            rows = pl.ds(hf * half, half)
            y = jnp.dot(sels[hf], yl, preferred_element_type=F32)
            x2 = x1_ref[rows, :] + gt2_ref[0] * y
            inv = lax.rsqrt(jnp.mean(x2 * x2, axis=-1, keepdims=True) + NORM_EPS)
            o_ref[rows, :] = (x2 * inv) * fg_ref[...]

    pl.when(used_rows <= COMMON_ROWS)(lambda: body(COMMON_ROWS))
    pl.when(used_rows > COMMON_ROWS)(lambda: body(CHUNK_ROWS))


def _combine(used_rows, slot_of, yt, pos_c, x1, mod3, final_g, seq):
    n = x1.shape[0]
    per_seq = seq // CHUNK
    grid_spec = pltpu.PrefetchScalarGridSpec(
        num_scalar_prefetch=2,
        grid=(n // CHUNK,),
        in_specs=[pl.BlockSpec(memory_space=pl.ANY),
                  pl.BlockSpec((CHUNK, LANES), lambda c, u, s: (c, 0)),
                  pl.BlockSpec((CHUNK, D_MODEL), lambda c, u, s: (c, 0)),
                  pl.BlockSpec((1, 1, D_MODEL), lambda c, u, s: (c // per_seq, 0, 5)),
                  pl.BlockSpec((1, D_MODEL), lambda c, u, s: (0, 0))],
        out_specs=pl.BlockSpec((CHUNK, D_MODEL), lambda c, u, s: (c, 0)),
        scratch_shapes=[pltpu.VMEM((2, CHUNK_BLOCKS, ROW_BLOCK, D_MODEL), BF16),
                        pltpu.SemaphoreType.DMA((2,))])
    return pl.pallas_call(
        _combine_kernel,
        grid_spec=grid_spec,
        out_shape=jax.ShapeDtypeStruct((n, D_MODEL), F32),
        compiler_params=_cparams(("arbitrary",)),
        name="moe_combine",
    )(used_rows, slot_of, yt, pos_c, x1, mod3, final_g)


def _rope_tables(seq):
    n_freq = HEAD_DIM // 4
    inv_freq = ROPE_BASE ** (-jnp.arange(n_freq, dtype=F32) / n_freq)
    rows = seq // GRID_W
    row = jnp.repeat(jnp.arange(rows, dtype=F32), GRID_W)
    col = jnp.tile(jnp.arange(GRID_W, dtype=F32), rows)
    ang = jnp.concatenate([row[:, None] * inv_freq, col[:, None] * inv_freq], axis=-1)
    cos, sin = jnp.cos(ang), jnp.sin(ang)
    return jnp.tile(cos, (1, 4)), jnp.concatenate([-sin, sin, -sin, sin], axis=1)


def _window_bias():
    key = np.arange(3 * BLOCK)[:, None]
    qry = np.arange(BLOCK)[None, :]
    valid = (key - qry >= 0) & (key - qry <= 2 * BLOCK)
    return jnp.asarray(np.where(valid, 0.0, NEG_INF), F32)


def kernel(x, c, ctx, c_ctx, w_ada, b_ada, norm1_g, w_in, w_conv, b_conv, w_a, w_b, sink, w_o,
           norm2_g, w_group, b_group, w_router, b_router, w_up, w_down, final_g):
    bsz, seq, _ = x.shape
    assert w_ada.shape[0] == 1 and seq % SEQ_TILE == 0

    cc = jnp.zeros((16, D_MODEL), F32).at[:bsz].set(c).at[bsz].set(c_ctx)
    mod = _modulation(cc, w_ada[0], b_ada[0][None, :])
    mod3 = mod[:bsz].reshape(bsz, 1, N_MOD * D_MODEL)
    csh1 = mod[bsz:bsz + 1, 0:D_MODEL]
    csc1 = mod[bsz:bsz + 1, D_MODEL:2 * D_MODEL]

    w = w_in[0]
    n1g = norm1_g[0][None, :]
    n2g = norm2_g[0][None, :]
    w_kv = w[:, OFF_K:OFF_GA].astype(BF16)
    w_a_cols = jnp.concatenate([w[:, OFF_CG:OFF_Q], w[:, OFF_K:OFF_GA]], axis=1).astype(BF16)
    w_b_cols = jnp.concatenate([w[:, OFF_B:OFF_CG], w[:, OFF_Q:OFF_K], w[:, OFF_GA:]],
                               axis=1).astype(BF16)
    cos_t, sin_t = _rope_tables(seq)

    k_ctx, vt_ctx = _context_kv(ctx, n1g, csh1, csc1, w_kv)
    uc, k_rot, v_t = _proj_a(x, n1g, mod3, w_a_cols, cos_t, sin_t)

    sink_row = jnp.repeat(sink[0].astype(F32), BLOCK)[None, :]
    pad_g = jnp.zeros((D_MODEL, GROUP_ROWS - N_GROUPS), F32)
    pad_e = jnp.zeros((D_MODEL, ROUTER_ROWS - GROUP_ROWS - N_EXPERTS), F32)
    wr_t = jnp.concatenate([w_group[0], pad_g, w_router[0], pad_e], axis=1).T.astype(BF16)
    br_col = jnp.concatenate([b_group[0], pad_g[0], b_router[0], pad_e[0]])[:, None]

    x1, h2, route, w_up_bf, w_down_bf = _mix(
        x, n1g, n2g, mod3, w_b_cols, cos_t, sin_t, uc, k_rot, v_t, k_ctx, vt_ctx,
        w_conv[0], b_conv[0][None, :], w_a[0].astype(BF16), w_b[0].astype(BF16),
        w_o[0].astype(BF16), sink_row, _window_bias(), wr_t, br_col, w_up[0], w_down[0])

    n = bsz * seq
    nch = n // CHUNK
    upper = jnp.asarray(np.triu(np.ones((CHUNK, CHUNK), np.float32), 1), BF16)
    lower = jnp.asarray(np.tril(np.ones((N_EXPERTS, N_EXPERTS), np.float32), -1), BF16)
    xl, pos_c, nb = _dispatch(h2.reshape(n, D_MODEL), route, upper, lower)
    nb = nb[:, :, 0]
    expert_plan, slot_of = _tile_plan(nb, nch)
    yt = _grouped_mlp(expert_plan, xl.reshape(-1, ROW_BLOCK, XL_COLS), w_up_bf, w_down_bf,
                      _max_tiles(nch))
    used_rows = (ROW_BLOCK * jnp.sum(nb, axis=1)).astype(I32)
    out = _combine(used_rows, slot_of, yt, pos_c, x1.reshape(n, D_MODEL), mod3, final_g[None, :],
                   seq)
    return out.reshape(bsz, seq, D_MODEL)
```

```python
import numpy as np
import jax
import jax.numpy as jnp
from jax import lax
from jax.experimental import pallas as pl
from jax.experimental.pallas import tpu as pltpu

F32 = jnp.float32
BF16 = jnp.bfloat16
I32 = jnp.int32

D_MODEL = 1024
GRID_W = 64
CONV_W = 512
N_HEADS = 8
N_KV_HEADS = 2
HEAD_DIM = 64
ATT_W = N_HEADS * HEAD_DIM
KV_W = N_KV_HEADS * HEAD_DIM
BLOCK = 128
ROPE_BASE = 10000.0
N_GROUPS = 4
EXPERTS_PER_GROUP = 8
N_EXPERTS = N_GROUPS * EXPERTS_PER_GROUP
EXPERT_FF = 256
N_MOD = 6
NORM_EPS = 1e-6
NEG_INF = -1e30
LOG2E = 1.4426950408889634
ONES_ROWS = 16

OFF_B, OFF_CG, OFF_XIN, OFF_Q, OFF_K, OFF_V, OFF_GA, OFF_GB = (
    0, 512, 1024, 1536, 2048, 2176, 2304, 3328)
IN_COLS = 4352

LANES = 128
SEQ_TILE = 512
BLOCKS_PER_TILE = SEQ_TILE // BLOCK
ROUTE_ROWS = 8
GROUP_ROWS = 8
ROUTER_ROWS = 48
VMEM_LIMIT = 56 * 1024 * 1024

CHUNK = SEQ_TILE
ROW_BLOCK = 16
CHUNK_ROWS = -(-(2 * CHUNK + N_EXPERTS * (ROW_BLOCK - 1)) // 256) * 256
CHUNK_BLOCKS = CHUNK_ROWS // ROW_BLOCK
GATE_COLS = LANES
XL_COLS = D_MODEL + GATE_COLS
COMMON_ROWS = 2 * CHUNK + 256
COMMON_BLOCKS = COMMON_ROWS // ROW_BLOCK
TILE_BLOCKS = 32
TILE_ROWS = TILE_BLOCKS * ROW_BLOCK
X_SLOTS = 6
Y_SLOTS = 4
W_SLOTS = 3


def _cparams(sem):
    return pltpu.CompilerParams(dimension_semantics=sem, vmem_limit_bytes=VMEM_LIMIT)


def _rms_mod(x, g, shift, scale):
    inv = lax.rsqrt(jnp.mean(x * x, axis=-1, keepdims=True) + NORM_EPS)
    return (x * inv) * (g * (1.0 + scale)) + shift


def _mod_kernel(c_ref, w_ref, b_ref, o_ref):
    c = c_ref[...]
    a = (c * jax.nn.sigmoid(c)).astype(BF16)
    o_ref[...] = jnp.dot(a, w_ref[...].astype(BF16), preferred_element_type=F32) + b_ref[...]


def _modulation(cc, w_ada, b_ada):
    rows = cc.shape[0]
    cols = w_ada.shape[1]
    tile = 1024
    return pl.pallas_call(
        _mod_kernel,
        grid=(cols // tile,),
        in_specs=[pl.BlockSpec((rows, D_MODEL), lambda j: (0, 0)),
                  pl.BlockSpec((D_MODEL, tile), lambda j: (0, j)),
                  pl.BlockSpec((1, tile), lambda j: (0, j))],
        out_specs=pl.BlockSpec((rows, tile), lambda j: (0, j)),
        out_shape=jax.ShapeDtypeStruct((rows, cols), F32),
        compiler_params=_cparams(("arbitrary",)),
        name="adaln_mod",
    )(cc, w_ada, b_ada)


def _ctx_kernel(x_ref, g_ref, sh_ref, sc_ref, w_ref, k_ref, vt_ref):
    h = _rms_mod(x_ref[0], g_ref[...], sh_ref[...], sc_ref[...]).astype(BF16)
    kv = jnp.dot(h, w_ref[...], preferred_element_type=F32)
    k_ref[0] = kv[:, :KV_W].astype(BF16)
    vt_ref[0] = kv[:, KV_W:].T.astype(BF16)


def _context_kv(ctx, norm_g, csh, csc, w_kv):
    bsz, clen, _ = ctx.shape
    return pl.pallas_call(
        _ctx_kernel,
        grid=(bsz,),
        in_specs=[pl.BlockSpec((1, clen, D_MODEL), lambda b: (b, 0, 0)),
                  pl.BlockSpec((1, D_MODEL), lambda b: (0, 0)),
                  pl.BlockSpec((1, D_MODEL), lambda b: (0, 0)),
                  pl.BlockSpec((1, D_MODEL), lambda b: (0, 0)),
                  pl.BlockSpec((D_MODEL, 2 * KV_W), lambda b: (0, 0))],
        out_specs=[pl.BlockSpec((1, clen, KV_W), lambda b: (b, 0, 0)),
                   pl.BlockSpec((1, KV_W, clen), lambda b: (b, 0, 0))],
        out_shape=[jax.ShapeDtypeStruct((bsz, clen, KV_W), BF16),
                   jax.ShapeDtypeStruct((bsz, KV_W, clen), BF16)],
        compiler_params=_cparams(("arbitrary",)),
        name="context_kv",
    )(ctx, norm_g, csh, csc, w_kv)


def _rope(t, cos, sin_signed):
    lane = lax.broadcasted_iota(I32, (1, LANES), 1)
    first_half = (lane % HEAD_DIM) < (HEAD_DIM // 2)
    outs = []
    for j in range(t.shape[1] // LANES):
        tj = t[:, j * LANES:(j + 1) * LANES]
        partner = jnp.where(first_half,
                            pltpu.roll(tj, LANES - HEAD_DIM // 2, axis=1),
                            pltpu.roll(tj, HEAD_DIM // 2, axis=1))
        outs.append(tj * cos + partner * sin_signed)
    return outs[0] if len(outs) == 1 else jnp.concatenate(outs, axis=1)


def _proj_a_kernel(x_ref, g_ref, sh_ref, sc_ref, w_ref, cos_ref, sin_ref, uc_ref, k_ref, vt_ref):
    h = _rms_mod(x_ref[0], g_ref[...], sh_ref[0], sc_ref[0]).astype(BF16)
    u = jnp.dot(h, w_ref[...], preferred_element_type=F32)
    uc_ref[0] = u[:, :CONV_W] * u[:, CONV_W:2 * CONV_W]
    k = u[:, 2 * CONV_W:2 * CONV_W + KV_W]
    k_ref[0] = _rope(k, cos_ref[...], sin_ref[...]).astype(BF16)
    vt_ref[0] = u[:, 2 * CONV_W + KV_W:].T.astype(BF16)


def _proj_a(x, norm_g, mod3, w_a_cols, cos_t, sin_t):
    bsz, seq, _ = x.shape
    nt = seq // SEQ_TILE
    wcols = w_a_cols.shape[1]
    return pl.pallas_call(
        _proj_a_kernel,
        grid=(nt, bsz),
        in_specs=[pl.BlockSpec((1, SEQ_TILE, D_MODEL), lambda t, b: (b, t, 0)),
                  pl.BlockSpec((1, D_MODEL), lambda t, b: (0, 0)),
                  pl.BlockSpec((1, 1, D_MODEL), lambda t, b: (b, 0, 0)),
                  pl.BlockSpec((1, 1, D_MODEL), lambda t, b: (b, 0, 1)),
                  pl.BlockSpec((D_MODEL, wcols), lambda t, b: (0, 0)),
                  pl.BlockSpec((SEQ_TILE, LANES), lambda t, b: (t, 0)),
                  pl.BlockSpec((SEQ_TILE, LANES), lambda t, b: (t, 0))],
        out_specs=[pl.BlockSpec((1, SEQ_TILE, CONV_W), lambda t, b: (b, t, 0)),
                   pl.BlockSpec((1, SEQ_TILE, KV_W), lambda t, b: (b, t, 0)),
                   pl.BlockSpec((1, KV_W, SEQ_TILE), lambda t, b: (b, 0, t))],
        out_shape=[jax.ShapeDtypeStruct((bsz, seq, CONV_W), F32),
                   jax.ShapeDtypeStruct((bsz, seq, KV_W), BF16),
                   jax.ShapeDtypeStruct((bsz, KV_W, seq), BF16)],
        compiler_params=_cparams(("arbitrary", "arbitrary")),
        name="proj_a",
    )(x, norm_g, mod3, mod3, w_a_cols, cos_t, sin_t)


def _sigmoid(x):
    return 1.0 / (1.0 + jnp.exp2(x * (-LOG2E)))


def _attn_scores(q_t, g, k_win, k_ctx, bias_p4, bias_n4):
    zeros = jnp.zeros((HEAD_DIM, 4 * BLOCK), BF16)
    qg = jnp.concatenate([q_t[(4 * g + h) * HEAD_DIM:(4 * g + h + 1) * HEAD_DIM, :]
                          for h in range(4)], axis=1)
    rhs = jnp.concatenate([qg, zeros] if g == 0 else [zeros, qg], axis=0)
    s_win = jnp.dot(k_win, rhs, preferred_element_type=F32)
    s_ctx = jnp.dot(k_ctx, rhs, preferred_element_type=F32)
    return (s_win[0:BLOCK] + bias_p4, s_win[BLOCK:2 * BLOCK], s_win[2 * BLOCK:] + bias_n4, s_ctx)


def _attn_probs(scores, sink):
    m = sink
    for s in scores:
        m = jnp.maximum(m, jnp.max(s, axis=0, keepdims=True))
    p_win = jnp.concatenate([jnp.exp2(s - m).astype(BF16) for s in scores[:3]], axis=0)
    p_ctx = jnp.exp2(scores[3] - m).astype(BF16)
    return p_win, p_ctx, m


def _attn_values(probs, sink, vt_win_g, vt_ctx_g):
    p_win, p_ctx, m = probs

    def with_ones(vt):
        r = lax.broadcasted_iota(I32, (ONES_ROWS, vt.shape[1]), 0)
        return jnp.concatenate([vt, jnp.where(r == 0, 1.0, 0.0).astype(BF16)], axis=0)

    o_ext = (jnp.dot(with_ones(vt_win_g), p_win, preferred_element_type=F32)
             + jnp.dot(with_ones(vt_ctx_g), p_ctx, preferred_element_type=F32))
    denom = o_ext[HEAD_DIM:HEAD_DIM + 1, :] + jnp.exp2(sink - m)
    return o_ext[:HEAD_DIM, :] / denom


def _route(logits_t):
    t = logits_t.shape[1]
    grow = lax.broadcasted_iota(I32, (GROUP_ROWS, t), 0)
    gl = jnp.where(grow < N_GROUPS, logits_t[0:GROUP_ROWS, :], NEG_INF)
    gm = jnp.max(gl, axis=0, keepdims=True)
    p_g = 1.0 / jnp.sum(jnp.exp(gl - gm), axis=0, keepdims=True)
    g_idx = jnp.min(jnp.where(gl == gm, grow, N_GROUPS), axis=0, keepdims=True)

    erow = lax.broadcasted_iota(I32, (N_EXPERTS, t), 0)
    el = logits_t[GROUP_ROWS:GROUP_ROWS + N_EXPERTS, :]
    sel = (erow // EXPERTS_PER_GROUP) == g_idx
    em = jnp.where(sel, el, NEG_INF)
    m1 = jnp.max(em, axis=0, keepdims=True)
    i1 = jnp.min(jnp.where(em == m1, erow, N_EXPERTS), axis=0, keepdims=True)
    em2 = jnp.where(erow == i1, NEG_INF, em)
    m2 = jnp.max(em2, axis=0, keepdims=True)
    i2 = jnp.min(jnp.where(em2 == m2, erow, N_EXPERTS), axis=0, keepdims=True)
    z = jnp.sum(jnp.where(sel, jnp.exp(el - m1), 0.0), axis=0, keepdims=True)
    p1 = 1.0 / z
    p2 = jnp.exp(m2 - m1) / z
    gate1 = p_g * p1 / (p1 + p2)
    gate2 = p_g * p2 / (p1 + p2)
    pad = jnp.zeros((ROUTE_ROWS - 4, t), F32)
    return jnp.concatenate([i1.astype(F32), i2.astype(F32), gate1, gate2, pad], axis=0)


def _mix_kernel(x_ref, g1n_ref, g2n_ref, sh1_ref, sc1_ref, gt1_ref, sh2_ref, sc2_ref,
                wb_ref, cos_ref, sin_ref,
                uc_ref, ucp_ref, ucn_ref, k_ref, kp_ref, kn_ref, vt_ref, vtp_ref, vtn_ref,
                kc_ref, vtc_ref, wconv_ref, bconv_ref, wa_ref, wbb_ref, wo_ref,
                sink_ref, bias_ref, wr_ref, br_ref, wu_ref, wd_ref,
                x1_ref, h2_ref, route_ref, wub_ref, wdb_ref):
    t_idx = pl.program_id(0)
    n_tiles = pl.num_programs(0)
    is_first = t_idx == 0
    is_last = t_idx == n_tiles - 1

    x = x_ref[0]

    uc = uc_ref[0]
    row = lax.broadcasted_iota(I32, (SEQ_TILE, 1), 0)
    prev_row = jnp.where(is_first, 0.0, ucp_ref[0][7:8, :])
    next_row = jnp.where(is_last, 0.0, ucn_ref[0][0:1, :])
    up = jnp.where(row == 0, prev_row, pltpu.roll(uc, 1, axis=0))
    dn = jnp.where(row == SEQ_TILE - 1, next_row, pltpu.roll(uc, SEQ_TILE - 1, axis=0))
    wconv = wconv_ref[...]
    y = bconv_ref[...] + (up * wconv[0:1, :] + uc * wconv[1:2, :] + dn * wconv[2:3, :])

    hb = _rms_mod(x, g1n_ref[...], sh1_ref[0], sc1_ref[0]).astype(BF16)
    bq = jnp.dot(hb, wb_ref[:, 0:2 * CONV_W], preferred_element_type=F32)
    ya = jnp.dot((bq[:, :CONV_W] * y).astype(BF16), wa_ref[...], preferred_element_type=F32)
    q = _rope(bq[:, CONV_W:], cos_ref[...], sin_ref[...]) * (HEAD_DIM ** -0.5 * LOG2E)

    k_all = jnp.concatenate([kp_ref[0], k_ref[0], kn_ref[0]], axis=0)
    vt_all = jnp.concatenate([vtp_ref[0], vt_ref[0], vtn_ref[0]], axis=1)
    k_ctx = kc_ref[0]
    vt_ctx = vtc_ref[0]
    bias_prev = bias_ref[0:BLOCK, :]
    bias_next = bias_ref[2 * BLOCK:3 * BLOCK, :]
    sink_row = sink_ref[...] * LOG2E
    n_units = BLOCKS_PER_TILE * N_KV_HEADS
    gate_cols = 2 * D_MODEL // n_units
    gate_chunks = []

    def gate_chunk(u):
        c0 = 2 * CONV_W + u * gate_cols
        gate_chunks.append(jnp.dot(hb, wb_ref[:, c0:c0 + gate_cols], preferred_element_type=F32))

    q_ts, biases = [], []
    for jb in range(BLOCKS_PER_TILE):
        bias_p, bias_n = bias_prev, bias_next
        if jb == 0:
            bias_p = bias_prev + jnp.where(is_first, NEG_INF, 0.0)
        if jb == BLOCKS_PER_TILE - 1:
            bias_n = bias_next + jnp.where(is_last, NEG_INF, 0.0)
        biases.append((jnp.concatenate([bias_p] * 4, axis=1), jnp.concatenate([bias_n] * 4, axis=1)))
        q_ts.append(q[jb * BLOCK:(jb + 1) * BLOCK, :].T.astype(BF16))

    scores, probs, outs = {}, {}, {}
    for t in range(n_units + 2):
        if t == 3:
            wub_ref[...] = wu_ref[...].astype(BF16)
        if t == 5:
            wdb_ref[...] = wd_ref[...].astype(BF16)
        if t < n_units:
            jb, g = divmod(t, N_KV_HEADS)
            scores[t] = _attn_scores(q_ts[jb], g, k_all[jb * BLOCK:(jb + 3) * BLOCK, :], k_ctx,
                                     *biases[jb])
            gate_chunk(t)
        if 0 <= t - 1 < n_units:
            g = (t - 1) % N_KV_HEADS
            probs[t - 1] = _attn_probs(scores.pop(t - 1),
                                       sink_row[:, g * 4 * BLOCK:(g + 1) * 4 * BLOCK])
        if 0 <= t - 2 < n_units:
            jb, g = divmod(t - 2, N_KV_HEADS)
            outs[t - 2] = _attn_values(
                probs.pop(t - 2), sink_row[:, g * 4 * BLOCK:(g + 1) * 4 * BLOCK],
                vt_all[g * HEAD_DIM:(g + 1) * HEAD_DIM, jb * BLOCK:(jb + 3) * BLOCK],
                vt_ctx[g * HEAD_DIM:(g + 1) * HEAD_DIM, :])
    o_blocks = []
    for jb in range(BLOCKS_PER_TILE):
        o_rows = [outs[jb * N_KV_HEADS + g][:, h * BLOCK:(h + 1) * BLOCK]
                  for g in range(N_KV_HEADS) for h in range(4)]
        o_blocks.append(jnp.concatenate(o_rows, axis=0).T)
    gates = jnp.concatenate(gate_chunks, axis=1)

    o = jnp.concatenate(o_blocks, axis=0).astype(BF16)
    yb = jnp.dot(o, wbb_ref[...], preferred_element_type=F32)
    merged = (_sigmoid(gates[:, :D_MODEL]) * ya + _sigmoid(gates[:, D_MODEL:]) * yb).astype(BF16)
    x1 = x + gt1_ref[0] * jnp.dot(merged, wo_ref[...], preferred_element_type=F32)
    x1_ref[0] = x1
    h2 = _rms_mod(x1, g2n_ref[...], sh2_ref[0], sc2_ref[0]).astype(BF16)
    h2_ref[0] = h2
    logits_t = lax.dot_general(wr_ref[...], h2, (((1,), (1,)), ((), ())),
                               preferred_element_type=F32) + br_ref[...]
    route_ref[0] = _route(logits_t)


def _mix(x, norm1_g, norm2_g, mod3, w_b_cols, cos_t, sin_t, uc, k_rot, v_t, k_ctx, vt_ctx,
         w_conv, b_conv, w_a, w_b, w_o, sink_row, bias, wr_t, br_col, w_up, w_down):
    bsz, seq, _ = x.shape
    nt = seq // SEQ_TILE
    nblk = seq // BLOCK
    clen = k_ctx.shape[1]
    rows8 = SEQ_TILE // 8
    assert N_EXPERTS % (nt * bsz) == 0, "each grid step converts an equal share of the experts"
    e_step = N_EXPERTS // (nt * bsz)

    def mod_spec(j):
        return pl.BlockSpec((1, 1, D_MODEL), lambda t, b, j=j: (b, 0, j))

    def const_spec(shape):
        return pl.BlockSpec(shape, lambda t, b: tuple(0 for _ in shape))

    in_specs = [
        pl.BlockSpec((1, SEQ_TILE, D_MODEL), lambda t, b: (b, t, 0)),
        const_spec((1, D_MODEL)), const_spec((1, D_MODEL)),
        mod_spec(0), mod_spec(1), mod_spec(2), mod_spec(3), mod_spec(4),
        const_spec(w_b_cols.shape),
        pl.BlockSpec((SEQ_TILE, LANES), lambda t, b: (t, 0)),
        pl.BlockSpec((SEQ_TILE, LANES), lambda t, b: (t, 0)),
        pl.BlockSpec((1, SEQ_TILE, CONV_W), lambda t, b: (b, t, 0)),
        pl.BlockSpec((1, 8, CONV_W), lambda t, b: (b, jnp.maximum(t * rows8 - 1, 0), 0)),
        pl.BlockSpec((1, 8, CONV_W),
                     lambda t, b: (b, jnp.minimum((t + 1) * rows8, seq // 8 - 1), 0)),
        pl.BlockSpec((1, SEQ_TILE, KV_W), lambda t, b: (b, t, 0)),
        pl.BlockSpec((1, BLOCK, KV_W),
                     lambda t, b: (b, jnp.maximum(t * BLOCKS_PER_TILE - 1, 0), 0)),
        pl.BlockSpec((1, BLOCK, KV_W),
                     lambda t, b: (b, jnp.minimum((t + 1) * BLOCKS_PER_TILE, nblk - 1), 0)),
        pl.BlockSpec((1, KV_W, SEQ_TILE), lambda t, b: (b, 0, t)),
        pl.BlockSpec((1, KV_W, BLOCK),
                     lambda t, b: (b, 0, jnp.maximum(t * BLOCKS_PER_TILE - 1, 0))),
        pl.BlockSpec((1, KV_W, BLOCK),
                     lambda t, b: (b, 0, jnp.minimum((t + 1) * BLOCKS_PER_TILE, nblk - 1))),
        pl.BlockSpec((1, clen, KV_W), lambda t, b: (b, 0, 0)),
        pl.BlockSpec((1, KV_W, clen), lambda t, b: (b, 0, 0)),
        const_spec(w_conv.shape), const_spec(b_conv.shape),
        const_spec(w_a.shape), const_spec(w_b.shape), const_spec(w_o.shape),
        const_spec(sink_row.shape), const_spec(bias.shape),
        const_spec(wr_t.shape), const_spec(br_col.shape),
        pl.BlockSpec((e_step,) + w_up.shape[1:], lambda t, b: (t * bsz + b, 0, 0)),
        pl.BlockSpec((e_step,) + w_down.shape[1:], lambda t, b: (t * bsz + b, 0, 0)),
    ]
    out_specs = [
        pl.BlockSpec((1, SEQ_TILE, D_MODEL), lambda t, b: (b, t, 0)),
        pl.BlockSpec((1, SEQ_TILE, D_MODEL), lambda t, b: (b, t, 0)),
        pl.BlockSpec((1, ROUTE_ROWS, SEQ_TILE), lambda t, b: (b, 0, t)),
        pl.BlockSpec((e_step,) + w_up.shape[1:], lambda t, b: (t * bsz + b, 0, 0)),
        pl.BlockSpec((e_step,) + w_down.shape[1:], lambda t, b: (t * bsz + b, 0, 0)),
    ]
    out_shape = [
        jax.ShapeDtypeStruct((bsz, seq, D_MODEL), F32),
        jax.ShapeDtypeStruct((bsz, seq, D_MODEL), BF16),
        jax.ShapeDtypeStruct((bsz, ROUTE_ROWS, seq), F32),
        jax.ShapeDtypeStruct(w_up.shape, BF16),
        jax.ShapeDtypeStruct(w_down.shape, BF16),
    ]
    return pl.pallas_call(
        _mix_kernel,
        grid=(nt, bsz),
        in_specs=in_specs,
        out_specs=out_specs,
        out_shape=out_shape,
        compiler_params=_cparams(("arbitrary", "arbitrary")),
        name="token_mix",
    )(x, norm1_g, norm2_g, mod3, mod3, mod3, mod3, mod3, w_b_cols, cos_t, sin_t,
      uc, uc, uc, k_rot, k_rot, k_rot, v_t, v_t, v_t, k_ctx, vt_ctx,
      w_conv, b_conv, w_a, w_b, w_o, sink_row, bias, wr_t, br_col, w_up, w_down)


def _bf16_parts(v):
    hi = v.astype(BF16).astype(F32)
    r1 = v - hi
    mid = r1.astype(BF16).astype(F32)
    lo = (r1 - mid).astype(BF16).astype(F32)
    return hi, mid, lo


def _dispatch_kernel(h_ref, route_ref, upper_ref, lower_ref, xl_ref, pos_ref, nb_ref):
    route = route_ref[0]
    e1 = route[0:1, :].astype(I32)
    e2 = route[1:2, :].astype(I32)
    erow = lax.broadcasted_iota(I32, (N_EXPERTS, CHUNK), 0)
    hit1 = erow == e1
    hit2 = erow == e2
    onehot = jnp.where(hit1, 1.0, 0.0) + jnp.where(hit2, 1.0, 0.0)
    cum = jnp.dot(onehot.astype(BF16), upper_ref[...], preferred_element_type=F32)
    cnt = jnp.sum(onehot, axis=1, keepdims=True)
    nblk = jnp.floor((cnt + (ROW_BLOCK - 1)) * (1.0 / ROW_BLOCK))
    nblk_b = jnp.broadcast_to(nblk, (N_EXPERTS, LANES))
    seg = jnp.dot(lower_ref[...], nblk_b.astype(BF16), preferred_element_type=F32) * ROW_BLOCK
    base = seg[:, 0:1] + cum
    pos1 = jnp.sum(jnp.where(hit1, base, 0.0), axis=0, keepdims=True)
    pos2 = jnp.sum(jnp.where(hit2, base, 0.0), axis=0, keepdims=True)
    p1i = pos1.astype(I32)
    p2i = pos2.astype(I32)
    used_rows = (jnp.sum(nblk) * ROW_BLOCK).astype(I32)

    prow = lax.broadcasted_iota(I32, (LANES, CHUNK), 0)
    parts = _bf16_parts(route[2:3, :]) + _bf16_parts(route[3:4, :])
    gpart_rows = jnp.zeros((LANES, CHUNK), F32)
    for j, part in enumerate(parts):
        gpart_rows = jnp.where(prow == j, part, gpart_rows)
    h_ext = jnp.concatenate([h_ref[...], gpart_rows.T.astype(BF16)], axis=1)
    glane = lax.broadcasted_iota(I32, (1, GATE_COLS), 1)
    one = jnp.ones((), BF16)
    zero = jnp.zeros((), BF16)
    def permute(row0, n_rows):
        r16 = lax.broadcasted_iota(I32, (n_rows, CHUNK), 0).astype(jnp.int16)
        q1 = (p1i - row0).astype(jnp.int16)
        q2 = (p2i - row0).astype(jnp.int16)
        sel1 = jnp.where(r16 == q1, one, zero)
        sel = jnp.where(r16 == q2, one, sel1)
        is_slot1 = jnp.max(sel1, axis=1, keepdims=True)
        rows = pl.ds(row0, n_rows)
        xg = jnp.dot(sel, h_ext, preferred_element_type=F32)
        xl_ref[rows, 0:D_MODEL] = xg[:, :D_MODEL].astype(BF16)
        g6 = xg[:, D_MODEL:]
        g3 = jnp.where(is_slot1.astype(F32) > 0.0, g6, pltpu.roll(g6, GATE_COLS - 3, axis=1))
        xl_ref[rows, D_MODEL:XL_COLS] = jnp.where(glane < 3, g3, 0.0).astype(BF16)

    permute(0, COMMON_ROWS)
    tail_rows = CHUNK_ROWS - COMMON_ROWS

    @pl.when(COMMON_ROWS < used_rows)
    def _():
        permute(COMMON_ROWS, tail_rows)

    @pl.when(COMMON_ROWS >= used_rows)
    def _():
        xl_ref[pl.ds(COMMON_ROWS, tail_rows), :] = jnp.zeros((tail_rows, XL_COLS), BF16)

    pos_rows = jnp.where(prow == 0, pos1, jnp.where(prow == 1, pos2, 0.0))
    pos_ref[...] = pos_rows.T
    nb_ref[0] = nblk_b.astype(I32)


def _dispatch(h2, route, upper, lower):
    n = h2.shape[0]
    nch = n // CHUNK
    per_seq = route.shape[2] // CHUNK
    return pl.pallas_call(
        _dispatch_kernel,
        grid=(nch,),
        in_specs=[pl.BlockSpec((CHUNK, D_MODEL), lambda c: (c, 0)),
                  pl.BlockSpec((1, ROUTE_ROWS, CHUNK), lambda c: (c // per_seq, 0, c % per_seq)),
                  pl.BlockSpec(upper.shape, lambda c: (0, 0)),
                  pl.BlockSpec(lower.shape, lambda c: (0, 0))],
        out_specs=[pl.BlockSpec((CHUNK_ROWS, XL_COLS), lambda c: (c, 0)),
                   pl.BlockSpec((CHUNK, LANES), lambda c: (c, 0)),
                   pl.BlockSpec((1, N_EXPERTS, LANES), lambda c: (c, 0, 0))],
        out_shape=[jax.ShapeDtypeStruct((nch * CHUNK_ROWS, XL_COLS), BF16),
                   jax.ShapeDtypeStruct((n, LANES), F32),
                   jax.ShapeDtypeStruct((nch, N_EXPERTS, LANES), I32)],
        compiler_params=_cparams(("arbitrary",)),
        name="moe_dispatch",
    )(h2, route, upper, lower)


def _max_tiles(nch):
    max_blocks = nch * (2 * CHUNK + N_EXPERTS * (ROW_BLOCK - 1)) // ROW_BLOCK
    return max_blocks // TILE_BLOCKS + N_EXPERTS


def _masked_prefix(le, values):
    delta = values - jnp.concatenate([jnp.zeros((1,), values.dtype), values[:-1]])
    return jnp.sum(jnp.where(le, delta[None, :], 0), axis=1)


def _tile_plan(nb, nch):
    n_tiles = _max_tiles(nch)
    nbt = nb.T
    nbe = jnp.sum(nbt, axis=1)
    nte = (nbe + TILE_BLOCKS - 1) // TILE_BLOCKS
    tile_end = jnp.cumsum(nte)
    tile_start = tile_end - nte
    n_act = tile_end[-1]
    tiles = jnp.arange(n_tiles, dtype=I32)
    te = jnp.sum((tile_end[None, :] <= tiles[:, None]).astype(I32), axis=1)
    e_ar = jnp.arange(N_EXPERTS, dtype=I32)
    active = nte > 0
    te = jnp.where(tiles < n_act, te, jnp.max(jnp.where(active, e_ar, 0)))
    first = jnp.logical_and(te != jnp.concatenate([jnp.full((1,), -1, I32), te[:-1]]),
                            tiles < n_act).astype(I32)
    rank = jnp.cumsum(active.astype(I32)) - 1
    k_ar = jnp.arange(N_EXPERTS + W_SLOTS, dtype=I32)
    eseq = jnp.sum(jnp.where(jnp.logical_and(active[None, :], rank[None, :] == k_ar[:, None]),
                             e_ar[None, :], 0), axis=1)
    n_exp = jnp.sum(active.astype(I32))

    cb_excl = jnp.cumsum(nbt, axis=1) - nbt
    gs = TILE_BLOCKS * tile_start[:, None] + cb_excl
    seg_blk = jnp.cumsum(nb, axis=1) - nb
    base_blk = jnp.arange(nch, dtype=I32)[None, :] * CHUNK_BLOCKS + seg_blk.T
    gs_f = gs.reshape(-1)
    slots = jnp.arange(n_tiles * TILE_BLOCKS, dtype=I32)
    le = gs_f[None, :] <= slots[:, None]
    blk = slots + _masked_prefix(le, (base_blk - gs).reshape(-1))
    valid = slots < _masked_prefix(le, (gs + nbt).reshape(-1))
    gblk = jnp.concatenate([jnp.where(valid, blk, 0).astype(I32),
                            jnp.zeros(((X_SLOTS - 1) * TILE_BLOCKS,), I32)])

    m = jnp.arange(CHUNK_BLOCKS, dtype=I32)
    le_c = seg_blk[:, None, :] <= m[None, :, None]
    shift = (gs.T - seg_blk)
    delta = shift - jnp.concatenate([jnp.zeros((nch, 1), I32), shift[:, :-1]], axis=1)
    slot_of = m[None, :] + jnp.sum(jnp.where(le_c, delta[:, None, :], 0), axis=2)
    used_blocks = jnp.sum(nb, axis=1)
    slot_of = jnp.where(m[None, :] < used_blocks[:, None], slot_of, 0).astype(I32)
    slot_of = jnp.concatenate([slot_of.reshape(-1), jnp.zeros((CHUNK_BLOCKS,), I32)])
    used_rows = jnp.concatenate([ROW_BLOCK * used_blocks, jnp.zeros((1,), I32)]).astype(I32)
    return ((n_act.reshape(1).astype(I32), n_exp.reshape(1), first, eseq, gblk),
            (used_rows, slot_of))


def _gmm_kernel(nact_ref, nexp_ref, first_ref, eseq_ref, gblk_ref,
                xl_hbm, wu_hbm, wd_hbm, yt_hbm,
                xbuf, ybuf, zbuf, wu_st, wd_st, in_sem, out_sem, w_sem, z_sem):
    n_act = nact_ref[0]
    n_exp = nexp_ref[0]
    n_tiles = yt_hbm.shape[0] // TILE_BLOCKS
    prefetch = X_SLOTS - 1

    def gather(tile, b):
        s = tile % X_SLOTS
        blk = gblk_ref[tile * TILE_BLOCKS + b]
        return pltpu.make_async_copy(xl_hbm.at[blk], xbuf.at[s, b], in_sem.at[s])

    def write_back(tile):
        s = tile % Y_SLOTS
        return pltpu.make_async_copy(ybuf.at[s], yt_hbm.at[pl.ds(tile * TILE_BLOCKS, TILE_BLOCKS)],
                                     out_sem.at[s])

    def zero_fill(tile):
        return pltpu.make_async_copy(zbuf, yt_hbm.at[pl.ds(tile * TILE_BLOCKS, TILE_BLOCKS)], z_sem)

    def weights(q):
        e = eseq_ref[q]
        s = q % W_SLOTS
        return (pltpu.make_async_copy(wu_hbm.at[e], wu_st.at[s], w_sem.at[0, s]),
                pltpu.make_async_copy(wd_hbm.at[e], wd_st.at[s], w_sem.at[1, s]))

    def tile_blocks(fn):
        for b in range(TILE_BLOCKS):
            fn(b)

    for q0 in range(W_SLOTS - 1):
        @pl.when(q0 < n_exp)
        def _(q0=q0):
            for cp in weights(q0):
                cp.start()
    for t0 in range(prefetch):
        tile_blocks(lambda b, t0=t0: gather(t0, b).start())

    def body(i, q):
        tile_blocks(lambda b: gather(i, b).wait())

        @pl.when(i >= Y_SLOTS)
        def _():
            write_back(i - Y_SLOTS).wait()

        is_first = first_ref[i] == 1

        @pl.when(is_first)
        def _():
            for cp in weights(q):
                cp.wait()

            @pl.when(q + W_SLOTS - 1 < n_exp)
            def _():
                for cp in weights(q + W_SLOTS - 1):
                    cp.start()

        q = q + is_first.astype(I32)
        ws = (q - 1) % W_SLOTS
        xs = i % X_SLOTS
        ys = i % Y_SLOTS

        x = xbuf[xs].reshape(TILE_ROWS, XL_COLS)
        gate = jnp.sum(x[:, D_MODEL:].astype(F32), axis=1, keepdims=True)
        au = jnp.dot(x[:, :D_MODEL], wu_st[ws], preferred_element_type=F32)
        tile_blocks(lambda b: gather(i + prefetch, b).start())
        a = au[:, :EXPERT_FF]
        act = (a * _sigmoid(a)) * au[:, EXPERT_FF:]
        y = jnp.dot(act.astype(BF16), wd_st[ws], preferred_element_type=F32)
        ybuf[ys] = (gate * y).astype(BF16).reshape(TILE_BLOCKS, ROW_BLOCK, D_MODEL)
        write_back(i).start()

        @pl.when(n_act + i < n_tiles)
        def _():
            zero_fill(n_act + i).start()
        return q

    zbuf[...] = jnp.zeros_like(zbuf)
    lax.fori_loop(0, n_act, body, jnp.int32(0))

    def fill_rest(t, carry):
        zero_fill(t).start()
        return carry

    lax.fori_loop(jnp.minimum(2 * n_act, n_tiles), n_tiles, fill_rest, 0)

    def drain(t, carry):
        zero_fill(t).wait()
        return carry

    lax.fori_loop(n_act, n_tiles, drain, 0)
    for k in range(prefetch):
        tile_blocks(lambda b, k=k: gather(n_act + k, b).wait())
    for k in range(Y_SLOTS):
        @pl.when(n_act - 1 - k >= 0)
        def _(k=k):
            write_back(n_act - 1 - k).wait()


def _grouped_mlp(plan, xl, w_up, w_down, n_tiles):
    grid_spec = pltpu.PrefetchScalarGridSpec(
        num_scalar_prefetch=len(plan),
        grid=(1,),
        in_specs=[pl.BlockSpec(memory_space=pl.ANY)] * 3,
        out_specs=pl.BlockSpec(memory_space=pl.ANY),
        scratch_shapes=[pltpu.VMEM((X_SLOTS, TILE_BLOCKS, ROW_BLOCK, XL_COLS), BF16),
                        pltpu.VMEM((Y_SLOTS, TILE_BLOCKS, ROW_BLOCK, D_MODEL), BF16),
                        pltpu.VMEM((TILE_BLOCKS, ROW_BLOCK, D_MODEL), BF16),
                        pltpu.VMEM((W_SLOTS, D_MODEL, 2 * EXPERT_FF), BF16),
                        pltpu.VMEM((W_SLOTS, EXPERT_FF, D_MODEL), BF16),
                        pltpu.SemaphoreType.DMA((X_SLOTS,)),
                        pltpu.SemaphoreType.DMA((Y_SLOTS,)),
                        pltpu.SemaphoreType.DMA((2, W_SLOTS)),
                        pltpu.SemaphoreType.DMA(())])
    return pl.pallas_call(
        _gmm_kernel,
        grid_spec=grid_spec,
        out_shape=jax.ShapeDtypeStruct((n_tiles * TILE_BLOCKS, ROW_BLOCK, D_MODEL), BF16),
        compiler_params=_cparams(("arbitrary",)),
        name="moe_experts",
    )(*plan, xl, w_up, w_down)


def _combine_kernel(used_ref, slot_ref, yt_hbm, pos_ref, x1_ref, gt2_ref, fg_ref, o_ref, ybuf, sem):
    c = pl.program_id(0)
    n_chunks = pl.num_programs(0)
    used_rows = used_ref[c]

    def fetch(chunk, b):
        s = chunk % 2
        slot = slot_ref[chunk * CHUNK_BLOCKS + b]
        return pltpu.make_async_copy(yt_hbm.at[slot], ybuf.at[s, b], sem.at[s])

    def common_blocks(fn):
        for b in range(COMMON_BLOCKS):
            fn(b)

    def tail_blocks(chunk, fn):
        @pl.when(used_ref[chunk] > COMMON_ROWS)
        def _():
            for b in range(COMMON_BLOCKS, CHUNK_BLOCKS):
                fn(b)

    @pl.when(c == 0)
    def _():
        common_blocks(lambda b: fetch(0, b).start())
        tail_blocks(0, lambda b: fetch(0, b).start())

    tail_blocks(c + 1, lambda b: fetch(c + 1, b).start())
    common_blocks(lambda b: fetch(c, b).wait())
    tail_blocks(c, lambda b: fetch(c, b).wait())

    def body(k_rows):
        pos = pos_ref[...]
        p1 = pos[:, 0:1].astype(I32)
        p2 = pos[:, 1:2].astype(I32)
        r = lax.broadcasted_iota(I32, (CHUNK, k_rows), 1)
        sel = jnp.where(r == p1, 1.0, jnp.where(r == p2, 1.0, 0.0)).astype(BF16)
        yl = ybuf[c % 2, 0:k_rows // ROW_BLOCK].reshape(k_rows, D_MODEL)
        y = jnp.dot(sel, yl, preferred_element_type=F32)
        common_blocks(lambda b: fetch(c + 1, b).start())
        x2 = x1_ref[...] + gt2_ref[0] * y
        inv = lax.rsqrt(jnp.mean(x2 * x2, axis=-1, keepdims=True) + NORM_EPS)
        o_ref[...] = (x2 * inv) * fg_ref[...]

    pl.when(used_rows <= COMMON_ROWS)(lambda: body(COMMON_ROWS))
    pl.when(used_rows > COMMON_ROWS)(lambda: body(CHUNK_ROWS))

    @pl.when(c == n_chunks - 1)
    def _():
        common_blocks(lambda b: fetch(c + 1, b).wait())


def _combine(used_rows, slot_of, yt, pos_c, x1, mod3, final_g, seq):
    n = x1.shape[0]
    per_seq = seq // CHUNK
    grid_spec = pltpu.PrefetchScalarGridSpec(
        num_scalar_prefetch=2,
        grid=(n // CHUNK,),
        in_specs=[pl.BlockSpec(memory_space=pl.ANY),
                  pl.BlockSpec((CHUNK, LANES), lambda c, u, s: (c, 0)),
                  pl.BlockSpec((CHUNK, D_MODEL), lambda c, u, s: (c, 0)),
                  pl.BlockSpec((1, 1, D_MODEL), lambda c, u, s: (c // per_seq, 0, 5)),
                  pl.BlockSpec((1, D_MODEL), lambda c, u, s: (0, 0))],
        out_specs=pl.BlockSpec((CHUNK, D_MODEL), lambda c, u, s: (c, 0)),
        scratch_shapes=[pltpu.VMEM((2, CHUNK_BLOCKS, ROW_BLOCK, D_MODEL), BF16),
                        pltpu.SemaphoreType.DMA((2,))])
    return pl.pallas_call(
        _combine_kernel,
        grid_spec=grid_spec,
        out_shape=jax.ShapeDtypeStruct((n, D_MODEL), F32),
        compiler_params=_cparams(("arbitrary",)),
        name="moe_combine",
    )(used_rows, slot_of, yt, pos_c, x1, mod3, final_g)


def _rope_tables(seq):
    n_freq = HEAD_DIM // 4
    inv_freq = ROPE_BASE ** (-jnp.arange(n_freq, dtype=F32) / n_freq)
    rows = seq // GRID_W
    row = jnp.repeat(jnp.arange(rows, dtype=F32), GRID_W)
    col = jnp.tile(jnp.arange(GRID_W, dtype=F32), rows)
    ang = jnp.concatenate([row[:, None] * inv_freq, col[:, None] * inv_freq], axis=-1)
    cos, sin = jnp.cos(ang), jnp.sin(ang)
    return jnp.tile(cos, (1, 4)), jnp.concatenate([-sin, sin, -sin, sin], axis=1)


def _window_bias():
    key = np.arange(3 * BLOCK)[:, None]
    qry = np.arange(BLOCK)[None, :]
    valid = (key - qry >= 0) & (key - qry <= 2 * BLOCK)
    return jnp.asarray(np.where(valid, 0.0, NEG_INF), F32)


def kernel(x, c, ctx, c_ctx, w_ada, b_ada, norm1_g, w_in, w_conv, b_conv, w_a, w_b, sink, w_o,
           norm2_g, w_group, b_group, w_router, b_router, w_up, w_down, final_g):
    bsz, seq, _ = x.shape
    assert w_ada.shape[0] == 1 and seq % SEQ_TILE == 0

    cc = jnp.zeros((16, D_MODEL), F32).at[:bsz].set(c).at[bsz].set(c_ctx)
    mod = _modulation(cc, w_ada[0], b_ada[0][None, :])
    mod3 = mod[:bsz].reshape(bsz, 1, N_MOD * D_MODEL)
    csh1 = mod[bsz:bsz + 1, 0:D_MODEL]
    csc1 = mod[bsz:bsz + 1, D_MODEL:2 * D_MODEL]

    w = w_in[0]
    n1g = norm1_g[0][None, :]
    n2g = norm2_g[0][None, :]
    w_kv = w[:, OFF_K:OFF_GA].astype(BF16)
    w_a_cols = jnp.concatenate([w[:, OFF_CG:OFF_Q], w[:, OFF_K:OFF_GA]], axis=1).astype(BF16)
    w_b_cols = jnp.concatenate([w[:, OFF_B:OFF_CG], w[:, OFF_Q:OFF_K], w[:, OFF_GA:]],
                               axis=1).astype(BF16)
    cos_t, sin_t = _rope_tables(seq)

    k_ctx, vt_ctx = _context_kv(ctx, n1g, csh1, csc1, w_kv)
    uc, k_rot, v_t = _proj_a(x, n1g, mod3, w_a_cols, cos_t, sin_t)

    sink_row = jnp.repeat(sink[0].astype(F32), BLOCK)[None, :]
    pad_g = jnp.zeros((D_MODEL, GROUP_ROWS - N_GROUPS), F32)
    pad_e = jnp.zeros((D_MODEL, ROUTER_ROWS - GROUP_ROWS - N_EXPERTS), F32)
    wr_t = jnp.concatenate([w_group[0], pad_g, w_router[0], pad_e], axis=1).T.astype(BF16)
    br_col = jnp.concatenate([b_group[0], pad_g[0], b_router[0], pad_e[0]])[:, None]

    x1, h2, route, w_up_bf, w_down_bf = _mix(
        x, n1g, n2g, mod3, w_b_cols, cos_t, sin_t, uc, k_rot, v_t, k_ctx, vt_ctx,
        w_conv[0], b_conv[0][None, :], w_a[0].astype(BF16), w_b[0].astype(BF16),
        w_o[0].astype(BF16), sink_row, _window_bias(), wr_t, br_col, w_up[0], w_down[0])

    n = bsz * seq
    nch = n // CHUNK
    upper = jnp.asarray(np.triu(np.ones((CHUNK, CHUNK), np.float32), 1), BF16)
    lower = jnp.asarray(np.tril(np.ones((N_EXPERTS, N_EXPERTS), np.float32), -1), BF16)
    xl, pos_c, nb = _dispatch(h2.reshape(n, D_MODEL), route, upper, lower)
    nb = nb[:, :, 0]
    expert_plan, combine_plan = _tile_plan(nb, nch)
    yt = _grouped_mlp(expert_plan, xl.reshape(-1, ROW_BLOCK, XL_COLS), w_up_bf, w_down_bf,
                      _max_tiles(nch))
    out = _combine(*combine_plan, yt, pos_c, x1.reshape(n, D_MODEL), mod3, final_g[None, :], seq)
    return out.reshape(bsz, seq, D_MODEL)
```

```python
import numpy as np
import jax
import jax.numpy as jnp
from jax import lax
from jax.experimental import pallas as pl
from jax.experimental.pallas import tpu as pltpu

F32 = jnp.float32
BF16 = jnp.bfloat16
I32 = jnp.int32

D_MODEL = 1024
GRID_W = 64
CONV_W = 512
N_HEADS = 8
N_KV_HEADS = 2
HEAD_DIM = 64
ATT_W = N_HEADS * HEAD_DIM
KV_W = N_KV_HEADS * HEAD_DIM
BLOCK = 128
ROPE_BASE = 10000.0
N_GROUPS = 4
EXPERTS_PER_GROUP = 8
N_EXPERTS = N_GROUPS * EXPERTS_PER_GROUP
EXPERT_FF = 256
N_MOD = 6
NORM_EPS = 1e-6
NEG_INF = -1e30
LOG2E = 1.4426950408889634
ONES_ROWS = 16

OFF_B, OFF_CG, OFF_XIN, OFF_Q, OFF_K, OFF_V, OFF_GA, OFF_GB = (
    0, 512, 1024, 1536, 2048, 2176, 2304, 3328)
IN_COLS = 4352

LANES = 128
SEQ_TILE = 512
BLOCKS_PER_TILE = SEQ_TILE // BLOCK
ROUTE_ROWS = 8
GROUP_ROWS = 8
ROUTER_ROWS = 48
VMEM_LIMIT = 56 * 1024 * 1024

CHUNK = SEQ_TILE
ROW_BLOCK = 16
CHUNK_ROWS = -(-(2 * CHUNK + N_EXPERTS * (ROW_BLOCK - 1)) // 256) * 256
CHUNK_BLOCKS = CHUNK_ROWS // ROW_BLOCK
GATE_COLS = LANES
XL_COLS = D_MODEL + GATE_COLS
COMMON_ROWS = 2 * CHUNK + 256
COMMON_BLOCKS = COMMON_ROWS // ROW_BLOCK
TILE_BLOCKS = 32
TILE_ROWS = TILE_BLOCKS * ROW_BLOCK
X_SLOTS = 6
Y_SLOTS = 4
W_SLOTS = 3
Y_RING = 3


def _cparams(sem):
    return pltpu.CompilerParams(dimension_semantics=sem, vmem_limit_bytes=VMEM_LIMIT)


def _rms_mod(x, g, shift, scale):
    inv = lax.rsqrt(jnp.mean(x * x, axis=-1, keepdims=True) + NORM_EPS)
    return (x * inv) * (g * (1.0 + scale)) + shift


def _mod_kernel(c_ref, w_ref, b_ref, o_ref):
    c = c_ref[...]
    a = (c * jax.nn.sigmoid(c)).astype(BF16)
    o_ref[...] = jnp.dot(a, w_ref[...].astype(BF16), preferred_element_type=F32) + b_ref[...]


def _modulation(cc, w_ada, b_ada):
    rows = cc.shape[0]
    cols = w_ada.shape[1]
    tile = 1024
    return pl.pallas_call(
        _mod_kernel,
        grid=(cols // tile,),
        in_specs=[pl.BlockSpec((rows, D_MODEL), lambda j: (0, 0)),
                  pl.BlockSpec((D_MODEL, tile), lambda j: (0, j)),
                  pl.BlockSpec((1, tile), lambda j: (0, j))],
        out_specs=pl.BlockSpec((rows, tile), lambda j: (0, j)),
        out_shape=jax.ShapeDtypeStruct((rows, cols), F32),
        compiler_params=_cparams(("arbitrary",)),
        name="adaln_mod",
    )(cc, w_ada, b_ada)


def _ctx_kernel(x_ref, g_ref, sh_ref, sc_ref, w_ref, k_ref, vt_ref):
    h = _rms_mod(x_ref[0], g_ref[...], sh_ref[...], sc_ref[...]).astype(BF16)
    kv = jnp.dot(h, w_ref[...], preferred_element_type=F32)
    k_ref[0] = kv[:, :KV_W].astype(BF16)
    vt_ref[0] = kv[:, KV_W:].T.astype(BF16)


def _context_kv(ctx, norm_g, csh, csc, w_kv):
    bsz, clen, _ = ctx.shape
    return pl.pallas_call(
        _ctx_kernel,
        grid=(bsz,),
        in_specs=[pl.BlockSpec((1, clen, D_MODEL), lambda b: (b, 0, 0)),
                  pl.BlockSpec((1, D_MODEL), lambda b: (0, 0)),
                  pl.BlockSpec((1, D_MODEL), lambda b: (0, 0)),
                  pl.BlockSpec((1, D_MODEL), lambda b: (0, 0)),
                  pl.BlockSpec((D_MODEL, 2 * KV_W), lambda b: (0, 0))],
        out_specs=[pl.BlockSpec((1, clen, KV_W), lambda b: (b, 0, 0)),
                   pl.BlockSpec((1, KV_W, clen), lambda b: (b, 0, 0))],
        out_shape=[jax.ShapeDtypeStruct((bsz, clen, KV_W), BF16),
                   jax.ShapeDtypeStruct((bsz, KV_W, clen), BF16)],
        compiler_params=_cparams(("arbitrary",)),
        name="context_kv",
    )(ctx, norm_g, csh, csc, w_kv)


def _rope(t, cos, sin_signed):
    lane = lax.broadcasted_iota(I32, (1, LANES), 1)
    first_half = (lane % HEAD_DIM) < (HEAD_DIM // 2)
    outs = []
    for j in range(t.shape[1] // LANES):
        tj = t[:, j * LANES:(j + 1) * LANES]
        partner = jnp.where(first_half,
                            pltpu.roll(tj, LANES - HEAD_DIM // 2, axis=1),
                            pltpu.roll(tj, HEAD_DIM // 2, axis=1))
        outs.append(tj * cos + partner * sin_signed)
    return outs[0] if len(outs) == 1 else jnp.concatenate(outs, axis=1)


def _proj_a_kernel(x_ref, g_ref, sh_ref, sc_ref, w_ref, cos_ref, sin_ref, uc_ref, k_ref, vt_ref):
    h = _rms_mod(x_ref[0], g_ref[...], sh_ref[0], sc_ref[0]).astype(BF16)
    u = jnp.dot(h, w_ref[...], preferred_element_type=F32)
    uc_ref[0] = u[:, :CONV_W] * u[:, CONV_W:2 * CONV_W]
    k = u[:, 2 * CONV_W:2 * CONV_W + KV_W]
    k_ref[0] = _rope(k, cos_ref[...], sin_ref[...]).astype(BF16)
    vt_ref[0] = u[:, 2 * CONV_W + KV_W:].T.astype(BF16)


def _proj_a(x, norm_g, mod3, w_a_cols, cos_t, sin_t):
    bsz, seq, _ = x.shape
    nt = seq // SEQ_TILE
    wcols = w_a_cols.shape[1]
    return pl.pallas_call(
        _proj_a_kernel,
        grid=(nt, bsz),
        in_specs=[pl.BlockSpec((1, SEQ_TILE, D_MODEL), lambda t, b: (b, t, 0)),
                  pl.BlockSpec((1, D_MODEL), lambda t, b: (0, 0)),
                  pl.BlockSpec((1, 1, D_MODEL), lambda t, b: (b, 0, 0)),
                  pl.BlockSpec((1, 1, D_MODEL), lambda t, b: (b, 0, 1)),
                  pl.BlockSpec((D_MODEL, wcols), lambda t, b: (0, 0)),
                  pl.BlockSpec((SEQ_TILE, LANES), lambda t, b: (t, 0)),
                  pl.BlockSpec((SEQ_TILE, LANES), lambda t, b: (t, 0))],
        out_specs=[pl.BlockSpec((1, SEQ_TILE, CONV_W), lambda t, b: (b, t, 0)),
                   pl.BlockSpec((1, SEQ_TILE, KV_W), lambda t, b: (b, t, 0)),
                   pl.BlockSpec((1, KV_W, SEQ_TILE), lambda t, b: (b, 0, t))],
        out_shape=[jax.ShapeDtypeStruct((bsz, seq, CONV_W), F32),
                   jax.ShapeDtypeStruct((bsz, seq, KV_W), BF16),
                   jax.ShapeDtypeStruct((bsz, KV_W, seq), BF16)],
        compiler_params=_cparams(("arbitrary", "arbitrary")),
        name="proj_a",
    )(x, norm_g, mod3, mod3, w_a_cols, cos_t, sin_t)


def _sigmoid(x):
    return 1.0 / (1.0 + jnp.exp2(x * (-LOG2E)))


def _attn_scores(q_t, g, k_win, k_ctx, bias_p4, bias_n4):
    zeros = jnp.zeros((HEAD_DIM, 4 * BLOCK), BF16)
    qg = jnp.concatenate([q_t[(4 * g + h) * HEAD_DIM:(4 * g + h + 1) * HEAD_DIM, :]
                          for h in range(4)], axis=1)
    rhs = jnp.concatenate([qg, zeros] if g == 0 else [zeros, qg], axis=0)
    s_win = jnp.dot(k_win, rhs, preferred_element_type=F32)
    s_ctx = jnp.dot(k_ctx, rhs, preferred_element_type=F32)
    return (s_win[0:BLOCK] + bias_p4, s_win[BLOCK:2 * BLOCK], s_win[2 * BLOCK:] + bias_n4, s_ctx)


def _attn_probs(scores, sink):
    m = sink
    for s in scores:
        m = jnp.maximum(m, jnp.max(s, axis=0, keepdims=True))
    p_win = jnp.concatenate([jnp.exp2(s - m).astype(BF16) for s in scores[:3]], axis=0)
    p_ctx = jnp.exp2(scores[3] - m).astype(BF16)
    return p_win, p_ctx, m


def _attn_values(probs, sink, vt_win_g, vt_ctx_g):
    p_win, p_ctx, m = probs

    def with_ones(vt):
        r = lax.broadcasted_iota(I32, (ONES_ROWS, vt.shape[1]), 0)
        return jnp.concatenate([vt, jnp.where(r == 0, 1.0, 0.0).astype(BF16)], axis=0)

    o_ext = (jnp.dot(with_ones(vt_win_g), p_win, preferred_element_type=F32)
             + jnp.dot(with_ones(vt_ctx_g), p_ctx, preferred_element_type=F32))
    denom = o_ext[HEAD_DIM:HEAD_DIM + 1, :] + jnp.exp2(sink - m)
    return o_ext[:HEAD_DIM, :] / denom


def _route(logits_t):
    t = logits_t.shape[1]
    grow = lax.broadcasted_iota(I32, (GROUP_ROWS, t), 0)
    gl = jnp.where(grow < N_GROUPS, logits_t[0:GROUP_ROWS, :], NEG_INF)
    gm = jnp.max(gl, axis=0, keepdims=True)
    p_g = 1.0 / jnp.sum(jnp.exp(gl - gm), axis=0, keepdims=True)
    g_idx = jnp.min(jnp.where(gl == gm, grow, N_GROUPS), axis=0, keepdims=True)

    erow = lax.broadcasted_iota(I32, (N_EXPERTS, t), 0)
    el = logits_t[GROUP_ROWS:GROUP_ROWS + N_EXPERTS, :]
    sel = (erow // EXPERTS_PER_GROUP) == g_idx
    em = jnp.where(sel, el, NEG_INF)
    m1 = jnp.max(em, axis=0, keepdims=True)
    i1 = jnp.min(jnp.where(em == m1, erow, N_EXPERTS), axis=0, keepdims=True)
    em2 = jnp.where(erow == i1, NEG_INF, em)
    m2 = jnp.max(em2, axis=0, keepdims=True)
    i2 = jnp.min(jnp.where(em2 == m2, erow, N_EXPERTS), axis=0, keepdims=True)
    z = jnp.sum(jnp.where(sel, jnp.exp(el - m1), 0.0), axis=0, keepdims=True)
    p1 = 1.0 / z
    p2 = jnp.exp(m2 - m1) / z
    gate1 = p_g * p1 / (p1 + p2)
    gate2 = p_g * p2 / (p1 + p2)
    pad = jnp.zeros((ROUTE_ROWS - 4, t), F32)
    return jnp.concatenate([i1.astype(F32), i2.astype(F32), gate1, gate2, pad], axis=0)


def _mix_kernel(x_ref, g1n_ref, g2n_ref, sh1_ref, sc1_ref, gt1_ref, sh2_ref, sc2_ref,
                wb_ref, cos_ref, sin_ref,
                uc_ref, ucp_ref, ucn_ref, k_ref, kp_ref, kn_ref, vt_ref, vtp_ref, vtn_ref,
                kc_ref, vtc_ref, wconv_ref, bconv_ref, wa_ref, wbb_ref, wo_ref,
                sink_ref, bias_ref, wr_ref, br_ref, wu_ref, wd_ref,
                x1_ref, h2_ref, route_ref, wub_ref, wdb_ref):
    t_idx = pl.program_id(0)
    n_tiles = pl.num_programs(0)
    is_first = t_idx == 0
    is_last = t_idx == n_tiles - 1

    x = x_ref[0]

    uc = uc_ref[0]
    row = lax.broadcasted_iota(I32, (SEQ_TILE, 1), 0)
    prev_row = jnp.where(is_first, 0.0, ucp_ref[0][7:8, :])
    next_row = jnp.where(is_last, 0.0, ucn_ref[0][0:1, :])
    up = jnp.where(row == 0, prev_row, pltpu.roll(uc, 1, axis=0))
    dn = jnp.where(row == SEQ_TILE - 1, next_row, pltpu.roll(uc, SEQ_TILE - 1, axis=0))
    wconv = wconv_ref[...]
    y = bconv_ref[...] + (up * wconv[0:1, :] + uc * wconv[1:2, :] + dn * wconv[2:3, :])

    hb = _rms_mod(x, g1n_ref[...], sh1_ref[0], sc1_ref[0]).astype(BF16)
    bq = jnp.dot(hb, wb_ref[:, 0:2 * CONV_W], preferred_element_type=F32)
    ya = jnp.dot((bq[:, :CONV_W] * y).astype(BF16), wa_ref[...], preferred_element_type=F32)
    q = _rope(bq[:, CONV_W:], cos_ref[...], sin_ref[...]) * (HEAD_DIM ** -0.5 * LOG2E)

    k_all = jnp.concatenate([kp_ref[0], k_ref[0], kn_ref[0]], axis=0)
    vt_all = jnp.concatenate([vtp_ref[0], vt_ref[0], vtn_ref[0]], axis=1)
    k_ctx = kc_ref[0]
    vt_ctx = vtc_ref[0]
    bias_prev = bias_ref[0:BLOCK, :]
    bias_next = bias_ref[2 * BLOCK:3 * BLOCK, :]
    sink_row = sink_ref[...] * LOG2E
    n_units = BLOCKS_PER_TILE * N_KV_HEADS
    gate_cols = 2 * D_MODEL // n_units
    gate_chunks = []

    def gate_chunk(u):
        c0 = 2 * CONV_W + u * gate_cols
        gate_chunks.append(jnp.dot(hb, wb_ref[:, c0:c0 + gate_cols], preferred_element_type=F32))

    q_ts, biases = [], []
    for jb in range(BLOCKS_PER_TILE):
        bias_p, bias_n = bias_prev, bias_next
        if jb == 0:
            bias_p = bias_prev + jnp.where(is_first, NEG_INF, 0.0)
        if jb == BLOCKS_PER_TILE - 1:
            bias_n = bias_next + jnp.where(is_last, NEG_INF, 0.0)
        biases.append((jnp.concatenate([bias_p] * 4, axis=1), jnp.concatenate([bias_n] * 4, axis=1)))
        q_ts.append(q[jb * BLOCK:(jb + 1) * BLOCK, :].T.astype(BF16))

    scores, probs, outs = {}, {}, {}
    for t in range(n_units + 2):
        if t == 3:
            wub_ref[...] = wu_ref[...].astype(BF16)
        if t == 5:
            wdb_ref[...] = wd_ref[...].astype(BF16)
        if t < n_units:
            jb, g = divmod(t, N_KV_HEADS)
            scores[t] = _attn_scores(q_ts[jb], g, k_all[jb * BLOCK:(jb + 3) * BLOCK, :], k_ctx,
                                     *biases[jb])
            gate_chunk(t)
        if 0 <= t - 1 < n_units:
            g = (t - 1) % N_KV_HEADS
            probs[t - 1] = _attn_probs(scores.pop(t - 1),
                                       sink_row[:, g * 4 * BLOCK:(g + 1) * 4 * BLOCK])
        if 0 <= t - 2 < n_units:
            jb, g = divmod(t - 2, N_KV_HEADS)
            outs[t - 2] = _attn_values(
                probs.pop(t - 2), sink_row[:, g * 4 * BLOCK:(g + 1) * 4 * BLOCK],
                vt_all[g * HEAD_DIM:(g + 1) * HEAD_DIM, jb * BLOCK:(jb + 3) * BLOCK],
                vt_ctx[g * HEAD_DIM:(g + 1) * HEAD_DIM, :])
    o_blocks = []
    for jb in range(BLOCKS_PER_TILE):
        o_rows = [outs[jb * N_KV_HEADS + g][:, h * BLOCK:(h + 1) * BLOCK]
                  for g in range(N_KV_HEADS) for h in range(4)]
        o_blocks.append(jnp.concatenate(o_rows, axis=0).T)
    gates = jnp.concatenate(gate_chunks, axis=1)

    o = jnp.concatenate(o_blocks, axis=0).astype(BF16)
    yb = jnp.dot(o, wbb_ref[...], preferred_element_type=F32)
    merged = (_sigmoid(gates[:, :D_MODEL]) * ya + _sigmoid(gates[:, D_MODEL:]) * yb).astype(BF16)
    x1 = x + gt1_ref[0] * jnp.dot(merged, wo_ref[...], preferred_element_type=F32)
    x1_ref[0] = x1
    h2 = _rms_mod(x1, g2n_ref[...], sh2_ref[0], sc2_ref[0]).astype(BF16)
    h2_ref[0] = h2
    logits_t = lax.dot_general(wr_ref[...], h2, (((1,), (1,)), ((), ())),
                               preferred_element_type=F32) + br_ref[...]
    route_ref[0] = _route(logits_t)


def _mix(x, norm1_g, norm2_g, mod3, w_b_cols, cos_t, sin_t, uc, k_rot, v_t, k_ctx, vt_ctx,
         w_conv, b_conv, w_a, w_b, w_o, sink_row, bias, wr_t, br_col, w_up, w_down):
    bsz, seq, _ = x.shape
    nt = seq // SEQ_TILE
    nblk = seq // BLOCK
    clen = k_ctx.shape[1]
    rows8 = SEQ_TILE // 8
    assert N_EXPERTS % (nt * bsz) == 0, "each grid step converts an equal share of the experts"
    e_step = N_EXPERTS // (nt * bsz)

    def mod_spec(j):
        return pl.BlockSpec((1, 1, D_MODEL), lambda t, b, j=j: (b, 0, j))

    def const_spec(shape):
        return pl.BlockSpec(shape, lambda t, b: tuple(0 for _ in shape))

    in_specs = [
        pl.BlockSpec((1, SEQ_TILE, D_MODEL), lambda t, b: (b, t, 0)),
        const_spec((1, D_MODEL)), const_spec((1, D_MODEL)),
        mod_spec(0), mod_spec(1), mod_spec(2), mod_spec(3), mod_spec(4),
        const_spec(w_b_cols.shape),
        pl.BlockSpec((SEQ_TILE, LANES), lambda t, b: (t, 0)),
        pl.BlockSpec((SEQ_TILE, LANES), lambda t, b: (t, 0)),
        pl.BlockSpec((1, SEQ_TILE, CONV_W), lambda t, b: (b, t, 0)),
        pl.BlockSpec((1, 8, CONV_W), lambda t, b: (b, jnp.maximum(t * rows8 - 1, 0), 0)),
        pl.BlockSpec((1, 8, CONV_W),
                     lambda t, b: (b, jnp.minimum((t + 1) * rows8, seq // 8 - 1), 0)),
        pl.BlockSpec((1, SEQ_TILE, KV_W), lambda t, b: (b, t, 0)),
        pl.BlockSpec((1, BLOCK, KV_W),
                     lambda t, b: (b, jnp.maximum(t * BLOCKS_PER_TILE - 1, 0), 0)),
        pl.BlockSpec((1, BLOCK, KV_W),
                     lambda t, b: (b, jnp.minimum((t + 1) * BLOCKS_PER_TILE, nblk - 1), 0)),
        pl.BlockSpec((1, KV_W, SEQ_TILE), lambda t, b: (b, 0, t)),
        pl.BlockSpec((1, KV_W, BLOCK),
                     lambda t, b: (b, 0, jnp.maximum(t * BLOCKS_PER_TILE - 1, 0))),
        pl.BlockSpec((1, KV_W, BLOCK),
                     lambda t, b: (b, 0, jnp.minimum((t + 1) * BLOCKS_PER_TILE, nblk - 1))),
        pl.BlockSpec((1, clen, KV_W), lambda t, b: (b, 0, 0)),
        pl.BlockSpec((1, KV_W, clen), lambda t, b: (b, 0, 0)),
        const_spec(w_conv.shape), const_spec(b_conv.shape),
        const_spec(w_a.shape), const_spec(w_b.shape), const_spec(w_o.shape),
        const_spec(sink_row.shape), const_spec(bias.shape),
        const_spec(wr_t.shape), const_spec(br_col.shape),
        pl.BlockSpec((e_step,) + w_up.shape[1:], lambda t, b: (t * bsz + b, 0, 0)),
        pl.BlockSpec((e_step,) + w_down.shape[1:], lambda t, b: (t * bsz + b, 0, 0)),
    ]
    out_specs = [
        pl.BlockSpec((1, SEQ_TILE, D_MODEL), lambda t, b: (b, t, 0)),
        pl.BlockSpec((1, SEQ_TILE, D_MODEL), lambda t, b: (b, t, 0)),
        pl.BlockSpec((1, ROUTE_ROWS, SEQ_TILE), lambda t, b: (b, 0, t)),
        pl.BlockSpec((e_step,) + w_up.shape[1:], lambda t, b: (t * bsz + b, 0, 0)),
        pl.BlockSpec((e_step,) + w_down.shape[1:], lambda t, b: (t * bsz + b, 0, 0)),
    ]
    out_shape = [
        jax.ShapeDtypeStruct((bsz, seq, D_MODEL), F32),
        jax.ShapeDtypeStruct((bsz, seq, D_MODEL), BF16),
        jax.ShapeDtypeStruct((bsz, ROUTE_ROWS, seq), F32),
        jax.ShapeDtypeStruct(w_up.shape, BF16),
        jax.ShapeDtypeStruct(w_down.shape, BF16),
    ]
    return pl.pallas_call(
        _mix_kernel,
        grid=(nt, bsz),
        in_specs=in_specs,
        out_specs=out_specs,
        out_shape=out_shape,
        compiler_params=_cparams(("arbitrary", "arbitrary")),
        name="token_mix",
    )(x, norm1_g, norm2_g, mod3, mod3, mod3, mod3, mod3, w_b_cols, cos_t, sin_t,
      uc, uc, uc, k_rot, k_rot, k_rot, v_t, v_t, v_t, k_ctx, vt_ctx,
      w_conv, b_conv, w_a, w_b, w_o, sink_row, bias, wr_t, br_col, w_up, w_down)


def _bf16_parts(v):
    hi = v.astype(BF16).astype(F32)
    r1 = v - hi
    mid = r1.astype(BF16).astype(F32)
    lo = (r1 - mid).astype(BF16).astype(F32)
    return hi, mid, lo


def _dispatch_kernel(h_ref, route_ref, upper_ref, lower_ref, xl_ref, pos_ref, nb_ref):
    route = route_ref[0]
    e1 = route[0:1, :].astype(I32)
    e2 = route[1:2, :].astype(I32)
    erow = lax.broadcasted_iota(I32, (N_EXPERTS, CHUNK), 0)
    hit1 = erow == e1
    hit2 = erow == e2
    onehot = jnp.where(hit1, 1.0, 0.0) + jnp.where(hit2, 1.0, 0.0)
    cum = jnp.dot(onehot.astype(BF16), upper_ref[...], preferred_element_type=F32)
    cnt = jnp.sum(onehot, axis=1, keepdims=True)
    nblk = jnp.floor((cnt + (ROW_BLOCK - 1)) * (1.0 / ROW_BLOCK))
    nblk_b = jnp.broadcast_to(nblk, (N_EXPERTS, LANES))
    seg = jnp.dot(lower_ref[...], nblk_b.astype(BF16), preferred_element_type=F32) * ROW_BLOCK
    base = seg[:, 0:1] + cum
    pos1 = jnp.sum(jnp.where(hit1, base, 0.0), axis=0, keepdims=True)
    pos2 = jnp.sum(jnp.where(hit2, base, 0.0), axis=0, keepdims=True)
    p1i = pos1.astype(I32)
    p2i = pos2.astype(I32)
    used_rows = (jnp.sum(nblk) * ROW_BLOCK).astype(I32)

    prow = lax.broadcasted_iota(I32, (LANES, CHUNK), 0)
    parts = _bf16_parts(route[2:3, :]) + _bf16_parts(route[3:4, :])
    gpart_rows = jnp.zeros((LANES, CHUNK), F32)
    for j, part in enumerate(parts):
        gpart_rows = jnp.where(prow == j, part, gpart_rows)
    h_ext = jnp.concatenate([h_ref[...], gpart_rows.T.astype(BF16)], axis=1)
    glane = lax.broadcasted_iota(I32, (1, GATE_COLS), 1)
    one = jnp.ones((), BF16)
    zero = jnp.zeros((), BF16)
    def permute(row0, n_rows):
        r16 = lax.broadcasted_iota(I32, (n_rows, CHUNK), 0).astype(jnp.int16)
        q1 = (p1i - row0).astype(jnp.int16)
        q2 = (p2i - row0).astype(jnp.int16)
        sel1 = jnp.where(r16 == q1, one, zero)
        sel = jnp.where(r16 == q2, one, sel1)
        is_slot1 = jnp.max(sel1, axis=1, keepdims=True)
        rows = pl.ds(row0, n_rows)
        xg = jnp.dot(sel, h_ext, preferred_element_type=F32)
        xl_ref[rows, 0:D_MODEL] = xg[:, :D_MODEL].astype(BF16)
        g6 = xg[:, D_MODEL:]
        g3 = jnp.where(is_slot1.astype(F32) > 0.0, g6, pltpu.roll(g6, GATE_COLS - 3, axis=1))
        xl_ref[rows, D_MODEL:XL_COLS] = jnp.where(glane < 3, g3, 0.0).astype(BF16)

    permute(0, COMMON_ROWS)
    tail_rows = CHUNK_ROWS - COMMON_ROWS

    @pl.when(COMMON_ROWS < used_rows)
    def _():
        permute(COMMON_ROWS, tail_rows)

    @pl.when(COMMON_ROWS >= used_rows)
    def _():
        xl_ref[pl.ds(COMMON_ROWS, tail_rows), :] = jnp.zeros((tail_rows, XL_COLS), BF16)

    pos_rows = jnp.where(prow == 0, pos1, jnp.where(prow == 1, pos2, 0.0))
    pos_ref[...] = pos_rows.T
    nb_ref[0] = nblk_b.astype(I32)


def _dispatch(h2, route, upper, lower):
    n = h2.shape[0]
    nch = n // CHUNK
    per_seq = route.shape[2] // CHUNK
    return pl.pallas_call(
        _dispatch_kernel,
        grid=(nch,),
        in_specs=[pl.BlockSpec((CHUNK, D_MODEL), lambda c: (c, 0)),
                  pl.BlockSpec((1, ROUTE_ROWS, CHUNK), lambda c: (c // per_seq, 0, c % per_seq)),
                  pl.BlockSpec(upper.shape, lambda c: (0, 0)),
                  pl.BlockSpec(lower.shape, lambda c: (0, 0))],
        out_specs=[pl.BlockSpec((CHUNK_ROWS, XL_COLS), lambda c: (c, 0)),
                   pl.BlockSpec((CHUNK, LANES), lambda c: (c, 0)),
                   pl.BlockSpec((1, N_EXPERTS, LANES), lambda c: (c, 0, 0))],
        out_shape=[jax.ShapeDtypeStruct((nch * CHUNK_ROWS, XL_COLS), BF16),
                   jax.ShapeDtypeStruct((n, LANES), F32),
                   jax.ShapeDtypeStruct((nch, N_EXPERTS, LANES), I32)],
        compiler_params=_cparams(("arbitrary",)),
        name="moe_dispatch",
    )(h2, route, upper, lower)


def _max_tiles(nch):
    max_blocks = nch * (2 * CHUNK + N_EXPERTS * (ROW_BLOCK - 1)) // ROW_BLOCK
    return max_blocks // TILE_BLOCKS + N_EXPERTS


def _masked_prefix(le, values):
    delta = values - jnp.concatenate([jnp.zeros((1,), values.dtype), values[:-1]])
    return jnp.sum(jnp.where(le, delta[None, :], 0), axis=1)


def _tile_plan(nb, nch):
    n_tiles = _max_tiles(nch)
    nbt = nb.T
    nbe = jnp.sum(nbt, axis=1)
    nte = (nbe + TILE_BLOCKS - 1) // TILE_BLOCKS
    tile_end = jnp.cumsum(nte)
    tile_start = tile_end - nte
    n_act = tile_end[-1]
    tiles = jnp.arange(n_tiles, dtype=I32)
    te = jnp.sum((tile_end[None, :] <= tiles[:, None]).astype(I32), axis=1)
    e_ar = jnp.arange(N_EXPERTS, dtype=I32)
    active = nte > 0
    te = jnp.where(tiles < n_act, te, jnp.max(jnp.where(active, e_ar, 0)))
    first = jnp.logical_and(te != jnp.concatenate([jnp.full((1,), -1, I32), te[:-1]]),
                            tiles < n_act).astype(I32)
    rank = jnp.cumsum(active.astype(I32)) - 1
    k_ar = jnp.arange(N_EXPERTS + W_SLOTS, dtype=I32)
    eseq = jnp.sum(jnp.where(jnp.logical_and(active[None, :], rank[None, :] == k_ar[:, None]),
                             e_ar[None, :], 0), axis=1)
    n_exp = jnp.sum(active.astype(I32))

    cb_excl = jnp.cumsum(nbt, axis=1) - nbt
    gs = TILE_BLOCKS * tile_start[:, None] + cb_excl
    seg_blk = jnp.cumsum(nb, axis=1) - nb
    base_blk = jnp.arange(nch, dtype=I32)[None, :] * CHUNK_BLOCKS + seg_blk.T
    gs_f = gs.reshape(-1)
    slots = jnp.arange(n_tiles * TILE_BLOCKS, dtype=I32)
    le = gs_f[None, :] <= slots[:, None]
    blk = slots + _masked_prefix(le, (base_blk - gs).reshape(-1))
    valid = slots < _masked_prefix(le, (gs + nbt).reshape(-1))
    gblk = jnp.concatenate([jnp.where(valid, blk, 0).astype(I32),
                            jnp.zeros(((X_SLOTS - 1) * TILE_BLOCKS,), I32)])

    m = jnp.arange(CHUNK_BLOCKS, dtype=I32)
    le_c = seg_blk[:, None, :] <= m[None, :, None]
    shift = (gs.T - seg_blk)
    delta = shift - jnp.concatenate([jnp.zeros((nch, 1), I32), shift[:, :-1]], axis=1)
    slot_of = m[None, :] + jnp.sum(jnp.where(le_c, delta[:, None, :], 0), axis=2)
    used_blocks = jnp.sum(nb, axis=1)
    slot_of = jnp.where(m[None, :] < used_blocks[:, None], slot_of, 0).astype(I32)
    pad = Y_RING - 1
    slot_of = jnp.concatenate([slot_of.reshape(-1), jnp.zeros((pad * CHUNK_BLOCKS,), I32)])
    used_rows = jnp.concatenate([ROW_BLOCK * used_blocks, jnp.zeros((pad,), I32)]).astype(I32)
    return ((n_act.reshape(1).astype(I32), n_exp.reshape(1), first, eseq, gblk),
            (used_rows, slot_of))


def _gmm_kernel(nact_ref, nexp_ref, first_ref, eseq_ref, gblk_ref,
                xl_hbm, wu_hbm, wd_hbm, yt_hbm,
                xbuf, ybuf, zbuf, wu_st, wd_st, in_sem, out_sem, w_sem, z_sem):
    n_act = nact_ref[0]
    n_exp = nexp_ref[0]
    n_tiles = yt_hbm.shape[0] // TILE_BLOCKS
    prefetch = X_SLOTS - 1

    def gather(tile, b):
        s = tile % X_SLOTS
        blk = gblk_ref[tile * TILE_BLOCKS + b]
        return pltpu.make_async_copy(xl_hbm.at[blk], xbuf.at[s, b], in_sem.at[s])

    def write_back(tile):
        s = tile % Y_SLOTS
        return pltpu.make_async_copy(ybuf.at[s], yt_hbm.at[pl.ds(tile * TILE_BLOCKS, TILE_BLOCKS)],
                                     out_sem.at[s])

    def zero_fill(tile):
        return pltpu.make_async_copy(zbuf, yt_hbm.at[pl.ds(tile * TILE_BLOCKS, TILE_BLOCKS)], z_sem)

    def weights(q):
        e = eseq_ref[q]
        s = q % W_SLOTS
        return (pltpu.make_async_copy(wu_hbm.at[e], wu_st.at[s], w_sem.at[0, s]),
                pltpu.make_async_copy(wd_hbm.at[e], wd_st.at[s], w_sem.at[1, s]))

    def tile_blocks(fn):
        for b in range(TILE_BLOCKS):
            fn(b)

    for q0 in range(W_SLOTS - 1):
        @pl.when(q0 < n_exp)
        def _(q0=q0):
            for cp in weights(q0):
                cp.start()
    for t0 in range(prefetch):
        tile_blocks(lambda b, t0=t0: gather(t0, b).start())

    def body(i, q):
        tile_blocks(lambda b: gather(i, b).wait())

        @pl.when(i >= Y_SLOTS)
        def _():
            write_back(i - Y_SLOTS).wait()

        is_first = first_ref[i] == 1

        @pl.when(is_first)
        def _():
            for cp in weights(q):
                cp.wait()

            @pl.when(q + W_SLOTS - 1 < n_exp)
            def _():
                for cp in weights(q + W_SLOTS - 1):
                    cp.start()

        q = q + is_first.astype(I32)
        ws = (q - 1) % W_SLOTS
        xs = i % X_SLOTS
        ys = i % Y_SLOTS

        x = xbuf[xs].reshape(TILE_ROWS, XL_COLS)
        gate = jnp.sum(x[:, D_MODEL:].astype(F32), axis=1, keepdims=True)
        au = jnp.dot(x[:, :D_MODEL], wu_st[ws], preferred_element_type=F32)
        tile_blocks(lambda b: gather(i + prefetch, b).start())
        a = au[:, :EXPERT_FF]
        act = (a * _sigmoid(a)) * au[:, EXPERT_FF:]
        y = jnp.dot(act.astype(BF16), wd_st[ws], preferred_element_type=F32)
        ybuf[ys] = (gate * y).astype(BF16).reshape(TILE_BLOCKS, ROW_BLOCK, D_MODEL)
        write_back(i).start()

        @pl.when(n_act + i < n_tiles)
        def _():
            zero_fill(n_act + i).start()
        return q

    zbuf[...] = jnp.zeros_like(zbuf)
    lax.fori_loop(0, n_act, body, jnp.int32(0))

    def fill_rest(t, carry):
        zero_fill(t).start()
        return carry

    lax.fori_loop(jnp.minimum(2 * n_act, n_tiles), n_tiles, fill_rest, 0)

    def drain(t, carry):
        zero_fill(t).wait()
        return carry

    lax.fori_loop(n_act, n_tiles, drain, 0)
    for k in range(prefetch):
        tile_blocks(lambda b, k=k: gather(n_act + k, b).wait())
    for k in range(Y_SLOTS):
        @pl.when(n_act - 1 - k >= 0)
        def _(k=k):
            write_back(n_act - 1 - k).wait()


def _grouped_mlp(plan, xl, w_up, w_down, n_tiles):
    grid_spec = pltpu.PrefetchScalarGridSpec(
        num_scalar_prefetch=len(plan),
        grid=(1,),
        in_specs=[pl.BlockSpec(memory_space=pl.ANY)] * 3,
        out_specs=pl.BlockSpec(memory_space=pl.ANY),
        scratch_shapes=[pltpu.VMEM((X_SLOTS, TILE_BLOCKS, ROW_BLOCK, XL_COLS), BF16),
                        pltpu.VMEM((Y_SLOTS, TILE_BLOCKS, ROW_BLOCK, D_MODEL), BF16),
                        pltpu.VMEM((TILE_BLOCKS, ROW_BLOCK, D_MODEL), BF16),
                        pltpu.VMEM((W_SLOTS, D_MODEL, 2 * EXPERT_FF), BF16),
                        pltpu.VMEM((W_SLOTS, EXPERT_FF, D_MODEL), BF16),
                        pltpu.SemaphoreType.DMA((X_SLOTS,)),
                        pltpu.SemaphoreType.DMA((Y_SLOTS,)),
                        pltpu.SemaphoreType.DMA((2, W_SLOTS)),
                        pltpu.SemaphoreType.DMA(())])
    return pl.pallas_call(
        _gmm_kernel,
        grid_spec=grid_spec,
        out_shape=jax.ShapeDtypeStruct((n_tiles * TILE_BLOCKS, ROW_BLOCK, D_MODEL), BF16),
        compiler_params=_cparams(("arbitrary",)),
        name="moe_experts",
    )(*plan, xl, w_up, w_down)


def _combine_kernel(used_ref, slot_ref, yt_hbm, pos_ref, x1_ref, gt2_ref, fg_ref, o_ref, ybuf, sem):
    c = pl.program_id(0)
    n_chunks = pl.num_programs(0)
    used_rows = used_ref[c]

    def chunk_blocks(chunk, op):
        s = chunk % Y_RING

        def copy(b):
            slot = slot_ref[chunk * CHUNK_BLOCKS + b]
            return pltpu.make_async_copy(yt_hbm.at[slot], ybuf.at[s, b], sem.at[s])

        for b in range(COMMON_BLOCKS):
            op(copy(b))

        @pl.when(used_ref[chunk] > COMMON_ROWS)
        def _():
            for b in range(COMMON_BLOCKS, CHUNK_BLOCKS):
                op(copy(b))

    @pl.when(c == 0)
    def _():
        for k in range(Y_RING - 1):
            chunk_blocks(k, lambda cp: cp.start())

    chunk_blocks(c + Y_RING - 1, lambda cp: cp.start())
    chunk_blocks(c, lambda cp: cp.wait())

    def body(k_rows):
        pos = pos_ref[...]
        p1 = pos[:, 0:1].astype(I32)
        p2 = pos[:, 1:2].astype(I32)
        r = lax.broadcasted_iota(I32, (CHUNK, k_rows), 1)
        sel = jnp.where(r == p1, 1.0, jnp.where(r == p2, 1.0, 0.0)).astype(BF16)
        yl = ybuf[c % Y_RING, 0:k_rows // ROW_BLOCK].reshape(k_rows, D_MODEL)
        y = jnp.dot(sel, yl, preferred_element_type=F32)
        x2 = x1_ref[...] + gt2_ref[0] * y
        inv = lax.rsqrt(jnp.mean(x2 * x2, axis=-1, keepdims=True) + NORM_EPS)
        o_ref[...] = (x2 * inv) * fg_ref[...]

    pl.when(used_rows <= COMMON_ROWS)(lambda: body(COMMON_ROWS))
    pl.when(used_rows > COMMON_ROWS)(lambda: body(CHUNK_ROWS))

    @pl.when(c == n_chunks - 1)
    def _():
        for k in range(1, Y_RING):
            chunk_blocks(c + k, lambda cp: cp.wait())


def _combine(used_rows, slot_of, yt, pos_c, x1, mod3, final_g, seq):
    n = x1.shape[0]
    per_seq = seq // CHUNK
    grid_spec = pltpu.PrefetchScalarGridSpec(
        num_scalar_prefetch=2,
        grid=(n // CHUNK,),
        in_specs=[pl.BlockSpec(memory_space=pl.ANY),
                  pl.BlockSpec((CHUNK, LANES), lambda c, u, s: (c, 0)),
                  pl.BlockSpec((CHUNK, D_MODEL), lambda c, u, s: (c, 0)),
                  pl.BlockSpec((1, 1, D_MODEL), lambda c, u, s: (c // per_seq, 0, 5)),
                  pl.BlockSpec((1, D_MODEL), lambda c, u, s: (0, 0))],
        out_specs=pl.BlockSpec((CHUNK, D_MODEL), lambda c, u, s: (c, 0)),
        scratch_shapes=[pltpu.VMEM((Y_RING, CHUNK_BLOCKS, ROW_BLOCK, D_MODEL), BF16),
                        pltpu.SemaphoreType.DMA((Y_RING,))])
    return pl.pallas_call(
        _combine_kernel,
        grid_spec=grid_spec,
        out_shape=jax.ShapeDtypeStruct((n, D_MODEL), F32),
        compiler_params=_cparams(("arbitrary",)),
        name="moe_combine",
    )(used_rows, slot_of, yt, pos_c, x1, mod3, final_g)


def _rope_tables(seq):
    n_freq = HEAD_DIM // 4
    inv_freq = ROPE_BASE ** (-jnp.arange(n_freq, dtype=F32) / n_freq)
    rows = seq // GRID_W
    row = jnp.repeat(jnp.arange(rows, dtype=F32), GRID_W)
    col = jnp.tile(jnp.arange(GRID_W, dtype=F32), rows)
    ang = jnp.concatenate([row[:, None] * inv_freq, col[:, None] * inv_freq], axis=-1)
    cos, sin = jnp.cos(ang), jnp.sin(ang)
    return jnp.tile(cos, (1, 4)), jnp.concatenate([-sin, sin, -sin, sin], axis=1)


def _window_bias():
    key = np.arange(3 * BLOCK)[:, None]
    qry = np.arange(BLOCK)[None, :]
    valid = (key - qry >= 0) & (key - qry <= 2 * BLOCK)
    return jnp.asarray(np.where(valid, 0.0, NEG_INF), F32)


def kernel(x, c, ctx, c_ctx, w_ada, b_ada, norm1_g, w_in, w_conv, b_conv, w_a, w_b, sink, w_o,
           norm2_g, w_group, b_group, w_router, b_router, w_up, w_down, final_g):
    bsz, seq, _ = x.shape
    assert w_ada.shape[0] == 1 and seq % SEQ_TILE == 0

    cc = jnp.zeros((16, D_MODEL), F32).at[:bsz].set(c).at[bsz].set(c_ctx)
    mod = _modulation(cc, w_ada[0], b_ada[0][None, :])
    mod3 = mod[:bsz].reshape(bsz, 1, N_MOD * D_MODEL)
    csh1 = mod[bsz:bsz + 1, 0:D_MODEL]
    csc1 = mod[bsz:bsz + 1, D_MODEL:2 * D_MODEL]

    w = w_in[0]
    n1g = norm1_g[0][None, :]
    n2g = norm2_g[0][None, :]
    w_kv = w[:, OFF_K:OFF_GA].astype(BF16)
    w_a_cols = jnp.concatenate([w[:, OFF_CG:OFF_Q], w[:, OFF_K:OFF_GA]], axis=1).astype(BF16)
    w_b_cols = jnp.concatenate([w[:, OFF_B:OFF_CG], w[:, OFF_Q:OFF_K], w[:, OFF_GA:]],
                               axis=1).astype(BF16)
    cos_t, sin_t = _rope_tables(seq)

    k_ctx, vt_ctx = _context_kv(ctx, n1g, csh1, csc1, w_kv)
    uc, k_rot, v_t = _proj_a(x, n1g, mod3, w_a_cols, cos_t, sin_t)

    sink_row = jnp.repeat(sink[0].astype(F32), BLOCK)[None, :]
    pad_g = jnp.zeros((D_MODEL, GROUP_ROWS - N_GROUPS), F32)
    pad_e = jnp.zeros((D_MODEL, ROUTER_ROWS - GROUP_ROWS - N_EXPERTS), F32)
    wr_t = jnp.concatenate([w_group[0], pad_g, w_router[0], pad_e], axis=1).T.astype(BF16)
    br_col = jnp.concatenate([b_group[0], pad_g[0], b_router[0], pad_e[0]])[:, None]

    x1, h2, route, w_up_bf, w_down_bf = _mix(
        x, n1g, n2g, mod3, w_b_cols, cos_t, sin_t, uc, k_rot, v_t, k_ctx, vt_ctx,
        w_conv[0], b_conv[0][None, :], w_a[0].astype(BF16), w_b[0].astype(BF16),
        w_o[0].astype(BF16), sink_row, _window_bias(), wr_t, br_col, w_up[0], w_down[0])

    n = bsz * seq
    nch = n // CHUNK
    upper = jnp.asarray(np.triu(np.ones((CHUNK, CHUNK), np.float32), 1), BF16)
    lower = jnp.asarray(np.tril(np.ones((N_EXPERTS, N_EXPERTS), np.float32), -1), BF16)
    xl, pos_c, nb = _dispatch(h2.reshape(n, D_MODEL), route, upper, lower)
    nb = nb[:, :, 0]
    expert_plan, combine_plan = _tile_plan(nb, nch)
    yt = _grouped_mlp(expert_plan, xl.reshape(-1, ROW_BLOCK, XL_COLS), w_up_bf, w_down_bf,
                      _max_tiles(nch))
    out = _combine(*combine_plan, yt, pos_c, x1.reshape(n, D_MODEL), mod3, final_g[None, :], seq)
    return out.reshape(bsz, seq, D_MODEL)
```

```python
import numpy as np
import jax
import jax.numpy as jnp
from jax import lax
from jax.experimental import pallas as pl
from jax.experimental.pallas import tpu as pltpu

F32 = jnp.float32
BF16 = jnp.bfloat16
I32 = jnp.int32

D_MODEL = 1024
GRID_W = 64
CONV_W = 512
N_HEADS = 8
N_KV_HEADS = 2
HEAD_DIM = 64
ATT_W = N_HEADS * HEAD_DIM
KV_W = N_KV_HEADS * HEAD_DIM
BLOCK = 128
ROPE_BASE = 10000.0
N_GROUPS = 4
EXPERTS_PER_GROUP = 8
N_EXPERTS = N_GROUPS * EXPERTS_PER_GROUP
EXPERT_FF = 256
N_MOD = 6
NORM_EPS = 1e-6
NEG_INF = -1e30
LOG2E = 1.4426950408889634
ONES_ROWS = 16

OFF_B, OFF_CG, OFF_XIN, OFF_Q, OFF_K, OFF_V, OFF_GA, OFF_GB = (
    0, 512, 1024, 1536, 2048, 2176, 2304, 3328)
IN_COLS = 4352

LANES = 128
SEQ_TILE = 512
BLOCKS_PER_TILE = SEQ_TILE // BLOCK
ROUTE_ROWS = 8
GROUP_ROWS = 8
ROUTER_ROWS = 48
VMEM_LIMIT = 56 * 1024 * 1024

CHUNK = SEQ_TILE
ROW_BLOCK = 16
CHUNK_ROWS = -(-(2 * CHUNK + N_EXPERTS * (ROW_BLOCK - 1)) // 256) * 256
CHUNK_BLOCKS = CHUNK_ROWS // ROW_BLOCK
GATE_COLS = LANES
XL_COLS = D_MODEL + GATE_COLS
COMMON_ROWS = 2 * CHUNK + 256
COMMON_BLOCKS = COMMON_ROWS // ROW_BLOCK
TILE_BLOCKS = 32
TILE_ROWS = TILE_BLOCKS * ROW_BLOCK
X_SLOTS = 6
Y_SLOTS = 4
W_SLOTS = 3
Y_RING = 3


def _cparams(sem):
    return pltpu.CompilerParams(dimension_semantics=sem, vmem_limit_bytes=VMEM_LIMIT)


def _rms_mod(x, g, shift, scale):
    inv = lax.rsqrt(jnp.mean(x * x, axis=-1, keepdims=True) + NORM_EPS)
    return (x * inv) * (g * (1.0 + scale)) + shift


def _mod_kernel(c_ref, w_ref, b_ref, o_ref):
    c = c_ref[...]
    a = (c * jax.nn.sigmoid(c)).astype(BF16)
    o_ref[...] = jnp.dot(a, w_ref[...].astype(BF16), preferred_element_type=F32) + b_ref[...]


def _modulation(cc, w_ada, b_ada):
    rows = cc.shape[0]
    cols = w_ada.shape[1]
    tile = 1024
    return pl.pallas_call(
        _mod_kernel,
        grid=(cols // tile,),
        in_specs=[pl.BlockSpec((rows, D_MODEL), lambda j: (0, 0)),
                  pl.BlockSpec((D_MODEL, tile), lambda j: (0, j)),
                  pl.BlockSpec((1, tile), lambda j: (0, j))],
        out_specs=pl.BlockSpec((rows, tile), lambda j: (0, j)),
        out_shape=jax.ShapeDtypeStruct((rows, cols), F32),
        compiler_params=_cparams(("arbitrary",)),
        name="adaln_mod",
    )(cc, w_ada, b_ada)


def _ctx_kernel(x_ref, g_ref, sh_ref, sc_ref, w_ref, k_ref, vt_ref):
    h = _rms_mod(x_ref[0], g_ref[...], sh_ref[...], sc_ref[...]).astype(BF16)
    kv = jnp.dot(h, w_ref[...], preferred_element_type=F32)
    k_ref[0] = kv[:, :KV_W].astype(BF16)
    vt_ref[0] = kv[:, KV_W:].T.astype(BF16)


def _context_kv(ctx, norm_g, csh, csc, w_kv):
    bsz, clen, _ = ctx.shape
    return pl.pallas_call(
        _ctx_kernel,
        grid=(bsz,),
        in_specs=[pl.BlockSpec((1, clen, D_MODEL), lambda b: (b, 0, 0)),
                  pl.BlockSpec((1, D_MODEL), lambda b: (0, 0)),
                  pl.BlockSpec((1, D_MODEL), lambda b: (0, 0)),
                  pl.BlockSpec((1, D_MODEL), lambda b: (0, 0)),
                  pl.BlockSpec((D_MODEL, 2 * KV_W), lambda b: (0, 0))],
        out_specs=[pl.BlockSpec((1, clen, KV_W), lambda b: (b, 0, 0)),
                   pl.BlockSpec((1, KV_W, clen), lambda b: (b, 0, 0))],
        out_shape=[jax.ShapeDtypeStruct((bsz, clen, KV_W), BF16),
                   jax.ShapeDtypeStruct((bsz, KV_W, clen), BF16)],
        compiler_params=_cparams(("arbitrary",)),
        name="context_kv",
    )(ctx, norm_g, csh, csc, w_kv)


def _rope(t, cos, sin_signed):
    lane = lax.broadcasted_iota(I32, (1, LANES), 1)
    first_half = (lane % HEAD_DIM) < (HEAD_DIM // 2)
    outs = []
    for j in range(t.shape[1] // LANES):
        tj = t[:, j * LANES:(j + 1) * LANES]
        partner = jnp.where(first_half,
                            pltpu.roll(tj, LANES - HEAD_DIM // 2, axis=1),
                            pltpu.roll(tj, HEAD_DIM // 2, axis=1))
        outs.append(tj * cos + partner * sin_signed)
    return outs[0] if len(outs) == 1 else jnp.concatenate(outs, axis=1)


def _proj_a_kernel(x_ref, g_ref, sh_ref, sc_ref, w_ref, cos_ref, sin_ref, uc_ref, k_ref, vt_ref):
    h = _rms_mod(x_ref[0], g_ref[...], sh_ref[0], sc_ref[0]).astype(BF16)
    u = jnp.dot(h, w_ref[...], preferred_element_type=F32)
    uc_ref[0] = u[:, :CONV_W] * u[:, CONV_W:2 * CONV_W]
    k = u[:, 2 * CONV_W:2 * CONV_W + KV_W]
    k_ref[0] = _rope(k, cos_ref[...], sin_ref[...]).astype(BF16)
    vt_ref[0] = u[:, 2 * CONV_W + KV_W:].T.astype(BF16)


def _proj_a(x, norm_g, mod3, w_a_cols, cos_t, sin_t):
    bsz, seq, _ = x.shape
    nt = seq // SEQ_TILE
    wcols = w_a_cols.shape[1]
    return pl.pallas_call(
        _proj_a_kernel,
        grid=(nt, bsz),
        in_specs=[pl.BlockSpec((1, SEQ_TILE, D_MODEL), lambda t, b: (b, t, 0)),
                  pl.BlockSpec((1, D_MODEL), lambda t, b: (0, 0)),
                  pl.BlockSpec((1, 1, D_MODEL), lambda t, b: (b, 0, 0)),
                  pl.BlockSpec((1, 1, D_MODEL), lambda t, b: (b, 0, 1)),
                  pl.BlockSpec((D_MODEL, wcols), lambda t, b: (0, 0)),
                  pl.BlockSpec((SEQ_TILE, LANES), lambda t, b: (t, 0)),
                  pl.BlockSpec((SEQ_TILE, LANES), lambda t, b: (t, 0))],
        out_specs=[pl.BlockSpec((1, SEQ_TILE, CONV_W), lambda t, b: (b, t, 0)),
                   pl.BlockSpec((1, SEQ_TILE, KV_W), lambda t, b: (b, t, 0)),
                   pl.BlockSpec((1, KV_W, SEQ_TILE), lambda t, b: (b, 0, t))],
        out_shape=[jax.ShapeDtypeStruct((bsz, seq, CONV_W), F32),
                   jax.ShapeDtypeStruct((bsz, seq, KV_W), BF16),
                   jax.ShapeDtypeStruct((bsz, KV_W, seq), BF16)],
        compiler_params=_cparams(("arbitrary", "arbitrary")),
        name="proj_a",
    )(x, norm_g, mod3, mod3, w_a_cols, cos_t, sin_t)


def _sigmoid(x):
    return 1.0 / (1.0 + jnp.exp2(x * (-LOG2E)))


def _attn_scores(q_t, g, k_win, k_ctx, bias_p4, bias_n4):
    zeros = jnp.zeros((HEAD_DIM, 4 * BLOCK), BF16)
    qg = jnp.concatenate([q_t[(4 * g + h) * HEAD_DIM:(4 * g + h + 1) * HEAD_DIM, :]
                          for h in range(4)], axis=1)
    rhs = jnp.concatenate([qg, zeros] if g == 0 else [zeros, qg], axis=0)
    s_win = jnp.dot(k_win, rhs, preferred_element_type=F32)
    s_ctx = jnp.dot(k_ctx, rhs, preferred_element_type=F32)
    return (s_win[0:BLOCK] + bias_p4, s_win[BLOCK:2 * BLOCK], s_win[2 * BLOCK:] + bias_n4, s_ctx)


def _attn_probs(scores, sink):
    m = sink
    for s in scores:
        m = jnp.maximum(m, jnp.max(s, axis=0, keepdims=True))
    p_win = jnp.concatenate([jnp.exp2(s - m).astype(BF16) for s in scores[:3]], axis=0)
    p_ctx = jnp.exp2(scores[3] - m).astype(BF16)
    return p_win, p_ctx, m


def _attn_values(probs, sink, vt_win_g, vt_ctx_g):
    p_win, p_ctx, m = probs

    def with_ones(vt):
        r = lax.broadcasted_iota(I32, (ONES_ROWS, vt.shape[1]), 0)
        return jnp.concatenate([vt, jnp.where(r == 0, 1.0, 0.0).astype(BF16)], axis=0)

    o_ext = (jnp.dot(with_ones(vt_win_g), p_win, preferred_element_type=F32)
             + jnp.dot(with_ones(vt_ctx_g), p_ctx, preferred_element_type=F32))
    denom = o_ext[HEAD_DIM:HEAD_DIM + 1, :] + jnp.exp2(sink - m)
    return o_ext[:HEAD_DIM, :] / denom


def _route(logits_t):
    t = logits_t.shape[1]
    grow = lax.broadcasted_iota(I32, (GROUP_ROWS, t), 0)
    gl = jnp.where(grow < N_GROUPS, logits_t[0:GROUP_ROWS, :], NEG_INF)
    gm = jnp.max(gl, axis=0, keepdims=True)
    p_g = 1.0 / jnp.sum(jnp.exp(gl - gm), axis=0, keepdims=True)
    g_idx = jnp.min(jnp.where(gl == gm, grow, N_GROUPS), axis=0, keepdims=True)

    erow = lax.broadcasted_iota(I32, (N_EXPERTS, t), 0)
    el = logits_t[GROUP_ROWS:GROUP_ROWS + N_EXPERTS, :]
    sel = (erow // EXPERTS_PER_GROUP) == g_idx
    em = jnp.where(sel, el, NEG_INF)
    m1 = jnp.max(em, axis=0, keepdims=True)
    i1 = jnp.min(jnp.where(em == m1, erow, N_EXPERTS), axis=0, keepdims=True)
    em2 = jnp.where(erow == i1, NEG_INF, em)
    m2 = jnp.max(em2, axis=0, keepdims=True)
    i2 = jnp.min(jnp.where(em2 == m2, erow, N_EXPERTS), axis=0, keepdims=True)
    z = jnp.sum(jnp.where(sel, jnp.exp(el - m1), 0.0), axis=0, keepdims=True)
    p1 = 1.0 / z
    p2 = jnp.exp(m2 - m1) / z
    gate1 = p_g * p1 / (p1 + p2)
    gate2 = p_g * p2 / (p1 + p2)
    pad = jnp.zeros((ROUTE_ROWS - 4, t), F32)
    return jnp.concatenate([i1.astype(F32), i2.astype(F32), gate1, gate2, pad], axis=0)


def _mix_kernel(x_ref, g1n_ref, g2n_ref, sh1_ref, sc1_ref, gt1_ref, sh2_ref, sc2_ref,
                wb_ref, cos_ref, sin_ref,
                uc_ref, ucp_ref, ucn_ref, k_ref, kp_ref, kn_ref, vt_ref, vtp_ref, vtn_ref,
                kc_ref, vtc_ref, wconv_ref, bconv_ref, wa_ref, wbb_ref, wo_ref,
                sink_ref, bias_ref, wr_ref, br_ref, wu_ref, wd_ref, upper_ref, lower_ref,
                x1_ref, xl_ref, pos_ref, nb_ref, wub_ref, wdb_ref):
    t_idx = pl.program_id(0)
    n_tiles = pl.num_programs(0)
    is_first = t_idx == 0
    is_last = t_idx == n_tiles - 1

    x = x_ref[0]

    uc = uc_ref[0]
    row = lax.broadcasted_iota(I32, (SEQ_TILE, 1), 0)
    prev_row = jnp.where(is_first, 0.0, ucp_ref[0][7:8, :])
    next_row = jnp.where(is_last, 0.0, ucn_ref[0][0:1, :])
    up = jnp.where(row == 0, prev_row, pltpu.roll(uc, 1, axis=0))
    dn = jnp.where(row == SEQ_TILE - 1, next_row, pltpu.roll(uc, SEQ_TILE - 1, axis=0))
    wconv = wconv_ref[...]
    y = bconv_ref[...] + (up * wconv[0:1, :] + uc * wconv[1:2, :] + dn * wconv[2:3, :])

    hb = _rms_mod(x, g1n_ref[...], sh1_ref[0], sc1_ref[0]).astype(BF16)
    bq = jnp.dot(hb, wb_ref[:, 0:2 * CONV_W], preferred_element_type=F32)
    ya = jnp.dot((bq[:, :CONV_W] * y).astype(BF16), wa_ref[...], preferred_element_type=F32)
    q = _rope(bq[:, CONV_W:], cos_ref[...], sin_ref[...]) * (HEAD_DIM ** -0.5 * LOG2E)

    k_all = jnp.concatenate([kp_ref[0], k_ref[0], kn_ref[0]], axis=0)
    vt_all = jnp.concatenate([vtp_ref[0], vt_ref[0], vtn_ref[0]], axis=1)
    k_ctx = kc_ref[0]
    vt_ctx = vtc_ref[0]
    bias_prev = bias_ref[0:BLOCK, :]
    bias_next = bias_ref[2 * BLOCK:3 * BLOCK, :]
    sink_row = sink_ref[...] * LOG2E
    n_units = BLOCKS_PER_TILE * N_KV_HEADS
    gate_cols = 2 * D_MODEL // n_units
    gate_chunks = []

    def gate_chunk(u):
        c0 = 2 * CONV_W + u * gate_cols
        gate_chunks.append(jnp.dot(hb, wb_ref[:, c0:c0 + gate_cols], preferred_element_type=F32))

    q_ts, biases = [], []
    for jb in range(BLOCKS_PER_TILE):
        bias_p, bias_n = bias_prev, bias_next
        if jb == 0:
            bias_p = bias_prev + jnp.where(is_first, NEG_INF, 0.0)
        if jb == BLOCKS_PER_TILE - 1:
            bias_n = bias_next + jnp.where(is_last, NEG_INF, 0.0)
        biases.append((jnp.concatenate([bias_p] * 4, axis=1), jnp.concatenate([bias_n] * 4, axis=1)))
        q_ts.append(q[jb * BLOCK:(jb + 1) * BLOCK, :].T.astype(BF16))

    scores, probs, outs = {}, {}, {}
    for t in range(n_units + 2):
        if t == 3:
            wub_ref[...] = wu_ref[...].astype(BF16)
        if t == 5:
            wdb_ref[...] = wd_ref[...].astype(BF16)
        if t < n_units:
            jb, g = divmod(t, N_KV_HEADS)
            scores[t] = _attn_scores(q_ts[jb], g, k_all[jb * BLOCK:(jb + 3) * BLOCK, :], k_ctx,
                                     *biases[jb])
            gate_chunk(t)
        if 0 <= t - 1 < n_units:
            g = (t - 1) % N_KV_HEADS
            probs[t - 1] = _attn_probs(scores.pop(t - 1),
                                       sink_row[:, g * 4 * BLOCK:(g + 1) * 4 * BLOCK])
        if 0 <= t - 2 < n_units:
            jb, g = divmod(t - 2, N_KV_HEADS)
            outs[t - 2] = _attn_values(
                probs.pop(t - 2), sink_row[:, g * 4 * BLOCK:(g + 1) * 4 * BLOCK],
                vt_all[g * HEAD_DIM:(g + 1) * HEAD_DIM, jb * BLOCK:(jb + 3) * BLOCK],
                vt_ctx[g * HEAD_DIM:(g + 1) * HEAD_DIM, :])
    o_blocks = []
    for jb in range(BLOCKS_PER_TILE):
        o_rows = [outs[jb * N_KV_HEADS + g][:, h * BLOCK:(h + 1) * BLOCK]
                  for g in range(N_KV_HEADS) for h in range(4)]
        o_blocks.append(jnp.concatenate(o_rows, axis=0).T)
    gates = jnp.concatenate(gate_chunks, axis=1)

    o = jnp.concatenate(o_blocks, axis=0).astype(BF16)
    yb = jnp.dot(o, wbb_ref[...], preferred_element_type=F32)
    merged = (_sigmoid(gates[:, :D_MODEL]) * ya + _sigmoid(gates[:, D_MODEL:]) * yb).astype(BF16)
    x1 = x + gt1_ref[0] * jnp.dot(merged, wo_ref[...], preferred_element_type=F32)
    x1_ref[0] = x1
    h2 = _rms_mod(x1, g2n_ref[...], sh2_ref[0], sc2_ref[0]).astype(BF16)
    logits_t = lax.dot_general(wr_ref[...], h2, (((1,), (1,)), ((), ())),
                               preferred_element_type=F32) + br_ref[...]
    _sort_chunk(h2, _route(logits_t), upper_ref, lower_ref, xl_ref, pos_ref, nb_ref)


def _mix(x, norm1_g, norm2_g, mod3, w_b_cols, cos_t, sin_t, uc, k_rot, v_t, k_ctx, vt_ctx,
         w_conv, b_conv, w_a, w_b, w_o, sink_row, bias, wr_t, br_col, w_up, w_down, upper, lower):
    bsz, seq, _ = x.shape
    nt = seq // SEQ_TILE
    nblk = seq // BLOCK
    clen = k_ctx.shape[1]
    rows8 = SEQ_TILE // 8
    assert N_EXPERTS % (nt * bsz) == 0, "each grid step converts an equal share of the experts"
    e_step = N_EXPERTS // (nt * bsz)

    def mod_spec(j):
        return pl.BlockSpec((1, 1, D_MODEL), lambda t, b, j=j: (b, 0, j))

    def const_spec(shape):
        return pl.BlockSpec(shape, lambda t, b: tuple(0 for _ in shape))

    in_specs = [
        pl.BlockSpec((1, SEQ_TILE, D_MODEL), lambda t, b: (b, t, 0)),
        const_spec((1, D_MODEL)), const_spec((1, D_MODEL)),
        mod_spec(0), mod_spec(1), mod_spec(2), mod_spec(3), mod_spec(4),
        const_spec(w_b_cols.shape),
        pl.BlockSpec((SEQ_TILE, LANES), lambda t, b: (t, 0)),
        pl.BlockSpec((SEQ_TILE, LANES), lambda t, b: (t, 0)),
        pl.BlockSpec((1, SEQ_TILE, CONV_W), lambda t, b: (b, t, 0)),
        pl.BlockSpec((1, 8, CONV_W), lambda t, b: (b, jnp.maximum(t * rows8 - 1, 0), 0)),
        pl.BlockSpec((1, 8, CONV_W),
                     lambda t, b: (b, jnp.minimum((t + 1) * rows8, seq // 8 - 1), 0)),
        pl.BlockSpec((1, SEQ_TILE, KV_W), lambda t, b: (b, t, 0)),
        pl.BlockSpec((1, BLOCK, KV_W),
                     lambda t, b: (b, jnp.maximum(t * BLOCKS_PER_TILE - 1, 0), 0)),
        pl.BlockSpec((1, BLOCK, KV_W),
                     lambda t, b: (b, jnp.minimum((t + 1) * BLOCKS_PER_TILE, nblk - 1), 0)),
        pl.BlockSpec((1, KV_W, SEQ_TILE), lambda t, b: (b, 0, t)),
        pl.BlockSpec((1, KV_W, BLOCK),
                     lambda t, b: (b, 0, jnp.maximum(t * BLOCKS_PER_TILE - 1, 0))),
        pl.BlockSpec((1, KV_W, BLOCK),
                     lambda t, b: (b, 0, jnp.minimum((t + 1) * BLOCKS_PER_TILE, nblk - 1))),
        pl.BlockSpec((1, clen, KV_W), lambda t, b: (b, 0, 0)),
        pl.BlockSpec((1, KV_W, clen), lambda t, b: (b, 0, 0)),
        const_spec(w_conv.shape), const_spec(b_conv.shape),
        const_spec(w_a.shape), const_spec(w_b.shape), const_spec(w_o.shape),
        const_spec(sink_row.shape), const_spec(bias.shape),
        const_spec(wr_t.shape), const_spec(br_col.shape),
        pl.BlockSpec((e_step,) + w_up.shape[1:], lambda t, b: (t * bsz + b, 0, 0)),
        pl.BlockSpec((e_step,) + w_down.shape[1:], lambda t, b: (t * bsz + b, 0, 0)),
        const_spec(upper.shape), const_spec(lower.shape),
    ]
    nch = bsz * nt
    out_specs = [
        pl.BlockSpec((1, SEQ_TILE, D_MODEL), lambda t, b: (b, t, 0)),
        pl.BlockSpec((CHUNK_ROWS, XL_COLS), lambda t, b: (b * nt + t, 0)),
        pl.BlockSpec((CHUNK, LANES), lambda t, b: (b * nt + t, 0)),
        pl.BlockSpec((1, N_EXPERTS, LANES), lambda t, b: (b * nt + t, 0, 0)),
        pl.BlockSpec((e_step,) + w_up.shape[1:], lambda t, b: (t * bsz + b, 0, 0)),
        pl.BlockSpec((e_step,) + w_down.shape[1:], lambda t, b: (t * bsz + b, 0, 0)),
    ]
    out_shape = [
        jax.ShapeDtypeStruct((bsz, seq, D_MODEL), F32),
        jax.ShapeDtypeStruct((nch * CHUNK_ROWS, XL_COLS), BF16),
        jax.ShapeDtypeStruct((nch * CHUNK, LANES), F32),
        jax.ShapeDtypeStruct((nch, N_EXPERTS, LANES), I32),
        jax.ShapeDtypeStruct(w_up.shape, BF16),
        jax.ShapeDtypeStruct(w_down.shape, BF16),
    ]
    return pl.pallas_call(
        _mix_kernel,
        grid=(nt, bsz),
        in_specs=in_specs,
        out_specs=out_specs,
        out_shape=out_shape,
        compiler_params=_cparams(("arbitrary", "arbitrary")),
        name="token_mix",
    )(x, norm1_g, norm2_g, mod3, mod3, mod3, mod3, mod3, w_b_cols, cos_t, sin_t,
      uc, uc, uc, k_rot, k_rot, k_rot, v_t, v_t, v_t, k_ctx, vt_ctx,
      w_conv, b_conv, w_a, w_b, w_o, sink_row, bias, wr_t, br_col, w_up, w_down, upper, lower)


def _bf16_parts(v):
    hi = v.astype(BF16).astype(F32)
    r1 = v - hi
    mid = r1.astype(BF16).astype(F32)
    lo = (r1 - mid).astype(BF16).astype(F32)
    return hi, mid, lo


def _sort_chunk(h, route, upper_ref, lower_ref, xl_ref, pos_ref, nb_ref):
    e1 = route[0:1, :].astype(I32)
    e2 = route[1:2, :].astype(I32)
    erow = lax.broadcasted_iota(I32, (N_EXPERTS, CHUNK), 0)
    hit1 = erow == e1
    hit2 = erow == e2
    onehot = jnp.where(hit1, 1.0, 0.0) + jnp.where(hit2, 1.0, 0.0)
    cum = jnp.dot(onehot.astype(BF16), upper_ref[...], preferred_element_type=F32)
    cnt = jnp.sum(onehot, axis=1, keepdims=True)
    nblk = jnp.floor((cnt + (ROW_BLOCK - 1)) * (1.0 / ROW_BLOCK))
    nblk_b = jnp.broadcast_to(nblk, (N_EXPERTS, LANES))
    seg = jnp.dot(lower_ref[...], nblk_b.astype(BF16), preferred_element_type=F32) * ROW_BLOCK
    base = seg[:, 0:1] + cum
    pos1 = jnp.sum(jnp.where(hit1, base, 0.0), axis=0, keepdims=True)
    pos2 = jnp.sum(jnp.where(hit2, base, 0.0), axis=0, keepdims=True)
    p1i = pos1.astype(I32)
    p2i = pos2.astype(I32)
    used_rows = (jnp.sum(nblk) * ROW_BLOCK).astype(I32)

    prow = lax.broadcasted_iota(I32, (LANES, CHUNK), 0)
    parts = _bf16_parts(route[2:3, :]) + _bf16_parts(route[3:4, :])
    gpart_rows = jnp.zeros((LANES, CHUNK), F32)
    for j, part in enumerate(parts):
        gpart_rows = jnp.where(prow == j, part, gpart_rows)
    h_ext = jnp.concatenate([h, gpart_rows.T.astype(BF16)], axis=1)
    glane = lax.broadcasted_iota(I32, (1, GATE_COLS), 1)
    one = jnp.ones((), BF16)
    zero = jnp.zeros((), BF16)
    def permute(row0, n_rows):
        r16 = lax.broadcasted_iota(I32, (n_rows, CHUNK), 0).astype(jnp.int16)
        q1 = (p1i - row0).astype(jnp.int16)
        q2 = (p2i - row0).astype(jnp.int16)
        sel1 = jnp.where(r16 == q1, one, zero)
        sel = jnp.where(r16 == q2, one, sel1)
        is_slot1 = jnp.max(sel1, axis=1, keepdims=True)
        rows = pl.ds(row0, n_rows)
        xg = jnp.dot(sel, h_ext, preferred_element_type=F32)
        xl_ref[rows, 0:D_MODEL] = xg[:, :D_MODEL].astype(BF16)
        g6 = xg[:, D_MODEL:]
        g3 = jnp.where(is_slot1.astype(F32) > 0.0, g6, pltpu.roll(g6, GATE_COLS - 3, axis=1))
        xl_ref[rows, D_MODEL:XL_COLS] = jnp.where(glane < 3, g3, 0.0).astype(BF16)

    permute(0, COMMON_ROWS)
    tail_rows = CHUNK_ROWS - COMMON_ROWS

    @pl.when(COMMON_ROWS < used_rows)
    def _():
        permute(COMMON_ROWS, tail_rows)

    @pl.when(COMMON_ROWS >= used_rows)
    def _():
        xl_ref[pl.ds(COMMON_ROWS, tail_rows), :] = jnp.zeros((tail_rows, XL_COLS), BF16)

    pos_rows = jnp.where(prow == 0, pos1, jnp.where(prow == 1, pos2, 0.0))
    pos_ref[...] = pos_rows.T
    nb_ref[0] = nblk_b.astype(I32)


def _max_tiles(nch):
    max_blocks = nch * (2 * CHUNK + N_EXPERTS * (ROW_BLOCK - 1)) // ROW_BLOCK
    return max_blocks // TILE_BLOCKS + N_EXPERTS


def _masked_prefix(le, values):
    delta = values - jnp.concatenate([jnp.zeros((1,), values.dtype), values[:-1]])
    return jnp.sum(jnp.where(le, delta[None, :], 0), axis=1)


def _tile_plan(nb, nch):
    n_tiles = _max_tiles(nch)
    nbt = nb.T
    nbe = jnp.sum(nbt, axis=1)
    nte = (nbe + TILE_BLOCKS - 1) // TILE_BLOCKS
    tile_end = jnp.cumsum(nte)
    tile_start = tile_end - nte
    n_act = tile_end[-1]
    tiles = jnp.arange(n_tiles, dtype=I32)
    te = jnp.sum((tile_end[None, :] <= tiles[:, None]).astype(I32), axis=1)
    e_ar = jnp.arange(N_EXPERTS, dtype=I32)
    active = nte > 0
    te = jnp.where(tiles < n_act, te, jnp.max(jnp.where(active, e_ar, 0)))
    first = jnp.logical_and(te != jnp.concatenate([jnp.full((1,), -1, I32), te[:-1]]),
                            tiles < n_act).astype(I32)
    rank = jnp.cumsum(active.astype(I32)) - 1
    k_ar = jnp.arange(N_EXPERTS + W_SLOTS, dtype=I32)
    eseq = jnp.sum(jnp.where(jnp.logical_and(active[None, :], rank[None, :] == k_ar[:, None]),
                             e_ar[None, :], 0), axis=1)
    n_exp = jnp.sum(active.astype(I32))

    cb_excl = jnp.cumsum(nbt, axis=1) - nbt
    gs = TILE_BLOCKS * tile_start[:, None] + cb_excl
    seg_blk = jnp.cumsum(nb, axis=1) - nb
    base_blk = jnp.arange(nch, dtype=I32)[None, :] * CHUNK_BLOCKS + seg_blk.T
    gs_f = gs.reshape(-1)
    slots = jnp.arange(n_tiles * TILE_BLOCKS, dtype=I32)
    le = gs_f[None, :] <= slots[:, None]
    blk = slots + _masked_prefix(le, (base_blk - gs).reshape(-1))
    valid = slots < _masked_prefix(le, (gs + nbt).reshape(-1))
    gblk = jnp.concatenate([jnp.where(valid, blk, 0).astype(I32),
                            jnp.zeros(((X_SLOTS - 1) * TILE_BLOCKS,), I32)])

    m = jnp.arange(CHUNK_BLOCKS, dtype=I32)
    le_c = seg_blk[:, None, :] <= m[None, :, None]
    shift = (gs.T - seg_blk)
    delta = shift - jnp.concatenate([jnp.zeros((nch, 1), I32), shift[:, :-1]], axis=1)
    slot_of = m[None, :] + jnp.sum(jnp.where(le_c, delta[:, None, :], 0), axis=2)
    used_blocks = jnp.sum(nb, axis=1)
    slot_of = jnp.where(m[None, :] < used_blocks[:, None], slot_of, 0).astype(I32)
    pad = Y_RING - 1
    slot_of = jnp.concatenate([slot_of.reshape(-1), jnp.zeros((pad * CHUNK_BLOCKS,), I32)])
    used_rows = jnp.concatenate([ROW_BLOCK * used_blocks, jnp.zeros((pad,), I32)]).astype(I32)
    return ((n_act.reshape(1).astype(I32), n_exp.reshape(1), first, eseq, gblk),
            (used_rows, slot_of))


def _gmm_kernel(nact_ref, nexp_ref, first_ref, eseq_ref, gblk_ref,
                xl_hbm, wu_hbm, wd_hbm, yt_hbm,
                xbuf, ybuf, zbuf, wu_st, wd_st, in_sem, out_sem, w_sem, z_sem):
    n_act = nact_ref[0]
    n_exp = nexp_ref[0]
    n_tiles = yt_hbm.shape[0] // TILE_BLOCKS
    prefetch = X_SLOTS - 1

    def gather(tile, b):
        s = tile % X_SLOTS
        blk = gblk_ref[tile * TILE_BLOCKS + b]
        return pltpu.make_async_copy(xl_hbm.at[blk], xbuf.at[s, b], in_sem.at[s])

    def write_back(tile):
        s = tile % Y_SLOTS
        return pltpu.make_async_copy(ybuf.at[s], yt_hbm.at[pl.ds(tile * TILE_BLOCKS, TILE_BLOCKS)],
                                     out_sem.at[s])

    def zero_fill(tile):
        return pltpu.make_async_copy(zbuf, yt_hbm.at[pl.ds(tile * TILE_BLOCKS, TILE_BLOCKS)], z_sem)

    def weights(q):
        e = eseq_ref[q]
        s = q % W_SLOTS
        return (pltpu.make_async_copy(wu_hbm.at[e], wu_st.at[s], w_sem.at[0, s]),
                pltpu.make_async_copy(wd_hbm.at[e], wd_st.at[s], w_sem.at[1, s]))

    def tile_blocks(fn):
        for b in range(TILE_BLOCKS):
            fn(b)

    for q0 in range(W_SLOTS - 1):
        @pl.when(q0 < n_exp)
        def _(q0=q0):
            for cp in weights(q0):
                cp.start()
    for t0 in range(prefetch):
        tile_blocks(lambda b, t0=t0: gather(t0, b).start())

    def body(i, q):
        tile_blocks(lambda b: gather(i, b).wait())

        @pl.when(i >= Y_SLOTS)
        def _():
            write_back(i - Y_SLOTS).wait()

        is_first = first_ref[i] == 1

        @pl.when(is_first)
        def _():
            for cp in weights(q):
                cp.wait()

            @pl.when(q + W_SLOTS - 1 < n_exp)
            def _():
                for cp in weights(q + W_SLOTS - 1):
                    cp.start()

        q = q + is_first.astype(I32)
        ws = (q - 1) % W_SLOTS
        xs = i % X_SLOTS
        ys = i % Y_SLOTS

        x = xbuf[xs].reshape(TILE_ROWS, XL_COLS)
        gate = jnp.sum(x[:, D_MODEL:].astype(F32), axis=1, keepdims=True)
        au = jnp.dot(x[:, :D_MODEL], wu_st[ws], preferred_element_type=F32)
        tile_blocks(lambda b: gather(i + prefetch, b).start())
        a = au[:, :EXPERT_FF]
        act = (a * _sigmoid(a)) * au[:, EXPERT_FF:]
        y = jnp.dot(act.astype(BF16), wd_st[ws], preferred_element_type=F32)
        ybuf[ys] = (gate * y).astype(BF16).reshape(TILE_BLOCKS, ROW_BLOCK, D_MODEL)
        write_back(i).start()

        @pl.when(n_act + i < n_tiles)
        def _():
            zero_fill(n_act + i).start()
        return q

    zbuf[...] = jnp.zeros_like(zbuf)
    lax.fori_loop(0, n_act, body, jnp.int32(0))

    def fill_rest(t, carry):
        zero_fill(t).start()
        return carry

    lax.fori_loop(jnp.minimum(2 * n_act, n_tiles), n_tiles, fill_rest, 0)

    def drain(t, carry):
        zero_fill(t).wait()
        return carry

    lax.fori_loop(n_act, n_tiles, drain, 0)
    for k in range(prefetch):
        tile_blocks(lambda b, k=k: gather(n_act + k, b).wait())
    for k in range(Y_SLOTS):
        @pl.when(n_act - 1 - k >= 0)
        def _(k=k):
            write_back(n_act - 1 - k).wait()


def _grouped_mlp(plan, xl, w_up, w_down, n_tiles):
    grid_spec = pltpu.PrefetchScalarGridSpec(
        num_scalar_prefetch=len(plan),
        grid=(1,),
        in_specs=[pl.BlockSpec(memory_space=pl.ANY)] * 3,
        out_specs=pl.BlockSpec(memory_space=pl.ANY),
        scratch_shapes=[pltpu.VMEM((X_SLOTS, TILE_BLOCKS, ROW_BLOCK, XL_COLS), BF16),
                        pltpu.VMEM((Y_SLOTS, TILE_BLOCKS, ROW_BLOCK, D_MODEL), BF16),
                        pltpu.VMEM((TILE_BLOCKS, ROW_BLOCK, D_MODEL), BF16),
                        pltpu.VMEM((W_SLOTS, D_MODEL, 2 * EXPERT_FF), BF16),
                        pltpu.VMEM((W_SLOTS, EXPERT_FF, D_MODEL), BF16),
                        pltpu.SemaphoreType.DMA((X_SLOTS,)),
                        pltpu.SemaphoreType.DMA((Y_SLOTS,)),
                        pltpu.SemaphoreType.DMA((2, W_SLOTS)),
                        pltpu.SemaphoreType.DMA(())])
    return pl.pallas_call(
        _gmm_kernel,
        grid_spec=grid_spec,
        out_shape=jax.ShapeDtypeStruct((n_tiles * TILE_BLOCKS, ROW_BLOCK, D_MODEL), BF16),
        compiler_params=_cparams(("arbitrary",)),
        name="moe_experts",
    )(*plan, xl, w_up, w_down)


def _combine_kernel(used_ref, slot_ref, yt_hbm, pos_ref, x1_ref, gt2_ref, fg_ref, o_ref, ybuf, sem):
    c = pl.program_id(0)
    n_chunks = pl.num_programs(0)
    used_rows = used_ref[c]

    def chunk_blocks(chunk, op):
        s = chunk % Y_RING

        def copy(b):
            slot = slot_ref[chunk * CHUNK_BLOCKS + b]
            return pltpu.make_async_copy(yt_hbm.at[slot], ybuf.at[s, b], sem.at[s])

        for b in range(COMMON_BLOCKS):
            op(copy(b))

        @pl.when(used_ref[chunk] > COMMON_ROWS)
        def _():
            for b in range(COMMON_BLOCKS, CHUNK_BLOCKS):
                op(copy(b))

    @pl.when(c == 0)
    def _():
        for k in range(Y_RING - 1):
            chunk_blocks(k, lambda cp: cp.start())

    chunk_blocks(c + Y_RING - 1, lambda cp: cp.start())
    chunk_blocks(c, lambda cp: cp.wait())

    def body(k_rows):
        pos = pos_ref[...]
        p1 = pos[:, 0:1].astype(I32)
        p2 = pos[:, 1:2].astype(I32)
        r = lax.broadcasted_iota(I32, (CHUNK, k_rows), 1)
        sel = jnp.where(r == p1, 1.0, jnp.where(r == p2, 1.0, 0.0)).astype(BF16)
        yl = ybuf[c % Y_RING, 0:k_rows // ROW_BLOCK].reshape(k_rows, D_MODEL)
        y = jnp.dot(sel, yl, preferred_element_type=F32)
        x2 = x1_ref[...] + gt2_ref[0] * y
        inv = lax.rsqrt(jnp.mean(x2 * x2, axis=-1, keepdims=True) + NORM_EPS)
        o_ref[...] = (x2 * inv) * fg_ref[...]

    pl.when(used_rows <= COMMON_ROWS)(lambda: body(COMMON_ROWS))
    pl.when(used_rows > COMMON_ROWS)(lambda: body(CHUNK_ROWS))

    @pl.when(c == n_chunks - 1)
    def _():
        for k in range(1, Y_RING):
            chunk_blocks(c + k, lambda cp: cp.wait())


def _combine(used_rows, slot_of, yt, pos_c, x1, mod3, final_g, seq):
    n = x1.shape[0]
    per_seq = seq // CHUNK
    grid_spec = pltpu.PrefetchScalarGridSpec(
        num_scalar_prefetch=2,
        grid=(n // CHUNK,),
        in_specs=[pl.BlockSpec(memory_space=pl.ANY),
                  pl.BlockSpec((CHUNK, LANES), lambda c, u, s: (c, 0)),
                  pl.BlockSpec((CHUNK, D_MODEL), lambda c, u, s: (c, 0)),
                  pl.BlockSpec((1, 1, D_MODEL), lambda c, u, s: (c // per_seq, 0, 5)),
                  pl.BlockSpec((1, D_MODEL), lambda c, u, s: (0, 0))],
        out_specs=pl.BlockSpec((CHUNK, D_MODEL), lambda c, u, s: (c, 0)),
        scratch_shapes=[pltpu.VMEM((Y_RING, CHUNK_BLOCKS, ROW_BLOCK, D_MODEL), BF16),
                        pltpu.SemaphoreType.DMA((Y_RING,))])
    return pl.pallas_call(
        _combine_kernel,
        grid_spec=grid_spec,
        out_shape=jax.ShapeDtypeStruct((n, D_MODEL), F32),
        compiler_params=_cparams(("arbitrary",)),
        name="moe_combine",
    )(used_rows, slot_of, yt, pos_c, x1, mod3, final_g)


def _rope_tables(seq):
    n_freq = HEAD_DIM // 4
    inv_freq = ROPE_BASE ** (-jnp.arange(n_freq, dtype=F32) / n_freq)
    rows = seq // GRID_W
    row = jnp.repeat(jnp.arange(rows, dtype=F32), GRID_W)
    col = jnp.tile(jnp.arange(GRID_W, dtype=F32), rows)
    ang = jnp.concatenate([row[:, None] * inv_freq, col[:, None] * inv_freq], axis=-1)
    cos, sin = jnp.cos(ang), jnp.sin(ang)
    return jnp.tile(cos, (1, 4)), jnp.concatenate([-sin, sin, -sin, sin], axis=1)


def _window_bias():
    key = np.arange(3 * BLOCK)[:, None]
    qry = np.arange(BLOCK)[None, :]
    valid = (key - qry >= 0) & (key - qry <= 2 * BLOCK)
    return jnp.asarray(np.where(valid, 0.0, NEG_INF), F32)


def kernel(x, c, ctx, c_ctx, w_ada, b_ada, norm1_g, w_in, w_conv, b_conv, w_a, w_b, sink, w_o,
           norm2_g, w_group, b_group, w_router, b_router, w_up, w_down, final_g):
    bsz, seq, _ = x.shape
    assert w_ada.shape[0] == 1 and seq % SEQ_TILE == 0

    cc = jnp.zeros((16, D_MODEL), F32).at[:bsz].set(c).at[bsz].set(c_ctx)
    mod = _modulation(cc, w_ada[0], b_ada[0][None, :])
    mod3 = mod[:bsz].reshape(bsz, 1, N_MOD * D_MODEL)
    csh1 = mod[bsz:bsz + 1, 0:D_MODEL]
    csc1 = mod[bsz:bsz + 1, D_MODEL:2 * D_MODEL]

    w = w_in[0]
    n1g = norm1_g[0][None, :]
    n2g = norm2_g[0][None, :]
    w_kv = w[:, OFF_K:OFF_GA].astype(BF16)
    w_a_cols = jnp.concatenate([w[:, OFF_CG:OFF_Q], w[:, OFF_K:OFF_GA]], axis=1).astype(BF16)
    w_b_cols = jnp.concatenate([w[:, OFF_B:OFF_CG], w[:, OFF_Q:OFF_K], w[:, OFF_GA:]],
                               axis=1).astype(BF16)
    cos_t, sin_t = _rope_tables(seq)

    k_ctx, vt_ctx = _context_kv(ctx, n1g, csh1, csc1, w_kv)
    uc, k_rot, v_t = _proj_a(x, n1g, mod3, w_a_cols, cos_t, sin_t)

    sink_row = jnp.repeat(sink[0].astype(F32), BLOCK)[None, :]
    pad_g = jnp.zeros((D_MODEL, GROUP_ROWS - N_GROUPS), F32)
    pad_e = jnp.zeros((D_MODEL, ROUTER_ROWS - GROUP_ROWS - N_EXPERTS), F32)
    wr_t = jnp.concatenate([w_group[0], pad_g, w_router[0], pad_e], axis=1).T.astype(BF16)
    br_col = jnp.concatenate([b_group[0], pad_g[0], b_router[0], pad_e[0]])[:, None]

    upper = jnp.asarray(np.triu(np.ones((CHUNK, CHUNK), np.float32), 1), BF16)
    lower = jnp.asarray(np.tril(np.ones((N_EXPERTS, N_EXPERTS), np.float32), -1), BF16)
    x1, xl, pos_c, nb, w_up_bf, w_down_bf = _mix(
        x, n1g, n2g, mod3, w_b_cols, cos_t, sin_t, uc, k_rot, v_t, k_ctx, vt_ctx,
        w_conv[0], b_conv[0][None, :], w_a[0].astype(BF16), w_b[0].astype(BF16),
        w_o[0].astype(BF16), sink_row, _window_bias(), wr_t, br_col, w_up[0], w_down[0],
        upper, lower)

    n = bsz * seq
    nch = n // CHUNK
    nb = nb[:, :, 0]
    expert_plan, combine_plan = _tile_plan(nb, nch)
    yt = _grouped_mlp(expert_plan, xl.reshape(-1, ROW_BLOCK, XL_COLS), w_up_bf, w_down_bf,
                      _max_tiles(nch))
    out = _combine(*combine_plan, yt, pos_c, x1.reshape(n, D_MODEL), mod3, final_g[None, :], seq)
    return out.reshape(bsz, seq, D_MODEL)
```

```python
import functools

import numpy as np
import jax
import jax.numpy as jnp
from jax import lax
from jax.experimental import pallas as pl
from jax.experimental.pallas import tpu as pltpu

F32 = jnp.float32
BF16 = jnp.bfloat16
I32 = jnp.int32

D_MODEL = 1024
GRID_W = 64
CONV_W = 512
N_HEADS = 8
N_KV_HEADS = 2
HEAD_DIM = 64
ATT_W = N_HEADS * HEAD_DIM
KV_W = N_KV_HEADS * HEAD_DIM
BLOCK = 128
ROPE_BASE = 10000.0
N_GROUPS = 4
EXPERTS_PER_GROUP = 8
N_EXPERTS = N_GROUPS * EXPERTS_PER_GROUP
EXPERT_FF = 256
N_MOD = 6
NORM_EPS = 1e-6
NEG_INF = -1e30
LOG2E = 1.4426950408889634
ONES_ROWS = 16

OFF_B, OFF_CG, OFF_XIN, OFF_Q, OFF_K, OFF_V, OFF_GA, OFF_GB = (
    0, 512, 1024, 1536, 2048, 2176, 2304, 3328)
IN_COLS = 4352

LANES = 128
SEQ_TILE = 512
BLOCKS_PER_TILE = SEQ_TILE // BLOCK
ROUTE_ROWS = 8
GROUP_ROWS = 8
ROUTER_ROWS = 48
VMEM_LIMIT = 56 * 1024 * 1024

CHUNK = SEQ_TILE
ROW_BLOCK = 16
CHUNK_ROWS = -(-(2 * CHUNK + N_EXPERTS * (ROW_BLOCK - 1)) // 256) * 256
CHUNK_BLOCKS = CHUNK_ROWS // ROW_BLOCK
GATE_COLS = LANES
XL_COLS = D_MODEL + GATE_COLS
COMMON_ROWS = 2 * CHUNK + 256
COMMON_BLOCKS = COMMON_ROWS // ROW_BLOCK
TILE_BLOCKS = 32
TILE_ROWS = TILE_BLOCKS * ROW_BLOCK
X_SLOTS = 6
Y_SLOTS = 4
W_SLOTS = 3
Y_RING = 3


def _cparams(sem):
    return pltpu.CompilerParams(dimension_semantics=sem, vmem_limit_bytes=VMEM_LIMIT)


def _rms_mod(x, g, shift, scale):
    inv = lax.rsqrt(jnp.mean(x * x, axis=-1, keepdims=True) + NORM_EPS)
    return (x * inv) * (g * (1.0 + scale)) + shift


def _mod_kernel(c_ref, w_ref, b_ref, o_ref):
    c = c_ref[...]
    a = (c * jax.nn.sigmoid(c)).astype(BF16)
    o_ref[...] = jnp.dot(a, w_ref[...].astype(BF16), preferred_element_type=F32) + b_ref[...]


def _modulation(cc, w_ada, b_ada):
    rows = cc.shape[0]
    cols = w_ada.shape[1]
    tile = 1024
    return pl.pallas_call(
        _mod_kernel,
        grid=(cols // tile,),
        in_specs=[pl.BlockSpec((rows, D_MODEL), lambda j: (0, 0)),
                  pl.BlockSpec((D_MODEL, tile), lambda j: (0, j)),
                  pl.BlockSpec((1, tile), lambda j: (0, j))],
        out_specs=pl.BlockSpec((rows, tile), lambda j: (0, j)),
        out_shape=jax.ShapeDtypeStruct((rows, cols), F32),
        compiler_params=_cparams(("arbitrary",)),
        name="adaln_mod",
    )(cc, w_ada, b_ada)


def _ctx_kernel(x_ref, g_ref, sh_ref, sc_ref, w_ref, k_ref, vt_ref):
    h = _rms_mod(x_ref[0], g_ref[...], sh_ref[...], sc_ref[...]).astype(BF16)
    kv = jnp.dot(h, w_ref[...], preferred_element_type=F32)
    k_ref[0] = kv[:, :KV_W].astype(BF16)
    vt_ref[0] = kv[:, KV_W:].T.astype(BF16)


def _context_kv(ctx, norm_g, csh, csc, w_kv):
    bsz, clen, _ = ctx.shape
    return pl.pallas_call(
        _ctx_kernel,
        grid=(bsz,),
        in_specs=[pl.BlockSpec((1, clen, D_MODEL), lambda b: (b, 0, 0)),
                  pl.BlockSpec((1, D_MODEL), lambda b: (0, 0)),
                  pl.BlockSpec((1, D_MODEL), lambda b: (0, 0)),
                  pl.BlockSpec((1, D_MODEL), lambda b: (0, 0)),
                  pl.BlockSpec((D_MODEL, 2 * KV_W), lambda b: (0, 0))],
        out_specs=[pl.BlockSpec((1, clen, KV_W), lambda b: (b, 0, 0)),
                   pl.BlockSpec((1, KV_W, clen), lambda b: (b, 0, 0))],
        out_shape=[jax.ShapeDtypeStruct((bsz, clen, KV_W), BF16),
                   jax.ShapeDtypeStruct((bsz, KV_W, clen), BF16)],
        compiler_params=_cparams(("arbitrary",)),
        name="context_kv",
    )(ctx, norm_g, csh, csc, w_kv)


def _rope(t, cos, sin_signed):
    lane = lax.broadcasted_iota(I32, (1, LANES), 1)
    first_half = (lane % HEAD_DIM) < (HEAD_DIM // 2)
    outs = []
    for j in range(t.shape[1] // LANES):
        tj = t[:, j * LANES:(j + 1) * LANES]
        partner = jnp.where(first_half,
                            pltpu.roll(tj, LANES - HEAD_DIM // 2, axis=1),
                            pltpu.roll(tj, HEAD_DIM // 2, axis=1))
        outs.append(tj * cos + partner * sin_signed)
    return outs[0] if len(outs) == 1 else jnp.concatenate(outs, axis=1)


def _proj_a_kernel(x_ref, g_ref, sh_ref, sc_ref, w_ref, cos_ref, sin_ref, uc_ref, k_ref, vt_ref):
    h = _rms_mod(x_ref[0], g_ref[...], sh_ref[0], sc_ref[0]).astype(BF16)
    u = jnp.dot(h, w_ref[...], preferred_element_type=F32)
    uc_ref[0] = u[:, :CONV_W] * u[:, CONV_W:2 * CONV_W]
    k = u[:, 2 * CONV_W:2 * CONV_W + KV_W]
    k_ref[0] = _rope(k, cos_ref[...], sin_ref[...]).astype(BF16)
    vt_ref[0] = u[:, 2 * CONV_W + KV_W:].T.astype(BF16)


def _proj_a(x, norm_g, mod3, w_a_cols, cos_t, sin_t):
    bsz, seq, _ = x.shape
    nt = seq // SEQ_TILE
    wcols = w_a_cols.shape[1]
    return pl.pallas_call(
        _proj_a_kernel,
        grid=(nt, bsz),
        in_specs=[pl.BlockSpec((1, SEQ_TILE, D_MODEL), lambda t, b: (b, t, 0)),
                  pl.BlockSpec((1, D_MODEL), lambda t, b: (0, 0)),
                  pl.BlockSpec((1, 1, D_MODEL), lambda t, b: (b, 0, 0)),
                  pl.BlockSpec((1, 1, D_MODEL), lambda t, b: (b, 0, 1)),
                  pl.BlockSpec((D_MODEL, wcols), lambda t, b: (0, 0)),
                  pl.BlockSpec((SEQ_TILE, LANES), lambda t, b: (t, 0)),
                  pl.BlockSpec((SEQ_TILE, LANES), lambda t, b: (t, 0))],
        out_specs=[pl.BlockSpec((1, SEQ_TILE, CONV_W), lambda t, b: (b, t, 0)),
                   pl.BlockSpec((1, SEQ_TILE, KV_W), lambda t, b: (b, t, 0)),
                   pl.BlockSpec((1, KV_W, SEQ_TILE), lambda t, b: (b, 0, t))],
        out_shape=[jax.ShapeDtypeStruct((bsz, seq, CONV_W), F32),
                   jax.ShapeDtypeStruct((bsz, seq, KV_W), BF16),
                   jax.ShapeDtypeStruct((bsz, KV_W, seq), BF16)],
        compiler_params=_cparams(("arbitrary", "arbitrary")),
        name="proj_a",
    )(x, norm_g, mod3, mod3, w_a_cols, cos_t, sin_t)


def _sigmoid(x):
    return 1.0 / (1.0 + jnp.exp2(x * (-LOG2E)))


def _attn_scores(q_t, g, k_win, k_ctx, bias_p4, bias_n4):
    zeros = jnp.zeros((HEAD_DIM, 4 * BLOCK), BF16)
    qg = jnp.concatenate([q_t[(4 * g + h) * HEAD_DIM:(4 * g + h + 1) * HEAD_DIM, :]
                          for h in range(4)], axis=1)
    rhs = jnp.concatenate([qg, zeros] if g == 0 else [zeros, qg], axis=0)
    s_win = jnp.dot(k_win, rhs, preferred_element_type=F32)
    s_ctx = jnp.dot(k_ctx, rhs, preferred_element_type=F32)
    return (s_win[0:BLOCK] + bias_p4, s_win[BLOCK:2 * BLOCK], s_win[2 * BLOCK:] + bias_n4, s_ctx)


def _attn_probs(scores, sink):
    m = sink
    for s in scores:
        m = jnp.maximum(m, jnp.max(s, axis=0, keepdims=True))
    p_win = jnp.concatenate([jnp.exp2(s - m).astype(BF16) for s in scores[:3]], axis=0)
    p_ctx = jnp.exp2(scores[3] - m).astype(BF16)
    return p_win, p_ctx, m


def _attn_values(probs, sink, vt_win_g, vt_ctx_g):
    p_win, p_ctx, m = probs

    def with_ones(vt):
        r = lax.broadcasted_iota(I32, (ONES_ROWS, vt.shape[1]), 0)
        return jnp.concatenate([vt, jnp.where(r == 0, 1.0, 0.0).astype(BF16)], axis=0)

    o_ext = (jnp.dot(with_ones(vt_win_g), p_win, preferred_element_type=F32)
             + jnp.dot(with_ones(vt_ctx_g), p_ctx, preferred_element_type=F32))
    denom = o_ext[HEAD_DIM:HEAD_DIM + 1, :] + jnp.exp2(sink - m)
    return o_ext[:HEAD_DIM, :] / denom


def _route(logits_t):
    t = logits_t.shape[1]
    grow = lax.broadcasted_iota(I32, (GROUP_ROWS, t), 0)
    gl = jnp.where(grow < N_GROUPS, logits_t[0:GROUP_ROWS, :], NEG_INF)
    gm = jnp.max(gl, axis=0, keepdims=True)
    p_g = 1.0 / jnp.sum(jnp.exp(gl - gm), axis=0, keepdims=True)
    g_idx = jnp.min(jnp.where(gl == gm, grow, N_GROUPS), axis=0, keepdims=True)

    erow = lax.broadcasted_iota(I32, (N_EXPERTS, t), 0)
    el = logits_t[GROUP_ROWS:GROUP_ROWS + N_EXPERTS, :]
    sel = (erow // EXPERTS_PER_GROUP) == g_idx
    em = jnp.where(sel, el, NEG_INF)
    m1 = jnp.max(em, axis=0, keepdims=True)
    i1 = jnp.min(jnp.where(em == m1, erow, N_EXPERTS), axis=0, keepdims=True)
    em2 = jnp.where(erow == i1, NEG_INF, em)
    m2 = jnp.max(em2, axis=0, keepdims=True)
    i2 = jnp.min(jnp.where(em2 == m2, erow, N_EXPERTS), axis=0, keepdims=True)
    z = jnp.sum(jnp.where(sel, jnp.exp(el - m1), 0.0), axis=0, keepdims=True)
    p1 = 1.0 / z
    p2 = jnp.exp(m2 - m1) / z
    gate1 = p_g * p1 / (p1 + p2)
    gate2 = p_g * p2 / (p1 + p2)
    pad = jnp.zeros((ROUTE_ROWS - 4, t), F32)
    return jnp.concatenate([i1.astype(F32), i2.astype(F32), gate1, gate2, pad], axis=0)


def _mix_kernel(x_ref, g1n_ref, g2n_ref, sh1_ref, sc1_ref, gt1_ref, sh2_ref, sc2_ref,
                wb_ref, cos_ref, sin_ref,
                uc_ref, ucp_ref, ucn_ref, k_ref, kp_ref, kn_ref, vt_ref, vtp_ref, vtn_ref,
                kc_ref, vtc_ref, wconv_ref, bconv_ref, wa_ref, wbb_ref, wo_ref,
                sink_ref, bias_ref, wr_ref, br_ref, wu_ref, wd_ref, upper_ref, lower_ref,
                x1_ref, xl_ref, pos_ref, nb_ref, wub_ref, wdb_ref, h2_scr, route_scr,
                *, n_seq_tiles, batch):
    step = pl.program_id(0)
    t_idx = jnp.minimum(step, n_seq_tiles * batch - 1) // batch
    is_first = t_idx == 0
    is_last = t_idx == n_seq_tiles - 1

    @pl.when(step == 0)
    def _():
        h2_scr[...] = jnp.zeros_like(h2_scr)
        route_scr[...] = jnp.zeros_like(route_scr)

    permute, permute_tail = _sort_chunk(h2_scr[...], route_scr[...], upper_ref, lower_ref,
                                        xl_ref, pos_ref, nb_ref)
    x = x_ref[0]

    uc = uc_ref[0]
    row = lax.broadcasted_iota(I32, (SEQ_TILE, 1), 0)
    prev_row = jnp.where(is_first, 0.0, ucp_ref[0][7:8, :])
    next_row = jnp.where(is_last, 0.0, ucn_ref[0][0:1, :])
    up = jnp.where(row == 0, prev_row, pltpu.roll(uc, 1, axis=0))
    dn = jnp.where(row == SEQ_TILE - 1, next_row, pltpu.roll(uc, SEQ_TILE - 1, axis=0))
    wconv = wconv_ref[...]
    y = bconv_ref[...] + (up * wconv[0:1, :] + uc * wconv[1:2, :] + dn * wconv[2:3, :])

    hb = _rms_mod(x, g1n_ref[...], sh1_ref[0], sc1_ref[0]).astype(BF16)
    bq = jnp.dot(hb, wb_ref[:, 0:2 * CONV_W], preferred_element_type=F32)
    ya = jnp.dot((bq[:, :CONV_W] * y).astype(BF16), wa_ref[...], preferred_element_type=F32)
    q = _rope(bq[:, CONV_W:], cos_ref[...], sin_ref[...]) * (HEAD_DIM ** -0.5 * LOG2E)

    k_all = jnp.concatenate([kp_ref[0], k_ref[0], kn_ref[0]], axis=0)
    vt_all = jnp.concatenate([vtp_ref[0], vt_ref[0], vtn_ref[0]], axis=1)
    k_ctx = kc_ref[0]
    vt_ctx = vtc_ref[0]
    bias_prev = bias_ref[0:BLOCK, :]
    bias_next = bias_ref[2 * BLOCK:3 * BLOCK, :]
    sink_row = sink_ref[...] * LOG2E
    n_units = BLOCKS_PER_TILE * N_KV_HEADS
    gate_cols = 2 * D_MODEL // n_units
    gate_chunks = []

    def gate_chunk(u):
        c0 = 2 * CONV_W + u * gate_cols
        gate_chunks.append(jnp.dot(hb, wb_ref[:, c0:c0 + gate_cols], preferred_element_type=F32))

    q_ts, biases = [], []
    for jb in range(BLOCKS_PER_TILE):
        bias_p, bias_n = bias_prev, bias_next
        if jb == 0:
            bias_p = bias_prev + jnp.where(is_first, NEG_INF, 0.0)
        if jb == BLOCKS_PER_TILE - 1:
            bias_n = bias_next + jnp.where(is_last, NEG_INF, 0.0)
        biases.append((jnp.concatenate([bias_p] * 4, axis=1), jnp.concatenate([bias_n] * 4, axis=1)))
        q_ts.append(q[jb * BLOCK:(jb + 1) * BLOCK, :].T.astype(BF16))

    scores, probs, outs = {}, {}, {}
    for t in range(n_units + 2):
        if t == 3:
            wub_ref[...] = wu_ref[...].astype(BF16)
        if t == 5:
            wdb_ref[...] = wd_ref[...].astype(BF16)
        if t == 2:
            permute(0, COMMON_ROWS // 2)
        if t == 6:
            permute(COMMON_ROWS // 2, COMMON_ROWS // 2)
        if t < n_units:
            jb, g = divmod(t, N_KV_HEADS)
            scores[t] = _attn_scores(q_ts[jb], g, k_all[jb * BLOCK:(jb + 3) * BLOCK, :], k_ctx,
                                     *biases[jb])
            gate_chunk(t)
        if 0 <= t - 1 < n_units:
            g = (t - 1) % N_KV_HEADS
            probs[t - 1] = _attn_probs(scores.pop(t - 1),
                                       sink_row[:, g * 4 * BLOCK:(g + 1) * 4 * BLOCK])
        if 0 <= t - 2 < n_units:
            jb, g = divmod(t - 2, N_KV_HEADS)
            outs[t - 2] = _attn_values(
                probs.pop(t - 2), sink_row[:, g * 4 * BLOCK:(g + 1) * 4 * BLOCK],
                vt_all[g * HEAD_DIM:(g + 1) * HEAD_DIM, jb * BLOCK:(jb + 3) * BLOCK],
                vt_ctx[g * HEAD_DIM:(g + 1) * HEAD_DIM, :])
    o_blocks = []
    for jb in range(BLOCKS_PER_TILE):
        o_rows = [outs[jb * N_KV_HEADS + g][:, h * BLOCK:(h + 1) * BLOCK]
                  for g in range(N_KV_HEADS) for h in range(4)]
        o_blocks.append(jnp.concatenate(o_rows, axis=0).T)
    gates = jnp.concatenate(gate_chunks, axis=1)

    o = jnp.concatenate(o_blocks, axis=0).astype(BF16)
    yb = jnp.dot(o, wbb_ref[...], preferred_element_type=F32)
    merged = (_sigmoid(gates[:, :D_MODEL]) * ya + _sigmoid(gates[:, D_MODEL:]) * yb).astype(BF16)
    x1 = x + gt1_ref[0] * jnp.dot(merged, wo_ref[...], preferred_element_type=F32)
    x1_ref[0] = x1
    h2 = _rms_mod(x1, g2n_ref[...], sh2_ref[0], sc2_ref[0]).astype(BF16)
    logits_t = lax.dot_general(wr_ref[...], h2, (((1,), (1,)), ((), ())),
                               preferred_element_type=F32) + br_ref[...]
    permute_tail()
    h2_scr[...] = h2
    route_scr[...] = _route(logits_t)


def _mix(x, norm1_g, norm2_g, mod3, w_b_cols, cos_t, sin_t, uc, k_rot, v_t, k_ctx, vt_ctx,
         w_conv, b_conv, w_a, w_b, w_o, sink_row, bias, wr_t, br_col, w_up, w_down, upper, lower):
    bsz, seq, _ = x.shape
    nt = seq // SEQ_TILE
    nblk = seq // BLOCK
    clen = k_ctx.shape[1]
    rows8 = SEQ_TILE // 8
    assert N_EXPERTS % (nt * bsz) == 0, "each grid step converts an equal share of the experts"
    e_step = N_EXPERTS // (nt * bsz)

    def mod_spec(j):
        return pl.BlockSpec((1, 1, D_MODEL), lambda t, b, j=j: (b, 0, j))

    def const_spec(shape):
        return pl.BlockSpec(shape, lambda t, b: tuple(0 for _ in shape))

    in_specs = [
        pl.BlockSpec((1, SEQ_TILE, D_MODEL), lambda t, b: (b, t, 0)),
        const_spec((1, D_MODEL)), const_spec((1, D_MODEL)),
        mod_spec(0), mod_spec(1), mod_spec(2), mod_spec(3), mod_spec(4),
        const_spec(w_b_cols.shape),
        pl.BlockSpec((SEQ_TILE, LANES), lambda t, b: (t, 0)),
        pl.BlockSpec((SEQ_TILE, LANES), lambda t, b: (t, 0)),
        pl.BlockSpec((1, SEQ_TILE, CONV_W), lambda t, b: (b, t, 0)),
        pl.BlockSpec((1, 8, CONV_W), lambda t, b: (b, jnp.maximum(t * rows8 - 1, 0), 0)),
        pl.BlockSpec((1, 8, CONV_W),
                     lambda t, b: (b, jnp.minimum((t + 1) * rows8, seq // 8 - 1), 0)),
        pl.BlockSpec((1, SEQ_TILE, KV_W), lambda t, b: (b, t, 0)),
        pl.BlockSpec((1, BLOCK, KV_W),
                     lambda t, b: (b, jnp.maximum(t * BLOCKS_PER_TILE - 1, 0), 0)),
        pl.BlockSpec((1, BLOCK, KV_W),
                     lambda t, b: (b, jnp.minimum((t + 1) * BLOCKS_PER_TILE, nblk - 1), 0)),
        pl.BlockSpec((1, KV_W, SEQ_TILE), lambda t, b: (b, 0, t)),
        pl.BlockSpec((1, KV_W, BLOCK),
                     lambda t, b: (b, 0, jnp.maximum(t * BLOCKS_PER_TILE - 1, 0))),
        pl.BlockSpec((1, KV_W, BLOCK),
                     lambda t, b: (b, 0, jnp.minimum((t + 1) * BLOCKS_PER_TILE, nblk - 1))),
        pl.BlockSpec((1, clen, KV_W), lambda t, b: (b, 0, 0)),
        pl.BlockSpec((1, KV_W, clen), lambda t, b: (b, 0, 0)),
        const_spec(w_conv.shape), const_spec(b_conv.shape),
        const_spec(w_a.shape), const_spec(w_b.shape), const_spec(w_o.shape),
        const_spec(sink_row.shape), const_spec(bias.shape),
        const_spec(wr_t.shape), const_spec(br_col.shape),
        pl.BlockSpec((e_step,) + w_up.shape[1:], lambda t, b: (t * bsz + b, 0, 0)),
        pl.BlockSpec((e_step,) + w_down.shape[1:], lambda t, b: (t * bsz + b, 0, 0)),
        const_spec(upper.shape), const_spec(lower.shape),
    ]
    nch = bsz * nt
    out_specs = [
        pl.BlockSpec((1, SEQ_TILE, D_MODEL), lambda t, b: (b, t, 0)),
        pl.BlockSpec((CHUNK_ROWS, XL_COLS), lambda t, b: (b * nt + t, 0)),
        pl.BlockSpec((CHUNK, LANES), lambda t, b: (b * nt + t, 0)),
        pl.BlockSpec((1, N_EXPERTS, LANES), lambda t, b: (b * nt + t, 0, 0)),
        pl.BlockSpec((e_step,) + w_up.shape[1:], lambda t, b: (t * bsz + b, 0, 0)),
        pl.BlockSpec((e_step,) + w_down.shape[1:], lambda t, b: (t * bsz + b, 0, 0)),
    ]
    out_shape = [
        jax.ShapeDtypeStruct((bsz, seq, D_MODEL), F32),
        jax.ShapeDtypeStruct((nch * CHUNK_ROWS, XL_COLS), BF16),
        jax.ShapeDtypeStruct((nch * CHUNK, LANES), F32),
        jax.ShapeDtypeStruct((nch, N_EXPERTS, LANES), I32),
        jax.ShapeDtypeStruct(w_up.shape, BF16),
        jax.ShapeDtypeStruct(w_down.shape, BF16),
    ]
    n_main = nt * bsz

    def tile_of(s):
        s = jnp.minimum(s, n_main - 1)
        return s // bsz, s % bsz

    def this_tile(spec):
        return pl.BlockSpec(spec.block_shape, lambda s, f=spec.index_map: f(*tile_of(s)))

    def prev_tile(spec):
        return pl.BlockSpec(spec.block_shape,
                            lambda s, f=spec.index_map: f(*tile_of(jnp.maximum(s - 1, 0))))

    in_specs = [this_tile(sp) for sp in in_specs]
    out_specs = [this_tile(out_specs[0])] + [prev_tile(sp) for sp in out_specs[1:4]] + [
        this_tile(sp) for sp in out_specs[4:]]
    return pl.pallas_call(
        functools.partial(_mix_kernel, n_seq_tiles=nt, batch=bsz),
        grid=(n_main + 1,),
        in_specs=in_specs,
        out_specs=out_specs,
        out_shape=out_shape,
        scratch_shapes=[pltpu.VMEM((CHUNK, D_MODEL), BF16), pltpu.VMEM((ROUTE_ROWS, CHUNK), F32)],
        compiler_params=_cparams(("arbitrary",)),
        name="token_mix",
    )(x, norm1_g, norm2_g, mod3, mod3, mod3, mod3, mod3, w_b_cols, cos_t, sin_t,
      uc, uc, uc, k_rot, k_rot, k_rot, v_t, v_t, v_t, k_ctx, vt_ctx,
      w_conv, b_conv, w_a, w_b, w_o, sink_row, bias, wr_t, br_col, w_up, w_down, upper, lower)


def _bf16_parts(v):
    hi = v.astype(BF16).astype(F32)
    r1 = v - hi
    mid = r1.astype(BF16).astype(F32)
    lo = (r1 - mid).astype(BF16).astype(F32)
    return hi, mid, lo


def _sort_chunk(h, route, upper_ref, lower_ref, xl_ref, pos_ref, nb_ref):
    e1 = route[0:1, :].astype(I32)
    e2 = route[1:2, :].astype(I32)
    erow = lax.broadcasted_iota(I32, (N_EXPERTS, CHUNK), 0)
    hit1 = erow == e1
    hit2 = erow == e2
    onehot = jnp.where(hit1, 1.0, 0.0) + jnp.where(hit2, 1.0, 0.0)
    cum = jnp.dot(onehot.astype(BF16), upper_ref[...], preferred_element_type=F32)
    cnt = jnp.sum(onehot, axis=1, keepdims=True)
    nblk = jnp.floor((cnt + (ROW_BLOCK - 1)) * (1.0 / ROW_BLOCK))
    nblk_b = jnp.broadcast_to(nblk, (N_EXPERTS, LANES))
    seg = jnp.dot(lower_ref[...], nblk_b.astype(BF16), preferred_element_type=F32) * ROW_BLOCK
    base = seg[:, 0:1] + cum
    pos1 = jnp.sum(jnp.where(hit1, base, 0.0), axis=0, keepdims=True)
    pos2 = jnp.sum(jnp.where(hit2, base, 0.0), axis=0, keepdims=True)
    p1i = pos1.astype(I32)
    p2i = pos2.astype(I32)
    used_rows = (jnp.sum(nblk) * ROW_BLOCK).astype(I32)

    prow = lax.broadcasted_iota(I32, (LANES, CHUNK), 0)
    parts = _bf16_parts(route[2:3, :]) + _bf16_parts(route[3:4, :])
    gpart_rows = jnp.zeros((LANES, CHUNK), F32)
    for j, part in enumerate(parts):
        gpart_rows = jnp.where(prow == j, part, gpart_rows)
    h_ext = jnp.concatenate([h, gpart_rows.T.astype(BF16)], axis=1)
    glane = lax.broadcasted_iota(I32, (1, GATE_COLS), 1)
    one = jnp.ones((), BF16)
    zero = jnp.zeros((), BF16)
    def permute(row0, n_rows):
        r16 = lax.broadcasted_iota(I32, (n_rows, CHUNK), 0).astype(jnp.int16)
        q1 = (p1i - row0).astype(jnp.int16)
        q2 = (p2i - row0).astype(jnp.int16)
        sel1 = jnp.where(r16 == q1, one, zero)
        sel = jnp.where(r16 == q2, one, sel1)
        is_slot1 = jnp.max(sel1, axis=1, keepdims=True)
        rows = pl.ds(row0, n_rows)
        xg = jnp.dot(sel, h_ext, preferred_element_type=F32)
        xl_ref[rows, 0:D_MODEL] = xg[:, :D_MODEL].astype(BF16)
        g6 = xg[:, D_MODEL:]
        g3 = jnp.where(is_slot1.astype(F32) > 0.0, g6, pltpu.roll(g6, GATE_COLS - 3, axis=1))
        xl_ref[rows, D_MODEL:XL_COLS] = jnp.where(glane < 3, g3, 0.0).astype(BF16)

    pos_rows = jnp.where(prow == 0, pos1, jnp.where(prow == 1, pos2, 0.0))
    pos_ref[...] = pos_rows.T
    nb_ref[0] = nblk_b.astype(I32)

    def permute_tail():
        tail_rows = CHUNK_ROWS - COMMON_ROWS

        @pl.when(COMMON_ROWS < used_rows)
        def _():
            permute(COMMON_ROWS, tail_rows)

        @pl.when(COMMON_ROWS >= used_rows)
        def _():
            xl_ref[pl.ds(COMMON_ROWS, tail_rows), :] = jnp.zeros((tail_rows, XL_COLS), BF16)

    return permute, permute_tail


def _max_tiles(nch):
    max_blocks = nch * (2 * CHUNK + N_EXPERTS * (ROW_BLOCK - 1)) // ROW_BLOCK
    return max_blocks // TILE_BLOCKS + N_EXPERTS


def _masked_prefix(le, values):
    delta = values - jnp.concatenate([jnp.zeros((1,), values.dtype), values[:-1]])
    return jnp.sum(jnp.where(le, delta[None, :], 0), axis=1)


def _tile_plan(nb, nch):
    n_tiles = _max_tiles(nch)
    nbt = nb.T
    nbe = jnp.sum(nbt, axis=1)
    nte = (nbe + TILE_BLOCKS - 1) // TILE_BLOCKS
    tile_end = jnp.cumsum(nte)
    tile_start = tile_end - nte
    n_act = tile_end[-1]
    tiles = jnp.arange(n_tiles, dtype=I32)
    te = jnp.sum((tile_end[None, :] <= tiles[:, None]).astype(I32), axis=1)
    e_ar = jnp.arange(N_EXPERTS, dtype=I32)
    active = nte > 0
    te = jnp.where(tiles < n_act, te, jnp.max(jnp.where(active, e_ar, 0)))
    first = jnp.logical_and(te != jnp.concatenate([jnp.full((1,), -1, I32), te[:-1]]),
                            tiles < n_act).astype(I32)
    rank = jnp.cumsum(active.astype(I32)) - 1
    k_ar = jnp.arange(N_EXPERTS + W_SLOTS, dtype=I32)
    eseq = jnp.sum(jnp.where(jnp.logical_and(active[None, :], rank[None, :] == k_ar[:, None]),
                             e_ar[None, :], 0), axis=1)
    n_exp = jnp.sum(active.astype(I32))

    cb_excl = jnp.cumsum(nbt, axis=1) - nbt
    gs = TILE_BLOCKS * tile_start[:, None] + cb_excl
    seg_blk = jnp.cumsum(nb, axis=1) - nb
    base_blk = jnp.arange(nch, dtype=I32)[None, :] * CHUNK_BLOCKS + seg_blk.T
    gs_f = gs.reshape(-1)
    slots = jnp.arange(n_tiles * TILE_BLOCKS, dtype=I32)
    le = gs_f[None, :] <= slots[:, None]
    blk = slots + _masked_prefix(le, (base_blk - gs).reshape(-1))
    valid = slots < _masked_prefix(le, (gs + nbt).reshape(-1))
    gblk = jnp.concatenate([jnp.where(valid, blk, 0).astype(I32),
                            jnp.zeros(((X_SLOTS - 1) * TILE_BLOCKS,), I32)])

    m = jnp.arange(CHUNK_BLOCKS, dtype=I32)
    le_c = seg_blk[:, None, :] <= m[None, :, None]
    shift = (gs.T - seg_blk)
    delta = shift - jnp.concatenate([jnp.zeros((nch, 1), I32), shift[:, :-1]], axis=1)
    slot_of = m[None, :] + jnp.sum(jnp.where(le_c, delta[:, None, :], 0), axis=2)
    used_blocks = jnp.sum(nb, axis=1)
    slot_of = jnp.where(m[None, :] < used_blocks[:, None], slot_of, 0).astype(I32)
    pad = Y_RING - 1
    slot_of = jnp.concatenate([slot_of.reshape(-1), jnp.zeros((pad * CHUNK_BLOCKS,), I32)])
    used_rows = jnp.concatenate([ROW_BLOCK * used_blocks, jnp.zeros((pad,), I32)]).astype(I32)
    return ((n_act.reshape(1).astype(I32), n_exp.reshape(1), first, eseq, gblk),
            (used_rows, slot_of))


def _gmm_kernel(nact_ref, nexp_ref, first_ref, eseq_ref, gblk_ref,
                xl_hbm, wu_hbm, wd_hbm, yt_hbm,
                xbuf, ybuf, zbuf, wu_st, wd_st, in_sem, out_sem, w_sem, z_sem):
    n_act = nact_ref[0]
    n_exp = nexp_ref[0]
    n_tiles = yt_hbm.shape[0] // TILE_BLOCKS
    prefetch = X_SLOTS - 1

    def gather(tile, b):
        s = tile % X_SLOTS
        blk = gblk_ref[tile * TILE_BLOCKS + b]
        return pltpu.make_async_copy(xl_hbm.at[blk], xbuf.at[s, b], in_sem.at[s])

    def write_back(tile):
        s = tile % Y_SLOTS
        return pltpu.make_async_copy(ybuf.at[s], yt_hbm.at[pl.ds(tile * TILE_BLOCKS, TILE_BLOCKS)],
                                     out_sem.at[s])

    def zero_fill(tile):
        return pltpu.make_async_copy(zbuf, yt_hbm.at[pl.ds(tile * TILE_BLOCKS, TILE_BLOCKS)], z_sem)

    def weights(q):
        e = eseq_ref[q]
        s = q % W_SLOTS
        return (pltpu.make_async_copy(wu_hbm.at[e], wu_st.at[s], w_sem.at[0, s]),
                pltpu.make_async_copy(wd_hbm.at[e], wd_st.at[s], w_sem.at[1, s]))

    def tile_blocks(fn):
        for b in range(TILE_BLOCKS):
            fn(b)

    for q0 in range(W_SLOTS - 1):
        @pl.when(q0 < n_exp)
        def _(q0=q0):
            for cp in weights(q0):
                cp.start()
    for t0 in range(prefetch):
        tile_blocks(lambda b, t0=t0: gather(t0, b).start())

    def body(i, q):
        tile_blocks(lambda b: gather(i, b).wait())

        @pl.when(i >= Y_SLOTS)
        def _():
            write_back(i - Y_SLOTS).wait()

        is_first = first_ref[i] == 1

        @pl.when(is_first)
        def _():
            for cp in weights(q):
                cp.wait()

            @pl.when(q + W_SLOTS - 1 < n_exp)
            def _():
                for cp in weights(q + W_SLOTS - 1):
                    cp.start()

        q = q + is_first.astype(I32)
        ws = (q - 1) % W_SLOTS
        xs = i % X_SLOTS
        ys = i % Y_SLOTS

        x = xbuf[xs].reshape(TILE_ROWS, XL_COLS)
        gate = jnp.sum(x[:, D_MODEL:].astype(F32), axis=1, keepdims=True)
        au = jnp.dot(x[:, :D_MODEL], wu_st[ws], preferred_element_type=F32)
        tile_blocks(lambda b: gather(i + prefetch, b).start())
        a = au[:, :EXPERT_FF]
        act = (a * _sigmoid(a)) * au[:, EXPERT_FF:]
        y = jnp.dot(act.astype(BF16), wd_st[ws], preferred_element_type=F32)
        ybuf[ys] = (gate * y).astype(BF16).reshape(TILE_BLOCKS, ROW_BLOCK, D_MODEL)
        write_back(i).start()

        @pl.when(n_act + i < n_tiles)
        def _():
            zero_fill(n_act + i).start()
        return q

    zbuf[...] = jnp.zeros_like(zbuf)
    lax.fori_loop(0, n_act, body, jnp.int32(0))

    def fill_rest(t, carry):
        zero_fill(t).start()
        return carry

    lax.fori_loop(jnp.minimum(2 * n_act, n_tiles), n_tiles, fill_rest, 0)

    def drain(t, carry):
        zero_fill(t).wait()
        return carry

    lax.fori_loop(n_act, n_tiles, drain, 0)
    for k in range(prefetch):
        tile_blocks(lambda b, k=k: gather(n_act + k, b).wait())
    for k in range(Y_SLOTS):
        @pl.when(n_act - 1 - k >= 0)
        def _(k=k):
            write_back(n_act - 1 - k).wait()


def _grouped_mlp(plan, xl, w_up, w_down, n_tiles):
    grid_spec = pltpu.PrefetchScalarGridSpec(
        num_scalar_prefetch=len(plan),
        grid=(1,),
        in_specs=[pl.BlockSpec(memory_space=pl.ANY)] * 3,
        out_specs=pl.BlockSpec(memory_space=pl.ANY),
        scratch_shapes=[pltpu.VMEM((X_SLOTS, TILE_BLOCKS, ROW_BLOCK, XL_COLS), BF16),
                        pltpu.VMEM((Y_SLOTS, TILE_BLOCKS, ROW_BLOCK, D_MODEL), BF16),
                        pltpu.VMEM((TILE_BLOCKS, ROW_BLOCK, D_MODEL), BF16),
                        pltpu.VMEM((W_SLOTS, D_MODEL, 2 * EXPERT_FF), BF16),
                        pltpu.VMEM((W_SLOTS, EXPERT_FF, D_MODEL), BF16),
                        pltpu.SemaphoreType.DMA((X_SLOTS,)),
                        pltpu.SemaphoreType.DMA((Y_SLOTS,)),
                        pltpu.SemaphoreType.DMA((2, W_SLOTS)),
                        pltpu.SemaphoreType.DMA(())])
    return pl.pallas_call(
        _gmm_kernel,
        grid_spec=grid_spec,
        out_shape=jax.ShapeDtypeStruct((n_tiles * TILE_BLOCKS, ROW_BLOCK, D_MODEL), BF16),
        compiler_params=_cparams(("arbitrary",)),
        name="moe_experts",
    )(*plan, xl, w_up, w_down)


def _combine_kernel(used_ref, slot_ref, yt_hbm, pos_ref, x1_ref, gt2_ref, fg_ref, o_ref, ybuf, sem):
    c = pl.program_id(0)
    n_chunks = pl.num_programs(0)
    used_rows = used_ref[c]

    def chunk_blocks(chunk, op):
        s = chunk % Y_RING

        def copy(b):
            slot = slot_ref[chunk * CHUNK_BLOCKS + b]
            return pltpu.make_async_copy(yt_hbm.at[slot], ybuf.at[s, b], sem.at[s])

        for b in range(COMMON_BLOCKS):
            op(copy(b))

        @pl.when(used_ref[chunk] > COMMON_ROWS)
        def _():
            for b in range(COMMON_BLOCKS, CHUNK_BLOCKS):
                op(copy(b))

    @pl.when(c == 0)
    def _():
        for k in range(Y_RING - 1):
            chunk_blocks(k, lambda cp: cp.start())

    chunk_blocks(c + Y_RING - 1, lambda cp: cp.start())
    chunk_blocks(c, lambda cp: cp.wait())

    def body(k_rows):
        pos = pos_ref[...]
        p1 = pos[:, 0:1].astype(I32)
        p2 = pos[:, 1:2].astype(I32)
        r = lax.broadcasted_iota(I32, (CHUNK, k_rows), 1)
        sel = jnp.where(r == p1, 1.0, jnp.where(r == p2, 1.0, 0.0)).astype(BF16)
        yl = ybuf[c % Y_RING, 0:k_rows // ROW_BLOCK].reshape(k_rows, D_MODEL)
        y = jnp.dot(sel, yl, preferred_element_type=F32)
        x2 = x1_ref[...] + gt2_ref[0] * y
        inv = lax.rsqrt(jnp.mean(x2 * x2, axis=-1, keepdims=True) + NORM_EPS)
        o_ref[...] = (x2 * inv) * fg_ref[...]

    pl.when(used_rows <= COMMON_ROWS)(lambda: body(COMMON_ROWS))
    pl.when(used_rows > COMMON_ROWS)(lambda: body(CHUNK_ROWS))

    @pl.when(c == n_chunks - 1)
    def _():
        for k in range(1, Y_RING):
            chunk_blocks(c + k, lambda cp: cp.wait())


def _combine(used_rows, slot_of, yt, pos_c, x1, mod3, final_g, seq):
    n = x1.shape[0]
    per_seq = seq // CHUNK
    grid_spec = pltpu.PrefetchScalarGridSpec(
        num_scalar_prefetch=2,
        grid=(n // CHUNK,),
        in_specs=[pl.BlockSpec(memory_space=pl.ANY),
                  pl.BlockSpec((CHUNK, LANES), lambda c, u, s: (c, 0)),
                  pl.BlockSpec((CHUNK, D_MODEL), lambda c, u, s: (c, 0)),
                  pl.BlockSpec((1, 1, D_MODEL), lambda c, u, s: (c // per_seq, 0, 5)),
                  pl.BlockSpec((1, D_MODEL), lambda c, u, s: (0, 0))],
        out_specs=pl.BlockSpec((CHUNK, D_MODEL), lambda c, u, s: (c, 0)),
        scratch_shapes=[pltpu.VMEM((Y_RING, CHUNK_BLOCKS, ROW_BLOCK, D_MODEL), BF16),
                        pltpu.SemaphoreType.DMA((Y_RING,))])
    return pl.pallas_call(
        _combine_kernel,
        grid_spec=grid_spec,
        out_shape=jax.ShapeDtypeStruct((n, D_MODEL), F32),
        compiler_params=_cparams(("arbitrary",)),
        name="moe_combine",
    )(used_rows, slot_of, yt, pos_c, x1, mod3, final_g)


def _rope_tables(seq):
    n_freq = HEAD_DIM // 4
    inv_freq = ROPE_BASE ** (-jnp.arange(n_freq, dtype=F32) / n_freq)
    rows = seq // GRID_W
    row = jnp.repeat(jnp.arange(rows, dtype=F32), GRID_W)
    col = jnp.tile(jnp.arange(GRID_W, dtype=F32), rows)
    ang = jnp.concatenate([row[:, None] * inv_freq, col[:, None] * inv_freq], axis=-1)
    cos, sin = jnp.cos(ang), jnp.sin(ang)
    return jnp.tile(cos, (1, 4)), jnp.concatenate([-sin, sin, -sin, sin], axis=1)


def _window_bias():
    key = np.arange(3 * BLOCK)[:, None]
    qry = np.arange(BLOCK)[None, :]
    valid = (key - qry >= 0) & (key - qry <= 2 * BLOCK)
    return jnp.asarray(np.where(valid, 0.0, NEG_INF), F32)


def kernel(x, c, ctx, c_ctx, w_ada, b_ada, norm1_g, w_in, w_conv, b_conv, w_a, w_b, sink, w_o,
           norm2_g, w_group, b_group, w_router, b_router, w_up, w_down, final_g):
    bsz, seq, _ = x.shape
    assert w_ada.shape[0] == 1 and seq % SEQ_TILE == 0

    cc = jnp.zeros((16, D_MODEL), F32).at[:bsz].set(c).at[bsz].set(c_ctx)
    mod = _modulation(cc, w_ada[0], b_ada[0][None, :])
    mod3 = mod[:bsz].reshape(bsz, 1, N_MOD * D_MODEL)
    csh1 = mod[bsz:bsz + 1, 0:D_MODEL]
    csc1 = mod[bsz:bsz + 1, D_MODEL:2 * D_MODEL]

    w = w_in[0]
    n1g = norm1_g[0][None, :]
    n2g = norm2_g[0][None, :]
    w_kv = w[:, OFF_K:OFF_GA].astype(BF16)
    w_a_cols = jnp.concatenate([w[:, OFF_CG:OFF_Q], w[:, OFF_K:OFF_GA]], axis=1).astype(BF16)
    w_b_cols = jnp.concatenate([w[:, OFF_B:OFF_CG], w[:, OFF_Q:OFF_K], w[:, OFF_GA:]],
                               axis=1).astype(BF16)
    cos_t, sin_t = _rope_tables(seq)

    k_ctx, vt_ctx = _context_kv(ctx, n1g, csh1, csc1, w_kv)
    uc, k_rot, v_t = _proj_a(x, n1g, mod3, w_a_cols, cos_t, sin_t)

    sink_row = jnp.repeat(sink[0].astype(F32), BLOCK)[None, :]
    pad_g = jnp.zeros((D_MODEL, GROUP_ROWS - N_GROUPS), F32)
    pad_e = jnp.zeros((D_MODEL, ROUTER_ROWS - GROUP_ROWS - N_EXPERTS), F32)
    wr_t = jnp.concatenate([w_group[0], pad_g, w_router[0], pad_e], axis=1).T.astype(BF16)
    br_col = jnp.concatenate([b_group[0], pad_g[0], b_router[0], pad_e[0]])[:, None]

    upper = jnp.asarray(np.triu(np.ones((CHUNK, CHUNK), np.float32), 1), BF16)
    lower = jnp.asarray(np.tril(np.ones((N_EXPERTS, N_EXPERTS), np.float32), -1), BF16)
    x1, xl, pos_c, nb, w_up_bf, w_down_bf = _mix(
        x, n1g, n2g, mod3, w_b_cols, cos_t, sin_t, uc, k_rot, v_t, k_ctx, vt_ctx,
        w_conv[0], b_conv[0][None, :], w_a[0].astype(BF16), w_b[0].astype(BF16),
        w_o[0].astype(BF16), sink_row, _window_bias(), wr_t, br_col, w_up[0], w_down[0],
        upper, lower)

    n = bsz * seq
    nch = n // CHUNK
    nb = nb[:, :, 0]
    expert_plan, combine_plan = _tile_plan(nb, nch)
    yt = _grouped_mlp(expert_plan, xl.reshape(-1, ROW_BLOCK, XL_COLS), w_up_bf, w_down_bf,
                      _max_tiles(nch))
    out = _combine(*combine_plan, yt, pos_c, x1.reshape(n, D_MODEL), mod3, final_g[None, :], seq)
    return out.reshape(bsz, seq, D_MODEL)
```

```python
import functools

import numpy as np
import jax
import jax.numpy as jnp
from jax import lax
from jax.experimental import pallas as pl
from jax.experimental.pallas import tpu as pltpu

F32 = jnp.float32
BF16 = jnp.bfloat16
I32 = jnp.int32

D_MODEL = 1024
GRID_W = 64
CONV_W = 512
N_HEADS = 8
N_KV_HEADS = 2
HEAD_DIM = 64
ATT_W = N_HEADS * HEAD_DIM
KV_W = N_KV_HEADS * HEAD_DIM
BLOCK = 128
ROPE_BASE = 10000.0
N_GROUPS = 4
EXPERTS_PER_GROUP = 8
N_EXPERTS = N_GROUPS * EXPERTS_PER_GROUP
EXPERT_FF = 256
N_MOD = 6
NORM_EPS = 1e-6
NEG_INF = -1e30
LOG2E = 1.4426950408889634
ONES_ROWS = 16

OFF_B, OFF_CG, OFF_XIN, OFF_Q, OFF_K, OFF_V, OFF_GA, OFF_GB = (
    0, 512, 1024, 1536, 2048, 2176, 2304, 3328)
IN_COLS = 4352

LANES = 128
SEQ_TILE = 512
BLOCKS_PER_TILE = SEQ_TILE // BLOCK
ROUTE_ROWS = 8
GROUP_ROWS = 8
ROUTER_ROWS = 48
VMEM_LIMIT = 56 * 1024 * 1024

CHUNK = SEQ_TILE
ROW_BLOCK = 16
CHUNK_ROWS = -(-(2 * CHUNK + N_EXPERTS * (ROW_BLOCK - 1)) // 256) * 256
CHUNK_BLOCKS = CHUNK_ROWS // ROW_BLOCK
GATE_COLS = LANES
XL_COLS = D_MODEL + GATE_COLS
COMMON_ROWS = 2 * CHUNK + 256
COMMON_BLOCKS = COMMON_ROWS // ROW_BLOCK
TILE_BLOCKS = 32
TILE_ROWS = TILE_BLOCKS * ROW_BLOCK
X_SLOTS = 6
Y_SLOTS = 4
W_SLOTS = 3
Y_RING = 3


def _cparams(sem):
    return pltpu.CompilerParams(dimension_semantics=sem, vmem_limit_bytes=VMEM_LIMIT)


def _rms_mod(x, g, shift, scale):
    inv = lax.rsqrt(jnp.mean(x * x, axis=-1, keepdims=True) + NORM_EPS)
    return (x * inv) * (g * (1.0 + scale)) + shift


def _mod_kernel(c_ref, w_ref, b_ref, o_ref):
    c = c_ref[...]
    a = (c * jax.nn.sigmoid(c)).astype(BF16)
    o_ref[...] = jnp.dot(a, w_ref[...].astype(BF16), preferred_element_type=F32) + b_ref[...]


def _modulation(cc, w_ada, b_ada):
    rows = cc.shape[0]
    cols = w_ada.shape[1]
    tile = 1024
    return pl.pallas_call(
        _mod_kernel,
        grid=(cols // tile,),
        in_specs=[pl.BlockSpec((rows, D_MODEL), lambda j: (0, 0)),
                  pl.BlockSpec((D_MODEL, tile), lambda j: (0, j)),
                  pl.BlockSpec((1, tile), lambda j: (0, j))],
        out_specs=pl.BlockSpec((rows, tile), lambda j: (0, j)),
        out_shape=jax.ShapeDtypeStruct((rows, cols), F32),
        compiler_params=_cparams(("arbitrary",)),
        name="adaln_mod",
    )(cc, w_ada, b_ada)


def _ctx_kernel(x_ref, g_ref, sh_ref, sc_ref, w_ref, k_ref, vt_ref):
    h = _rms_mod(x_ref[0], g_ref[...], sh_ref[...], sc_ref[...]).astype(BF16)
    kv = jnp.dot(h, w_ref[...], preferred_element_type=F32)
    k_ref[0] = kv[:, :KV_W].astype(BF16)
    vt_ref[0] = kv[:, KV_W:].T.astype(BF16)


def _context_kv(ctx, norm_g, csh, csc, w_kv):
    bsz, clen, _ = ctx.shape
    return pl.pallas_call(
        _ctx_kernel,
        grid=(bsz,),
        in_specs=[pl.BlockSpec((1, clen, D_MODEL), lambda b: (b, 0, 0)),
                  pl.BlockSpec((1, D_MODEL), lambda b: (0, 0)),
                  pl.BlockSpec((1, D_MODEL), lambda b: (0, 0)),
                  pl.BlockSpec((1, D_MODEL), lambda b: (0, 0)),
                  pl.BlockSpec((D_MODEL, 2 * KV_W), lambda b: (0, 0))],
        out_specs=[pl.BlockSpec((1, clen, KV_W), lambda b: (b, 0, 0)),
                   pl.BlockSpec((1, KV_W, clen), lambda b: (b, 0, 0))],
        out_shape=[jax.ShapeDtypeStruct((bsz, clen, KV_W), BF16),
                   jax.ShapeDtypeStruct((bsz, KV_W, clen), BF16)],
        compiler_params=_cparams(("arbitrary",)),
        name="context_kv",
    )(ctx, norm_g, csh, csc, w_kv)


def _rope(t, cos, sin_signed):
    lane = lax.broadcasted_iota(I32, (1, LANES), 1)
    first_half = (lane % HEAD_DIM) < (HEAD_DIM // 2)
    outs = []
    for j in range(t.shape[1] // LANES):
        tj = t[:, j * LANES:(j + 1) * LANES]
        partner = jnp.where(first_half,
                            pltpu.roll(tj, LANES - HEAD_DIM // 2, axis=1),
                            pltpu.roll(tj, HEAD_DIM // 2, axis=1))
        outs.append(tj * cos + partner * sin_signed)
    return outs[0] if len(outs) == 1 else jnp.concatenate(outs, axis=1)


def _proj_a_kernel(x_ref, g_ref, sh_ref, sc_ref, w_ref, cos_ref, sin_ref, uc_ref, k_ref, vt_ref):
    h = _rms_mod(x_ref[0], g_ref[...], sh_ref[0], sc_ref[0]).astype(BF16)
    u = jnp.dot(h, w_ref[...], preferred_element_type=F32)
    uc_ref[0] = u[:, :CONV_W] * u[:, CONV_W:2 * CONV_W]
    k = u[:, 2 * CONV_W:2 * CONV_W + KV_W]
    k_ref[0] = _rope(k, cos_ref[...], sin_ref[...]).astype(BF16)
    vt_ref[0] = u[:, 2 * CONV_W + KV_W:].T.astype(BF16)


def _proj_a(x, norm_g, mod3, w_a_cols, cos_t, sin_t):
    bsz, seq, _ = x.shape
    nt = seq // SEQ_TILE
    wcols = w_a_cols.shape[1]
    return pl.pallas_call(
        _proj_a_kernel,
        grid=(nt, bsz),
        in_specs=[pl.BlockSpec((1, SEQ_TILE, D_MODEL), lambda t, b: (b, t, 0)),
                  pl.BlockSpec((1, D_MODEL), lambda t, b: (0, 0)),
                  pl.BlockSpec((1, 1, D_MODEL), lambda t, b: (b, 0, 0)),
                  pl.BlockSpec((1, 1, D_MODEL), lambda t, b: (b, 0, 1)),
                  pl.BlockSpec((D_MODEL, wcols), lambda t, b: (0, 0)),
                  pl.BlockSpec((SEQ_TILE, LANES), lambda t, b: (t, 0)),
                  pl.BlockSpec((SEQ_TILE, LANES), lambda t, b: (t, 0))],
        out_specs=[pl.BlockSpec((1, SEQ_TILE, CONV_W), lambda t, b: (b, t, 0)),
                   pl.BlockSpec((1, SEQ_TILE, KV_W), lambda t, b: (b, t, 0)),
                   pl.BlockSpec((1, KV_W, SEQ_TILE), lambda t, b: (b, 0, t))],
        out_shape=[jax.ShapeDtypeStruct((bsz, seq, CONV_W), F32),
                   jax.ShapeDtypeStruct((bsz, seq, KV_W), BF16),
                   jax.ShapeDtypeStruct((bsz, KV_W, seq), BF16)],
        compiler_params=_cparams(("arbitrary", "arbitrary")),
        name="proj_a",
    )(x, norm_g, mod3, mod3, w_a_cols, cos_t, sin_t)


def _sigmoid(x):
    return 1.0 / (1.0 + jnp.exp2(x * (-LOG2E)))


def _attn_scores(q_t, g, k_win, k_ctx, bias_p4, bias_n4):
    zeros = jnp.zeros((HEAD_DIM, 4 * BLOCK), BF16)
    qg = jnp.concatenate([q_t[(4 * g + h) * HEAD_DIM:(4 * g + h + 1) * HEAD_DIM, :]
                          for h in range(4)], axis=1)
    rhs = jnp.concatenate([qg, zeros] if g == 0 else [zeros, qg], axis=0)
    s_win = jnp.dot(k_win, rhs, preferred_element_type=F32)
    s_ctx = jnp.dot(k_ctx, rhs, preferred_element_type=F32)
    return (s_win[0:BLOCK] + bias_p4, s_win[BLOCK:2 * BLOCK], s_win[2 * BLOCK:] + bias_n4, s_ctx)


def _attn_probs(scores, sink):
    m = sink
    for s in scores:
        m = jnp.maximum(m, jnp.max(s, axis=0, keepdims=True))
    p_win = jnp.concatenate([jnp.exp2(s - m).astype(BF16) for s in scores[:3]], axis=0)
    p_ctx = jnp.exp2(scores[3] - m).astype(BF16)
    return p_win, p_ctx, m


def _attn_values(probs, sink, vt_win_g, vt_ctx_g):
    p_win, p_ctx, m = probs

    def with_ones(vt):
        r = lax.broadcasted_iota(I32, (ONES_ROWS, vt.shape[1]), 0)
        return jnp.concatenate([vt, jnp.where(r == 0, 1.0, 0.0).astype(BF16)], axis=0)

    o_ext = (jnp.dot(with_ones(vt_win_g), p_win, preferred_element_type=F32)
             + jnp.dot(with_ones(vt_ctx_g), p_ctx, preferred_element_type=F32))
    denom = o_ext[HEAD_DIM:HEAD_DIM + 1, :] + jnp.exp2(sink - m)
    return o_ext[:HEAD_DIM, :] / denom


def _route(logits_t):
    t = logits_t.shape[1]
    grow = lax.broadcasted_iota(I32, (GROUP_ROWS, t), 0)
    gl = jnp.where(grow < N_GROUPS, logits_t[0:GROUP_ROWS, :], NEG_INF)
    gm = jnp.max(gl, axis=0, keepdims=True)
    p_g = 1.0 / jnp.sum(jnp.exp(gl - gm), axis=0, keepdims=True)
    g_idx = jnp.min(jnp.where(gl == gm, grow, N_GROUPS), axis=0, keepdims=True)

    erow = lax.broadcasted_iota(I32, (N_EXPERTS, t), 0)
    el = logits_t[GROUP_ROWS:GROUP_ROWS + N_EXPERTS, :]
    sel = (erow // EXPERTS_PER_GROUP) == g_idx
    em = jnp.where(sel, el, NEG_INF)
    m1 = jnp.max(em, axis=0, keepdims=True)
    i1 = jnp.min(jnp.where(em == m1, erow, N_EXPERTS), axis=0, keepdims=True)
    em2 = jnp.where(erow == i1, NEG_INF, em)
    m2 = jnp.max(em2, axis=0, keepdims=True)
    i2 = jnp.min(jnp.where(em2 == m2, erow, N_EXPERTS), axis=0, keepdims=True)
    z = jnp.sum(jnp.where(sel, jnp.exp(el - m1), 0.0), axis=0, keepdims=True)
    p1 = 1.0 / z
    p2 = jnp.exp(m2 - m1) / z
    gate1 = p_g * p1 / (p1 + p2)
    gate2 = p_g * p2 / (p1 + p2)
    pad = jnp.zeros((ROUTE_ROWS - 4, t), F32)
    return jnp.concatenate([i1.astype(F32), i2.astype(F32), gate1, gate2, pad], axis=0)


def _mix_kernel(x_ref, g1n_ref, g2n_ref, sh1_ref, sc1_ref, gt1_ref, sh2_ref, sc2_ref,
                wb_ref, cos_ref, sin_ref,
                uc_ref, ucp_ref, ucn_ref, k_ref, kp_ref, kn_ref, vt_ref, vtp_ref, vtn_ref,
                kc_ref, vtc_ref, wconv_ref, bconv_ref, wa_ref, wbb_ref, wo_ref,
                sink_ref, bias_ref, wr_ref, br_ref, wu_ref, wd_ref, upper_ref, lower_ref,
                x1_ref, xl_ref, pos_ref, nb_ref, wub_ref, wdb_ref, h2_scr, route_scr,
                *, n_seq_tiles, batch):
    step = pl.program_id(0)
    t_idx = jnp.minimum(step, n_seq_tiles * batch - 1) // batch
    is_first = t_idx == 0
    is_last = t_idx == n_seq_tiles - 1

    @pl.when(step == 0)
    def _():
        h2_scr[...] = jnp.zeros_like(h2_scr)
        route_scr[...] = jnp.zeros_like(route_scr)

    permute, permute_tail = _sort_chunk(h2_scr[...], route_scr[...], upper_ref, lower_ref,
                                        xl_ref, pos_ref, nb_ref)
    x = x_ref[0]

    uc = uc_ref[0]
    row = lax.broadcasted_iota(I32, (SEQ_TILE, 1), 0)
    prev_row = jnp.where(is_first, 0.0, ucp_ref[0][7:8, :])
    next_row = jnp.where(is_last, 0.0, ucn_ref[0][0:1, :])
    up = jnp.where(row == 0, prev_row, pltpu.roll(uc, 1, axis=0))
    dn = jnp.where(row == SEQ_TILE - 1, next_row, pltpu.roll(uc, SEQ_TILE - 1, axis=0))
    wconv = wconv_ref[...]
    y = bconv_ref[...] + (up * wconv[0:1, :] + uc * wconv[1:2, :] + dn * wconv[2:3, :])

    hb = _rms_mod(x, g1n_ref[...], sh1_ref[0], sc1_ref[0]).astype(BF16)
    bq = jnp.dot(hb, wb_ref[:, 0:2 * CONV_W], preferred_element_type=F32)
    ya = jnp.dot((bq[:, :CONV_W] * y).astype(BF16), wa_ref[...], preferred_element_type=F32)
    q = _rope(bq[:, CONV_W:], cos_ref[...], sin_ref[...]) * (HEAD_DIM ** -0.5 * LOG2E)

    k_all = jnp.concatenate([kp_ref[0], k_ref[0], kn_ref[0]], axis=0)
    vt_all = jnp.concatenate([vtp_ref[0], vt_ref[0], vtn_ref[0]], axis=1)
    k_ctx = kc_ref[0]
    vt_ctx = vtc_ref[0]
    bias_prev = bias_ref[0:BLOCK, :]
    bias_next = bias_ref[2 * BLOCK:3 * BLOCK, :]
    sink_row = sink_ref[...] * LOG2E
    n_units = BLOCKS_PER_TILE * N_KV_HEADS
    gate_cols = 2 * D_MODEL // n_units
    gate_chunks = []

    def gate_chunk(u):
        c0 = 2 * CONV_W + u * gate_cols
        gate_chunks.append(jnp.dot(hb, wb_ref[:, c0:c0 + gate_cols], preferred_element_type=F32))

    q_ts, biases = [], []
    for jb in range(BLOCKS_PER_TILE):
        bias_p, bias_n = bias_prev, bias_next
        if jb == 0:
            bias_p = bias_prev + jnp.where(is_first, NEG_INF, 0.0)
        if jb == BLOCKS_PER_TILE - 1:
            bias_n = bias_next + jnp.where(is_last, NEG_INF, 0.0)
        biases.append((jnp.concatenate([bias_p] * 4, axis=1), jnp.concatenate([bias_n] * 4, axis=1)))
        q_ts.append(q[jb * BLOCK:(jb + 1) * BLOCK, :].T.astype(BF16))

    scores, probs, outs = {}, {}, {}
    for t in range(n_units + 2):
        if t == 3:
            wub_ref[...] = wu_ref[...].astype(BF16)
        if t == 5:
            wdb_ref[...] = wd_ref[...].astype(BF16)
        if t == 0:
            permute(0, COMMON_ROWS // 2)
        if t == 4:
            permute(COMMON_ROWS // 2, COMMON_ROWS // 2)
        if t < n_units:
            jb, g = divmod(t, N_KV_HEADS)
            scores[t] = _attn_scores(q_ts[jb], g, k_all[jb * BLOCK:(jb + 3) * BLOCK, :], k_ctx,
                                     *biases[jb])
            gate_chunk(t)
        if 0 <= t - 1 < n_units:
            g = (t - 1) % N_KV_HEADS
            probs[t - 1] = _attn_probs(scores.pop(t - 1),
                                       sink_row[:, g * 4 * BLOCK:(g + 1) * 4 * BLOCK])
        if 0 <= t - 2 < n_units:
            jb, g = divmod(t - 2, N_KV_HEADS)
            outs[t - 2] = _attn_values(
                probs.pop(t - 2), sink_row[:, g * 4 * BLOCK:(g + 1) * 4 * BLOCK],
                vt_all[g * HEAD_DIM:(g + 1) * HEAD_DIM, jb * BLOCK:(jb + 3) * BLOCK],
                vt_ctx[g * HEAD_DIM:(g + 1) * HEAD_DIM, :])
    o_blocks = []
    for jb in range(BLOCKS_PER_TILE):
        o_rows = [outs[jb * N_KV_HEADS + g][:, h * BLOCK:(h + 1) * BLOCK]
                  for g in range(N_KV_HEADS) for h in range(4)]
        o_blocks.append(jnp.concatenate(o_rows, axis=0).T)
    gates = jnp.concatenate(gate_chunks, axis=1)

    o = jnp.concatenate(o_blocks, axis=0).astype(BF16)
    yb = jnp.dot(o, wbb_ref[...], preferred_element_type=F32)
    merged = (_sigmoid(gates[:, :D_MODEL]) * ya + _sigmoid(gates[:, D_MODEL:]) * yb).astype(BF16)
    x1 = x + gt1_ref[0] * jnp.dot(merged, wo_ref[...], preferred_element_type=F32)
    x1_ref[0] = x1
    h2 = _rms_mod(x1, g2n_ref[...], sh2_ref[0], sc2_ref[0]).astype(BF16)
    logits_t = lax.dot_general(wr_ref[...], h2, (((1,), (1,)), ((), ())),
                               preferred_element_type=F32) + br_ref[...]
    permute_tail()
    h2_scr[...] = h2
    route_scr[...] = _route(logits_t)


def _mix(x, norm1_g, norm2_g, mod3, w_b_cols, cos_t, sin_t, uc, k_rot, v_t, k_ctx, vt_ctx,
         w_conv, b_conv, w_a, w_b, w_o, sink_row, bias, wr_t, br_col, w_up, w_down, upper, lower):
    bsz, seq, _ = x.shape
    nt = seq // SEQ_TILE
    nblk = seq // BLOCK
    clen = k_ctx.shape[1]
    rows8 = SEQ_TILE // 8
    assert N_EXPERTS % (nt * bsz) == 0, "each grid step converts an equal share of the experts"
    e_step = N_EXPERTS // (nt * bsz)

    def mod_spec(j):
        return pl.BlockSpec((1, 1, D_MODEL), lambda t, b, j=j: (b, 0, j))

    def const_spec(shape):
        return pl.BlockSpec(shape, lambda t, b: tuple(0 for _ in shape))

    in_specs = [
        pl.BlockSpec((1, SEQ_TILE, D_MODEL), lambda t, b: (b, t, 0)),
        const_spec((1, D_MODEL)), const_spec((1, D_MODEL)),
        mod_spec(0), mod_spec(1), mod_spec(2), mod_spec(3), mod_spec(4),
        const_spec(w_b_cols.shape),
        pl.BlockSpec((SEQ_TILE, LANES), lambda t, b: (t, 0)),
        pl.BlockSpec((SEQ_TILE, LANES), lambda t, b: (t, 0)),
        pl.BlockSpec((1, SEQ_TILE, CONV_W), lambda t, b: (b, t, 0)),
        pl.BlockSpec((1, 8, CONV_W), lambda t, b: (b, jnp.maximum(t * rows8 - 1, 0), 0)),
        pl.BlockSpec((1, 8, CONV_W),
                     lambda t, b: (b, jnp.minimum((t + 1) * rows8, seq // 8 - 1), 0)),
        pl.BlockSpec((1, SEQ_TILE, KV_W), lambda t, b: (b, t, 0)),
        pl.BlockSpec((1, BLOCK, KV_W),
                     lambda t, b: (b, jnp.maximum(t * BLOCKS_PER_TILE - 1, 0), 0)),
        pl.BlockSpec((1, BLOCK, KV_W),
                     lambda t, b: (b, jnp.minimum((t + 1) * BLOCKS_PER_TILE, nblk - 1), 0)),
        pl.BlockSpec((1, KV_W, SEQ_TILE), lambda t, b: (b, 0, t)),
        pl.BlockSpec((1, KV_W, BLOCK),
                     lambda t, b: (b, 0, jnp.maximum(t * BLOCKS_PER_TILE - 1, 0))),
        pl.BlockSpec((1, KV_W, BLOCK),
                     lambda t, b: (b, 0, jnp.minimum((t + 1) * BLOCKS_PER_TILE, nblk - 1))),
        pl.BlockSpec((1, clen, KV_W), lambda t, b: (b, 0, 0)),
        pl.BlockSpec((1, KV_W, clen), lambda t, b: (b, 0, 0)),
        const_spec(w_conv.shape), const_spec(b_conv.shape),
        const_spec(w_a.shape), const_spec(w_b.shape), const_spec(w_o.shape),
        const_spec(sink_row.shape), const_spec(bias.shape),
        const_spec(wr_t.shape), const_spec(br_col.shape),
        pl.BlockSpec((e_step,) + w_up.shape[1:], lambda t, b: (t * bsz + b, 0, 0)),
        pl.BlockSpec((e_step,) + w_down.shape[1:], lambda t, b: (t * bsz + b, 0, 0)),
        const_spec(upper.shape), const_spec(lower.shape),
    ]
    nch = bsz * nt
    out_specs = [
        pl.BlockSpec((1, SEQ_TILE, D_MODEL), lambda t, b: (b, t, 0)),
        pl.BlockSpec((CHUNK_ROWS, XL_COLS), lambda t, b: (b * nt + t, 0)),
        pl.BlockSpec((CHUNK, LANES), lambda t, b: (b * nt + t, 0)),
        pl.BlockSpec((1, N_EXPERTS, LANES), lambda t, b: (b * nt + t, 0, 0)),
        pl.BlockSpec((e_step,) + w_up.shape[1:], lambda t, b: (t * bsz + b, 0, 0)),
        pl.BlockSpec((e_step,) + w_down.shape[1:], lambda t, b: (t * bsz + b, 0, 0)),
    ]
    out_shape = [
        jax.ShapeDtypeStruct((bsz, seq, D_MODEL), F32),
        jax.ShapeDtypeStruct((nch * CHUNK_ROWS, XL_COLS), BF16),
        jax.ShapeDtypeStruct((nch * CHUNK, LANES), F32),
        jax.ShapeDtypeStruct((nch, N_EXPERTS, LANES), I32),
        jax.ShapeDtypeStruct(w_up.shape, BF16),
        jax.ShapeDtypeStruct(w_down.shape, BF16),
    ]
    n_main = nt * bsz

    def tile_of(s):
        s = jnp.minimum(s, n_main - 1)
        return s // bsz, s % bsz

    def this_tile(spec):
        return pl.BlockSpec(spec.block_shape, lambda s, f=spec.index_map: f(*tile_of(s)))

    def prev_tile(spec):
        return pl.BlockSpec(spec.block_shape,
                            lambda s, f=spec.index_map: f(*tile_of(jnp.maximum(s - 1, 0))))

    in_specs = [this_tile(sp) for sp in in_specs]
    out_specs = [this_tile(out_specs[0])] + [prev_tile(sp) for sp in out_specs[1:4]] + [
        this_tile(sp) for sp in out_specs[4:]]
    return pl.pallas_call(
        functools.partial(_mix_kernel, n_seq_tiles=nt, batch=bsz),
        grid=(n_main + 1,),
        in_specs=in_specs,
        out_specs=out_specs,
        out_shape=out_shape,
        scratch_shapes=[pltpu.VMEM((CHUNK, D_MODEL), BF16), pltpu.VMEM((ROUTE_ROWS, CHUNK), F32)],
        compiler_params=_cparams(("arbitrary",)),
        name="token_mix",
    )(x, norm1_g, norm2_g, mod3, mod3, mod3, mod3, mod3, w_b_cols, cos_t, sin_t,
      uc, uc, uc, k_rot, k_rot, k_rot, v_t, v_t, v_t, k_ctx, vt_ctx,
      w_conv, b_conv, w_a, w_b, w_o, sink_row, bias, wr_t, br_col, w_up, w_down, upper, lower)


def _bf16_parts(v):
    hi = v.astype(BF16).astype(F32)
    r1 = v - hi
    mid = r1.astype(BF16).astype(F32)
    lo = (r1 - mid).astype(BF16).astype(F32)
    return hi, mid, lo


def _sort_chunk(h, route, upper_ref, lower_ref, xl_ref, pos_ref, nb_ref):
    e1 = route[0:1, :].astype(I32)
    e2 = route[1:2, :].astype(I32)
    erow = lax.broadcasted_iota(I32, (N_EXPERTS, CHUNK), 0)
    hit1 = erow == e1
    hit2 = erow == e2
    onehot = jnp.where(hit1, 1.0, 0.0) + jnp.where(hit2, 1.0, 0.0)
    cum = jnp.dot(onehot.astype(BF16), upper_ref[...], preferred_element_type=F32)
    cnt = jnp.sum(onehot, axis=1, keepdims=True)
    nblk = jnp.floor((cnt + (ROW_BLOCK - 1)) * (1.0 / ROW_BLOCK))
    nblk_b = jnp.broadcast_to(nblk, (N_EXPERTS, LANES))
    seg = jnp.dot(lower_ref[...], nblk_b.astype(BF16), preferred_element_type=F32) * ROW_BLOCK
    base = seg[:, 0:1] + cum
    pos1 = jnp.sum(jnp.where(hit1, base, 0.0), axis=0, keepdims=True)
    pos2 = jnp.sum(jnp.where(hit2, base, 0.0), axis=0, keepdims=True)
    p1i = pos1.astype(I32)
    p2i = pos2.astype(I32)
    used_rows = (jnp.sum(nblk) * ROW_BLOCK).astype(I32)

    prow = lax.broadcasted_iota(I32, (LANES, CHUNK), 0)
    parts = _bf16_parts(route[2:3, :]) + _bf16_parts(route[3:4, :])
    gpart_rows = jnp.zeros((LANES, CHUNK), F32)
    for j, part in enumerate(parts):
        gpart_rows = jnp.where(prow == j, part, gpart_rows)
    h_ext = jnp.concatenate([h, gpart_rows.T.astype(BF16)], axis=1)
    glane = lax.broadcasted_iota(I32, (1, GATE_COLS), 1)
    one = jnp.ones((), BF16)
    zero = jnp.zeros((), BF16)
    def permute(row0, n_rows):
        r16 = lax.broadcasted_iota(I32, (n_rows, CHUNK), 0).astype(jnp.int16)
        q1 = (p1i - row0).astype(jnp.int16)
        q2 = (p2i - row0).astype(jnp.int16)
        sel1 = jnp.where(r16 == q1, one, zero)
        sel = jnp.where(r16 == q2, one, sel1)
        is_slot1 = jnp.max(sel1, axis=1, keepdims=True)
        rows = pl.ds(row0, n_rows)
        xg = jnp.dot(sel, h_ext, preferred_element_type=F32)
        xl_ref[rows, 0:D_MODEL] = xg[:, :D_MODEL].astype(BF16)
        g6 = xg[:, D_MODEL:]
        g3 = jnp.where(is_slot1.astype(F32) > 0.0, g6, pltpu.roll(g6, GATE_COLS - 3, axis=1))
        xl_ref[rows, D_MODEL:XL_COLS] = jnp.where(glane < 3, g3, 0.0).astype(BF16)

    pos_rows = jnp.where(prow == 0, pos1, jnp.where(prow == 1, pos2, 0.0))
    pos_ref[...] = pos_rows.T
    nb_ref[0] = nblk_b.astype(I32)

    def permute_tail():
        tail_rows = CHUNK_ROWS - COMMON_ROWS

        @pl.when(COMMON_ROWS < used_rows)
        def _():
            permute(COMMON_ROWS, tail_rows)

        @pl.when(COMMON_ROWS >= used_rows)
        def _():
            xl_ref[pl.ds(COMMON_ROWS, tail_rows), :] = jnp.zeros((tail_rows, XL_COLS), BF16)

    return permute, permute_tail


def _max_tiles(nch):
    max_blocks = nch * (2 * CHUNK + N_EXPERTS * (ROW_BLOCK - 1)) // ROW_BLOCK
    return max_blocks // TILE_BLOCKS + N_EXPERTS


def _masked_prefix(le, values):
    delta = values - jnp.concatenate([jnp.zeros((1,), values.dtype), values[:-1]])
    return jnp.sum(jnp.where(le, delta[None, :], 0), axis=1)


def _tile_plan(nb, nch):
    n_tiles = _max_tiles(nch)
    nbt = nb.T
    nbe = jnp.sum(nbt, axis=1)
    nte = (nbe + TILE_BLOCKS - 1) // TILE_BLOCKS
    tile_end = jnp.cumsum(nte)
    tile_start = tile_end - nte
    n_act = tile_end[-1]
    tiles = jnp.arange(n_tiles, dtype=I32)
    te = jnp.sum((tile_end[None, :] <= tiles[:, None]).astype(I32), axis=1)
    e_ar = jnp.arange(N_EXPERTS, dtype=I32)
    active = nte > 0
    te = jnp.where(tiles < n_act, te, jnp.max(jnp.where(active, e_ar, 0)))
    first = jnp.logical_and(te != jnp.concatenate([jnp.full((1,), -1, I32), te[:-1]]),
                            tiles < n_act).astype(I32)
    rank = jnp.cumsum(active.astype(I32)) - 1
    k_ar = jnp.arange(N_EXPERTS + W_SLOTS, dtype=I32)
    eseq = jnp.sum(jnp.where(jnp.logical_and(active[None, :], rank[None, :] == k_ar[:, None]),
                             e_ar[None, :], 0), axis=1)
    n_exp = jnp.sum(active.astype(I32))

    cb_excl = jnp.cumsum(nbt, axis=1) - nbt
    gs = TILE_BLOCKS * tile_start[:, None] + cb_excl
    seg_blk = jnp.cumsum(nb, axis=1) - nb
    base_blk = jnp.arange(nch, dtype=I32)[None, :] * CHUNK_BLOCKS + seg_blk.T
    gs_f = gs.reshape(-1)
    slots = jnp.arange(n_tiles * TILE_BLOCKS, dtype=I32)
    le = gs_f[None, :] <= slots[:, None]
    blk = slots + _masked_prefix(le, (base_blk - gs).reshape(-1))
    valid = slots < _masked_prefix(le, (gs + nbt).reshape(-1))
    gblk = jnp.concatenate([jnp.where(valid, blk, 0).astype(I32),
                            jnp.zeros(((X_SLOTS - 1) * TILE_BLOCKS,), I32)])

    m = jnp.arange(CHUNK_BLOCKS, dtype=I32)
    le_c = seg_blk[:, None, :] <= m[None, :, None]
    shift = (gs.T - seg_blk)
    delta = shift - jnp.concatenate([jnp.zeros((nch, 1), I32), shift[:, :-1]], axis=1)
    slot_of = m[None, :] + jnp.sum(jnp.where(le_c, delta[:, None, :], 0), axis=2)
    used_blocks = jnp.sum(nb, axis=1)
    slot_of = jnp.where(m[None, :] < used_blocks[:, None], slot_of, 0).astype(I32)
    pad = Y_RING - 1
    slot_of = jnp.concatenate([slot_of.reshape(-1), jnp.zeros((pad * CHUNK_BLOCKS,), I32)])
    used_rows = jnp.concatenate([ROW_BLOCK * used_blocks, jnp.zeros((pad,), I32)]).astype(I32)
    return ((n_act.reshape(1).astype(I32), n_exp.reshape(1), first, eseq, gblk),
            (used_rows, slot_of))


def _gmm_kernel(nact_ref, nexp_ref, first_ref, eseq_ref, gblk_ref,
                xl_hbm, wu_hbm, wd_hbm, yt_hbm,
                xbuf, ybuf, zbuf, wu_st, wd_st, in_sem, out_sem, w_sem, z_sem):
    n_act = nact_ref[0]
    n_exp = nexp_ref[0]
    n_tiles = yt_hbm.shape[0] // TILE_BLOCKS
    prefetch = X_SLOTS - 1

    def gather(tile, b):
        s = tile % X_SLOTS
        blk = gblk_ref[tile * TILE_BLOCKS + b]
        return pltpu.make_async_copy(xl_hbm.at[blk], xbuf.at[s, b], in_sem.at[s])

    def write_back(tile):
        s = tile % Y_SLOTS
        return pltpu.make_async_copy(ybuf.at[s], yt_hbm.at[pl.ds(tile * TILE_BLOCKS, TILE_BLOCKS)],
                                     out_sem.at[s])

    def zero_fill(tile):
        return pltpu.make_async_copy(zbuf, yt_hbm.at[pl.ds(tile * TILE_BLOCKS, TILE_BLOCKS)], z_sem)

    def weights(q):
        e = eseq_ref[q]
        s = q % W_SLOTS
        return (pltpu.make_async_copy(wu_hbm.at[e], wu_st.at[s], w_sem.at[0, s]),
                pltpu.make_async_copy(wd_hbm.at[e], wd_st.at[s], w_sem.at[1, s]))

    def tile_blocks(fn):
        for b in range(TILE_BLOCKS):
            fn(b)

    for q0 in range(W_SLOTS - 1):
        @pl.when(q0 < n_exp)
        def _(q0=q0):
            for cp in weights(q0):
                cp.start()
    for t0 in range(prefetch):
        tile_blocks(lambda b, t0=t0: gather(t0, b).start())

    def body(i, q):
        tile_blocks(lambda b: gather(i, b).wait())

        @pl.when(i >= Y_SLOTS)
        def _():
            write_back(i - Y_SLOTS).wait()

        is_first = first_ref[i] == 1

        @pl.when(is_first)
        def _():
            for cp in weights(q):
                cp.wait()

            @pl.when(q + W_SLOTS - 1 < n_exp)
            def _():
                for cp in weights(q + W_SLOTS - 1):
                    cp.start()

        q = q + is_first.astype(I32)
        ws = (q - 1) % W_SLOTS
        xs = i % X_SLOTS
        ys = i % Y_SLOTS

        x = xbuf[xs].reshape(TILE_ROWS, XL_COLS)
        gate = jnp.sum(x[:, D_MODEL:].astype(F32), axis=1, keepdims=True)
        au = jnp.dot(x[:, :D_MODEL], wu_st[ws], preferred_element_type=F32)
        tile_blocks(lambda b: gather(i + prefetch, b).start())
        a = au[:, :EXPERT_FF]
        act = (a * _sigmoid(a)) * au[:, EXPERT_FF:]
        y = jnp.dot(act.astype(BF16), wd_st[ws], preferred_element_type=F32)
        ybuf[ys] = (gate * y).astype(BF16).reshape(TILE_BLOCKS, ROW_BLOCK, D_MODEL)
        write_back(i).start()

        @pl.when(n_act + i < n_tiles)
        def _():
            zero_fill(n_act + i).start()
        return q

    zbuf[...] = jnp.zeros_like(zbuf)
    lax.fori_loop(0, n_act, body, jnp.int32(0))

    def fill_rest(t, carry):
        zero_fill(t).start()
        return carry

    lax.fori_loop(jnp.minimum(2 * n_act, n_tiles), n_tiles, fill_rest, 0)

    def drain(t, carry):
        zero_fill(t).wait()
        return carry

    lax.fori_loop(n_act, n_tiles, drain, 0)
    for k in range(prefetch):
        tile_blocks(lambda b, k=k: gather(n_act + k, b).wait())
    for k in range(Y_SLOTS):
        @pl.when(n_act - 1 - k >= 0)
        def _(k=k):
            write_back(n_act - 1 - k).wait()


def _grouped_mlp(plan, xl, w_up, w_down, n_tiles):
    grid_spec = pltpu.PrefetchScalarGridSpec(
        num_scalar_prefetch=len(plan),
        grid=(1,),
        in_specs=[pl.BlockSpec(memory_space=pl.ANY)] * 3,
        out_specs=pl.BlockSpec(memory_space=pl.ANY),
        scratch_shapes=[pltpu.VMEM((X_SLOTS, TILE_BLOCKS, ROW_BLOCK, XL_COLS), BF16),
                        pltpu.VMEM((Y_SLOTS, TILE_BLOCKS, ROW_BLOCK, D_MODEL), BF16),
                        pltpu.VMEM((TILE_BLOCKS, ROW_BLOCK, D_MODEL), BF16),
                        pltpu.VMEM((W_SLOTS, D_MODEL, 2 * EXPERT_FF), BF16),
                        pltpu.VMEM((W_SLOTS, EXPERT_FF, D_MODEL), BF16),
                        pltpu.SemaphoreType.DMA((X_SLOTS,)),
                        pltpu.SemaphoreType.DMA((Y_SLOTS,)),
                        pltpu.SemaphoreType.DMA((2, W_SLOTS)),
                        pltpu.SemaphoreType.DMA(())])
    return pl.pallas_call(
        _gmm_kernel,
        grid_spec=grid_spec,
        out_shape=jax.ShapeDtypeStruct((n_tiles * TILE_BLOCKS, ROW_BLOCK, D_MODEL), BF16),
        compiler_params=_cparams(("arbitrary",)),
        name="moe_experts",
    )(*plan, xl, w_up, w_down)


def _combine_kernel(used_ref, slot_ref, yt_hbm, pos_ref, x1_ref, gt2_ref, fg_ref, o_ref, ybuf, sem):
    c = pl.program_id(0)
    n_chunks = pl.num_programs(0)
    used_rows = used_ref[c]

    def chunk_blocks(chunk, op):
        s = chunk % Y_RING

        def copy(b):
            slot = slot_ref[chunk * CHUNK_BLOCKS + b]
            return pltpu.make_async_copy(yt_hbm.at[slot], ybuf.at[s, b], sem.at[s])

        for b in range(COMMON_BLOCKS):
            op(copy(b))

        @pl.when(used_ref[chunk] > COMMON_ROWS)
        def _():
            for b in range(COMMON_BLOCKS, CHUNK_BLOCKS):
                op(copy(b))

    @pl.when(c == 0)
    def _():
        for k in range(Y_RING - 1):
            chunk_blocks(k, lambda cp: cp.start())

    chunk_blocks(c + Y_RING - 1, lambda cp: cp.start())
    chunk_blocks(c, lambda cp: cp.wait())

    def body(k_rows):
        pos = pos_ref[...]
        p1 = pos[:, 0:1].astype(I32)
        p2 = pos[:, 1:2].astype(I32)
        r = lax.broadcasted_iota(I32, (CHUNK, k_rows), 1)
        sel = jnp.where(r == p1, 1.0, jnp.where(r == p2, 1.0, 0.0)).astype(BF16)
        yl = ybuf[c % Y_RING, 0:k_rows // ROW_BLOCK].reshape(k_rows, D_MODEL)
        y = jnp.dot(sel, yl, preferred_element_type=F32)
        x2 = x1_ref[...] + gt2_ref[0] * y
        inv = lax.rsqrt(jnp.mean(x2 * x2, axis=-1, keepdims=True) + NORM_EPS)
        o_ref[...] = (x2 * inv) * fg_ref[...]

    pl.when(used_rows <= COMMON_ROWS)(lambda: body(COMMON_ROWS))
    pl.when(used_rows > COMMON_ROWS)(lambda: body(CHUNK_ROWS))

    @pl.when(c == n_chunks - 1)
    def _():
        for k in range(1, Y_RING):
            chunk_blocks(c + k, lambda cp: cp.wait())


def _combine(used_rows, slot_of, yt, pos_c, x1, mod3, final_g, seq):
    n = x1.shape[0]
    per_seq = seq // CHUNK
    grid_spec = pltpu.PrefetchScalarGridSpec(
        num_scalar_prefetch=2,
        grid=(n // CHUNK,),
        in_specs=[pl.BlockSpec(memory_space=pl.ANY),
                  pl.BlockSpec((CHUNK, LANES), lambda c, u, s: (c, 0)),
                  pl.BlockSpec((CHUNK, D_MODEL), lambda c, u, s: (c, 0)),
                  pl.BlockSpec((1, 1, D_MODEL), lambda c, u, s: (c // per_seq, 0, 5)),
                  pl.BlockSpec((1, D_MODEL), lambda c, u, s: (0, 0))],
        out_specs=pl.BlockSpec((CHUNK, D_MODEL), lambda c, u, s: (c, 0)),
        scratch_shapes=[pltpu.VMEM((Y_RING, CHUNK_BLOCKS, ROW_BLOCK, D_MODEL), BF16),
                        pltpu.SemaphoreType.DMA((Y_RING,))])
    return pl.pallas_call(
        _combine_kernel,
        grid_spec=grid_spec,
        out_shape=jax.ShapeDtypeStruct((n, D_MODEL), F32),
        compiler_params=_cparams(("arbitrary",)),
        name="moe_combine",
    )(used_rows, slot_of, yt, pos_c, x1, mod3, final_g)


def _rope_tables(seq):
    n_freq = HEAD_DIM // 4
    inv_freq = ROPE_BASE ** (-jnp.arange(n_freq, dtype=F32) / n_freq)
    rows = seq // GRID_W
    row = jnp.repeat(jnp.arange(rows, dtype=F32), GRID_W)
    col = jnp.tile(jnp.arange(GRID_W, dtype=F32), rows)
    ang = jnp.concatenate([row[:, None] * inv_freq, col[:, None] * inv_freq], axis=-1)
    cos, sin = jnp.cos(ang), jnp.sin(ang)
    return jnp.tile(cos, (1, 4)), jnp.concatenate([-sin, sin, -sin, sin], axis=1)


def _window_bias():
    key = np.arange(3 * BLOCK)[:, None]
    qry = np.arange(BLOCK)[None, :]
    valid = (key - qry >= 0) & (key - qry <= 2 * BLOCK)
    return jnp.asarray(np.where(valid, 0.0, NEG_INF), F32)


def kernel(x, c, ctx, c_ctx, w_ada, b_ada, norm1_g, w_in, w_conv, b_conv, w_a, w_b, sink, w_o,
           norm2_g, w_group, b_group, w_router, b_router, w_up, w_down, final_g):
    bsz, seq, _ = x.shape
    assert w_ada.shape[0] == 1 and seq % SEQ_TILE == 0

    cc = jnp.zeros((16, D_MODEL), F32).at[:bsz].set(c).at[bsz].set(c_ctx)
    mod = _modulation(cc, w_ada[0], b_ada[0][None, :])
    mod3 = mod[:bsz].reshape(bsz, 1, N_MOD * D_MODEL)
    csh1 = mod[bsz:bsz + 1, 0:D_MODEL]
    csc1 = mod[bsz:bsz + 1, D_MODEL:2 * D_MODEL]

    w = w_in[0]
    n1g = norm1_g[0][None, :]
    n2g = norm2_g[0][None, :]
    w_kv = w[:, OFF_K:OFF_GA].astype(BF16)
    w_a_cols = jnp.concatenate([w[:, OFF_CG:OFF_Q], w[:, OFF_K:OFF_GA]], axis=1).astype(BF16)
    w_b_cols = jnp.concatenate([w[:, OFF_B:OFF_CG], w[:, OFF_Q:OFF_K], w[:, OFF_GA:]],
                               axis=1).astype(BF16)
    cos_t, sin_t = _rope_tables(seq)

    k_ctx, vt_ctx = _context_kv(ctx, n1g, csh1, csc1, w_kv)
    uc, k_rot, v_t = _proj_a(x, n1g, mod3, w_a_cols, cos_t, sin_t)

    sink_row = jnp.repeat(sink[0].astype(F32), BLOCK)[None, :]
    pad_g = jnp.zeros((D_MODEL, GROUP_ROWS - N_GROUPS), F32)
    pad_e = jnp.zeros((D_MODEL, ROUTER_ROWS - GROUP_ROWS - N_EXPERTS), F32)
    wr_t = jnp.concatenate([w_group[0], pad_g, w_router[0], pad_e], axis=1).T.astype(BF16)
    br_col = jnp.concatenate([b_group[0], pad_g[0], b_router[0], pad_e[0]])[:, None]

    upper = jnp.asarray(np.triu(np.ones((CHUNK, CHUNK), np.float32), 1), BF16)
    lower = jnp.asarray(np.tril(np.ones((N_EXPERTS, N_EXPERTS), np.float32), -1), BF16)
    x1, xl, pos_c, nb, w_up_bf, w_down_bf = _mix(
        x, n1g, n2g, mod3, w_b_cols, cos_t, sin_t, uc, k_rot, v_t, k_ctx, vt_ctx,
        w_conv[0], b_conv[0][None, :], w_a[0].astype(BF16), w_b[0].astype(BF16),
        w_o[0].astype(BF16), sink_row, _window_bias(), wr_t, br_col, w_up[0], w_down[0],
        upper, lower)

    n = bsz * seq
    nch = n // CHUNK
    nb = nb[:, :, 0]
    expert_plan, combine_plan = _tile_plan(nb, nch)
    yt = _grouped_mlp(expert_plan, xl.reshape(-1, ROW_BLOCK, XL_COLS), w_up_bf, w_down_bf,
                      _max_tiles(nch))
    out = _combine(*combine_plan, yt, pos_c, x1.reshape(n, D_MODEL), mod3, final_g[None, :], seq)
    return out.reshape(bsz, seq, D_MODEL)
```

```python
import functools

import numpy as np
import jax
import jax.numpy as jnp
from jax import lax
from jax.experimental import pallas as pl
from jax.experimental.pallas import tpu as pltpu

F32 = jnp.float32
BF16 = jnp.bfloat16
I32 = jnp.int32

D_MODEL = 1024
GRID_W = 64
CONV_W = 512
N_HEADS = 8
N_KV_HEADS = 2
HEAD_DIM = 64
ATT_W = N_HEADS * HEAD_DIM
KV_W = N_KV_HEADS * HEAD_DIM
BLOCK = 128
ROPE_BASE = 10000.0
N_GROUPS = 4
EXPERTS_PER_GROUP = 8
N_EXPERTS = N_GROUPS * EXPERTS_PER_GROUP
EXPERT_FF = 256
N_MOD = 6
NORM_EPS = 1e-6
NEG_INF = -1e30
LOG2E = 1.4426950408889634
ONES_ROWS = 16

OFF_B, OFF_CG, OFF_XIN, OFF_Q, OFF_K, OFF_V, OFF_GA, OFF_GB = (
    0, 512, 1024, 1536, 2048, 2176, 2304, 3328)
IN_COLS = 4352

LANES = 128
SEQ_TILE = 512
BLOCKS_PER_TILE = SEQ_TILE // BLOCK
ROUTE_ROWS = 8
GROUP_ROWS = 8
ROUTER_ROWS = 48
VMEM_LIMIT = 56 * 1024 * 1024

CHUNK = SEQ_TILE
ROW_BLOCK = 16
CHUNK_ROWS = -(-(2 * CHUNK + N_EXPERTS * (ROW_BLOCK - 1)) // 256) * 256
CHUNK_BLOCKS = CHUNK_ROWS // ROW_BLOCK
GATE_COLS = LANES
XL_COLS = D_MODEL + GATE_COLS
COMMON_ROWS = 2 * CHUNK + 256
COMMON_BLOCKS = COMMON_ROWS // ROW_BLOCK
TILE_BLOCKS = 32
TILE_ROWS = TILE_BLOCKS * ROW_BLOCK
X_SLOTS = 6
Y_SLOTS = 4
W_SLOTS = 3
Y_RING = 3


def _cparams(sem):
    return pltpu.CompilerParams(dimension_semantics=sem, vmem_limit_bytes=VMEM_LIMIT)


def _rms_mod(x, g, shift, scale):
    inv = lax.rsqrt(jnp.mean(x * x, axis=-1, keepdims=True) + NORM_EPS)
    return (x * inv) * (g * (1.0 + scale)) + shift


def _mod_kernel(c_ref, w_ref, b_ref, o_ref):
    c = c_ref[...]
    a = (c * jax.nn.sigmoid(c)).astype(BF16)
    o_ref[...] = jnp.dot(a, w_ref[...].astype(BF16), preferred_element_type=F32) + b_ref[...]


def _modulation(cc, w_ada, b_ada):
    rows = cc.shape[0]
    cols = w_ada.shape[1]
    tile = 1024
    return pl.pallas_call(
        _mod_kernel,
        grid=(cols // tile,),
        in_specs=[pl.BlockSpec((rows, D_MODEL), lambda j: (0, 0)),
                  pl.BlockSpec((D_MODEL, tile), lambda j: (0, j)),
                  pl.BlockSpec((1, tile), lambda j: (0, j))],
        out_specs=pl.BlockSpec((rows, tile), lambda j: (0, j)),
        out_shape=jax.ShapeDtypeStruct((rows, cols), F32),
        compiler_params=_cparams(("arbitrary",)),
        name="adaln_mod",
    )(cc, w_ada, b_ada)


def _ctx_kernel(x_ref, g_ref, sh_ref, sc_ref, w_ref, k_ref, vt_ref):
    h = _rms_mod(x_ref[0], g_ref[...], sh_ref[...], sc_ref[...]).astype(BF16)
    kv = jnp.dot(h, w_ref[...], preferred_element_type=F32)
    k_ref[0] = kv[:, :KV_W].astype(BF16)
    vt_ref[0] = kv[:, KV_W:].T.astype(BF16)


def _context_kv(ctx, norm_g, csh, csc, w_kv):
    bsz, clen, _ = ctx.shape
    return pl.pallas_call(
        _ctx_kernel,
        grid=(bsz,),
        in_specs=[pl.BlockSpec((1, clen, D_MODEL), lambda b: (b, 0, 0)),
                  pl.BlockSpec((1, D_MODEL), lambda b: (0, 0)),
                  pl.BlockSpec((1, D_MODEL), lambda b: (0, 0)),
                  pl.BlockSpec((1, D_MODEL), lambda b: (0, 0)),
                  pl.BlockSpec((D_MODEL, 2 * KV_W), lambda b: (0, 0))],
        out_specs=[pl.BlockSpec((1, clen, KV_W), lambda b: (b, 0, 0)),
                   pl.BlockSpec((1, KV_W, clen), lambda b: (b, 0, 0))],
        out_shape=[jax.ShapeDtypeStruct((bsz, clen, KV_W), BF16),
                   jax.ShapeDtypeStruct((bsz, KV_W, clen), BF16)],
        compiler_params=_cparams(("arbitrary",)),
        name="context_kv",
    )(ctx, norm_g, csh, csc, w_kv)


def _rope(t, cos, sin_signed):
    lane = lax.broadcasted_iota(I32, (1, LANES), 1)
    first_half = (lane % HEAD_DIM) < (HEAD_DIM // 2)
    outs = []
    for j in range(t.shape[1] // LANES):
        tj = t[:, j * LANES:(j + 1) * LANES]
        partner = jnp.where(first_half,
                            pltpu.roll(tj, LANES - HEAD_DIM // 2, axis=1),
                            pltpu.roll(tj, HEAD_DIM // 2, axis=1))
        outs.append(tj * cos + partner * sin_signed)
    return outs[0] if len(outs) == 1 else jnp.concatenate(outs, axis=1)


def _proj_a_kernel(x_ref, g_ref, sh_ref, sc_ref, w_ref, cos_ref, sin_ref, uc_ref, k_ref, vt_ref):
    h = _rms_mod(x_ref[0], g_ref[...], sh_ref[0], sc_ref[0]).astype(BF16)
    u = jnp.dot(h, w_ref[...], preferred_element_type=F32)
    uc_ref[0] = u[:, :CONV_W] * u[:, CONV_W:2 * CONV_W]
    k = u[:, 2 * CONV_W:2 * CONV_W + KV_W]
    k_ref[0] = _rope(k, cos_ref[...], sin_ref[...]).astype(BF16)
    vt_ref[0] = u[:, 2 * CONV_W + KV_W:].T.astype(BF16)


def _proj_a(x, norm_g, mod3, w_a_cols, cos_t, sin_t):
    bsz, seq, _ = x.shape
    nt = seq // SEQ_TILE
    wcols = w_a_cols.shape[1]
    return pl.pallas_call(
        _proj_a_kernel,
        grid=(nt, bsz),
        in_specs=[pl.BlockSpec((1, SEQ_TILE, D_MODEL), lambda t, b: (b, t, 0)),
                  pl.BlockSpec((1, D_MODEL), lambda t, b: (0, 0)),
                  pl.BlockSpec((1, 1, D_MODEL), lambda t, b: (b, 0, 0)),
                  pl.BlockSpec((1, 1, D_MODEL), lambda t, b: (b, 0, 1)),
                  pl.BlockSpec((D_MODEL, wcols), lambda t, b: (0, 0)),
                  pl.BlockSpec((SEQ_TILE, LANES), lambda t, b: (t, 0)),
                  pl.BlockSpec((SEQ_TILE, LANES), lambda t, b: (t, 0))],
        out_specs=[pl.BlockSpec((1, SEQ_TILE, CONV_W), lambda t, b: (b, t, 0)),
                   pl.BlockSpec((1, SEQ_TILE, KV_W), lambda t, b: (b, t, 0)),
                   pl.BlockSpec((1, KV_W, SEQ_TILE), lambda t, b: (b, 0, t))],
        out_shape=[jax.ShapeDtypeStruct((bsz, seq, CONV_W), F32),
                   jax.ShapeDtypeStruct((bsz, seq, KV_W), BF16),
                   jax.ShapeDtypeStruct((bsz, KV_W, seq), BF16)],
        compiler_params=_cparams(("arbitrary", "arbitrary")),
        name="proj_a",
    )(x, norm_g, mod3, mod3, w_a_cols, cos_t, sin_t)


def _sigmoid(x):
    return 1.0 / (1.0 + jnp.exp2(x * (-LOG2E)))


def _attn_scores(q_t, g, k_win, k_ctx, bias_p4, bias_n4):
    zeros = jnp.zeros((HEAD_DIM, 4 * BLOCK), BF16)
    qg = jnp.concatenate([q_t[(4 * g + h) * HEAD_DIM:(4 * g + h + 1) * HEAD_DIM, :]
                          for h in range(4)], axis=1)
    rhs = jnp.concatenate([qg, zeros] if g == 0 else [zeros, qg], axis=0)
    s_win = jnp.dot(k_win, rhs, preferred_element_type=F32)
    s_ctx = jnp.dot(k_ctx, rhs, preferred_element_type=F32)
    return (s_win[0:BLOCK] + bias_p4, s_win[BLOCK:2 * BLOCK], s_win[2 * BLOCK:] + bias_n4, s_ctx)


def _attn_probs(scores, sink):
    m = sink
    for s in scores:
        m = jnp.maximum(m, jnp.max(s, axis=0, keepdims=True))
    p_win = jnp.concatenate([jnp.exp2(s - m).astype(BF16) for s in scores[:3]], axis=0)
    p_ctx = jnp.exp2(scores[3] - m).astype(BF16)
    return p_win, p_ctx, m


def _attn_values(probs, sink, vt_win_g, vt_ctx_g):
    p_win, p_ctx, m = probs

    def with_ones(vt):
        r = lax.broadcasted_iota(I32, (ONES_ROWS, vt.shape[1]), 0)
        return jnp.concatenate([vt, jnp.where(r == 0, 1.0, 0.0).astype(BF16)], axis=0)

    o_ext = (jnp.dot(with_ones(vt_win_g), p_win, preferred_element_type=F32)
             + jnp.dot(with_ones(vt_ctx_g), p_ctx, preferred_element_type=F32))
    denom = o_ext[HEAD_DIM:HEAD_DIM + 1, :] + jnp.exp2(sink - m)
    return o_ext[:HEAD_DIM, :] / denom


def _route(logits_t):
    t = logits_t.shape[1]
    grow = lax.broadcasted_iota(I32, (GROUP_ROWS, t), 0)
    gl = jnp.where(grow < N_GROUPS, logits_t[0:GROUP_ROWS, :], NEG_INF)
    gm = jnp.max(gl, axis=0, keepdims=True)
    p_g = 1.0 / jnp.sum(jnp.exp(gl - gm), axis=0, keepdims=True)
    g_idx = jnp.min(jnp.where(gl == gm, grow, N_GROUPS), axis=0, keepdims=True)

    erow = lax.broadcasted_iota(I32, (N_EXPERTS, t), 0)
    el = logits_t[GROUP_ROWS:GROUP_ROWS + N_EXPERTS, :]
    sel = (erow // EXPERTS_PER_GROUP) == g_idx
    em = jnp.where(sel, el, NEG_INF)
    m1 = jnp.max(em, axis=0, keepdims=True)
    i1 = jnp.min(jnp.where(em == m1, erow, N_EXPERTS), axis=0, keepdims=True)
    em2 = jnp.where(erow == i1, NEG_INF, em)
    m2 = jnp.max(em2, axis=0, keepdims=True)
    i2 = jnp.min(jnp.where(em2 == m2, erow, N_EXPERTS), axis=0, keepdims=True)
    z = jnp.sum(jnp.where(sel, jnp.exp(el - m1), 0.0), axis=0, keepdims=True)
    p1 = 1.0 / z
    p2 = jnp.exp(m2 - m1) / z
    gate1 = p_g * p1 / (p1 + p2)
    gate2 = p_g * p2 / (p1 + p2)
    pad = jnp.zeros((ROUTE_ROWS - 4, t), F32)
    return jnp.concatenate([i1.astype(F32), i2.astype(F32), gate1, gate2, pad], axis=0)


def _mix_step(x_ref, g1n_ref, g2n_ref, sh1_ref, sc1_ref, gt1_ref, sh2_ref, sc2_ref,
                wb_ref, cos_ref, sin_ref,
                uc_ref, ucp_ref, ucn_ref, k_ref, kp_ref, kn_ref, vt_ref, vtp_ref, vtn_ref,
                kc_ref, vtc_ref, wconv_ref, bconv_ref, wa_ref, wbb_ref, wo_ref,
                sink_ref, bias_ref, wr_ref, br_ref, wu_ref, wd_ref, upper_ref, lower_ref,
                x1_ref, xl_ref, pos_ref, nb_ref, wub_ref, wdb_ref, h2_scr, route_scr,
                *, n_seq_tiles, batch):
    t_idx = pl.program_id(0) // batch
    is_first = t_idx == 0
    is_last = t_idx == n_seq_tiles - 1

    permute, permute_tail = _sort_chunk(h2_scr[...], route_scr[...], upper_ref, lower_ref,
                                        xl_ref, pos_ref, nb_ref)
    x = x_ref[0]

    uc = uc_ref[0]
    row = lax.broadcasted_iota(I32, (SEQ_TILE, 1), 0)
    prev_row = jnp.where(is_first, 0.0, ucp_ref[0][7:8, :])
    next_row = jnp.where(is_last, 0.0, ucn_ref[0][0:1, :])
    up = jnp.where(row == 0, prev_row, pltpu.roll(uc, 1, axis=0))
    dn = jnp.where(row == SEQ_TILE - 1, next_row, pltpu.roll(uc, SEQ_TILE - 1, axis=0))
    wconv = wconv_ref[...]
    y = bconv_ref[...] + (up * wconv[0:1, :] + uc * wconv[1:2, :] + dn * wconv[2:3, :])

    hb = _rms_mod(x, g1n_ref[...], sh1_ref[0], sc1_ref[0]).astype(BF16)
    bq = jnp.dot(hb, wb_ref[:, 0:2 * CONV_W], preferred_element_type=F32)
    ya = jnp.dot((bq[:, :CONV_W] * y).astype(BF16), wa_ref[...], preferred_element_type=F32)
    q = _rope(bq[:, CONV_W:], cos_ref[...], sin_ref[...]) * (HEAD_DIM ** -0.5 * LOG2E)

    k_all = jnp.concatenate([kp_ref[0], k_ref[0], kn_ref[0]], axis=0)
    vt_all = jnp.concatenate([vtp_ref[0], vt_ref[0], vtn_ref[0]], axis=1)
    k_ctx = kc_ref[0]
    vt_ctx = vtc_ref[0]
    bias_prev = bias_ref[0:BLOCK, :]
    bias_next = bias_ref[2 * BLOCK:3 * BLOCK, :]
    sink_row = sink_ref[...] * LOG2E
    n_units = BLOCKS_PER_TILE * N_KV_HEADS
    gate_cols = 2 * D_MODEL // n_units
    gate_chunks = []

    def gate_chunk(u):
        c0 = 2 * CONV_W + u * gate_cols
        gate_chunks.append(jnp.dot(hb, wb_ref[:, c0:c0 + gate_cols], preferred_element_type=F32))

    q_ts, biases = [], []
    for jb in range(BLOCKS_PER_TILE):
        bias_p, bias_n = bias_prev, bias_next
        if jb == 0:
            bias_p = bias_prev + jnp.where(is_first, NEG_INF, 0.0)
        if jb == BLOCKS_PER_TILE - 1:
            bias_n = bias_next + jnp.where(is_last, NEG_INF, 0.0)
        biases.append((jnp.concatenate([bias_p] * 4, axis=1), jnp.concatenate([bias_n] * 4, axis=1)))
        q_ts.append(q[jb * BLOCK:(jb + 1) * BLOCK, :].T.astype(BF16))

    scores, probs, outs = {}, {}, {}
    for t in range(n_units + 2):
        if t == 3:
            wub_ref[...] = wu_ref[...].astype(BF16)
        if t == 5:
            wdb_ref[...] = wd_ref[...].astype(BF16)
        if t == 0:
            permute(0, COMMON_ROWS // 2)
        if t == 4:
            permute(COMMON_ROWS // 2, COMMON_ROWS // 2)
        if t < n_units:
            jb, g = divmod(t, N_KV_HEADS)
            scores[t] = _attn_scores(q_ts[jb], g, k_all[jb * BLOCK:(jb + 3) * BLOCK, :], k_ctx,
                                     *biases[jb])
            gate_chunk(t)
        if 0 <= t - 1 < n_units:
            g = (t - 1) % N_KV_HEADS
            probs[t - 1] = _attn_probs(scores.pop(t - 1),
                                       sink_row[:, g * 4 * BLOCK:(g + 1) * 4 * BLOCK])
        if 0 <= t - 2 < n_units:
            jb, g = divmod(t - 2, N_KV_HEADS)
            outs[t - 2] = _attn_values(
                probs.pop(t - 2), sink_row[:, g * 4 * BLOCK:(g + 1) * 4 * BLOCK],
                vt_all[g * HEAD_DIM:(g + 1) * HEAD_DIM, jb * BLOCK:(jb + 3) * BLOCK],
                vt_ctx[g * HEAD_DIM:(g + 1) * HEAD_DIM, :])
    o_blocks = []
    for jb in range(BLOCKS_PER_TILE):
        o_rows = [outs[jb * N_KV_HEADS + g][:, h * BLOCK:(h + 1) * BLOCK]
                  for g in range(N_KV_HEADS) for h in range(4)]
        o_blocks.append(jnp.concatenate(o_rows, axis=0).T)
    gates = jnp.concatenate(gate_chunks, axis=1)

    o = jnp.concatenate(o_blocks, axis=0).astype(BF16)
    yb = jnp.dot(o, wbb_ref[...], preferred_element_type=F32)
    merged = (_sigmoid(gates[:, :D_MODEL]) * ya + _sigmoid(gates[:, D_MODEL:]) * yb).astype(BF16)
    x1 = x + gt1_ref[0] * jnp.dot(merged, wo_ref[...], preferred_element_type=F32)
    x1_ref[0] = x1
    h2 = _rms_mod(x1, g2n_ref[...], sh2_ref[0], sc2_ref[0]).astype(BF16)
    logits_t = lax.dot_general(wr_ref[...], h2, (((1,), (1,)), ((), ())),
                               preferred_element_type=F32) + br_ref[...]
    permute_tail()
    h2_scr[...] = h2
    route_scr[...] = _route(logits_t)


N_MIX_OUTPUTS = 6
N_MIX_SCRATCH = 2


def _mix_kernel(*refs, n_seq_tiles, batch):
    n_tail = N_MIX_OUTPUTS + N_MIX_SCRATCH
    upper_ref, lower_ref = refs[-n_tail - 2:-n_tail]
    _, xl_ref, pos_ref, nb_ref, _, _ = refs[-n_tail:-N_MIX_SCRATCH]
    h2_scr, route_scr = refs[-N_MIX_SCRATCH:]
    step = pl.program_id(0)
    n_main = n_seq_tiles * batch

    @pl.when(step == 0)
    def _():
        h2_scr[...] = jnp.zeros_like(h2_scr)
        route_scr[...] = jnp.zeros_like(route_scr)

    @pl.when(step < n_main)
    def _():
        _mix_step(*refs, n_seq_tiles=n_seq_tiles, batch=batch)

    @pl.when(step == n_main)
    def _():
        permute, permute_tail = _sort_chunk(h2_scr[...], route_scr[...], upper_ref, lower_ref,
                                            xl_ref, pos_ref, nb_ref)
        permute(0, COMMON_ROWS)
        permute_tail()


def _mix(x, norm1_g, norm2_g, mod3, w_b_cols, cos_t, sin_t, uc, k_rot, v_t, k_ctx, vt_ctx,
         w_conv, b_conv, w_a, w_b, w_o, sink_row, bias, wr_t, br_col, w_up, w_down, upper, lower):
    bsz, seq, _ = x.shape
    nt = seq // SEQ_TILE
    nblk = seq // BLOCK
    clen = k_ctx.shape[1]
    rows8 = SEQ_TILE // 8
    assert N_EXPERTS % (nt * bsz) == 0, "each grid step converts an equal share of the experts"
    e_step = N_EXPERTS // (nt * bsz)

    def mod_spec(j):
        return pl.BlockSpec((1, 1, D_MODEL), lambda t, b, j=j: (b, 0, j))

    def const_spec(shape):
        return pl.BlockSpec(shape, lambda t, b: tuple(0 for _ in shape))

    in_specs = [
        pl.BlockSpec((1, SEQ_TILE, D_MODEL), lambda t, b: (b, t, 0)),
        const_spec((1, D_MODEL)), const_spec((1, D_MODEL)),
        mod_spec(0), mod_spec(1), mod_spec(2), mod_spec(3), mod_spec(4),
        const_spec(w_b_cols.shape),
        pl.BlockSpec((SEQ_TILE, LANES), lambda t, b: (t, 0)),
        pl.BlockSpec((SEQ_TILE, LANES), lambda t, b: (t, 0)),
        pl.BlockSpec((1, SEQ_TILE, CONV_W), lambda t, b: (b, t, 0)),
        pl.BlockSpec((1, 8, CONV_W), lambda t, b: (b, jnp.maximum(t * rows8 - 1, 0), 0)),
        pl.BlockSpec((1, 8, CONV_W),
                     lambda t, b: (b, jnp.minimum((t + 1) * rows8, seq // 8 - 1), 0)),
        pl.BlockSpec((1, SEQ_TILE, KV_W), lambda t, b: (b, t, 0)),
        pl.BlockSpec((1, BLOCK, KV_W),
                     lambda t, b: (b, jnp.maximum(t * BLOCKS_PER_TILE - 1, 0), 0)),
        pl.BlockSpec((1, BLOCK, KV_W),
                     lambda t, b: (b, jnp.minimum((t + 1) * BLOCKS_PER_TILE, nblk - 1), 0)),
        pl.BlockSpec((1, KV_W, SEQ_TILE), lambda t, b: (b, 0, t)),
        pl.BlockSpec((1, KV_W, BLOCK),
                     lambda t, b: (b, 0, jnp.maximum(t * BLOCKS_PER_TILE - 1, 0))),
        pl.BlockSpec((1, KV_W, BLOCK),
                     lambda t, b: (b, 0, jnp.minimum((t + 1) * BLOCKS_PER_TILE, nblk - 1))),
        pl.BlockSpec((1, clen, KV_W), lambda t, b: (b, 0, 0)),
        pl.BlockSpec((1, KV_W, clen), lambda t, b: (b, 0, 0)),
        const_spec(w_conv.shape), const_spec(b_conv.shape),
        const_spec(w_a.shape), const_spec(w_b.shape), const_spec(w_o.shape),
        const_spec(sink_row.shape), const_spec(bias.shape),
        const_spec(wr_t.shape), const_spec(br_col.shape),
        pl.BlockSpec((e_step,) + w_up.shape[1:], lambda t, b: (t * bsz + b, 0, 0)),
        pl.BlockSpec((e_step,) + w_down.shape[1:], lambda t, b: (t * bsz + b, 0, 0)),
        const_spec(upper.shape), const_spec(lower.shape),
    ]
    nch = bsz * nt
    out_specs = [
        pl.BlockSpec((1, SEQ_TILE, D_MODEL), lambda t, b: (b, t, 0)),
        pl.BlockSpec((CHUNK_ROWS, XL_COLS), lambda t, b: (b * nt + t, 0)),
        pl.BlockSpec((CHUNK, LANES), lambda t, b: (b * nt + t, 0)),
        pl.BlockSpec((1, N_EXPERTS, LANES), lambda t, b: (b * nt + t, 0, 0)),
        pl.BlockSpec((e_step,) + w_up.shape[1:], lambda t, b: (t * bsz + b, 0, 0)),
        pl.BlockSpec((e_step,) + w_down.shape[1:], lambda t, b: (t * bsz + b, 0, 0)),
    ]
    out_shape = [
        jax.ShapeDtypeStruct((bsz, seq, D_MODEL), F32),
        jax.ShapeDtypeStruct((nch * CHUNK_ROWS, XL_COLS), BF16),
        jax.ShapeDtypeStruct((nch * CHUNK, LANES), F32),
        jax.ShapeDtypeStruct((nch, N_EXPERTS, LANES), I32),
        jax.ShapeDtypeStruct(w_up.shape, BF16),
        jax.ShapeDtypeStruct(w_down.shape, BF16),
    ]
    n_main = nt * bsz

    def tile_of(s):
        s = jnp.minimum(s, n_main - 1)
        return s // bsz, s % bsz

    def this_tile(spec):
        return pl.BlockSpec(spec.block_shape, lambda s, f=spec.index_map: f(*tile_of(s)))

    def prev_tile(spec):
        return pl.BlockSpec(spec.block_shape,
                            lambda s, f=spec.index_map: f(*tile_of(jnp.maximum(s - 1, 0))))

    in_specs = [this_tile(sp) for sp in in_specs]
    out_specs = [this_tile(out_specs[0])] + [prev_tile(sp) for sp in out_specs[1:4]] + [
        this_tile(sp) for sp in out_specs[4:]]
    return pl.pallas_call(
        functools.partial(_mix_kernel, n_seq_tiles=nt, batch=bsz),
        grid=(n_main + 1,),
        in_specs=in_specs,
        out_specs=out_specs,
        out_shape=out_shape,
        scratch_shapes=[pltpu.VMEM((CHUNK, D_MODEL), BF16), pltpu.VMEM((ROUTE_ROWS, CHUNK), F32)],
        compiler_params=_cparams(("arbitrary",)),
        name="token_mix",
    )(x, norm1_g, norm2_g, mod3, mod3, mod3, mod3, mod3, w_b_cols, cos_t, sin_t,
      uc, uc, uc, k_rot, k_rot, k_rot, v_t, v_t, v_t, k_ctx, vt_ctx,
      w_conv, b_conv, w_a, w_b, w_o, sink_row, bias, wr_t, br_col, w_up, w_down, upper, lower)


def _bf16_parts(v):
    hi = v.astype(BF16).astype(F32)
    r1 = v - hi
    mid = r1.astype(BF16).astype(F32)
    lo = (r1 - mid).astype(BF16).astype(F32)
    return hi, mid, lo


def _sort_chunk(h, route, upper_ref, lower_ref, xl_ref, pos_ref, nb_ref):
    e1 = route[0:1, :].astype(I32)
    e2 = route[1:2, :].astype(I32)
    erow = lax.broadcasted_iota(I32, (N_EXPERTS, CHUNK), 0)
    hit1 = erow == e1
    hit2 = erow == e2
    onehot = jnp.where(hit1, 1.0, 0.0) + jnp.where(hit2, 1.0, 0.0)
    cum = jnp.dot(onehot.astype(BF16), upper_ref[...], preferred_element_type=F32)
    cnt = jnp.sum(onehot, axis=1, keepdims=True)
    nblk = jnp.floor((cnt + (ROW_BLOCK - 1)) * (1.0 / ROW_BLOCK))
    nblk_b = jnp.broadcast_to(nblk, (N_EXPERTS, LANES))
    seg = jnp.dot(lower_ref[...], nblk_b.astype(BF16), preferred_element_type=F32) * ROW_BLOCK
    base = seg[:, 0:1] + cum
    pos1 = jnp.sum(jnp.where(hit1, base, 0.0), axis=0, keepdims=True)
    pos2 = jnp.sum(jnp.where(hit2, base, 0.0), axis=0, keepdims=True)
    p1i = pos1.astype(I32)
    p2i = pos2.astype(I32)
    used_rows = (jnp.sum(nblk) * ROW_BLOCK).astype(I32)

    prow = lax.broadcasted_iota(I32, (LANES, CHUNK), 0)
    parts = _bf16_parts(route[2:3, :]) + _bf16_parts(route[3:4, :])
    gpart_rows = jnp.zeros((LANES, CHUNK), F32)
    for j, part in enumerate(parts):
        gpart_rows = jnp.where(prow == j, part, gpart_rows)
    h_ext = jnp.concatenate([h, gpart_rows.T.astype(BF16)], axis=1)
    glane = lax.broadcasted_iota(I32, (1, GATE_COLS), 1)
    one = jnp.ones((), BF16)
    zero = jnp.zeros((), BF16)
    def permute(row0, n_rows):
        r16 = lax.broadcasted_iota(I32, (n_rows, CHUNK), 0).astype(jnp.int16)
        q1 = (p1i - row0).astype(jnp.int16)
        q2 = (p2i - row0).astype(jnp.int16)
        sel1 = jnp.where(r16 == q1, one, zero)
        sel = jnp.where(r16 == q2, one, sel1)
        is_slot1 = jnp.max(sel1, axis=1, keepdims=True)
        rows = pl.ds(row0, n_rows)
        xg = jnp.dot(sel, h_ext, preferred_element_type=F32)
        xl_ref[rows, 0:D_MODEL] = xg[:, :D_MODEL].astype(BF16)
        g6 = xg[:, D_MODEL:]
        g3 = jnp.where(is_slot1.astype(F32) > 0.0, g6, pltpu.roll(g6, GATE_COLS - 3, axis=1))
        xl_ref[rows, D_MODEL:XL_COLS] = jnp.where(glane < 3, g3, 0.0).astype(BF16)

    pos_rows = jnp.where(prow == 0, pos1, jnp.where(prow == 1, pos2, 0.0))
    pos_ref[...] = pos_rows.T
    nb_ref[0] = nblk_b.astype(I32)

    def permute_tail():
        tail_rows = CHUNK_ROWS - COMMON_ROWS

        @pl.when(COMMON_ROWS < used_rows)
        def _():
            permute(COMMON_ROWS, tail_rows)

        @pl.when(COMMON_ROWS >= used_rows)
        def _():
            xl_ref[pl.ds(COMMON_ROWS, tail_rows), :] = jnp.zeros((tail_rows, XL_COLS), BF16)

    return permute, permute_tail


def _max_tiles(nch):
    max_blocks = nch * (2 * CHUNK + N_EXPERTS * (ROW_BLOCK - 1)) // ROW_BLOCK
    return max_blocks // TILE_BLOCKS + N_EXPERTS


def _masked_prefix(le, values):
    delta = values - jnp.concatenate([jnp.zeros((1,), values.dtype), values[:-1]])
    return jnp.sum(jnp.where(le, delta[None, :], 0), axis=1)


def _tile_plan(nb, nch):
    n_tiles = _max_tiles(nch)
    nbt = nb.T
    nbe = jnp.sum(nbt, axis=1)
    nte = (nbe + TILE_BLOCKS - 1) // TILE_BLOCKS
    tile_end = jnp.cumsum(nte)
    tile_start = tile_end - nte
    n_act = tile_end[-1]
    tiles = jnp.arange(n_tiles, dtype=I32)
    te = jnp.sum((tile_end[None, :] <= tiles[:, None]).astype(I32), axis=1)
    e_ar = jnp.arange(N_EXPERTS, dtype=I32)
    active = nte > 0
    te = jnp.where(tiles < n_act, te, jnp.max(jnp.where(active, e_ar, 0)))
    first = jnp.logical_and(te != jnp.concatenate([jnp.full((1,), -1, I32), te[:-1]]),
                            tiles < n_act).astype(I32)
    rank = jnp.cumsum(active.astype(I32)) - 1
    k_ar = jnp.arange(N_EXPERTS + W_SLOTS, dtype=I32)
    eseq = jnp.sum(jnp.where(jnp.logical_and(active[None, :], rank[None, :] == k_ar[:, None]),
                             e_ar[None, :], 0), axis=1)
    n_exp = jnp.sum(active.astype(I32))

    cb_excl = jnp.cumsum(nbt, axis=1) - nbt
    gs = TILE_BLOCKS * tile_start[:, None] + cb_excl
    seg_blk = jnp.cumsum(nb, axis=1) - nb
    base_blk = jnp.arange(nch, dtype=I32)[None, :] * CHUNK_BLOCKS + seg_blk.T
    gs_f = gs.reshape(-1)
    slots = jnp.arange(n_tiles * TILE_BLOCKS, dtype=I32)
    le = gs_f[None, :] <= slots[:, None]
    blk = slots + _masked_prefix(le, (base_blk - gs).reshape(-1))
    valid = slots < _masked_prefix(le, (gs + nbt).reshape(-1))
    gblk = jnp.concatenate([jnp.where(valid, blk, 0).astype(I32),
                            jnp.zeros(((X_SLOTS - 1) * TILE_BLOCKS,), I32)])

    m = jnp.arange(CHUNK_BLOCKS, dtype=I32)
    le_c = seg_blk[:, None, :] <= m[None, :, None]
    shift = (gs.T - seg_blk)
    delta = shift - jnp.concatenate([jnp.zeros((nch, 1), I32), shift[:, :-1]], axis=1)
    slot_of = m[None, :] + jnp.sum(jnp.where(le_c, delta[:, None, :], 0), axis=2)
    used_blocks = jnp.sum(nb, axis=1)
    slot_of = jnp.where(m[None, :] < used_blocks[:, None], slot_of, 0).astype(I32)
    pad = Y_RING - 1
    slot_of = jnp.concatenate([slot_of.reshape(-1), jnp.zeros((pad * CHUNK_BLOCKS,), I32)])
    used_rows = jnp.concatenate([ROW_BLOCK * used_blocks, jnp.zeros((pad,), I32)]).astype(I32)
    return ((n_act.reshape(1).astype(I32), n_exp.reshape(1), first, eseq, gblk),
            (used_rows, slot_of))


def _gmm_kernel(nact_ref, nexp_ref, first_ref, eseq_ref, gblk_ref,
                xl_hbm, wu_hbm, wd_hbm, yt_hbm,
                xbuf, ybuf, zbuf, wu_st, wd_st, in_sem, out_sem, w_sem, z_sem):
    n_act = nact_ref[0]
    n_exp = nexp_ref[0]
    n_tiles = yt_hbm.shape[0] // TILE_BLOCKS
    prefetch = X_SLOTS - 1

    def gather(tile, b):
        s = tile % X_SLOTS
        blk = gblk_ref[tile * TILE_BLOCKS + b]
        return pltpu.make_async_copy(xl_hbm.at[blk], xbuf.at[s, b], in_sem.at[s])

    def write_back(tile):
        s = tile % Y_SLOTS
        return pltpu.make_async_copy(ybuf.at[s], yt_hbm.at[pl.ds(tile * TILE_BLOCKS, TILE_BLOCKS)],
                                     out_sem.at[s])

    def zero_fill(tile):
        return pltpu.make_async_copy(zbuf, yt_hbm.at[pl.ds(tile * TILE_BLOCKS, TILE_BLOCKS)], z_sem)

    def weights(q):
        e = eseq_ref[q]
        s = q % W_SLOTS
        return (pltpu.make_async_copy(wu_hbm.at[e], wu_st.at[s], w_sem.at[0, s]),
                pltpu.make_async_copy(wd_hbm.at[e], wd_st.at[s], w_sem.at[1, s]))

    def tile_blocks(fn):
        for b in range(TILE_BLOCKS):
            fn(b)

    for q0 in range(W_SLOTS - 1):
        @pl.when(q0 < n_exp)
        def _(q0=q0):
            for cp in weights(q0):
                cp.start()
    for t0 in range(prefetch):
        tile_blocks(lambda b, t0=t0: gather(t0, b).start())

    def body(i, q):
        tile_blocks(lambda b: gather(i, b).wait())

        @pl.when(i >= Y_SLOTS)
        def _():
            write_back(i - Y_SLOTS).wait()

        is_first = first_ref[i] == 1

        @pl.when(is_first)
        def _():
            for cp in weights(q):
                cp.wait()

            @pl.when(q + W_SLOTS - 1 < n_exp)
            def _():
                for cp in weights(q + W_SLOTS - 1):
                    cp.start()

        q = q + is_first.astype(I32)
        ws = (q - 1) % W_SLOTS
        xs = i % X_SLOTS
        ys = i % Y_SLOTS

        x = xbuf[xs].reshape(TILE_ROWS, XL_COLS)
        gate = jnp.sum(x[:, D_MODEL:].astype(F32), axis=1, keepdims=True)
        au = jnp.dot(x[:, :D_MODEL], wu_st[ws], preferred_element_type=F32)
        tile_blocks(lambda b: gather(i + prefetch, b).start())
        a = au[:, :EXPERT_FF]
        act = (a * _sigmoid(a)) * au[:, EXPERT_FF:]
        y = jnp.dot(act.astype(BF16), wd_st[ws], preferred_element_type=F32)
        ybuf[ys] = (gate * y).astype(BF16).reshape(TILE_BLOCKS, ROW_BLOCK, D_MODEL)
        write_back(i).start()

        @pl.when(n_act + i < n_tiles)
        def _():
            zero_fill(n_act + i).start()
        return q

    zbuf[...] = jnp.zeros_like(zbuf)
    lax.fori_loop(0, n_act, body, jnp.int32(0))

    def fill_rest(t, carry):
        zero_fill(t).start()
        return carry

    lax.fori_loop(jnp.minimum(2 * n_act, n_tiles), n_tiles, fill_rest, 0)

    def drain(t, carry):
        zero_fill(t).wait()
        return carry

    lax.fori_loop(n_act, n_tiles, drain, 0)
    for k in range(prefetch):
        tile_blocks(lambda b, k=k: gather(n_act + k, b).wait())
    for k in range(Y_SLOTS):
        @pl.when(n_act - 1 - k >= 0)
        def _(k=k):
            write_back(n_act - 1 - k).wait()


def _grouped_mlp(plan, xl, w_up, w_down, n_tiles):
    grid_spec = pltpu.PrefetchScalarGridSpec(
        num_scalar_prefetch=len(plan),
        grid=(1,),
        in_specs=[pl.BlockSpec(memory_space=pl.ANY)] * 3,
        out_specs=pl.BlockSpec(memory_space=pl.ANY),
        scratch_shapes=[pltpu.VMEM((X_SLOTS, TILE_BLOCKS, ROW_BLOCK, XL_COLS), BF16),
                        pltpu.VMEM((Y_SLOTS, TILE_BLOCKS, ROW_BLOCK, D_MODEL), BF16),
                        pltpu.VMEM((TILE_BLOCKS, ROW_BLOCK, D_MODEL), BF16),
                        pltpu.VMEM((W_SLOTS, D_MODEL, 2 * EXPERT_FF), BF16),
                        pltpu.VMEM((W_SLOTS, EXPERT_FF, D_MODEL), BF16),
                        pltpu.SemaphoreType.DMA((X_SLOTS,)),
                        pltpu.SemaphoreType.DMA((Y_SLOTS,)),
                        pltpu.SemaphoreType.DMA((2, W_SLOTS)),
                        pltpu.SemaphoreType.DMA(())])
    return pl.pallas_call(
        _gmm_kernel,
        grid_spec=grid_spec,
        out_shape=jax.ShapeDtypeStruct((n_tiles * TILE_BLOCKS, ROW_BLOCK, D_MODEL), BF16),
        compiler_params=_cparams(("arbitrary",)),
        name="moe_experts",
    )(*plan, xl, w_up, w_down)


def _combine_kernel(used_ref, slot_ref, yt_hbm, pos_ref, x1_ref, gt2_ref, fg_ref, o_ref, ybuf, sem):
    c = pl.program_id(0)
    n_chunks = pl.num_programs(0)
    used_rows = used_ref[c]

    def chunk_blocks(chunk, op):
        s = chunk % Y_RING

        def copy(b):
            slot = slot_ref[chunk * CHUNK_BLOCKS + b]
            return pltpu.make_async_copy(yt_hbm.at[slot], ybuf.at[s, b], sem.at[s])

        for b in range(COMMON_BLOCKS):
            op(copy(b))

        @pl.when(used_ref[chunk] > COMMON_ROWS)
        def _():
            for b in range(COMMON_BLOCKS, CHUNK_BLOCKS):
                op(copy(b))

    @pl.when(c == 0)
    def _():
        for k in range(Y_RING - 1):
            chunk_blocks(k, lambda cp: cp.start())

    chunk_blocks(c + Y_RING - 1, lambda cp: cp.start())
    chunk_blocks(c, lambda cp: cp.wait())

    def body(k_rows):
        pos = pos_ref[...]
        p1 = pos[:, 0:1].astype(I32)
        p2 = pos[:, 1:2].astype(I32)
        r = lax.broadcasted_iota(I32, (CHUNK, k_rows), 1)
        sel = jnp.where(r == p1, 1.0, jnp.where(r == p2, 1.0, 0.0)).astype(BF16)
        yl = ybuf[c % Y_RING, 0:k_rows // ROW_BLOCK].reshape(k_rows, D_MODEL)
        y = jnp.dot(sel, yl, preferred_element_type=F32)
        x2 = x1_ref[...] + gt2_ref[0] * y
        inv = lax.rsqrt(jnp.mean(x2 * x2, axis=-1, keepdims=True) + NORM_EPS)
        o_ref[...] = (x2 * inv) * fg_ref[...]

    pl.when(used_rows <= COMMON_ROWS)(lambda: body(COMMON_ROWS))
    pl.when(used_rows > COMMON_ROWS)(lambda: body(CHUNK_ROWS))

    @pl.when(c == n_chunks - 1)
    def _():
        for k in range(1, Y_RING):
            chunk_blocks(c + k, lambda cp: cp.wait())


def _combine(used_rows, slot_of, yt, pos_c, x1, mod3, final_g, seq):
    n = x1.shape[0]
    per_seq = seq // CHUNK
    grid_spec = pltpu.PrefetchScalarGridSpec(
        num_scalar_prefetch=2,
        grid=(n // CHUNK,),
        in_specs=[pl.BlockSpec(memory_space=pl.ANY),
                  pl.BlockSpec((CHUNK, LANES), lambda c, u, s: (c, 0)),
                  pl.BlockSpec((CHUNK, D_MODEL), lambda c, u, s: (c, 0)),
                  pl.BlockSpec((1, 1, D_MODEL), lambda c, u, s: (c // per_seq, 0, 5)),
                  pl.BlockSpec((1, D_MODEL), lambda c, u, s: (0, 0))],
        out_specs=pl.BlockSpec((CHUNK, D_MODEL), lambda c, u, s: (c, 0)),
        scratch_shapes=[pltpu.VMEM((Y_RING, CHUNK_BLOCKS, ROW_BLOCK, D_MODEL), BF16),
                        pltpu.SemaphoreType.DMA((Y_RING,))])
    return pl.pallas_call(
        _combine_kernel,
        grid_spec=grid_spec,
        out_shape=jax.ShapeDtypeStruct((n, D_MODEL), F32),
        compiler_params=_cparams(("arbitrary",)),
        name="moe_combine",
    )(used_rows, slot_of, yt, pos_c, x1, mod3, final_g)


def _rope_tables(seq):
    n_freq = HEAD_DIM // 4
    inv_freq = ROPE_BASE ** (-jnp.arange(n_freq, dtype=F32) / n_freq)
    rows = seq // GRID_W
    row = jnp.repeat(jnp.arange(rows, dtype=F32), GRID_W)
    col = jnp.tile(jnp.arange(GRID_W, dtype=F32), rows)
    ang = jnp.concatenate([row[:, None] * inv_freq, col[:, None] * inv_freq], axis=-1)
    cos, sin = jnp.cos(ang), jnp.sin(ang)
    return jnp.tile(cos, (1, 4)), jnp.concatenate([-sin, sin, -sin, sin], axis=1)


def _window_bias():
    key = np.arange(3 * BLOCK)[:, None]
    qry = np.arange(BLOCK)[None, :]
    valid = (key - qry >= 0) & (key - qry <= 2 * BLOCK)
    return jnp.asarray(np.where(valid, 0.0, NEG_INF), F32)


def kernel(x, c, ctx, c_ctx, w_ada, b_ada, norm1_g, w_in, w_conv, b_conv, w_a, w_b, sink, w_o,
           norm2_g, w_group, b_group, w_router, b_router, w_up, w_down, final_g):
    bsz, seq, _ = x.shape
    assert w_ada.shape[0] == 1 and seq % SEQ_TILE == 0

    cc = jnp.zeros((16, D_MODEL), F32).at[:bsz].set(c).at[bsz].set(c_ctx)
    mod = _modulation(cc, w_ada[0], b_ada[0][None, :])
    mod3 = mod[:bsz].reshape(bsz, 1, N_MOD * D_MODEL)
    csh1 = mod[bsz:bsz + 1, 0:D_MODEL]
    csc1 = mod[bsz:bsz + 1, D_MODEL:2 * D_MODEL]

    w = w_in[0]
    n1g = norm1_g[0][None, :]
    n2g = norm2_g[0][None, :]
    w_kv = w[:, OFF_K:OFF_GA].astype(BF16)
    w_a_cols = jnp.concatenate([w[:, OFF_CG:OFF_Q], w[:, OFF_K:OFF_GA]], axis=1).astype(BF16)
    w_b_cols = jnp.concatenate([w[:, OFF_B:OFF_CG], w[:, OFF_Q:OFF_K], w[:, OFF_GA:]],
                               axis=1).astype(BF16)
    cos_t, sin_t = _rope_tables(seq)

    k_ctx, vt_ctx = _context_kv(ctx, n1g, csh1, csc1, w_kv)
    uc, k_rot, v_t = _proj_a(x, n1g, mod3, w_a_cols, cos_t, sin_t)

    sink_row = jnp.repeat(sink[0].astype(F32), BLOCK)[None, :]
    pad_g = jnp.zeros((D_MODEL, GROUP_ROWS - N_GROUPS), F32)
    pad_e = jnp.zeros((D_MODEL, ROUTER_ROWS - GROUP_ROWS - N_EXPERTS), F32)
    wr_t = jnp.concatenate([w_group[0], pad_g, w_router[0], pad_e], axis=1).T.astype(BF16)
    br_col = jnp.concatenate([b_group[0], pad_g[0], b_router[0], pad_e[0]])[:, None]

    upper = jnp.asarray(np.triu(np.ones((CHUNK, CHUNK), np.float32), 1), BF16)
    lower = jnp.asarray(np.tril(np.ones((N_EXPERTS, N_EXPERTS), np.float32), -1), BF16)
    x1, xl, pos_c, nb, w_up_bf, w_down_bf = _mix(
        x, n1g, n2g, mod3, w_b_cols, cos_t, sin_t, uc, k_rot, v_t, k_ctx, vt_ctx,
        w_conv[0], b_conv[0][None, :], w_a[0].astype(BF16), w_b[0].astype(BF16),
        w_o[0].astype(BF16), sink_row, _window_bias(), wr_t, br_col, w_up[0], w_down[0],
        upper, lower)

    n = bsz * seq
    nch = n // CHUNK
    nb = nb[:, :, 0]
    expert_plan, combine_plan = _tile_plan(nb, nch)
    yt = _grouped_mlp(expert_plan, xl.reshape(-1, ROW_BLOCK, XL_COLS), w_up_bf, w_down_bf,
                      _max_tiles(nch))
    out = _combine(*combine_plan, yt, pos_c, x1.reshape(n, D_MODEL), mod3, final_g[None, :], seq)
    return out.reshape(bsz, seq, D_MODEL)
```

```python
import functools

import numpy as np
import jax
import jax.numpy as jnp
from jax import lax
from jax.experimental import pallas as pl
from jax.experimental.pallas import tpu as pltpu

F32 = jnp.float32
BF16 = jnp.bfloat16
I32 = jnp.int32

D_MODEL = 1024
GRID_W = 64
CONV_W = 512
N_HEADS = 8
N_KV_HEADS = 2
HEAD_DIM = 64
ATT_W = N_HEADS * HEAD_DIM
KV_W = N_KV_HEADS * HEAD_DIM
BLOCK = 128
ROPE_BASE = 10000.0
N_GROUPS = 4
EXPERTS_PER_GROUP = 8
N_EXPERTS = N_GROUPS * EXPERTS_PER_GROUP
EXPERT_FF = 256
N_MOD = 6
NORM_EPS = 1e-6
NEG_INF = -1e30
LOG2E = 1.4426950408889634
ONES_ROWS = 16

OFF_B, OFF_CG, OFF_XIN, OFF_Q, OFF_K, OFF_V, OFF_GA, OFF_GB = (
    0, 512, 1024, 1536, 2048, 2176, 2304, 3328)
IN_COLS = 4352

LANES = 128
SEQ_TILE = 512
BLOCKS_PER_TILE = SEQ_TILE // BLOCK
ROUTE_ROWS = 8
GROUP_ROWS = 8
ROUTER_ROWS = 48
VMEM_LIMIT = 62 * 1024 * 1024

CHUNK = SEQ_TILE
ROW_BLOCK = 16
CHUNK_ROWS = -(-(2 * CHUNK + N_EXPERTS * (ROW_BLOCK - 1)) // 256) * 256
CHUNK_BLOCKS = CHUNK_ROWS // ROW_BLOCK
GATE_COLS = LANES
XL_COLS = D_MODEL + GATE_COLS
COMMON_ROWS = 2 * CHUNK + 256
COMMON_BLOCKS = COMMON_ROWS // ROW_BLOCK
TILE_BLOCKS = 32
TILE_ROWS = TILE_BLOCKS * ROW_BLOCK
X_SLOTS = 6
Y_SLOTS = 4
W_SLOTS = 3
Y_RING = 3
RING = 3


def _cparams(sem):
    return pltpu.CompilerParams(dimension_semantics=sem, vmem_limit_bytes=VMEM_LIMIT)


def _rms_mod(x, g, shift, scale):
    inv = lax.rsqrt(jnp.mean(x * x, axis=-1, keepdims=True) + NORM_EPS)
    return (x * inv) * (g * (1.0 + scale)) + shift


def _mod_kernel(c_ref, w_ref, b_ref, o_ref):
    c = c_ref[...]
    a = (c * jax.nn.sigmoid(c)).astype(BF16)
    o_ref[...] = jnp.dot(a, w_ref[...].astype(BF16), preferred_element_type=F32) + b_ref[...]


def _modulation(cc, w_ada, b_ada):
    rows = cc.shape[0]
    cols = w_ada.shape[1]
    tile = 1024
    return pl.pallas_call(
        _mod_kernel,
        grid=(cols // tile,),
        in_specs=[pl.BlockSpec((rows, D_MODEL), lambda j: (0, 0)),
                  pl.BlockSpec((D_MODEL, tile), lambda j: (0, j)),
                  pl.BlockSpec((1, tile), lambda j: (0, j))],
        out_specs=pl.BlockSpec((rows, tile), lambda j: (0, j)),
        out_shape=jax.ShapeDtypeStruct((rows, cols), F32),
        compiler_params=_cparams(("arbitrary",)),
        name="adaln_mod",
    )(cc, w_ada, b_ada)


def _ctx_kernel(x_ref, g_ref, sh_ref, sc_ref, w_ref, k_ref, vt_ref):
    h = _rms_mod(x_ref[0], g_ref[...], sh_ref[...], sc_ref[...]).astype(BF16)
    kv = jnp.dot(h, w_ref[...], preferred_element_type=F32)
    k_ref[0] = kv[:, :KV_W].astype(BF16)
    vt_ref[0] = kv[:, KV_W:].T.astype(BF16)


def _context_kv(ctx, norm_g, csh, csc, w_kv):
    bsz, clen, _ = ctx.shape
    return pl.pallas_call(
        _ctx_kernel,
        grid=(bsz,),
        in_specs=[pl.BlockSpec((1, clen, D_MODEL), lambda b: (b, 0, 0)),
                  pl.BlockSpec((1, D_MODEL), lambda b: (0, 0)),
                  pl.BlockSpec((1, D_MODEL), lambda b: (0, 0)),
                  pl.BlockSpec((1, D_MODEL), lambda b: (0, 0)),
                  pl.BlockSpec((D_MODEL, 2 * KV_W), lambda b: (0, 0))],
        out_specs=[pl.BlockSpec((1, clen, KV_W), lambda b: (b, 0, 0)),
                   pl.BlockSpec((1, KV_W, clen), lambda b: (b, 0, 0))],
        out_shape=[jax.ShapeDtypeStruct((bsz, clen, KV_W), BF16),
                   jax.ShapeDtypeStruct((bsz, KV_W, clen), BF16)],
        compiler_params=_cparams(("arbitrary",)),
        name="context_kv",
    )(ctx, norm_g, csh, csc, w_kv)


def _rope(t, cos, sin_signed):
    lane = lax.broadcasted_iota(I32, (1, LANES), 1)
    first_half = (lane % HEAD_DIM) < (HEAD_DIM // 2)
    outs = []
    for j in range(t.shape[1] // LANES):
        tj = t[:, j * LANES:(j + 1) * LANES]
        partner = jnp.where(first_half,
                            pltpu.roll(tj, LANES - HEAD_DIM // 2, axis=1),
                            pltpu.roll(tj, HEAD_DIM // 2, axis=1))
        outs.append(tj * cos + partner * sin_signed)
    return outs[0] if len(outs) == 1 else jnp.concatenate(outs, axis=1)


def _sigmoid(x):
    return 1.0 / (1.0 + jnp.exp2(x * (-LOG2E)))


def _attn_scores(q_t, g, k_win, k_ctx, bias_p4, bias_n4):
    zeros = jnp.zeros((HEAD_DIM, 4 * BLOCK), BF16)
    qg = jnp.concatenate([q_t[(4 * g + h) * HEAD_DIM:(4 * g + h + 1) * HEAD_DIM, :]
                          for h in range(4)], axis=1)
    rhs = jnp.concatenate([qg, zeros] if g == 0 else [zeros, qg], axis=0)
    s_win = jnp.dot(k_win, rhs, preferred_element_type=F32)
    s_ctx = jnp.dot(k_ctx, rhs, preferred_element_type=F32)
    return (s_win[0:BLOCK] + bias_p4, s_win[BLOCK:2 * BLOCK], s_win[2 * BLOCK:] + bias_n4, s_ctx)


def _attn_probs(scores, sink):
    m = sink
    for s in scores:
        m = jnp.maximum(m, jnp.max(s, axis=0, keepdims=True))
    p_win = jnp.concatenate([jnp.exp2(s - m).astype(BF16) for s in scores[:3]], axis=0)
    p_ctx = jnp.exp2(scores[3] - m).astype(BF16)
    return p_win, p_ctx, m


def _attn_values(probs, sink, vt_win_g, vt_ctx_g):
    p_win, p_ctx, m = probs

    def with_ones(vt):
        r = lax.broadcasted_iota(I32, (ONES_ROWS, vt.shape[1]), 0)
        return jnp.concatenate([vt, jnp.where(r == 0, 1.0, 0.0).astype(BF16)], axis=0)

    o_ext = (jnp.dot(with_ones(vt_win_g), p_win, preferred_element_type=F32)
             + jnp.dot(with_ones(vt_ctx_g), p_ctx, preferred_element_type=F32))
    denom = o_ext[HEAD_DIM:HEAD_DIM + 1, :] + jnp.exp2(sink - m)
    return o_ext[:HEAD_DIM, :] / denom


def _route(logits_t):
    t = logits_t.shape[1]
    grow = lax.broadcasted_iota(I32, (GROUP_ROWS, t), 0)
    gl = jnp.where(grow < N_GROUPS, logits_t[0:GROUP_ROWS, :], NEG_INF)
    gm = jnp.max(gl, axis=0, keepdims=True)
    p_g = 1.0 / jnp.sum(jnp.exp(gl - gm), axis=0, keepdims=True)
    g_idx = jnp.min(jnp.where(gl == gm, grow, N_GROUPS), axis=0, keepdims=True)

    erow = lax.broadcasted_iota(I32, (N_EXPERTS, t), 0)
    el = logits_t[GROUP_ROWS:GROUP_ROWS + N_EXPERTS, :]
    sel = (erow // EXPERTS_PER_GROUP) == g_idx
    em = jnp.where(sel, el, NEG_INF)
    m1 = jnp.max(em, axis=0, keepdims=True)
    i1 = jnp.min(jnp.where(em == m1, erow, N_EXPERTS), axis=0, keepdims=True)
    em2 = jnp.where(erow == i1, NEG_INF, em)
    m2 = jnp.max(em2, axis=0, keepdims=True)
    i2 = jnp.min(jnp.where(em2 == m2, erow, N_EXPERTS), axis=0, keepdims=True)
    z = jnp.sum(jnp.where(sel, jnp.exp(el - m1), 0.0), axis=0, keepdims=True)
    p1 = 1.0 / z
    p2 = jnp.exp(m2 - m1) / z
    gate1 = p_g * p1 / (p1 + p2)
    gate2 = p_g * p2 / (p1 + p2)
    pad = jnp.zeros((ROUTE_ROWS - 4, t), F32)
    return jnp.concatenate([i1.astype(F32), i2.astype(F32), gate1, gate2, pad], axis=0)


def _project_a(x_ref, g_ref, sh_ref, sc_ref, w_ref, cos_ref, sin_ref, slot,
               uc_ring, k_ring, vt_ring):
    h = _rms_mod(x_ref[0], g_ref[...], sh_ref[0], sc_ref[0]).astype(BF16)
    u = jnp.dot(h, w_ref[...], preferred_element_type=F32)
    uc_ring[slot] = u[:, :CONV_W] * u[:, CONV_W:2 * CONV_W]
    k = u[:, 2 * CONV_W:2 * CONV_W + KV_W]
    k_ring[slot] = _rope(k, cos_ref[...], sin_ref[...]).astype(BF16)
    vt_ring[slot] = u[:, 2 * CONV_W + KV_W:].T.astype(BF16)


def _mix_step(x_ref, g1n_ref, g2n_ref, sh1_ref, sc1_ref, gt1_ref, sh2_ref, sc2_ref,
              wb_ref, cos_ref, sin_ref,
              xn_ref, shn_ref, scn_ref, cosn_ref, sinn_ref, wac_ref,
              kc_ref, vtc_ref, wconv_ref, bconv_ref, wa_ref, wbb_ref, wo_ref,
              sink_ref, bias_ref, wr_ref, br_ref, wu_ref, wd_ref, upper_ref, lower_ref,
              x1_ref, xl_ref, pos_ref, nb_ref, wub_ref, wdb_ref,
              h2_scr, route_scr, uc_ring, k_ring, vt_ring,
              *, n_seq_tiles):
    step = pl.program_id(0)
    t_idx = step % n_seq_tiles
    is_first = t_idx == 0
    is_last = t_idx == n_seq_tiles - 1
    own, prv, nxt = step % RING, (step + RING - 1) % RING, (step + 1) % RING

    permute, permute_tail = _sort_chunk(h2_scr[...], route_scr[...], upper_ref, lower_ref,
                                        xl_ref, pos_ref, nb_ref)

    @pl.when(step == 0)
    def _():
        uc_ring[...] = jnp.zeros_like(uc_ring)
        k_ring[...] = jnp.zeros_like(k_ring)
        vt_ring[...] = jnp.zeros_like(vt_ring)
        _project_a(x_ref, g1n_ref, sh1_ref, sc1_ref, wac_ref, cos_ref, sin_ref, 0,
                   uc_ring, k_ring, vt_ring)

    _project_a(xn_ref, g1n_ref, shn_ref, scn_ref, wac_ref, cosn_ref, sinn_ref, nxt,
               uc_ring, k_ring, vt_ring)
    x = x_ref[0]

    uc = uc_ring[own]
    row = lax.broadcasted_iota(I32, (SEQ_TILE, 1), 0)
    prev_row = jnp.where(is_first, 0.0, uc_ring[prv, SEQ_TILE - 1:SEQ_TILE, :])
    next_row = jnp.where(is_last, 0.0, uc_ring[nxt, 0:1, :])
    up = jnp.where(row == 0, prev_row, pltpu.roll(uc, 1, axis=0))
    dn = jnp.where(row == SEQ_TILE - 1, next_row, pltpu.roll(uc, SEQ_TILE - 1, axis=0))
    wconv = wconv_ref[...]
    y = bconv_ref[...] + (up * wconv[0:1, :] + uc * wconv[1:2, :] + dn * wconv[2:3, :])

    hb = _rms_mod(x, g1n_ref[...], sh1_ref[0], sc1_ref[0]).astype(BF16)
    bq = jnp.dot(hb, wb_ref[:, 0:2 * CONV_W], preferred_element_type=F32)
    ya = jnp.dot((bq[:, :CONV_W] * y).astype(BF16), wa_ref[...], preferred_element_type=F32)
    q = _rope(bq[:, CONV_W:], cos_ref[...], sin_ref[...]) * (HEAD_DIM ** -0.5 * LOG2E)

    k_all = jnp.concatenate([k_ring[prv, SEQ_TILE - BLOCK:SEQ_TILE, :], k_ring[own],
                             k_ring[nxt, 0:BLOCK, :]], axis=0)
    vt_all = jnp.concatenate([vt_ring[prv, :, SEQ_TILE - BLOCK:SEQ_TILE], vt_ring[own],
                              vt_ring[nxt, :, 0:BLOCK]], axis=1)
    k_ctx = kc_ref[0]
    vt_ctx = vtc_ref[0]
    bias_prev = bias_ref[0:BLOCK, :]
    bias_next = bias_ref[2 * BLOCK:3 * BLOCK, :]
    sink_row = sink_ref[...] * LOG2E
    n_units = BLOCKS_PER_TILE * N_KV_HEADS
    gate_cols = 2 * D_MODEL // n_units
    gate_chunks = []

    def gate_chunk(u):
        c0 = 2 * CONV_W + u * gate_cols
        gate_chunks.append(jnp.dot(hb, wb_ref[:, c0:c0 + gate_cols], preferred_element_type=F32))

    q_ts, biases = [], []
    for jb in range(BLOCKS_PER_TILE):
        bias_p, bias_n = bias_prev, bias_next
        if jb == 0:
            bias_p = bias_prev + jnp.where(is_first, NEG_INF, 0.0)
        if jb == BLOCKS_PER_TILE - 1:
            bias_n = bias_next + jnp.where(is_last, NEG_INF, 0.0)
        biases.append((jnp.concatenate([bias_p] * 4, axis=1), jnp.concatenate([bias_n] * 4, axis=1)))
        q_ts.append(q[jb * BLOCK:(jb + 1) * BLOCK, :].T.astype(BF16))

    scores, probs, outs = {}, {}, {}
    for t in range(n_units + 2):
        if t == 3:
            wub_ref[...] = wu_ref[...].astype(BF16)
        if t == 5:
            wdb_ref[...] = wd_ref[...].astype(BF16)
        if t == 0:
            permute(0, COMMON_ROWS // 2)
        if t == 4:
            permute(COMMON_ROWS // 2, COMMON_ROWS // 2)
        if t < n_units:
            jb, g = divmod(t, N_KV_HEADS)
            scores[t] = _attn_scores(q_ts[jb], g, k_all[jb * BLOCK:(jb + 3) * BLOCK, :], k_ctx,
                                     *biases[jb])
            gate_chunk(t)
        if 0 <= t - 1 < n_units:
            g = (t - 1) % N_KV_HEADS
            probs[t - 1] = _attn_probs(scores.pop(t - 1),
                                       sink_row[:, g * 4 * BLOCK:(g + 1) * 4 * BLOCK])
        if 0 <= t - 2 < n_units:
            jb, g = divmod(t - 2, N_KV_HEADS)
            outs[t - 2] = _attn_values(
                probs.pop(t - 2), sink_row[:, g * 4 * BLOCK:(g + 1) * 4 * BLOCK],
                vt_all[g * HEAD_DIM:(g + 1) * HEAD_DIM, jb * BLOCK:(jb + 3) * BLOCK],
                vt_ctx[g * HEAD_DIM:(g + 1) * HEAD_DIM, :])
    o_blocks = []
    for jb in range(BLOCKS_PER_TILE):
        o_rows = [outs[jb * N_KV_HEADS + g][:, h * BLOCK:(h + 1) * BLOCK]
                  for g in range(N_KV_HEADS) for h in range(4)]
        o_blocks.append(jnp.concatenate(o_rows, axis=0).T)
    gates = jnp.concatenate(gate_chunks, axis=1)

    o = jnp.concatenate(o_blocks, axis=0).astype(BF16)
    yb = jnp.dot(o, wbb_ref[...], preferred_element_type=F32)
    merged = (_sigmoid(gates[:, :D_MODEL]) * ya + _sigmoid(gates[:, D_MODEL:]) * yb).astype(BF16)
    x1 = x + gt1_ref[0] * jnp.dot(merged, wo_ref[...], preferred_element_type=F32)
    x1_ref[0] = x1
    h2 = _rms_mod(x1, g2n_ref[...], sh2_ref[0], sc2_ref[0]).astype(BF16)
    logits_t = lax.dot_general(wr_ref[...], h2, (((1,), (1,)), ((), ())),
                               preferred_element_type=F32) + br_ref[...]
    permute_tail()
    h2_scr[...] = h2
    route_scr[...] = _route(logits_t)


N_MIX_OUTPUTS = 6
N_MIX_SCRATCH = 5


def _mix_kernel(*refs, n_seq_tiles, batch):
    n_tail = N_MIX_OUTPUTS + N_MIX_SCRATCH
    upper_ref, lower_ref = refs[-n_tail - 2:-n_tail]
    _, xl_ref, pos_ref, nb_ref, _, _ = refs[-n_tail:-N_MIX_SCRATCH]
    h2_scr, route_scr = refs[-N_MIX_SCRATCH:-N_MIX_SCRATCH + 2]
    step = pl.program_id(0)
    n_main = n_seq_tiles * batch

    @pl.when(step == 0)
    def _():
        h2_scr[...] = jnp.zeros_like(h2_scr)
        route_scr[...] = jnp.zeros_like(route_scr)

    @pl.when(step < n_main)
    def _():
        _mix_step(*refs, n_seq_tiles=n_seq_tiles)

    @pl.when(step == n_main)
    def _():
        permute, permute_tail = _sort_chunk(h2_scr[...], route_scr[...], upper_ref, lower_ref,
                                            xl_ref, pos_ref, nb_ref)
        permute(0, COMMON_ROWS)
        permute_tail()


def _mix(x, norm1_g, norm2_g, mod3, w_a_cols, w_b_cols, cos_t, sin_t, k_ctx, vt_ctx,
         w_conv, b_conv, w_a, w_b, w_o, sink_row, bias, wr_t, br_col, w_up, w_down, upper, lower):
    bsz, seq, _ = x.shape
    nt = seq // SEQ_TILE
    clen = k_ctx.shape[1]
    nch = n_main = bsz * nt
    assert N_EXPERTS % n_main == 0, "each grid step converts an equal share of the experts"
    e_step = N_EXPERTS // n_main

    def tile(s):
        return divmod(jnp.clip(s, 0, n_main - 1), nt)

    def per_tile(shape, index, shift=0):
        return pl.BlockSpec(shape, lambda s: index(*tile(s + shift)))

    def const_spec(shape):
        return pl.BlockSpec(shape, lambda s: tuple(0 for _ in shape),
                            pipeline_mode=pl.Buffered(1))

    def x_spec(shift=0):
        return per_tile((1, SEQ_TILE, D_MODEL), lambda b, t: (b, t, 0), shift)

    def mod_spec(j, shift=0):
        return per_tile((1, 1, D_MODEL), lambda b, t: (b, 0, j), shift)

    def rope_spec(shift=0):
        return per_tile((SEQ_TILE, LANES), lambda b, t: (t, 0), shift)

    def expert_spec(w):
        return pl.BlockSpec((e_step,) + w.shape[1:], lambda s: (jnp.minimum(s, n_main - 1), 0, 0))

    def chunk_spec(shape):
        return pl.BlockSpec(shape, lambda s: (jnp.maximum(s - 1, 0),) + (0,) * (len(shape) - 1))

    in_specs = [
        x_spec(), const_spec((1, D_MODEL)), const_spec((1, D_MODEL)),
        mod_spec(0), mod_spec(1), mod_spec(2), mod_spec(3), mod_spec(4),
        const_spec(w_b_cols.shape), rope_spec(), rope_spec(),
        x_spec(1), mod_spec(0, 1), mod_spec(1, 1), rope_spec(1), rope_spec(1),
        const_spec(w_a_cols.shape),
        per_tile((1, clen, KV_W), lambda b, t: (b, 0, 0)),
        per_tile((1, KV_W, clen), lambda b, t: (b, 0, 0)),
        const_spec(w_conv.shape), const_spec(b_conv.shape),
        const_spec(w_a.shape), const_spec(w_b.shape), const_spec(w_o.shape),
        const_spec(sink_row.shape), const_spec(bias.shape),
        const_spec(wr_t.shape), const_spec(br_col.shape),
        expert_spec(w_up), expert_spec(w_down),
        const_spec(upper.shape), const_spec(lower.shape),
    ]
    out_specs = [
        x_spec(), chunk_spec((CHUNK_ROWS, XL_COLS)), chunk_spec((CHUNK, LANES)),
        chunk_spec((1, N_EXPERTS, LANES)), expert_spec(w_up), expert_spec(w_down),
    ]
    out_shape = [
        jax.ShapeDtypeStruct((bsz, seq, D_MODEL), F32),
        jax.ShapeDtypeStruct((nch * CHUNK_ROWS, XL_COLS), BF16),
        jax.ShapeDtypeStruct((nch * CHUNK, LANES), F32),
        jax.ShapeDtypeStruct((nch, N_EXPERTS, LANES), I32),
        jax.ShapeDtypeStruct(w_up.shape, BF16),
        jax.ShapeDtypeStruct(w_down.shape, BF16),
    ]
    return pl.pallas_call(
        functools.partial(_mix_kernel, n_seq_tiles=nt, batch=bsz),
        grid=(n_main + 1,),
        in_specs=in_specs,
        out_specs=out_specs,
        out_shape=out_shape,
        scratch_shapes=[pltpu.VMEM((CHUNK, D_MODEL), BF16), pltpu.VMEM((ROUTE_ROWS, CHUNK), F32),
                        pltpu.VMEM((RING, SEQ_TILE, CONV_W), F32),
                        pltpu.VMEM((RING, SEQ_TILE, KV_W), BF16),
                        pltpu.VMEM((RING, KV_W, SEQ_TILE), BF16)],
        compiler_params=_cparams(("arbitrary",)),
        name="token_mix",
    )(x, norm1_g, norm2_g, mod3, mod3, mod3, mod3, mod3, w_b_cols, cos_t, sin_t,
      x, mod3, mod3, cos_t, sin_t, w_a_cols, k_ctx, vt_ctx,
      w_conv, b_conv, w_a, w_b, w_o, sink_row, bias, wr_t, br_col, w_up, w_down, upper, lower)


def _bf16_parts(v):
    hi = v.astype(BF16).astype(F32)
    r1 = v - hi
    mid = r1.astype(BF16).astype(F32)
    lo = (r1 - mid).astype(BF16).astype(F32)
    return hi, mid, lo


def _sort_chunk(h, route, upper_ref, lower_ref, xl_ref, pos_ref, nb_ref):
    e1 = route[0:1, :].astype(I32)
    e2 = route[1:2, :].astype(I32)
    erow = lax.broadcasted_iota(I32, (N_EXPERTS, CHUNK), 0)
    hit1 = erow == e1
    hit2 = erow == e2
    onehot = jnp.where(hit1, 1.0, 0.0) + jnp.where(hit2, 1.0, 0.0)
    cum = jnp.dot(onehot.astype(BF16), upper_ref[...], preferred_element_type=F32)
    cnt = jnp.sum(onehot, axis=1, keepdims=True)
    nblk = jnp.floor((cnt + (ROW_BLOCK - 1)) * (1.0 / ROW_BLOCK))
    nblk_b = jnp.broadcast_to(nblk, (N_EXPERTS, LANES))
    seg = jnp.dot(lower_ref[...], nblk_b.astype(BF16), preferred_element_type=F32) * ROW_BLOCK
    base = seg[:, 0:1] + cum
    pos1 = jnp.sum(jnp.where(hit1, base, 0.0), axis=0, keepdims=True)
    pos2 = jnp.sum(jnp.where(hit2, base, 0.0), axis=0, keepdims=True)
    p1i = pos1.astype(I32)
    p2i = pos2.astype(I32)
    used_rows = (jnp.sum(nblk) * ROW_BLOCK).astype(I32)

    prow = lax.broadcasted_iota(I32, (LANES, CHUNK), 0)
    parts = _bf16_parts(route[2:3, :]) + _bf16_parts(route[3:4, :])
    gpart_rows = jnp.zeros((LANES, CHUNK), F32)
    for j, part in enumerate(parts):
        gpart_rows = jnp.where(prow == j, part, gpart_rows)
    h_ext = jnp.concatenate([h, gpart_rows.T.astype(BF16)], axis=1)
    glane = lax.broadcasted_iota(I32, (1, GATE_COLS), 1)
    one = jnp.ones((), BF16)
    zero = jnp.zeros((), BF16)
    def permute(row0, n_rows):
        r16 = lax.broadcasted_iota(I32, (n_rows, CHUNK), 0).astype(jnp.int16)
        q1 = (p1i - row0).astype(jnp.int16)
        q2 = (p2i - row0).astype(jnp.int16)
        sel1 = jnp.where(r16 == q1, one, zero)
        sel = jnp.where(r16 == q2, one, sel1)
        is_slot1 = jnp.max(sel1, axis=1, keepdims=True)
        rows = pl.ds(row0, n_rows)
        xg = jnp.dot(sel, h_ext, preferred_element_type=F32)
        xl_ref[rows, 0:D_MODEL] = xg[:, :D_MODEL].astype(BF16)
        g6 = xg[:, D_MODEL:]
        g3 = jnp.where(is_slot1.astype(F32) > 0.0, g6, pltpu.roll(g6, GATE_COLS - 3, axis=1))
        xl_ref[rows, D_MODEL:XL_COLS] = jnp.where(glane < 3, g3, 0.0).astype(BF16)

    pos_rows = jnp.where(prow == 0, pos1, jnp.where(prow == 1, pos2, 0.0))
    pos_ref[...] = pos_rows.T
    nb_ref[0] = nblk_b.astype(I32)

    def permute_tail():
        tail_rows = CHUNK_ROWS - COMMON_ROWS

        @pl.when(COMMON_ROWS < used_rows)
        def _():
            permute(COMMON_ROWS, tail_rows)

        @pl.when(COMMON_ROWS >= used_rows)
        def _():
            xl_ref[pl.ds(COMMON_ROWS, tail_rows), :] = jnp.zeros((tail_rows, XL_COLS), BF16)

    return permute, permute_tail


def _max_tiles(nch):
    max_blocks = nch * (2 * CHUNK + N_EXPERTS * (ROW_BLOCK - 1)) // ROW_BLOCK
    return max_blocks // TILE_BLOCKS + N_EXPERTS


def _masked_prefix(le, values):
    delta = values - jnp.concatenate([jnp.zeros((1,), values.dtype), values[:-1]])
    return jnp.sum(jnp.where(le, delta[None, :], 0), axis=1)


def _tile_plan(nb, nch):
    n_tiles = _max_tiles(nch)
    nbt = nb.T
    nbe = jnp.sum(nbt, axis=1)
    nte = (nbe + TILE_BLOCKS - 1) // TILE_BLOCKS
    tile_end = jnp.cumsum(nte)
    tile_start = tile_end - nte
    n_act = tile_end[-1]
    tiles = jnp.arange(n_tiles, dtype=I32)
    te = jnp.sum((tile_end[None, :] <= tiles[:, None]).astype(I32), axis=1)
    e_ar = jnp.arange(N_EXPERTS, dtype=I32)
    active = nte > 0
    te = jnp.where(tiles < n_act, te, jnp.max(jnp.where(active, e_ar, 0)))
    first = jnp.logical_and(te != jnp.concatenate([jnp.full((1,), -1, I32), te[:-1]]),
                            tiles < n_act).astype(I32)
    rank = jnp.cumsum(active.astype(I32)) - 1
    k_ar = jnp.arange(N_EXPERTS + W_SLOTS, dtype=I32)
    eseq = jnp.sum(jnp.where(jnp.logical_and(active[None, :], rank[None, :] == k_ar[:, None]),
                             e_ar[None, :], 0), axis=1)
    n_exp = jnp.sum(active.astype(I32))

    cb_excl = jnp.cumsum(nbt, axis=1) - nbt
    gs = TILE_BLOCKS * tile_start[:, None] + cb_excl
    seg_blk = jnp.cumsum(nb, axis=1) - nb
    base_blk = jnp.arange(nch, dtype=I32)[None, :] * CHUNK_BLOCKS + seg_blk.T
    gs_f = gs.reshape(-1)
    slots = jnp.arange(n_tiles * TILE_BLOCKS, dtype=I32)
    le = gs_f[None, :] <= slots[:, None]
    blk = slots + _masked_prefix(le, (base_blk - gs).reshape(-1))
    valid = slots < _masked_prefix(le, (gs + nbt).reshape(-1))
    gblk = jnp.concatenate([jnp.where(valid, blk, 0).astype(I32),
                            jnp.zeros(((X_SLOTS - 1) * TILE_BLOCKS,), I32)])

    m = jnp.arange(CHUNK_BLOCKS, dtype=I32)
    le_c = seg_blk[:, None, :] <= m[None, :, None]
    shift = (gs.T - seg_blk)
    delta = shift - jnp.concatenate([jnp.zeros((nch, 1), I32), shift[:, :-1]], axis=1)
    slot_of = m[None, :] + jnp.sum(jnp.where(le_c, delta[:, None, :], 0), axis=2)
    used_blocks = jnp.sum(nb, axis=1)
    slot_of = jnp.where(m[None, :] < used_blocks[:, None], slot_of, 0).astype(I32)
    pad = Y_RING - 1
    slot_of = jnp.concatenate([slot_of.reshape(-1), jnp.zeros((pad * CHUNK_BLOCKS,), I32)])
    used_rows = jnp.concatenate([ROW_BLOCK * used_blocks, jnp.zeros((pad,), I32)]).astype(I32)
    return ((n_act.reshape(1).astype(I32), n_exp.reshape(1), first, eseq, gblk),
            (used_rows, slot_of))


def _gmm_kernel(nact_ref, nexp_ref, first_ref, eseq_ref, gblk_ref,
                xl_hbm, wu_hbm, wd_hbm, yt_hbm,
                xbuf, ybuf, zbuf, wu_st, wd_st, in_sem, out_sem, w_sem, z_sem):
    n_act = nact_ref[0]
    n_exp = nexp_ref[0]
    n_tiles = yt_hbm.shape[0] // TILE_BLOCKS
    prefetch = X_SLOTS - 1

    def gather(tile, b):
        s = tile % X_SLOTS
        blk = gblk_ref[tile * TILE_BLOCKS + b]
        return pltpu.make_async_copy(xl_hbm.at[blk], xbuf.at[s, b], in_sem.at[s])

    def write_back(tile):
        s = tile % Y_SLOTS
        return pltpu.make_async_copy(ybuf.at[s], yt_hbm.at[pl.ds(tile * TILE_BLOCKS, TILE_BLOCKS)],
                                     out_sem.at[s])

    def zero_fill(tile):
        return pltpu.make_async_copy(zbuf, yt_hbm.at[pl.ds(tile * TILE_BLOCKS, TILE_BLOCKS)], z_sem)

    def weights(q):
        e = eseq_ref[q]
        s = q % W_SLOTS
        return (pltpu.make_async_copy(wu_hbm.at[e], wu_st.at[s], w_sem.at[0, s]),
                pltpu.make_async_copy(wd_hbm.at[e], wd_st.at[s], w_sem.at[1, s]))

    def tile_blocks(fn):
        for b in range(TILE_BLOCKS):
            fn(b)

    for q0 in range(W_SLOTS - 1):
        @pl.when(q0 < n_exp)
        def _(q0=q0):
            for cp in weights(q0):
                cp.start()
    for t0 in range(prefetch):
        tile_blocks(lambda b, t0=t0: gather(t0, b).start())

    def body(i, q):
        tile_blocks(lambda b: gather(i, b).wait())

        @pl.when(i >= Y_SLOTS)
        def _():
            write_back(i - Y_SLOTS).wait()

        is_first = first_ref[i] == 1

        @pl.when(is_first)
        def _():
            for cp in weights(q):
                cp.wait()

            @pl.when(q + W_SLOTS - 1 < n_exp)
            def _():
                for cp in weights(q + W_SLOTS - 1):
                    cp.start()

        q = q + is_first.astype(I32)
        ws = (q - 1) % W_SLOTS
        xs = i % X_SLOTS
        ys = i % Y_SLOTS

        x = xbuf[xs].reshape(TILE_ROWS, XL_COLS)
        gate = jnp.sum(x[:, D_MODEL:].astype(F32), axis=1, keepdims=True)
        au = jnp.dot(x[:, :D_MODEL], wu_st[ws], preferred_element_type=F32)
        tile_blocks(lambda b: gather(i + prefetch, b).start())
        a = au[:, :EXPERT_FF]
        act = (a * _sigmoid(a)) * au[:, EXPERT_FF:]
        y = jnp.dot(act.astype(BF16), wd_st[ws], preferred_element_type=F32)
        ybuf[ys] = (gate * y).astype(BF16).reshape(TILE_BLOCKS, ROW_BLOCK, D_MODEL)
        write_back(i).start()

        @pl.when(n_act + i < n_tiles)
        def _():
            zero_fill(n_act + i).start()
        return q

    zbuf[...] = jnp.zeros_like(zbuf)
    lax.fori_loop(0, n_act, body, jnp.int32(0))

    def fill_rest(t, carry):
        zero_fill(t).start()
        return carry

    lax.fori_loop(jnp.minimum(2 * n_act, n_tiles), n_tiles, fill_rest, 0)

    def drain(t, carry):
        zero_fill(t).wait()
        return carry

    lax.fori_loop(n_act, n_tiles, drain, 0)
    for k in range(prefetch):
        tile_blocks(lambda b, k=k: gather(n_act + k, b).wait())
    for k in range(Y_SLOTS):
        @pl.when(n_act - 1 - k >= 0)
        def _(k=k):
            write_back(n_act - 1 - k).wait()


def _grouped_mlp(plan, xl, w_up, w_down, n_tiles):
    grid_spec = pltpu.PrefetchScalarGridSpec(
        num_scalar_prefetch=len(plan),
        grid=(1,),
        in_specs=[pl.BlockSpec(memory_space=pl.ANY)] * 3,
        out_specs=pl.BlockSpec(memory_space=pl.ANY),
        scratch_shapes=[pltpu.VMEM((X_SLOTS, TILE_BLOCKS, ROW_BLOCK, XL_COLS), BF16),
                        pltpu.VMEM((Y_SLOTS, TILE_BLOCKS, ROW_BLOCK, D_MODEL), BF16),
                        pltpu.VMEM((TILE_BLOCKS, ROW_BLOCK, D_MODEL), BF16),
                        pltpu.VMEM((W_SLOTS, D_MODEL, 2 * EXPERT_FF), BF16),
                        pltpu.VMEM((W_SLOTS, EXPERT_FF, D_MODEL), BF16),
                        pltpu.SemaphoreType.DMA((X_SLOTS,)),
                        pltpu.SemaphoreType.DMA((Y_SLOTS,)),
                        pltpu.SemaphoreType.DMA((2, W_SLOTS)),
                        pltpu.SemaphoreType.DMA(())])
    return pl.pallas_call(
        _gmm_kernel,
        grid_spec=grid_spec,
        out_shape=jax.ShapeDtypeStruct((n_tiles * TILE_BLOCKS, ROW_BLOCK, D_MODEL), BF16),
        compiler_params=_cparams(("arbitrary",)),
        name="moe_experts",
    )(*plan, xl, w_up, w_down)


def _combine_kernel(used_ref, slot_ref, yt_hbm, pos_ref, x1_ref, gt2_ref, fg_ref, o_ref, ybuf, sem):
    c = pl.program_id(0)
    n_chunks = pl.num_programs(0)
    used_rows = used_ref[c]

    def chunk_blocks(chunk, op):
        s = chunk % Y_RING

        def copy(b):
            slot = slot_ref[chunk * CHUNK_BLOCKS + b]
            return pltpu.make_async_copy(yt_hbm.at[slot], ybuf.at[s, b], sem.at[s])

        for b in range(COMMON_BLOCKS):
            op(copy(b))

        @pl.when(used_ref[chunk] > COMMON_ROWS)
        def _():
            for b in range(COMMON_BLOCKS, CHUNK_BLOCKS):
                op(copy(b))

    @pl.when(c == 0)
    def _():
        for k in range(Y_RING - 1):
            chunk_blocks(k, lambda cp: cp.start())

    chunk_blocks(c + Y_RING - 1, lambda cp: cp.start())
    chunk_blocks(c, lambda cp: cp.wait())

    def body(k_rows):
        pos = pos_ref[...]
        p1 = pos[:, 0:1].astype(I32)
        p2 = pos[:, 1:2].astype(I32)
        r = lax.broadcasted_iota(I32, (CHUNK, k_rows), 1)
        sel = jnp.where(r == p1, 1.0, jnp.where(r == p2, 1.0, 0.0)).astype(BF16)
        yl = ybuf[c % Y_RING, 0:k_rows // ROW_BLOCK].reshape(k_rows, D_MODEL)
        y = jnp.dot(sel, yl, preferred_element_type=F32)
        x2 = x1_ref[...] + gt2_ref[0] * y
        inv = lax.rsqrt(jnp.mean(x2 * x2, axis=-1, keepdims=True) + NORM_EPS)
        o_ref[...] = (x2 * inv) * fg_ref[...]

    pl.when(used_rows <= COMMON_ROWS)(lambda: body(COMMON_ROWS))
    pl.when(used_rows > COMMON_ROWS)(lambda: body(CHUNK_ROWS))

    @pl.when(c == n_chunks - 1)
    def _():
        for k in range(1, Y_RING):
            chunk_blocks(c + k, lambda cp: cp.wait())


def _combine(used_rows, slot_of, yt, pos_c, x1, mod3, final_g, seq):
    n = x1.shape[0]
    per_seq = seq // CHUNK
    grid_spec = pltpu.PrefetchScalarGridSpec(
        num_scalar_prefetch=2,
        grid=(n // CHUNK,),
        in_specs=[pl.BlockSpec(memory_space=pl.ANY),
                  pl.BlockSpec((CHUNK, LANES), lambda c, u, s: (c, 0)),
                  pl.BlockSpec((CHUNK, D_MODEL), lambda c, u, s: (c, 0)),
                  pl.BlockSpec((1, 1, D_MODEL), lambda c, u, s: (c // per_seq, 0, 5)),
                  pl.BlockSpec((1, D_MODEL), lambda c, u, s: (0, 0))],
        out_specs=pl.BlockSpec((CHUNK, D_MODEL), lambda c, u, s: (c, 0)),
        scratch_shapes=[pltpu.VMEM((Y_RING, CHUNK_BLOCKS, ROW_BLOCK, D_MODEL), BF16),
                        pltpu.SemaphoreType.DMA((Y_RING,))])
    return pl.pallas_call(
        _combine_kernel,
        grid_spec=grid_spec,
        out_shape=jax.ShapeDtypeStruct((n, D_MODEL), F32),
        compiler_params=_cparams(("arbitrary",)),
        name="moe_combine",
    )(used_rows, slot_of, yt, pos_c, x1, mod3, final_g)


def _rope_tables(seq):
    n_freq = HEAD_DIM // 4
    inv_freq = ROPE_BASE ** (-jnp.arange(n_freq, dtype=F32) / n_freq)
    rows = seq // GRID_W
    row = jnp.repeat(jnp.arange(rows, dtype=F32), GRID_W)
    col = jnp.tile(jnp.arange(GRID_W, dtype=F32), rows)
    ang = jnp.concatenate([row[:, None] * inv_freq, col[:, None] * inv_freq], axis=-1)
    cos, sin = jnp.cos(ang), jnp.sin(ang)
    return jnp.tile(cos, (1, 4)), jnp.concatenate([-sin, sin, -sin, sin], axis=1)


def _window_bias():
    key = np.arange(3 * BLOCK)[:, None]
    qry = np.arange(BLOCK)[None, :]
    valid = (key - qry >= 0) & (key - qry <= 2 * BLOCK)
    return jnp.asarray(np.where(valid, 0.0, NEG_INF), F32)


def kernel(x, c, ctx, c_ctx, w_ada, b_ada, norm1_g, w_in, w_conv, b_conv, w_a, w_b, sink, w_o,
           norm2_g, w_group, b_group, w_router, b_router, w_up, w_down, final_g):
    bsz, seq, _ = x.shape
    assert w_ada.shape[0] == 1 and seq % SEQ_TILE == 0

    cc = jnp.zeros((16, D_MODEL), F32).at[:bsz].set(c).at[bsz].set(c_ctx)
    mod = _modulation(cc, w_ada[0], b_ada[0][None, :])
    mod3 = mod[:bsz].reshape(bsz, 1, N_MOD * D_MODEL)
    csh1 = mod[bsz:bsz + 1, 0:D_MODEL]
    csc1 = mod[bsz:bsz + 1, D_MODEL:2 * D_MODEL]

    w = w_in[0]
    n1g = norm1_g[0][None, :]
    n2g = norm2_g[0][None, :]
    w_kv = w[:, OFF_K:OFF_GA].astype(BF16)
    w_a_cols = jnp.concatenate([w[:, OFF_CG:OFF_Q], w[:, OFF_K:OFF_GA]], axis=1).astype(BF16)
    w_b_cols = jnp.concatenate([w[:, OFF_B:OFF_CG], w[:, OFF_Q:OFF_K], w[:, OFF_GA:]],
                               axis=1).astype(BF16)
    cos_t, sin_t = _rope_tables(seq)

    k_ctx, vt_ctx = _context_kv(ctx, n1g, csh1, csc1, w_kv)
    sink_row = jnp.repeat(sink[0].astype(F32), BLOCK)[None, :]
    pad_g = jnp.zeros((D_MODEL, GROUP_ROWS - N_GROUPS), F32)
    pad_e = jnp.zeros((D_MODEL, ROUTER_ROWS - GROUP_ROWS - N_EXPERTS), F32)
    wr_t = jnp.concatenate([w_group[0], pad_g, w_router[0], pad_e], axis=1).T.astype(BF16)
    br_col = jnp.concatenate([b_group[0], pad_g[0], b_router[0], pad_e[0]])[:, None]

    upper = jnp.asarray(np.triu(np.ones((CHUNK, CHUNK), np.float32), 1), BF16)
    lower = jnp.asarray(np.tril(np.ones((N_EXPERTS, N_EXPERTS), np.float32), -1), BF16)
    x1, xl, pos_c, nb, w_up_bf, w_down_bf = _mix(
        x, n1g, n2g, mod3, w_a_cols, w_b_cols, cos_t, sin_t, k_ctx, vt_ctx,
        w_conv[0], b_conv[0][None, :], w_a[0].astype(BF16), w_b[0].astype(BF16),
        w_o[0].astype(BF16), sink_row, _window_bias(), wr_t, br_col, w_up[0], w_down[0],
        upper, lower)

    n = bsz * seq
    nch = n // CHUNK
    nb = nb[:, :, 0]
    expert_plan, combine_plan = _tile_plan(nb, nch)
    yt = _grouped_mlp(expert_plan, xl.reshape(-1, ROW_BLOCK, XL_COLS), w_up_bf, w_down_bf,
                      _max_tiles(nch))
    out = _combine(*combine_plan, yt, pos_c, x1.reshape(n, D_MODEL), mod3, final_g[None, :], seq)
    return out.reshape(bsz, seq, D_MODEL)
```

```python
import functools

import numpy as np
import jax
import jax.numpy as jnp
from jax import lax
from jax.experimental import pallas as pl
from jax.experimental.pallas import tpu as pltpu

F32 = jnp.float32
BF16 = jnp.bfloat16
I32 = jnp.int32

D_MODEL = 1024
GRID_W = 64
CONV_W = 512
N_HEADS = 8
N_KV_HEADS = 2
HEAD_DIM = 64
ATT_W = N_HEADS * HEAD_DIM
KV_W = N_KV_HEADS * HEAD_DIM
BLOCK = 128
ROPE_BASE = 10000.0
N_GROUPS = 4
EXPERTS_PER_GROUP = 8
N_EXPERTS = N_GROUPS * EXPERTS_PER_GROUP
EXPERT_FF = 256
N_MOD = 6
NORM_EPS = 1e-6
NEG_INF = -1e30
LOG2E = 1.4426950408889634
ONES_ROWS = 16

OFF_B, OFF_CG, OFF_XIN, OFF_Q, OFF_K, OFF_V, OFF_GA, OFF_GB = (
    0, 512, 1024, 1536, 2048, 2176, 2304, 3328)
IN_COLS = 4352

LANES = 128
SEQ_TILE = 512
BLOCKS_PER_TILE = SEQ_TILE // BLOCK
ROUTE_ROWS = 8
GROUP_ROWS = 8
ROUTER_ROWS = 48
VMEM_LIMIT = 62 * 1024 * 1024

CHUNK = SEQ_TILE
ROW_BLOCK = 16
CHUNK_ROWS = -(-(2 * CHUNK + N_EXPERTS * (ROW_BLOCK - 1)) // 256) * 256
CHUNK_BLOCKS = CHUNK_ROWS // ROW_BLOCK
GATE_COLS = LANES
XL_COLS = D_MODEL + GATE_COLS
COMMON_ROWS = 2 * CHUNK + 256
COMMON_BLOCKS = COMMON_ROWS // ROW_BLOCK
TILE_BLOCKS = 32
TILE_ROWS = TILE_BLOCKS * ROW_BLOCK
X_SLOTS = 6
Y_SLOTS = 4
W_SLOTS = 3
Y_RING = 3
RING = 3


def _cparams(sem):
    return pltpu.CompilerParams(dimension_semantics=sem, vmem_limit_bytes=VMEM_LIMIT)


def _rms_mod(x, g, shift, scale):
    inv = lax.rsqrt(jnp.mean(x * x, axis=-1, keepdims=True) + NORM_EPS)
    return (x * inv) * (g * (1.0 + scale)) + shift


def _mod_kernel(c_ref, w_ref, b_ref, o_ref):
    c = c_ref[...]
    a = (c * jax.nn.sigmoid(c)).astype(BF16)
    o_ref[...] = jnp.dot(a, w_ref[...].astype(BF16), preferred_element_type=F32) + b_ref[...]


def _modulation(cc, w_ada, b_ada):
    rows = cc.shape[0]
    cols = w_ada.shape[1]
    tile = 1024
    return pl.pallas_call(
        _mod_kernel,
        grid=(cols // tile,),
        in_specs=[pl.BlockSpec((rows, D_MODEL), lambda j: (0, 0)),
                  pl.BlockSpec((D_MODEL, tile), lambda j: (0, j)),
                  pl.BlockSpec((1, tile), lambda j: (0, j))],
        out_specs=pl.BlockSpec((rows, tile), lambda j: (0, j)),
        out_shape=jax.ShapeDtypeStruct((rows, cols), F32),
        compiler_params=_cparams(("arbitrary",)),
        name="adaln_mod",
    )(cc, w_ada, b_ada)


def _ctx_kernel(x_ref, g_ref, sh_ref, sc_ref, w_ref, k_ref, vt_ref):
    h = _rms_mod(x_ref[0], g_ref[...], sh_ref[...], sc_ref[...]).astype(BF16)
    kv = jnp.dot(h, w_ref[...], preferred_element_type=F32)
    k_ref[0] = kv[:, :KV_W].astype(BF16)
    vt_ref[0] = kv[:, KV_W:].T.astype(BF16)


def _context_kv(ctx, norm_g, csh, csc, w_kv):
    bsz, clen, _ = ctx.shape
    return pl.pallas_call(
        _ctx_kernel,
        grid=(bsz,),
        in_specs=[pl.BlockSpec((1, clen, D_MODEL), lambda b: (b, 0, 0)),
                  pl.BlockSpec((1, D_MODEL), lambda b: (0, 0)),
                  pl.BlockSpec((1, D_MODEL), lambda b: (0, 0)),
                  pl.BlockSpec((1, D_MODEL), lambda b: (0, 0)),
                  pl.BlockSpec((D_MODEL, 2 * KV_W), lambda b: (0, 0))],
        out_specs=[pl.BlockSpec((1, clen, KV_W), lambda b: (b, 0, 0)),
                   pl.BlockSpec((1, KV_W, clen), lambda b: (b, 0, 0))],
        out_shape=[jax.ShapeDtypeStruct((bsz, clen, KV_W), BF16),
                   jax.ShapeDtypeStruct((bsz, KV_W, clen), BF16)],
        compiler_params=_cparams(("arbitrary",)),
        name="context_kv",
    )(ctx, norm_g, csh, csc, w_kv)


def _rope(t, cos, sin_signed):
    lane = lax.broadcasted_iota(I32, (1, LANES), 1)
    first_half = (lane % HEAD_DIM) < (HEAD_DIM // 2)
    outs = []
    for j in range(t.shape[1] // LANES):
        tj = t[:, j * LANES:(j + 1) * LANES]
        partner = jnp.where(first_half,
                            pltpu.roll(tj, LANES - HEAD_DIM // 2, axis=1),
                            pltpu.roll(tj, HEAD_DIM // 2, axis=1))
        outs.append(tj * cos + partner * sin_signed)
    return outs[0] if len(outs) == 1 else jnp.concatenate(outs, axis=1)


def _sigmoid(x):
    return 1.0 / (1.0 + jnp.exp2(x * (-LOG2E)))


def _attn_scores(q_t, g, k_win, k_ctx, bias_p4, bias_n4):
    zeros = jnp.zeros((HEAD_DIM, 4 * BLOCK), BF16)
    qg = jnp.concatenate([q_t[(4 * g + h) * HEAD_DIM:(4 * g + h + 1) * HEAD_DIM, :]
                          for h in range(4)], axis=1)
    rhs = jnp.concatenate([qg, zeros] if g == 0 else [zeros, qg], axis=0)
    s_win = jnp.dot(k_win, rhs, preferred_element_type=F32)
    s_ctx = jnp.dot(k_ctx, rhs, preferred_element_type=F32)
    return (s_win[0:BLOCK] + bias_p4, s_win[BLOCK:2 * BLOCK], s_win[2 * BLOCK:] + bias_n4, s_ctx)


def _attn_probs(scores, sink):
    m = sink
    for s in scores:
        m = jnp.maximum(m, jnp.max(s, axis=0, keepdims=True))
    p_win = jnp.concatenate([jnp.exp2(s - m).astype(BF16) for s in scores[:3]], axis=0)
    p_ctx = jnp.exp2(scores[3] - m).astype(BF16)
    return p_win, p_ctx, m


def _attn_values(probs, sink, vt_win_g, vt_ctx_g):
    p_win, p_ctx, m = probs

    def with_ones(vt):
        r = lax.broadcasted_iota(I32, (ONES_ROWS, vt.shape[1]), 0)
        return jnp.concatenate([vt, jnp.where(r == 0, 1.0, 0.0).astype(BF16)], axis=0)

    o_ext = (jnp.dot(with_ones(vt_win_g), p_win, preferred_element_type=F32)
             + jnp.dot(with_ones(vt_ctx_g), p_ctx, preferred_element_type=F32))
    denom = o_ext[HEAD_DIM:HEAD_DIM + 1, :] + jnp.exp2(sink - m)
    return o_ext[:HEAD_DIM, :] / denom


def _route(logits_t):
    t = logits_t.shape[1]
    grow = lax.broadcasted_iota(I32, (GROUP_ROWS, t), 0)
    gl = jnp.where(grow < N_GROUPS, logits_t[0:GROUP_ROWS, :], NEG_INF)
    gm = jnp.max(gl, axis=0, keepdims=True)
    p_g = 1.0 / jnp.sum(jnp.exp(gl - gm), axis=0, keepdims=True)
    g_idx = jnp.min(jnp.where(gl == gm, grow, N_GROUPS), axis=0, keepdims=True)

    erow = lax.broadcasted_iota(I32, (N_EXPERTS, t), 0)
    el = logits_t[GROUP_ROWS:GROUP_ROWS + N_EXPERTS, :]
    sel = (erow // EXPERTS_PER_GROUP) == g_idx
    em = jnp.where(sel, el, NEG_INF)
    m1 = jnp.max(em, axis=0, keepdims=True)
    i1 = jnp.min(jnp.where(em == m1, erow, N_EXPERTS), axis=0, keepdims=True)
    em2 = jnp.where(erow == i1, NEG_INF, em)
    m2 = jnp.max(em2, axis=0, keepdims=True)
    i2 = jnp.min(jnp.where(em2 == m2, erow, N_EXPERTS), axis=0, keepdims=True)
    z = jnp.sum(jnp.where(sel, jnp.exp(el - m1), 0.0), axis=0, keepdims=True)
    p1 = 1.0 / z
    p2 = jnp.exp(m2 - m1) / z
    gate1 = p_g * p1 / (p1 + p2)
    gate2 = p_g * p2 / (p1 + p2)
    pad = jnp.zeros((ROUTE_ROWS - 4, t), F32)
    return jnp.concatenate([i1.astype(F32), i2.astype(F32), gate1, gate2, pad], axis=0)


def _project_a(x_ref, g_ref, sh_ref, sc_ref, w_ref, cos_ref, sin_ref, slot, uc_slot,
               uc_ring, k_ring, vt_ring):
    h = _rms_mod(x_ref[0], g_ref[...], sh_ref[0], sc_ref[0]).astype(BF16)
    u = jnp.dot(h, w_ref[...], preferred_element_type=F32)
    uc_ring[uc_slot] = u[:, :CONV_W] * u[:, CONV_W:2 * CONV_W]
    k = u[:, 2 * CONV_W:2 * CONV_W + KV_W]
    k_ring[slot] = _rope(k, cos_ref[...], sin_ref[...]).astype(BF16)
    vt_ring[slot] = u[:, 2 * CONV_W + KV_W:].T.astype(BF16)


def _mix_step(x_ref, g1n_ref, g2n_ref, sh1_ref, sc1_ref, gt1_ref, sh2_ref, sc2_ref,
              wb_ref, cos_ref, sin_ref,
              xn_ref, shn_ref, scn_ref, cosn_ref, sinn_ref, wac_ref,
              kc_ref, vtc_ref, wconv_ref, bconv_ref, wa_ref, wbb_ref, wo_ref,
              sink_ref, bias_ref, wr_ref, br_ref, wu_ref, wd_ref, upper_ref, lower_ref,
              x1_ref, xl_ref, pos_ref, nb_ref, wub_ref, wdb_ref,
              h2_scr, route_scr, uc_ring, uc_last, k_ring, vt_ring,
              *, n_seq_tiles):
    step = pl.program_id(0)
    t_idx = step % n_seq_tiles
    is_first = t_idx == 0
    is_last = t_idx == n_seq_tiles - 1
    own, prv, nxt = step % RING, (step + RING - 1) % RING, (step + 1) % RING

    permute, permute_tail = _sort_chunk(h2_scr[...], route_scr[...], upper_ref, lower_ref,
                                        xl_ref, pos_ref, nb_ref)

    @pl.when(step == 0)
    def _():
        uc_ring[...] = jnp.zeros_like(uc_ring)
        uc_last[...] = jnp.zeros_like(uc_last)
        k_ring[...] = jnp.zeros_like(k_ring)
        vt_ring[...] = jnp.zeros_like(vt_ring)
        _project_a(x_ref, g1n_ref, sh1_ref, sc1_ref, wac_ref, cos_ref, sin_ref, 0, 0,
                   uc_ring, k_ring, vt_ring)

    x = x_ref[0]
    hb = _rms_mod(x, g1n_ref[...], sh1_ref[0], sc1_ref[0]).astype(BF16)
    bq = jnp.dot(hb, wb_ref[:, 0:2 * CONV_W], preferred_element_type=F32)
    q = _rope(bq[:, CONV_W:], cos_ref[...], sin_ref[...]) * (HEAD_DIM ** -0.5 * LOG2E)

    def block_ring(ring, slot_, blk, axis):
        lo = blk * BLOCK
        return ring[slot_, lo:lo + BLOCK, :] if axis == 0 else ring[slot_, :, lo:lo + BLOCK]

    def window(ring, jb, axis):
        parts = []
        for blk in (jb - 1, jb, jb + 1):
            if blk < 0:
                parts.append(block_ring(ring, prv, BLOCKS_PER_TILE - 1, axis))
            elif blk >= BLOCKS_PER_TILE:
                parts.append(block_ring(ring, nxt, 0, axis))
            else:
                parts.append(block_ring(ring, own, blk, axis))
        return jnp.concatenate(parts, axis=axis)

    k_ctx = kc_ref[0]
    vt_ctx = vtc_ref[0]
    bias_prev = bias_ref[0:BLOCK, :]
    bias_next = bias_ref[2 * BLOCK:3 * BLOCK, :]
    sink_row = sink_ref[...] * LOG2E
    n_units = BLOCKS_PER_TILE * N_KV_HEADS
    gate_cols = 2 * D_MODEL // n_units
    gate_chunks = []

    def gate_chunk(u):
        c0 = 2 * CONV_W + u * gate_cols
        gate_chunks.append(jnp.dot(hb, wb_ref[:, c0:c0 + gate_cols], preferred_element_type=F32))

    q_ts, biases = [], []
    for jb in range(BLOCKS_PER_TILE):
        bias_p, bias_n = bias_prev, bias_next
        if jb == 0:
            bias_p = bias_prev + jnp.where(is_first, NEG_INF, 0.0)
        if jb == BLOCKS_PER_TILE - 1:
            bias_n = bias_next + jnp.where(is_last, NEG_INF, 0.0)
        biases.append((jnp.concatenate([bias_p] * 4, axis=1), jnp.concatenate([bias_n] * 4, axis=1)))
        q_ts.append(q[jb * BLOCK:(jb + 1) * BLOCK, :].T.astype(BF16))

    scores, probs, outs = {}, {}, {}
    for t in range(n_units + 2):
        if t == 3:
            wub_ref[...] = wu_ref[...].astype(BF16)
        if t == 5:
            wdb_ref[...] = wd_ref[...].astype(BF16)
        if t == 0:
            permute(0, COMMON_ROWS // 2)
        if t == 4:
            permute(COMMON_ROWS // 2, COMMON_ROWS // 2)
        if t == 2:
            _project_a(xn_ref, g1n_ref, shn_ref, scn_ref, wac_ref, cosn_ref, sinn_ref, nxt,
                       (step + 1) % 2, uc_ring, k_ring, vt_ring)
        if t < n_units:
            jb, g = divmod(t, N_KV_HEADS)
            scores[t] = _attn_scores(q_ts[jb], g, window(k_ring, jb, 0), k_ctx, *biases[jb])
            gate_chunk(t)
        if 0 <= t - 1 < n_units:
            g = (t - 1) % N_KV_HEADS
            probs[t - 1] = _attn_probs(scores.pop(t - 1),
                                       sink_row[:, g * 4 * BLOCK:(g + 1) * 4 * BLOCK])
        if 0 <= t - 2 < n_units:
            jb, g = divmod(t - 2, N_KV_HEADS)
            outs[t - 2] = _attn_values(
                probs.pop(t - 2), sink_row[:, g * 4 * BLOCK:(g + 1) * 4 * BLOCK],
                window(vt_ring, jb, 1)[g * HEAD_DIM:(g + 1) * HEAD_DIM, :],
                vt_ctx[g * HEAD_DIM:(g + 1) * HEAD_DIM, :])
    o_blocks = []
    for jb in range(BLOCKS_PER_TILE):
        o_rows = [outs[jb * N_KV_HEADS + g][:, h * BLOCK:(h + 1) * BLOCK]
                  for g in range(N_KV_HEADS) for h in range(4)]
        o_blocks.append(jnp.concatenate(o_rows, axis=0).T)
    gates = jnp.concatenate(gate_chunks, axis=1)

    uc = uc_ring[step % 2]
    row = lax.broadcasted_iota(I32, (SEQ_TILE, 1), 0)
    prev_row = jnp.where(is_first, 0.0, uc_last[0:1, :])
    next_row = jnp.where(is_last, 0.0, uc_ring[(step + 1) % 2, 0:1, :])
    uc_last[0:1, :] = uc[SEQ_TILE - 1:SEQ_TILE, :]
    up = jnp.where(row == 0, prev_row, pltpu.roll(uc, 1, axis=0))
    dn = jnp.where(row == SEQ_TILE - 1, next_row, pltpu.roll(uc, SEQ_TILE - 1, axis=0))
    wconv = wconv_ref[...]
    y = bconv_ref[...] + (up * wconv[0:1, :] + uc * wconv[1:2, :] + dn * wconv[2:3, :])
    ya = jnp.dot((bq[:, :CONV_W] * y).astype(BF16), wa_ref[...], preferred_element_type=F32)

    o = jnp.concatenate(o_blocks, axis=0).astype(BF16)
    yb = jnp.dot(o, wbb_ref[...], preferred_element_type=F32)
    merged = (_sigmoid(gates[:, :D_MODEL]) * ya + _sigmoid(gates[:, D_MODEL:]) * yb).astype(BF16)
    x1 = x + gt1_ref[0] * jnp.dot(merged, wo_ref[...], preferred_element_type=F32)
    x1_ref[0] = x1
    h2 = _rms_mod(x1, g2n_ref[...], sh2_ref[0], sc2_ref[0]).astype(BF16)
    logits_t = lax.dot_general(wr_ref[...], h2, (((1,), (1,)), ((), ())),
                               preferred_element_type=F32) + br_ref[...]
    permute_tail()
    h2_scr[...] = h2
    route_scr[...] = _route(logits_t)


N_MIX_OUTPUTS = 6
N_MIX_SCRATCH = 6


def _mix_kernel(*refs, n_seq_tiles, batch):
    n_tail = N_MIX_OUTPUTS + N_MIX_SCRATCH
    upper_ref, lower_ref = refs[-n_tail - 2:-n_tail]
    _, xl_ref, pos_ref, nb_ref, _, _ = refs[-n_tail:-N_MIX_SCRATCH]
    h2_scr, route_scr = refs[-N_MIX_SCRATCH:-N_MIX_SCRATCH + 2]
    step = pl.program_id(0)
    n_main = n_seq_tiles * batch

    @pl.when(step == 0)
    def _():
        h2_scr[...] = jnp.zeros_like(h2_scr)
        route_scr[...] = jnp.zeros_like(route_scr)

    @pl.when(step < n_main)
    def _():
        _mix_step(*refs, n_seq_tiles=n_seq_tiles)

    @pl.when(step == n_main)
    def _():
        permute, permute_tail = _sort_chunk(h2_scr[...], route_scr[...], upper_ref, lower_ref,
                                            xl_ref, pos_ref, nb_ref)
        permute(0, COMMON_ROWS)
        permute_tail()


def _mix(x, norm1_g, norm2_g, mod3, w_a_cols, w_b_cols, cos_t, sin_t, k_ctx, vt_ctx,
         w_conv, b_conv, w_a, w_b, w_o, sink_row, bias, wr_t, br_col, w_up, w_down, upper, lower):
    bsz, seq, _ = x.shape
    nt = seq // SEQ_TILE
    clen = k_ctx.shape[1]
    nch = n_main = bsz * nt
    assert N_EXPERTS % n_main == 0, "each grid step converts an equal share of the experts"
    e_step = N_EXPERTS // n_main

    def tile(s):
        return divmod(jnp.clip(s, 0, n_main - 1), nt)

    def per_tile(shape, index, shift=0):
        return pl.BlockSpec(shape, lambda s: index(*tile(s + shift)))

    def const_spec(shape):
        return pl.BlockSpec(shape, lambda s: tuple(0 for _ in shape),
                            pipeline_mode=pl.Buffered(1))

    def x_spec(shift=0):
        return per_tile((1, SEQ_TILE, D_MODEL), lambda b, t: (b, t, 0), shift)

    def mod_spec(j, shift=0):
        return per_tile((1, 1, D_MODEL), lambda b, t: (b, 0, j), shift)

    def rope_spec(shift=0):
        return per_tile((SEQ_TILE, LANES), lambda b, t: (t, 0), shift)

    def expert_spec(w):
        return pl.BlockSpec((e_step,) + w.shape[1:], lambda s: (jnp.minimum(s, n_main - 1), 0, 0))

    def chunk_spec(shape):
        return pl.BlockSpec(shape, lambda s: (jnp.maximum(s - 1, 0),) + (0,) * (len(shape) - 1))

    in_specs = [
        x_spec(), const_spec((1, D_MODEL)), const_spec((1, D_MODEL)),
        mod_spec(0), mod_spec(1), mod_spec(2), mod_spec(3), mod_spec(4),
        const_spec(w_b_cols.shape), rope_spec(), rope_spec(),
        x_spec(1), mod_spec(0, 1), mod_spec(1, 1), rope_spec(1), rope_spec(1),
        const_spec(w_a_cols.shape),
        per_tile((1, clen, KV_W), lambda b, t: (b, 0, 0)),
        per_tile((1, KV_W, clen), lambda b, t: (b, 0, 0)),
        const_spec(w_conv.shape), const_spec(b_conv.shape),
        const_spec(w_a.shape), const_spec(w_b.shape), const_spec(w_o.shape),
        const_spec(sink_row.shape), const_spec(bias.shape),
        const_spec(wr_t.shape), const_spec(br_col.shape),
        expert_spec(w_up), expert_spec(w_down),
        const_spec(upper.shape), const_spec(lower.shape),
    ]
    out_specs = [
        x_spec(), chunk_spec((CHUNK_ROWS, XL_COLS)), chunk_spec((CHUNK, LANES)),
        chunk_spec((1, N_EXPERTS, LANES)), expert_spec(w_up), expert_spec(w_down),
    ]
    out_shape = [
        jax.ShapeDtypeStruct((bsz, seq, D_MODEL), F32),
        jax.ShapeDtypeStruct((nch * CHUNK_ROWS, XL_COLS), BF16),
        jax.ShapeDtypeStruct((nch * CHUNK, LANES), F32),
        jax.ShapeDtypeStruct((nch, N_EXPERTS, LANES), I32),
        jax.ShapeDtypeStruct(w_up.shape, BF16),
        jax.ShapeDtypeStruct(w_down.shape, BF16),
    ]
    return pl.pallas_call(
        functools.partial(_mix_kernel, n_seq_tiles=nt, batch=bsz),
        grid=(n_main + 1,),
        in_specs=in_specs,
        out_specs=out_specs,
        out_shape=out_shape,
        scratch_shapes=[pltpu.VMEM((CHUNK, D_MODEL), BF16), pltpu.VMEM((ROUTE_ROWS, CHUNK), F32),
                        pltpu.VMEM((2, SEQ_TILE, CONV_W), F32), pltpu.VMEM((8, CONV_W), F32),
                        pltpu.VMEM((RING, SEQ_TILE, KV_W), BF16),
                        pltpu.VMEM((RING, KV_W, SEQ_TILE), BF16)],
        compiler_params=_cparams(("arbitrary",)),
        name="token_mix",
    )(x, norm1_g, norm2_g, mod3, mod3, mod3, mod3, mod3, w_b_cols, cos_t, sin_t,
      x, mod3, mod3, cos_t, sin_t, w_a_cols, k_ctx, vt_ctx,
      w_conv, b_conv, w_a, w_b, w_o, sink_row, bias, wr_t, br_col, w_up, w_down, upper, lower)


def _bf16_parts(v):
    hi = v.astype(BF16).astype(F32)
    r1 = v - hi
    mid = r1.astype(BF16).astype(F32)
    lo = (r1 - mid).astype(BF16).astype(F32)
    return hi, mid, lo


def _sort_chunk(h, route, upper_ref, lower_ref, xl_ref, pos_ref, nb_ref):
    e1 = route[0:1, :].astype(I32)
    e2 = route[1:2, :].astype(I32)
    erow = lax.broadcasted_iota(I32, (N_EXPERTS, CHUNK), 0)
    hit1 = erow == e1
    hit2 = erow == e2
    onehot = jnp.where(hit1, 1.0, 0.0) + jnp.where(hit2, 1.0, 0.0)
    cum = jnp.dot(onehot.astype(BF16), upper_ref[...], preferred_element_type=F32)
    cnt = jnp.sum(onehot, axis=1, keepdims=True)
    nblk = jnp.floor((cnt + (ROW_BLOCK - 1)) * (1.0 / ROW_BLOCK))
    nblk_b = jnp.broadcast_to(nblk, (N_EXPERTS, LANES))
    seg = jnp.dot(lower_ref[...], nblk_b.astype(BF16), preferred_element_type=F32) * ROW_BLOCK
    base = seg[:, 0:1] + cum
    pos1 = jnp.sum(jnp.where(hit1, base, 0.0), axis=0, keepdims=True)
    pos2 = jnp.sum(jnp.where(hit2, base, 0.0), axis=0, keepdims=True)
    p1i = pos1.astype(I32)
    p2i = pos2.astype(I32)
    used_rows = (jnp.sum(nblk) * ROW_BLOCK).astype(I32)

    prow = lax.broadcasted_iota(I32, (LANES, CHUNK), 0)
    parts = _bf16_parts(route[2:3, :]) + _bf16_parts(route[3:4, :])
    gpart_rows = jnp.zeros((LANES, CHUNK), F32)
    for j, part in enumerate(parts):
        gpart_rows = jnp.where(prow == j, part, gpart_rows)
    h_ext = jnp.concatenate([h, gpart_rows.T.astype(BF16)], axis=1)
    glane = lax.broadcasted_iota(I32, (1, GATE_COLS), 1)
    one = jnp.ones((), BF16)
    zero = jnp.zeros((), BF16)
    def permute(row0, n_rows):
        r16 = lax.broadcasted_iota(I32, (n_rows, CHUNK), 0).astype(jnp.int16)
        q1 = (p1i - row0).astype(jnp.int16)
        q2 = (p2i - row0).astype(jnp.int16)
        sel1 = jnp.where(r16 == q1, one, zero)
        sel = jnp.where(r16 == q2, one, sel1)
        is_slot1 = jnp.max(sel1, axis=1, keepdims=True)
        rows = pl.ds(row0, n_rows)
        xg = jnp.dot(sel, h_ext, preferred_element_type=F32)
        xl_ref[rows, 0:D_MODEL] = xg[:, :D_MODEL].astype(BF16)
        g6 = xg[:, D_MODEL:]
        g3 = jnp.where(is_slot1.astype(F32) > 0.0, g6, pltpu.roll(g6, GATE_COLS - 3, axis=1))
        xl_ref[rows, D_MODEL:XL_COLS] = jnp.where(glane < 3, g3, 0.0).astype(BF16)

    pos_rows = jnp.where(prow == 0, pos1, jnp.where(prow == 1, pos2, 0.0))
    pos_ref[...] = pos_rows.T
    nb_ref[0] = nblk_b.astype(I32)

    def permute_tail():
        tail_rows = CHUNK_ROWS - COMMON_ROWS

        @pl.when(COMMON_ROWS < used_rows)
        def _():
            permute(COMMON_ROWS, tail_rows)

        @pl.when(COMMON_ROWS >= used_rows)
        def _():
            xl_ref[pl.ds(COMMON_ROWS, tail_rows), :] = jnp.zeros((tail_rows, XL_COLS), BF16)

    return permute, permute_tail


def _max_tiles(nch):
    max_blocks = nch * (2 * CHUNK + N_EXPERTS * (ROW_BLOCK - 1)) // ROW_BLOCK
    return max_blocks // TILE_BLOCKS + N_EXPERTS


def _masked_prefix(le, values):
    delta = values - jnp.concatenate([jnp.zeros((1,), values.dtype), values[:-1]])
    return jnp.sum(jnp.where(le, delta[None, :], 0), axis=1)


def _tile_plan(nb, nch):
    n_tiles = _max_tiles(nch)
    nbt = nb.T
    nbe = jnp.sum(nbt, axis=1)
    nte = (nbe + TILE_BLOCKS - 1) // TILE_BLOCKS
    tile_end = jnp.cumsum(nte)
    tile_start = tile_end - nte
    n_act = tile_end[-1]
    tiles = jnp.arange(n_tiles, dtype=I32)
    te = jnp.sum((tile_end[None, :] <= tiles[:, None]).astype(I32), axis=1)
    e_ar = jnp.arange(N_EXPERTS, dtype=I32)
    active = nte > 0
    te = jnp.where(tiles < n_act, te, jnp.max(jnp.where(active, e_ar, 0)))
    first = jnp.logical_and(te != jnp.concatenate([jnp.full((1,), -1, I32), te[:-1]]),
                            tiles < n_act).astype(I32)
    rank = jnp.cumsum(active.astype(I32)) - 1
    k_ar = jnp.arange(N_EXPERTS + W_SLOTS, dtype=I32)
    eseq = jnp.sum(jnp.where(jnp.logical_and(active[None, :], rank[None, :] == k_ar[:, None]),
                             e_ar[None, :], 0), axis=1)
    n_exp = jnp.sum(active.astype(I32))

    cb_excl = jnp.cumsum(nbt, axis=1) - nbt
    gs = TILE_BLOCKS * tile_start[:, None] + cb_excl
    seg_blk = jnp.cumsum(nb, axis=1) - nb
    base_blk = jnp.arange(nch, dtype=I32)[None, :] * CHUNK_BLOCKS + seg_blk.T
    gs_f = gs.reshape(-1)
    slots = jnp.arange(n_tiles * TILE_BLOCKS, dtype=I32)
    le = gs_f[None, :] <= slots[:, None]
    blk = slots + _masked_prefix(le, (base_blk - gs).reshape(-1))
    valid = slots < _masked_prefix(le, (gs + nbt).reshape(-1))
    gblk = jnp.concatenate([jnp.where(valid, blk, 0).astype(I32),
                            jnp.zeros(((X_SLOTS - 1) * TILE_BLOCKS,), I32)])

    m = jnp.arange(CHUNK_BLOCKS, dtype=I32)
    le_c = seg_blk[:, None, :] <= m[None, :, None]
    shift = (gs.T - seg_blk)
    delta = shift - jnp.concatenate([jnp.zeros((nch, 1), I32), shift[:, :-1]], axis=1)
    slot_of = m[None, :] + jnp.sum(jnp.where(le_c, delta[:, None, :], 0), axis=2)
    used_blocks = jnp.sum(nb, axis=1)
    slot_of = jnp.where(m[None, :] < used_blocks[:, None], slot_of, 0).astype(I32)
    pad = Y_RING - 1
    slot_of = jnp.concatenate([slot_of.reshape(-1), jnp.zeros((pad * CHUNK_BLOCKS,), I32)])
    used_rows = jnp.concatenate([ROW_BLOCK * used_blocks, jnp.zeros((pad,), I32)]).astype(I32)
    return ((n_act.reshape(1).astype(I32), n_exp.reshape(1), first, eseq, gblk),
            (used_rows, slot_of))


def _gmm_kernel(nact_ref, nexp_ref, first_ref, eseq_ref, gblk_ref,
                xl_hbm, wu_hbm, wd_hbm, yt_hbm,
                xbuf, ybuf, zbuf, wu_st, wd_st, in_sem, out_sem, w_sem, z_sem):
    n_act = nact_ref[0]
    n_exp = nexp_ref[0]
    n_tiles = yt_hbm.shape[0] // TILE_BLOCKS
    prefetch = X_SLOTS - 1

    def gather(tile, b):
        s = tile % X_SLOTS
        blk = gblk_ref[tile * TILE_BLOCKS + b]
        return pltpu.make_async_copy(xl_hbm.at[blk], xbuf.at[s, b], in_sem.at[s])

    def write_back(tile):
        s = tile % Y_SLOTS
        return pltpu.make_async_copy(ybuf.at[s], yt_hbm.at[pl.ds(tile * TILE_BLOCKS, TILE_BLOCKS)],
                                     out_sem.at[s])

    def zero_fill(tile):
        return pltpu.make_async_copy(zbuf, yt_hbm.at[pl.ds(tile * TILE_BLOCKS, TILE_BLOCKS)], z_sem)

    def weights(q):
        e = eseq_ref[q]
        s = q % W_SLOTS
        return (pltpu.make_async_copy(wu_hbm.at[e], wu_st.at[s], w_sem.at[0, s]),
                pltpu.make_async_copy(wd_hbm.at[e], wd_st.at[s], w_sem.at[1, s]))

    def tile_blocks(fn):
        for b in range(TILE_BLOCKS):
            fn(b)

    for q0 in range(W_SLOTS - 1):
        @pl.when(q0 < n_exp)
        def _(q0=q0):
            for cp in weights(q0):
                cp.start()
    for t0 in range(prefetch):
        tile_blocks(lambda b, t0=t0: gather(t0, b).start())

    def body(i, q):
        tile_blocks(lambda b: gather(i, b).wait())

        @pl.when(i >= Y_SLOTS)
        def _():
            write_back(i - Y_SLOTS).wait()

        is_first = first_ref[i] == 1

        @pl.when(is_first)
        def _():
            for cp in weights(q):
                cp.wait()

            @pl.when(q + W_SLOTS - 1 < n_exp)
            def _():
                for cp in weights(q + W_SLOTS - 1):
                    cp.start()

        q = q + is_first.astype(I32)
        ws = (q - 1) % W_SLOTS
        xs = i % X_SLOTS
        ys = i % Y_SLOTS

        x = xbuf[xs].reshape(TILE_ROWS, XL_COLS)
        gate = jnp.sum(x[:, D_MODEL:].astype(F32), axis=1, keepdims=True)
        au = jnp.dot(x[:, :D_MODEL], wu_st[ws], preferred_element_type=F32)
        tile_blocks(lambda b: gather(i + prefetch, b).start())
        a = au[:, :EXPERT_FF]
        act = (a * _sigmoid(a)) * au[:, EXPERT_FF:]
        y = jnp.dot(act.astype(BF16), wd_st[ws], preferred_element_type=F32)
        ybuf[ys] = (gate * y).astype(BF16).reshape(TILE_BLOCKS, ROW_BLOCK, D_MODEL)
        write_back(i).start()

        @pl.when(n_act + i < n_tiles)
        def _():
            zero_fill(n_act + i).start()
        return q

    zbuf[...] = jnp.zeros_like(zbuf)
    lax.fori_loop(0, n_act, body, jnp.int32(0))

    def fill_rest(t, carry):
        zero_fill(t).start()
        return carry

    lax.fori_loop(jnp.minimum(2 * n_act, n_tiles), n_tiles, fill_rest, 0)

    def drain(t, carry):
        zero_fill(t).wait()
        return carry

    lax.fori_loop(n_act, n_tiles, drain, 0)
    for k in range(prefetch):
        tile_blocks(lambda b, k=k: gather(n_act + k, b).wait())
    for k in range(Y_SLOTS):
        @pl.when(n_act - 1 - k >= 0)
        def _(k=k):
            write_back(n_act - 1 - k).wait()


def _grouped_mlp(plan, xl, w_up, w_down, n_tiles):
    grid_spec = pltpu.PrefetchScalarGridSpec(
        num_scalar_prefetch=len(plan),
        grid=(1,),
        in_specs=[pl.BlockSpec(memory_space=pl.ANY)] * 3,
        out_specs=pl.BlockSpec(memory_space=pl.ANY),
        scratch_shapes=[pltpu.VMEM((X_SLOTS, TILE_BLOCKS, ROW_BLOCK, XL_COLS), BF16),
                        pltpu.VMEM((Y_SLOTS, TILE_BLOCKS, ROW_BLOCK, D_MODEL), BF16),
                        pltpu.VMEM((TILE_BLOCKS, ROW_BLOCK, D_MODEL), BF16),
                        pltpu.VMEM((W_SLOTS, D_MODEL, 2 * EXPERT_FF), BF16),
                        pltpu.VMEM((W_SLOTS, EXPERT_FF, D_MODEL), BF16),
                        pltpu.SemaphoreType.DMA((X_SLOTS,)),
                        pltpu.SemaphoreType.DMA((Y_SLOTS,)),
                        pltpu.SemaphoreType.DMA((2, W_SLOTS)),
                        pltpu.SemaphoreType.DMA(())])
    return pl.pallas_call(
        _gmm_kernel,
        grid_spec=grid_spec,
        out_shape=jax.ShapeDtypeStruct((n_tiles * TILE_BLOCKS, ROW_BLOCK, D_MODEL), BF16),
        compiler_params=_cparams(("arbitrary",)),
        name="moe_experts",
    )(*plan, xl, w_up, w_down)


def _combine_kernel(used_ref, slot_ref, yt_hbm, pos_ref, x1_ref, gt2_ref, fg_ref, o_ref, ybuf, sem):
    c = pl.program_id(0)
    n_chunks = pl.num_programs(0)
    used_rows = used_ref[c]

    def chunk_blocks(chunk, op):
        s = chunk % Y_RING

        def copy(b):
            slot = slot_ref[chunk * CHUNK_BLOCKS + b]
            return pltpu.make_async_copy(yt_hbm.at[slot], ybuf.at[s, b], sem.at[s])

        for b in range(COMMON_BLOCKS):
            op(copy(b))

        @pl.when(used_ref[chunk] > COMMON_ROWS)
        def _():
            for b in range(COMMON_BLOCKS, CHUNK_BLOCKS):
                op(copy(b))

    @pl.when(c == 0)
    def _():
        for k in range(Y_RING - 1):
            chunk_blocks(k, lambda cp: cp.start())

    chunk_blocks(c + Y_RING - 1, lambda cp: cp.start())
    chunk_blocks(c, lambda cp: cp.wait())

    def body(k_rows):
        pos = pos_ref[...]
        p1 = pos[:, 0:1].astype(I32)
        p2 = pos[:, 1:2].astype(I32)
        r = lax.broadcasted_iota(I32, (CHUNK, k_rows), 1)
        sel = jnp.where(r == p1, 1.0, jnp.where(r == p2, 1.0, 0.0)).astype(BF16)
        yl = ybuf[c % Y_RING, 0:k_rows // ROW_BLOCK].reshape(k_rows, D_MODEL)
        y = jnp.dot(sel, yl, preferred_element_type=F32)
        x2 = x1_ref[...] + gt2_ref[0] * y
        inv = lax.rsqrt(jnp.mean(x2 * x2, axis=-1, keepdims=True) + NORM_EPS)
        o_ref[...] = (x2 * inv) * fg_ref[...]

    pl.when(used_rows <= COMMON_ROWS)(lambda: body(COMMON_ROWS))
    pl.when(used_rows > COMMON_ROWS)(lambda: body(CHUNK_ROWS))

    @pl.when(c == n_chunks - 1)
    def _():
        for k in range(1, Y_RING):
            chunk_blocks(c + k, lambda cp: cp.wait())


def _combine(used_rows, slot_of, yt, pos_c, x1, mod3, final_g, seq):
    n = x1.shape[0]
    per_seq = seq // CHUNK
    grid_spec = pltpu.PrefetchScalarGridSpec(
        num_scalar_prefetch=2,
        grid=(n // CHUNK,),
        in_specs=[pl.BlockSpec(memory_space=pl.ANY),
                  pl.BlockSpec((CHUNK, LANES), lambda c, u, s: (c, 0)),
                  pl.BlockSpec((CHUNK, D_MODEL), lambda c, u, s: (c, 0)),
                  pl.BlockSpec((1, 1, D_MODEL), lambda c, u, s: (c // per_seq, 0, 5)),
                  pl.BlockSpec((1, D_MODEL), lambda c, u, s: (0, 0))],
        out_specs=pl.BlockSpec((CHUNK, D_MODEL), lambda c, u, s: (c, 0)),
        scratch_shapes=[pltpu.VMEM((Y_RING, CHUNK_BLOCKS, ROW_BLOCK, D_MODEL), BF16),
                        pltpu.SemaphoreType.DMA((Y_RING,))])
    return pl.pallas_call(
        _combine_kernel,
        grid_spec=grid_spec,
        out_shape=jax.ShapeDtypeStruct((n, D_MODEL), F32),
        compiler_params=_cparams(("arbitrary",)),
        name="moe_combine",
    )(used_rows, slot_of, yt, pos_c, x1, mod3, final_g)


def _rope_tables(seq):
    n_freq = HEAD_DIM // 4
    inv_freq = ROPE_BASE ** (-jnp.arange(n_freq, dtype=F32) / n_freq)
    rows = seq // GRID_W
    row = jnp.repeat(jnp.arange(rows, dtype=F32), GRID_W)
    col = jnp.tile(jnp.arange(GRID_W, dtype=F32), rows)
    ang = jnp.concatenate([row[:, None] * inv_freq, col[:, None] * inv_freq], axis=-1)
    cos, sin = jnp.cos(ang), jnp.sin(ang)
    return jnp.tile(cos, (1, 4)), jnp.concatenate([-sin, sin, -sin, sin], axis=1)


def _window_bias():
    key = np.arange(3 * BLOCK)[:, None]
    qry = np.arange(BLOCK)[None, :]
    valid = (key - qry >= 0) & (key - qry <= 2 * BLOCK)
    return jnp.asarray(np.where(valid, 0.0, NEG_INF), F32)


def kernel(x, c, ctx, c_ctx, w_ada, b_ada, norm1_g, w_in, w_conv, b_conv, w_a, w_b, sink, w_o,
           norm2_g, w_group, b_group, w_router, b_router, w_up, w_down, final_g):
    bsz, seq, _ = x.shape
    assert w_ada.shape[0] == 1 and seq % SEQ_TILE == 0

    cc = jnp.zeros((16, D_MODEL), F32).at[:bsz].set(c).at[bsz].set(c_ctx)
    mod = _modulation(cc, w_ada[0], b_ada[0][None, :])
    mod3 = mod[:bsz].reshape(bsz, 1, N_MOD * D_MODEL)
    csh1 = mod[bsz:bsz + 1, 0:D_MODEL]
    csc1 = mod[bsz:bsz + 1, D_MODEL:2 * D_MODEL]

    w = w_in[0]
    n1g = norm1_g[0][None, :]
    n2g = norm2_g[0][None, :]
    w_kv = w[:, OFF_K:OFF_GA].astype(BF16)
    w_a_cols = jnp.concatenate([w[:, OFF_CG:OFF_Q], w[:, OFF_K:OFF_GA]], axis=1).astype(BF16)
    w_b_cols = jnp.concatenate([w[:, OFF_B:OFF_CG], w[:, OFF_Q:OFF_K], w[:, OFF_GA:]],
                               axis=1).astype(BF16)
    cos_t, sin_t = _rope_tables(seq)

    k_ctx, vt_ctx = _context_kv(ctx, n1g, csh1, csc1, w_kv)
    sink_row = jnp.repeat(sink[0].astype(F32), BLOCK)[None, :]
    pad_g = jnp.zeros((D_MODEL, GROUP_ROWS - N_GROUPS), F32)
    pad_e = jnp.zeros((D_MODEL, ROUTER_ROWS - GROUP_ROWS - N_EXPERTS), F32)
    wr_t = jnp.concatenate([w_group[0], pad_g, w_router[0], pad_e], axis=1).T.astype(BF16)
    br_col = jnp.concatenate([b_group[0], pad_g[0], b_router[0], pad_e[0]])[:, None]

    upper = jnp.asarray(np.triu(np.ones((CHUNK, CHUNK), np.float32), 1), BF16)
    lower = jnp.asarray(np.tril(np.ones((N_EXPERTS, N_EXPERTS), np.float32), -1), BF16)
    x1, xl, pos_c, nb, w_up_bf, w_down_bf = _mix(
        x, n1g, n2g, mod3, w_a_cols, w_b_cols, cos_t, sin_t, k_ctx, vt_ctx,
        w_conv[0], b_conv[0][None, :], w_a[0].astype(BF16), w_b[0].astype(BF16),
        w_o[0].astype(BF16), sink_row, _window_bias(), wr_t, br_col, w_up[0], w_down[0],
        upper, lower)

    n = bsz * seq
    nch = n // CHUNK
    nb = nb[:, :, 0]
    expert_plan, combine_plan = _tile_plan(nb, nch)
    yt = _grouped_mlp(expert_plan, xl.reshape(-1, ROW_BLOCK, XL_COLS), w_up_bf, w_down_bf,
                      _max_tiles(nch))
    out = _combine(*combine_plan, yt, pos_c, x1.reshape(n, D_MODEL), mod3, final_g[None, :], seq)
    return out.reshape(bsz, seq, D_MODEL)
```

```python
import functools

import numpy as np
import jax
import jax.numpy as jnp
from jax import lax
from jax.experimental import pallas as pl
from jax.experimental.pallas import tpu as pltpu

F32 = jnp.float32
BF16 = jnp.bfloat16
I32 = jnp.int32

D_MODEL = 1024
GRID_W = 64
CONV_W = 512
N_HEADS = 8
N_KV_HEADS = 2
HEAD_DIM = 64
ATT_W = N_HEADS * HEAD_DIM
KV_W = N_KV_HEADS * HEAD_DIM
BLOCK = 128
ROPE_BASE = 10000.0
N_GROUPS = 4
EXPERTS_PER_GROUP = 8
N_EXPERTS = N_GROUPS * EXPERTS_PER_GROUP
EXPERT_FF = 256
N_MOD = 6
MOD_SHIFT1, MOD_SCALE1, MOD_GATE1, MOD_SHIFT2, MOD_SCALE2, MOD_GATE2 = range(N_MOD)
NORM_EPS = 1e-6
NEG_INF = -1e30
LOG2E = 1.4426950408889634
ONES_ROWS = 16

OFF_B, OFF_CG, OFF_XIN, OFF_Q, OFF_K, OFF_V, OFF_GA, OFF_GB = (
    0, 512, 1024, 1536, 2048, 2176, 2304, 3328)
IN_COLS = 4352

LANES = 128
SEQ_TILE = 512
BLOCKS_PER_TILE = SEQ_TILE // BLOCK
ROUTE_ROWS = 8
GROUP_ROWS = 8
ROUTER_ROWS = 48
VMEM_LIMIT = 62 * 1024 * 1024

CHUNK = SEQ_TILE
ROW_BLOCK = 16
CHUNK_ROWS = -(-(2 * CHUNK + N_EXPERTS * (ROW_BLOCK - 1)) // 256) * 256
CHUNK_BLOCKS = CHUNK_ROWS // ROW_BLOCK
GATE_COLS = LANES
XL_COLS = D_MODEL + GATE_COLS
COMMON_ROWS = 2 * CHUNK + 256
COMMON_BLOCKS = COMMON_ROWS // ROW_BLOCK
TILE_BLOCKS = 32
TILE_ROWS = TILE_BLOCKS * ROW_BLOCK
X_SLOTS = 8
Y_SLOTS = 4
W_SLOTS = 3
Y_RING = 4
RING = 3


def _cparams(sem):
    return pltpu.CompilerParams(dimension_semantics=sem, vmem_limit_bytes=VMEM_LIMIT)


def _rms_mod(x, g, shift, scale):
    inv = lax.rsqrt(jnp.mean(x * x, axis=-1, keepdims=True) + NORM_EPS)
    return (x * inv) * (g * (1.0 + scale)) + shift


def _mod_kernel(c_ref, w_ref, b_ref, o_ref):
    c = c_ref[...]
    a = (c * jax.nn.sigmoid(c)).astype(BF16)
    o_ref[...] = jnp.dot(a, w_ref[...].astype(BF16), preferred_element_type=F32) + b_ref[...]


def _modulation(cc, w_ada, b_ada):
    rows = cc.shape[0]
    cols = w_ada.shape[1]
    tile = 1024
    return pl.pallas_call(
        _mod_kernel,
        grid=(cols // tile,),
        in_specs=[pl.BlockSpec((rows, D_MODEL), lambda j: (0, 0)),
                  pl.BlockSpec((D_MODEL, tile), lambda j: (0, j)),
                  pl.BlockSpec((1, tile), lambda j: (0, j))],
        out_specs=pl.BlockSpec((rows, tile), lambda j: (0, j)),
        out_shape=jax.ShapeDtypeStruct((rows, cols), F32),
        compiler_params=_cparams(("arbitrary",)),
        name="adaln_mod",
    )(cc, w_ada, b_ada)


def _ctx_kernel(x_ref, g_ref, sh_ref, sc_ref, w_ref, k_ref, vt_ref):
    h = _rms_mod(x_ref[0], g_ref[...], sh_ref[...], sc_ref[...]).astype(BF16)
    kv = jnp.dot(h, w_ref[...], preferred_element_type=F32)
    k_ref[0] = kv[:, :KV_W].astype(BF16)
    vt_ref[0] = kv[:, KV_W:].T.astype(BF16)


def _context_kv(ctx, norm_g, csh, csc, w_kv):
    bsz, clen, _ = ctx.shape
    return pl.pallas_call(
        _ctx_kernel,
        grid=(bsz,),
        in_specs=[pl.BlockSpec((1, clen, D_MODEL), lambda b: (b, 0, 0)),
                  pl.BlockSpec((1, D_MODEL), lambda b: (0, 0)),
                  pl.BlockSpec((1, D_MODEL), lambda b: (0, 0)),
                  pl.BlockSpec((1, D_MODEL), lambda b: (0, 0)),
                  pl.BlockSpec((D_MODEL, 2 * KV_W), lambda b: (0, 0))],
        out_specs=[pl.BlockSpec((1, clen, KV_W), lambda b: (b, 0, 0)),
                   pl.BlockSpec((1, KV_W, clen), lambda b: (b, 0, 0))],
        out_shape=[jax.ShapeDtypeStruct((bsz, clen, KV_W), BF16),
                   jax.ShapeDtypeStruct((bsz, KV_W, clen), BF16)],
        compiler_params=_cparams(("arbitrary",)),
        name="context_kv",
    )(ctx, norm_g, csh, csc, w_kv)


def _rope(t, cos, sin_signed):
    lane = lax.broadcasted_iota(I32, (1, LANES), 1)
    first_half = (lane % HEAD_DIM) < (HEAD_DIM // 2)
    outs = []
    for j in range(t.shape[1] // LANES):
        tj = t[:, j * LANES:(j + 1) * LANES]
        partner = jnp.where(first_half,
                            pltpu.roll(tj, LANES - HEAD_DIM // 2, axis=1),
                            pltpu.roll(tj, HEAD_DIM // 2, axis=1))
        outs.append(tj * cos + partner * sin_signed)
    return outs[0] if len(outs) == 1 else jnp.concatenate(outs, axis=1)


def _sigmoid(x):
    return 1.0 / (1.0 + jnp.exp2(x * (-LOG2E)))


def _attn_scores(q_t, g, k_win, k_ctx, bias_p4, bias_n4):
    zeros = jnp.zeros((HEAD_DIM, 4 * BLOCK), BF16)
    qg = jnp.concatenate([q_t[(4 * g + h) * HEAD_DIM:(4 * g + h + 1) * HEAD_DIM, :]
                          for h in range(4)], axis=1)
    rhs = jnp.concatenate([qg, zeros] if g == 0 else [zeros, qg], axis=0)
    s_win = jnp.dot(k_win, rhs, preferred_element_type=F32)
    s_ctx = jnp.dot(k_ctx, rhs, preferred_element_type=F32)
    return (s_win[0:BLOCK] + bias_p4, s_win[BLOCK:2 * BLOCK], s_win[2 * BLOCK:] + bias_n4, s_ctx)


def _attn_probs(scores, sink):
    m = sink
    for s in scores:
        m = jnp.maximum(m, jnp.max(s, axis=0, keepdims=True))
    p_win = jnp.concatenate([jnp.exp2(s - m).astype(BF16) for s in scores[:3]], axis=0)
    p_ctx = jnp.exp2(scores[3] - m).astype(BF16)
    return p_win, p_ctx, m


def _attn_values(probs, sink, vt_win_g, vt_ctx_g):
    p_win, p_ctx, m = probs

    def with_ones(vt):
        r = lax.broadcasted_iota(I32, (ONES_ROWS, vt.shape[1]), 0)
        return jnp.concatenate([vt, jnp.where(r == 0, 1.0, 0.0).astype(BF16)], axis=0)

    o_ext = (jnp.dot(with_ones(vt_win_g), p_win, preferred_element_type=F32)
             + jnp.dot(with_ones(vt_ctx_g), p_ctx, preferred_element_type=F32))
    denom = o_ext[HEAD_DIM:HEAD_DIM + 1, :] + jnp.exp2(sink - m)
    return o_ext[:HEAD_DIM, :] / denom


def _route(logits_t):
    t = logits_t.shape[1]
    grow = lax.broadcasted_iota(I32, (GROUP_ROWS, t), 0)
    gl = jnp.where(grow < N_GROUPS, logits_t[0:GROUP_ROWS, :], NEG_INF)
    gm = jnp.max(gl, axis=0, keepdims=True)
    p_g = 1.0 / jnp.sum(jnp.exp(gl - gm), axis=0, keepdims=True)
    g_idx = jnp.min(jnp.where(gl == gm, grow, N_GROUPS), axis=0, keepdims=True)

    erow = lax.broadcasted_iota(I32, (N_EXPERTS, t), 0)
    el = logits_t[GROUP_ROWS:GROUP_ROWS + N_EXPERTS, :]
    sel = (erow // EXPERTS_PER_GROUP) == g_idx
    em = jnp.where(sel, el, NEG_INF)
    m1 = jnp.max(em, axis=0, keepdims=True)
    i1 = jnp.min(jnp.where(em == m1, erow, N_EXPERTS), axis=0, keepdims=True)
    em2 = jnp.where(erow == i1, NEG_INF, em)
    m2 = jnp.max(em2, axis=0, keepdims=True)
    i2 = jnp.min(jnp.where(em2 == m2, erow, N_EXPERTS), axis=0, keepdims=True)
    z = jnp.sum(jnp.where(sel, jnp.exp(el - m1), 0.0), axis=0, keepdims=True)
    p1 = 1.0 / z
    p2 = jnp.exp(m2 - m1) / z
    gate1 = p_g * p1 / (p1 + p2)
    gate2 = p_g * p2 / (p1 + p2)
    pad = jnp.zeros((ROUTE_ROWS - 4, t), F32)
    return jnp.concatenate([i1.astype(F32), i2.astype(F32), gate1, gate2, pad], axis=0)


def _project_a(x_ref, g_ref, sh_ref, sc_ref, w_ref, cos_ref, sin_ref, slot, uc_slot,
               uc_ring, k_ring, vt_ring):
    h = _rms_mod(x_ref[0], g_ref[...], sh_ref[0], sc_ref[0]).astype(BF16)
    u = jnp.dot(h, w_ref[...], preferred_element_type=F32)
    uc_ring[uc_slot] = u[:, :CONV_W] * u[:, CONV_W:2 * CONV_W]
    k = u[:, 2 * CONV_W:2 * CONV_W + KV_W]
    k_ring[slot] = _rope(k, cos_ref[...], sin_ref[...]).astype(BF16)
    vt_ring[slot] = u[:, 2 * CONV_W + KV_W:].T.astype(BF16)


def _mix_step(x_ref, g1n_ref, g2n_ref, sh1_ref, sc1_ref, gt1_ref, sh2_ref, sc2_ref,
              wb_ref, cos_ref, sin_ref,
              xn_ref, shn_ref, scn_ref, cosn_ref, sinn_ref, wac_ref,
              kc_ref, vtc_ref, wconv_ref, bconv_ref, wa_ref, wbb_ref, wo_ref,
              sink_ref, bias_ref, wr_ref, br_ref, wu_ref, wd_ref, upper_ref, lower_ref,
              x1_ref, xl_ref, pos_ref, nb_ref, wub_ref, wdb_ref,
              h2_scr, route_scr, uc_ring, uc_last, k_ring, vt_ring,
              *, n_seq_tiles):
    step = pl.program_id(0)
    t_idx = step % n_seq_tiles
    is_first = t_idx == 0
    is_last = t_idx == n_seq_tiles - 1
    own, prv, nxt = step % RING, (step + RING - 1) % RING, (step + 1) % RING

    permute, permute_tail = _sort_chunk(h2_scr[...], route_scr[...], upper_ref, lower_ref,
                                        xl_ref, pos_ref, nb_ref)

    @pl.when(step == 0)
    def _():
        uc_ring[...] = jnp.zeros_like(uc_ring)
        uc_last[...] = jnp.zeros_like(uc_last)
        k_ring[...] = jnp.zeros_like(k_ring)
        vt_ring[...] = jnp.zeros_like(vt_ring)
        _project_a(x_ref, g1n_ref, sh1_ref, sc1_ref, wac_ref, cos_ref, sin_ref, 0, 0,
                   uc_ring, k_ring, vt_ring)

    x = x_ref[0]
    hb = _rms_mod(x, g1n_ref[...], sh1_ref[0], sc1_ref[0]).astype(BF16)
    bq = jnp.dot(hb, wb_ref[:, 0:2 * CONV_W], preferred_element_type=F32)
    q = _rope(bq[:, CONV_W:], cos_ref[...], sin_ref[...]) * (HEAD_DIM ** -0.5 * LOG2E)

    def block_ring(ring, slot_, blk, axis):
        lo = blk * BLOCK
        return ring[slot_, lo:lo + BLOCK, :] if axis == 0 else ring[slot_, :, lo:lo + BLOCK]

    def window(ring, jb, axis):
        parts = []
        for blk in (jb - 1, jb, jb + 1):
            if blk < 0:
                parts.append(block_ring(ring, prv, BLOCKS_PER_TILE - 1, axis))
            elif blk >= BLOCKS_PER_TILE:
                parts.append(block_ring(ring, nxt, 0, axis))
            else:
                parts.append(block_ring(ring, own, blk, axis))
        return jnp.concatenate(parts, axis=axis)

    k_ctx = kc_ref[0]
    vt_ctx = vtc_ref[0]
    bias_prev = bias_ref[0:BLOCK, :]
    bias_next = bias_ref[2 * BLOCK:3 * BLOCK, :]
    sink_row = sink_ref[...] * LOG2E
    n_units = BLOCKS_PER_TILE * N_KV_HEADS
    gate_cols = 2 * D_MODEL // n_units
    gate_chunks = []

    def gate_chunk(u):
        c0 = 2 * CONV_W + u * gate_cols
        gate_chunks.append(jnp.dot(hb, wb_ref[:, c0:c0 + gate_cols], preferred_element_type=F32))

    q_ts, biases = [], []
    for jb in range(BLOCKS_PER_TILE):
        bias_p, bias_n = bias_prev, bias_next
        if jb == 0:
            bias_p = bias_prev + jnp.where(is_first, NEG_INF, 0.0)
        if jb == BLOCKS_PER_TILE - 1:
            bias_n = bias_next + jnp.where(is_last, NEG_INF, 0.0)
        biases.append((jnp.concatenate([bias_p] * 4, axis=1), jnp.concatenate([bias_n] * 4, axis=1)))
        q_ts.append(q[jb * BLOCK:(jb + 1) * BLOCK, :].T.astype(BF16))

    scores, probs, outs = {}, {}, {}
    for t in range(n_units + 2):
        if t == 3:
            wub_ref[...] = wu_ref[...].astype(BF16)
        if t == 5:
            wdb_ref[...] = wd_ref[...].astype(BF16)
        if t == 0:
            permute(0, COMMON_ROWS // 2)
        if t == 4:
            permute(COMMON_ROWS // 2, COMMON_ROWS // 2)
        if t == 2:
            _project_a(xn_ref, g1n_ref, shn_ref, scn_ref, wac_ref, cosn_ref, sinn_ref, nxt,
                       (step + 1) % 2, uc_ring, k_ring, vt_ring)
        if t < n_units:
            jb, g = divmod(t, N_KV_HEADS)
            scores[t] = _attn_scores(q_ts[jb], g, window(k_ring, jb, 0), k_ctx, *biases[jb])
            gate_chunk(t)
        if 0 <= t - 1 < n_units:
            g = (t - 1) % N_KV_HEADS
            probs[t - 1] = _attn_probs(scores.pop(t - 1),
                                       sink_row[:, g * 4 * BLOCK:(g + 1) * 4 * BLOCK])
        if 0 <= t - 2 < n_units:
            jb, g = divmod(t - 2, N_KV_HEADS)
            outs[t - 2] = _attn_values(
                probs.pop(t - 2), sink_row[:, g * 4 * BLOCK:(g + 1) * 4 * BLOCK],
                window(vt_ring, jb, 1)[g * HEAD_DIM:(g + 1) * HEAD_DIM, :],
                vt_ctx[g * HEAD_DIM:(g + 1) * HEAD_DIM, :])
    o_blocks = []
    for jb in range(BLOCKS_PER_TILE):
        o_rows = [outs[jb * N_KV_HEADS + g][:, h * BLOCK:(h + 1) * BLOCK]
                  for g in range(N_KV_HEADS) for h in range(4)]
        o_blocks.append(jnp.concatenate(o_rows, axis=0).T)
    gates = jnp.concatenate(gate_chunks, axis=1)

    uc = uc_ring[step % 2]
    row = lax.broadcasted_iota(I32, (SEQ_TILE, 1), 0)
    prev_row = jnp.where(is_first, 0.0, uc_last[0:1, :])
    next_row = jnp.where(is_last, 0.0, uc_ring[(step + 1) % 2, 0:1, :])
    uc_last[0:1, :] = uc[SEQ_TILE - 1:SEQ_TILE, :]
    up = jnp.where(row == 0, prev_row, pltpu.roll(uc, 1, axis=0))
    dn = jnp.where(row == SEQ_TILE - 1, next_row, pltpu.roll(uc, SEQ_TILE - 1, axis=0))
    wconv = wconv_ref[...]
    y = bconv_ref[...] + (up * wconv[0:1, :] + uc * wconv[1:2, :] + dn * wconv[2:3, :])
    ya = jnp.dot((bq[:, :CONV_W] * y).astype(BF16), wa_ref[...], preferred_element_type=F32)

    o = jnp.concatenate(o_blocks, axis=0).astype(BF16)
    yb = jnp.dot(o, wbb_ref[...], preferred_element_type=F32)
    merged = (_sigmoid(gates[:, :D_MODEL]) * ya + _sigmoid(gates[:, D_MODEL:]) * yb).astype(BF16)
    x1 = x + gt1_ref[0] * jnp.dot(merged, wo_ref[...], preferred_element_type=F32)
    x1_ref[0] = x1
    h2 = _rms_mod(x1, g2n_ref[...], sh2_ref[0], sc2_ref[0]).astype(BF16)
    logits_t = lax.dot_general(wr_ref[...], h2, (((1,), (1,)), ((), ())),
                               preferred_element_type=F32) + br_ref[...]
    permute_tail()
    h2_scr[...] = h2
    route_scr[...] = _route(logits_t)


N_MIX_OUTPUTS = 6
N_MIX_SCRATCH = 6


def _mix_kernel(*refs, n_seq_tiles, batch):
    n_tail = N_MIX_OUTPUTS + N_MIX_SCRATCH
    upper_ref, lower_ref = refs[-n_tail - 2:-n_tail]
    _, xl_ref, pos_ref, nb_ref, _, _ = refs[-n_tail:-N_MIX_SCRATCH]
    h2_scr, route_scr = refs[-N_MIX_SCRATCH:-N_MIX_SCRATCH + 2]
    step = pl.program_id(0)
    n_main = n_seq_tiles * batch

    @pl.when(step == 0)
    def _():
        h2_scr[...] = jnp.zeros_like(h2_scr)
        route_scr[...] = jnp.zeros_like(route_scr)

    @pl.when(step < n_main)
    def _():
        _mix_step(*refs, n_seq_tiles=n_seq_tiles)

    @pl.when(step == n_main)
    def _():
        permute, permute_tail = _sort_chunk(h2_scr[...], route_scr[...], upper_ref, lower_ref,
                                            xl_ref, pos_ref, nb_ref)
        permute(0, COMMON_ROWS)
        permute_tail()


def _mix(x, norm1_g, norm2_g, mod3, w_a_cols, w_b_cols, cos_t, sin_t, k_ctx, vt_ctx,
         w_conv, b_conv, w_a, w_b, w_o, sink_row, bias, wr_t, br_col, w_up, w_down, upper, lower):
    bsz, seq, _ = x.shape
    nt = seq // SEQ_TILE
    clen = k_ctx.shape[1]
    nch = n_main = bsz * nt
    assert N_EXPERTS % n_main == 0, "each grid step converts an equal share of the experts"
    e_step = N_EXPERTS // n_main

    def tile(s):
        return divmod(jnp.clip(s, 0, n_main - 1), nt)

    def per_tile(shape, index, shift=0):
        return pl.BlockSpec(shape, lambda s: index(*tile(s + shift)))

    def const_spec(shape):
        return pl.BlockSpec(shape, lambda s: tuple(0 for _ in shape),
                            pipeline_mode=pl.Buffered(1))

    def x_spec(shift=0):
        return per_tile((1, SEQ_TILE, D_MODEL), lambda b, t: (b, t, 0), shift)

    def mod_spec(j, shift=0):
        return per_tile((1, 1, D_MODEL), lambda b, t: (b, 0, j), shift)

    def rope_spec(shift=0):
        return per_tile((SEQ_TILE, LANES), lambda b, t: (t, 0), shift)

    def expert_spec(w):
        return pl.BlockSpec((e_step,) + w.shape[1:], lambda s: (jnp.minimum(s, n_main - 1), 0, 0))

    def chunk_spec(shape):
        return pl.BlockSpec(shape, lambda s: (jnp.maximum(s - 1, 0),) + (0,) * (len(shape) - 1))

    in_specs = [
        x_spec(), const_spec((1, D_MODEL)), const_spec((1, D_MODEL)),
        mod_spec(MOD_SHIFT1), mod_spec(MOD_SCALE1), mod_spec(MOD_GATE1), mod_spec(MOD_SHIFT2),
        mod_spec(MOD_SCALE2),
        const_spec(w_b_cols.shape), rope_spec(), rope_spec(),
        x_spec(1), mod_spec(MOD_SHIFT1, 1), mod_spec(MOD_SCALE1, 1), rope_spec(1), rope_spec(1),
        const_spec(w_a_cols.shape),
        per_tile((1, clen, KV_W), lambda b, t: (b, 0, 0)),
        per_tile((1, KV_W, clen), lambda b, t: (b, 0, 0)),
        const_spec(w_conv.shape), const_spec(b_conv.shape),
        const_spec(w_a.shape), const_spec(w_b.shape), const_spec(w_o.shape),
        const_spec(sink_row.shape), const_spec(bias.shape),
        const_spec(wr_t.shape), const_spec(br_col.shape),
        expert_spec(w_up), expert_spec(w_down),
        const_spec(upper.shape), const_spec(lower.shape),
    ]
    out_specs = [
        x_spec(), chunk_spec((CHUNK_ROWS, XL_COLS)), chunk_spec((CHUNK, LANES)),
        chunk_spec((1, N_EXPERTS, LANES)), expert_spec(w_up), expert_spec(w_down),
    ]
    out_shape = [
        jax.ShapeDtypeStruct((bsz, seq, D_MODEL), F32),
        jax.ShapeDtypeStruct((nch * CHUNK_ROWS, XL_COLS), BF16),
        jax.ShapeDtypeStruct((nch * CHUNK, LANES), F32),
        jax.ShapeDtypeStruct((nch, N_EXPERTS, LANES), I32),
        jax.ShapeDtypeStruct(w_up.shape, BF16),
        jax.ShapeDtypeStruct(w_down.shape, BF16),
    ]
    return pl.pallas_call(
        functools.partial(_mix_kernel, n_seq_tiles=nt, batch=bsz),
        grid=(n_main + 1,),
        in_specs=in_specs,
        out_specs=out_specs,
        out_shape=out_shape,
        scratch_shapes=[pltpu.VMEM((CHUNK, D_MODEL), BF16), pltpu.VMEM((ROUTE_ROWS, CHUNK), F32),
                        pltpu.VMEM((2, SEQ_TILE, CONV_W), F32), pltpu.VMEM((8, CONV_W), F32),
                        pltpu.VMEM((RING, SEQ_TILE, KV_W), BF16),
                        pltpu.VMEM((RING, KV_W, SEQ_TILE), BF16)],
        compiler_params=_cparams(("arbitrary",)),
        name="token_mix",
    )(x, norm1_g, norm2_g, mod3, mod3, mod3, mod3, mod3, w_b_cols, cos_t, sin_t,
      x, mod3, mod3, cos_t, sin_t, w_a_cols, k_ctx, vt_ctx,
      w_conv, b_conv, w_a, w_b, w_o, sink_row, bias, wr_t, br_col, w_up, w_down, upper, lower)


def _bf16_parts(v):
    hi = v.astype(BF16).astype(F32)
    r1 = v - hi
    mid = r1.astype(BF16).astype(F32)
    lo = (r1 - mid).astype(BF16).astype(F32)
    return hi, mid, lo


def _sort_chunk(h, route, upper_ref, lower_ref, xl_ref, pos_ref, nb_ref):
    e1 = route[0:1, :].astype(I32)
    e2 = route[1:2, :].astype(I32)
    erow = lax.broadcasted_iota(I32, (N_EXPERTS, CHUNK), 0)
    hit1 = erow == e1
    hit2 = erow == e2
    onehot = jnp.where(hit1, 1.0, 0.0) + jnp.where(hit2, 1.0, 0.0)
    cum = jnp.dot(onehot.astype(BF16), upper_ref[...], preferred_element_type=F32)
    cnt = jnp.sum(onehot, axis=1, keepdims=True)
    nblk = jnp.floor((cnt + (ROW_BLOCK - 1)) * (1.0 / ROW_BLOCK))
    nblk_b = jnp.broadcast_to(nblk, (N_EXPERTS, LANES))
    seg = jnp.dot(lower_ref[...], nblk_b.astype(BF16), preferred_element_type=F32) * ROW_BLOCK
    base = seg[:, 0:1] + cum
    pos1 = jnp.sum(jnp.where(hit1, base, 0.0), axis=0, keepdims=True)
    pos2 = jnp.sum(jnp.where(hit2, base, 0.0), axis=0, keepdims=True)
    p1i = pos1.astype(I32)
    p2i = pos2.astype(I32)
    used_rows = (jnp.sum(nblk) * ROW_BLOCK).astype(I32)

    prow = lax.broadcasted_iota(I32, (LANES, CHUNK), 0)
    parts = _bf16_parts(route[2:3, :]) + _bf16_parts(route[3:4, :])
    gpart_rows = jnp.zeros((LANES, CHUNK), F32)
    for j, part in enumerate(parts):
        gpart_rows = jnp.where(prow == j, part, gpart_rows)
    h_ext = jnp.concatenate([h, gpart_rows.T.astype(BF16)], axis=1)
    glane = lax.broadcasted_iota(I32, (1, GATE_COLS), 1)
    one = jnp.ones((), BF16)
    zero = jnp.zeros((), BF16)
    def permute(row0, n_rows):
        r16 = lax.broadcasted_iota(I32, (n_rows, CHUNK), 0).astype(jnp.int16)
        q1 = (p1i - row0).astype(jnp.int16)
        q2 = (p2i - row0).astype(jnp.int16)
        sel1 = jnp.where(r16 == q1, one, zero)
        sel = jnp.where(r16 == q2, one, sel1)
        is_slot1 = jnp.max(sel1, axis=1, keepdims=True)
        rows = pl.ds(row0, n_rows)
        xg = jnp.dot(sel, h_ext, preferred_element_type=F32)
        xl_ref[rows, 0:D_MODEL] = xg[:, :D_MODEL].astype(BF16)
        g6 = xg[:, D_MODEL:]
        g3 = jnp.where(is_slot1.astype(F32) > 0.0, g6, pltpu.roll(g6, GATE_COLS - 3, axis=1))
        xl_ref[rows, D_MODEL:XL_COLS] = jnp.where(glane < 3, g3, 0.0).astype(BF16)

    pos_rows = jnp.where(prow == 0, pos1, jnp.where(prow == 1, pos2, 0.0))
    pos_ref[...] = pos_rows.T
    nb_ref[0] = nblk_b.astype(I32)

    def permute_tail():
        tail_rows = CHUNK_ROWS - COMMON_ROWS

        @pl.when(COMMON_ROWS < used_rows)
        def _():
            permute(COMMON_ROWS, tail_rows)

        @pl.when(COMMON_ROWS >= used_rows)
        def _():
            xl_ref[pl.ds(COMMON_ROWS, tail_rows), :] = jnp.zeros((tail_rows, XL_COLS), BF16)

    return permute, permute_tail


def _max_tiles(nch):
    max_blocks = nch * (2 * CHUNK + N_EXPERTS * (ROW_BLOCK - 1)) // ROW_BLOCK
    return max_blocks // TILE_BLOCKS + N_EXPERTS


def _masked_prefix(le, values):
    delta = values - jnp.concatenate([jnp.zeros((1,), values.dtype), values[:-1]])
    return jnp.sum(jnp.where(le, delta[None, :], 0), axis=1)


def _tile_plan(nb, nch):
    n_tiles = _max_tiles(nch)
    nbt = nb.T
    nbe = jnp.sum(nbt, axis=1)
    nte = (nbe + TILE_BLOCKS - 1) // TILE_BLOCKS
    tile_end = jnp.cumsum(nte)
    tile_start = tile_end - nte
    n_act = tile_end[-1]
    tiles = jnp.arange(n_tiles, dtype=I32)
    te = jnp.sum((tile_end[None, :] <= tiles[:, None]).astype(I32), axis=1)
    e_ar = jnp.arange(N_EXPERTS, dtype=I32)
    active = nte > 0
    te = jnp.where(tiles < n_act, te, jnp.max(jnp.where(active, e_ar, 0)))
    first = jnp.logical_and(te != jnp.concatenate([jnp.full((1,), -1, I32), te[:-1]]),
                            tiles < n_act).astype(I32)
    rank = jnp.cumsum(active.astype(I32)) - 1
    k_ar = jnp.arange(N_EXPERTS + W_SLOTS, dtype=I32)
    eseq = jnp.sum(jnp.where(jnp.logical_and(active[None, :], rank[None, :] == k_ar[:, None]),
                             e_ar[None, :], 0), axis=1)
    n_exp = jnp.sum(active.astype(I32))

    cb_excl = jnp.cumsum(nbt, axis=1) - nbt
    gs = TILE_BLOCKS * tile_start[:, None] + cb_excl
    seg_blk = jnp.cumsum(nb, axis=1) - nb
    base_blk = jnp.arange(nch, dtype=I32)[None, :] * CHUNK_BLOCKS + seg_blk.T
    gs_f = gs.reshape(-1)
    slots = jnp.arange(n_tiles * TILE_BLOCKS, dtype=I32)
    le = gs_f[None, :] <= slots[:, None]
    blk = slots + _masked_prefix(le, (base_blk - gs).reshape(-1))
    valid = slots < _masked_prefix(le, (gs + nbt).reshape(-1))
    gblk = jnp.concatenate([jnp.where(valid, blk, 0).astype(I32),
                            jnp.zeros(((X_SLOTS - 1) * TILE_BLOCKS,), I32)])

    m = jnp.arange(CHUNK_BLOCKS, dtype=I32)
    le_c = seg_blk[:, None, :] <= m[None, :, None]
    shift = (gs.T - seg_blk)
    delta = shift - jnp.concatenate([jnp.zeros((nch, 1), I32), shift[:, :-1]], axis=1)
    slot_of = m[None, :] + jnp.sum(jnp.where(le_c, delta[:, None, :], 0), axis=2)
    used_blocks = jnp.sum(nb, axis=1)
    slot_of = jnp.where(m[None, :] < used_blocks[:, None], slot_of, 0).astype(I32)
    pad = Y_RING - 1
    slot_of = jnp.concatenate([slot_of.reshape(-1), jnp.zeros((pad * CHUNK_BLOCKS,), I32)])
    used_rows = jnp.concatenate([ROW_BLOCK * used_blocks, jnp.zeros((pad,), I32)]).astype(I32)
    return ((n_act.reshape(1).astype(I32), n_exp.reshape(1), first, eseq, gblk),
            (used_rows, slot_of))


def _gmm_kernel(nact_ref, nexp_ref, first_ref, eseq_ref, gblk_ref,
                xl_hbm, wu_hbm, wd_hbm, yt_hbm,
                xbuf, ybuf, zbuf, wu_st, wd_st, in_sem, out_sem, w_sem, z_sem):
    n_act = nact_ref[0]
    n_exp = nexp_ref[0]
    n_tiles = yt_hbm.shape[0] // TILE_BLOCKS
    prefetch = X_SLOTS - 1

    def gather(tile, b):
        s = tile % X_SLOTS
        blk = gblk_ref[tile * TILE_BLOCKS + b]
        return pltpu.make_async_copy(xl_hbm.at[blk], xbuf.at[s, b], in_sem.at[s])

    def write_back(tile):
        s = tile % Y_SLOTS
        return pltpu.make_async_copy(ybuf.at[s], yt_hbm.at[pl.ds(tile * TILE_BLOCKS, TILE_BLOCKS)],
                                     out_sem.at[s])

    def zero_fill(tile):
        return pltpu.make_async_copy(zbuf, yt_hbm.at[pl.ds(tile * TILE_BLOCKS, TILE_BLOCKS)], z_sem)

    def weights(q):
        e = eseq_ref[q]
        s = q % W_SLOTS
        return (pltpu.make_async_copy(wu_hbm.at[e], wu_st.at[s], w_sem.at[0, s]),
                pltpu.make_async_copy(wd_hbm.at[e], wd_st.at[s], w_sem.at[1, s]))

    def tile_blocks(fn):
        for b in range(TILE_BLOCKS):
            fn(b)

    for q0 in range(W_SLOTS - 1):
        @pl.when(q0 < n_exp)
        def _(q0=q0):
            for cp in weights(q0):
                cp.start()
    for t0 in range(prefetch):
        tile_blocks(lambda b, t0=t0: gather(t0, b).start())

    def body(i, q):
        tile_blocks(lambda b: gather(i, b).wait())

        @pl.when(i >= Y_SLOTS)
        def _():
            write_back(i - Y_SLOTS).wait()

        is_first = first_ref[i] == 1

        @pl.when(is_first)
        def _():
            for cp in weights(q):
                cp.wait()

            @pl.when(q + W_SLOTS - 1 < n_exp)
            def _():
                for cp in weights(q + W_SLOTS - 1):
                    cp.start()

        q = q + is_first.astype(I32)
        ws = (q - 1) % W_SLOTS
        xs = i % X_SLOTS
        ys = i % Y_SLOTS

        x = xbuf[xs].reshape(TILE_ROWS, XL_COLS)
        gate = jnp.sum(x[:, D_MODEL:].astype(F32), axis=1, keepdims=True)
        au = jnp.dot(x[:, :D_MODEL], wu_st[ws], preferred_element_type=F32)
        tile_blocks(lambda b: gather(i + prefetch, b).start())
        a = au[:, :EXPERT_FF]
        act = (a * _sigmoid(a)) * au[:, EXPERT_FF:]
        y = jnp.dot(act.astype(BF16), wd_st[ws], preferred_element_type=F32)
        ybuf[ys] = (gate * y).astype(BF16).reshape(TILE_BLOCKS, ROW_BLOCK, D_MODEL)
        write_back(i).start()

        @pl.when(n_act + i < n_tiles)
        def _():
            zero_fill(n_act + i).start()
        return q

    zbuf[...] = jnp.zeros_like(zbuf)
    lax.fori_loop(0, n_act, body, jnp.int32(0))

    def fill_rest(t, carry):
        zero_fill(t).start()
        return carry

    lax.fori_loop(jnp.minimum(2 * n_act, n_tiles), n_tiles, fill_rest, 0)

    def drain(t, carry):
        zero_fill(t).wait()
        return carry

    lax.fori_loop(n_act, n_tiles, drain, 0)
    for k in range(prefetch):
        tile_blocks(lambda b, k=k: gather(n_act + k, b).wait())
    for k in range(Y_SLOTS):
        @pl.when(n_act - 1 - k >= 0)
        def _(k=k):
            write_back(n_act - 1 - k).wait()


def _grouped_mlp(plan, xl, w_up, w_down, n_tiles):
    grid_spec = pltpu.PrefetchScalarGridSpec(
        num_scalar_prefetch=len(plan),
        grid=(1,),
        in_specs=[pl.BlockSpec(memory_space=pl.ANY)] * 3,
        out_specs=pl.BlockSpec(memory_space=pl.ANY),
        scratch_shapes=[pltpu.VMEM((X_SLOTS, TILE_BLOCKS, ROW_BLOCK, XL_COLS), BF16),
                        pltpu.VMEM((Y_SLOTS, TILE_BLOCKS, ROW_BLOCK, D_MODEL), BF16),
                        pltpu.VMEM((TILE_BLOCKS, ROW_BLOCK, D_MODEL), BF16),
                        pltpu.VMEM((W_SLOTS, D_MODEL, 2 * EXPERT_FF), BF16),
                        pltpu.VMEM((W_SLOTS, EXPERT_FF, D_MODEL), BF16),
                        pltpu.SemaphoreType.DMA((X_SLOTS,)),
                        pltpu.SemaphoreType.DMA((Y_SLOTS,)),
                        pltpu.SemaphoreType.DMA((2, W_SLOTS)),
                        pltpu.SemaphoreType.DMA(())])
    return pl.pallas_call(
        _gmm_kernel,
        grid_spec=grid_spec,
        out_shape=jax.ShapeDtypeStruct((n_tiles * TILE_BLOCKS, ROW_BLOCK, D_MODEL), BF16),
        compiler_params=_cparams(("arbitrary",)),
        name="moe_experts",
    )(*plan, xl, w_up, w_down)


def _combine_kernel(used_ref, slot_ref, yt_hbm, pos_ref, x1_ref, gt2_ref, fg_ref, o_ref, ybuf, sem):
    c = pl.program_id(0)
    n_chunks = pl.num_programs(0)
    used_rows = used_ref[c]

    def chunk_blocks(chunk, op):
        s = chunk % Y_RING

        def copy(b):
            slot = slot_ref[chunk * CHUNK_BLOCKS + b]
            return pltpu.make_async_copy(yt_hbm.at[slot], ybuf.at[s, b], sem.at[s])

        for b in range(COMMON_BLOCKS):
            op(copy(b))

        @pl.when(used_ref[chunk] > COMMON_ROWS)
        def _():
            for b in range(COMMON_BLOCKS, CHUNK_BLOCKS):
                op(copy(b))

    @pl.when(c == 0)
    def _():
        for k in range(Y_RING - 1):
            chunk_blocks(k, lambda cp: cp.start())

    chunk_blocks(c + Y_RING - 1, lambda cp: cp.start())
    chunk_blocks(c, lambda cp: cp.wait())

    def body(k_rows):
        pos = pos_ref[...]
        p1 = pos[:, 0:1].astype(I32)
        p2 = pos[:, 1:2].astype(I32)
        r = lax.broadcasted_iota(I32, (CHUNK, k_rows), 1)
        sel = jnp.where(r == p1, 1.0, jnp.where(r == p2, 1.0, 0.0)).astype(BF16)
        yl = ybuf[c % Y_RING, 0:k_rows // ROW_BLOCK].reshape(k_rows, D_MODEL)
        y = jnp.dot(sel, yl, preferred_element_type=F32)
        x2 = x1_ref[...] + gt2_ref[0] * y
        inv = lax.rsqrt(jnp.mean(x2 * x2, axis=-1, keepdims=True) + NORM_EPS)
        o_ref[...] = (x2 * inv) * fg_ref[...]

    pl.when(used_rows <= COMMON_ROWS)(lambda: body(COMMON_ROWS))
    pl.when(used_rows > COMMON_ROWS)(lambda: body(CHUNK_ROWS))

    @pl.when(c == n_chunks - 1)
    def _():
        for k in range(1, Y_RING):
            chunk_blocks(c + k, lambda cp: cp.wait())


def _combine(used_rows, slot_of, yt, pos_c, x1, mod3, final_g, seq):
    n = x1.shape[0]
    per_seq = seq // CHUNK
    grid_spec = pltpu.PrefetchScalarGridSpec(
        num_scalar_prefetch=2,
        grid=(n // CHUNK,),
        in_specs=[pl.BlockSpec(memory_space=pl.ANY),
                  pl.BlockSpec((CHUNK, LANES), lambda c, u, s: (c, 0)),
                  pl.BlockSpec((CHUNK, D_MODEL), lambda c, u, s: (c, 0)),
                  pl.BlockSpec((1, 1, D_MODEL), lambda c, u, s: (c // per_seq, 0, MOD_GATE2)),
                  pl.BlockSpec((1, D_MODEL), lambda c, u, s: (0, 0))],
        out_specs=pl.BlockSpec((CHUNK, D_MODEL), lambda c, u, s: (c, 0)),
        scratch_shapes=[pltpu.VMEM((Y_RING, CHUNK_BLOCKS, ROW_BLOCK, D_MODEL), BF16),
                        pltpu.SemaphoreType.DMA((Y_RING,))])
    return pl.pallas_call(
        _combine_kernel,
        grid_spec=grid_spec,
        out_shape=jax.ShapeDtypeStruct((n, D_MODEL), F32),
        compiler_params=_cparams(("arbitrary",)),
        name="moe_combine",
    )(used_rows, slot_of, yt, pos_c, x1, mod3, final_g)


def _rope_tables(seq):
    n_freq = HEAD_DIM // 4
    inv_freq = ROPE_BASE ** (-jnp.arange(n_freq, dtype=F32) / n_freq)
    rows = seq // GRID_W
    row = jnp.repeat(jnp.arange(rows, dtype=F32), GRID_W)
    col = jnp.tile(jnp.arange(GRID_W, dtype=F32), rows)
    ang = jnp.concatenate([row[:, None] * inv_freq, col[:, None] * inv_freq], axis=-1)
    cos, sin = jnp.cos(ang), jnp.sin(ang)
    return jnp.tile(cos, (1, 4)), jnp.concatenate([-sin, sin, -sin, sin], axis=1)


def _window_bias():
    key = np.arange(3 * BLOCK)[:, None]
    qry = np.arange(BLOCK)[None, :]
    valid = (key - qry >= 0) & (key - qry <= 2 * BLOCK)
    return jnp.asarray(np.where(valid, 0.0, NEG_INF), F32)


def kernel(x, c, ctx, c_ctx, w_ada, b_ada, norm1_g, w_in, w_conv, b_conv, w_a, w_b, sink, w_o,
           norm2_g, w_group, b_group, w_router, b_router, w_up, w_down, final_g):
    bsz, seq, _ = x.shape
    assert w_ada.shape[0] == 1 and seq % SEQ_TILE == 0

    cc = jnp.zeros((16, D_MODEL), F32).at[:bsz].set(c).at[bsz].set(c_ctx)
    mod = _modulation(cc, w_ada[0], b_ada[0][None, :])
    mod3 = mod[:bsz].reshape(bsz, 1, N_MOD * D_MODEL)
    csh1 = mod[bsz:bsz + 1, 0:D_MODEL]
    csc1 = mod[bsz:bsz + 1, D_MODEL:2 * D_MODEL]

    w = w_in[0]
    n1g = norm1_g[0][None, :]
    n2g = norm2_g[0][None, :]
    w_kv = w[:, OFF_K:OFF_GA].astype(BF16)
    w_a_cols = jnp.concatenate([w[:, OFF_CG:OFF_Q], w[:, OFF_K:OFF_GA]], axis=1).astype(BF16)
    w_b_cols = jnp.concatenate([w[:, OFF_B:OFF_CG], w[:, OFF_Q:OFF_K], w[:, OFF_GA:]],
                               axis=1).astype(BF16)
    cos_t, sin_t = _rope_tables(seq)

    k_ctx, vt_ctx = _context_kv(ctx, n1g, csh1, csc1, w_kv)
    sink_row = jnp.repeat(sink[0].astype(F32), BLOCK)[None, :]
    pad_g = jnp.zeros((D_MODEL, GROUP_ROWS - N_GROUPS), F32)
    pad_e = jnp.zeros((D_MODEL, ROUTER_ROWS - GROUP_ROWS - N_EXPERTS), F32)
    wr_t = jnp.concatenate([w_group[0], pad_g, w_router[0], pad_e], axis=1).T.astype(BF16)
    br_col = jnp.concatenate([b_group[0], pad_g[0], b_router[0], pad_e[0]])[:, None]

    upper = jnp.asarray(np.triu(np.ones((CHUNK, CHUNK), np.float32), 1), BF16)
    lower = jnp.asarray(np.tril(np.ones((N_EXPERTS, N_EXPERTS), np.float32), -1), BF16)
    x1, xl, pos_c, nb, w_up_bf, w_down_bf = _mix(
        x, n1g, n2g, mod3, w_a_cols, w_b_cols, cos_t, sin_t, k_ctx, vt_ctx,
        w_conv[0], b_conv[0][None, :], w_a[0].astype(BF16), w_b[0].astype(BF16),
        w_o[0].astype(BF16), sink_row, _window_bias(), wr_t, br_col, w_up[0], w_down[0],
        upper, lower)

    n = bsz * seq
    nch = n // CHUNK
    nb = nb[:, :, 0]
    expert_plan, combine_plan = _tile_plan(nb, nch)
    yt = _grouped_mlp(expert_plan, xl.reshape(-1, ROW_BLOCK, XL_COLS), w_up_bf, w_down_bf,
                      _max_tiles(nch))
    out = _combine(*combine_plan, yt, pos_c, x1.reshape(n, D_MODEL), mod3, final_g[None, :], seq)
    return out.reshape(bsz, seq, D_MODEL)
```

```python
import functools

import numpy as np
import jax
import jax.numpy as jnp
from jax import lax
from jax.experimental import pallas as pl
from jax.experimental.pallas import tpu as pltpu

F32 = jnp.float32
BF16 = jnp.bfloat16
I32 = jnp.int32

D_MODEL = 1024
GRID_W = 64
CONV_W = 512
N_HEADS = 8
N_KV_HEADS = 2
HEAD_DIM = 64
ATT_W = N_HEADS * HEAD_DIM
KV_W = N_KV_HEADS * HEAD_DIM
BLOCK = 128
ROPE_BASE = 10000.0
N_GROUPS = 4
EXPERTS_PER_GROUP = 8
N_EXPERTS = N_GROUPS * EXPERTS_PER_GROUP
EXPERT_FF = 256
N_MOD = 6
MOD_SHIFT1, MOD_SCALE1, MOD_GATE1, MOD_SHIFT2, MOD_SCALE2, MOD_GATE2 = range(N_MOD)
NORM_EPS = 1e-6
NEG_INF = -1e30
LOG2E = 1.4426950408889634
ONES_ROWS = 16

OFF_B, OFF_CG, OFF_XIN, OFF_Q, OFF_K, OFF_V, OFF_GA, OFF_GB = (
    0, 512, 1024, 1536, 2048, 2176, 2304, 3328)
IN_COLS = 4352

LANES = 128
SEQ_TILE = 512
BLOCKS_PER_TILE = SEQ_TILE // BLOCK
ROUTE_ROWS = 8
GROUP_ROWS = 8
ROUTER_ROWS = 48
VMEM_LIMIT = 62 * 1024 * 1024

CHUNK = SEQ_TILE
ROW_BLOCK = 16
CHUNK_ROWS = -(-(2 * CHUNK + N_EXPERTS * (ROW_BLOCK - 1)) // 256) * 256
CHUNK_BLOCKS = CHUNK_ROWS // ROW_BLOCK
GATE_COLS = LANES
XL_COLS = D_MODEL + GATE_COLS
COMMON_ROWS = 2 * CHUNK + 256
COMMON_BLOCKS = COMMON_ROWS // ROW_BLOCK
TILE_BLOCKS = 32
TILE_ROWS = TILE_BLOCKS * ROW_BLOCK
X_SLOTS = 6
Y_SLOTS = 4
W_SLOTS = 3
Y_RING = 3
RING = 3


def _cparams(sem):
    return pltpu.CompilerParams(dimension_semantics=sem, vmem_limit_bytes=VMEM_LIMIT)


def _rms_mod(x, g, shift, scale):
    inv = lax.rsqrt(jnp.mean(x * x, axis=-1, keepdims=True) + NORM_EPS)
    return (x * inv) * (g * (1.0 + scale)) + shift


def _mod_kernel(c_ref, w_ref, b_ref, o_ref):
    c = c_ref[...]
    a = (c * jax.nn.sigmoid(c)).astype(BF16)
    o_ref[...] = jnp.dot(a, w_ref[...].astype(BF16), preferred_element_type=F32) + b_ref[...]


def _modulation(cc, w_ada, b_ada):
    rows = cc.shape[0]
    cols = w_ada.shape[1]
    tile = 1024
    return pl.pallas_call(
        _mod_kernel,
        grid=(cols // tile,),
        in_specs=[pl.BlockSpec((rows, D_MODEL), lambda j: (0, 0)),
                  pl.BlockSpec((D_MODEL, tile), lambda j: (0, j)),
                  pl.BlockSpec((1, tile), lambda j: (0, j))],
        out_specs=pl.BlockSpec((rows, tile), lambda j: (0, j)),
        out_shape=jax.ShapeDtypeStruct((rows, cols), F32),
        compiler_params=_cparams(("arbitrary",)),
        name="adaln_mod",
    )(cc, w_ada, b_ada)


def _ctx_kernel(x_ref, g_ref, sh_ref, sc_ref, w_ref, k_ref, vt_ref):
    h = _rms_mod(x_ref[0], g_ref[...], sh_ref[...], sc_ref[...]).astype(BF16)
    kv = jnp.dot(h, w_ref[...], preferred_element_type=F32)
    k_ref[0] = kv[:, :KV_W].astype(BF16)
    vt_ref[0] = kv[:, KV_W:].T.astype(BF16)


def _context_kv(ctx, norm_g, csh, csc, w_kv):
    bsz, clen, _ = ctx.shape
    return pl.pallas_call(
        _ctx_kernel,
        grid=(bsz,),
        in_specs=[pl.BlockSpec((1, clen, D_MODEL), lambda b: (b, 0, 0)),
                  pl.BlockSpec((1, D_MODEL), lambda b: (0, 0)),
                  pl.BlockSpec((1, D_MODEL), lambda b: (0, 0)),
                  pl.BlockSpec((1, D_MODEL), lambda b: (0, 0)),
                  pl.BlockSpec((D_MODEL, 2 * KV_W), lambda b: (0, 0))],
        out_specs=[pl.BlockSpec((1, clen, KV_W), lambda b: (b, 0, 0)),
                   pl.BlockSpec((1, KV_W, clen), lambda b: (b, 0, 0))],
        out_shape=[jax.ShapeDtypeStruct((bsz, clen, KV_W), BF16),
                   jax.ShapeDtypeStruct((bsz, KV_W, clen), BF16)],
        compiler_params=_cparams(("arbitrary",)),
        name="context_kv",
    )(ctx, norm_g, csh, csc, w_kv)


def _rope(t, cos, sin_signed):
    lane = lax.broadcasted_iota(I32, (1, LANES), 1)
    first_half = (lane % HEAD_DIM) < (HEAD_DIM // 2)
    outs = []
    for j in range(t.shape[1] // LANES):
        tj = t[:, j * LANES:(j + 1) * LANES]
        partner = jnp.where(first_half,
                            pltpu.roll(tj, LANES - HEAD_DIM // 2, axis=1),
                            pltpu.roll(tj, HEAD_DIM // 2, axis=1))
        outs.append(tj * cos + partner * sin_signed)
    return outs[0] if len(outs) == 1 else jnp.concatenate(outs, axis=1)


def _sigmoid(x):
    return 1.0 / (1.0 + jnp.exp2(x * (-LOG2E)))


def _attn_scores(q_t, g, k_win, k_ctx, bias_p4, bias_n4):
    zeros = jnp.zeros((HEAD_DIM, 4 * BLOCK), BF16)
    qg = jnp.concatenate([q_t[(4 * g + h) * HEAD_DIM:(4 * g + h + 1) * HEAD_DIM, :]
                          for h in range(4)], axis=1)
    rhs = jnp.concatenate([qg, zeros] if g == 0 else [zeros, qg], axis=0)
    s_win = jnp.dot(k_win, rhs, preferred_element_type=F32)
    s_ctx = jnp.dot(k_ctx, rhs, preferred_element_type=F32)
    return (s_win[0:BLOCK] + bias_p4, s_win[BLOCK:2 * BLOCK], s_win[2 * BLOCK:] + bias_n4, s_ctx)


def _attn_probs(scores, sink):
    m = sink
    for s in scores:
        m = jnp.maximum(m, jnp.max(s, axis=0, keepdims=True))
    p_win = jnp.concatenate([jnp.exp2(s - m).astype(BF16) for s in scores[:3]], axis=0)
    p_ctx = jnp.exp2(scores[3] - m).astype(BF16)
    return p_win, p_ctx, m


def _attn_values(probs, sink, vt_win_g, vt_ctx_g):
    p_win, p_ctx, m = probs

    def with_ones(vt):
        r = lax.broadcasted_iota(I32, (ONES_ROWS, vt.shape[1]), 0)
        return jnp.concatenate([vt, jnp.where(r == 0, 1.0, 0.0).astype(BF16)], axis=0)

    o_ext = (jnp.dot(with_ones(vt_win_g), p_win, preferred_element_type=F32)
             + jnp.dot(with_ones(vt_ctx_g), p_ctx, preferred_element_type=F32))
    denom = o_ext[HEAD_DIM:HEAD_DIM + 1, :] + jnp.exp2(sink - m)
    return o_ext[:HEAD_DIM, :] / denom


def _route(logits_t):
    t = logits_t.shape[1]
    grow = lax.broadcasted_iota(I32, (GROUP_ROWS, t), 0)
    gl = jnp.where(grow < N_GROUPS, logits_t[0:GROUP_ROWS, :], NEG_INF)
    gm = jnp.max(gl, axis=0, keepdims=True)
    p_g = 1.0 / jnp.sum(jnp.exp(gl - gm), axis=0, keepdims=True)
    g_idx = jnp.min(jnp.where(gl == gm, grow, N_GROUPS), axis=0, keepdims=True)

    erow = lax.broadcasted_iota(I32, (N_EXPERTS, t), 0)
    el = logits_t[GROUP_ROWS:GROUP_ROWS + N_EXPERTS, :]
    sel = (erow // EXPERTS_PER_GROUP) == g_idx
    em = jnp.where(sel, el, NEG_INF)
    m1 = jnp.max(em, axis=0, keepdims=True)
    i1 = jnp.min(jnp.where(em == m1, erow, N_EXPERTS), axis=0, keepdims=True)
    em2 = jnp.where(erow == i1, NEG_INF, em)
    m2 = jnp.max(em2, axis=0, keepdims=True)
    i2 = jnp.min(jnp.where(em2 == m2, erow, N_EXPERTS), axis=0, keepdims=True)
    z = jnp.sum(jnp.where(sel, jnp.exp(el - m1), 0.0), axis=0, keepdims=True)
    p1 = 1.0 / z
    p2 = jnp.exp(m2 - m1) / z
    gate1 = p_g * p1 / (p1 + p2)
    gate2 = p_g * p2 / (p1 + p2)
    pad = jnp.zeros((ROUTE_ROWS - 4, t), F32)
    return jnp.concatenate([i1.astype(F32), i2.astype(F32), gate1, gate2, pad], axis=0)


def _project_a(x_ref, g_ref, sh_ref, sc_ref, w_ref, cos_ref, sin_ref, slot, uc_slot,
               uc_ring, k_ring, vt_ring):
    h = _rms_mod(x_ref[0], g_ref[...], sh_ref[0], sc_ref[0]).astype(BF16)
    u = jnp.dot(h, w_ref[...], preferred_element_type=F32)
    uc_ring[uc_slot] = u[:, :CONV_W] * u[:, CONV_W:2 * CONV_W]
    k = u[:, 2 * CONV_W:2 * CONV_W + KV_W]
    k_ring[slot] = _rope(k, cos_ref[...], sin_ref[...]).astype(BF16)
    vt_ring[slot] = u[:, 2 * CONV_W + KV_W:].T.astype(BF16)


def _mix_step(x_ref, g1n_ref, g2n_ref, sh1_ref, sc1_ref, gt1_ref, sh2_ref, sc2_ref,
              wb_ref, cos_ref, sin_ref,
              xn_ref, shn_ref, scn_ref, cosn_ref, sinn_ref, wac_ref,
              kc_ref, vtc_ref, wconv_ref, bconv_ref, wa_ref, wbb_ref, wo_ref,
              sink_ref, bias_ref, wr_ref, br_ref, wu_ref, wd_ref, upper_ref, lower_ref,
              x1_ref, xl_ref, pos_ref, nb_ref, wub_ref, wdb_ref,
              h2_scr, route_scr, uc_ring, uc_last, k_ring, vt_ring,
              *, n_seq_tiles):
    step = pl.program_id(0)
    t_idx = step % n_seq_tiles
    is_first = t_idx == 0
    is_last = t_idx == n_seq_tiles - 1
    own, prv, nxt = step % RING, (step + RING - 1) % RING, (step + 1) % RING

    permute, permute_tail = _sort_chunk(h2_scr[...], route_scr[...], upper_ref, lower_ref,
                                        xl_ref, pos_ref, nb_ref)

    @pl.when(step == 0)
    def _():
        uc_ring[...] = jnp.zeros_like(uc_ring)
        uc_last[...] = jnp.zeros_like(uc_last)
        k_ring[...] = jnp.zeros_like(k_ring)
        vt_ring[...] = jnp.zeros_like(vt_ring)
        _project_a(x_ref, g1n_ref, sh1_ref, sc1_ref, wac_ref, cos_ref, sin_ref, 0, 0,
                   uc_ring, k_ring, vt_ring)

    x = x_ref[0]
    hb = _rms_mod(x, g1n_ref[...], sh1_ref[0], sc1_ref[0]).astype(BF16)
    bq = jnp.dot(hb, wb_ref[:, 0:2 * CONV_W], preferred_element_type=F32)
    q = _rope(bq[:, CONV_W:], cos_ref[...], sin_ref[...]) * (HEAD_DIM ** -0.5 * LOG2E)

    def block_ring(ring, slot_, blk, axis):
        lo = blk * BLOCK
        return ring[slot_, lo:lo + BLOCK, :] if axis == 0 else ring[slot_, :, lo:lo + BLOCK]

    def window(ring, jb, axis):
        parts = []
        for blk in (jb - 1, jb, jb + 1):
            if blk < 0:
                parts.append(block_ring(ring, prv, BLOCKS_PER_TILE - 1, axis))
            elif blk >= BLOCKS_PER_TILE:
                parts.append(block_ring(ring, nxt, 0, axis))
            else:
                parts.append(block_ring(ring, own, blk, axis))
        return jnp.concatenate(parts, axis=axis)

    k_ctx = kc_ref[0]
    vt_ctx = vtc_ref[0]
    bias_prev = bias_ref[0:BLOCK, :]
    bias_next = bias_ref[2 * BLOCK:3 * BLOCK, :]
    sink_row = sink_ref[...] * LOG2E
    n_units = BLOCKS_PER_TILE * N_KV_HEADS
    gate_cols = 2 * D_MODEL // n_units
    gate_chunks = []

    def gate_chunk(u):
        c0 = 2 * CONV_W + u * gate_cols
        gate_chunks.append(jnp.dot(hb, wb_ref[:, c0:c0 + gate_cols], preferred_element_type=F32))

    q_ts, biases = [], []
    for jb in range(BLOCKS_PER_TILE):
        bias_p, bias_n = bias_prev, bias_next
        if jb == 0:
            bias_p = bias_prev + jnp.where(is_first, NEG_INF, 0.0)
        if jb == BLOCKS_PER_TILE - 1:
            bias_n = bias_next + jnp.where(is_last, NEG_INF, 0.0)
        biases.append((jnp.concatenate([bias_p] * 4, axis=1), jnp.concatenate([bias_n] * 4, axis=1)))
        q_ts.append(q[jb * BLOCK:(jb + 1) * BLOCK, :].T.astype(BF16))

    scores, probs, outs = {}, {}, {}
    for t in range(n_units + 2):
        if t == 3:
            wub_ref[...] = wu_ref[...].astype(BF16)
        if t == 5:
            wdb_ref[...] = wd_ref[...].astype(BF16)
        if t == 0:
            permute(0, COMMON_ROWS // 2)
        if t == 4:
            permute(COMMON_ROWS // 2, COMMON_ROWS // 2)
        if t == 2:
            _project_a(xn_ref, g1n_ref, shn_ref, scn_ref, wac_ref, cosn_ref, sinn_ref, nxt,
                       (step + 1) % 2, uc_ring, k_ring, vt_ring)
        if t < n_units:
            jb, g = divmod(t, N_KV_HEADS)
            scores[t] = _attn_scores(q_ts[jb], g, window(k_ring, jb, 0), k_ctx, *biases[jb])
            gate_chunk(t)
        if 0 <= t - 1 < n_units:
            g = (t - 1) % N_KV_HEADS
            probs[t - 1] = _attn_probs(scores.pop(t - 1),
                                       sink_row[:, g * 4 * BLOCK:(g + 1) * 4 * BLOCK])
        if 0 <= t - 2 < n_units:
            jb, g = divmod(t - 2, N_KV_HEADS)
            outs[t - 2] = _attn_values(
                probs.pop(t - 2), sink_row[:, g * 4 * BLOCK:(g + 1) * 4 * BLOCK],
                window(vt_ring, jb, 1)[g * HEAD_DIM:(g + 1) * HEAD_DIM, :],
                vt_ctx[g * HEAD_DIM:(g + 1) * HEAD_DIM, :])
    o_blocks = []
    for jb in range(BLOCKS_PER_TILE):
        o_rows = [outs[jb * N_KV_HEADS + g][:, h * BLOCK:(h + 1) * BLOCK]
                  for g in range(N_KV_HEADS) for h in range(4)]
        o_blocks.append(jnp.concatenate(o_rows, axis=0).T)
    gates = jnp.concatenate(gate_chunks, axis=1)

    uc = uc_ring[step % 2]
    row = lax.broadcasted_iota(I32, (SEQ_TILE, 1), 0)
    prev_row = jnp.where(is_first, 0.0, uc_last[0:1, :])
    next_row = jnp.where(is_last, 0.0, uc_ring[(step + 1) % 2, 0:1, :])
    uc_last[0:1, :] = uc[SEQ_TILE - 1:SEQ_TILE, :]
    up = jnp.where(row == 0, prev_row, pltpu.roll(uc, 1, axis=0))
    dn = jnp.where(row == SEQ_TILE - 1, next_row, pltpu.roll(uc, SEQ_TILE - 1, axis=0))
    wconv = wconv_ref[...]
    y = bconv_ref[...] + (up * wconv[0:1, :] + uc * wconv[1:2, :] + dn * wconv[2:3, :])
    ya = jnp.dot((bq[:, :CONV_W] * y).astype(BF16), wa_ref[...], preferred_element_type=F32)

    o = jnp.concatenate(o_blocks, axis=0).astype(BF16)
    yb = jnp.dot(o, wbb_ref[...], preferred_element_type=F32)
    merged = (_sigmoid(gates[:, :D_MODEL]) * ya + _sigmoid(gates[:, D_MODEL:]) * yb).astype(BF16)
    x1 = x + gt1_ref[0] * jnp.dot(merged, wo_ref[...], preferred_element_type=F32)
    x1_ref[0] = x1
    h2 = _rms_mod(x1, g2n_ref[...], sh2_ref[0], sc2_ref[0]).astype(BF16)
    logits_t = lax.dot_general(wr_ref[...], h2, (((1,), (1,)), ((), ())),
                               preferred_element_type=F32) + br_ref[...]
    permute_tail()
    h2_scr[...] = h2
    route_scr[...] = _route(logits_t)


N_MIX_OUTPUTS = 6
N_MIX_SCRATCH = 6


def _mix_kernel(*refs, n_seq_tiles, batch):
    n_tail = N_MIX_OUTPUTS + N_MIX_SCRATCH
    upper_ref, lower_ref = refs[-n_tail - 2:-n_tail]
    _, xl_ref, pos_ref, nb_ref, _, _ = refs[-n_tail:-N_MIX_SCRATCH]
    h2_scr, route_scr = refs[-N_MIX_SCRATCH:-N_MIX_SCRATCH + 2]
    step = pl.program_id(0)
    n_main = n_seq_tiles * batch

    @pl.when(step == 0)
    def _():
        h2_scr[...] = jnp.zeros_like(h2_scr)
        route_scr[...] = jnp.zeros_like(route_scr)

    @pl.when(step < n_main)
    def _():
        _mix_step(*refs, n_seq_tiles=n_seq_tiles)

    @pl.when(step == n_main)
    def _():
        permute, permute_tail = _sort_chunk(h2_scr[...], route_scr[...], upper_ref, lower_ref,
                                            xl_ref, pos_ref, nb_ref)
        permute(0, COMMON_ROWS)
        permute_tail()


def _mix(x, norm1_g, norm2_g, mod3, w_a_cols, w_b_cols, cos_t, sin_t, k_ctx, vt_ctx,
         w_conv, b_conv, w_a, w_b, w_o, sink_row, bias, wr_t, br_col, w_up, w_down, upper, lower):
    bsz, seq, _ = x.shape
    nt = seq // SEQ_TILE
    clen = k_ctx.shape[1]
    nch = n_main = bsz * nt
    assert N_EXPERTS % n_main == 0, "each grid step converts an equal share of the experts"
    e_step = N_EXPERTS // n_main

    def tile(s):
        return divmod(jnp.clip(s, 0, n_main - 1), nt)

    def per_tile(shape, index, shift=0):
        return pl.BlockSpec(shape, lambda s: index(*tile(s + shift)))

    def const_spec(shape):
        return pl.BlockSpec(shape, lambda s: tuple(0 for _ in shape),
                            pipeline_mode=pl.Buffered(1))

    def x_spec(shift=0):
        return per_tile((1, SEQ_TILE, D_MODEL), lambda b, t: (b, t, 0), shift)

    def mod_spec(j, shift=0):
        return per_tile((1, 1, D_MODEL), lambda b, t: (b, 0, j), shift)

    def rope_spec(shift=0):
        return per_tile((SEQ_TILE, LANES), lambda b, t: (t, 0), shift)

    def expert_spec(w):
        return pl.BlockSpec((e_step,) + w.shape[1:], lambda s: (jnp.minimum(s, n_main - 1), 0, 0))

    def chunk_spec(shape):
        return pl.BlockSpec(shape, lambda s: (jnp.maximum(s - 1, 0),) + (0,) * (len(shape) - 1))

    in_specs = [
        x_spec(), const_spec((1, D_MODEL)), const_spec((1, D_MODEL)),
        mod_spec(MOD_SHIFT1), mod_spec(MOD_SCALE1), mod_spec(MOD_GATE1), mod_spec(MOD_SHIFT2),
        mod_spec(MOD_SCALE2),
        const_spec(w_b_cols.shape), rope_spec(), rope_spec(),
        x_spec(1), mod_spec(MOD_SHIFT1, 1), mod_spec(MOD_SCALE1, 1), rope_spec(1), rope_spec(1),
        const_spec(w_a_cols.shape),
        per_tile((1, clen, KV_W), lambda b, t: (b, 0, 0)),
        per_tile((1, KV_W, clen), lambda b, t: (b, 0, 0)),
        const_spec(w_conv.shape), const_spec(b_conv.shape),
        const_spec(w_a.shape), const_spec(w_b.shape), const_spec(w_o.shape),
        const_spec(sink_row.shape), const_spec(bias.shape),
        const_spec(wr_t.shape), const_spec(br_col.shape),
        expert_spec(w_up), expert_spec(w_down),
        const_spec(upper.shape), const_spec(lower.shape),
    ]
    out_specs = [
        x_spec(), chunk_spec((CHUNK_ROWS, XL_COLS)), chunk_spec((CHUNK, LANES)),
        chunk_spec((1, N_EXPERTS, LANES)), expert_spec(w_up), expert_spec(w_down),
    ]
    out_shape = [
        jax.ShapeDtypeStruct((bsz, seq, D_MODEL), F32),
        jax.ShapeDtypeStruct((nch * CHUNK_ROWS, XL_COLS), BF16),
        jax.ShapeDtypeStruct((nch * CHUNK, LANES), F32),
        jax.ShapeDtypeStruct((nch, N_EXPERTS, LANES), I32),
        jax.ShapeDtypeStruct(w_up.shape, BF16),
        jax.ShapeDtypeStruct(w_down.shape, BF16),
    ]
    return pl.pallas_call(
        functools.partial(_mix_kernel, n_seq_tiles=nt, batch=bsz),
        grid=(n_main + 1,),
        in_specs=in_specs,
        out_specs=out_specs,
        out_shape=out_shape,
        scratch_shapes=[pltpu.VMEM((CHUNK, D_MODEL), BF16), pltpu.VMEM((ROUTE_ROWS, CHUNK), F32),
                        pltpu.VMEM((2, SEQ_TILE, CONV_W), F32), pltpu.VMEM((8, CONV_W), F32),
                        pltpu.VMEM((RING, SEQ_TILE, KV_W), BF16),
                        pltpu.VMEM((RING, KV_W, SEQ_TILE), BF16)],
        compiler_params=_cparams(("arbitrary",)),
        name="token_mix",
    )(x, norm1_g, norm2_g, mod3, mod3, mod3, mod3, mod3, w_b_cols, cos_t, sin_t,
      x, mod3, mod3, cos_t, sin_t, w_a_cols, k_ctx, vt_ctx,
      w_conv, b_conv, w_a, w_b, w_o, sink_row, bias, wr_t, br_col, w_up, w_down, upper, lower)


def _bf16_parts(v):
    hi = v.astype(BF16).astype(F32)
    r1 = v - hi
    mid = r1.astype(BF16).astype(F32)
    lo = (r1 - mid).astype(BF16).astype(F32)
    return hi, mid, lo


def _sort_chunk(h, route, upper_ref, lower_ref, xl_ref, pos_ref, nb_ref):
    e1 = route[0:1, :].astype(I32)
    e2 = route[1:2, :].astype(I32)
    erow = lax.broadcasted_iota(I32, (N_EXPERTS, CHUNK), 0)
    hit1 = erow == e1
    hit2 = erow == e2
    onehot = jnp.where(hit1, 1.0, 0.0) + jnp.where(hit2, 1.0, 0.0)
    cum = jnp.dot(onehot.astype(BF16), upper_ref[...], preferred_element_type=F32)
    cnt = jnp.sum(onehot, axis=1, keepdims=True)
    nblk = jnp.floor((cnt + (ROW_BLOCK - 1)) * (1.0 / ROW_BLOCK))
    nblk_b = jnp.broadcast_to(nblk, (N_EXPERTS, LANES))
    seg = jnp.dot(lower_ref[...], nblk_b.astype(BF16), preferred_element_type=F32) * ROW_BLOCK
    base = seg[:, 0:1] + cum
    pos1 = jnp.sum(jnp.where(hit1, base, 0.0), axis=0, keepdims=True)
    pos2 = jnp.sum(jnp.where(hit2, base, 0.0), axis=0, keepdims=True)
    p1i = pos1.astype(I32)
    p2i = pos2.astype(I32)
    used_rows = (jnp.sum(nblk) * ROW_BLOCK).astype(I32)

    prow = lax.broadcasted_iota(I32, (LANES, CHUNK), 0)
    parts = _bf16_parts(route[2:3, :]) + _bf16_parts(route[3:4, :])
    gpart_rows = jnp.zeros((LANES, CHUNK), F32)
    for j, part in enumerate(parts):
        gpart_rows = jnp.where(prow == j, part, gpart_rows)
    h_ext = jnp.concatenate([h, gpart_rows.T.astype(BF16)], axis=1)
    glane = lax.broadcasted_iota(I32, (1, GATE_COLS), 1)
    one = jnp.ones((), BF16)
    zero = jnp.zeros((), BF16)
    def permute(row0, n_rows):
        r16 = lax.broadcasted_iota(I32, (n_rows, CHUNK), 0).astype(jnp.int16)
        q1 = (p1i - row0).astype(jnp.int16)
        q2 = (p2i - row0).astype(jnp.int16)
        sel1 = jnp.where(r16 == q1, one, zero)
        sel = jnp.where(r16 == q2, one, sel1)
        is_slot1 = jnp.max(sel1, axis=1, keepdims=True)
        rows = pl.ds(row0, n_rows)
        xg = jnp.dot(sel, h_ext, preferred_element_type=F32)
        xl_ref[rows, 0:D_MODEL] = xg[:, :D_MODEL].astype(BF16)
        g6 = xg[:, D_MODEL:]
        g3 = jnp.where(is_slot1.astype(F32) > 0.0, g6, pltpu.roll(g6, GATE_COLS - 3, axis=1))
        xl_ref[rows, D_MODEL:XL_COLS] = jnp.where(glane < 3, g3, 0.0).astype(BF16)

    pos_rows = jnp.where(prow == 0, pos1, jnp.where(prow == 1, pos2, 0.0))
    pos_ref[...] = pos_rows.T
    nb_ref[0] = nblk_b.astype(I32)

    def permute_tail():
        tail_rows = CHUNK_ROWS - COMMON_ROWS

        @pl.when(COMMON_ROWS < used_rows)
        def _():
            permute(COMMON_ROWS, tail_rows)

        @pl.when(COMMON_ROWS >= used_rows)
        def _():
            xl_ref[pl.ds(COMMON_ROWS, tail_rows), :] = jnp.zeros((tail_rows, XL_COLS), BF16)

    return permute, permute_tail


def _max_tiles(nch):
    max_blocks = nch * (2 * CHUNK + N_EXPERTS * (ROW_BLOCK - 1)) // ROW_BLOCK
    return max_blocks // TILE_BLOCKS + N_EXPERTS


def _masked_prefix(le, values):
    delta = values - jnp.concatenate([jnp.zeros((1,), values.dtype), values[:-1]])
    return jnp.sum(jnp.where(le, delta[None, :], 0), axis=1)


def _tile_plan(nb, nch):
    n_tiles = _max_tiles(nch)
    nbt = nb.T
    nbe = jnp.sum(nbt, axis=1)
    nte = (nbe + TILE_BLOCKS - 1) // TILE_BLOCKS
    tile_end = jnp.cumsum(nte)
    tile_start = tile_end - nte
    n_act = tile_end[-1]
    tiles = jnp.arange(n_tiles, dtype=I32)
    te = jnp.sum((tile_end[None, :] <= tiles[:, None]).astype(I32), axis=1)
    e_ar = jnp.arange(N_EXPERTS, dtype=I32)
    active = nte > 0
    te = jnp.where(tiles < n_act, te, jnp.max(jnp.where(active, e_ar, 0)))
    first = jnp.logical_and(te != jnp.concatenate([jnp.full((1,), -1, I32), te[:-1]]),
                            tiles < n_act).astype(I32)
    rank = jnp.cumsum(active.astype(I32)) - 1
    k_ar = jnp.arange(N_EXPERTS + W_SLOTS, dtype=I32)
    eseq = jnp.sum(jnp.where(jnp.logical_and(active[None, :], rank[None, :] == k_ar[:, None]),
                             e_ar[None, :], 0), axis=1)
    n_exp = jnp.sum(active.astype(I32))

    cb_excl = jnp.cumsum(nbt, axis=1) - nbt
    gs = TILE_BLOCKS * tile_start[:, None] + cb_excl
    seg_blk = jnp.cumsum(nb, axis=1) - nb
    base_blk = jnp.arange(nch, dtype=I32)[None, :] * CHUNK_BLOCKS + seg_blk.T
    gs_f = gs.reshape(-1)
    slots = jnp.arange(n_tiles * TILE_BLOCKS, dtype=I32)
    le = gs_f[None, :] <= slots[:, None]
    blk = slots + _masked_prefix(le, (base_blk - gs).reshape(-1))
    valid = slots < _masked_prefix(le, (gs + nbt).reshape(-1))
    gblk = jnp.concatenate([jnp.where(valid, blk, 0).astype(I32),
                            jnp.zeros(((X_SLOTS - 1) * TILE_BLOCKS,), I32)])

    m = jnp.arange(CHUNK_BLOCKS, dtype=I32)
    le_c = seg_blk[:, None, :] <= m[None, :, None]
    shift = (gs.T - seg_blk)
    delta = shift - jnp.concatenate([jnp.zeros((nch, 1), I32), shift[:, :-1]], axis=1)
    slot_of = m[None, :] + jnp.sum(jnp.where(le_c, delta[:, None, :], 0), axis=2)
    used_blocks = jnp.sum(nb, axis=1)
    slot_of = jnp.where(m[None, :] < used_blocks[:, None], slot_of, 0).astype(I32)
    pad = Y_RING - 1
    slot_of = jnp.concatenate([slot_of.reshape(-1), jnp.zeros((pad * CHUNK_BLOCKS,), I32)])
    used_rows = jnp.concatenate([ROW_BLOCK * used_blocks, jnp.zeros((pad,), I32)]).astype(I32)
    return ((n_act.reshape(1).astype(I32), n_exp.reshape(1), first, eseq, gblk),
            (used_rows, slot_of))


def _gmm_kernel(nact_ref, nexp_ref, first_ref, eseq_ref, gblk_ref,
                xl_hbm, wu_hbm, wd_hbm, yt_hbm,
                xbuf, ybuf, zbuf, wu_st, wd_st, in_sem, out_sem, w_sem, z_sem):
    n_act = nact_ref[0]
    n_exp = nexp_ref[0]
    n_tiles = yt_hbm.shape[0] // TILE_BLOCKS
    prefetch = X_SLOTS - 1

    def gather(tile, b):
        s = tile % X_SLOTS
        blk = gblk_ref[tile * TILE_BLOCKS + b]
        return pltpu.make_async_copy(xl_hbm.at[blk], xbuf.at[s, b], in_sem.at[s])

    def write_back(tile):
        s = tile % Y_SLOTS
        return pltpu.make_async_copy(ybuf.at[s], yt_hbm.at[pl.ds(tile * TILE_BLOCKS, TILE_BLOCKS)],
                                     out_sem.at[s])

    def zero_fill(tile):
        return pltpu.make_async_copy(zbuf, yt_hbm.at[pl.ds(tile * TILE_BLOCKS, TILE_BLOCKS)], z_sem)

    def weights(q):
        e = eseq_ref[q]
        s = q % W_SLOTS
        return (pltpu.make_async_copy(wu_hbm.at[e], wu_st.at[s], w_sem.at[0, s]),
                pltpu.make_async_copy(wd_hbm.at[e], wd_st.at[s], w_sem.at[1, s]))

    def tile_blocks(fn):
        for b in range(TILE_BLOCKS):
            fn(b)

    for q0 in range(W_SLOTS - 1):
        @pl.when(q0 < n_exp)
        def _(q0=q0):
            for cp in weights(q0):
                cp.start()
    for t0 in range(prefetch):
        tile_blocks(lambda b, t0=t0: gather(t0, b).start())

    def body(i, q):
        tile_blocks(lambda b: gather(i, b).wait())

        @pl.when(i >= Y_SLOTS)
        def _():
            write_back(i - Y_SLOTS).wait()

        is_first = first_ref[i] == 1

        @pl.when(is_first)
        def _():
            for cp in weights(q):
                cp.wait()

            @pl.when(q + W_SLOTS - 1 < n_exp)
            def _():
                for cp in weights(q + W_SLOTS - 1):
                    cp.start()

        q = q + is_first.astype(I32)
        ws = (q - 1) % W_SLOTS
        xs = i % X_SLOTS
        ys = i % Y_SLOTS

        x = xbuf[xs].reshape(TILE_ROWS, XL_COLS)
        gate = jnp.sum(x[:, D_MODEL:].astype(F32), axis=1, keepdims=True)
        au = jnp.dot(x[:, :D_MODEL], wu_st[ws], preferred_element_type=F32)
        tile_blocks(lambda b: gather(i + prefetch, b).start())
        a = au[:, :EXPERT_FF]
        act = (a * _sigmoid(a)) * au[:, EXPERT_FF:]
        y = jnp.dot(act.astype(BF16), wd_st[ws], preferred_element_type=F32)
        ybuf[ys] = (gate * y).astype(BF16).reshape(TILE_BLOCKS, ROW_BLOCK, D_MODEL)
        write_back(i).start()

        @pl.when(n_act + i < n_tiles)
        def _():
            zero_fill(n_act + i).start()
        return q

    zbuf[...] = jnp.zeros_like(zbuf)
    lax.fori_loop(0, n_act, body, jnp.int32(0))

    def fill_rest(t, carry):
        zero_fill(t).start()
        return carry

    lax.fori_loop(jnp.minimum(2 * n_act, n_tiles), n_tiles, fill_rest, 0)

    def drain(t, carry):
        zero_fill(t).wait()
        return carry

    lax.fori_loop(n_act, n_tiles, drain, 0)
    for k in range(prefetch):
        tile_blocks(lambda b, k=k: gather(n_act + k, b).wait())
    for k in range(Y_SLOTS):
        @pl.when(n_act - 1 - k >= 0)
        def _(k=k):
            write_back(n_act - 1 - k).wait()


def _grouped_mlp(plan, xl, w_up, w_down, n_tiles):
    grid_spec = pltpu.PrefetchScalarGridSpec(
        num_scalar_prefetch=len(plan),
        grid=(1,),
        in_specs=[pl.BlockSpec(memory_space=pl.ANY)] * 3,
        out_specs=pl.BlockSpec(memory_space=pl.ANY),
        scratch_shapes=[pltpu.VMEM((X_SLOTS, TILE_BLOCKS, ROW_BLOCK, XL_COLS), BF16),
                        pltpu.VMEM((Y_SLOTS, TILE_BLOCKS, ROW_BLOCK, D_MODEL), BF16),
                        pltpu.VMEM((TILE_BLOCKS, ROW_BLOCK, D_MODEL), BF16),
                        pltpu.VMEM((W_SLOTS, D_MODEL, 2 * EXPERT_FF), BF16),
                        pltpu.VMEM((W_SLOTS, EXPERT_FF, D_MODEL), BF16),
                        pltpu.SemaphoreType.DMA((X_SLOTS,)),
                        pltpu.SemaphoreType.DMA((Y_SLOTS,)),
                        pltpu.SemaphoreType.DMA((2, W_SLOTS)),
                        pltpu.SemaphoreType.DMA(())])
    return pl.pallas_call(
        _gmm_kernel,
        grid_spec=grid_spec,
        out_shape=jax.ShapeDtypeStruct((n_tiles * TILE_BLOCKS, ROW_BLOCK, D_MODEL), BF16),
        compiler_params=_cparams(("arbitrary",)),
        name="moe_experts",
    )(*plan, xl, w_up, w_down)


def _combine_kernel(used_ref, slot_ref, yt_hbm, pos_ref, x1_ref, gt2_ref, fg_ref, o_ref, ybuf, sem):
    c = pl.program_id(0)
    n_chunks = pl.num_programs(0)
    used_rows = used_ref[c]

    def chunk_blocks(chunk, op):
        s = chunk % Y_RING

        def copy(b):
            slot = slot_ref[chunk * CHUNK_BLOCKS + b]
            return pltpu.make_async_copy(yt_hbm.at[slot], ybuf.at[s, b], sem.at[s])

        for b in range(COMMON_BLOCKS):
            op(copy(b))

        @pl.when(used_ref[chunk] > COMMON_ROWS)
        def _():
            for b in range(COMMON_BLOCKS, CHUNK_BLOCKS):
                op(copy(b))

    @pl.when(c == 0)
    def _():
        for k in range(Y_RING - 1):
            chunk_blocks(k, lambda cp: cp.start())

    chunk_blocks(c + Y_RING - 1, lambda cp: cp.start())
    chunk_blocks(c, lambda cp: cp.wait())

    def body(k_rows):
        pos = pos_ref[...]
        p1 = pos[:, 0:1].astype(I32)
        p2 = pos[:, 1:2].astype(I32)
        r = lax.broadcasted_iota(I32, (CHUNK, k_rows), 1)
        sel = jnp.where(r == p1, 1.0, jnp.where(r == p2, 1.0, 0.0)).astype(BF16)
        yl = ybuf[c % Y_RING, 0:k_rows // ROW_BLOCK].reshape(k_rows, D_MODEL)
        y = jnp.dot(sel, yl, preferred_element_type=F32)
        x2 = x1_ref[...] + gt2_ref[0] * y
        inv = lax.rsqrt(jnp.mean(x2 * x2, axis=-1, keepdims=True) + NORM_EPS)
        o_ref[...] = (x2 * inv) * fg_ref[...]

    pl.when(used_rows <= COMMON_ROWS)(lambda: body(COMMON_ROWS))
    pl.when(used_rows > COMMON_ROWS)(lambda: body(CHUNK_ROWS))

    @pl.when(c == n_chunks - 1)
    def _():
        for k in range(1, Y_RING):
            chunk_blocks(c + k, lambda cp: cp.wait())


def _combine(used_rows, slot_of, yt, pos_c, x1, mod3, final_g, seq):
    n = x1.shape[0]
    per_seq = seq // CHUNK
    grid_spec = pltpu.PrefetchScalarGridSpec(
        num_scalar_prefetch=2,
        grid=(n // CHUNK,),
        in_specs=[pl.BlockSpec(memory_space=pl.ANY),
                  pl.BlockSpec((CHUNK, LANES), lambda c, u, s: (c, 0)),
                  pl.BlockSpec((CHUNK, D_MODEL), lambda c, u, s: (c, 0)),
                  pl.BlockSpec((1, 1, D_MODEL), lambda c, u, s: (c // per_seq, 0, MOD_GATE2)),
                  pl.BlockSpec((1, D_MODEL), lambda c, u, s: (0, 0))],
        out_specs=pl.BlockSpec((CHUNK, D_MODEL), lambda c, u, s: (c, 0)),
        scratch_shapes=[pltpu.VMEM((Y_RING, CHUNK_BLOCKS, ROW_BLOCK, D_MODEL), BF16),
                        pltpu.SemaphoreType.DMA((Y_RING,))])
    return pl.pallas_call(
        _combine_kernel,
        grid_spec=grid_spec,
        out_shape=jax.ShapeDtypeStruct((n, D_MODEL), F32),
        compiler_params=_cparams(("arbitrary",)),
        name="moe_combine",
    )(used_rows, slot_of, yt, pos_c, x1, mod3, final_g)


def _rope_tables(seq):
    n_freq = HEAD_DIM // 4
    inv_freq = ROPE_BASE ** (-jnp.arange(n_freq, dtype=F32) / n_freq)
    rows = seq // GRID_W
    row = jnp.repeat(jnp.arange(rows, dtype=F32), GRID_W)
    col = jnp.tile(jnp.arange(GRID_W, dtype=F32), rows)
    ang = jnp.concatenate([row[:, None] * inv_freq, col[:, None] * inv_freq], axis=-1)
    cos, sin = jnp.cos(ang), jnp.sin(ang)
    return jnp.tile(cos, (1, 4)), jnp.concatenate([-sin, sin, -sin, sin], axis=1)


def _window_bias():
    key = np.arange(3 * BLOCK)[:, None]
    qry = np.arange(BLOCK)[None, :]
    valid = (key - qry >= 0) & (key - qry <= 2 * BLOCK)
    return jnp.asarray(np.where(valid, 0.0, NEG_INF), F32)


def kernel(x, c, ctx, c_ctx, w_ada, b_ada, norm1_g, w_in, w_conv, b_conv, w_a, w_b, sink, w_o,
           norm2_g, w_group, b_group, w_router, b_router, w_up, w_down, final_g):
    bsz, seq, _ = x.shape
    assert w_ada.shape[0] == 1 and seq % SEQ_TILE == 0

    cc = jnp.zeros((16, D_MODEL), F32).at[:bsz].set(c).at[bsz].set(c_ctx)
    mod = _modulation(cc, w_ada[0], b_ada[0][None, :])
    mod3 = mod[:bsz].reshape(bsz, 1, N_MOD * D_MODEL)
    csh1 = mod[bsz:bsz + 1, 0:D_MODEL]
    csc1 = mod[bsz:bsz + 1, D_MODEL:2 * D_MODEL]

    w = w_in[0]
    n1g = norm1_g[0][None, :]
    n2g = norm2_g[0][None, :]
    w_kv = w[:, OFF_K:OFF_GA].astype(BF16)
    w_a_cols = jnp.concatenate([w[:, OFF_CG:OFF_Q], w[:, OFF_K:OFF_GA]], axis=1).astype(BF16)
    w_b_cols = jnp.concatenate([w[:, OFF_B:OFF_CG], w[:, OFF_Q:OFF_K], w[:, OFF_GA:]],
                               axis=1).astype(BF16)
    cos_t, sin_t = _rope_tables(seq)

    k_ctx, vt_ctx = _context_kv(ctx, n1g, csh1, csc1, w_kv)
    sink_row = jnp.repeat(sink[0].astype(F32), BLOCK)[None, :]
    pad_g = jnp.zeros((D_MODEL, GROUP_ROWS - N_GROUPS), F32)
    pad_e = jnp.zeros((D_MODEL, ROUTER_ROWS - GROUP_ROWS - N_EXPERTS), F32)
    wr_t = jnp.concatenate([w_group[0], pad_g, w_router[0], pad_e], axis=1).T.astype(BF16)
    br_col = jnp.concatenate([b_group[0], pad_g[0], b_router[0], pad_e[0]])[:, None]

    upper = jnp.asarray(np.triu(np.ones((CHUNK, CHUNK), np.float32), 1), BF16)
    lower = jnp.asarray(np.tril(np.ones((N_EXPERTS, N_EXPERTS), np.float32), -1), BF16)
    x1, xl, pos_c, nb, w_up_bf, w_down_bf = _mix(
        x, n1g, n2g, mod3, w_a_cols, w_b_cols, cos_t, sin_t, k_ctx, vt_ctx,
        w_conv[0], b_conv[0][None, :], w_a[0].astype(BF16), w_b[0].astype(BF16),
        w_o[0].astype(BF16), sink_row, _window_bias(), wr_t, br_col, w_up[0], w_down[0],
        upper, lower)

    n = bsz * seq
    nch = n // CHUNK
    nb = nb[:, :, 0]
    expert_plan, combine_plan = _tile_plan(nb, nch)
    yt = _grouped_mlp(expert_plan, xl.reshape(-1, ROW_BLOCK, XL_COLS), w_up_bf, w_down_bf,
                      _max_tiles(nch))
    out = _combine(*combine_plan, yt, pos_c, x1.reshape(n, D_MODEL), mod3, final_g[None, :], seq)
    return out.reshape(bsz, seq, D_MODEL)
```

```python
import functools

import numpy as np
import jax
import jax.numpy as jnp
from jax import lax
from jax.experimental import pallas as pl
from jax.experimental.pallas import tpu as pltpu

F32 = jnp.float32
BF16 = jnp.bfloat16
I32 = jnp.int32

D_MODEL = 1024
GRID_W = 64
CONV_W = 512
N_HEADS = 8
N_KV_HEADS = 2
HEAD_DIM = 64
ATT_W = N_HEADS * HEAD_DIM
KV_W = N_KV_HEADS * HEAD_DIM
BLOCK = 128
ROPE_BASE = 10000.0
N_GROUPS = 4
EXPERTS_PER_GROUP = 8
N_EXPERTS = N_GROUPS * EXPERTS_PER_GROUP
EXPERT_FF = 256
N_MOD = 6
MOD_SHIFT1, MOD_SCALE1, MOD_GATE1, MOD_SHIFT2, MOD_SCALE2, MOD_GATE2 = range(N_MOD)
NORM_EPS = 1e-6
NEG_INF = -1e30
LOG2E = 1.4426950408889634
ONES_ROWS = 16

OFF_B, OFF_CG, OFF_XIN, OFF_Q, OFF_K, OFF_V, OFF_GA, OFF_GB = (
    0, 512, 1024, 1536, 2048, 2176, 2304, 3328)
IN_COLS = 4352

LANES = 128
SEQ_TILE = 512
BLOCKS_PER_TILE = SEQ_TILE // BLOCK
ROUTE_ROWS = 8
GROUP_ROWS = 8
ROUTER_ROWS = 48
VMEM_LIMIT = 62 * 1024 * 1024

CHUNK = SEQ_TILE
ROW_BLOCK = 16
CHUNK_ROWS = -(-(2 * CHUNK + N_EXPERTS * (ROW_BLOCK - 1)) // 256) * 256
CHUNK_BLOCKS = CHUNK_ROWS // ROW_BLOCK
GATE_COLS = LANES
XL_COLS = D_MODEL + GATE_COLS
COMMON_ROWS = 2 * CHUNK + 256
COMMON_BLOCKS = COMMON_ROWS // ROW_BLOCK
TILE_BLOCKS = 32
TILE_ROWS = TILE_BLOCKS * ROW_BLOCK
X_SLOTS = 6
Y_SLOTS = 4
W_SLOTS = 3
Y_RING = 3
RING = 3


def _cparams(sem):
    return pltpu.CompilerParams(dimension_semantics=sem, vmem_limit_bytes=VMEM_LIMIT)


def _rms_mod(x, g, shift, scale):
    inv = lax.rsqrt(jnp.mean(x * x, axis=-1, keepdims=True) + NORM_EPS)
    return (x * inv) * (g * (1.0 + scale)) + shift


def _mod_kernel(c_ref, w_ref, b_ref, o_ref):
    c = c_ref[...]
    a = (c * jax.nn.sigmoid(c)).astype(BF16)
    o_ref[...] = jnp.dot(a, w_ref[...].astype(BF16), preferred_element_type=F32) + b_ref[...]


def _modulation(cc, w_ada, b_ada):
    rows = cc.shape[0]
    cols = w_ada.shape[1]
    tile = 1024
    return pl.pallas_call(
        _mod_kernel,
        grid=(cols // tile,),
        in_specs=[pl.BlockSpec((rows, D_MODEL), lambda j: (0, 0)),
                  pl.BlockSpec((D_MODEL, tile), lambda j: (0, j)),
                  pl.BlockSpec((1, tile), lambda j: (0, j))],
        out_specs=pl.BlockSpec((rows, tile), lambda j: (0, j)),
        out_shape=jax.ShapeDtypeStruct((rows, cols), F32),
        compiler_params=_cparams(("arbitrary",)),
        name="adaln_mod",
    )(cc, w_ada, b_ada)


def _ctx_kernel(x_ref, g_ref, sh_ref, sc_ref, w_ref, k_ref, vt_ref):
    h = _rms_mod(x_ref[0], g_ref[...], sh_ref[...], sc_ref[...]).astype(BF16)
    kv = jnp.dot(h, w_ref[...], preferred_element_type=F32)
    k_ref[0] = kv[:, :KV_W].astype(BF16)
    vt_ref[0] = kv[:, KV_W:].T.astype(BF16)


def _context_kv(ctx, norm_g, csh, csc, w_kv):
    bsz, clen, _ = ctx.shape
    return pl.pallas_call(
        _ctx_kernel,
        grid=(bsz,),
        in_specs=[pl.BlockSpec((1, clen, D_MODEL), lambda b: (b, 0, 0)),
                  pl.BlockSpec((1, D_MODEL), lambda b: (0, 0)),
                  pl.BlockSpec((1, D_MODEL), lambda b: (0, 0)),
                  pl.BlockSpec((1, D_MODEL), lambda b: (0, 0)),
                  pl.BlockSpec((D_MODEL, 2 * KV_W), lambda b: (0, 0))],
        out_specs=[pl.BlockSpec((1, clen, KV_W), lambda b: (b, 0, 0)),
                   pl.BlockSpec((1, KV_W, clen), lambda b: (b, 0, 0))],
        out_shape=[jax.ShapeDtypeStruct((bsz, clen, KV_W), BF16),
                   jax.ShapeDtypeStruct((bsz, KV_W, clen), BF16)],
        compiler_params=_cparams(("arbitrary",)),
        name="context_kv",
    )(ctx, norm_g, csh, csc, w_kv)


def _rope(t, cos, sin_signed):
    lane = lax.broadcasted_iota(I32, (1, LANES), 1)
    first_half = (lane % HEAD_DIM) < (HEAD_DIM // 2)
    outs = []
    for j in range(t.shape[1] // LANES):
        tj = t[:, j * LANES:(j + 1) * LANES]
        partner = jnp.where(first_half,
                            pltpu.roll(tj, LANES - HEAD_DIM // 2, axis=1),
                            pltpu.roll(tj, HEAD_DIM // 2, axis=1))
        outs.append(tj * cos + partner * sin_signed)
    return outs[0] if len(outs) == 1 else jnp.concatenate(outs, axis=1)


def _sigmoid(x):
    return 1.0 / (1.0 + jnp.exp2(x * (-LOG2E)))


def _attn_scores(q_t, g, k_win, k_ctx, bias_p4, bias_n4):
    zeros = jnp.zeros((HEAD_DIM, 4 * BLOCK), BF16)
    qg = jnp.concatenate([q_t[(4 * g + h) * HEAD_DIM:(4 * g + h + 1) * HEAD_DIM, :]
                          for h in range(4)], axis=1)
    rhs = jnp.concatenate([qg, zeros] if g == 0 else [zeros, qg], axis=0)
    s_win = jnp.dot(k_win, rhs, preferred_element_type=F32)
    s_ctx = jnp.dot(k_ctx, rhs, preferred_element_type=F32)
    return (s_win[0:BLOCK] + bias_p4, s_win[BLOCK:2 * BLOCK], s_win[2 * BLOCK:] + bias_n4, s_ctx)


def _attn_probs(scores, sink):
    m = sink
    for s in scores:
        m = jnp.maximum(m, jnp.max(s, axis=0, keepdims=True))
    p_win = jnp.concatenate([jnp.exp2(s - m).astype(BF16) for s in scores[:3]], axis=0)
    p_ctx = jnp.exp2(scores[3] - m).astype(BF16)
    return p_win, p_ctx, m


def _attn_values(probs, sink, vt_win_g, vt_ctx_g):
    p_win, p_ctx, m = probs

    def with_ones(vt):
        r = lax.broadcasted_iota(I32, (ONES_ROWS, vt.shape[1]), 0)
        return jnp.concatenate([vt, jnp.where(r == 0, 1.0, 0.0).astype(BF16)], axis=0)

    o_ext = (jnp.dot(with_ones(vt_win_g), p_win, preferred_element_type=F32)
             + jnp.dot(with_ones(vt_ctx_g), p_ctx, preferred_element_type=F32))
    denom = o_ext[HEAD_DIM:HEAD_DIM + 1, :] + jnp.exp2(sink - m)
    return o_ext[:HEAD_DIM, :] / denom


def _route(logits_t):
    t = logits_t.shape[1]
    grow = lax.broadcasted_iota(I32, (GROUP_ROWS, t), 0)
    gl = jnp.where(grow < N_GROUPS, logits_t[0:GROUP_ROWS, :], NEG_INF)
    gm = jnp.max(gl, axis=0, keepdims=True)
    p_g = 1.0 / jnp.sum(jnp.exp(gl - gm), axis=0, keepdims=True)
    g_idx = jnp.min(jnp.where(gl == gm, grow, N_GROUPS), axis=0, keepdims=True)

    erow = lax.broadcasted_iota(I32, (N_EXPERTS, t), 0)
    el = logits_t[GROUP_ROWS:GROUP_ROWS + N_EXPERTS, :]
    sel = (erow // EXPERTS_PER_GROUP) == g_idx
    em = jnp.where(sel, el, NEG_INF)
    m1 = jnp.max(em, axis=0, keepdims=True)
    i1 = jnp.min(jnp.where(em == m1, erow, N_EXPERTS), axis=0, keepdims=True)
    em2 = jnp.where(erow == i1, NEG_INF, em)
    m2 = jnp.max(em2, axis=0, keepdims=True)
    i2 = jnp.min(jnp.where(em2 == m2, erow, N_EXPERTS), axis=0, keepdims=True)
    z = jnp.sum(jnp.where(sel, jnp.exp(el - m1), 0.0), axis=0, keepdims=True)
    p1 = 1.0 / z
    p2 = jnp.exp(m2 - m1) / z
    gate1 = p_g * p1 / (p1 + p2)
    gate2 = p_g * p2 / (p1 + p2)
    pad = jnp.zeros((ROUTE_ROWS - 4, t), F32)
    return jnp.concatenate([i1.astype(F32), i2.astype(F32), gate1, gate2, pad], axis=0)


def _project_a(x_ref, g_ref, sh_ref, sc_ref, w_ref, cos_ref, sin_ref, slot, uc_slot,
               uc_ring, k_ring, vt_ring):
    h = _rms_mod(x_ref[0], g_ref[...], sh_ref[0], sc_ref[0]).astype(BF16)
    u = jnp.dot(h, w_ref[...], preferred_element_type=F32)
    uc_ring[uc_slot] = u[:, :CONV_W] * u[:, CONV_W:2 * CONV_W]
    k = u[:, 2 * CONV_W:2 * CONV_W + KV_W]
    k_ring[slot] = _rope(k, cos_ref[...], sin_ref[...]).astype(BF16)
    vt_ring[slot] = u[:, 2 * CONV_W + KV_W:].T.astype(BF16)


def _mix_step(x_ref, g1n_ref, g2n_ref, sh1_ref, sc1_ref, gt1_ref, sh2_ref, sc2_ref,
              wb_ref, cos_ref, sin_ref,
              xn_ref, shn_ref, scn_ref, cosn_ref, sinn_ref, wac_ref,
              kc_ref, vtc_ref, wconv_ref, bconv_ref, wa_ref, wbb_ref, wo_ref,
              sink_ref, bias_ref, wr_ref, br_ref, wu_ref, wd_ref, upper_ref, lower_ref,
              x1_ref, xl_ref, pos_ref, nb_ref, wub_ref, wdb_ref,
              h2_scr, route_scr, uc_ring, uc_last, k_ring, vt_ring,
              *, n_seq_tiles):
    step = pl.program_id(0)
    t_idx = step % n_seq_tiles
    is_first = t_idx == 0
    is_last = t_idx == n_seq_tiles - 1
    own, prv, nxt = step % RING, (step + RING - 1) % RING, (step + 1) % RING

    permute, permute_tail = _sort_chunk(h2_scr[...], route_scr[...], upper_ref, lower_ref,
                                        xl_ref, pos_ref, nb_ref)

    @pl.when(step == 0)
    def _():
        uc_ring[...] = jnp.zeros_like(uc_ring)
        uc_last[...] = jnp.zeros_like(uc_last)
        k_ring[...] = jnp.zeros_like(k_ring)
        vt_ring[...] = jnp.zeros_like(vt_ring)
        _project_a(x_ref, g1n_ref, sh1_ref, sc1_ref, wac_ref, cos_ref, sin_ref, 0, 0,
                   uc_ring, k_ring, vt_ring)

    x = x_ref[0]
    hb = _rms_mod(x, g1n_ref[...], sh1_ref[0], sc1_ref[0]).astype(BF16)
    bq = jnp.dot(hb, wb_ref[:, 0:2 * CONV_W], preferred_element_type=F32)
    q = _rope(bq[:, CONV_W:], cos_ref[...], sin_ref[...]) * (HEAD_DIM ** -0.5 * LOG2E)

    def block_ring(ring, slot_, blk, axis):
        lo = blk * BLOCK
        return ring[slot_, lo:lo + BLOCK, :] if axis == 0 else ring[slot_, :, lo:lo + BLOCK]

    def window(ring, jb, axis):
        parts = []
        for blk in (jb - 1, jb, jb + 1):
            if blk < 0:
                parts.append(block_ring(ring, prv, BLOCKS_PER_TILE - 1, axis))
            elif blk >= BLOCKS_PER_TILE:
                parts.append(block_ring(ring, nxt, 0, axis))
            else:
                parts.append(block_ring(ring, own, blk, axis))
        return jnp.concatenate(parts, axis=axis)

    k_ctx = kc_ref[0]
    vt_ctx = vtc_ref[0]
    bias_prev = bias_ref[0:BLOCK, :]
    bias_next = bias_ref[2 * BLOCK:3 * BLOCK, :]
    sink_row = sink_ref[...] * LOG2E
    n_units = BLOCKS_PER_TILE * N_KV_HEADS
    gate_cols = 2 * D_MODEL // n_units
    gate_chunks = []

    def gate_chunk(u):
        c0 = 2 * CONV_W + u * gate_cols
        gate_chunks.append(jnp.dot(hb, wb_ref[:, c0:c0 + gate_cols], preferred_element_type=F32))

    q_ts, biases = [], []
    for jb in range(BLOCKS_PER_TILE):
        bias_p, bias_n = bias_prev, bias_next
        if jb == 0:
            bias_p = bias_prev + jnp.where(is_first, NEG_INF, 0.0)
        if jb == BLOCKS_PER_TILE - 1:
            bias_n = bias_next + jnp.where(is_last, NEG_INF, 0.0)
        biases.append((jnp.concatenate([bias_p] * 4, axis=1), jnp.concatenate([bias_n] * 4, axis=1)))
        q_ts.append(q[jb * BLOCK:(jb + 1) * BLOCK, :].T.astype(BF16))

    scores, probs, outs = {}, {}, {}
    for t in range(n_units + 2):
        if t == 3:
            wub_ref[...] = wu_ref[...].astype(BF16)
        if t == 5:
            wdb_ref[...] = wd_ref[...].astype(BF16)
        if t == 0:
            permute(0, COMMON_ROWS // 2)
        if t == 4:
            permute(COMMON_ROWS // 2, COMMON_ROWS // 2)
        if t == 2:
            _project_a(xn_ref, g1n_ref, shn_ref, scn_ref, wac_ref, cosn_ref, sinn_ref, nxt,
                       (step + 1) % 2, uc_ring, k_ring, vt_ring)
        if t < n_units:
            jb, g = divmod(t, N_KV_HEADS)
            scores[t] = _attn_scores(q_ts[jb], g, window(k_ring, jb, 0), k_ctx, *biases[jb])
            gate_chunk(t)
        if 0 <= t - 1 < n_units:
            g = (t - 1) % N_KV_HEADS
            probs[t - 1] = _attn_probs(scores.pop(t - 1),
                                       sink_row[:, g * 4 * BLOCK:(g + 1) * 4 * BLOCK])
        if 0 <= t - 2 < n_units:
            jb, g = divmod(t - 2, N_KV_HEADS)
            outs[t - 2] = _attn_values(
                probs.pop(t - 2), sink_row[:, g * 4 * BLOCK:(g + 1) * 4 * BLOCK],
                window(vt_ring, jb, 1)[g * HEAD_DIM:(g + 1) * HEAD_DIM, :],
                vt_ctx[g * HEAD_DIM:(g + 1) * HEAD_DIM, :])
    o_blocks = []
    for jb in range(BLOCKS_PER_TILE):
        o_rows = [outs[jb * N_KV_HEADS + g][:, h * BLOCK:(h + 1) * BLOCK]
                  for g in range(N_KV_HEADS) for h in range(4)]
        o_blocks.append(jnp.concatenate(o_rows, axis=0).T)
    gates = jnp.concatenate(gate_chunks, axis=1)

    uc = uc_ring[step % 2]
    row = lax.broadcasted_iota(I32, (SEQ_TILE, 1), 0)
    prev_row = jnp.where(is_first, 0.0, uc_last[0:1, :])
    next_row = jnp.where(is_last, 0.0, uc_ring[(step + 1) % 2, 0:1, :])
    uc_last[0:1, :] = uc[SEQ_TILE - 1:SEQ_TILE, :]
    up = jnp.where(row == 0, prev_row, pltpu.roll(uc, 1, axis=0))
    dn = jnp.where(row == SEQ_TILE - 1, next_row, pltpu.roll(uc, SEQ_TILE - 1, axis=0))
    wconv = wconv_ref[...]
    y = bconv_ref[...] + (up * wconv[0:1, :] + uc * wconv[1:2, :] + dn * wconv[2:3, :])
    ya = jnp.dot((bq[:, :CONV_W] * y).astype(BF16), wa_ref[...], preferred_element_type=F32)

    o = jnp.concatenate(o_blocks, axis=0).astype(BF16)
    yb = jnp.dot(o, wbb_ref[...], preferred_element_type=F32)
    merged = (_sigmoid(gates[:, :D_MODEL]) * ya + _sigmoid(gates[:, D_MODEL:]) * yb).astype(BF16)
    x1 = x + gt1_ref[0] * jnp.dot(merged, wo_ref[...], preferred_element_type=F32)
    x1_ref[0] = x1
    h2 = _rms_mod(x1, g2n_ref[...], sh2_ref[0], sc2_ref[0]).astype(BF16)
    logits_t = lax.dot_general(wr_ref[...], h2, (((1,), (1,)), ((), ())),
                               preferred_element_type=F32) + br_ref[...]
    permute_tail()
    h2_scr[...] = h2
    route_scr[...] = _route(logits_t)


N_MIX_OUTPUTS = 6
N_MIX_SCRATCH = 6


def _mix_kernel(*refs, n_seq_tiles, batch):
    n_tail = N_MIX_OUTPUTS + N_MIX_SCRATCH
    upper_ref, lower_ref = refs[-n_tail - 2:-n_tail]
    _, xl_ref, pos_ref, nb_ref, _, _ = refs[-n_tail:-N_MIX_SCRATCH]
    h2_scr, route_scr = refs[-N_MIX_SCRATCH:-N_MIX_SCRATCH + 2]
    step = pl.program_id(0)
    n_main = n_seq_tiles * batch

    @pl.when(step == 0)
    def _():
        h2_scr[...] = jnp.zeros_like(h2_scr)
        route_scr[...] = jnp.zeros_like(route_scr)

    @pl.when(step < n_main)
    def _():
        _mix_step(*refs, n_seq_tiles=n_seq_tiles)

    @pl.when(step == n_main)
    def _():
        permute, permute_tail = _sort_chunk(h2_scr[...], route_scr[...], upper_ref, lower_ref,
                                            xl_ref, pos_ref, nb_ref)
        permute(0, COMMON_ROWS)
        permute_tail()


def _mix(x, norm1_g, norm2_g, mod3, w_a_cols, w_b_cols, cos_t, sin_t, k_ctx, vt_ctx,
         w_conv, b_conv, w_a, w_b, w_o, sink_row, bias, wr_t, br_col, w_up, w_down, upper, lower):
    bsz, seq, _ = x.shape
    nt = seq // SEQ_TILE
    clen = k_ctx.shape[1]
    nch = n_main = bsz * nt
    assert N_EXPERTS % n_main == 0, "each grid step converts an equal share of the experts"
    e_step = N_EXPERTS // n_main

    def tile(s):
        return divmod(jnp.clip(s, 0, n_main - 1), nt)

    def per_tile(shape, index, shift=0):
        return pl.BlockSpec(shape, lambda s: index(*tile(s + shift)))

    def const_spec(shape):
        return pl.BlockSpec(shape, lambda s: tuple(0 for _ in shape),
                            pipeline_mode=pl.Buffered(1))

    def x_spec(shift=0):
        return per_tile((1, SEQ_TILE, D_MODEL), lambda b, t: (b, t, 0), shift)

    def mod_spec(j, shift=0):
        return per_tile((1, 1, D_MODEL), lambda b, t: (b, 0, j), shift)

    def rope_spec(shift=0):
        return per_tile((SEQ_TILE, LANES), lambda b, t: (t, 0), shift)

    def expert_spec(w):
        return pl.BlockSpec((e_step,) + w.shape[1:], lambda s: (jnp.minimum(s, n_main - 1), 0, 0))

    def chunk_spec(shape):
        return pl.BlockSpec(shape, lambda s: (jnp.maximum(s - 1, 0),) + (0,) * (len(shape) - 1))

    in_specs = [
        x_spec(), const_spec((1, D_MODEL)), const_spec((1, D_MODEL)),
        mod_spec(MOD_SHIFT1), mod_spec(MOD_SCALE1), mod_spec(MOD_GATE1), mod_spec(MOD_SHIFT2),
        mod_spec(MOD_SCALE2),
        const_spec(w_b_cols.shape), rope_spec(), rope_spec(),
        x_spec(1), mod_spec(MOD_SHIFT1, 1), mod_spec(MOD_SCALE1, 1), rope_spec(1), rope_spec(1),
        const_spec(w_a_cols.shape),
        per_tile((1, clen, KV_W), lambda b, t: (b, 0, 0)),
        per_tile((1, KV_W, clen), lambda b, t: (b, 0, 0)),
        const_spec(w_conv.shape), const_spec(b_conv.shape),
        const_spec(w_a.shape), const_spec(w_b.shape), const_spec(w_o.shape),
        const_spec(sink_row.shape), const_spec(bias.shape),
        const_spec(wr_t.shape), const_spec(br_col.shape),
        expert_spec(w_up), expert_spec(w_down),
        const_spec(upper.shape), const_spec(lower.shape),
    ]
    out_specs = [
        x_spec(), chunk_spec((CHUNK_ROWS, XL_COLS)), chunk_spec((CHUNK, LANES)),
        chunk_spec((1, N_EXPERTS, LANES)), expert_spec(w_up), expert_spec(w_down),
    ]
    out_shape = [
        jax.ShapeDtypeStruct((bsz, seq, D_MODEL), F32),
        jax.ShapeDtypeStruct((nch * CHUNK_ROWS, XL_COLS), BF16),
        jax.ShapeDtypeStruct((nch * CHUNK, LANES), F32),
        jax.ShapeDtypeStruct((nch, N_EXPERTS, LANES), I32),
        jax.ShapeDtypeStruct(w_up.shape, BF16),
        jax.ShapeDtypeStruct(w_down.shape, BF16),
    ]
    return pl.pallas_call(
        functools.partial(_mix_kernel, n_seq_tiles=nt, batch=bsz),
        grid=(n_main + 1,),
        in_specs=in_specs,
        out_specs=out_specs,
        out_shape=out_shape,
        scratch_shapes=[pltpu.VMEM((CHUNK, D_MODEL), BF16), pltpu.VMEM((ROUTE_ROWS, CHUNK), F32),
                        pltpu.VMEM((2, SEQ_TILE, CONV_W), F32), pltpu.VMEM((8, CONV_W), F32),
                        pltpu.VMEM((RING, SEQ_TILE, KV_W), BF16),
                        pltpu.VMEM((RING, KV_W, SEQ_TILE), BF16)],
        compiler_params=_cparams(("arbitrary",)),
        name="token_mix",
    )(x, norm1_g, norm2_g, mod3, mod3, mod3, mod3, mod3, w_b_cols, cos_t, sin_t,
      x, mod3, mod3, cos_t, sin_t, w_a_cols, k_ctx, vt_ctx,
      w_conv, b_conv, w_a, w_b, w_o, sink_row, bias, wr_t, br_col, w_up, w_down, upper, lower)


def _bf16_parts(v):
    hi = v.astype(BF16).astype(F32)
    r1 = v - hi
    mid = r1.astype(BF16).astype(F32)
    lo = (r1 - mid).astype(BF16).astype(F32)
    return hi, mid, lo


def _sort_chunk(h, route, upper_ref, lower_ref, xl_ref, pos_ref, nb_ref):
    e1 = route[0:1, :].astype(I32)
    e2 = route[1:2, :].astype(I32)
    erow = lax.broadcasted_iota(I32, (N_EXPERTS, CHUNK), 0)
    hit1 = erow == e1
    hit2 = erow == e2
    onehot = jnp.where(hit1, 1.0, 0.0) + jnp.where(hit2, 1.0, 0.0)
    cum = jnp.dot(onehot.astype(BF16), upper_ref[...], preferred_element_type=F32)
    cnt = jnp.sum(onehot, axis=1, keepdims=True)
    nblk = jnp.floor((cnt + (ROW_BLOCK - 1)) * (1.0 / ROW_BLOCK))
    nblk_b = jnp.broadcast_to(nblk, (N_EXPERTS, LANES))
    seg = jnp.dot(lower_ref[...], nblk_b.astype(BF16), preferred_element_type=F32) * ROW_BLOCK
    base = seg[:, 0:1] + cum
    pos1 = jnp.sum(jnp.where(hit1, base, 0.0), axis=0, keepdims=True)
    pos2 = jnp.sum(jnp.where(hit2, base, 0.0), axis=0, keepdims=True)
    p1i = pos1.astype(I32)
    p2i = pos2.astype(I32)
    used_rows = (jnp.sum(nblk) * ROW_BLOCK).astype(I32)

    prow = lax.broadcasted_iota(I32, (LANES, CHUNK), 0)
    parts = _bf16_parts(route[2:3, :]) + _bf16_parts(route[3:4, :])
    gpart_rows = jnp.zeros((LANES, CHUNK), F32)
    for j, part in enumerate(parts):
        gpart_rows = jnp.where(prow == j, part, gpart_rows)
    h_ext = jnp.concatenate([h, gpart_rows.T.astype(BF16)], axis=1)
    glane = lax.broadcasted_iota(I32, (1, GATE_COLS), 1)
    one = jnp.ones((), BF16)
    zero = jnp.zeros((), BF16)
    def permute(row0, n_rows):
        r16 = lax.broadcasted_iota(I32, (n_rows, CHUNK), 0).astype(jnp.int16)
        q1 = (p1i - row0).astype(jnp.int16)
        q2 = (p2i - row0).astype(jnp.int16)
        sel1 = jnp.where(r16 == q1, one, zero)
        sel = jnp.where(r16 == q2, one, sel1)
        is_slot1 = jnp.max(sel1, axis=1, keepdims=True)
        rows = pl.ds(row0, n_rows)
        xg = jnp.dot(sel, h_ext, preferred_element_type=F32)
        xl_ref[rows, 0:D_MODEL] = xg[:, :D_MODEL].astype(BF16)
        g6 = xg[:, D_MODEL:]
        g3 = jnp.where(is_slot1.astype(F32) > 0.0, g6, pltpu.roll(g6, GATE_COLS - 3, axis=1))
        xl_ref[rows, D_MODEL:XL_COLS] = jnp.where(glane < 3, g3, 0.0).astype(BF16)

    pos_rows = jnp.where(prow == 0, pos1, jnp.where(prow == 1, pos2, 0.0))
    pos_ref[...] = pos_rows.T
    nb_ref[0] = nblk_b.astype(I32)

    def permute_tail():
        tail_rows = CHUNK_ROWS - COMMON_ROWS

        @pl.when(COMMON_ROWS < used_rows)
        def _():
            permute(COMMON_ROWS, tail_rows)

        @pl.when(COMMON_ROWS >= used_rows)
        def _():
            xl_ref[pl.ds(COMMON_ROWS, tail_rows), :] = jnp.zeros((tail_rows, XL_COLS), BF16)

    return permute, permute_tail


def _max_tiles(nch):
    max_blocks = nch * (2 * CHUNK + N_EXPERTS * (ROW_BLOCK - 1)) // ROW_BLOCK
    return max_blocks // TILE_BLOCKS + N_EXPERTS


def _masked_prefix(le, values):
    delta = values - jnp.concatenate([jnp.zeros((1,), values.dtype), values[:-1]])
    return jnp.sum(jnp.where(le, delta[None, :], 0), axis=1)


def _tile_plan(nb, nch):
    n_tiles = _max_tiles(nch)
    nbt = nb.T
    nbe = jnp.sum(nbt, axis=1)
    nte = (nbe + TILE_BLOCKS - 1) // TILE_BLOCKS
    tile_end = jnp.cumsum(nte)
    tile_start = tile_end - nte
    n_act = tile_end[-1]
    tiles = jnp.arange(n_tiles, dtype=I32)
    te = jnp.sum((tile_end[None, :] <= tiles[:, None]).astype(I32), axis=1)
    e_ar = jnp.arange(N_EXPERTS, dtype=I32)
    active = nte > 0
    te = jnp.where(tiles < n_act, te, jnp.max(jnp.where(active, e_ar, 0)))
    first = jnp.logical_and(te != jnp.concatenate([jnp.full((1,), -1, I32), te[:-1]]),
                            tiles < n_act).astype(I32)
    rank = jnp.cumsum(active.astype(I32)) - 1
    k_ar = jnp.arange(N_EXPERTS + W_SLOTS, dtype=I32)
    eseq = jnp.sum(jnp.where(jnp.logical_and(active[None, :], rank[None, :] == k_ar[:, None]),
                             e_ar[None, :], 0), axis=1)
    n_exp = jnp.sum(active.astype(I32))

    cb_excl = jnp.cumsum(nbt, axis=1) - nbt
    gs = TILE_BLOCKS * tile_start[:, None] + cb_excl
    seg_blk = jnp.cumsum(nb, axis=1) - nb
    base_blk = jnp.arange(nch, dtype=I32)[None, :] * CHUNK_BLOCKS + seg_blk.T
    gs_f = gs.reshape(-1)
    slots = jnp.arange(n_tiles * TILE_BLOCKS, dtype=I32)
    le = gs_f[None, :] <= slots[:, None]
    blk = slots + _masked_prefix(le, (base_blk - gs).reshape(-1))
    valid = slots < _masked_prefix(le, (gs + nbt).reshape(-1))
    gblk = jnp.concatenate([jnp.where(valid, blk, 0).astype(I32),
                            jnp.zeros(((X_SLOTS - 1) * TILE_BLOCKS,), I32)])
    n_valid = jnp.concatenate([jnp.sum(valid.reshape(n_tiles, TILE_BLOCKS).astype(I32), axis=1),
                               jnp.zeros((X_SLOTS - 1,), I32)])

    m = jnp.arange(CHUNK_BLOCKS, dtype=I32)
    le_c = seg_blk[:, None, :] <= m[None, :, None]
    shift = (gs.T - seg_blk)
    delta = shift - jnp.concatenate([jnp.zeros((nch, 1), I32), shift[:, :-1]], axis=1)
    slot_of = m[None, :] + jnp.sum(jnp.where(le_c, delta[:, None, :], 0), axis=2)
    used_blocks = jnp.sum(nb, axis=1)
    slot_of = jnp.where(m[None, :] < used_blocks[:, None], slot_of, 0).astype(I32)
    pad = Y_RING - 1
    slot_of = jnp.concatenate([slot_of.reshape(-1), jnp.zeros((pad * CHUNK_BLOCKS,), I32)])
    used_rows = jnp.concatenate([ROW_BLOCK * used_blocks, jnp.zeros((pad,), I32)]).astype(I32)
    return ((n_act.reshape(1).astype(I32), n_exp.reshape(1), first, eseq, gblk, n_valid),
            (used_rows, slot_of))


def _gmm_kernel(nact_ref, nexp_ref, first_ref, eseq_ref, gblk_ref, nval_ref,
                xl_hbm, wu_hbm, wd_hbm, yt_hbm,
                xbuf, ybuf, zbuf, wu_st, wd_st, in_sem, out_sem, w_sem, z_sem):
    n_act = nact_ref[0]
    n_exp = nexp_ref[0]
    n_tiles = yt_hbm.shape[0] // TILE_BLOCKS
    prefetch = X_SLOTS - 1

    def gather(tile, b):
        s = tile % X_SLOTS
        blk = gblk_ref[tile * TILE_BLOCKS + b]
        return pltpu.make_async_copy(xl_hbm.at[blk], xbuf.at[s, b], in_sem.at[s])

    def write_back(tile):
        s = tile % Y_SLOTS
        return pltpu.make_async_copy(ybuf.at[s], yt_hbm.at[pl.ds(tile * TILE_BLOCKS, TILE_BLOCKS)],
                                     out_sem.at[s])

    def zero_fill(tile):
        return pltpu.make_async_copy(zbuf, yt_hbm.at[pl.ds(tile * TILE_BLOCKS, TILE_BLOCKS)], z_sem)

    def weights(q):
        e = eseq_ref[q]
        s = q % W_SLOTS
        return (pltpu.make_async_copy(wu_hbm.at[e], wu_st.at[s], w_sem.at[0, s]),
                pltpu.make_async_copy(wd_hbm.at[e], wd_st.at[s], w_sem.at[1, s]))

    def tile_blocks(tile, fn):
        def one(b, carry):
            fn(b)
            return carry

        lax.fori_loop(0, nval_ref[tile], one, 0)

    xbuf[...] = jnp.zeros_like(xbuf)
    for q0 in range(W_SLOTS - 1):
        @pl.when(q0 < n_exp)
        def _(q0=q0):
            for cp in weights(q0):
                cp.start()
    for t0 in range(prefetch):
        tile_blocks(t0, lambda b, t0=t0: gather(t0, b).start())

    def body(i, q):
        tile_blocks(i, lambda b: gather(i, b).wait())

        @pl.when(i >= Y_SLOTS)
        def _():
            write_back(i - Y_SLOTS).wait()

        is_first = first_ref[i] == 1

        @pl.when(is_first)
        def _():
            for cp in weights(q):
                cp.wait()

            @pl.when(q + W_SLOTS - 1 < n_exp)
            def _():
                for cp in weights(q + W_SLOTS - 1):
                    cp.start()

        q = q + is_first.astype(I32)
        ws = (q - 1) % W_SLOTS
        xs = i % X_SLOTS
        ys = i % Y_SLOTS

        x = xbuf[xs].reshape(TILE_ROWS, XL_COLS)
        gate = jnp.sum(x[:, D_MODEL:].astype(F32), axis=1, keepdims=True)
        au = jnp.dot(x[:, :D_MODEL], wu_st[ws], preferred_element_type=F32)
        tile_blocks(i + prefetch, lambda b: gather(i + prefetch, b).start())
        a = au[:, :EXPERT_FF]
        act = (a * _sigmoid(a)) * au[:, EXPERT_FF:]
        y = jnp.dot(act.astype(BF16), wd_st[ws], preferred_element_type=F32)
        ybuf[ys] = (gate * y).astype(BF16).reshape(TILE_BLOCKS, ROW_BLOCK, D_MODEL)
        write_back(i).start()

        @pl.when(n_act + i < n_tiles)
        def _():
            zero_fill(n_act + i).start()
        return q

    zbuf[...] = jnp.zeros_like(zbuf)
    lax.fori_loop(0, n_act, body, jnp.int32(0))

    def fill_rest(t, carry):
        zero_fill(t).start()
        return carry

    lax.fori_loop(jnp.minimum(2 * n_act, n_tiles), n_tiles, fill_rest, 0)

    def drain(t, carry):
        zero_fill(t).wait()
        return carry

    lax.fori_loop(n_act, n_tiles, drain, 0)
    for k in range(prefetch):
        tile_blocks(n_act + k, lambda b, k=k: gather(n_act + k, b).wait())
    for k in range(Y_SLOTS):
        @pl.when(n_act - 1 - k >= 0)
        def _(k=k):
            write_back(n_act - 1 - k).wait()


def _grouped_mlp(plan, xl, w_up, w_down, n_tiles):
    grid_spec = pltpu.PrefetchScalarGridSpec(
        num_scalar_prefetch=len(plan),
        grid=(1,),
        in_specs=[pl.BlockSpec(memory_space=pl.ANY)] * 3,
        out_specs=pl.BlockSpec(memory_space=pl.ANY),
        scratch_shapes=[pltpu.VMEM((X_SLOTS, TILE_BLOCKS, ROW_BLOCK, XL_COLS), BF16),
                        pltpu.VMEM((Y_SLOTS, TILE_BLOCKS, ROW_BLOCK, D_MODEL), BF16),
                        pltpu.VMEM((TILE_BLOCKS, ROW_BLOCK, D_MODEL), BF16),
                        pltpu.VMEM((W_SLOTS, D_MODEL, 2 * EXPERT_FF), BF16),
                        pltpu.VMEM((W_SLOTS, EXPERT_FF, D_MODEL), BF16),
                        pltpu.SemaphoreType.DMA((X_SLOTS,)),
                        pltpu.SemaphoreType.DMA((Y_SLOTS,)),
                        pltpu.SemaphoreType.DMA((2, W_SLOTS)),
                        pltpu.SemaphoreType.DMA(())])
    return pl.pallas_call(
        _gmm_kernel,
        grid_spec=grid_spec,
        out_shape=jax.ShapeDtypeStruct((n_tiles * TILE_BLOCKS, ROW_BLOCK, D_MODEL), BF16),
        compiler_params=_cparams(("arbitrary",)),
        name="moe_experts",
    )(*plan, xl, w_up, w_down)


def _combine_kernel(used_ref, slot_ref, yt_hbm, pos_ref, x1_ref, gt2_ref, fg_ref, o_ref, ybuf, sem):
    c = pl.program_id(0)
    n_chunks = pl.num_programs(0)
    used_rows = used_ref[c]

    def chunk_blocks(chunk, op):
        s = chunk % Y_RING

        def copy(b):
            slot = slot_ref[chunk * CHUNK_BLOCKS + b]
            return pltpu.make_async_copy(yt_hbm.at[slot], ybuf.at[s, b], sem.at[s])

        for b in range(COMMON_BLOCKS):
            op(copy(b))

        @pl.when(used_ref[chunk] > COMMON_ROWS)
        def _():
            for b in range(COMMON_BLOCKS, CHUNK_BLOCKS):
                op(copy(b))

    @pl.when(c == 0)
    def _():
        for k in range(Y_RING - 1):
            chunk_blocks(k, lambda cp: cp.start())

    chunk_blocks(c + Y_RING - 1, lambda cp: cp.start())
    chunk_blocks(c, lambda cp: cp.wait())

    def body(k_rows):
        pos = pos_ref[...]
        p1 = pos[:, 0:1].astype(I32)
        p2 = pos[:, 1:2].astype(I32)
        r = lax.broadcasted_iota(I32, (CHUNK, k_rows), 1)
        sel = jnp.where(r == p1, 1.0, jnp.where(r == p2, 1.0, 0.0)).astype(BF16)
        yl = ybuf[c % Y_RING, 0:k_rows // ROW_BLOCK].reshape(k_rows, D_MODEL)
        y = jnp.dot(sel, yl, preferred_element_type=F32)
        x2 = x1_ref[...] + gt2_ref[0] * y
        inv = lax.rsqrt(jnp.mean(x2 * x2, axis=-1, keepdims=True) + NORM_EPS)
        o_ref[...] = (x2 * inv) * fg_ref[...]

    pl.when(used_rows <= COMMON_ROWS)(lambda: body(COMMON_ROWS))
    pl.when(used_rows > COMMON_ROWS)(lambda: body(CHUNK_ROWS))

    @pl.when(c == n_chunks - 1)
    def _():
        for k in range(1, Y_RING):
            chunk_blocks(c + k, lambda cp: cp.wait())


def _combine(used_rows, slot_of, yt, pos_c, x1, mod3, final_g, seq):
    n = x1.shape[0]
    per_seq = seq // CHUNK
    grid_spec = pltpu.PrefetchScalarGridSpec(
        num_scalar_prefetch=2,
        grid=(n // CHUNK,),
        in_specs=[pl.BlockSpec(memory_space=pl.ANY),
                  pl.BlockSpec((CHUNK, LANES), lambda c, u, s: (c, 0)),
                  pl.BlockSpec((CHUNK, D_MODEL), lambda c, u, s: (c, 0)),
                  pl.BlockSpec((1, 1, D_MODEL), lambda c, u, s: (c // per_seq, 0, MOD_GATE2)),
                  pl.BlockSpec((1, D_MODEL), lambda c, u, s: (0, 0))],
        out_specs=pl.BlockSpec((CHUNK, D_MODEL), lambda c, u, s: (c, 0)),
        scratch_shapes=[pltpu.VMEM((Y_RING, CHUNK_BLOCKS, ROW_BLOCK, D_MODEL), BF16),
                        pltpu.SemaphoreType.DMA((Y_RING,))])
    return pl.pallas_call(
        _combine_kernel,
        grid_spec=grid_spec,
        out_shape=jax.ShapeDtypeStruct((n, D_MODEL), F32),
        compiler_params=_cparams(("arbitrary",)),
        name="moe_combine",
    )(used_rows, slot_of, yt, pos_c, x1, mod3, final_g)


def _rope_tables(seq):
    n_freq = HEAD_DIM // 4
    inv_freq = ROPE_BASE ** (-jnp.arange(n_freq, dtype=F32) / n_freq)
    rows = seq // GRID_W
    row = jnp.repeat(jnp.arange(rows, dtype=F32), GRID_W)
    col = jnp.tile(jnp.arange(GRID_W, dtype=F32), rows)
    ang = jnp.concatenate([row[:, None] * inv_freq, col[:, None] * inv_freq], axis=-1)
    cos, sin = jnp.cos(ang), jnp.sin(ang)
    return jnp.tile(cos, (1, 4)), jnp.concatenate([-sin, sin, -sin, sin], axis=1)


def _window_bias():
    key = np.arange(3 * BLOCK)[:, None]
    qry = np.arange(BLOCK)[None, :]
    valid = (key - qry >= 0) & (key - qry <= 2 * BLOCK)
    return jnp.asarray(np.where(valid, 0.0, NEG_INF), F32)


def kernel(x, c, ctx, c_ctx, w_ada, b_ada, norm1_g, w_in, w_conv, b_conv, w_a, w_b, sink, w_o,
           norm2_g, w_group, b_group, w_router, b_router, w_up, w_down, final_g):
    bsz, seq, _ = x.shape
    assert w_ada.shape[0] == 1 and seq % SEQ_TILE == 0

    cc = jnp.zeros((16, D_MODEL), F32).at[:bsz].set(c).at[bsz].set(c_ctx)
    mod = _modulation(cc, w_ada[0], b_ada[0][None, :])
    mod3 = mod[:bsz].reshape(bsz, 1, N_MOD * D_MODEL)
    csh1 = mod[bsz:bsz + 1, 0:D_MODEL]
    csc1 = mod[bsz:bsz + 1, D_MODEL:2 * D_MODEL]

    w = w_in[0]
    n1g = norm1_g[0][None, :]
    n2g = norm2_g[0][None, :]
    w_kv = w[:, OFF_K:OFF_GA].astype(BF16)
    w_a_cols = jnp.concatenate([w[:, OFF_CG:OFF_Q], w[:, OFF_K:OFF_GA]], axis=1).astype(BF16)
    w_b_cols = jnp.concatenate([w[:, OFF_B:OFF_CG], w[:, OFF_Q:OFF_K], w[:, OFF_GA:]],
                               axis=1).astype(BF16)
    cos_t, sin_t = _rope_tables(seq)

    k_ctx, vt_ctx = _context_kv(ctx, n1g, csh1, csc1, w_kv)
    sink_row = jnp.repeat(sink[0].astype(F32), BLOCK)[None, :]
    pad_g = jnp.zeros((D_MODEL, GROUP_ROWS - N_GROUPS), F32)
    pad_e = jnp.zeros((D_MODEL, ROUTER_ROWS - GROUP_ROWS - N_EXPERTS), F32)
    wr_t = jnp.concatenate([w_group[0], pad_g, w_router[0], pad_e], axis=1).T.astype(BF16)
    br_col = jnp.concatenate([b_group[0], pad_g[0], b_router[0], pad_e[0]])[:, None]

    upper = jnp.asarray(np.triu(np.ones((CHUNK, CHUNK), np.float32), 1), BF16)
    lower = jnp.asarray(np.tril(np.ones((N_EXPERTS, N_EXPERTS), np.float32), -1), BF16)
    x1, xl, pos_c, nb, w_up_bf, w_down_bf = _mix(
        x, n1g, n2g, mod3, w_a_cols, w_b_cols, cos_t, sin_t, k_ctx, vt_ctx,
        w_conv[0], b_conv[0][None, :], w_a[0].astype(BF16), w_b[0].astype(BF16),
        w_o[0].astype(BF16), sink_row, _window_bias(), wr_t, br_col, w_up[0], w_down[0],
        upper, lower)

    n = bsz * seq
    nch = n // CHUNK
    nb = nb[:, :, 0]
    expert_plan, combine_plan = _tile_plan(nb, nch)
    yt = _grouped_mlp(expert_plan, xl.reshape(-1, ROW_BLOCK, XL_COLS), w_up_bf, w_down_bf,
                      _max_tiles(nch))
    out = _combine(*combine_plan, yt, pos_c, x1.reshape(n, D_MODEL), mod3, final_g[None, :], seq)
    return out.reshape(bsz, seq, D_MODEL)
```

```python
import functools

import numpy as np
import jax
import jax.numpy as jnp
from jax import lax
from jax.experimental import pallas as pl
from jax.experimental.pallas import tpu as pltpu

F32 = jnp.float32
BF16 = jnp.bfloat16
I32 = jnp.int32

D_MODEL = 1024
GRID_W = 64
CONV_W = 512
N_HEADS = 8
N_KV_HEADS = 2
HEAD_DIM = 64
ATT_W = N_HEADS * HEAD_DIM
KV_W = N_KV_HEADS * HEAD_DIM
BLOCK = 128
ROPE_BASE = 10000.0
N_GROUPS = 4
EXPERTS_PER_GROUP = 8
N_EXPERTS = N_GROUPS * EXPERTS_PER_GROUP
EXPERT_FF = 256
N_MOD = 6
MOD_SHIFT1, MOD_SCALE1, MOD_GATE1, MOD_SHIFT2, MOD_SCALE2, MOD_GATE2 = range(N_MOD)
NORM_EPS = 1e-6
NEG_INF = -1e30
LOG2E = 1.4426950408889634
ONES_ROWS = 16

OFF_B, OFF_CG, OFF_XIN, OFF_Q, OFF_K, OFF_V, OFF_GA, OFF_GB = (
    0, 512, 1024, 1536, 2048, 2176, 2304, 3328)
IN_COLS = 4352

LANES = 128
SEQ_TILE = 512
BLOCKS_PER_TILE = SEQ_TILE // BLOCK
ROUTE_ROWS = 8
GROUP_ROWS = 8
ROUTER_ROWS = 48
VMEM_LIMIT = 62 * 1024 * 1024

CHUNK = SEQ_TILE
ROW_BLOCK = 16
CHUNK_ROWS = -(-(2 * CHUNK + N_EXPERTS * (ROW_BLOCK - 1)) // 256) * 256
CHUNK_BLOCKS = CHUNK_ROWS // ROW_BLOCK
GATE_COLS = LANES
XL_COLS = D_MODEL + GATE_COLS
COMMON_ROWS = 2 * CHUNK + 256
COMMON_BLOCKS = COMMON_ROWS // ROW_BLOCK
TILE_BLOCKS = 32
TILE_ROWS = TILE_BLOCKS * ROW_BLOCK
X_SLOTS = 6
Y_SLOTS = 4
W_SLOTS = 3
Y_RING = 3
RING = 3


def _cparams(sem):
    return pltpu.CompilerParams(dimension_semantics=sem, vmem_limit_bytes=VMEM_LIMIT)


def _rms_mod(x, g, shift, scale):
    inv = lax.rsqrt(jnp.mean(x * x, axis=-1, keepdims=True) + NORM_EPS)
    return (x * inv) * (g * (1.0 + scale)) + shift


def _mod_kernel(c_ref, w_ref, b_ref, o_ref):
    c = c_ref[...]
    a = (c * jax.nn.sigmoid(c)).astype(BF16)
    o_ref[...] = jnp.dot(a, w_ref[...].astype(BF16), preferred_element_type=F32) + b_ref[...]


def _modulation(cc, w_ada, b_ada):
    rows = cc.shape[0]
    cols = w_ada.shape[1]
    tile = 1024
    return pl.pallas_call(
        _mod_kernel,
        grid=(cols // tile,),
        in_specs=[pl.BlockSpec((rows, D_MODEL), lambda j: (0, 0)),
                  pl.BlockSpec((D_MODEL, tile), lambda j: (0, j)),
                  pl.BlockSpec((1, tile), lambda j: (0, j))],
        out_specs=pl.BlockSpec((rows, tile), lambda j: (0, j)),
        out_shape=jax.ShapeDtypeStruct((rows, cols), F32),
        compiler_params=_cparams(("arbitrary",)),
        name="adaln_mod",
    )(cc, w_ada, b_ada)


def _ctx_kernel(x_ref, g_ref, sh_ref, sc_ref, w_ref, k_ref, vt_ref):
    h = _rms_mod(x_ref[0], g_ref[...], sh_ref[...], sc_ref[...]).astype(BF16)
    kv = jnp.dot(h, w_ref[...], preferred_element_type=F32)
    k_ref[0] = kv[:, :KV_W].astype(BF16)
    vt_ref[0] = kv[:, KV_W:].T.astype(BF16)


def _context_kv(ctx, norm_g, csh, csc, w_kv):
    bsz, clen, _ = ctx.shape
    return pl.pallas_call(
        _ctx_kernel,
        grid=(bsz,),
        in_specs=[pl.BlockSpec((1, clen, D_MODEL), lambda b: (b, 0, 0)),
                  pl.BlockSpec((1, D_MODEL), lambda b: (0, 0)),
                  pl.BlockSpec((1, D_MODEL), lambda b: (0, 0)),
                  pl.BlockSpec((1, D_MODEL), lambda b: (0, 0)),
                  pl.BlockSpec((D_MODEL, 2 * KV_W), lambda b: (0, 0))],
        out_specs=[pl.BlockSpec((1, clen, KV_W), lambda b: (b, 0, 0)),
                   pl.BlockSpec((1, KV_W, clen), lambda b: (b, 0, 0))],
        out_shape=[jax.ShapeDtypeStruct((bsz, clen, KV_W), BF16),
                   jax.ShapeDtypeStruct((bsz, KV_W, clen), BF16)],
        compiler_params=_cparams(("arbitrary",)),
        name="context_kv",
    )(ctx, norm_g, csh, csc, w_kv)


def _rope(t, cos, sin_signed):
    lane = lax.broadcasted_iota(I32, (1, LANES), 1)
    first_half = (lane % HEAD_DIM) < (HEAD_DIM // 2)
    outs = []
    for j in range(t.shape[1] // LANES):
        tj = t[:, j * LANES:(j + 1) * LANES]
        partner = jnp.where(first_half,
                            pltpu.roll(tj, LANES - HEAD_DIM // 2, axis=1),
                            pltpu.roll(tj, HEAD_DIM // 2, axis=1))
        outs.append(tj * cos + partner * sin_signed)
    return outs[0] if len(outs) == 1 else jnp.concatenate(outs, axis=1)


def _sigmoid(x):
    return 1.0 / (1.0 + jnp.exp2(x * (-LOG2E)))


def _attn_scores(q_t, g, k_win, k_ctx, bias_p4, bias_n4):
    zeros = jnp.zeros((HEAD_DIM, 4 * BLOCK), BF16)
    qg = jnp.concatenate([q_t[(4 * g + h) * HEAD_DIM:(4 * g + h + 1) * HEAD_DIM, :]
                          for h in range(4)], axis=1)
    rhs = jnp.concatenate([qg, zeros] if g == 0 else [zeros, qg], axis=0)
    s_win = jnp.dot(k_win, rhs, preferred_element_type=F32)
    s_ctx = jnp.dot(k_ctx, rhs, preferred_element_type=F32)
    return (s_win[0:BLOCK] + bias_p4, s_win[BLOCK:2 * BLOCK], s_win[2 * BLOCK:] + bias_n4, s_ctx)


def _attn_probs(scores, sink):
    m = sink
    for s in scores:
        m = jnp.maximum(m, jnp.max(s, axis=0, keepdims=True))
    p_win = jnp.concatenate([jnp.exp2(s - m).astype(BF16) for s in scores[:3]], axis=0)
    p_ctx = jnp.exp2(scores[3] - m).astype(BF16)
    return p_win, p_ctx, m


def _attn_values(probs, sink, vt_win_g, vt_ctx_g):
    p_win, p_ctx, m = probs

    def with_ones(vt):
        r = lax.broadcasted_iota(I32, (ONES_ROWS, vt.shape[1]), 0)
        return jnp.concatenate([vt, jnp.where(r == 0, 1.0, 0.0).astype(BF16)], axis=0)

    o_ext = (jnp.dot(with_ones(vt_win_g), p_win, preferred_element_type=F32)
             + jnp.dot(with_ones(vt_ctx_g), p_ctx, preferred_element_type=F32))
    denom = o_ext[HEAD_DIM:HEAD_DIM + 1, :] + jnp.exp2(sink - m)
    return o_ext[:HEAD_DIM, :] / denom


def _route(logits_t):
    t = logits_t.shape[1]
    grow = lax.broadcasted_iota(I32, (GROUP_ROWS, t), 0)
    gl = jnp.where(grow < N_GROUPS, logits_t[0:GROUP_ROWS, :], NEG_INF)
    gm = jnp.max(gl, axis=0, keepdims=True)
    p_g = 1.0 / jnp.sum(jnp.exp(gl - gm), axis=0, keepdims=True)
    g_idx = jnp.min(jnp.where(gl == gm, grow, N_GROUPS), axis=0, keepdims=True)

    erow = lax.broadcasted_iota(I32, (N_EXPERTS, t), 0)
    el = logits_t[GROUP_ROWS:GROUP_ROWS + N_EXPERTS, :]
    sel = (erow // EXPERTS_PER_GROUP) == g_idx
    em = jnp.where(sel, el, NEG_INF)
    m1 = jnp.max(em, axis=0, keepdims=True)
    i1 = jnp.min(jnp.where(em == m1, erow, N_EXPERTS), axis=0, keepdims=True)
    em2 = jnp.where(erow == i1, NEG_INF, em)
    m2 = jnp.max(em2, axis=0, keepdims=True)
    i2 = jnp.min(jnp.where(em2 == m2, erow, N_EXPERTS), axis=0, keepdims=True)
    z = jnp.sum(jnp.where(sel, jnp.exp(el - m1), 0.0), axis=0, keepdims=True)
    p1 = 1.0 / z
    p2 = jnp.exp(m2 - m1) / z
    gate1 = p_g * p1 / (p1 + p2)
    gate2 = p_g * p2 / (p1 + p2)
    pad = jnp.zeros((ROUTE_ROWS - 4, t), F32)
    return jnp.concatenate([i1.astype(F32), i2.astype(F32), gate1, gate2, pad], axis=0)


def _project_a(x_ref, g_ref, sh_ref, sc_ref, w_ref, cos_ref, sin_ref, slot, uc_slot,
               uc_ring, k_ring, vt_ring):
    h = _rms_mod(x_ref[0], g_ref[...], sh_ref[0], sc_ref[0]).astype(BF16)
    u = jnp.dot(h, w_ref[...], preferred_element_type=F32)
    uc_ring[uc_slot] = u[:, :CONV_W] * u[:, CONV_W:2 * CONV_W]
    k = u[:, 2 * CONV_W:2 * CONV_W + KV_W]
    k_ring[slot] = _rope(k, cos_ref[...], sin_ref[...]).astype(BF16)
    vt_ring[slot] = u[:, 2 * CONV_W + KV_W:].T.astype(BF16)


def _mix_step(x_ref, g1n_ref, g2n_ref, sh1_ref, sc1_ref, gt1_ref, sh2_ref, sc2_ref,
              wb_ref, cos_ref, sin_ref,
              xn_ref, shn_ref, scn_ref, cosn_ref, sinn_ref, wac_ref,
              kc_ref, vtc_ref, wconv_ref, bconv_ref, wa_ref, wbb_ref, wo_ref,
              sink_ref, bias_ref, wr_ref, br_ref, wu_ref, wd_ref, upper_ref, lower_ref,
              x1_ref, xl_ref, pos_ref, nb_ref, wub_ref, wdb_ref,
              h2_scr, route_scr, uc_ring, uc_last, k_ring, vt_ring,
              *, n_seq_tiles):
    step = pl.program_id(0)
    t_idx = step % n_seq_tiles
    is_first = t_idx == 0
    is_last = t_idx == n_seq_tiles - 1
    own, prv, nxt = step % RING, (step + RING - 1) % RING, (step + 1) % RING

    permute, permute_tail = _sort_chunk(h2_scr[...], route_scr[...], upper_ref, lower_ref,
                                        xl_ref, pos_ref, nb_ref)

    @pl.when(step == 0)
    def _():
        uc_ring[...] = jnp.zeros_like(uc_ring)
        uc_last[...] = jnp.zeros_like(uc_last)
        k_ring[...] = jnp.zeros_like(k_ring)
        vt_ring[...] = jnp.zeros_like(vt_ring)
        _project_a(x_ref, g1n_ref, sh1_ref, sc1_ref, wac_ref, cos_ref, sin_ref, 0, 0,
                   uc_ring, k_ring, vt_ring)

    x = x_ref[0]
    hb = _rms_mod(x, g1n_ref[...], sh1_ref[0], sc1_ref[0]).astype(BF16)
    bq = jnp.dot(hb, wb_ref[:, 0:2 * CONV_W], preferred_element_type=F32)
    q = _rope(bq[:, CONV_W:], cos_ref[...], sin_ref[...]) * (HEAD_DIM ** -0.5 * LOG2E)

    def block_ring(ring, slot_, blk, axis):
        lo = blk * BLOCK
        return ring[slot_, lo:lo + BLOCK, :] if axis == 0 else ring[slot_, :, lo:lo + BLOCK]

    def window(ring, jb, axis):
        parts = []
        for blk in (jb - 1, jb, jb + 1):
            if blk < 0:
                parts.append(block_ring(ring, prv, BLOCKS_PER_TILE - 1, axis))
            elif blk >= BLOCKS_PER_TILE:
                parts.append(block_ring(ring, nxt, 0, axis))
            else:
                parts.append(block_ring(ring, own, blk, axis))
        return jnp.concatenate(parts, axis=axis)

    k_ctx = kc_ref[0]
    vt_ctx = vtc_ref[0]
    bias_prev = bias_ref[0:BLOCK, :]
    bias_next = bias_ref[2 * BLOCK:3 * BLOCK, :]
    sink_row = sink_ref[...] * LOG2E
    n_units = BLOCKS_PER_TILE * N_KV_HEADS
    gate_cols = 2 * D_MODEL // n_units
    gate_chunks = []

    def gate_chunk(u):
        c0 = 2 * CONV_W + u * gate_cols
        gate_chunks.append(jnp.dot(hb, wb_ref[:, c0:c0 + gate_cols], preferred_element_type=F32))

    q_ts, biases = [], []
    for jb in range(BLOCKS_PER_TILE):
        bias_p, bias_n = bias_prev, bias_next
        if jb == 0:
            bias_p = bias_prev + jnp.where(is_first, NEG_INF, 0.0)
        if jb == BLOCKS_PER_TILE - 1:
            bias_n = bias_next + jnp.where(is_last, NEG_INF, 0.0)
        biases.append((jnp.concatenate([bias_p] * 4, axis=1), jnp.concatenate([bias_n] * 4, axis=1)))
        q_ts.append(q[jb * BLOCK:(jb + 1) * BLOCK, :].T.astype(BF16))

    scores, probs, outs = {}, {}, {}
    for t in range(n_units + 2):
        if t == 3:
            wub_ref[...] = wu_ref[...].astype(BF16)
        if t == 5:
            wdb_ref[...] = wd_ref[...].astype(BF16)
        if t == 0:
            permute(0, COMMON_ROWS)
        if t == 2:
            _project_a(xn_ref, g1n_ref, shn_ref, scn_ref, wac_ref, cosn_ref, sinn_ref, nxt,
                       (step + 1) % 2, uc_ring, k_ring, vt_ring)
        if t < n_units:
            jb, g = divmod(t, N_KV_HEADS)
            scores[t] = _attn_scores(q_ts[jb], g, window(k_ring, jb, 0), k_ctx, *biases[jb])
            gate_chunk(t)
        if 0 <= t - 1 < n_units:
            g = (t - 1) % N_KV_HEADS
            probs[t - 1] = _attn_probs(scores.pop(t - 1),
                                       sink_row[:, g * 4 * BLOCK:(g + 1) * 4 * BLOCK])
        if 0 <= t - 2 < n_units:
            jb, g = divmod(t - 2, N_KV_HEADS)
            outs[t - 2] = _attn_values(
                probs.pop(t - 2), sink_row[:, g * 4 * BLOCK:(g + 1) * 4 * BLOCK],
                window(vt_ring, jb, 1)[g * HEAD_DIM:(g + 1) * HEAD_DIM, :],
                vt_ctx[g * HEAD_DIM:(g + 1) * HEAD_DIM, :])
    o_blocks = []
    for jb in range(BLOCKS_PER_TILE):
        o_rows = [outs[jb * N_KV_HEADS + g][:, h * BLOCK:(h + 1) * BLOCK]
                  for g in range(N_KV_HEADS) for h in range(4)]
        o_blocks.append(jnp.concatenate(o_rows, axis=0).T)
    gates = jnp.concatenate(gate_chunks, axis=1)

    uc = uc_ring[step % 2]
    row = lax.broadcasted_iota(I32, (SEQ_TILE, 1), 0)
    prev_row = jnp.where(is_first, 0.0, uc_last[0:1, :])
    next_row = jnp.where(is_last, 0.0, uc_ring[(step + 1) % 2, 0:1, :])
    uc_last[0:1, :] = uc[SEQ_TILE - 1:SEQ_TILE, :]
    up = jnp.where(row == 0, prev_row, pltpu.roll(uc, 1, axis=0))
    dn = jnp.where(row == SEQ_TILE - 1, next_row, pltpu.roll(uc, SEQ_TILE - 1, axis=0))
    wconv = wconv_ref[...]
    y = bconv_ref[...] + (up * wconv[0:1, :] + uc * wconv[1:2, :] + dn * wconv[2:3, :])
    ya = jnp.dot((bq[:, :CONV_W] * y).astype(BF16), wa_ref[...], preferred_element_type=F32)

    o = jnp.concatenate(o_blocks, axis=0).astype(BF16)
    yb = jnp.dot(o, wbb_ref[...], preferred_element_type=F32)
    merged = (_sigmoid(gates[:, :D_MODEL]) * ya + _sigmoid(gates[:, D_MODEL:]) * yb).astype(BF16)
    x1 = x + gt1_ref[0] * jnp.dot(merged, wo_ref[...], preferred_element_type=F32)
    x1_ref[0] = x1
    h2 = _rms_mod(x1, g2n_ref[...], sh2_ref[0], sc2_ref[0]).astype(BF16)
    logits_t = lax.dot_general(wr_ref[...], h2, (((1,), (1,)), ((), ())),
                               preferred_element_type=F32) + br_ref[...]
    permute_tail()
    h2_scr[...] = h2
    route_scr[...] = _route(logits_t)


N_MIX_OUTPUTS = 6
N_MIX_SCRATCH = 6


def _mix_kernel(*refs, n_seq_tiles, batch):
    n_tail = N_MIX_OUTPUTS + N_MIX_SCRATCH
    upper_ref, lower_ref = refs[-n_tail - 2:-n_tail]
    _, xl_ref, pos_ref, nb_ref, _, _ = refs[-n_tail:-N_MIX_SCRATCH]
    h2_scr, route_scr = refs[-N_MIX_SCRATCH:-N_MIX_SCRATCH + 2]
    step = pl.program_id(0)
    n_main = n_seq_tiles * batch

    @pl.when(step == 0)
    def _():
        h2_scr[...] = jnp.zeros_like(h2_scr)
        route_scr[...] = jnp.zeros_like(route_scr)

    @pl.when(step < n_main)
    def _():
        _mix_step(*refs, n_seq_tiles=n_seq_tiles)

    @pl.when(step == n_main)
    def _():
        permute, permute_tail = _sort_chunk(h2_scr[...], route_scr[...], upper_ref, lower_ref,
                                            xl_ref, pos_ref, nb_ref)
        permute(0, COMMON_ROWS)
        permute_tail()


def _mix(x, norm1_g, norm2_g, mod3, w_a_cols, w_b_cols, cos_t, sin_t, k_ctx, vt_ctx,
         w_conv, b_conv, w_a, w_b, w_o, sink_row, bias, wr_t, br_col, w_up, w_down, upper, lower):
    bsz, seq, _ = x.shape
    nt = seq // SEQ_TILE
    clen = k_ctx.shape[1]
    nch = n_main = bsz * nt
    assert N_EXPERTS % n_main == 0, "each grid step converts an equal share of the experts"
    e_step = N_EXPERTS // n_main

    def tile(s):
        return divmod(jnp.clip(s, 0, n_main - 1), nt)

    def per_tile(shape, index, shift=0):
        return pl.BlockSpec(shape, lambda s: index(*tile(s + shift)))

    def const_spec(shape):
        return pl.BlockSpec(shape, lambda s: tuple(0 for _ in shape),
                            pipeline_mode=pl.Buffered(1))

    def x_spec(shift=0):
        return per_tile((1, SEQ_TILE, D_MODEL), lambda b, t: (b, t, 0), shift)

    def mod_spec(j, shift=0):
        return per_tile((1, 1, D_MODEL), lambda b, t: (b, 0, j), shift)

    def rope_spec(shift=0):
        return per_tile((SEQ_TILE, LANES), lambda b, t: (t, 0), shift)

    def expert_spec(w):
        return pl.BlockSpec((e_step,) + w.shape[1:], lambda s: (jnp.minimum(s, n_main - 1), 0, 0))

    def chunk_spec(shape):
        return pl.BlockSpec(shape, lambda s: (jnp.maximum(s - 1, 0),) + (0,) * (len(shape) - 1))

    in_specs = [
        x_spec(), const_spec((1, D_MODEL)), const_spec((1, D_MODEL)),
        mod_spec(MOD_SHIFT1), mod_spec(MOD_SCALE1), mod_spec(MOD_GATE1), mod_spec(MOD_SHIFT2),
        mod_spec(MOD_SCALE2),
        const_spec(w_b_cols.shape), rope_spec(), rope_spec(),
        x_spec(1), mod_spec(MOD_SHIFT1, 1), mod_spec(MOD_SCALE1, 1), rope_spec(1), rope_spec(1),
        const_spec(w_a_cols.shape),
        per_tile((1, clen, KV_W), lambda b, t: (b, 0, 0)),
        per_tile((1, KV_W, clen), lambda b, t: (b, 0, 0)),
        const_spec(w_conv.shape), const_spec(b_conv.shape),
        const_spec(w_a.shape), const_spec(w_b.shape), const_spec(w_o.shape),
        const_spec(sink_row.shape), const_spec(bias.shape),
        const_spec(wr_t.shape), const_spec(br_col.shape),
        expert_spec(w_up), expert_spec(w_down),
        const_spec(upper.shape), const_spec(lower.shape),
    ]
    out_specs = [
        x_spec(), chunk_spec((CHUNK_ROWS, XL_COLS)), chunk_spec((CHUNK, LANES)),
        chunk_spec((1, N_EXPERTS, LANES)), expert_spec(w_up), expert_spec(w_down),
    ]
    out_shape = [
        jax.ShapeDtypeStruct((bsz, seq, D_MODEL), F32),
        jax.ShapeDtypeStruct((nch * CHUNK_ROWS, XL_COLS), BF16),
        jax.ShapeDtypeStruct((nch * CHUNK, LANES), F32),
        jax.ShapeDtypeStruct((nch, N_EXPERTS, LANES), I32),
        jax.ShapeDtypeStruct(w_up.shape, BF16),
        jax.ShapeDtypeStruct(w_down.shape, BF16),
    ]
    return pl.pallas_call(
        functools.partial(_mix_kernel, n_seq_tiles=nt, batch=bsz),
        grid=(n_main + 1,),
        in_specs=in_specs,
        out_specs=out_specs,
        out_shape=out_shape,
        scratch_shapes=[pltpu.VMEM((CHUNK, D_MODEL), BF16), pltpu.VMEM((ROUTE_ROWS, CHUNK), F32),
                        pltpu.VMEM((2, SEQ_TILE, CONV_W), F32), pltpu.VMEM((8, CONV_W), F32),
                        pltpu.VMEM((RING, SEQ_TILE, KV_W), BF16),
                        pltpu.VMEM((RING, KV_W, SEQ_TILE), BF16)],
        compiler_params=_cparams(("arbitrary",)),
        name="token_mix",
    )(x, norm1_g, norm2_g, mod3, mod3, mod3, mod3, mod3, w_b_cols, cos_t, sin_t,
      x, mod3, mod3, cos_t, sin_t, w_a_cols, k_ctx, vt_ctx,
      w_conv, b_conv, w_a, w_b, w_o, sink_row, bias, wr_t, br_col, w_up, w_down, upper, lower)


def _bf16_parts(v):
    hi = v.astype(BF16).astype(F32)
    r1 = v - hi
    mid = r1.astype(BF16).astype(F32)
    lo = (r1 - mid).astype(BF16).astype(F32)
    return hi, mid, lo


def _sort_chunk(h, route, upper_ref, lower_ref, xl_ref, pos_ref, nb_ref):
    e1 = route[0:1, :].astype(I32)
    e2 = route[1:2, :].astype(I32)
    erow = lax.broadcasted_iota(I32, (N_EXPERTS, CHUNK), 0)
    hit1 = erow == e1
    hit2 = erow == e2
    onehot = jnp.where(hit1, 1.0, 0.0) + jnp.where(hit2, 1.0, 0.0)
    cum = jnp.dot(onehot.astype(BF16), upper_ref[...], preferred_element_type=F32)
    cnt = jnp.sum(onehot, axis=1, keepdims=True)
    nblk = jnp.floor((cnt + (ROW_BLOCK - 1)) * (1.0 / ROW_BLOCK))
    nblk_b = jnp.broadcast_to(nblk, (N_EXPERTS, LANES))
    seg = jnp.dot(lower_ref[...], nblk_b.astype(BF16), preferred_element_type=F32) * ROW_BLOCK
    base = seg[:, 0:1] + cum
    pos1 = jnp.sum(jnp.where(hit1, base, 0.0), axis=0, keepdims=True)
    pos2 = jnp.sum(jnp.where(hit2, base, 0.0), axis=0, keepdims=True)
    p1i = pos1.astype(I32)
    p2i = pos2.astype(I32)
    used_rows = (jnp.sum(nblk) * ROW_BLOCK).astype(I32)

    prow = lax.broadcasted_iota(I32, (LANES, CHUNK), 0)
    parts = _bf16_parts(route[2:3, :]) + _bf16_parts(route[3:4, :])
    gpart_rows = jnp.zeros((LANES, CHUNK), F32)
    for j, part in enumerate(parts):
        gpart_rows = jnp.where(prow == j, part, gpart_rows)
    h_ext = jnp.concatenate([h, gpart_rows.T.astype(BF16)], axis=1)
    glane = lax.broadcasted_iota(I32, (1, GATE_COLS), 1)
    one = jnp.ones((), BF16)
    zero = jnp.zeros((), BF16)
    def permute(row0, n_rows):
        r16 = lax.broadcasted_iota(I32, (n_rows, CHUNK), 0).astype(jnp.int16)
        q1 = (p1i - row0).astype(jnp.int16)
        q2 = (p2i - row0).astype(jnp.int16)
        sel1 = jnp.where(r16 == q1, one, zero)
        sel = jnp.where(r16 == q2, one, sel1)
        is_slot1 = jnp.max(sel1, axis=1, keepdims=True)
        rows = pl.ds(row0, n_rows)
        xg = jnp.dot(sel, h_ext, preferred_element_type=F32)
        xl_ref[rows, 0:D_MODEL] = xg[:, :D_MODEL].astype(BF16)
        g6 = xg[:, D_MODEL:]
        g3 = jnp.where(is_slot1.astype(F32) > 0.0, g6, pltpu.roll(g6, GATE_COLS - 3, axis=1))
        xl_ref[rows, D_MODEL:XL_COLS] = jnp.where(glane < 3, g3, 0.0).astype(BF16)

    pos_rows = jnp.where(prow == 0, pos1, jnp.where(prow == 1, pos2, 0.0))
    pos_ref[...] = pos_rows.T
    nb_ref[0] = nblk_b.astype(I32)

    def permute_tail():
        tail_rows = CHUNK_ROWS - COMMON_ROWS

        @pl.when(COMMON_ROWS < used_rows)
        def _():
            permute(COMMON_ROWS, tail_rows)

        @pl.when(COMMON_ROWS >= used_rows)
        def _():
            xl_ref[pl.ds(COMMON_ROWS, tail_rows), :] = jnp.zeros((tail_rows, XL_COLS), BF16)

    return permute, permute_tail


def _max_tiles(nch):
    max_blocks = nch * (2 * CHUNK + N_EXPERTS * (ROW_BLOCK - 1)) // ROW_BLOCK
    return max_blocks // TILE_BLOCKS + N_EXPERTS


def _masked_prefix(le, values):
    delta = values - jnp.concatenate([jnp.zeros((1,), values.dtype), values[:-1]])
    return jnp.sum(jnp.where(le, delta[None, :], 0), axis=1)


def _tile_plan(nb, nch):
    n_tiles = _max_tiles(nch)
    nbt = nb.T
    nbe = jnp.sum(nbt, axis=1)
    nte = (nbe + TILE_BLOCKS - 1) // TILE_BLOCKS
    tile_end = jnp.cumsum(nte)
    tile_start = tile_end - nte
    n_act = tile_end[-1]
    tiles = jnp.arange(n_tiles, dtype=I32)
    te = jnp.sum((tile_end[None, :] <= tiles[:, None]).astype(I32), axis=1)
    e_ar = jnp.arange(N_EXPERTS, dtype=I32)
    active = nte > 0
    te = jnp.where(tiles < n_act, te, jnp.max(jnp.where(active, e_ar, 0)))
    first = jnp.logical_and(te != jnp.concatenate([jnp.full((1,), -1, I32), te[:-1]]),
                            tiles < n_act).astype(I32)
    rank = jnp.cumsum(active.astype(I32)) - 1
    k_ar = jnp.arange(N_EXPERTS + W_SLOTS, dtype=I32)
    eseq = jnp.sum(jnp.where(jnp.logical_and(active[None, :], rank[None, :] == k_ar[:, None]),
                             e_ar[None, :], 0), axis=1)
    n_exp = jnp.sum(active.astype(I32))

    cb_excl = jnp.cumsum(nbt, axis=1) - nbt
    gs = TILE_BLOCKS * tile_start[:, None] + cb_excl
    seg_blk = jnp.cumsum(nb, axis=1) - nb
    base_blk = jnp.arange(nch, dtype=I32)[None, :] * CHUNK_BLOCKS + seg_blk.T
    gs_f = gs.reshape(-1)
    slots = jnp.arange(n_tiles * TILE_BLOCKS, dtype=I32)
    le = gs_f[None, :] <= slots[:, None]
    blk = slots + _masked_prefix(le, (base_blk - gs).reshape(-1))
    valid = slots < _masked_prefix(le, (gs + nbt).reshape(-1))
    gblk = jnp.concatenate([jnp.where(valid, blk, 0).astype(I32),
                            jnp.zeros(((X_SLOTS - 1) * TILE_BLOCKS,), I32)])

    m = jnp.arange(CHUNK_BLOCKS, dtype=I32)
    le_c = seg_blk[:, None, :] <= m[None, :, None]
    shift = (gs.T - seg_blk)
    delta = shift - jnp.concatenate([jnp.zeros((nch, 1), I32), shift[:, :-1]], axis=1)
    slot_of = m[None, :] + jnp.sum(jnp.where(le_c, delta[:, None, :], 0), axis=2)
    used_blocks = jnp.sum(nb, axis=1)
    slot_of = jnp.where(m[None, :] < used_blocks[:, None], slot_of, 0).astype(I32)
    pad = Y_RING - 1
    slot_of = jnp.concatenate([slot_of.reshape(-1), jnp.zeros((pad * CHUNK_BLOCKS,), I32)])
    used_rows = jnp.concatenate([ROW_BLOCK * used_blocks, jnp.zeros((pad,), I32)]).astype(I32)
    return ((n_act.reshape(1).astype(I32), n_exp.reshape(1), first, eseq, gblk),
            (used_rows, slot_of))


def _gmm_kernel(nact_ref, nexp_ref, first_ref, eseq_ref, gblk_ref,
                xl_hbm, wu_hbm, wd_hbm, yt_hbm,
                xbuf, ybuf, zbuf, wu_st, wd_st, in_sem, out_sem, w_sem, z_sem):
    n_act = nact_ref[0]
    n_exp = nexp_ref[0]
    n_tiles = yt_hbm.shape[0] // TILE_BLOCKS
    prefetch = X_SLOTS - 1

    def gather(tile, b):
        s = tile % X_SLOTS
        blk = gblk_ref[tile * TILE_BLOCKS + b]
        return pltpu.make_async_copy(xl_hbm.at[blk], xbuf.at[s, b], in_sem.at[s])

    def write_back(tile):
        s = tile % Y_SLOTS
        return pltpu.make_async_copy(ybuf.at[s], yt_hbm.at[pl.ds(tile * TILE_BLOCKS, TILE_BLOCKS)],
                                     out_sem.at[s])

    def zero_fill(tile):
        return pltpu.make_async_copy(zbuf, yt_hbm.at[pl.ds(tile * TILE_BLOCKS, TILE_BLOCKS)], z_sem)

    def weights(q):
        e = eseq_ref[q]
        s = q % W_SLOTS
        return (pltpu.make_async_copy(wu_hbm.at[e], wu_st.at[s], w_sem.at[0, s]),
                pltpu.make_async_copy(wd_hbm.at[e], wd_st.at[s], w_sem.at[1, s]))

    def tile_blocks(fn):
        for b in range(TILE_BLOCKS):
            fn(b)

    for q0 in range(W_SLOTS - 1):
        @pl.when(q0 < n_exp)
        def _(q0=q0):
            for cp in weights(q0):
                cp.start()
    for t0 in range(prefetch):
        tile_blocks(lambda b, t0=t0: gather(t0, b).start())

    def body(i, q):
        tile_blocks(lambda b: gather(i, b).wait())

        @pl.when(i >= Y_SLOTS)
        def _():
            write_back(i - Y_SLOTS).wait()

        is_first = first_ref[i] == 1

        @pl.when(is_first)
        def _():
            for cp in weights(q):
                cp.wait()

            @pl.when(q + W_SLOTS - 1 < n_exp)
            def _():
                for cp in weights(q + W_SLOTS - 1):
                    cp.start()

        q = q + is_first.astype(I32)
        ws = (q - 1) % W_SLOTS
        xs = i % X_SLOTS
        ys = i % Y_SLOTS

        x = xbuf[xs].reshape(TILE_ROWS, XL_COLS)
        gate = jnp.sum(x[:, D_MODEL:].astype(F32), axis=1, keepdims=True)
        au = jnp.dot(x[:, :D_MODEL], wu_st[ws], preferred_element_type=F32)
        tile_blocks(lambda b: gather(i + prefetch, b).start())
        a = au[:, :EXPERT_FF]
        act = (a * _sigmoid(a)) * au[:, EXPERT_FF:]
        y = jnp.dot(act.astype(BF16), wd_st[ws], preferred_element_type=F32)
        ybuf[ys] = (gate * y).astype(BF16).reshape(TILE_BLOCKS, ROW_BLOCK, D_MODEL)
        write_back(i).start()

        @pl.when(n_act + i < n_tiles)
        def _():
            zero_fill(n_act + i).start()
        return q

    zbuf[...] = jnp.zeros_like(zbuf)
    lax.fori_loop(0, n_act, body, jnp.int32(0))

    def fill_rest(t, carry):
        zero_fill(t).start()
        return carry

    lax.fori_loop(jnp.minimum(2 * n_act, n_tiles), n_tiles, fill_rest, 0)

    def drain(t, carry):
        zero_fill(t).wait()
        return carry

    lax.fori_loop(n_act, n_tiles, drain, 0)
    for k in range(prefetch):
        tile_blocks(lambda b, k=k: gather(n_act + k, b).wait())
    for k in range(Y_SLOTS):
        @pl.when(n_act - 1 - k >= 0)
        def _(k=k):
            write_back(n_act - 1 - k).wait()


def _grouped_mlp(plan, xl, w_up, w_down, n_tiles):
    grid_spec = pltpu.PrefetchScalarGridSpec(
        num_scalar_prefetch=len(plan),
        grid=(1,),
        in_specs=[pl.BlockSpec(memory_space=pl.ANY)] * 3,
        out_specs=pl.BlockSpec(memory_space=pl.ANY),
        scratch_shapes=[pltpu.VMEM((X_SLOTS, TILE_BLOCKS, ROW_BLOCK, XL_COLS), BF16),
                        pltpu.VMEM((Y_SLOTS, TILE_BLOCKS, ROW_BLOCK, D_MODEL), BF16),
                        pltpu.VMEM((TILE_BLOCKS, ROW_BLOCK, D_MODEL), BF16),
                        pltpu.VMEM((W_SLOTS, D_MODEL, 2 * EXPERT_FF), BF16),
                        pltpu.VMEM((W_SLOTS, EXPERT_FF, D_MODEL), BF16),
                        pltpu.SemaphoreType.DMA((X_SLOTS,)),
                        pltpu.SemaphoreType.DMA((Y_SLOTS,)),
                        pltpu.SemaphoreType.DMA((2, W_SLOTS)),
                        pltpu.SemaphoreType.DMA(())])
    return pl.pallas_call(
        _gmm_kernel,
        grid_spec=grid_spec,
        out_shape=jax.ShapeDtypeStruct((n_tiles * TILE_BLOCKS, ROW_BLOCK, D_MODEL), BF16),
        compiler_params=_cparams(("arbitrary",)),
        name="moe_experts",
    )(*plan, xl, w_up, w_down)


def _combine_kernel(used_ref, slot_ref, yt_hbm, pos_ref, x1_ref, gt2_ref, fg_ref, o_ref, ybuf, sem):
    c = pl.program_id(0)
    n_chunks = pl.num_programs(0)
    used_rows = used_ref[c]

    def chunk_blocks(chunk, op):
        s = chunk % Y_RING

        def copy(b):
            slot = slot_ref[chunk * CHUNK_BLOCKS + b]
            return pltpu.make_async_copy(yt_hbm.at[slot], ybuf.at[s, b], sem.at[s])

        for b in range(COMMON_BLOCKS):
            op(copy(b))

        @pl.when(used_ref[chunk] > COMMON_ROWS)
        def _():
            for b in range(COMMON_BLOCKS, CHUNK_BLOCKS):
                op(copy(b))

    @pl.when(c == 0)
    def _():
        for k in range(Y_RING - 1):
            chunk_blocks(k, lambda cp: cp.start())

    chunk_blocks(c + Y_RING - 1, lambda cp: cp.start())
    chunk_blocks(c, lambda cp: cp.wait())

    def body(k_rows):
        pos = pos_ref[...]
        p1 = pos[:, 0:1].astype(I32)
        p2 = pos[:, 1:2].astype(I32)
        r = lax.broadcasted_iota(I32, (CHUNK, k_rows), 1)
        sel = jnp.where(r == p1, 1.0, jnp.where(r == p2, 1.0, 0.0)).astype(BF16)
        yl = ybuf[c % Y_RING, 0:k_rows // ROW_BLOCK].reshape(k_rows, D_MODEL)
        y = jnp.dot(sel, yl, preferred_element_type=F32)
        x2 = x1_ref[...] + gt2_ref[0] * y
        inv = lax.rsqrt(jnp.mean(x2 * x2, axis=-1, keepdims=True) + NORM_EPS)
        o_ref[...] = (x2 * inv) * fg_ref[...]

    pl.when(used_rows <= COMMON_ROWS)(lambda: body(COMMON_ROWS))
    pl.when(used_rows > COMMON_ROWS)(lambda: body(CHUNK_ROWS))

    @pl.when(c == n_chunks - 1)
    def _():
        for k in range(1, Y_RING):
            chunk_blocks(c + k, lambda cp: cp.wait())


def _combine(used_rows, slot_of, yt, pos_c, x1, mod3, final_g, seq):
    n = x1.shape[0]
    per_seq = seq // CHUNK
    grid_spec = pltpu.PrefetchScalarGridSpec(
        num_scalar_prefetch=2,
        grid=(n // CHUNK,),
        in_specs=[pl.BlockSpec(memory_space=pl.ANY),
                  pl.BlockSpec((CHUNK, LANES), lambda c, u, s: (c, 0)),
                  pl.BlockSpec((CHUNK, D_MODEL), lambda c, u, s: (c, 0)),
                  pl.BlockSpec((1, 1, D_MODEL), lambda c, u, s: (c // per_seq, 0, MOD_GATE2)),
                  pl.BlockSpec((1, D_MODEL), lambda c, u, s: (0, 0))],
        out_specs=pl.BlockSpec((CHUNK, D_MODEL), lambda c, u, s: (c, 0)),
        scratch_shapes=[pltpu.VMEM((Y_RING, CHUNK_BLOCKS, ROW_BLOCK, D_MODEL), BF16),
                        pltpu.SemaphoreType.DMA((Y_RING,))])
    return pl.pallas_call(
        _combine_kernel,
        grid_spec=grid_spec,
        out_shape=jax.ShapeDtypeStruct((n, D_MODEL), F32),
        compiler_params=_cparams(("arbitrary",)),
        name="moe_combine",
    )(used_rows, slot_of, yt, pos_c, x1, mod3, final_g)


def _rope_tables(seq):
    n_freq = HEAD_DIM // 4
    inv_freq = ROPE_BASE ** (-jnp.arange(n_freq, dtype=F32) / n_freq)
    rows = seq // GRID_W
    row = jnp.repeat(jnp.arange(rows, dtype=F32), GRID_W)
    col = jnp.tile(jnp.arange(GRID_W, dtype=F32), rows)
    ang = jnp.concatenate([row[:, None] * inv_freq, col[:, None] * inv_freq], axis=-1)
    cos, sin = jnp.cos(ang), jnp.sin(ang)
    return jnp.tile(cos, (1, 4)), jnp.concatenate([-sin, sin, -sin, sin], axis=1)


def _window_bias():
    key = np.arange(3 * BLOCK)[:, None]
    qry = np.arange(BLOCK)[None, :]
    valid = (key - qry >= 0) & (key - qry <= 2 * BLOCK)
    return jnp.asarray(np.where(valid, 0.0, NEG_INF), F32)


def kernel(x, c, ctx, c_ctx, w_ada, b_ada, norm1_g, w_in, w_conv, b_conv, w_a, w_b, sink, w_o,
           norm2_g, w_group, b_group, w_router, b_router, w_up, w_down, final_g):
    bsz, seq, _ = x.shape
    assert w_ada.shape[0] == 1 and seq % SEQ_TILE == 0

    cc = jnp.zeros((16, D_MODEL), F32).at[:bsz].set(c).at[bsz].set(c_ctx)
    mod = _modulation(cc, w_ada[0], b_ada[0][None, :])
    mod3 = mod[:bsz].reshape(bsz, 1, N_MOD * D_MODEL)
    csh1 = mod[bsz:bsz + 1, 0:D_MODEL]
    csc1 = mod[bsz:bsz + 1, D_MODEL:2 * D_MODEL]

    w = w_in[0]
    n1g = norm1_g[0][None, :]
    n2g = norm2_g[0][None, :]
    w_kv = w[:, OFF_K:OFF_GA].astype(BF16)
    w_a_cols = jnp.concatenate([w[:, OFF_CG:OFF_Q], w[:, OFF_K:OFF_GA]], axis=1).astype(BF16)
    w_b_cols = jnp.concatenate([w[:, OFF_B:OFF_CG], w[:, OFF_Q:OFF_K], w[:, OFF_GA:]],
                               axis=1).astype(BF16)
    cos_t, sin_t = _rope_tables(seq)

    k_ctx, vt_ctx = _context_kv(ctx, n1g, csh1, csc1, w_kv)
    sink_row = jnp.repeat(sink[0].astype(F32), BLOCK)[None, :]
    pad_g = jnp.zeros((D_MODEL, GROUP_ROWS - N_GROUPS), F32)
    pad_e = jnp.zeros((D_MODEL, ROUTER_ROWS - GROUP_ROWS - N_EXPERTS), F32)
    wr_t = jnp.concatenate([w_group[0], pad_g, w_router[0], pad_e], axis=1).T.astype(BF16)
    br_col = jnp.concatenate([b_group[0], pad_g[0], b_router[0], pad_e[0]])[:, None]

    upper = jnp.asarray(np.triu(np.ones((CHUNK, CHUNK), np.float32), 1), BF16)
    lower = jnp.asarray(np.tril(np.ones((N_EXPERTS, N_EXPERTS), np.float32), -1), BF16)
    x1, xl, pos_c, nb, w_up_bf, w_down_bf = _mix(
        x, n1g, n2g, mod3, w_a_cols, w_b_cols, cos_t, sin_t, k_ctx, vt_ctx,
        w_conv[0], b_conv[0][None, :], w_a[0].astype(BF16), w_b[0].astype(BF16),
        w_o[0].astype(BF16), sink_row, _window_bias(), wr_t, br_col, w_up[0], w_down[0],
        upper, lower)

    n = bsz * seq
    nch = n // CHUNK
    nb = nb[:, :, 0]
    expert_plan, combine_plan = _tile_plan(nb, nch)
    yt = _grouped_mlp(expert_plan, xl.reshape(-1, ROW_BLOCK, XL_COLS), w_up_bf, w_down_bf,
                      _max_tiles(nch))
    out = _combine(*combine_plan, yt, pos_c, x1.reshape(n, D_MODEL), mod3, final_g[None, :], seq)
    return out.reshape(bsz, seq, D_MODEL)
```

```python
import functools

import numpy as np
import jax
import jax.numpy as jnp
from jax import lax
from jax.experimental import pallas as pl
from jax.experimental.pallas import tpu as pltpu

F32 = jnp.float32
BF16 = jnp.bfloat16
I32 = jnp.int32

D_MODEL = 1024
GRID_W = 64
CONV_W = 512
N_HEADS = 8
N_KV_HEADS = 2
HEAD_DIM = 64
ATT_W = N_HEADS * HEAD_DIM
KV_W = N_KV_HEADS * HEAD_DIM
BLOCK = 128
ROPE_BASE = 10000.0
N_GROUPS = 4
EXPERTS_PER_GROUP = 8
N_EXPERTS = N_GROUPS * EXPERTS_PER_GROUP
EXPERT_FF = 256
N_MOD = 6
MOD_SHIFT1, MOD_SCALE1, MOD_GATE1, MOD_SHIFT2, MOD_SCALE2, MOD_GATE2 = range(N_MOD)
NORM_EPS = 1e-6
NEG_INF = -1e30
LOG2E = 1.4426950408889634
ONES_ROWS = 16

OFF_B, OFF_CG, OFF_XIN, OFF_Q, OFF_K, OFF_V, OFF_GA, OFF_GB = (
    0, 512, 1024, 1536, 2048, 2176, 2304, 3328)
IN_COLS = 4352

LANES = 128
SEQ_TILE = 512
BLOCKS_PER_TILE = SEQ_TILE // BLOCK
ROUTE_ROWS = 8
GROUP_ROWS = 8
ROUTER_ROWS = 48
VMEM_LIMIT = 62 * 1024 * 1024

CHUNK = SEQ_TILE
ROW_BLOCK = 16
CHUNK_ROWS = -(-(2 * CHUNK + N_EXPERTS * (ROW_BLOCK - 1)) // 256) * 256
CHUNK_BLOCKS = CHUNK_ROWS // ROW_BLOCK
GATE_COLS = LANES
XL_COLS = D_MODEL + GATE_COLS
COMMON_ROWS = 2 * CHUNK + 256
COMMON_BLOCKS = COMMON_ROWS // ROW_BLOCK
TILE_BLOCKS = 32
TILE_ROWS = TILE_BLOCKS * ROW_BLOCK
X_SLOTS = 6
Y_SLOTS = 4
W_SLOTS = 3
Y_RING = 3
RING = 3


def _cparams(sem):
    return pltpu.CompilerParams(dimension_semantics=sem, vmem_limit_bytes=VMEM_LIMIT)


def _rms_mod(x, g, shift, scale):
    inv = lax.rsqrt(jnp.mean(x * x, axis=-1, keepdims=True) + NORM_EPS)
    return (x * inv) * (g * (1.0 + scale)) + shift


def _mod_kernel(c_ref, w_ref, b_ref, o_ref):
    c = c_ref[...]
    a = (c * jax.nn.sigmoid(c)).astype(BF16)
    o_ref[...] = jnp.dot(a, w_ref[...].astype(BF16), preferred_element_type=F32) + b_ref[...]


def _modulation(cc, w_ada, b_ada):
    rows = cc.shape[0]
    cols = w_ada.shape[1]
    tile = 1024
    return pl.pallas_call(
        _mod_kernel,
        grid=(cols // tile,),
        in_specs=[pl.BlockSpec((rows, D_MODEL), lambda j: (0, 0)),
                  pl.BlockSpec((D_MODEL, tile), lambda j: (0, j)),
                  pl.BlockSpec((1, tile), lambda j: (0, j))],
        out_specs=pl.BlockSpec((rows, tile), lambda j: (0, j)),
        out_shape=jax.ShapeDtypeStruct((rows, cols), F32),
        compiler_params=_cparams(("arbitrary",)),
        name="adaln_mod",
    )(cc, w_ada, b_ada)


def _ctx_kernel(x_ref, g_ref, sh_ref, sc_ref, w_ref, k_ref, vt_ref):
    h = _rms_mod(x_ref[0], g_ref[...], sh_ref[...], sc_ref[...]).astype(BF16)
    kv = jnp.dot(h, w_ref[...], preferred_element_type=F32)
    k_ref[0] = kv[:, :KV_W].astype(BF16)
    vt_ref[0] = kv[:, KV_W:].T.astype(BF16)


def _context_kv(ctx, norm_g, csh, csc, w_kv):
    bsz, clen, _ = ctx.shape
    return pl.pallas_call(
        _ctx_kernel,
        grid=(bsz,),
        in_specs=[pl.BlockSpec((1, clen, D_MODEL), lambda b: (b, 0, 0)),
                  pl.BlockSpec((1, D_MODEL), lambda b: (0, 0)),
                  pl.BlockSpec((1, D_MODEL), lambda b: (0, 0)),
                  pl.BlockSpec((1, D_MODEL), lambda b: (0, 0)),
                  pl.BlockSpec((D_MODEL, 2 * KV_W), lambda b: (0, 0))],
        out_specs=[pl.BlockSpec((1, clen, KV_W), lambda b: (b, 0, 0)),
                   pl.BlockSpec((1, KV_W, clen), lambda b: (b, 0, 0))],
        out_shape=[jax.ShapeDtypeStruct((bsz, clen, KV_W), BF16),
                   jax.ShapeDtypeStruct((bsz, KV_W, clen), BF16)],
        compiler_params=_cparams(("arbitrary",)),
        name="context_kv",
    )(ctx, norm_g, csh, csc, w_kv)


def _rope(t, cos, sin_signed):
    lane = lax.broadcasted_iota(I32, (1, LANES), 1)
    first_half = (lane % HEAD_DIM) < (HEAD_DIM // 2)
    outs = []
    for j in range(t.shape[1] // LANES):
        tj = t[:, j * LANES:(j + 1) * LANES]
        partner = jnp.where(first_half,
                            pltpu.roll(tj, LANES - HEAD_DIM // 2, axis=1),
                            pltpu.roll(tj, HEAD_DIM // 2, axis=1))
        outs.append(tj * cos + partner * sin_signed)
    return outs[0] if len(outs) == 1 else jnp.concatenate(outs, axis=1)


def _sigmoid(x):
    return 1.0 / (1.0 + jnp.exp2(x * (-LOG2E)))


def _attn_scores(q_t, g, k_win, k_ctx, bias_p4, bias_n4):
    zeros = jnp.zeros((HEAD_DIM, 4 * BLOCK), BF16)
    qg = jnp.concatenate([q_t[(4 * g + h) * HEAD_DIM:(4 * g + h + 1) * HEAD_DIM, :]
                          for h in range(4)], axis=1)
    rhs = jnp.concatenate([qg, zeros] if g == 0 else [zeros, qg], axis=0)
    s = jnp.dot(jnp.concatenate([k_win, k_ctx], axis=0), rhs, preferred_element_type=F32)
    return (s[0:BLOCK] + bias_p4, s[BLOCK:2 * BLOCK], s[2 * BLOCK:3 * BLOCK] + bias_n4,
            s[3 * BLOCK:])


def _attn_probs(scores, sink):
    m = sink
    for s in scores:
        m = jnp.maximum(m, jnp.max(s, axis=0, keepdims=True))
    p_win = jnp.concatenate([jnp.exp2(s - m).astype(BF16) for s in scores[:3]], axis=0)
    p_ctx = jnp.exp2(scores[3] - m).astype(BF16)
    return p_win, p_ctx, m


def _attn_values(probs, sink, vt_win_g, vt_ctx_g):
    p_win, p_ctx, m = probs

    def with_ones(vt):
        r = lax.broadcasted_iota(I32, (ONES_ROWS, vt.shape[1]), 0)
        return jnp.concatenate([vt, jnp.where(r == 0, 1.0, 0.0).astype(BF16)], axis=0)

    o_ext = (jnp.dot(with_ones(vt_win_g), p_win, preferred_element_type=F32)
             + jnp.dot(with_ones(vt_ctx_g), p_ctx, preferred_element_type=F32))
    denom = o_ext[HEAD_DIM:HEAD_DIM + 1, :] + jnp.exp2(sink - m)
    return o_ext[:HEAD_DIM, :] / denom


def _route(logits_t):
    t = logits_t.shape[1]
    grow = lax.broadcasted_iota(I32, (GROUP_ROWS, t), 0)
    gl = jnp.where(grow < N_GROUPS, logits_t[0:GROUP_ROWS, :], NEG_INF)
    gm = jnp.max(gl, axis=0, keepdims=True)
    p_g = 1.0 / jnp.sum(jnp.exp(gl - gm), axis=0, keepdims=True)
    g_idx = jnp.min(jnp.where(gl == gm, grow, N_GROUPS), axis=0, keepdims=True)

    erow = lax.broadcasted_iota(I32, (N_EXPERTS, t), 0)
    el = logits_t[GROUP_ROWS:GROUP_ROWS + N_EXPERTS, :]
    sel = (erow // EXPERTS_PER_GROUP) == g_idx
    em = jnp.where(sel, el, NEG_INF)
    m1 = jnp.max(em, axis=0, keepdims=True)
    i1 = jnp.min(jnp.where(em == m1, erow, N_EXPERTS), axis=0, keepdims=True)
    em2 = jnp.where(erow == i1, NEG_INF, em)
    m2 = jnp.max(em2, axis=0, keepdims=True)
    i2 = jnp.min(jnp.where(em2 == m2, erow, N_EXPERTS), axis=0, keepdims=True)
    z = jnp.sum(jnp.where(sel, jnp.exp(el - m1), 0.0), axis=0, keepdims=True)
    p1 = 1.0 / z
    p2 = jnp.exp(m2 - m1) / z
    gate1 = p_g * p1 / (p1 + p2)
    gate2 = p_g * p2 / (p1 + p2)
    pad = jnp.zeros((ROUTE_ROWS - 4, t), F32)
    return jnp.concatenate([i1.astype(F32), i2.astype(F32), gate1, gate2, pad], axis=0)


def _project_a(x_ref, g_ref, sh_ref, sc_ref, w_ref, cos_ref, sin_ref, slot, uc_slot,
               uc_ring, k_ring, vt_ring):
    h = _rms_mod(x_ref[0], g_ref[...], sh_ref[0], sc_ref[0]).astype(BF16)
    u = jnp.dot(h, w_ref[...], preferred_element_type=F32)
    uc_ring[uc_slot] = u[:, :CONV_W] * u[:, CONV_W:2 * CONV_W]
    k = u[:, 2 * CONV_W:2 * CONV_W + KV_W]
    k_ring[slot] = _rope(k, cos_ref[...], sin_ref[...]).astype(BF16)
    vt_ring[slot] = u[:, 2 * CONV_W + KV_W:].T.astype(BF16)


def _mix_step(x_ref, g1n_ref, g2n_ref, sh1_ref, sc1_ref, gt1_ref, sh2_ref, sc2_ref,
              wb_ref, cos_ref, sin_ref,
              xn_ref, shn_ref, scn_ref, cosn_ref, sinn_ref, wac_ref,
              kc_ref, vtc_ref, wconv_ref, bconv_ref, wa_ref, wbb_ref, wo_ref,
              sink_ref, bias_ref, wr_ref, br_ref, wu_ref, wd_ref, upper_ref, lower_ref,
              x1_ref, xl_ref, pos_ref, nb_ref, wub_ref, wdb_ref,
              h2_scr, route_scr, uc_ring, uc_last, k_ring, vt_ring,
              *, n_seq_tiles):
    step = pl.program_id(0)
    t_idx = step % n_seq_tiles
    is_first = t_idx == 0
    is_last = t_idx == n_seq_tiles - 1
    own, prv, nxt = step % RING, (step + RING - 1) % RING, (step + 1) % RING

    permute, permute_tail = _sort_chunk(h2_scr[...], route_scr[...], upper_ref, lower_ref,
                                        xl_ref, pos_ref, nb_ref)

    @pl.when(step == 0)
    def _():
        uc_ring[...] = jnp.zeros_like(uc_ring)
        uc_last[...] = jnp.zeros_like(uc_last)
        k_ring[...] = jnp.zeros_like(k_ring)
        vt_ring[...] = jnp.zeros_like(vt_ring)
        _project_a(x_ref, g1n_ref, sh1_ref, sc1_ref, wac_ref, cos_ref, sin_ref, 0, 0,
                   uc_ring, k_ring, vt_ring)

    x = x_ref[0]
    hb = _rms_mod(x, g1n_ref[...], sh1_ref[0], sc1_ref[0]).astype(BF16)
    bq = jnp.dot(hb, wb_ref[:, 0:2 * CONV_W], preferred_element_type=F32)
    q = _rope(bq[:, CONV_W:], cos_ref[...], sin_ref[...]) * (HEAD_DIM ** -0.5 * LOG2E)

    def block_ring(ring, slot_, blk, axis):
        lo = blk * BLOCK
        return ring[slot_, lo:lo + BLOCK, :] if axis == 0 else ring[slot_, :, lo:lo + BLOCK]

    def window(ring, jb, axis):
        parts = []
        for blk in (jb - 1, jb, jb + 1):
            if blk < 0:
                parts.append(block_ring(ring, prv, BLOCKS_PER_TILE - 1, axis))
            elif blk >= BLOCKS_PER_TILE:
                parts.append(block_ring(ring, nxt, 0, axis))
            else:
                parts.append(block_ring(ring, own, blk, axis))
        return jnp.concatenate(parts, axis=axis)

    k_ctx = kc_ref[0]
    vt_ctx = vtc_ref[0]
    bias_prev = bias_ref[0:BLOCK, :]
    bias_next = bias_ref[2 * BLOCK:3 * BLOCK, :]
    sink_row = sink_ref[...] * LOG2E
    n_units = BLOCKS_PER_TILE * N_KV_HEADS
    gate_cols = 2 * D_MODEL // n_units
    gate_chunks = []

    def gate_chunk(u):
        c0 = 2 * CONV_W + u * gate_cols
        gate_chunks.append(jnp.dot(hb, wb_ref[:, c0:c0 + gate_cols], preferred_element_type=F32))

    q_ts, biases = [], []
    for jb in range(BLOCKS_PER_TILE):
        bias_p, bias_n = bias_prev, bias_next
        if jb == 0:
            bias_p = bias_prev + jnp.where(is_first, NEG_INF, 0.0)
        if jb == BLOCKS_PER_TILE - 1:
            bias_n = bias_next + jnp.where(is_last, NEG_INF, 0.0)
        biases.append((jnp.concatenate([bias_p] * 4, axis=1), jnp.concatenate([bias_n] * 4, axis=1)))
        q_ts.append(q[jb * BLOCK:(jb + 1) * BLOCK, :].T.astype(BF16))

    scores, probs, outs = {}, {}, {}
    for t in range(n_units + 2):
        if t == 3:
            wub_ref[...] = wu_ref[...].astype(BF16)
        if t == 5:
            wdb_ref[...] = wd_ref[...].astype(BF16)
        if t == 0:
            permute(0, COMMON_ROWS // 2)
        if t == 4:
            permute(COMMON_ROWS // 2, COMMON_ROWS // 2)
        if t == 2:
            _project_a(xn_ref, g1n_ref, shn_ref, scn_ref, wac_ref, cosn_ref, sinn_ref, nxt,
                       (step + 1) % 2, uc_ring, k_ring, vt_ring)
        if t < n_units:
            jb, g = divmod(t, N_KV_HEADS)
            scores[t] = _attn_scores(q_ts[jb], g, window(k_ring, jb, 0), k_ctx, *biases[jb])
            gate_chunk(t)
        if 0 <= t - 1 < n_units:
            g = (t - 1) % N_KV_HEADS
            probs[t - 1] = _attn_probs(scores.pop(t - 1),
                                       sink_row[:, g * 4 * BLOCK:(g + 1) * 4 * BLOCK])
        if 0 <= t - 2 < n_units:
            jb, g = divmod(t - 2, N_KV_HEADS)
            outs[t - 2] = _attn_values(
                probs.pop(t - 2), sink_row[:, g * 4 * BLOCK:(g + 1) * 4 * BLOCK],
                window(vt_ring, jb, 1)[g * HEAD_DIM:(g + 1) * HEAD_DIM, :],
                vt_ctx[g * HEAD_DIM:(g + 1) * HEAD_DIM, :])
    o_blocks = []
    for jb in range(BLOCKS_PER_TILE):
        o_rows = [outs[jb * N_KV_HEADS + g][:, h * BLOCK:(h + 1) * BLOCK]
                  for g in range(N_KV_HEADS) for h in range(4)]
        o_blocks.append(jnp.concatenate(o_rows, axis=0).T)
    gates = jnp.concatenate(gate_chunks, axis=1)

    uc = uc_ring[step % 2]
    row = lax.broadcasted_iota(I32, (SEQ_TILE, 1), 0)
    prev_row = jnp.where(is_first, 0.0, uc_last[0:1, :])
    next_row = jnp.where(is_last, 0.0, uc_ring[(step + 1) % 2, 0:1, :])
    uc_last[0:1, :] = uc[SEQ_TILE - 1:SEQ_TILE, :]
    up = jnp.where(row == 0, prev_row, pltpu.roll(uc, 1, axis=0))
    dn = jnp.where(row == SEQ_TILE - 1, next_row, pltpu.roll(uc, SEQ_TILE - 1, axis=0))
    wconv = wconv_ref[...]
    y = bconv_ref[...] + (up * wconv[0:1, :] + uc * wconv[1:2, :] + dn * wconv[2:3, :])
    ya = jnp.dot((bq[:, :CONV_W] * y).astype(BF16), wa_ref[...], preferred_element_type=F32)

    o = jnp.concatenate(o_blocks, axis=0).astype(BF16)
    yb = jnp.dot(o, wbb_ref[...], preferred_element_type=F32)
    merged = (_sigmoid(gates[:, :D_MODEL]) * ya + _sigmoid(gates[:, D_MODEL:]) * yb).astype(BF16)
    x1 = x + gt1_ref[0] * jnp.dot(merged, wo_ref[...], preferred_element_type=F32)
    x1_ref[0] = x1
    h2 = _rms_mod(x1, g2n_ref[...], sh2_ref[0], sc2_ref[0]).astype(BF16)
    logits_t = lax.dot_general(wr_ref[...], h2, (((1,), (1,)), ((), ())),
                               preferred_element_type=F32) + br_ref[...]
    permute_tail()
    h2_scr[...] = h2
    route_scr[...] = _route(logits_t)


N_MIX_OUTPUTS = 6
N_MIX_SCRATCH = 6


def _mix_kernel(*refs, n_seq_tiles, batch):
    n_tail = N_MIX_OUTPUTS + N_MIX_SCRATCH
    upper_ref, lower_ref = refs[-n_tail - 2:-n_tail]
    _, xl_ref, pos_ref, nb_ref, _, _ = refs[-n_tail:-N_MIX_SCRATCH]
    h2_scr, route_scr = refs[-N_MIX_SCRATCH:-N_MIX_SCRATCH + 2]
    step = pl.program_id(0)
    n_main = n_seq_tiles * batch

    @pl.when(step == 0)
    def _():
        h2_scr[...] = jnp.zeros_like(h2_scr)
        route_scr[...] = jnp.zeros_like(route_scr)

    @pl.when(step < n_main)
    def _():
        _mix_step(*refs, n_seq_tiles=n_seq_tiles)

    @pl.when(step == n_main)
    def _():
        permute, permute_tail = _sort_chunk(h2_scr[...], route_scr[...], upper_ref, lower_ref,
                                            xl_ref, pos_ref, nb_ref)
        permute(0, COMMON_ROWS)
        permute_tail()


def _mix(x, norm1_g, norm2_g, mod3, w_a_cols, w_b_cols, cos_t, sin_t, k_ctx, vt_ctx,
         w_conv, b_conv, w_a, w_b, w_o, sink_row, bias, wr_t, br_col, w_up, w_down, upper, lower):
    bsz, seq, _ = x.shape
    nt = seq // SEQ_TILE
    clen = k_ctx.shape[1]
    nch = n_main = bsz * nt
    assert N_EXPERTS % n_main == 0, "each grid step converts an equal share of the experts"
    e_step = N_EXPERTS // n_main

    def tile(s):
        return divmod(jnp.clip(s, 0, n_main - 1), nt)

    def per_tile(shape, index, shift=0):
        return pl.BlockSpec(shape, lambda s: index(*tile(s + shift)))

    def const_spec(shape):
        return pl.BlockSpec(shape, lambda s: tuple(0 for _ in shape),
                            pipeline_mode=pl.Buffered(1))

    def x_spec(shift=0):
        return per_tile((1, SEQ_TILE, D_MODEL), lambda b, t: (b, t, 0), shift)

    def mod_spec(j, shift=0):
        return per_tile((1, 1, D_MODEL), lambda b, t: (b, 0, j), shift)

    def rope_spec(shift=0):
        return per_tile((SEQ_TILE, LANES), lambda b, t: (t, 0), shift)

    def expert_spec(w):
        return pl.BlockSpec((e_step,) + w.shape[1:], lambda s: (jnp.minimum(s, n_main - 1), 0, 0))

    def chunk_spec(shape):
        return pl.BlockSpec(shape, lambda s: (jnp.maximum(s - 1, 0),) + (0,) * (len(shape) - 1))

    in_specs = [
        x_spec(), const_spec((1, D_MODEL)), const_spec((1, D_MODEL)),
        mod_spec(MOD_SHIFT1), mod_spec(MOD_SCALE1), mod_spec(MOD_GATE1), mod_spec(MOD_SHIFT2),
        mod_spec(MOD_SCALE2),
        const_spec(w_b_cols.shape), rope_spec(), rope_spec(),
        x_spec(1), mod_spec(MOD_SHIFT1, 1), mod_spec(MOD_SCALE1, 1), rope_spec(1), rope_spec(1),
        const_spec(w_a_cols.shape),
        per_tile((1, clen, KV_W), lambda b, t: (b, 0, 0)),
        per_tile((1, KV_W, clen), lambda b, t: (b, 0, 0)),
        const_spec(w_conv.shape), const_spec(b_conv.shape),
        const_spec(w_a.shape), const_spec(w_b.shape), const_spec(w_o.shape),
        const_spec(sink_row.shape), const_spec(bias.shape),
        const_spec(wr_t.shape), const_spec(br_col.shape),
        expert_spec(w_up), expert_spec(w_down),
        const_spec(upper.shape), const_spec(lower.shape),
    ]
    out_specs = [
        x_spec(), chunk_spec((CHUNK_ROWS, XL_COLS)), chunk_spec((CHUNK, LANES)),
        chunk_spec((1, N_EXPERTS, LANES)), expert_spec(w_up), expert_spec(w_down),
    ]
    out_shape = [
        jax.ShapeDtypeStruct((bsz, seq, D_MODEL), F32),
        jax.ShapeDtypeStruct((nch * CHUNK_ROWS, XL_COLS), BF16),
        jax.ShapeDtypeStruct((nch * CHUNK, LANES), F32),
        jax.ShapeDtypeStruct((nch, N_EXPERTS, LANES), I32),
        jax.ShapeDtypeStruct(w_up.shape, BF16),
        jax.ShapeDtypeStruct(w_down.shape, BF16),
    ]
    return pl.pallas_call(
        functools.partial(_mix_kernel, n_seq_tiles=nt, batch=bsz),
        grid=(n_main + 1,),
        in_specs=in_specs,
        out_specs=out_specs,
        out_shape=out_shape,
        scratch_shapes=[pltpu.VMEM((CHUNK, D_MODEL), BF16), pltpu.VMEM((ROUTE_ROWS, CHUNK), F32),
                        pltpu.VMEM((2, SEQ_TILE, CONV_W), F32), pltpu.VMEM((8, CONV_W), F32),
                        pltpu.VMEM((RING, SEQ_TILE, KV_W), BF16),
                        pltpu.VMEM((RING, KV_W, SEQ_TILE), BF16)],
        compiler_params=_cparams(("arbitrary",)),
        name="token_mix",
    )(x, norm1_g, norm2_g, mod3, mod3, mod3, mod3, mod3, w_b_cols, cos_t, sin_t,
      x, mod3, mod3, cos_t, sin_t, w_a_cols, k_ctx, vt_ctx,
      w_conv, b_conv, w_a, w_b, w_o, sink_row, bias, wr_t, br_col, w_up, w_down, upper, lower)


def _bf16_parts(v):
    hi = v.astype(BF16).astype(F32)
    r1 = v - hi
    mid = r1.astype(BF16).astype(F32)
    lo = (r1 - mid).astype(BF16).astype(F32)
    return hi, mid, lo


def _sort_chunk(h, route, upper_ref, lower_ref, xl_ref, pos_ref, nb_ref):
    e1 = route[0:1, :].astype(I32)
    e2 = route[1:2, :].astype(I32)
    erow = lax.broadcasted_iota(I32, (N_EXPERTS, CHUNK), 0)
    hit1 = erow == e1
    hit2 = erow == e2
    onehot = jnp.where(hit1, 1.0, 0.0) + jnp.where(hit2, 1.0, 0.0)
    cum = jnp.dot(onehot.astype(BF16), upper_ref[...], preferred_element_type=F32)
    cnt = jnp.sum(onehot, axis=1, keepdims=True)
    nblk = jnp.floor((cnt + (ROW_BLOCK - 1)) * (1.0 / ROW_BLOCK))
    nblk_b = jnp.broadcast_to(nblk, (N_EXPERTS, LANES))
    seg = jnp.dot(lower_ref[...], nblk_b.astype(BF16), preferred_element_type=F32) * ROW_BLOCK
    base = seg[:, 0:1] + cum
    pos1 = jnp.sum(jnp.where(hit1, base, 0.0), axis=0, keepdims=True)
    pos2 = jnp.sum(jnp.where(hit2, base, 0.0), axis=0, keepdims=True)
    p1i = pos1.astype(I32)
    p2i = pos2.astype(I32)
    used_rows = (jnp.sum(nblk) * ROW_BLOCK).astype(I32)

    prow = lax.broadcasted_iota(I32, (LANES, CHUNK), 0)
    parts = _bf16_parts(route[2:3, :]) + _bf16_parts(route[3:4, :])
    gpart_rows = jnp.zeros((LANES, CHUNK), F32)
    for j, part in enumerate(parts):
        gpart_rows = jnp.where(prow == j, part, gpart_rows)
    h_ext = jnp.concatenate([h, gpart_rows.T.astype(BF16)], axis=1)
    glane = lax.broadcasted_iota(I32, (1, GATE_COLS), 1)
    one = jnp.ones((), BF16)
    zero = jnp.zeros((), BF16)
    def permute(row0, n_rows):
        r16 = lax.broadcasted_iota(I32, (n_rows, CHUNK), 0).astype(jnp.int16)
        q1 = (p1i - row0).astype(jnp.int16)
        q2 = (p2i - row0).astype(jnp.int16)
        sel1 = jnp.where(r16 == q1, one, zero)
        sel = jnp.where(r16 == q2, one, sel1)
        is_slot1 = jnp.max(sel1, axis=1, keepdims=True)
        rows = pl.ds(row0, n_rows)
        xg = jnp.dot(sel, h_ext, preferred_element_type=F32)
        xl_ref[rows, 0:D_MODEL] = xg[:, :D_MODEL].astype(BF16)
        g6 = xg[:, D_MODEL:]
        g3 = jnp.where(is_slot1.astype(F32) > 0.0, g6, pltpu.roll(g6, GATE_COLS - 3, axis=1))
        xl_ref[rows, D_MODEL:XL_COLS] = jnp.where(glane < 3, g3, 0.0).astype(BF16)

    pos_rows = jnp.where(prow == 0, pos1, jnp.where(prow == 1, pos2, 0.0))
    pos_ref[...] = pos_rows.T
    nb_ref[0] = nblk_b.astype(I32)

    def permute_tail():
        tail_rows = CHUNK_ROWS - COMMON_ROWS

        @pl.when(COMMON_ROWS < used_rows)
        def _():
            permute(COMMON_ROWS, tail_rows)

        @pl.when(COMMON_ROWS >= used_rows)
        def _():
            xl_ref[pl.ds(COMMON_ROWS, tail_rows), :] = jnp.zeros((tail_rows, XL_COLS), BF16)

    return permute, permute_tail


def _max_tiles(nch):
    max_blocks = nch * (2 * CHUNK + N_EXPERTS * (ROW_BLOCK - 1)) // ROW_BLOCK
    return max_blocks // TILE_BLOCKS + N_EXPERTS


def _masked_prefix(le, values):
    delta = values - jnp.concatenate([jnp.zeros((1,), values.dtype), values[:-1]])
    return jnp.sum(jnp.where(le, delta[None, :], 0), axis=1)


def _tile_plan(nb, nch):
    n_tiles = _max_tiles(nch)
    nbt = nb.T
    nbe = jnp.sum(nbt, axis=1)
    nte = (nbe + TILE_BLOCKS - 1) // TILE_BLOCKS
    tile_end = jnp.cumsum(nte)
    tile_start = tile_end - nte
    n_act = tile_end[-1]
    tiles = jnp.arange(n_tiles, dtype=I32)
    te = jnp.sum((tile_end[None, :] <= tiles[:, None]).astype(I32), axis=1)
    e_ar = jnp.arange(N_EXPERTS, dtype=I32)
    active = nte > 0
    te = jnp.where(tiles < n_act, te, jnp.max(jnp.where(active, e_ar, 0)))
    first = jnp.logical_and(te != jnp.concatenate([jnp.full((1,), -1, I32), te[:-1]]),
                            tiles < n_act).astype(I32)
    rank = jnp.cumsum(active.astype(I32)) - 1
    k_ar = jnp.arange(N_EXPERTS + W_SLOTS, dtype=I32)
    eseq = jnp.sum(jnp.where(jnp.logical_and(active[None, :], rank[None, :] == k_ar[:, None]),
                             e_ar[None, :], 0), axis=1)
    n_exp = jnp.sum(active.astype(I32))

    cb_excl = jnp.cumsum(nbt, axis=1) - nbt
    gs = TILE_BLOCKS * tile_start[:, None] + cb_excl
    seg_blk = jnp.cumsum(nb, axis=1) - nb
    base_blk = jnp.arange(nch, dtype=I32)[None, :] * CHUNK_BLOCKS + seg_blk.T
    gs_f = gs.reshape(-1)
    slots = jnp.arange(n_tiles * TILE_BLOCKS, dtype=I32)
    le = gs_f[None, :] <= slots[:, None]
    blk = slots + _masked_prefix(le, (base_blk - gs).reshape(-1))
    valid = slots < _masked_prefix(le, (gs + nbt).reshape(-1))
    gblk = jnp.concatenate([jnp.where(valid, blk, 0).astype(I32),
                            jnp.zeros(((X_SLOTS - 1) * TILE_BLOCKS,), I32)])

    m = jnp.arange(CHUNK_BLOCKS, dtype=I32)
    le_c = seg_blk[:, None, :] <= m[None, :, None]
    shift = (gs.T - seg_blk)
    delta = shift - jnp.concatenate([jnp.zeros((nch, 1), I32), shift[:, :-1]], axis=1)
    slot_of = m[None, :] + jnp.sum(jnp.where(le_c, delta[:, None, :], 0), axis=2)
    used_blocks = jnp.sum(nb, axis=1)
    slot_of = jnp.where(m[None, :] < used_blocks[:, None], slot_of, 0).astype(I32)
    pad = Y_RING - 1
    slot_of = jnp.concatenate([slot_of.reshape(-1), jnp.zeros((pad * CHUNK_BLOCKS,), I32)])
    used_rows = jnp.concatenate([ROW_BLOCK * used_blocks, jnp.zeros((pad,), I32)]).astype(I32)
    return ((n_act.reshape(1).astype(I32), n_exp.reshape(1), first, eseq, gblk),
            (used_rows, slot_of))


def _gmm_kernel(nact_ref, nexp_ref, first_ref, eseq_ref, gblk_ref,
                xl_hbm, wu_hbm, wd_hbm, yt_hbm,
                xbuf, ybuf, zbuf, wu_st, wd_st, in_sem, out_sem, w_sem, z_sem):
    n_act = nact_ref[0]
    n_exp = nexp_ref[0]
    n_tiles = yt_hbm.shape[0] // TILE_BLOCKS
    prefetch = X_SLOTS - 1

    def gather(tile, b):
        s = tile % X_SLOTS
        blk = gblk_ref[tile * TILE_BLOCKS + b]
        return pltpu.make_async_copy(xl_hbm.at[blk], xbuf.at[s, b], in_sem.at[s])

    def write_back(tile):
        s = tile % Y_SLOTS
        return pltpu.make_async_copy(ybuf.at[s], yt_hbm.at[pl.ds(tile * TILE_BLOCKS, TILE_BLOCKS)],
                                     out_sem.at[s])

    def zero_fill(tile):
        return pltpu.make_async_copy(zbuf, yt_hbm.at[pl.ds(tile * TILE_BLOCKS, TILE_BLOCKS)], z_sem)

    def weights(q):
        e = eseq_ref[q]
        s = q % W_SLOTS
        return (pltpu.make_async_copy(wu_hbm.at[e], wu_st.at[s], w_sem.at[0, s]),
                pltpu.make_async_copy(wd_hbm.at[e], wd_st.at[s], w_sem.at[1, s]))

    def tile_blocks(fn):
        for b in range(TILE_BLOCKS):
            fn(b)

    for q0 in range(W_SLOTS - 1):
        @pl.when(q0 < n_exp)
        def _(q0=q0):
            for cp in weights(q0):
                cp.start()
    for t0 in range(prefetch):
        tile_blocks(lambda b, t0=t0: gather(t0, b).start())

    def body(i, q):
        tile_blocks(lambda b: gather(i, b).wait())

        @pl.when(i >= Y_SLOTS)
        def _():
            write_back(i - Y_SLOTS).wait()

        is_first = first_ref[i] == 1

        @pl.when(is_first)
        def _():
            for cp in weights(q):
                cp.wait()

            @pl.when(q + W_SLOTS - 1 < n_exp)
            def _():
                for cp in weights(q + W_SLOTS - 1):
                    cp.start()

        q = q + is_first.astype(I32)
        ws = (q - 1) % W_SLOTS
        xs = i % X_SLOTS
        ys = i % Y_SLOTS

        x = xbuf[xs].reshape(TILE_ROWS, XL_COLS)
        gate = jnp.sum(x[:, D_MODEL:].astype(F32), axis=1, keepdims=True)
        au = jnp.dot(x[:, :D_MODEL], wu_st[ws], preferred_element_type=F32)
        tile_blocks(lambda b: gather(i + prefetch, b).start())
        a = au[:, :EXPERT_FF]
        act = (a * _sigmoid(a)) * au[:, EXPERT_FF:]
        y = jnp.dot(act.astype(BF16), wd_st[ws], preferred_element_type=F32)
        ybuf[ys] = (gate * y).astype(BF16).reshape(TILE_BLOCKS, ROW_BLOCK, D_MODEL)
        write_back(i).start()

        @pl.when(n_act + i < n_tiles)
        def _():
            zero_fill(n_act + i).start()
        return q

    zbuf[...] = jnp.zeros_like(zbuf)
    lax.fori_loop(0, n_act, body, jnp.int32(0))

    def fill_rest(t, carry):
        zero_fill(t).start()
        return carry

    lax.fori_loop(jnp.minimum(2 * n_act, n_tiles), n_tiles, fill_rest, 0)

    def drain(t, carry):
        zero_fill(t).wait()
        return carry

    lax.fori_loop(n_act, n_tiles, drain, 0)
    for k in range(prefetch):
        tile_blocks(lambda b, k=k: gather(n_act + k, b).wait())
    for k in range(Y_SLOTS):
        @pl.when(n_act - 1 - k >= 0)
        def _(k=k):
            write_back(n_act - 1 - k).wait()


def _grouped_mlp(plan, xl, w_up, w_down, n_tiles):
    grid_spec = pltpu.PrefetchScalarGridSpec(
        num_scalar_prefetch=len(plan),
        grid=(1,),
        in_specs=[pl.BlockSpec(memory_space=pl.ANY)] * 3,
        out_specs=pl.BlockSpec(memory_space=pl.ANY),
        scratch_shapes=[pltpu.VMEM((X_SLOTS, TILE_BLOCKS, ROW_BLOCK, XL_COLS), BF16),
                        pltpu.VMEM((Y_SLOTS, TILE_BLOCKS, ROW_BLOCK, D_MODEL), BF16),
                        pltpu.VMEM((TILE_BLOCKS, ROW_BLOCK, D_MODEL), BF16),
                        pltpu.VMEM((W_SLOTS, D_MODEL, 2 * EXPERT_FF), BF16),
                        pltpu.VMEM((W_SLOTS, EXPERT_FF, D_MODEL), BF16),
                        pltpu.SemaphoreType.DMA((X_SLOTS,)),
                        pltpu.SemaphoreType.DMA((Y_SLOTS,)),
                        pltpu.SemaphoreType.DMA((2, W_SLOTS)),
                        pltpu.SemaphoreType.DMA(())])
    return pl.pallas_call(
        _gmm_kernel,
        grid_spec=grid_spec,
        out_shape=jax.ShapeDtypeStruct((n_tiles * TILE_BLOCKS, ROW_BLOCK, D_MODEL), BF16),
        compiler_params=_cparams(("arbitrary",)),
        name="moe_experts",
    )(*plan, xl, w_up, w_down)


def _combine_kernel(used_ref, slot_ref, yt_hbm, pos_ref, x1_ref, gt2_ref, fg_ref, o_ref, ybuf, sem):
    c = pl.program_id(0)
    n_chunks = pl.num_programs(0)
    used_rows = used_ref[c]

    def chunk_blocks(chunk, op):
        s = chunk % Y_RING

        def copy(b):
            slot = slot_ref[chunk * CHUNK_BLOCKS + b]
            return pltpu.make_async_copy(yt_hbm.at[slot], ybuf.at[s, b], sem.at[s])

        for b in range(COMMON_BLOCKS):
            op(copy(b))

        @pl.when(used_ref[chunk] > COMMON_ROWS)
        def _():
            for b in range(COMMON_BLOCKS, CHUNK_BLOCKS):
                op(copy(b))

    @pl.when(c == 0)
    def _():
        for k in range(Y_RING - 1):
            chunk_blocks(k, lambda cp: cp.start())

    chunk_blocks(c + Y_RING - 1, lambda cp: cp.start())
    chunk_blocks(c, lambda cp: cp.wait())

    def body(k_rows):
        pos = pos_ref[...]
        p1 = pos[:, 0:1].astype(I32)
        p2 = pos[:, 1:2].astype(I32)
        r = lax.broadcasted_iota(I32, (CHUNK, k_rows), 1)
        sel = jnp.where(r == p1, 1.0, jnp.where(r == p2, 1.0, 0.0)).astype(BF16)
        yl = ybuf[c % Y_RING, 0:k_rows // ROW_BLOCK].reshape(k_rows, D_MODEL)
        y = jnp.dot(sel, yl, preferred_element_type=F32)
        x2 = x1_ref[...] + gt2_ref[0] * y
        inv = lax.rsqrt(jnp.mean(x2 * x2, axis=-1, keepdims=True) + NORM_EPS)
        o_ref[...] = (x2 * inv) * fg_ref[...]

    pl.when(used_rows <= COMMON_ROWS)(lambda: body(COMMON_ROWS))
    pl.when(used_rows > COMMON_ROWS)(lambda: body(CHUNK_ROWS))

    @pl.when(c == n_chunks - 1)
    def _():
        for k in range(1, Y_RING):
            chunk_blocks(c + k, lambda cp: cp.wait())


def _combine(used_rows, slot_of, yt, pos_c, x1, mod3, final_g, seq):
    n = x1.shape[0]
    per_seq = seq // CHUNK
    grid_spec = pltpu.PrefetchScalarGridSpec(
        num_scalar_prefetch=2,
        grid=(n // CHUNK,),
        in_specs=[pl.BlockSpec(memory_space=pl.ANY),
                  pl.BlockSpec((CHUNK, LANES), lambda c, u, s: (c, 0)),
                  pl.BlockSpec((CHUNK, D_MODEL), lambda c, u, s: (c, 0)),
                  pl.BlockSpec((1, 1, D_MODEL), lambda c, u, s: (c // per_seq, 0, MOD_GATE2)),
                  pl.BlockSpec((1, D_MODEL), lambda c, u, s: (0, 0))],
        out_specs=pl.BlockSpec((CHUNK, D_MODEL), lambda c, u, s: (c, 0)),
        scratch_shapes=[pltpu.VMEM((Y_RING, CHUNK_BLOCKS, ROW_BLOCK, D_MODEL), BF16),
                        pltpu.SemaphoreType.DMA((Y_RING,))])
    return pl.pallas_call(
        _combine_kernel,
        grid_spec=grid_spec,
        out_shape=jax.ShapeDtypeStruct((n, D_MODEL), F32),
        compiler_params=_cparams(("arbitrary",)),
        name="moe_combine",
    )(used_rows, slot_of, yt, pos_c, x1, mod3, final_g)


def _rope_tables(seq):
    n_freq = HEAD_DIM // 4
    inv_freq = ROPE_BASE ** (-jnp.arange(n_freq, dtype=F32) / n_freq)
    rows = seq // GRID_W
    row = jnp.repeat(jnp.arange(rows, dtype=F32), GRID_W)
    col = jnp.tile(jnp.arange(GRID_W, dtype=F32), rows)
    ang = jnp.concatenate([row[:, None] * inv_freq, col[:, None] * inv_freq], axis=-1)
    cos, sin = jnp.cos(ang), jnp.sin(ang)
    return jnp.tile(cos, (1, 4)), jnp.concatenate([-sin, sin, -sin, sin], axis=1)


def _window_bias():
    key = np.arange(3 * BLOCK)[:, None]
    qry = np.arange(BLOCK)[None, :]
    valid = (key - qry >= 0) & (key - qry <= 2 * BLOCK)
    return jnp.asarray(np.where(valid, 0.0, NEG_INF), F32)


def kernel(x, c, ctx, c_ctx, w_ada, b_ada, norm1_g, w_in, w_conv, b_conv, w_a, w_b, sink, w_o,
           norm2_g, w_group, b_group, w_router, b_router, w_up, w_down, final_g):
    bsz, seq, _ = x.shape
    assert w_ada.shape[0] == 1 and seq % SEQ_TILE == 0

    cc = jnp.zeros((16, D_MODEL), F32).at[:bsz].set(c).at[bsz].set(c_ctx)
    mod = _modulation(cc, w_ada[0], b_ada[0][None, :])
    mod3 = mod[:bsz].reshape(bsz, 1, N_MOD * D_MODEL)
    csh1 = mod[bsz:bsz + 1, 0:D_MODEL]
    csc1 = mod[bsz:bsz + 1, D_MODEL:2 * D_MODEL]

    w = w_in[0]
    n1g = norm1_g[0][None, :]
    n2g = norm2_g[0][None, :]
    w_kv = w[:, OFF_K:OFF_GA].astype(BF16)
    w_a_cols = jnp.concatenate([w[:, OFF_CG:OFF_Q], w[:, OFF_K:OFF_GA]], axis=1).astype(BF16)
    w_b_cols = jnp.concatenate([w[:, OFF_B:OFF_CG], w[:, OFF_Q:OFF_K], w[:, OFF_GA:]],
                               axis=1).astype(BF16)
    cos_t, sin_t = _rope_tables(seq)

    k_ctx, vt_ctx = _context_kv(ctx, n1g, csh1, csc1, w_kv)
    sink_row = jnp.repeat(sink[0].astype(F32), BLOCK)[None, :]
    pad_g = jnp.zeros((D_MODEL, GROUP_ROWS - N_GROUPS), F32)
    pad_e = jnp.zeros((D_MODEL, ROUTER_ROWS - GROUP_ROWS - N_EXPERTS), F32)
    wr_t = jnp.concatenate([w_group[0], pad_g, w_router[0], pad_e], axis=1).T.astype(BF16)
    br_col = jnp.concatenate([b_group[0], pad_g[0], b_router[0], pad_e[0]])[:, None]

    upper = jnp.asarray(np.triu(np.ones((CHUNK, CHUNK), np.float32), 1), BF16)
    lower = jnp.asarray(np.tril(np.ones((N_EXPERTS, N_EXPERTS), np.float32), -1), BF16)
    x1, xl, pos_c, nb, w_up_bf, w_down_bf = _mix(
        x, n1g, n2g, mod3, w_a_cols, w_b_cols, cos_t, sin_t, k_ctx, vt_ctx,
        w_conv[0], b_conv[0][None, :], w_a[0].astype(BF16), w_b[0].astype(BF16),
        w_o[0].astype(BF16), sink_row, _window_bias(), wr_t, br_col, w_up[0], w_down[0],
        upper, lower)

    n = bsz * seq
    nch = n // CHUNK
    nb = nb[:, :, 0]
    expert_plan, combine_plan = _tile_plan(nb, nch)
    yt = _grouped_mlp(expert_plan, xl.reshape(-1, ROW_BLOCK, XL_COLS), w_up_bf, w_down_bf,
                      _max_tiles(nch))
    out = _combine(*combine_plan, yt, pos_c, x1.reshape(n, D_MODEL), mod3, final_g[None, :], seq)
    return out.reshape(bsz, seq, D_MODEL)
```

```python
import functools

import numpy as np
import jax
import jax.numpy as jnp
from jax import lax
from jax.experimental import pallas as pl
from jax.experimental.pallas import tpu as pltpu

F32 = jnp.float32
BF16 = jnp.bfloat16
I32 = jnp.int32

D_MODEL = 1024
GRID_W = 64
CONV_W = 512
N_HEADS = 8
N_KV_HEADS = 2
HEAD_DIM = 64
ATT_W = N_HEADS * HEAD_DIM
KV_W = N_KV_HEADS * HEAD_DIM
BLOCK = 128
ROPE_BASE = 10000.0
N_GROUPS = 4
EXPERTS_PER_GROUP = 8
N_EXPERTS = N_GROUPS * EXPERTS_PER_GROUP
EXPERT_FF = 256
N_MOD = 6
MOD_SHIFT1, MOD_SCALE1, MOD_GATE1, MOD_SHIFT2, MOD_SCALE2, MOD_GATE2 = range(N_MOD)
NORM_EPS = 1e-6
NEG_INF = -1e30
LOG2E = 1.4426950408889634
ONES_ROWS = 16

OFF_B, OFF_CG, OFF_XIN, OFF_Q, OFF_K, OFF_V, OFF_GA, OFF_GB = (
    0, 512, 1024, 1536, 2048, 2176, 2304, 3328)
IN_COLS = 4352

LANES = 128
SEQ_TILE = 512
BLOCKS_PER_TILE = SEQ_TILE // BLOCK
ROUTE_ROWS = 8
GROUP_ROWS = 8
ROUTER_ROWS = 48
VMEM_LIMIT = 62 * 1024 * 1024

CHUNK = SEQ_TILE
ROW_BLOCK = 16
CHUNK_ROWS = -(-(2 * CHUNK + N_EXPERTS * (ROW_BLOCK - 1)) // 256) * 256
CHUNK_BLOCKS = CHUNK_ROWS // ROW_BLOCK
GATE_COLS = LANES
XL_COLS = D_MODEL + GATE_COLS
COMMON_ROWS = 2 * CHUNK + 256
COMMON_BLOCKS = COMMON_ROWS // ROW_BLOCK
TILE_BLOCKS = 32
TILE_ROWS = TILE_BLOCKS * ROW_BLOCK
X_SLOTS = 6
Y_SLOTS = 4
W_SLOTS = 3
Y_RING = 3
RING = 3


def _cparams(sem):
    return pltpu.CompilerParams(dimension_semantics=sem, vmem_limit_bytes=VMEM_LIMIT)


def _rms_mod(x, g, shift, scale):
    inv = lax.rsqrt(jnp.mean(x * x, axis=-1, keepdims=True) + NORM_EPS)
    return (x * inv) * (g * (1.0 + scale)) + shift


def _mod_kernel(c_ref, w_ref, b_ref, o_ref):
    c = c_ref[...]
    a = (c * jax.nn.sigmoid(c)).astype(BF16)
    o_ref[...] = jnp.dot(a, w_ref[...].astype(BF16), preferred_element_type=F32) + b_ref[...]


def _modulation(cc, w_ada, b_ada):
    rows = cc.shape[0]
    cols = w_ada.shape[1]
    tile = 1024
    return pl.pallas_call(
        _mod_kernel,
        grid=(cols // tile,),
        in_specs=[pl.BlockSpec((rows, D_MODEL), lambda j: (0, 0)),
                  pl.BlockSpec((D_MODEL, tile), lambda j: (0, j)),
                  pl.BlockSpec((1, tile), lambda j: (0, j))],
        out_specs=pl.BlockSpec((rows, tile), lambda j: (0, j)),
        out_shape=jax.ShapeDtypeStruct((rows, cols), F32),
        compiler_params=_cparams(("arbitrary",)),
        name="adaln_mod",
    )(cc, w_ada, b_ada)


def _ctx_kernel(x_ref, g_ref, sh_ref, sc_ref, w_ref, k_ref, vt_ref):
    h = _rms_mod(x_ref[0], g_ref[...], sh_ref[...], sc_ref[...]).astype(BF16)
    kv = jnp.dot(h, w_ref[...], preferred_element_type=F32)
    k_ref[0] = kv[:, :KV_W].astype(BF16)
    vt_ref[0] = kv[:, KV_W:].T.astype(BF16)


def _context_kv(ctx, norm_g, csh, csc, w_kv):
    bsz, clen, _ = ctx.shape
    return pl.pallas_call(
        _ctx_kernel,
        grid=(bsz,),
        in_specs=[pl.BlockSpec((1, clen, D_MODEL), lambda b: (b, 0, 0)),
                  pl.BlockSpec((1, D_MODEL), lambda b: (0, 0)),
                  pl.BlockSpec((1, D_MODEL), lambda b: (0, 0)),
                  pl.BlockSpec((1, D_MODEL), lambda b: (0, 0)),
                  pl.BlockSpec((D_MODEL, 2 * KV_W), lambda b: (0, 0))],
        out_specs=[pl.BlockSpec((1, clen, KV_W), lambda b: (b, 0, 0)),
                   pl.BlockSpec((1, KV_W, clen), lambda b: (b, 0, 0))],
        out_shape=[jax.ShapeDtypeStruct((bsz, clen, KV_W), BF16),
                   jax.ShapeDtypeStruct((bsz, KV_W, clen), BF16)],
        compiler_params=_cparams(("arbitrary",)),
        name="context_kv",
    )(ctx, norm_g, csh, csc, w_kv)


def _rope(t, cos, sin_signed):
    lane = lax.broadcasted_iota(I32, (1, LANES), 1)
    first_half = (lane % HEAD_DIM) < (HEAD_DIM // 2)
    outs = []
    for j in range(t.shape[1] // LANES):
        tj = t[:, j * LANES:(j + 1) * LANES]
        partner = jnp.where(first_half,
                            pltpu.roll(tj, LANES - HEAD_DIM // 2, axis=1),
                            pltpu.roll(tj, HEAD_DIM // 2, axis=1))
        outs.append(tj * cos + partner * sin_signed)
    return outs[0] if len(outs) == 1 else jnp.concatenate(outs, axis=1)


def _sigmoid(x):
    return 1.0 / (1.0 + jnp.exp2(x * (-LOG2E)))


def _attn_scores(q_t, g, k_win, k_ctx, bias_p4, bias_n4):
    zeros = jnp.zeros((HEAD_DIM, 4 * BLOCK), BF16)
    qg = jnp.concatenate([q_t[(4 * g + h) * HEAD_DIM:(4 * g + h + 1) * HEAD_DIM, :]
                          for h in range(4)], axis=1)
    rhs = jnp.concatenate([qg, zeros] if g == 0 else [zeros, qg], axis=0)
    s = jnp.dot(jnp.concatenate([k_win, k_ctx], axis=0), rhs, preferred_element_type=F32)
    return (s[0:BLOCK] + bias_p4, s[BLOCK:2 * BLOCK], s[2 * BLOCK:3 * BLOCK] + bias_n4,
            s[3 * BLOCK:])


def _attn_probs(scores, sink):
    m = sink
    for s in scores:
        m = jnp.maximum(m, jnp.max(s, axis=0, keepdims=True))
    p_win = jnp.concatenate([jnp.exp2(s - m).astype(BF16) for s in scores[:3]], axis=0)
    p_ctx = jnp.exp2(scores[3] - m).astype(BF16)
    return p_win, p_ctx, m


def _attn_values(probs, sink, vt_win_g, vt_ctx_g):
    p_win, p_ctx, m = probs

    def with_ones(vt):
        r = lax.broadcasted_iota(I32, (ONES_ROWS, vt.shape[1]), 0)
        return jnp.concatenate([vt, jnp.where(r == 0, 1.0, 0.0).astype(BF16)], axis=0)

    o_ext = jnp.dot(with_ones(jnp.concatenate([vt_win_g, vt_ctx_g], axis=1)),
                    jnp.concatenate([p_win, p_ctx], axis=0),
                    preferred_element_type=F32)
    denom = o_ext[HEAD_DIM:HEAD_DIM + 1, :] + jnp.exp2(sink - m)
    return o_ext[:HEAD_DIM, :] / denom


def _route(logits_t):
    t = logits_t.shape[1]
    grow = lax.broadcasted_iota(I32, (GROUP_ROWS, t), 0)
    gl = jnp.where(grow < N_GROUPS, logits_t[0:GROUP_ROWS, :], NEG_INF)
    gm = jnp.max(gl, axis=0, keepdims=True)
    p_g = 1.0 / jnp.sum(jnp.exp(gl - gm), axis=0, keepdims=True)
    g_idx = jnp.min(jnp.where(gl == gm, grow, N_GROUPS), axis=0, keepdims=True)

    erow = lax.broadcasted_iota(I32, (N_EXPERTS, t), 0)
    el = logits_t[GROUP_ROWS:GROUP_ROWS + N_EXPERTS, :]
    sel = (erow // EXPERTS_PER_GROUP) == g_idx
    em = jnp.where(sel, el, NEG_INF)
    m1 = jnp.max(em, axis=0, keepdims=True)
    i1 = jnp.min(jnp.where(em == m1, erow, N_EXPERTS), axis=0, keepdims=True)
    em2 = jnp.where(erow == i1, NEG_INF, em)
    m2 = jnp.max(em2, axis=0, keepdims=True)
    i2 = jnp.min(jnp.where(em2 == m2, erow, N_EXPERTS), axis=0, keepdims=True)
    z = jnp.sum(jnp.where(sel, jnp.exp(el - m1), 0.0), axis=0, keepdims=True)
    p1 = 1.0 / z
    p2 = jnp.exp(m2 - m1) / z
    gate1 = p_g * p1 / (p1 + p2)
    gate2 = p_g * p2 / (p1 + p2)
    pad = jnp.zeros((ROUTE_ROWS - 4, t), F32)
    return jnp.concatenate([i1.astype(F32), i2.astype(F32), gate1, gate2, pad], axis=0)


def _project_a(x_ref, g_ref, sh_ref, sc_ref, w_ref, cos_ref, sin_ref, slot, uc_slot,
               uc_ring, k_ring, vt_ring):
    h = _rms_mod(x_ref[0], g_ref[...], sh_ref[0], sc_ref[0]).astype(BF16)
    u = jnp.dot(h, w_ref[...], preferred_element_type=F32)
    uc_ring[uc_slot] = u[:, :CONV_W] * u[:, CONV_W:2 * CONV_W]
    k = u[:, 2 * CONV_W:2 * CONV_W + KV_W]
    k_ring[slot] = _rope(k, cos_ref[...], sin_ref[...]).astype(BF16)
    vt_ring[slot] = u[:, 2 * CONV_W + KV_W:].T.astype(BF16)


def _mix_step(x_ref, g1n_ref, g2n_ref, sh1_ref, sc1_ref, gt1_ref, sh2_ref, sc2_ref,
              wb_ref, cos_ref, sin_ref,
              xn_ref, shn_ref, scn_ref, cosn_ref, sinn_ref, wac_ref,
              kc_ref, vtc_ref, wconv_ref, bconv_ref, wa_ref, wbb_ref, wo_ref,
              sink_ref, bias_ref, wr_ref, br_ref, wu_ref, wd_ref, upper_ref, lower_ref,
              x1_ref, xl_ref, pos_ref, nb_ref, wub_ref, wdb_ref,
              h2_scr, route_scr, uc_ring, uc_last, k_ring, vt_ring,
              *, n_seq_tiles):
    step = pl.program_id(0)
    t_idx = step % n_seq_tiles
    is_first = t_idx == 0
    is_last = t_idx == n_seq_tiles - 1
    own, prv, nxt = step % RING, (step + RING - 1) % RING, (step + 1) % RING

    permute, permute_tail = _sort_chunk(h2_scr[...], route_scr[...], upper_ref, lower_ref,
                                        xl_ref, pos_ref, nb_ref)

    @pl.when(step == 0)
    def _():
        uc_ring[...] = jnp.zeros_like(uc_ring)
        uc_last[...] = jnp.zeros_like(uc_last)
        k_ring[...] = jnp.zeros_like(k_ring)
        vt_ring[...] = jnp.zeros_like(vt_ring)
        _project_a(x_ref, g1n_ref, sh1_ref, sc1_ref, wac_ref, cos_ref, sin_ref, 0, 0,
                   uc_ring, k_ring, vt_ring)

    x = x_ref[0]
    hb = _rms_mod(x, g1n_ref[...], sh1_ref[0], sc1_ref[0]).astype(BF16)
    bq = jnp.dot(hb, wb_ref[:, 0:2 * CONV_W], preferred_element_type=F32)
    q = _rope(bq[:, CONV_W:], cos_ref[...], sin_ref[...]) * (HEAD_DIM ** -0.5 * LOG2E)

    def block_ring(ring, slot_, blk, axis):
        lo = blk * BLOCK
        return ring[slot_, lo:lo + BLOCK, :] if axis == 0 else ring[slot_, :, lo:lo + BLOCK]

    def window(ring, jb, axis):
        parts = []
        for blk in (jb - 1, jb, jb + 1):
            if blk < 0:
                parts.append(block_ring(ring, prv, BLOCKS_PER_TILE - 1, axis))
            elif blk >= BLOCKS_PER_TILE:
                parts.append(block_ring(ring, nxt, 0, axis))
            else:
                parts.append(block_ring(ring, own, blk, axis))
        return jnp.concatenate(parts, axis=axis)

    k_ctx = kc_ref[0]
    vt_ctx = vtc_ref[0]
    bias_prev = bias_ref[0:BLOCK, :]
    bias_next = bias_ref[2 * BLOCK:3 * BLOCK, :]
    sink_row = sink_ref[...] * LOG2E
    n_units = BLOCKS_PER_TILE * N_KV_HEADS
    gate_cols = 2 * D_MODEL // n_units
    gate_chunks = []

    def gate_chunk(u):
        c0 = 2 * CONV_W + u * gate_cols
        gate_chunks.append(jnp.dot(hb, wb_ref[:, c0:c0 + gate_cols], preferred_element_type=F32))

    q_ts, biases = [], []
    for jb in range(BLOCKS_PER_TILE):
        bias_p, bias_n = bias_prev, bias_next
        if jb == 0:
            bias_p = bias_prev + jnp.where(is_first, NEG_INF, 0.0)
        if jb == BLOCKS_PER_TILE - 1:
            bias_n = bias_next + jnp.where(is_last, NEG_INF, 0.0)
        biases.append((jnp.concatenate([bias_p] * 4, axis=1), jnp.concatenate([bias_n] * 4, axis=1)))
        q_ts.append(q[jb * BLOCK:(jb + 1) * BLOCK, :].T.astype(BF16))

    scores, probs, outs = {}, {}, {}
    for t in range(n_units + 2):
        if t == 3:
            wub_ref[...] = wu_ref[...].astype(BF16)
        if t == 5:
            wdb_ref[...] = wd_ref[...].astype(BF16)
        if t == 0:
            permute(0, COMMON_ROWS // 2)
        if t == 4:
            permute(COMMON_ROWS // 2, COMMON_ROWS // 2)
        if t == 2:
            _project_a(xn_ref, g1n_ref, shn_ref, scn_ref, wac_ref, cosn_ref, sinn_ref, nxt,
                       (step + 1) % 2, uc_ring, k_ring, vt_ring)
        if t < n_units:
            jb, g = divmod(t, N_KV_HEADS)
            scores[t] = _attn_scores(q_ts[jb], g, window(k_ring, jb, 0), k_ctx, *biases[jb])
            gate_chunk(t)
        if 0 <= t - 1 < n_units:
            g = (t - 1) % N_KV_HEADS
            probs[t - 1] = _attn_probs(scores.pop(t - 1),
                                       sink_row[:, g * 4 * BLOCK:(g + 1) * 4 * BLOCK])
        if 0 <= t - 2 < n_units:
            jb, g = divmod(t - 2, N_KV_HEADS)
            outs[t - 2] = _attn_values(
                probs.pop(t - 2), sink_row[:, g * 4 * BLOCK:(g + 1) * 4 * BLOCK],
                window(vt_ring, jb, 1)[g * HEAD_DIM:(g + 1) * HEAD_DIM, :],
                vt_ctx[g * HEAD_DIM:(g + 1) * HEAD_DIM, :])
    o_blocks = []
    for jb in range(BLOCKS_PER_TILE):
        o_rows = [outs[jb * N_KV_HEADS + g][:, h * BLOCK:(h + 1) * BLOCK]
                  for g in range(N_KV_HEADS) for h in range(4)]
        o_blocks.append(jnp.concatenate(o_rows, axis=0).T)
    gates = jnp.concatenate(gate_chunks, axis=1)

    uc = uc_ring[step % 2]
    row = lax.broadcasted_iota(I32, (SEQ_TILE, 1), 0)
    prev_row = jnp.where(is_first, 0.0, uc_last[0:1, :])
    next_row = jnp.where(is_last, 0.0, uc_ring[(step + 1) % 2, 0:1, :])
    uc_last[0:1, :] = uc[SEQ_TILE - 1:SEQ_TILE, :]
    up = jnp.where(row == 0, prev_row, pltpu.roll(uc, 1, axis=0))
    dn = jnp.where(row == SEQ_TILE - 1, next_row, pltpu.roll(uc, SEQ_TILE - 1, axis=0))
    wconv = wconv_ref[...]
    y = bconv_ref[...] + (up * wconv[0:1, :] + uc * wconv[1:2, :] + dn * wconv[2:3, :])
    ya = jnp.dot((bq[:, :CONV_W] * y).astype(BF16), wa_ref[...], preferred_element_type=F32)

    o = jnp.concatenate(o_blocks, axis=0).astype(BF16)
    yb = jnp.dot(o, wbb_ref[...], preferred_element_type=F32)
    merged = (_sigmoid(gates[:, :D_MODEL]) * ya + _sigmoid(gates[:, D_MODEL:]) * yb).astype(BF16)
    x1 = x + gt1_ref[0] * jnp.dot(merged, wo_ref[...], preferred_element_type=F32)
    x1_ref[0] = x1
    h2 = _rms_mod(x1, g2n_ref[...], sh2_ref[0], sc2_ref[0]).astype(BF16)
    logits_t = lax.dot_general(wr_ref[...], h2, (((1,), (1,)), ((), ())),
                               preferred_element_type=F32) + br_ref[...]
    permute_tail()
    h2_scr[...] = h2
    route_scr[...] = _route(logits_t)


N_MIX_OUTPUTS = 6
N_MIX_SCRATCH = 6


def _mix_kernel(*refs, n_seq_tiles, batch):
    n_tail = N_MIX_OUTPUTS + N_MIX_SCRATCH
    upper_ref, lower_ref = refs[-n_tail - 2:-n_tail]
    _, xl_ref, pos_ref, nb_ref, _, _ = refs[-n_tail:-N_MIX_SCRATCH]
    h2_scr, route_scr = refs[-N_MIX_SCRATCH:-N_MIX_SCRATCH + 2]
    step = pl.program_id(0)
    n_main = n_seq_tiles * batch

    @pl.when(step == 0)
    def _():
        h2_scr[...] = jnp.zeros_like(h2_scr)
        route_scr[...] = jnp.zeros_like(route_scr)

    @pl.when(step < n_main)
    def _():
        _mix_step(*refs, n_seq_tiles=n_seq_tiles)

    @pl.when(step == n_main)
    def _():
        permute, permute_tail = _sort_chunk(h2_scr[...], route_scr[...], upper_ref, lower_ref,
                                            xl_ref, pos_ref, nb_ref)
        permute(0, COMMON_ROWS)
        permute_tail()


def _mix(x, norm1_g, norm2_g, mod3, w_a_cols, w_b_cols, cos_t, sin_t, k_ctx, vt_ctx,
         w_conv, b_conv, w_a, w_b, w_o, sink_row, bias, wr_t, br_col, w_up, w_down, upper, lower):
    bsz, seq, _ = x.shape
    nt = seq // SEQ_TILE
    clen = k_ctx.shape[1]
    nch = n_main = bsz * nt
    assert N_EXPERTS % n_main == 0, "each grid step converts an equal share of the experts"
    e_step = N_EXPERTS // n_main

    def tile(s):
        return divmod(jnp.clip(s, 0, n_main - 1), nt)

    def per_tile(shape, index, shift=0):
        return pl.BlockSpec(shape, lambda s: index(*tile(s + shift)))

    def const_spec(shape):
        return pl.BlockSpec(shape, lambda s: tuple(0 for _ in shape),
                            pipeline_mode=pl.Buffered(1))

    def x_spec(shift=0):
        return per_tile((1, SEQ_TILE, D_MODEL), lambda b, t: (b, t, 0), shift)

    def mod_spec(j, shift=0):
        return per_tile((1, 1, D_MODEL), lambda b, t: (b, 0, j), shift)

    def rope_spec(shift=0):
        return per_tile((SEQ_TILE, LANES), lambda b, t: (t, 0), shift)

    def expert_spec(w):
        return pl.BlockSpec((e_step,) + w.shape[1:], lambda s: (jnp.minimum(s, n_main - 1), 0, 0))

    def chunk_spec(shape):
        return pl.BlockSpec(shape, lambda s: (jnp.maximum(s - 1, 0),) + (0,) * (len(shape) - 1))

    in_specs = [
        x_spec(), const_spec((1, D_MODEL)), const_spec((1, D_MODEL)),
        mod_spec(MOD_SHIFT1), mod_spec(MOD_SCALE1), mod_spec(MOD_GATE1), mod_spec(MOD_SHIFT2),
        mod_spec(MOD_SCALE2),
        const_spec(w_b_cols.shape), rope_spec(), rope_spec(),
        x_spec(1), mod_spec(MOD_SHIFT1, 1), mod_spec(MOD_SCALE1, 1), rope_spec(1), rope_spec(1),
        const_spec(w_a_cols.shape),
        per_tile((1, clen, KV_W), lambda b, t: (b, 0, 0)),
        per_tile((1, KV_W, clen), lambda b, t: (b, 0, 0)),
        const_spec(w_conv.shape), const_spec(b_conv.shape),
        const_spec(w_a.shape), const_spec(w_b.shape), const_spec(w_o.shape),
        const_spec(sink_row.shape), const_spec(bias.shape),
        const_spec(wr_t.shape), const_spec(br_col.shape),
        expert_spec(w_up), expert_spec(w_down),
        const_spec(upper.shape), const_spec(lower.shape),
    ]
    out_specs = [
        x_spec(), chunk_spec((CHUNK_ROWS, XL_COLS)), chunk_spec((CHUNK, LANES)),
        chunk_spec((1, N_EXPERTS, LANES)), expert_spec(w_up), expert_spec(w_down),
    ]
    out_shape = [
        jax.ShapeDtypeStruct((bsz, seq, D_MODEL), F32),
        jax.ShapeDtypeStruct((nch * CHUNK_ROWS, XL_COLS), BF16),
        jax.ShapeDtypeStruct((nch * CHUNK, LANES), F32),
        jax.ShapeDtypeStruct((nch, N_EXPERTS, LANES), I32),
        jax.ShapeDtypeStruct(w_up.shape, BF16),
        jax.ShapeDtypeStruct(w_down.shape, BF16),
    ]
    return pl.pallas_call(
        functools.partial(_mix_kernel, n_seq_tiles=nt, batch=bsz),
        grid=(n_main + 1,),
        in_specs=in_specs,
        out_specs=out_specs,
        out_shape=out_shape,
        scratch_shapes=[pltpu.VMEM((CHUNK, D_MODEL), BF16), pltpu.VMEM((ROUTE_ROWS, CHUNK), F32),
                        pltpu.VMEM((2, SEQ_TILE, CONV_W), F32), pltpu.VMEM((8, CONV_W), F32),
                        pltpu.VMEM((RING, SEQ_TILE, KV_W), BF16),
                        pltpu.VMEM((RING, KV_W, SEQ_TILE), BF16)],
        compiler_params=_cparams(("arbitrary",)),
        name="token_mix",
    )(x, norm1_g, norm2_g, mod3, mod3, mod3, mod3, mod3, w_b_cols, cos_t, sin_t,
      x, mod3, mod3, cos_t, sin_t, w_a_cols, k_ctx, vt_ctx,
      w_conv, b_conv, w_a, w_b, w_o, sink_row, bias, wr_t, br_col, w_up, w_down, upper, lower)


def _bf16_parts(v):
    hi = v.astype(BF16).astype(F32)
    r1 = v - hi
    mid = r1.astype(BF16).astype(F32)
    lo = (r1 - mid).astype(BF16).astype(F32)
    return hi, mid, lo


def _sort_chunk(h, route, upper_ref, lower_ref, xl_ref, pos_ref, nb_ref):
    e1 = route[0:1, :].astype(I32)
    e2 = route[1:2, :].astype(I32)
    erow = lax.broadcasted_iota(I32, (N_EXPERTS, CHUNK), 0)
    hit1 = erow == e1
    hit2 = erow == e2
    onehot = jnp.where(hit1, 1.0, 0.0) + jnp.where(hit2, 1.0, 0.0)
    cum = jnp.dot(onehot.astype(BF16), upper_ref[...], preferred_element_type=F32)
    cnt = jnp.sum(onehot, axis=1, keepdims=True)
    nblk = jnp.floor((cnt + (ROW_BLOCK - 1)) * (1.0 / ROW_BLOCK))
    nblk_b = jnp.broadcast_to(nblk, (N_EXPERTS, LANES))
    seg = jnp.dot(lower_ref[...], nblk_b.astype(BF16), preferred_element_type=F32) * ROW_BLOCK
    base = seg[:, 0:1] + cum
    pos1 = jnp.sum(jnp.where(hit1, base, 0.0), axis=0, keepdims=True)
    pos2 = jnp.sum(jnp.where(hit2, base, 0.0), axis=0, keepdims=True)
    p1i = pos1.astype(I32)
    p2i = pos2.astype(I32)
    used_rows = (jnp.sum(nblk) * ROW_BLOCK).astype(I32)

    prow = lax.broadcasted_iota(I32, (LANES, CHUNK), 0)
    parts = _bf16_parts(route[2:3, :]) + _bf16_parts(route[3:4, :])
    gpart_rows = jnp.zeros((LANES, CHUNK), F32)
    for j, part in enumerate(parts):
        gpart_rows = jnp.where(prow == j, part, gpart_rows)
    h_ext = jnp.concatenate([h, gpart_rows.T.astype(BF16)], axis=1)
    glane = lax.broadcasted_iota(I32, (1, GATE_COLS), 1)
    one = jnp.ones((), BF16)
    zero = jnp.zeros((), BF16)
    def permute(row0, n_rows):
        r16 = lax.broadcasted_iota(I32, (n_rows, CHUNK), 0).astype(jnp.int16)
        q1 = (p1i - row0).astype(jnp.int16)
        q2 = (p2i - row0).astype(jnp.int16)
        sel1 = jnp.where(r16 == q1, one, zero)
        sel = jnp.where(r16 == q2, one, sel1)
        is_slot1 = jnp.max(sel1, axis=1, keepdims=True)
        rows = pl.ds(row0, n_rows)
        xg = jnp.dot(sel, h_ext, preferred_element_type=F32)
        xl_ref[rows, 0:D_MODEL] = xg[:, :D_MODEL].astype(BF16)
        g6 = xg[:, D_MODEL:]
        g3 = jnp.where(is_slot1.astype(F32) > 0.0, g6, pltpu.roll(g6, GATE_COLS - 3, axis=1))
        xl_ref[rows, D_MODEL:XL_COLS] = jnp.where(glane < 3, g3, 0.0).astype(BF16)

    pos_rows = jnp.where(prow == 0, pos1, jnp.where(prow == 1, pos2, 0.0))
    pos_ref[...] = pos_rows.T
    nb_ref[0] = nblk_b.astype(I32)

    def permute_tail():
        tail_rows = CHUNK_ROWS - COMMON_ROWS

        @pl.when(COMMON_ROWS < used_rows)
        def _():
            permute(COMMON_ROWS, tail_rows)

        @pl.when(COMMON_ROWS >= used_rows)
        def _():
            xl_ref[pl.ds(COMMON_ROWS, tail_rows), :] = jnp.zeros((tail_rows, XL_COLS), BF16)

    return permute, permute_tail


def _max_tiles(nch):
    max_blocks = nch * (2 * CHUNK + N_EXPERTS * (ROW_BLOCK - 1)) // ROW_BLOCK
    return max_blocks // TILE_BLOCKS + N_EXPERTS


def _masked_prefix(le, values):
    delta = values - jnp.concatenate([jnp.zeros((1,), values.dtype), values[:-1]])
    return jnp.sum(jnp.where(le, delta[None, :], 0), axis=1)


def _tile_plan(nb, nch):
    n_tiles = _max_tiles(nch)
    nbt = nb.T
    nbe = jnp.sum(nbt, axis=1)
    nte = (nbe + TILE_BLOCKS - 1) // TILE_BLOCKS
    tile_end = jnp.cumsum(nte)
    tile_start = tile_end - nte
    n_act = tile_end[-1]
    tiles = jnp.arange(n_tiles, dtype=I32)
    te = jnp.sum((tile_end[None, :] <= tiles[:, None]).astype(I32), axis=1)
    e_ar = jnp.arange(N_EXPERTS, dtype=I32)
    active = nte > 0
    te = jnp.where(tiles < n_act, te, jnp.max(jnp.where(active, e_ar, 0)))
    first = jnp.logical_and(te != jnp.concatenate([jnp.full((1,), -1, I32), te[:-1]]),
                            tiles < n_act).astype(I32)
    rank = jnp.cumsum(active.astype(I32)) - 1
    k_ar = jnp.arange(N_EXPERTS + W_SLOTS, dtype=I32)
    eseq = jnp.sum(jnp.where(jnp.logical_and(active[None, :], rank[None, :] == k_ar[:, None]),
                             e_ar[None, :], 0), axis=1)
    n_exp = jnp.sum(active.astype(I32))

    cb_excl = jnp.cumsum(nbt, axis=1) - nbt
    gs = TILE_BLOCKS * tile_start[:, None] + cb_excl
    seg_blk = jnp.cumsum(nb, axis=1) - nb
    base_blk = jnp.arange(nch, dtype=I32)[None, :] * CHUNK_BLOCKS + seg_blk.T
    gs_f = gs.reshape(-1)
    slots = jnp.arange(n_tiles * TILE_BLOCKS, dtype=I32)
    le = gs_f[None, :] <= slots[:, None]
    blk = slots + _masked_prefix(le, (base_blk - gs).reshape(-1))
    valid = slots < _masked_prefix(le, (gs + nbt).reshape(-1))
    gblk = jnp.concatenate([jnp.where(valid, blk, 0).astype(I32),
                            jnp.zeros(((X_SLOTS - 1) * TILE_BLOCKS,), I32)])

    m = jnp.arange(CHUNK_BLOCKS, dtype=I32)
    le_c = seg_blk[:, None, :] <= m[None, :, None]
    shift = (gs.T - seg_blk)
    delta = shift - jnp.concatenate([jnp.zeros((nch, 1), I32), shift[:, :-1]], axis=1)
    slot_of = m[None, :] + jnp.sum(jnp.where(le_c, delta[:, None, :], 0), axis=2)
    used_blocks = jnp.sum(nb, axis=1)
    slot_of = jnp.where(m[None, :] < used_blocks[:, None], slot_of, 0).astype(I32)
    pad = Y_RING - 1
    slot_of = jnp.concatenate([slot_of.reshape(-1), jnp.zeros((pad * CHUNK_BLOCKS,), I32)])
    used_rows = jnp.concatenate([ROW_BLOCK * used_blocks, jnp.zeros((pad,), I32)]).astype(I32)
    return ((n_act.reshape(1).astype(I32), n_exp.reshape(1), first, eseq, gblk),
            (used_rows, slot_of))


def _gmm_kernel(nact_ref, nexp_ref, first_ref, eseq_ref, gblk_ref,
                xl_hbm, wu_hbm, wd_hbm, yt_hbm,
                xbuf, ybuf, zbuf, wu_st, wd_st, in_sem, out_sem, w_sem, z_sem):
    n_act = nact_ref[0]
    n_exp = nexp_ref[0]
    n_tiles = yt_hbm.shape[0] // TILE_BLOCKS
    prefetch = X_SLOTS - 1

    def gather(tile, b):
        s = tile % X_SLOTS
        blk = gblk_ref[tile * TILE_BLOCKS + b]
        return pltpu.make_async_copy(xl_hbm.at[blk], xbuf.at[s, b], in_sem.at[s])

    def write_back(tile):
        s = tile % Y_SLOTS
        return pltpu.make_async_copy(ybuf.at[s], yt_hbm.at[pl.ds(tile * TILE_BLOCKS, TILE_BLOCKS)],
                                     out_sem.at[s])

    def zero_fill(tile):
        return pltpu.make_async_copy(zbuf, yt_hbm.at[pl.ds(tile * TILE_BLOCKS, TILE_BLOCKS)], z_sem)

    def weights(q):
        e = eseq_ref[q]
        s = q % W_SLOTS
        return (pltpu.make_async_copy(wu_hbm.at[e], wu_st.at[s], w_sem.at[0, s]),
                pltpu.make_async_copy(wd_hbm.at[e], wd_st.at[s], w_sem.at[1, s]))

    def tile_blocks(fn):
        for b in range(TILE_BLOCKS):
            fn(b)

    for q0 in range(W_SLOTS - 1):
        @pl.when(q0 < n_exp)
        def _(q0=q0):
            for cp in weights(q0):
                cp.start()
    for t0 in range(prefetch):
        tile_blocks(lambda b, t0=t0: gather(t0, b).start())

    def body(i, q):
        tile_blocks(lambda b: gather(i, b).wait())

        @pl.when(i >= Y_SLOTS)
        def _():
            write_back(i - Y_SLOTS).wait()

        is_first = first_ref[i] == 1

        @pl.when(is_first)
        def _():
            for cp in weights(q):
                cp.wait()

            @pl.when(q + W_SLOTS - 1 < n_exp)
            def _():
                for cp in weights(q + W_SLOTS - 1):
                    cp.start()

        q = q + is_first.astype(I32)
        ws = (q - 1) % W_SLOTS
        xs = i % X_SLOTS
        ys = i % Y_SLOTS

        x = xbuf[xs].reshape(TILE_ROWS, XL_COLS)
        gate = jnp.sum(x[:, D_MODEL:].astype(F32), axis=1, keepdims=True)
        au = jnp.dot(x[:, :D_MODEL], wu_st[ws], preferred_element_type=F32)
        tile_blocks(lambda b: gather(i + prefetch, b).start())
        a = au[:, :EXPERT_FF]
        act = (a * _sigmoid(a)) * au[:, EXPERT_FF:]
        y = jnp.dot(act.astype(BF16), wd_st[ws], preferred_element_type=F32)
        ybuf[ys] = (gate * y).astype(BF16).reshape(TILE_BLOCKS, ROW_BLOCK, D_MODEL)
        write_back(i).start()

        @pl.when(n_act + i < n_tiles)
        def _():
            zero_fill(n_act + i).start()
        return q

    zbuf[...] = jnp.zeros_like(zbuf)
    lax.fori_loop(0, n_act, body, jnp.int32(0))

    def fill_rest(t, carry):
        zero_fill(t).start()
        return carry

    lax.fori_loop(jnp.minimum(2 * n_act, n_tiles), n_tiles, fill_rest, 0)

    def drain(t, carry):
        zero_fill(t).wait()
        return carry

    lax.fori_loop(n_act, n_tiles, drain, 0)
    for k in range(prefetch):
        tile_blocks(lambda b, k=k: gather(n_act + k, b).wait())
    for k in range(Y_SLOTS):
        @pl.when(n_act - 1 - k >= 0)
        def _(k=k):
            write_back(n_act - 1 - k).wait()


def _grouped_mlp(plan, xl, w_up, w_down, n_tiles):
    grid_spec = pltpu.PrefetchScalarGridSpec(
        num_scalar_prefetch=len(plan),
        grid=(1,),
        in_specs=[pl.BlockSpec(memory_space=pl.ANY)] * 3,
        out_specs=pl.BlockSpec(memory_space=pl.ANY),
        scratch_shapes=[pltpu.VMEM((X_SLOTS, TILE_BLOCKS, ROW_BLOCK, XL_COLS), BF16),
                        pltpu.VMEM((Y_SLOTS, TILE_BLOCKS, ROW_BLOCK, D_MODEL), BF16),
                        pltpu.VMEM((TILE_BLOCKS, ROW_BLOCK, D_MODEL), BF16),
                        pltpu.VMEM((W_SLOTS, D_MODEL, 2 * EXPERT_FF), BF16),
                        pltpu.VMEM((W_SLOTS, EXPERT_FF, D_MODEL), BF16),
                        pltpu.SemaphoreType.DMA((X_SLOTS,)),
                        pltpu.SemaphoreType.DMA((Y_SLOTS,)),
                        pltpu.SemaphoreType.DMA((2, W_SLOTS)),
                        pltpu.SemaphoreType.DMA(())])
    return pl.pallas_call(
        _gmm_kernel,
        grid_spec=grid_spec,
        out_shape=jax.ShapeDtypeStruct((n_tiles * TILE_BLOCKS, ROW_BLOCK, D_MODEL), BF16),
        compiler_params=_cparams(("arbitrary",)),
        name="moe_experts",
    )(*plan, xl, w_up, w_down)


def _combine_kernel(used_ref, slot_ref, yt_hbm, pos_ref, x1_ref, gt2_ref, fg_ref, o_ref, ybuf, sem):
    c = pl.program_id(0)
    n_chunks = pl.num_programs(0)
    used_rows = used_ref[c]

    def chunk_blocks(chunk, op):
        s = chunk % Y_RING

        def copy(b):
            slot = slot_ref[chunk * CHUNK_BLOCKS + b]
            return pltpu.make_async_copy(yt_hbm.at[slot], ybuf.at[s, b], sem.at[s])

        for b in range(COMMON_BLOCKS):
            op(copy(b))

        @pl.when(used_ref[chunk] > COMMON_ROWS)
        def _():
            for b in range(COMMON_BLOCKS, CHUNK_BLOCKS):
                op(copy(b))

    @pl.when(c == 0)
    def _():
        for k in range(Y_RING - 1):
            chunk_blocks(k, lambda cp: cp.start())

    chunk_blocks(c + Y_RING - 1, lambda cp: cp.start())
    chunk_blocks(c, lambda cp: cp.wait())

    def body(k_rows):
        pos = pos_ref[...]
        p1 = pos[:, 0:1].astype(I32)
        p2 = pos[:, 1:2].astype(I32)
        r = lax.broadcasted_iota(I32, (CHUNK, k_rows), 1)
        sel = jnp.where(r == p1, 1.0, jnp.where(r == p2, 1.0, 0.0)).astype(BF16)
        yl = ybuf[c % Y_RING, 0:k_rows // ROW_BLOCK].reshape(k_rows, D_MODEL)
        y = jnp.dot(sel, yl, preferred_element_type=F32)
        x2 = x1_ref[...] + gt2_ref[0] * y
        inv = lax.rsqrt(jnp.mean(x2 * x2, axis=-1, keepdims=True) + NORM_EPS)
        o_ref[...] = (x2 * inv) * fg_ref[...]

    pl.when(used_rows <= COMMON_ROWS)(lambda: body(COMMON_ROWS))
    pl.when(used_rows > COMMON_ROWS)(lambda: body(CHUNK_ROWS))

    @pl.when(c == n_chunks - 1)
    def _():
        for k in range(1, Y_RING):
            chunk_blocks(c + k, lambda cp: cp.wait())


def _combine(used_rows, slot_of, yt, pos_c, x1, mod3, final_g, seq):
    n = x1.shape[0]
    per_seq = seq // CHUNK
    grid_spec = pltpu.PrefetchScalarGridSpec(
        num_scalar_prefetch=2,
        grid=(n // CHUNK,),
        in_specs=[pl.BlockSpec(memory_space=pl.ANY),
                  pl.BlockSpec((CHUNK, LANES), lambda c, u, s: (c, 0)),
                  pl.BlockSpec((CHUNK, D_MODEL), lambda c, u, s: (c, 0)),
                  pl.BlockSpec((1, 1, D_MODEL), lambda c, u, s: (c // per_seq, 0, MOD_GATE2)),
                  pl.BlockSpec((1, D_MODEL), lambda c, u, s: (0, 0))],
        out_specs=pl.BlockSpec((CHUNK, D_MODEL), lambda c, u, s: (c, 0)),
        scratch_shapes=[pltpu.VMEM((Y_RING, CHUNK_BLOCKS, ROW_BLOCK, D_MODEL), BF16),
                        pltpu.SemaphoreType.DMA((Y_RING,))])
    return pl.pallas_call(
        _combine_kernel,
        grid_spec=grid_spec,
        out_shape=jax.ShapeDtypeStruct((n, D_MODEL), F32),
        compiler_params=_cparams(("arbitrary",)),
        name="moe_combine",
    )(used_rows, slot_of, yt, pos_c, x1, mod3, final_g)


def _rope_tables(seq):
    n_freq = HEAD_DIM // 4
    inv_freq = ROPE_BASE ** (-jnp.arange(n_freq, dtype=F32) / n_freq)
    rows = seq // GRID_W
    row = jnp.repeat(jnp.arange(rows, dtype=F32), GRID_W)
    col = jnp.tile(jnp.arange(GRID_W, dtype=F32), rows)
    ang = jnp.concatenate([row[:, None] * inv_freq, col[:, None] * inv_freq], axis=-1)
    cos, sin = jnp.cos(ang), jnp.sin(ang)
    return jnp.tile(cos, (1, 4)), jnp.concatenate([-sin, sin, -sin, sin], axis=1)


def _window_bias():
    key = np.arange(3 * BLOCK)[:, None]
    qry = np.arange(BLOCK)[None, :]
    valid = (key - qry >= 0) & (key - qry <= 2 * BLOCK)
    return jnp.asarray(np.where(valid, 0.0, NEG_INF), F32)


def kernel(x, c, ctx, c_ctx, w_ada, b_ada, norm1_g, w_in, w_conv, b_conv, w_a, w_b, sink, w_o,
           norm2_g, w_group, b_group, w_router, b_router, w_up, w_down, final_g):
    bsz, seq, _ = x.shape
    assert w_ada.shape[0] == 1 and seq % SEQ_TILE == 0

    cc = jnp.zeros((16, D_MODEL), F32).at[:bsz].set(c).at[bsz].set(c_ctx)
    mod = _modulation(cc, w_ada[0], b_ada[0][None, :])
    mod3 = mod[:bsz].reshape(bsz, 1, N_MOD * D_MODEL)
    csh1 = mod[bsz:bsz + 1, 0:D_MODEL]
    csc1 = mod[bsz:bsz + 1, D_MODEL:2 * D_MODEL]

    w = w_in[0]
    n1g = norm1_g[0][None, :]
    n2g = norm2_g[0][None, :]
    w_kv = w[:, OFF_K:OFF_GA].astype(BF16)
    w_a_cols = jnp.concatenate([w[:, OFF_CG:OFF_Q], w[:, OFF_K:OFF_GA]], axis=1).astype(BF16)
    w_b_cols = jnp.concatenate([w[:, OFF_B:OFF_CG], w[:, OFF_Q:OFF_K], w[:, OFF_GA:]],
                               axis=1).astype(BF16)
    cos_t, sin_t = _rope_tables(seq)

    k_ctx, vt_ctx = _context_kv(ctx, n1g, csh1, csc1, w_kv)
    sink_row = jnp.repeat(sink[0].astype(F32), BLOCK)[None, :]
    pad_g = jnp.zeros((D_MODEL, GROUP_ROWS - N_GROUPS), F32)
    pad_e = jnp.zeros((D_MODEL, ROUTER_ROWS - GROUP_ROWS - N_EXPERTS), F32)
    wr_t = jnp.concatenate([w_group[0], pad_g, w_router[0], pad_e], axis=1).T.astype(BF16)
    br_col = jnp.concatenate([b_group[0], pad_g[0], b_router[0], pad_e[0]])[:, None]

    upper = jnp.asarray(np.triu(np.ones((CHUNK, CHUNK), np.float32), 1), BF16)
    lower = jnp.asarray(np.tril(np.ones((N_EXPERTS, N_EXPERTS), np.float32), -1), BF16)
    x1, xl, pos_c, nb, w_up_bf, w_down_bf = _mix(
        x, n1g, n2g, mod3, w_a_cols, w_b_cols, cos_t, sin_t, k_ctx, vt_ctx,
        w_conv[0], b_conv[0][None, :], w_a[0].astype(BF16), w_b[0].astype(BF16),
        w_o[0].astype(BF16), sink_row, _window_bias(), wr_t, br_col, w_up[0], w_down[0],
        upper, lower)

    n = bsz * seq
    nch = n // CHUNK
    nb = nb[:, :, 0]
    expert_plan, combine_plan = _tile_plan(nb, nch)
    yt = _grouped_mlp(expert_plan, xl.reshape(-1, ROW_BLOCK, XL_COLS), w_up_bf, w_down_bf,
                      _max_tiles(nch))
    out = _combine(*combine_plan, yt, pos_c, x1.reshape(n, D_MODEL), mod3, final_g[None, :], seq)
    return out.reshape(bsz, seq, D_MODEL)
```

```python
import functools

import numpy as np
import jax
import jax.numpy as jnp
from jax import lax
from jax.experimental import pallas as pl
from jax.experimental.pallas import tpu as pltpu

F32 = jnp.float32
BF16 = jnp.bfloat16
I32 = jnp.int32

D_MODEL = 1024
GRID_W = 64
CONV_W = 512
N_HEADS = 8
N_KV_HEADS = 2
HEAD_DIM = 64
ATT_W = N_HEADS * HEAD_DIM
KV_W = N_KV_HEADS * HEAD_DIM
BLOCK = 128
ROPE_BASE = 10000.0
N_GROUPS = 4
EXPERTS_PER_GROUP = 8
N_EXPERTS = N_GROUPS * EXPERTS_PER_GROUP
EXPERT_FF = 256
N_MOD = 6
MOD_SHIFT1, MOD_SCALE1, MOD_GATE1, MOD_SHIFT2, MOD_SCALE2, MOD_GATE2 = range(N_MOD)
NORM_EPS = 1e-6
NEG_INF = -1e30
LOG2E = 1.4426950408889634
ONES_ROWS = 16

OFF_B, OFF_CG, OFF_XIN, OFF_Q, OFF_K, OFF_V, OFF_GA, OFF_GB = (
    0, 512, 1024, 1536, 2048, 2176, 2304, 3328)
IN_COLS = 4352

LANES = 128
SEQ_TILE = 512
BLOCKS_PER_TILE = SEQ_TILE // BLOCK
ROUTE_ROWS = 8
GROUP_ROWS = 8
ROUTER_ROWS = 48
VMEM_LIMIT = 62 * 1024 * 1024

CHUNK = SEQ_TILE
ROW_BLOCK = 16
CHUNK_ROWS = -(-(2 * CHUNK + N_EXPERTS * (ROW_BLOCK - 1)) // 256) * 256
CHUNK_BLOCKS = CHUNK_ROWS // ROW_BLOCK
GATE_COLS = LANES
XL_COLS = D_MODEL + GATE_COLS
COMMON_ROWS = 2 * CHUNK + 256
COMMON_BLOCKS = COMMON_ROWS // ROW_BLOCK
TILE_BLOCKS = 32
TILE_ROWS = TILE_BLOCKS * ROW_BLOCK
X_SLOTS = 6
Y_SLOTS = 4
W_SLOTS = 3
Y_RING = 3
RING = 3


def _cparams(sem):
    return pltpu.CompilerParams(dimension_semantics=sem, vmem_limit_bytes=VMEM_LIMIT)


def _rms_mod(x, g, shift, scale):
    inv = lax.rsqrt(jnp.mean(x * x, axis=-1, keepdims=True) + NORM_EPS)
    return (x * inv) * (g * (1.0 + scale)) + shift


def _mod_kernel(c_ref, w_ref, b_ref, o_ref):
    c = c_ref[...]
    a = (c * jax.nn.sigmoid(c)).astype(BF16)
    o_ref[...] = jnp.dot(a, w_ref[...].astype(BF16), preferred_element_type=F32) + b_ref[...]


def _modulation(cc, w_ada, b_ada):
    rows = cc.shape[0]
    cols = w_ada.shape[1]
    tile = 1024
    return pl.pallas_call(
        _mod_kernel,
        grid=(cols // tile,),
        in_specs=[pl.BlockSpec((rows, D_MODEL), lambda j: (0, 0)),
                  pl.BlockSpec((D_MODEL, tile), lambda j: (0, j)),
                  pl.BlockSpec((1, tile), lambda j: (0, j))],
        out_specs=pl.BlockSpec((rows, tile), lambda j: (0, j)),
        out_shape=jax.ShapeDtypeStruct((rows, cols), F32),
        compiler_params=_cparams(("arbitrary",)),
        name="adaln_mod",
    )(cc, w_ada, b_ada)


def _ctx_kernel(x_ref, g_ref, sh_ref, sc_ref, w_ref, k_ref, vt_ref):
    h = _rms_mod(x_ref[0], g_ref[...], sh_ref[...], sc_ref[...]).astype(BF16)
    kv = jnp.dot(h, w_ref[...], preferred_element_type=F32)
    k_ref[0] = kv[:, :KV_W].astype(BF16)
    vt_ref[0] = kv[:, KV_W:].T.astype(BF16)


def _context_kv(ctx, norm_g, csh, csc, w_kv):
    bsz, clen, _ = ctx.shape
    return pl.pallas_call(
        _ctx_kernel,
        grid=(bsz,),
        in_specs=[pl.BlockSpec((1, clen, D_MODEL), lambda b: (b, 0, 0)),
                  pl.BlockSpec((1, D_MODEL), lambda b: (0, 0)),
                  pl.BlockSpec((1, D_MODEL), lambda b: (0, 0)),
                  pl.BlockSpec((1, D_MODEL), lambda b: (0, 0)),
                  pl.BlockSpec((D_MODEL, 2 * KV_W), lambda b: (0, 0))],
        out_specs=[pl.BlockSpec((1, clen, KV_W), lambda b: (b, 0, 0)),
                   pl.BlockSpec((1, KV_W, clen), lambda b: (b, 0, 0))],
        out_shape=[jax.ShapeDtypeStruct((bsz, clen, KV_W), BF16),
                   jax.ShapeDtypeStruct((bsz, KV_W, clen), BF16)],
        compiler_params=_cparams(("arbitrary",)),
        name="context_kv",
    )(ctx, norm_g, csh, csc, w_kv)


def _rope(t, cos, sin_signed):
    lane = lax.broadcasted_iota(I32, (1, LANES), 1)
    first_half = (lane % HEAD_DIM) < (HEAD_DIM // 2)
    outs = []
    for j in range(t.shape[1] // LANES):
        tj = t[:, j * LANES:(j + 1) * LANES]
        partner = jnp.where(first_half,
                            pltpu.roll(tj, LANES - HEAD_DIM // 2, axis=1),
                            pltpu.roll(tj, HEAD_DIM // 2, axis=1))
        outs.append(tj * cos + partner * sin_signed)
    return outs[0] if len(outs) == 1 else jnp.concatenate(outs, axis=1)


def _sigmoid(x):
    return 1.0 / (1.0 + jnp.exp2(x * (-LOG2E)))


def _attn_scores(q_t, g, k_win, k_ctx, bias_p4, bias_n4):
    zeros = jnp.zeros((HEAD_DIM, 4 * BLOCK), BF16)
    qg = jnp.concatenate([q_t[(4 * g + h) * HEAD_DIM:(4 * g + h + 1) * HEAD_DIM, :]
                          for h in range(4)], axis=1)
    rhs = jnp.concatenate([qg, zeros] if g == 0 else [zeros, qg], axis=0)
    s = jnp.dot(jnp.concatenate([k_win, k_ctx], axis=0), rhs, preferred_element_type=F32)
    return (s[0:BLOCK] + bias_p4, s[BLOCK:2 * BLOCK], s[2 * BLOCK:3 * BLOCK] + bias_n4,
            s[3 * BLOCK:])


def _attn_probs(scores, sink):
    m = sink
    for s in scores:
        m = jnp.maximum(m, jnp.max(s, axis=0, keepdims=True))
    p_win = jnp.concatenate([jnp.exp2((s - m).astype(BF16)) for s in scores[:3]], axis=0)
    p_ctx = jnp.exp2((scores[3] - m).astype(BF16))
    return p_win, p_ctx, m


def _attn_values(probs, sink, vt_win_g, vt_ctx_g):
    p_win, p_ctx, m = probs

    def with_ones(vt):
        r = lax.broadcasted_iota(I32, (ONES_ROWS, vt.shape[1]), 0)
        return jnp.concatenate([vt, jnp.where(r == 0, 1.0, 0.0).astype(BF16)], axis=0)

    o_ext = jnp.dot(with_ones(jnp.concatenate([vt_win_g, vt_ctx_g], axis=1)),
                    jnp.concatenate([p_win, p_ctx], axis=0),
                    preferred_element_type=F32)
    denom = o_ext[HEAD_DIM:HEAD_DIM + 1, :] + jnp.exp2(sink - m)
    return o_ext[:HEAD_DIM, :] / denom


def _route(logits_t):
    t = logits_t.shape[1]
    grow = lax.broadcasted_iota(I32, (GROUP_ROWS, t), 0)
    gl = jnp.where(grow < N_GROUPS, logits_t[0:GROUP_ROWS, :], NEG_INF)
    gm = jnp.max(gl, axis=0, keepdims=True)
    p_g = 1.0 / jnp.sum(jnp.exp(gl - gm), axis=0, keepdims=True)
    g_idx = jnp.min(jnp.where(gl == gm, grow, N_GROUPS), axis=0, keepdims=True)

    erow = lax.broadcasted_iota(I32, (N_EXPERTS, t), 0)
    el = logits_t[GROUP_ROWS:GROUP_ROWS + N_EXPERTS, :]
    sel = (erow // EXPERTS_PER_GROUP) == g_idx
    em = jnp.where(sel, el, NEG_INF)
    m1 = jnp.max(em, axis=0, keepdims=True)
    i1 = jnp.min(jnp.where(em == m1, erow, N_EXPERTS), axis=0, keepdims=True)
    em2 = jnp.where(erow == i1, NEG_INF, em)
    m2 = jnp.max(em2, axis=0, keepdims=True)
    i2 = jnp.min(jnp.where(em2 == m2, erow, N_EXPERTS), axis=0, keepdims=True)
    z = jnp.sum(jnp.where(sel, jnp.exp(el - m1), 0.0), axis=0, keepdims=True)
    p1 = 1.0 / z
    p2 = jnp.exp(m2 - m1) / z
    gate1 = p_g * p1 / (p1 + p2)
    gate2 = p_g * p2 / (p1 + p2)
    pad = jnp.zeros((ROUTE_ROWS - 4, t), F32)
    return jnp.concatenate([i1.astype(F32), i2.astype(F32), gate1, gate2, pad], axis=0)


def _project_a(x_ref, g_ref, sh_ref, sc_ref, w_ref, cos_ref, sin_ref, slot, uc_slot,
               uc_ring, k_ring, vt_ring):
    h = _rms_mod(x_ref[0], g_ref[...], sh_ref[0], sc_ref[0]).astype(BF16)
    u = jnp.dot(h, w_ref[...], preferred_element_type=F32)
    uc_ring[uc_slot] = u[:, :CONV_W] * u[:, CONV_W:2 * CONV_W]
    k = u[:, 2 * CONV_W:2 * CONV_W + KV_W]
    k_ring[slot] = _rope(k, cos_ref[...], sin_ref[...]).astype(BF16)
    vt_ring[slot] = u[:, 2 * CONV_W + KV_W:].T.astype(BF16)


def _mix_step(x_ref, g1n_ref, g2n_ref, sh1_ref, sc1_ref, gt1_ref, sh2_ref, sc2_ref,
              wb_ref, cos_ref, sin_ref,
              xn_ref, shn_ref, scn_ref, cosn_ref, sinn_ref, wac_ref,
              kc_ref, vtc_ref, wconv_ref, bconv_ref, wa_ref, wbb_ref, wo_ref,
              sink_ref, bias_ref, wr_ref, br_ref, wu_ref, wd_ref, upper_ref, lower_ref,
              x1_ref, xl_ref, pos_ref, nb_ref, wub_ref, wdb_ref,
              h2_scr, route_scr, uc_ring, uc_last, k_ring, vt_ring,
              *, n_seq_tiles):
    step = pl.program_id(0)
    t_idx = step % n_seq_tiles
    is_first = t_idx == 0
    is_last = t_idx == n_seq_tiles - 1
    own, prv, nxt = step % RING, (step + RING - 1) % RING, (step + 1) % RING

    permute, permute_tail = _sort_chunk(h2_scr[...], route_scr[...], upper_ref, lower_ref,
                                        xl_ref, pos_ref, nb_ref)

    @pl.when(step == 0)
    def _():
        uc_ring[...] = jnp.zeros_like(uc_ring)
        uc_last[...] = jnp.zeros_like(uc_last)
        k_ring[...] = jnp.zeros_like(k_ring)
        vt_ring[...] = jnp.zeros_like(vt_ring)
        _project_a(x_ref, g1n_ref, sh1_ref, sc1_ref, wac_ref, cos_ref, sin_ref, 0, 0,
                   uc_ring, k_ring, vt_ring)

    x = x_ref[0]
    hb = _rms_mod(x, g1n_ref[...], sh1_ref[0], sc1_ref[0]).astype(BF16)
    bq = jnp.dot(hb, wb_ref[:, 0:2 * CONV_W], preferred_element_type=F32)
    q = _rope(bq[:, CONV_W:], cos_ref[...], sin_ref[...]) * (HEAD_DIM ** -0.5 * LOG2E)

    def block_ring(ring, slot_, blk, axis):
        lo = blk * BLOCK
        return ring[slot_, lo:lo + BLOCK, :] if axis == 0 else ring[slot_, :, lo:lo + BLOCK]

    def window(ring, jb, axis):
        parts = []
        for blk in (jb - 1, jb, jb + 1):
            if blk < 0:
                parts.append(block_ring(ring, prv, BLOCKS_PER_TILE - 1, axis))
            elif blk >= BLOCKS_PER_TILE:
                parts.append(block_ring(ring, nxt, 0, axis))
            else:
                parts.append(block_ring(ring, own, blk, axis))
        return jnp.concatenate(parts, axis=axis)

    k_ctx = kc_ref[0]
    vt_ctx = vtc_ref[0]
    bias_prev = bias_ref[0:BLOCK, :]
    bias_next = bias_ref[2 * BLOCK:3 * BLOCK, :]
    sink_row = sink_ref[...] * LOG2E
    n_units = BLOCKS_PER_TILE * N_KV_HEADS
    gate_cols = 2 * D_MODEL // n_units
    gate_chunks = []

    def gate_chunk(u):
        c0 = 2 * CONV_W + u * gate_cols
        gate_chunks.append(jnp.dot(hb, wb_ref[:, c0:c0 + gate_cols], preferred_element_type=F32))

    q_ts, biases = [], []
    for jb in range(BLOCKS_PER_TILE):
        bias_p, bias_n = bias_prev, bias_next
        if jb == 0:
            bias_p = bias_prev + jnp.where(is_first, NEG_INF, 0.0)
        if jb == BLOCKS_PER_TILE - 1:
            bias_n = bias_next + jnp.where(is_last, NEG_INF, 0.0)
        biases.append((jnp.concatenate([bias_p] * 4, axis=1), jnp.concatenate([bias_n] * 4, axis=1)))
        q_ts.append(q[jb * BLOCK:(jb + 1) * BLOCK, :].T.astype(BF16))

    scores, probs, outs = {}, {}, {}
    for t in range(n_units + 2):
        if t == 3:
            wub_ref[...] = wu_ref[...].astype(BF16)
        if t == 5:
            wdb_ref[...] = wd_ref[...].astype(BF16)
        if t == 0:
            permute(0, COMMON_ROWS // 2)
        if t == 4:
            permute(COMMON_ROWS // 2, COMMON_ROWS // 2)
        if t == 2:
            _project_a(xn_ref, g1n_ref, shn_ref, scn_ref, wac_ref, cosn_ref, sinn_ref, nxt,
                       (step + 1) % 2, uc_ring, k_ring, vt_ring)
        if t < n_units:
            jb, g = divmod(t, N_KV_HEADS)
            scores[t] = _attn_scores(q_ts[jb], g, window(k_ring, jb, 0), k_ctx, *biases[jb])
            gate_chunk(t)
        if 0 <= t - 1 < n_units:
            g = (t - 1) % N_KV_HEADS
            probs[t - 1] = _attn_probs(scores.pop(t - 1),
                                       sink_row[:, g * 4 * BLOCK:(g + 1) * 4 * BLOCK])
        if 0 <= t - 2 < n_units:
            jb, g = divmod(t - 2, N_KV_HEADS)
            outs[t - 2] = _attn_values(
                probs.pop(t - 2), sink_row[:, g * 4 * BLOCK:(g + 1) * 4 * BLOCK],
                window(vt_ring, jb, 1)[g * HEAD_DIM:(g + 1) * HEAD_DIM, :],
                vt_ctx[g * HEAD_DIM:(g + 1) * HEAD_DIM, :])
    o_blocks = []
    for jb in range(BLOCKS_PER_TILE):
        o_rows = [outs[jb * N_KV_HEADS + g][:, h * BLOCK:(h + 1) * BLOCK]
                  for g in range(N_KV_HEADS) for h in range(4)]
        o_blocks.append(jnp.concatenate(o_rows, axis=0).T)
    gates = jnp.concatenate(gate_chunks, axis=1)

    uc = uc_ring[step % 2]
    row = lax.broadcasted_iota(I32, (SEQ_TILE, 1), 0)
    prev_row = jnp.where(is_first, 0.0, uc_last[0:1, :])
    next_row = jnp.where(is_last, 0.0, uc_ring[(step + 1) % 2, 0:1, :])
    uc_last[0:1, :] = uc[SEQ_TILE - 1:SEQ_TILE, :]
    up = jnp.where(row == 0, prev_row, pltpu.roll(uc, 1, axis=0))
    dn = jnp.where(row == SEQ_TILE - 1, next_row, pltpu.roll(uc, SEQ_TILE - 1, axis=0))
    wconv = wconv_ref[...]
    y = bconv_ref[...] + (up * wconv[0:1, :] + uc * wconv[1:2, :] + dn * wconv[2:3, :])
    ya = jnp.dot((bq[:, :CONV_W] * y).astype(BF16), wa_ref[...], preferred_element_type=F32)

    o = jnp.concatenate(o_blocks, axis=0).astype(BF16)
    yb = jnp.dot(o, wbb_ref[...], preferred_element_type=F32)
    merged = (_sigmoid(gates[:, :D_MODEL]) * ya + _sigmoid(gates[:, D_MODEL:]) * yb).astype(BF16)
    x1 = x + gt1_ref[0] * jnp.dot(merged, wo_ref[...], preferred_element_type=F32)
    x1_ref[0] = x1
    h2 = _rms_mod(x1, g2n_ref[...], sh2_ref[0], sc2_ref[0]).astype(BF16)
    logits_t = lax.dot_general(wr_ref[...], h2, (((1,), (1,)), ((), ())),
                               preferred_element_type=F32) + br_ref[...]
    permute_tail()
    h2_scr[...] = h2
    route_scr[...] = _route(logits_t)


N_MIX_OUTPUTS = 6
N_MIX_SCRATCH = 6


def _mix_kernel(*refs, n_seq_tiles, batch):
    n_tail = N_MIX_OUTPUTS + N_MIX_SCRATCH
    upper_ref, lower_ref = refs[-n_tail - 2:-n_tail]
    _, xl_ref, pos_ref, nb_ref, _, _ = refs[-n_tail:-N_MIX_SCRATCH]
    h2_scr, route_scr = refs[-N_MIX_SCRATCH:-N_MIX_SCRATCH + 2]
    step = pl.program_id(0)
    n_main = n_seq_tiles * batch

    @pl.when(step == 0)
    def _():
        h2_scr[...] = jnp.zeros_like(h2_scr)
        route_scr[...] = jnp.zeros_like(route_scr)

    @pl.when(step < n_main)
    def _():
        _mix_step(*refs, n_seq_tiles=n_seq_tiles)

    @pl.when(step == n_main)
    def _():
        permute, permute_tail = _sort_chunk(h2_scr[...], route_scr[...], upper_ref, lower_ref,
                                            xl_ref, pos_ref, nb_ref)
        permute(0, COMMON_ROWS)
        permute_tail()


def _mix(x, norm1_g, norm2_g, mod3, w_a_cols, w_b_cols, cos_t, sin_t, k_ctx, vt_ctx,
         w_conv, b_conv, w_a, w_b, w_o, sink_row, bias, wr_t, br_col, w_up, w_down, upper, lower):
    bsz, seq, _ = x.shape
    nt = seq // SEQ_TILE
    clen = k_ctx.shape[1]
    nch = n_main = bsz * nt
    assert N_EXPERTS % n_main == 0, "each grid step converts an equal share of the experts"
    e_step = N_EXPERTS // n_main

    def tile(s):
        return divmod(jnp.clip(s, 0, n_main - 1), nt)

    def per_tile(shape, index, shift=0):
        return pl.BlockSpec(shape, lambda s: index(*tile(s + shift)))

    def const_spec(shape):
        return pl.BlockSpec(shape, lambda s: tuple(0 for _ in shape),
                            pipeline_mode=pl.Buffered(1))

    def x_spec(shift=0):
        return per_tile((1, SEQ_TILE, D_MODEL), lambda b, t: (b, t, 0), shift)

    def mod_spec(j, shift=0):
        return per_tile((1, 1, D_MODEL), lambda b, t: (b, 0, j), shift)

    def rope_spec(shift=0):
        return per_tile((SEQ_TILE, LANES), lambda b, t: (t, 0), shift)

    def expert_spec(w):
        return pl.BlockSpec((e_step,) + w.shape[1:], lambda s: (jnp.minimum(s, n_main - 1), 0, 0))

    def chunk_spec(shape):
        return pl.BlockSpec(shape, lambda s: (jnp.maximum(s - 1, 0),) + (0,) * (len(shape) - 1))

    in_specs = [
        x_spec(), const_spec((1, D_MODEL)), const_spec((1, D_MODEL)),
        mod_spec(MOD_SHIFT1), mod_spec(MOD_SCALE1), mod_spec(MOD_GATE1), mod_spec(MOD_SHIFT2),
        mod_spec(MOD_SCALE2),
        const_spec(w_b_cols.shape), rope_spec(), rope_spec(),
        x_spec(1), mod_spec(MOD_SHIFT1, 1), mod_spec(MOD_SCALE1, 1), rope_spec(1), rope_spec(1),
        const_spec(w_a_cols.shape),
        per_tile((1, clen, KV_W), lambda b, t: (b, 0, 0)),
        per_tile((1, KV_W, clen), lambda b, t: (b, 0, 0)),
        const_spec(w_conv.shape), const_spec(b_conv.shape),
        const_spec(w_a.shape), const_spec(w_b.shape), const_spec(w_o.shape),
        const_spec(sink_row.shape), const_spec(bias.shape),
        const_spec(wr_t.shape), const_spec(br_col.shape),
        expert_spec(w_up), expert_spec(w_down),
        const_spec(upper.shape), const_spec(lower.shape),
    ]
    out_specs = [
        x_spec(), chunk_spec((CHUNK_ROWS, XL_COLS)), chunk_spec((CHUNK, LANES)),
        chunk_spec((1, N_EXPERTS, LANES)), expert_spec(w_up), expert_spec(w_down),
    ]
    out_shape = [
        jax.ShapeDtypeStruct((bsz, seq, D_MODEL), F32),
        jax.ShapeDtypeStruct((nch * CHUNK_ROWS, XL_COLS), BF16),
        jax.ShapeDtypeStruct((nch * CHUNK, LANES), F32),
        jax.ShapeDtypeStruct((nch, N_EXPERTS, LANES), I32),
        jax.ShapeDtypeStruct(w_up.shape, BF16),
        jax.ShapeDtypeStruct(w_down.shape, BF16),
    ]
    return pl.pallas_call(
        functools.partial(_mix_kernel, n_seq_tiles=nt, batch=bsz),
        grid=(n_main + 1,),
        in_specs=in_specs,
        out_specs=out_specs,
        out_shape=out_shape,
        scratch_shapes=[pltpu.VMEM((CHUNK, D_MODEL), BF16), pltpu.VMEM((ROUTE_ROWS, CHUNK), F32),
                        pltpu.VMEM((2, SEQ_TILE, CONV_W), F32), pltpu.VMEM((8, CONV_W), F32),
                        pltpu.VMEM((RING, SEQ_TILE, KV_W), BF16),
                        pltpu.VMEM((RING, KV_W, SEQ_TILE), BF16)],
        compiler_params=_cparams(("arbitrary",)),
        name="token_mix",
    )(x, norm1_g, norm2_g, mod3, mod3, mod3, mod3, mod3, w_b_cols, cos_t, sin_t,
      x, mod3, mod3, cos_t, sin_t, w_a_cols, k_ctx, vt_ctx,
      w_conv, b_conv, w_a, w_b, w_o, sink_row, bias, wr_t, br_col, w_up, w_down, upper, lower)


def _bf16_parts(v):
    hi = v.astype(BF16).astype(F32)
    r1 = v - hi
    mid = r1.astype(BF16).astype(F32)
    lo = (r1 - mid).astype(BF16).astype(F32)
    return hi, mid, lo


def _sort_chunk(h, route, upper_ref, lower_ref, xl_ref, pos_ref, nb_ref):
    e1 = route[0:1, :].astype(I32)
    e2 = route[1:2, :].astype(I32)
    erow = lax.broadcasted_iota(I32, (N_EXPERTS, CHUNK), 0)
    hit1 = erow == e1
    hit2 = erow == e2
    onehot = jnp.where(hit1, 1.0, 0.0) + jnp.where(hit2, 1.0, 0.0)
    cum = jnp.dot(onehot.astype(BF16), upper_ref[...], preferred_element_type=F32)
    cnt = jnp.sum(onehot, axis=1, keepdims=True)
    nblk = jnp.floor((cnt + (ROW_BLOCK - 1)) * (1.0 / ROW_BLOCK))
    nblk_b = jnp.broadcast_to(nblk, (N_EXPERTS, LANES))
    seg = jnp.dot(lower_ref[...], nblk_b.astype(BF16), preferred_element_type=F32) * ROW_BLOCK
    base = seg[:, 0:1] + cum
    pos1 = jnp.sum(jnp.where(hit1, base, 0.0), axis=0, keepdims=True)
    pos2 = jnp.sum(jnp.where(hit2, base, 0.0), axis=0, keepdims=True)
    p1i = pos1.astype(I32)
    p2i = pos2.astype(I32)
    used_rows = (jnp.sum(nblk) * ROW_BLOCK).astype(I32)

    prow = lax.broadcasted_iota(I32, (LANES, CHUNK), 0)
    parts = _bf16_parts(route[2:3, :]) + _bf16_parts(route[3:4, :])
    gpart_rows = jnp.zeros((LANES, CHUNK), F32)
    for j, part in enumerate(parts):
        gpart_rows = jnp.where(prow == j, part, gpart_rows)
    h_ext = jnp.concatenate([h, gpart_rows.T.astype(BF16)], axis=1)
    glane = lax.broadcasted_iota(I32, (1, GATE_COLS), 1)
    one = jnp.ones((), BF16)
    zero = jnp.zeros((), BF16)
    def permute(row0, n_rows):
        r16 = lax.broadcasted_iota(I32, (n_rows, CHUNK), 0).astype(jnp.int16)
        q1 = (p1i - row0).astype(jnp.int16)
        q2 = (p2i - row0).astype(jnp.int16)
        sel1 = jnp.where(r16 == q1, one, zero)
        sel = jnp.where(r16 == q2, one, sel1)
        is_slot1 = jnp.max(sel1, axis=1, keepdims=True)
        rows = pl.ds(row0, n_rows)
        xg = jnp.dot(sel, h_ext, preferred_element_type=F32)
        xl_ref[rows, 0:D_MODEL] = xg[:, :D_MODEL].astype(BF16)
        g6 = xg[:, D_MODEL:]
        g3 = jnp.where(is_slot1.astype(F32) > 0.0, g6, pltpu.roll(g6, GATE_COLS - 3, axis=1))
        xl_ref[rows, D_MODEL:XL_COLS] = jnp.where(glane < 3, g3, 0.0).astype(BF16)

    pos_rows = jnp.where(prow == 0, pos1, jnp.where(prow == 1, pos2, 0.0))
    pos_ref[...] = pos_rows.T
    nb_ref[0] = nblk_b.astype(I32)

    def permute_tail():
        tail_rows = CHUNK_ROWS - COMMON_ROWS

        @pl.when(COMMON_ROWS < used_rows)
        def _():
            permute(COMMON_ROWS, tail_rows)

        @pl.when(COMMON_ROWS >= used_rows)
        def _():
            xl_ref[pl.ds(COMMON_ROWS, tail_rows), :] = jnp.zeros((tail_rows, XL_COLS), BF16)

    return permute, permute_tail


def _max_tiles(nch):
    max_blocks = nch * (2 * CHUNK + N_EXPERTS * (ROW_BLOCK - 1)) // ROW_BLOCK
    return max_blocks // TILE_BLOCKS + N_EXPERTS


def _masked_prefix(le, values):
    delta = values - jnp.concatenate([jnp.zeros((1,), values.dtype), values[:-1]])
    return jnp.sum(jnp.where(le, delta[None, :], 0), axis=1)


def _tile_plan(nb, nch):
    n_tiles = _max_tiles(nch)
    nbt = nb.T
    nbe = jnp.sum(nbt, axis=1)
    nte = (nbe + TILE_BLOCKS - 1) // TILE_BLOCKS
    tile_end = jnp.cumsum(nte)
    tile_start = tile_end - nte
    n_act = tile_end[-1]
    tiles = jnp.arange(n_tiles, dtype=I32)
    te = jnp.sum((tile_end[None, :] <= tiles[:, None]).astype(I32), axis=1)
    e_ar = jnp.arange(N_EXPERTS, dtype=I32)
    active = nte > 0
    te = jnp.where(tiles < n_act, te, jnp.max(jnp.where(active, e_ar, 0)))
    first = jnp.logical_and(te != jnp.concatenate([jnp.full((1,), -1, I32), te[:-1]]),
                            tiles < n_act).astype(I32)
    rank = jnp.cumsum(active.astype(I32)) - 1
    k_ar = jnp.arange(N_EXPERTS + W_SLOTS, dtype=I32)
    eseq = jnp.sum(jnp.where(jnp.logical_and(active[None, :], rank[None, :] == k_ar[:, None]),
                             e_ar[None, :], 0), axis=1)
    n_exp = jnp.sum(active.astype(I32))

    cb_excl = jnp.cumsum(nbt, axis=1) - nbt
    gs = TILE_BLOCKS * tile_start[:, None] + cb_excl
    seg_blk = jnp.cumsum(nb, axis=1) - nb
    base_blk = jnp.arange(nch, dtype=I32)[None, :] * CHUNK_BLOCKS + seg_blk.T
    gs_f = gs.reshape(-1)
    slots = jnp.arange(n_tiles * TILE_BLOCKS, dtype=I32)
    le = gs_f[None, :] <= slots[:, None]
    blk = slots + _masked_prefix(le, (base_blk - gs).reshape(-1))
    valid = slots < _masked_prefix(le, (gs + nbt).reshape(-1))
    gblk = jnp.concatenate([jnp.where(valid, blk, 0).astype(I32),
                            jnp.zeros(((X_SLOTS - 1) * TILE_BLOCKS,), I32)])

    m = jnp.arange(CHUNK_BLOCKS, dtype=I32)
    le_c = seg_blk[:, None, :] <= m[None, :, None]
    shift = (gs.T - seg_blk)
    delta = shift - jnp.concatenate([jnp.zeros((nch, 1), I32), shift[:, :-1]], axis=1)
    slot_of = m[None, :] + jnp.sum(jnp.where(le_c, delta[:, None, :], 0), axis=2)
    used_blocks = jnp.sum(nb, axis=1)
    slot_of = jnp.where(m[None, :] < used_blocks[:, None], slot_of, 0).astype(I32)
    pad = Y_RING - 1
    slot_of = jnp.concatenate([slot_of.reshape(-1), jnp.zeros((pad * CHUNK_BLOCKS,), I32)])
    used_rows = jnp.concatenate([ROW_BLOCK * used_blocks, jnp.zeros((pad,), I32)]).astype(I32)
    return ((n_act.reshape(1).astype(I32), n_exp.reshape(1), first, eseq, gblk),
            (used_rows, slot_of))


def _gmm_kernel(nact_ref, nexp_ref, first_ref, eseq_ref, gblk_ref,
                xl_hbm, wu_hbm, wd_hbm, yt_hbm,
                xbuf, ybuf, zbuf, wu_st, wd_st, in_sem, out_sem, w_sem, z_sem):
    n_act = nact_ref[0]
    n_exp = nexp_ref[0]
    n_tiles = yt_hbm.shape[0] // TILE_BLOCKS
    prefetch = X_SLOTS - 1

    def gather(tile, b):
        s = tile % X_SLOTS
        blk = gblk_ref[tile * TILE_BLOCKS + b]
        return pltpu.make_async_copy(xl_hbm.at[blk], xbuf.at[s, b], in_sem.at[s])

    def write_back(tile):
        s = tile % Y_SLOTS
        return pltpu.make_async_copy(ybuf.at[s], yt_hbm.at[pl.ds(tile * TILE_BLOCKS, TILE_BLOCKS)],
                                     out_sem.at[s])

    def zero_fill(tile):
        return pltpu.make_async_copy(zbuf, yt_hbm.at[pl.ds(tile * TILE_BLOCKS, TILE_BLOCKS)], z_sem)

    def weights(q):
        e = eseq_ref[q]
        s = q % W_SLOTS
        return (pltpu.make_async_copy(wu_hbm.at[e], wu_st.at[s], w_sem.at[0, s]),
                pltpu.make_async_copy(wd_hbm.at[e], wd_st.at[s], w_sem.at[1, s]))

    def tile_blocks(fn):
        for b in range(TILE_BLOCKS):
            fn(b)

    for q0 in range(W_SLOTS - 1):
        @pl.when(q0 < n_exp)
        def _(q0=q0):
            for cp in weights(q0):
                cp.start()
    for t0 in range(prefetch):
        tile_blocks(lambda b, t0=t0: gather(t0, b).start())

    def body(i, q):
        tile_blocks(lambda b: gather(i, b).wait())

        @pl.when(i >= Y_SLOTS)
        def _():
            write_back(i - Y_SLOTS).wait()

        is_first = first_ref[i] == 1

        @pl.when(is_first)
        def _():
            for cp in weights(q):
                cp.wait()

            @pl.when(q + W_SLOTS - 1 < n_exp)
            def _():
                for cp in weights(q + W_SLOTS - 1):
                    cp.start()

        q = q + is_first.astype(I32)
        ws = (q - 1) % W_SLOTS
        xs = i % X_SLOTS
        ys = i % Y_SLOTS

        x = xbuf[xs].reshape(TILE_ROWS, XL_COLS)
        gate = jnp.sum(x[:, D_MODEL:].astype(F32), axis=1, keepdims=True)
        au = jnp.dot(x[:, :D_MODEL], wu_st[ws], preferred_element_type=F32)
        tile_blocks(lambda b: gather(i + prefetch, b).start())
        a = au[:, :EXPERT_FF]
        act = (a * _sigmoid(a)) * au[:, EXPERT_FF:]
        y = jnp.dot(act.astype(BF16), wd_st[ws], preferred_element_type=F32)
        ybuf[ys] = (gate * y).astype(BF16).reshape(TILE_BLOCKS, ROW_BLOCK, D_MODEL)
        write_back(i).start()

        @pl.when(n_act + i < n_tiles)
        def _():
            zero_fill(n_act + i).start()
        return q

    zbuf[...] = jnp.zeros_like(zbuf)
    lax.fori_loop(0, n_act, body, jnp.int32(0))

    def fill_rest(t, carry):
        zero_fill(t).start()
        return carry

    lax.fori_loop(jnp.minimum(2 * n_act, n_tiles), n_tiles, fill_rest, 0)

    def drain(t, carry):
        zero_fill(t).wait()
        return carry

    lax.fori_loop(n_act, n_tiles, drain, 0)
    for k in range(prefetch):
        tile_blocks(lambda b, k=k: gather(n_act + k, b).wait())
    for k in range(Y_SLOTS):
        @pl.when(n_act - 1 - k >= 0)
        def _(k=k):
            write_back(n_act - 1 - k).wait()


def _grouped_mlp(plan, xl, w_up, w_down, n_tiles):
    grid_spec = pltpu.PrefetchScalarGridSpec(
        num_scalar_prefetch=len(plan),
        grid=(1,),
        in_specs=[pl.BlockSpec(memory_space=pl.ANY)] * 3,
        out_specs=pl.BlockSpec(memory_space=pl.ANY),
        scratch_shapes=[pltpu.VMEM((X_SLOTS, TILE_BLOCKS, ROW_BLOCK, XL_COLS), BF16),
                        pltpu.VMEM((Y_SLOTS, TILE_BLOCKS, ROW_BLOCK, D_MODEL), BF16),
                        pltpu.VMEM((TILE_BLOCKS, ROW_BLOCK, D_MODEL), BF16),
                        pltpu.VMEM((W_SLOTS, D_MODEL, 2 * EXPERT_FF), BF16),
                        pltpu.VMEM((W_SLOTS, EXPERT_FF, D_MODEL), BF16),
                        pltpu.SemaphoreType.DMA((X_SLOTS,)),
                        pltpu.SemaphoreType.DMA((Y_SLOTS,)),
                        pltpu.SemaphoreType.DMA((2, W_SLOTS)),
                        pltpu.SemaphoreType.DMA(())])
    return pl.pallas_call(
        _gmm_kernel,
        grid_spec=grid_spec,
        out_shape=jax.ShapeDtypeStruct((n_tiles * TILE_BLOCKS, ROW_BLOCK, D_MODEL), BF16),
        compiler_params=_cparams(("arbitrary",)),
        name="moe_experts",
    )(*plan, xl, w_up, w_down)


def _combine_kernel(used_ref, slot_ref, yt_hbm, pos_ref, x1_ref, gt2_ref, fg_ref, o_ref, ybuf, sem):
    c = pl.program_id(0)
    n_chunks = pl.num_programs(0)
    used_rows = used_ref[c]

    def chunk_blocks(chunk, op):
        s = chunk % Y_RING

        def copy(b):
            slot = slot_ref[chunk * CHUNK_BLOCKS + b]
            return pltpu.make_async_copy(yt_hbm.at[slot], ybuf.at[s, b], sem.at[s])

        for b in range(COMMON_BLOCKS):
            op(copy(b))

        @pl.when(used_ref[chunk] > COMMON_ROWS)
        def _():
            for b in range(COMMON_BLOCKS, CHUNK_BLOCKS):
                op(copy(b))

    @pl.when(c == 0)
    def _():
        for k in range(Y_RING - 1):
            chunk_blocks(k, lambda cp: cp.start())

    chunk_blocks(c + Y_RING - 1, lambda cp: cp.start())
    chunk_blocks(c, lambda cp: cp.wait())

    def body(k_rows):
        pos = pos_ref[...]
        p1 = pos[:, 0:1].astype(I32)
        p2 = pos[:, 1:2].astype(I32)
        r = lax.broadcasted_iota(I32, (CHUNK, k_rows), 1)
        sel = jnp.where(r == p1, 1.0, jnp.where(r == p2, 1.0, 0.0)).astype(BF16)
        yl = ybuf[c % Y_RING, 0:k_rows // ROW_BLOCK].reshape(k_rows, D_MODEL)
        y = jnp.dot(sel, yl, preferred_element_type=F32)
        x2 = x1_ref[...] + gt2_ref[0] * y
        inv = lax.rsqrt(jnp.mean(x2 * x2, axis=-1, keepdims=True) + NORM_EPS)
        o_ref[...] = (x2 * inv) * fg_ref[...]

    pl.when(used_rows <= COMMON_ROWS)(lambda: body(COMMON_ROWS))
    pl.when(used_rows > COMMON_ROWS)(lambda: body(CHUNK_ROWS))

    @pl.when(c == n_chunks - 1)
    def _():
        for k in range(1, Y_RING):
            chunk_blocks(c + k, lambda cp: cp.wait())


def _combine(used_rows, slot_of, yt, pos_c, x1, mod3, final_g, seq):
    n = x1.shape[0]
    per_seq = seq // CHUNK
    grid_spec = pltpu.PrefetchScalarGridSpec(
        num_scalar_prefetch=2,
        grid=(n // CHUNK,),
        in_specs=[pl.BlockSpec(memory_space=pl.ANY),
                  pl.BlockSpec((CHUNK, LANES), lambda c, u, s: (c, 0)),
                  pl.BlockSpec((CHUNK, D_MODEL), lambda c, u, s: (c, 0)),
                  pl.BlockSpec((1, 1, D_MODEL), lambda c, u, s: (c // per_seq, 0, MOD_GATE2)),
                  pl.BlockSpec((1, D_MODEL), lambda c, u, s: (0, 0))],
        out_specs=pl.BlockSpec((CHUNK, D_MODEL), lambda c, u, s: (c, 0)),
        scratch_shapes=[pltpu.VMEM((Y_RING, CHUNK_BLOCKS, ROW_BLOCK, D_MODEL), BF16),
                        pltpu.SemaphoreType.DMA((Y_RING,))])
    return pl.pallas_call(
        _combine_kernel,
        grid_spec=grid_spec,
        out_shape=jax.ShapeDtypeStruct((n, D_MODEL), F32),
        compiler_params=_cparams(("arbitrary",)),
        name="moe_combine",
    )(used_rows, slot_of, yt, pos_c, x1, mod3, final_g)


def _rope_tables(seq):
    n_freq = HEAD_DIM // 4
    inv_freq = ROPE_BASE ** (-jnp.arange(n_freq, dtype=F32) / n_freq)
    rows = seq // GRID_W
    row = jnp.repeat(jnp.arange(rows, dtype=F32), GRID_W)
    col = jnp.tile(jnp.arange(GRID_W, dtype=F32), rows)
    ang = jnp.concatenate([row[:, None] * inv_freq, col[:, None] * inv_freq], axis=-1)
    cos, sin = jnp.cos(ang), jnp.sin(ang)
    return jnp.tile(cos, (1, 4)), jnp.concatenate([-sin, sin, -sin, sin], axis=1)


def _window_bias():
    key = np.arange(3 * BLOCK)[:, None]
    qry = np.arange(BLOCK)[None, :]
    valid = (key - qry >= 0) & (key - qry <= 2 * BLOCK)
    return jnp.asarray(np.where(valid, 0.0, NEG_INF), F32)


def kernel(x, c, ctx, c_ctx, w_ada, b_ada, norm1_g, w_in, w_conv, b_conv, w_a, w_b, sink, w_o,
           norm2_g, w_group, b_group, w_router, b_router, w_up, w_down, final_g):
    bsz, seq, _ = x.shape
    assert w_ada.shape[0] == 1 and seq % SEQ_TILE == 0

    cc = jnp.zeros((16, D_MODEL), F32).at[:bsz].set(c).at[bsz].set(c_ctx)
    mod = _modulation(cc, w_ada[0], b_ada[0][None, :])
    mod3 = mod[:bsz].reshape(bsz, 1, N_MOD * D_MODEL)
    csh1 = mod[bsz:bsz + 1, 0:D_MODEL]
    csc1 = mod[bsz:bsz + 1, D_MODEL:2 * D_MODEL]

    w = w_in[0]
    n1g = norm1_g[0][None, :]
    n2g = norm2_g[0][None, :]
    w_kv = w[:, OFF_K:OFF_GA].astype(BF16)
    w_a_cols = jnp.concatenate([w[:, OFF_CG:OFF_Q], w[:, OFF_K:OFF_GA]], axis=1).astype(BF16)
    w_b_cols = jnp.concatenate([w[:, OFF_B:OFF_CG], w[:, OFF_Q:OFF_K], w[:, OFF_GA:]],
                               axis=1).astype(BF16)
    cos_t, sin_t = _rope_tables(seq)

    k_ctx, vt_ctx = _context_kv(ctx, n1g, csh1, csc1, w_kv)
    sink_row = jnp.repeat(sink[0].astype(F32), BLOCK)[None, :]
    pad_g = jnp.zeros((D_MODEL, GROUP_ROWS - N_GROUPS), F32)
    pad_e = jnp.zeros((D_MODEL, ROUTER_ROWS - GROUP_ROWS - N_EXPERTS), F32)
    wr_t = jnp.concatenate([w_group[0], pad_g, w_router[0], pad_e], axis=1).T.astype(BF16)
    br_col = jnp.concatenate([b_group[0], pad_g[0], b_router[0], pad_e[0]])[:, None]

    upper = jnp.asarray(np.triu(np.ones((CHUNK, CHUNK), np.float32), 1), BF16)
    lower = jnp.asarray(np.tril(np.ones((N_EXPERTS, N_EXPERTS), np.float32), -1), BF16)
    x1, xl, pos_c, nb, w_up_bf, w_down_bf = _mix(
        x, n1g, n2g, mod3, w_a_cols, w_b_cols, cos_t, sin_t, k_ctx, vt_ctx,
        w_conv[0], b_conv[0][None, :], w_a[0].astype(BF16), w_b[0].astype(BF16),
        w_o[0].astype(BF16), sink_row, _window_bias(), wr_t, br_col, w_up[0], w_down[0],
        upper, lower)

    n = bsz * seq
    nch = n // CHUNK
    nb = nb[:, :, 0]
    expert_plan, combine_plan = _tile_plan(nb, nch)
    yt = _grouped_mlp(expert_plan, xl.reshape(-1, ROW_BLOCK, XL_COLS), w_up_bf, w_down_bf,
                      _max_tiles(nch))
    out = _combine(*combine_plan, yt, pos_c, x1.reshape(n, D_MODEL), mod3, final_g[None, :], seq)
    return out.reshape(bsz, seq, D_MODEL)
```
